```python
import math
import jax, jax.numpy as jnp
from jax import lax
import numpy as np

D_MODEL = 1024
BATCH = 8
SEQ = 16384
DEPTH = 1

SSM_EXPAND = 2
SSM_D_INNER = SSM_EXPAND * D_MODEL
SSM_HEADDIM = 64
SSM_HEADS = SSM_D_INNER // SSM_HEADDIM
SSM_GROUPS = 4
SSM_HEADS_PER_GROUP = SSM_HEADS // SSM_GROUPS
SSM_STATE = 128
SSM_CONV = 4
SSM_CHUNK = 128
SSM_CONV_DIM = SSM_D_INNER + 2 * SSM_GROUPS * SSM_STATE
SSM_DT_MIN = 0.001
SSM_DT_MAX = 0.1

ATTN_HEADS = 16
ATTN_KV_HEADS = 2
ATTN_HEADDIM = 64
ATTN_GROUP = ATTN_HEADS // ATTN_KV_HEADS
WINDOW = 128
REL_BUCKETS = 32
REL_MAX_DIST = 128

D_FF = 2816
FFN_CONV = 3

DEEPNORM_ALPHA = (2.0 * DEPTH) ** 0.25
DEEPNORM_BETA = (8.0 * DEPTH) ** -0.25
LN_EPS = 1e-5
RMS_EPS = 1e-5

Z_COLS = SSM_D_INNER
XBC_COLS = SSM_CONV_DIM
DT_COLS = SSM_HEADS
Q_COLS = ATTN_HEADS * ATTN_HEADDIM
KV_COLS = ATTN_KV_HEADS * ATTN_HEADDIM
GATE_COLS = 2 * D_MODEL
IN_COLS = Z_COLS + XBC_COLS + DT_COLS + Q_COLS + 2 * KV_COLS + GATE_COLS
SPLIT_POINTS = [Z_COLS,
                Z_COLS + XBC_COLS,
                Z_COLS + XBC_COLS + DT_COLS,
                Z_COLS + XBC_COLS + DT_COLS + Q_COLS,
                Z_COLS + XBC_COLS + DT_COLS + Q_COLS + KV_COLS,
                Z_COLS + XBC_COLS + DT_COLS + Q_COLS + 2 * KV_COLS]

kernel_name = 'hybrid_ssd_swa_sink_convffn_deepnorm'


def layer_norm(x, g, b):
    xf = x.astype(jnp.float32)
    mu = jnp.mean(xf, axis=-1, keepdims=True)
    xc = xf - mu
    var = jnp.mean(xc * xc, axis=-1, keepdims=True)
    y = xc * lax.rsqrt(var + LN_EPS) * g.astype(jnp.float32) + b.astype(jnp.float32)
    return y.astype(x.dtype)


def causal_depthwise_conv(u, w, b):
    k, c = w.shape
    out = lax.conv_general_dilated(u, w[:, None, :].astype(u.dtype), window_strides=(1,),
                                   padding=[(k - 1, 0)],
                                   dimension_numbers=('NWC', 'WIO', 'NWC'),
                                   feature_group_count=c)
    return out + b.astype(u.dtype)


def ssd_chunked(xs, dt, a, bm, cm):
    b, s = xs.shape[:2]
    nc, lc = s // SSM_CHUNK, SSM_CHUNK
    G, E, P, N = SSM_GROUPS, SSM_HEADS_PER_GROUP, SSM_HEADDIM, SSM_STATE
    x = xs.reshape(b, nc, lc, G, E, P)
    dt_c = dt.reshape(b, nc, lc, G, E)
    bm = bm.reshape(b, nc, lc, G, N)
    cm = cm.reshape(b, nc, lc, G, N)
    a_dt = jnp.moveaxis(dt_c * a.reshape(G, E), 2, -1)
    a_cs = jnp.cumsum(a_dt, axis=-1)
    xdt = x * dt_c[..., None]
    seg = a_cs[..., :, None] - a_cs[..., None, :]
    causal = jnp.tril(jnp.ones((lc, lc), dtype=bool))
    decay = jnp.exp(jnp.where(causal, seg, -jnp.inf))
    cb = jnp.einsum('bclgn,bcsgn->bcgls', cm, bm)
    y_diag = jnp.einsum('bcgels,bcsgep->bclgep', cb[:, :, :, None] * decay, xdt)
    decay_states = jnp.moveaxis(jnp.exp(a_cs[..., -1:] - a_cs), -1, 2)
    states = jnp.einsum('bclgn,bclgep->bcgepn', bm, xdt * decay_states[..., None])
    chunk_decay = jnp.exp(a_cs[..., -1])

    def step(h, inp):
        st, dec = inp
        return h * dec[..., None, None] + st, h

    h0 = jnp.zeros((b, G, E, P, N), jnp.float32)
    _, prev = lax.scan(step, h0, (jnp.moveaxis(states, 1, 0), jnp.moveaxis(chunk_decay, 1, 0)))
    prev = jnp.moveaxis(prev, 0, 1)
    state_decay_out = jnp.moveaxis(jnp.exp(a_cs), -1, 2)
    y_off = jnp.einsum('bclgn,bcgepn->bclgep', cm, prev) * state_decay_out[..., None]
    return (y_diag + y_off).reshape(b, s, SSM_HEADS, P)


def mamba2_branch(z, xbc, dt_raw, conv_w, conv_b, dt_bias, a_log, d_skip, norm_w):
    b, s, _ = z.shape
    xbc = jax.nn.silu(causal_depthwise_conv(xbc, conv_w, conv_b))
    xs, bm, cm = jnp.split(xbc, [SSM_D_INNER, SSM_D_INNER + SSM_GROUPS * SSM_STATE], axis=-1)
    xs = xs.reshape(b, s, SSM_HEADS, SSM_HEADDIM).astype(jnp.float32)
    bm = bm.reshape(b, s, SSM_GROUPS, SSM_STATE).astype(jnp.float32)
    cm = cm.reshape(b, s, SSM_GROUPS, SSM_STATE).astype(jnp.float32)
    dt = jax.nn.softplus(dt_raw.astype(jnp.float32) + dt_bias.astype(jnp.float32))
    a = -jnp.exp(a_log.astype(jnp.float32))
    y = ssd_chunked(xs, dt, a, bm, cm) + xs * d_skip.astype(jnp.float32)[:, None]
    y = y.reshape(b, s, SSM_D_INNER) * jax.nn.silu(z.astype(jnp.float32))
    yg = y.reshape(b, s, SSM_GROUPS, SSM_D_INNER // SSM_GROUPS)
    yg = yg * lax.rsqrt(jnp.mean(yg * yg, axis=-1, keepdims=True) + RMS_EPS)
    return (yg.reshape(b, s, SSM_D_INNER) * norm_w.astype(jnp.float32)).astype(z.dtype)


def rel_bucket(rel):
    n = jnp.maximum(rel, 0)
    max_exact = REL_BUCKETS // 2
    nf = jnp.maximum(n, 1).astype(jnp.float32)
    large = max_exact + (jnp.log(nf / max_exact) / math.log(REL_MAX_DIST / max_exact)
                         * (REL_BUCKETS - max_exact)).astype(jnp.int32)
    large = jnp.minimum(large, REL_BUCKETS - 1)
    return jnp.where(n < max_exact, n, large)


def swa_sink_attention(q, k, v, sinks, rel_bias):
    b, s, _ = q.shape
    W = WINDOW
    nb = s // W
    KV, G, Dh = ATTN_KV_HEADS, ATTN_GROUP, ATTN_HEADDIM
    qb = q.reshape(b, nb, W, KV, G, Dh).astype(jnp.float32)
    kb = k.reshape(b, nb, W, KV, Dh).astype(jnp.float32)
    vb = v.reshape(b, nb, W, KV, Dh).astype(jnp.float32)
    pad = jnp.zeros_like(kb[:, :1])
    k_band = jnp.concatenate([jnp.concatenate([pad, kb[:, :-1]], axis=1), kb], axis=2)
    v_band = jnp.concatenate([jnp.concatenate([pad, vb[:, :-1]], axis=1), vb], axis=2)
    logits = jnp.einsum('bnqkgd,bnskd->bnkgqs', qb, k_band) * (Dh ** -0.5)
    qi = jnp.arange(W)[:, None] + W
    kj = jnp.arange(2 * W)[None, :]
    rel = qi - kj
    bias = rel_bias.astype(jnp.float32)[rel_bucket(rel)]
    bias = jnp.transpose(bias, (2, 0, 1)).reshape(KV, G, W, 2 * W)
    in_window = (rel >= 0) & (rel < W)
    block_idx = jnp.arange(nb)[:, None, None]
    valid = in_window[None] & ((block_idx > 0) | (kj >= W)[None])
    logits = jnp.where(valid[None, :, None, None], logits + bias, -jnp.inf)
    sink = sinks.astype(jnp.float32).reshape(1, 1, KV, G, 1, 1)
    m = jnp.maximum(jnp.max(logits, axis=-1, keepdims=True), sink)
    p = jnp.exp(logits - m)
    probs = p / (jnp.sum(p, axis=-1, keepdims=True) + jnp.exp(sink - m))
    out = jnp.einsum('bnkgqs,bnskd->bnqkgd', probs, v_band)
    return out.reshape(b, s, ATTN_HEADS * Dh).astype(q.dtype)


def conv_ffn(h, w_up, conv_w, conv_b, w_down):
    u = causal_depthwise_conv(jnp.einsum('bsd,df->bsf', h, w_up), conv_w, conv_b)
    gate, val = jnp.split(u, 2, axis=-1)
    return jnp.einsum('bsf,fd->bsd', jax.nn.silu(gate) * val, w_down)


def _fwd_setup_inputs(seed: int = 0) -> dict:
    key = jax.random.key(seed)
    ks = jax.random.split(key, 24)
    L = DEPTH
    f32 = jnp.float32
    nrm = lambda k, shape: jax.random.normal(k, shape, f32)
    x = nrm(ks[0], (BATCH, SEQ, D_MODEL))
    rel_bias = 0.1 * nrm(ks[1], (REL_BUCKETS, ATTN_HEADS))
    w_in = nrm(ks[2], (L, D_MODEL, IN_COLS)) * D_MODEL ** -0.5
    b_gate = 0.01 * nrm(ks[3], (L, GATE_COLS))
    ssm_conv_w = 0.5 * nrm(ks[4], (L, SSM_CONV, SSM_CONV_DIM))
    ssm_conv_b = 0.01 * nrm(ks[5], (L, SSM_CONV_DIM))
    u = jax.random.uniform(ks[6], (L, SSM_HEADS), f32)
    dt0 = jnp.exp(u * (math.log(SSM_DT_MAX) - math.log(SSM_DT_MIN)) + math.log(SSM_DT_MIN))
    ssm_dt_bias = dt0 + jnp.log(-jnp.expm1(-dt0))
    ssm_a_log = jnp.log(jax.random.uniform(ks[7], (L, SSM_HEADS), f32, 1.0, 16.0))
    ssm_d = 1.0 + 0.01 * nrm(ks[8], (L, SSM_HEADS))
    ssm_norm_w = 1.0 + 0.01 * nrm(ks[9], (L, SSM_D_INNER))
    attn_sinks = 0.1 * nrm(ks[10], (L, ATTN_HEADS))
    w_branch_ssm = nrm(ks[11], (L, SSM_D_INNER, D_MODEL)) * SSM_D_INNER ** -0.5 * DEEPNORM_BETA
    w_branch_attn = nrm(ks[12], (L, Q_COLS, D_MODEL)) * Q_COLS ** -0.5 * DEEPNORM_BETA
    w_mix_out = nrm(ks[13], (L, D_MODEL, D_MODEL)) * D_MODEL ** -0.5 * DEEPNORM_BETA
    ln1_g = 1.0 + 0.01 * nrm(ks[14], (L, D_MODEL))
    ln1_b = 0.01 * nrm(ks[15], (L, D_MODEL))
    w_up = nrm(ks[16], (L, D_MODEL, 2 * D_FF)) * D_MODEL ** -0.5 * DEEPNORM_BETA
    ffn_conv_w = nrm(ks[17], (L, FFN_CONV, 2 * D_FF)) * FFN_CONV ** -0.5
    ffn_conv_b = 0.01 * nrm(ks[18], (L, 2 * D_FF))
    w_down = nrm(ks[19], (L, D_FF, D_MODEL)) * D_FF ** -0.5 * DEEPNORM_BETA
    ln2_g = 1.0 + 0.01 * nrm(ks[20], (L, D_MODEL))
    ln2_b = 0.01 * nrm(ks[21], (L, D_MODEL))
    return {'x': x, 'rel_bias': rel_bias, 'w_in': w_in, 'b_gate': b_gate,
            'ssm_conv_w': ssm_conv_w, 'ssm_conv_b': ssm_conv_b, 'ssm_dt_bias': ssm_dt_bias,
            'ssm_a_log': ssm_a_log, 'ssm_d': ssm_d, 'ssm_norm_w': ssm_norm_w,
            'attn_sinks': attn_sinks, 'w_branch_ssm': w_branch_ssm, 'w_branch_attn': w_branch_attn,
            'w_mix_out': w_mix_out, 'ln1_g': ln1_g, 'ln1_b': ln1_b, 'w_up': w_up,
            'ffn_conv_w': ffn_conv_w, 'ffn_conv_b': ffn_conv_b, 'w_down': w_down,
            'ln2_g': ln2_g, 'ln2_b': ln2_b}


def _fwd_reference(x, rel_bias, w_in, b_gate, ssm_conv_w, ssm_conv_b, ssm_dt_bias, ssm_a_log, ssm_d,
              ssm_norm_w, attn_sinks, w_branch_ssm, w_branch_attn, w_mix_out, ln1_g, ln1_b,
              w_up, ffn_conv_w, ffn_conv_b, w_down, ln2_g, ln2_b):
    h = x
    for l in range(DEPTH):
        proj = jnp.einsum('bsd,dc->bsc', h, w_in[l])
        z, xbc, dt_raw, q, k, v, gates = jnp.split(proj, SPLIT_POINTS, axis=-1)
        y_ssm = mamba2_branch(z, xbc, dt_raw, ssm_conv_w[l], ssm_conv_b[l], ssm_dt_bias[l],
                              ssm_a_log[l], ssm_d[l], ssm_norm_w[l])
        y_attn = swa_sink_attention(q, k, v, attn_sinks[l], rel_bias)
        g_ssm, g_attn = jnp.split(jax.nn.sigmoid(gates + b_gate[l]), 2, axis=-1)
        merged = (g_ssm * jnp.einsum('bsi,id->bsd', y_ssm, w_branch_ssm[l])
                  + g_attn * jnp.einsum('bsi,id->bsd', y_attn, w_branch_attn[l]))
        mix_out = jnp.einsum('bsd,de->bse', merged, w_mix_out[l])
        h = layer_norm(DEEPNORM_ALPHA * h + mix_out, ln1_g[l], ln1_b[l])
        ffn_out = conv_ffn(h, w_up[l], ffn_conv_w[l], ffn_conv_b[l], w_down[l])
        h = layer_norm(DEEPNORM_ALPHA * h + ffn_out, ln2_g[l], ln2_b[l])
    return h


import jax as _jax
import jax.numpy as _jnp

TWIN_FORMAT = 'train_step'
FWD_PARAMS = ['x', 'rel_bias', 'w_in', 'b_gate', 'ssm_conv_w', 'ssm_conv_b', 'ssm_dt_bias', 'ssm_a_log', 'ssm_d', 'ssm_norm_w', 'attn_sinks', 'w_branch_ssm', 'w_branch_attn', 'w_mix_out', 'ln1_g', 'ln1_b', 'w_up', 'ffn_conv_w', 'ffn_conv_b', 'w_down', 'ln2_g', 'ln2_b']
TWIN_WEIGHTS = ['rel_bias', 'w_in', 'b_gate', 'ssm_conv_w', 'ssm_conv_b', 'ssm_dt_bias', 'ssm_a_log', 'ssm_d', 'ssm_norm_w', 'attn_sinks', 'w_branch_ssm', 'w_branch_attn', 'w_mix_out', 'ln1_g', 'ln1_b', 'w_up', 'ffn_conv_w', 'ffn_conv_b', 'w_down', 'ln2_g', 'ln2_b']
TWIN_DIFF_INPUT = 'x'
TWIN_INPUTS = ['x', 'rel_bias', 'w_in', 'b_gate', 'ssm_conv_w', 'ssm_conv_b', 'ssm_dt_bias', 'ssm_a_log', 'ssm_d', 'ssm_norm_w', 'attn_sinks', 'w_branch_ssm', 'w_branch_attn', 'w_mix_out', 'ln1_g', 'ln1_b', 'w_up', 'ffn_conv_w', 'ffn_conv_b', 'w_down', 'ln2_g', 'ln2_b', 'loss_target', 'm_rel_bias', 'm_w_in', 'm_b_gate', 'm_ssm_conv_w', 'm_ssm_conv_b', 'm_ssm_dt_bias', 'm_ssm_a_log', 'm_ssm_d', 'm_ssm_norm_w', 'm_attn_sinks', 'm_w_branch_ssm', 'm_w_branch_attn', 'm_w_mix_out', 'm_ln1_g', 'm_ln1_b', 'm_w_up', 'm_ffn_conv_w', 'm_ffn_conv_b', 'm_w_down', 'm_ln2_g', 'm_ln2_b', 'v_rel_bias', 'v_w_in', 'v_b_gate', 'v_ssm_conv_w', 'v_ssm_conv_b', 'v_ssm_dt_bias', 'v_ssm_a_log', 'v_ssm_d', 'v_ssm_norm_w', 'v_attn_sinks', 'v_w_branch_ssm', 'v_w_branch_attn', 'v_w_mix_out', 'v_ln1_g', 'v_ln1_b', 'v_w_up', 'v_ffn_conv_w', 'v_ffn_conv_b', 'v_w_down', 'v_ln2_g', 'v_ln2_b']
TWIN_OUTPUTS = ['loss', 'grad_x', 'grad_rel_bias', 'grad_w_in', 'grad_b_gate', 'grad_ssm_conv_w', 'grad_ssm_conv_b', 'grad_ssm_dt_bias', 'grad_ssm_a_log', 'grad_ssm_d', 'grad_ssm_norm_w', 'grad_attn_sinks', 'grad_w_branch_ssm', 'grad_w_branch_attn', 'grad_w_mix_out', 'grad_ln1_g', 'grad_ln1_b', 'grad_w_up', 'grad_ffn_conv_w', 'grad_ffn_conv_b', 'grad_w_down', 'grad_ln2_g', 'grad_ln2_b', 'delta_rel_bias', 'delta_w_in', 'delta_b_gate', 'delta_ssm_conv_w', 'delta_ssm_conv_b', 'delta_ssm_dt_bias', 'delta_ssm_a_log', 'delta_ssm_d', 'delta_ssm_norm_w', 'delta_attn_sinks', 'delta_w_branch_ssm', 'delta_w_branch_attn', 'delta_w_mix_out', 'delta_ln1_g', 'delta_ln1_b', 'delta_w_up', 'delta_ffn_conv_w', 'delta_ffn_conv_b', 'delta_w_down', 'delta_ln2_g', 'delta_ln2_b', 'new_m_rel_bias', 'new_m_w_in', 'new_m_b_gate', 'new_m_ssm_conv_w', 'new_m_ssm_conv_b', 'new_m_ssm_dt_bias', 'new_m_ssm_a_log', 'new_m_ssm_d', 'new_m_ssm_norm_w', 'new_m_attn_sinks', 'new_m_w_branch_ssm', 'new_m_w_branch_attn', 'new_m_w_mix_out', 'new_m_ln1_g', 'new_m_ln1_b', 'new_m_w_up', 'new_m_ffn_conv_w', 'new_m_ffn_conv_b', 'new_m_w_down', 'new_m_ln2_g', 'new_m_ln2_b', 'new_v_rel_bias', 'new_v_w_in', 'new_v_b_gate', 'new_v_ssm_conv_w', 'new_v_ssm_conv_b', 'new_v_ssm_dt_bias', 'new_v_ssm_a_log', 'new_v_ssm_d', 'new_v_ssm_norm_w', 'new_v_attn_sinks', 'new_v_w_branch_ssm', 'new_v_w_branch_attn', 'new_v_w_mix_out', 'new_v_ln1_g', 'new_v_ln1_b', 'new_v_w_up', 'new_v_ffn_conv_w', 'new_v_ffn_conv_b', 'new_v_w_down', 'new_v_ln2_g', 'new_v_ln2_b']
TWIN_LEAF_KINDS = {'loss': 'loss', 'grad_x': 'grad_x', 'grad_rel_bias': 'grad_w', 'grad_w_in': 'grad_w', 'grad_b_gate': 'grad_w', 'grad_ssm_conv_w': 'grad_w', 'grad_ssm_conv_b': 'grad_w', 'grad_ssm_dt_bias': 'grad_w', 'grad_ssm_a_log': 'grad_w', 'grad_ssm_d': 'grad_w', 'grad_ssm_norm_w': 'grad_w', 'grad_attn_sinks': 'grad_w', 'grad_w_branch_ssm': 'grad_w', 'grad_w_branch_attn': 'grad_w', 'grad_w_mix_out': 'grad_w', 'grad_ln1_g': 'grad_w', 'grad_ln1_b': 'grad_w', 'grad_w_up': 'grad_w', 'grad_ffn_conv_w': 'grad_w', 'grad_ffn_conv_b': 'grad_w', 'grad_w_down': 'grad_w', 'grad_ln2_g': 'grad_w', 'grad_ln2_b': 'grad_w', 'delta_rel_bias': 'delta_w', 'delta_w_in': 'delta_w', 'delta_b_gate': 'delta_w', 'delta_ssm_conv_w': 'delta_w', 'delta_ssm_conv_b': 'delta_w', 'delta_ssm_dt_bias': 'delta_w', 'delta_ssm_a_log': 'delta_w', 'delta_ssm_d': 'delta_w', 'delta_ssm_norm_w': 'delta_w', 'delta_attn_sinks': 'delta_w', 'delta_w_branch_ssm': 'delta_w', 'delta_w_branch_attn': 'delta_w', 'delta_w_mix_out': 'delta_w', 'delta_ln1_g': 'delta_w', 'delta_ln1_b': 'delta_w', 'delta_w_up': 'delta_w', 'delta_ffn_conv_w': 'delta_w', 'delta_ffn_conv_b': 'delta_w', 'delta_w_down': 'delta_w', 'delta_ln2_g': 'delta_w', 'delta_ln2_b': 'delta_w', 'new_m_rel_bias': 'new_m', 'new_m_w_in': 'new_m', 'new_m_b_gate': 'new_m', 'new_m_ssm_conv_w': 'new_m', 'new_m_ssm_conv_b': 'new_m', 'new_m_ssm_dt_bias': 'new_m', 'new_m_ssm_a_log': 'new_m', 'new_m_ssm_d': 'new_m', 'new_m_ssm_norm_w': 'new_m', 'new_m_attn_sinks': 'new_m', 'new_m_w_branch_ssm': 'new_m', 'new_m_w_branch_attn': 'new_m', 'new_m_w_mix_out': 'new_m', 'new_m_ln1_g': 'new_m', 'new_m_ln1_b': 'new_m', 'new_m_w_up': 'new_m', 'new_m_ffn_conv_w': 'new_m', 'new_m_ffn_conv_b': 'new_m', 'new_m_w_down': 'new_m', 'new_m_ln2_g': 'new_m', 'new_m_ln2_b': 'new_m', 'new_v_rel_bias': 'new_v', 'new_v_w_in': 'new_v', 'new_v_b_gate': 'new_v', 'new_v_ssm_conv_w': 'new_v', 'new_v_ssm_conv_b': 'new_v', 'new_v_ssm_dt_bias': 'new_v', 'new_v_ssm_a_log': 'new_v', 'new_v_ssm_d': 'new_v', 'new_v_ssm_norm_w': 'new_v', 'new_v_attn_sinks': 'new_v', 'new_v_w_branch_ssm': 'new_v', 'new_v_w_branch_attn': 'new_v', 'new_v_w_mix_out': 'new_v', 'new_v_ln1_g': 'new_v', 'new_v_ln1_b': 'new_v', 'new_v_w_up': 'new_v', 'new_v_ffn_conv_w': 'new_v', 'new_v_ffn_conv_b': 'new_v', 'new_v_w_down': 'new_v', 'new_v_ln2_g': 'new_v', 'new_v_ln2_b': 'new_v'}


def _forward(args):
    return _fwd_reference(*[args[k] for k in FWD_PARAMS])


def _output_shape():
    def fwd():
        inp = _fwd_setup_inputs(0)
        return _fwd_reference(*[inp[k] for k in FWD_PARAMS])
    out = _jax.eval_shape(fwd)
    return out.shape, out.dtype

N_MICROBATCH = 1
ADAM_LR = 0.001
ADAM_B1 = 0.9
ADAM_B2 = 0.999
ADAM_EPS = 1e-08
ADAM_WD = 0.01
ADAM_STEP = 10
PER_EXAMPLE_BATCH_AXIS = {'x': 0, 'loss_target': 0}
SHARED_INPUTS = []
_WEIGHT_DTYPES = {'rel_bias': _jnp.float32, 'w_in': _jnp.float32, 'b_gate': _jnp.float32, 'ssm_conv_w': _jnp.float32, 'ssm_conv_b': _jnp.float32, 'ssm_dt_bias': _jnp.float32, 'ssm_a_log': _jnp.float32, 'ssm_d': _jnp.float32, 'ssm_norm_w': _jnp.float32, 'attn_sinks': _jnp.float32, 'w_branch_ssm': _jnp.float32, 'w_branch_attn': _jnp.float32, 'w_mix_out': _jnp.float32, 'ln1_g': _jnp.float32, 'ln1_b': _jnp.float32, 'w_up': _jnp.float32, 'ffn_conv_w': _jnp.float32, 'ffn_conv_b': _jnp.float32, 'w_down': _jnp.float32, 'ln2_g': _jnp.float32, 'ln2_b': _jnp.float32}
MOMENT_SCALE = {'rel_bias': 1.436124e-02, 'w_in': 3.125279e-02, 'b_gate': 1.644670e-02, 'ssm_conv_w': 3.608753e-02, 'ssm_conv_b': 5.909501e-02, 'ssm_dt_bias': 2.049686e-01, 'ssm_a_log': 3.740345e-01, 'ssm_d': 2.184232e-01, 'ssm_norm_w': 4.178984e-02, 'attn_sinks': 9.459496e-03, 'w_branch_ssm': 1.003017e-01, 'w_branch_attn': 1.975975e-02, 'w_mix_out': 9.959986e-02, 'ln1_g': 1.704238e+00, 'ln1_b': 8.612248e-01, 'w_up': 3.602124e-02, 'ffn_conv_w': 2.140276e-02, 'ffn_conv_b': 3.815690e-02, 'w_down': 5.861744e-02, 'ln2_g': 1.280185e+02, 'ln2_b': 3.225096e+00}


def _to_microbatches(a, axis):
    t = _jnp.moveaxis(a, axis, 0)
    t = t.reshape((N_MICROBATCH, t.shape[0] // N_MICROBATCH) + t.shape[1:])
    return _jnp.moveaxis(t, 1, axis + 1)


def setup_inputs(seed: int = 0) -> dict:
    inp = _fwd_setup_inputs(seed)
    key = _jax.random.fold_in(_jax.random.key(seed), 7919)
    shape, _ = _output_shape()
    out = dict(inp)
    out["loss_target"] = _jax.random.normal(_jax.random.fold_in(key, 0), shape, _jnp.float32)
    for i, name in enumerate(TWIN_WEIGHTS):
        w = inp[name].astype(_jnp.float32)
        if MOMENT_SCALE is None:
            s = _jnp.sqrt(_jnp.mean(_jnp.square(w)) + 1e-30)
        else:
            s = MOMENT_SCALE[name]
        km, kv = _jax.random.split(_jax.random.fold_in(key, i + 1))
        out[name] = w
        out["m_" + name] = s * _jax.random.normal(km, w.shape, _jnp.float32)
        out["v_" + name] = (s * s) * _jax.random.uniform(kv, w.shape, _jnp.float32, 0.5, 1.5)
    if N_MICROBATCH > 1:
        for name, axis in PER_EXAMPLE_BATCH_AXIS.items():
            out[name] = _to_microbatches(out[name], axis)
    return {'x': out['x'], 'rel_bias': out['rel_bias'], 'w_in': out['w_in'], 'b_gate': out['b_gate'], 'ssm_conv_w': out['ssm_conv_w'], 'ssm_conv_b': out['ssm_conv_b'], 'ssm_dt_bias': out['ssm_dt_bias'], 'ssm_a_log': out['ssm_a_log'], 'ssm_d': out['ssm_d'], 'ssm_norm_w': out['ssm_norm_w'], 'attn_sinks': out['attn_sinks'], 'w_branch_ssm': out['w_branch_ssm'], 'w_branch_attn': out['w_branch_attn'], 'w_mix_out': out['w_mix_out'], 'ln1_g': out['ln1_g'], 'ln1_b': out['ln1_b'], 'w_up': out['w_up'], 'ffn_conv_w': out['ffn_conv_w'], 'ffn_conv_b': out['ffn_conv_b'], 'w_down': out['w_down'], 'ln2_g': out['ln2_g'], 'ln2_b': out['ln2_b'], 'loss_target': out['loss_target'], 'm_rel_bias': out['m_rel_bias'], 'm_w_in': out['m_w_in'], 'm_b_gate': out['m_b_gate'], 'm_ssm_conv_w': out['m_ssm_conv_w'], 'm_ssm_conv_b': out['m_ssm_conv_b'], 'm_ssm_dt_bias': out['m_ssm_dt_bias'], 'm_ssm_a_log': out['m_ssm_a_log'], 'm_ssm_d': out['m_ssm_d'], 'm_ssm_norm_w': out['m_ssm_norm_w'], 'm_attn_sinks': out['m_attn_sinks'], 'm_w_branch_ssm': out['m_w_branch_ssm'], 'm_w_branch_attn': out['m_w_branch_attn'], 'm_w_mix_out': out['m_w_mix_out'], 'm_ln1_g': out['m_ln1_g'], 'm_ln1_b': out['m_ln1_b'], 'm_w_up': out['m_w_up'], 'm_ffn_conv_w': out['m_ffn_conv_w'], 'm_ffn_conv_b': out['m_ffn_conv_b'], 'm_w_down': out['m_w_down'], 'm_ln2_g': out['m_ln2_g'], 'm_ln2_b': out['m_ln2_b'], 'v_rel_bias': out['v_rel_bias'], 'v_w_in': out['v_w_in'], 'v_b_gate': out['v_b_gate'], 'v_ssm_conv_w': out['v_ssm_conv_w'], 'v_ssm_conv_b': out['v_ssm_conv_b'], 'v_ssm_dt_bias': out['v_ssm_dt_bias'], 'v_ssm_a_log': out['v_ssm_a_log'], 'v_ssm_d': out['v_ssm_d'], 'v_ssm_norm_w': out['v_ssm_norm_w'], 'v_attn_sinks': out['v_attn_sinks'], 'v_w_branch_ssm': out['v_w_branch_ssm'], 'v_w_branch_attn': out['v_w_branch_attn'], 'v_w_mix_out': out['v_w_mix_out'], 'v_ln1_g': out['v_ln1_g'], 'v_ln1_b': out['v_ln1_b'], 'v_w_up': out['v_w_up'], 'v_ffn_conv_w': out['v_ffn_conv_w'], 'v_ffn_conv_b': out['v_ffn_conv_b'], 'v_w_down': out['v_w_down'], 'v_ln2_g': out['v_ln2_g'], 'v_ln2_b': out['v_ln2_b']}


def _loss(weights, diff, rest, loss_target):
    with _jax.named_scope("forward"):
        args = {**rest, TWIN_DIFF_INPUT: diff, **{k: w.astype(_WEIGHT_DTYPES[k]) for k, w in weights.items()}}
        y = _forward(args)
    with _jax.named_scope("loss_head"):
        err = _jnp.square(y.astype(_jnp.float32) - loss_target)
        return 0.5 * _jnp.sum(_jnp.mean(err, axis=-1)) if err.ndim else 0.5 * err


def _adamw(w, g, m, v):
    m = ADAM_B1 * m + (1.0 - ADAM_B1) * g
    v = ADAM_B2 * v + (1.0 - ADAM_B2) * _jnp.square(g)
    m_hat = m / (1.0 - ADAM_B1 ** ADAM_STEP)
    v_hat = v / (1.0 - ADAM_B2 ** ADAM_STEP)
    delta = -ADAM_LR * (m_hat / (_jnp.sqrt(v_hat) + ADAM_EPS) + ADAM_WD * w)
    return delta, m, v


def reference(x, rel_bias, w_in, b_gate, ssm_conv_w, ssm_conv_b, ssm_dt_bias, ssm_a_log, ssm_d, ssm_norm_w, attn_sinks, w_branch_ssm, w_branch_attn, w_mix_out, ln1_g, ln1_b, w_up, ffn_conv_w, ffn_conv_b, w_down, ln2_g, ln2_b, loss_target, m_rel_bias, m_w_in, m_b_gate, m_ssm_conv_w, m_ssm_conv_b, m_ssm_dt_bias, m_ssm_a_log, m_ssm_d, m_ssm_norm_w, m_attn_sinks, m_w_branch_ssm, m_w_branch_attn, m_w_mix_out, m_ln1_g, m_ln1_b, m_w_up, m_ffn_conv_w, m_ffn_conv_b, m_w_down, m_ln2_g, m_ln2_b, v_rel_bias, v_w_in, v_b_gate, v_ssm_conv_w, v_ssm_conv_b, v_ssm_dt_bias, v_ssm_a_log, v_ssm_d, v_ssm_norm_w, v_attn_sinks, v_w_branch_ssm, v_w_branch_attn, v_w_mix_out, v_ln1_g, v_ln1_b, v_w_up, v_ffn_conv_w, v_ffn_conv_b, v_w_down, v_ln2_g, v_ln2_b):
    given = dict(x=x, rel_bias=rel_bias, w_in=w_in, b_gate=b_gate, ssm_conv_w=ssm_conv_w, ssm_conv_b=ssm_conv_b, ssm_dt_bias=ssm_dt_bias, ssm_a_log=ssm_a_log, ssm_d=ssm_d, ssm_norm_w=ssm_norm_w, attn_sinks=attn_sinks, w_branch_ssm=w_branch_ssm, w_branch_attn=w_branch_attn, w_mix_out=w_mix_out, ln1_g=ln1_g, ln1_b=ln1_b, w_up=w_up, ffn_conv_w=ffn_conv_w, ffn_conv_b=ffn_conv_b, w_down=w_down, ln2_g=ln2_g, ln2_b=ln2_b, loss_target=loss_target, m_rel_bias=m_rel_bias, m_w_in=m_w_in, m_b_gate=m_b_gate, m_ssm_conv_w=m_ssm_conv_w, m_ssm_conv_b=m_ssm_conv_b, m_ssm_dt_bias=m_ssm_dt_bias, m_ssm_a_log=m_ssm_a_log, m_ssm_d=m_ssm_d, m_ssm_norm_w=m_ssm_norm_w, m_attn_sinks=m_attn_sinks, m_w_branch_ssm=m_w_branch_ssm, m_w_branch_attn=m_w_branch_attn, m_w_mix_out=m_w_mix_out, m_ln1_g=m_ln1_g, m_ln1_b=m_ln1_b, m_w_up=m_w_up, m_ffn_conv_w=m_ffn_conv_w, m_ffn_conv_b=m_ffn_conv_b, m_w_down=m_w_down, m_ln2_g=m_ln2_g, m_ln2_b=m_ln2_b, v_rel_bias=v_rel_bias, v_w_in=v_w_in, v_b_gate=v_b_gate, v_ssm_conv_w=v_ssm_conv_w, v_ssm_conv_b=v_ssm_conv_b, v_ssm_dt_bias=v_ssm_dt_bias, v_ssm_a_log=v_ssm_a_log, v_ssm_d=v_ssm_d, v_ssm_norm_w=v_ssm_norm_w, v_attn_sinks=v_attn_sinks, v_w_branch_ssm=v_w_branch_ssm, v_w_branch_attn=v_w_branch_attn, v_w_mix_out=v_w_mix_out, v_ln1_g=v_ln1_g, v_ln1_b=v_ln1_b, v_w_up=v_w_up, v_ffn_conv_w=v_ffn_conv_w, v_ffn_conv_b=v_ffn_conv_b, v_w_down=v_w_down, v_ln2_g=v_ln2_g, v_ln2_b=v_ln2_b)
    weights = {n: given[n] for n in TWIN_WEIGHTS}
    shared = {n: given[n] for n in SHARED_INPUTS}
    per_example = {n: given[n] for n in ['x']}
    grad_fn = _jax.value_and_grad(_loss, argnums=(0, 1))

    def one_microbatch(ex, loss_target):
        ex = dict(ex)
        diff = ex.pop(TWIN_DIFF_INPUT)
        return grad_fn(weights, diff, {**shared, **ex}, loss_target)

    if N_MICROBATCH == 1:
        loss, (grad_w, grad_x) = one_microbatch(per_example, given["loss_target"])
    else:
        def body(carry, xs):
            loss_sum, grad_sum = carry
            l_k, (gw_k, gx_k) = one_microbatch(xs[0], xs[1])
            with _jax.named_scope("update"):
                return (loss_sum + l_k, _jax.tree.map(_jnp.add, grad_sum, gw_k)), gx_k

        init = (_jnp.zeros((), _jnp.float32), _jax.tree.map(_jnp.zeros_like, weights))
        (loss, grad_w), grad_x = _jax.lax.scan(body, init, (per_example, given["loss_target"]))
    with _jax.named_scope("update"):
        delta_w, new_m, new_v = {}, {}, {}
        for n in TWIN_WEIGHTS:
            delta_w[n], new_m[n], new_v[n] = _adamw(weights[n], grad_w[n], given["m_" + n], given["v_" + n])
    return (loss, grad_x, *[grad_w[n] for n in TWIN_WEIGHTS], *[delta_w[n] for n in TWIN_WEIGHTS],
            *[new_m[n] for n in TWIN_WEIGHTS], *[new_v[n] for n in TWIN_WEIGHTS])
```

```python
import functools
import math

import numpy as np
import jax
import jax.numpy as jnp
from jax import lax
from jax.experimental import pallas as pl
from jax.experimental.pallas import tpu as pltpu

F32 = jnp.float32
_ACT = jnp.bfloat16
_MXU = jnp.bfloat16

D_MODEL = 1024
D_INNER = 2048
N_HEADS = 32
HEAD_P = 64
N_GROUPS = 4
N_STATE = 128
CHUNK = 128
CONV_DIM = 3072
SSM_K = 4
A_HEADS = 16
A_DH = 64
WIN = 128
REL_BUCKETS = 32
D_FF = 2816
FFN_K = 3
ALPHA = 2.0 ** 0.25
LN_EPS = 1e-5
RMS_EPS = 1e-5
IN_COLS = 8480
NEG = -1e30

ADAM_LR = 0.001
ADAM_B1 = 0.9
ADAM_B2 = 0.999
ADAM_EPS = 1e-08
ADAM_WD = 0.01
ADAM_STEP = 10

LANE = 128
SUB = 8

P_Z, P_XS, P_G, P_Q, P_BC, P_K, P_V, P_DT = 0, 2048, 4096, 6144, 7168, 8192, 8320, 8448
P_W = 8704
_PIECES = ((0, 2048, P_Z), (2048, 2048, P_XS), (4096, 1024, P_BC), (5120, 32, P_DT), (5152, 1024, P_Q),
           (6176, 128, P_K), (6304, 128, P_V), (6432, 2048, P_G))

N_CHIPS = 4
N_DEV = 8


def _cp(sem=None, vmem_mb=48):
    return pltpu.CompilerParams(dimension_semantics=sem, vmem_limit_bytes=vmem_mb * 1024 * 1024)


def _pick(n, cands):
    for c in cands:
        if n % c == 0:
            return c
    raise ValueError(f"no block size for {n}")


def _rows8(p):
    k, c = p.shape
    return jnp.concatenate([p.astype(F32), jnp.zeros((SUB - k, c), F32)], axis=0)


def _mm(a, b, name, *, trans_a=False, out_dtype=F32, res=None, res_scale=1.0):
    if trans_a:
        k_dim, m = a.shape
    else:
        m, k_dim = a.shape
    k2, n = b.shape
    assert k_dim == k2, (a.shape, b.shape)
    tm = _pick(m, (512, 256, 128))
    tn = _pick(n, (1024, 512, 256, 128))
    tk = _pick(k_dim, (1024, 512, 256, 128))
    nk = k_dim // tk
    dn = (((0,), (0,)), ((), ())) if trans_a else (((1,), (0,)), ((), ()))

    def body(*refs):
        if res is None:
            a_ref, b_ref, o_ref, acc = refs
        else:
            a_ref, b_ref, r_ref, o_ref, acc = refs
        k = pl.program_id(2)

        @pl.when(k == 0)
        def _():
            acc[...] = jnp.zeros_like(acc)

        acc[...] += lax.dot_general(a_ref[...].astype(_MXU), b_ref[...].astype(_MXU), dn,
                                    preferred_element_type=F32)

        @pl.when(k == nk - 1)
        def _():
            r = acc[...]
            if res is not None:
                r = r + res_scale * r_ref[...]
            o_ref[...] = r.astype(out_dtype)

    if trans_a:
        a_spec = pl.BlockSpec((tk, tm), lambda i, j, k: (k, i))
    else:
        a_spec = pl.BlockSpec((tm, tk), lambda i, j, k: (i, k))
    in_specs = [a_spec, pl.BlockSpec((tk, tn), lambda i, j, k: (k, j))]
    args = [a, b]
    if res is not None:
        in_specs.append(pl.BlockSpec((tm, tn), lambda i, j, k: (i, j)))
        args.append(res)
    return pl.pallas_call(
        body, name=name, out_shape=jax.ShapeDtypeStruct((m, n), out_dtype),
        grid=(m // tm, n // tn, nk), in_specs=in_specs,
        out_specs=pl.BlockSpec((tm, tn), lambda i, j, k: (i, j)),
        scratch_shapes=[pltpu.VMEM((tm, tn), F32)],
        compiler_params=_cp(("parallel", "parallel", "arbitrary")),
    )(*args)


def _shift_down(cur, prev8, s):
    r = pltpu.roll(cur, s, 0)
    p = pltpu.roll(prev8, s, 0)
    row8 = lax.broadcasted_iota(jnp.int32, (SUB, 1), 0)
    fixed = jnp.where(row8 < s, p, r[0:SUB])
    return jnp.concatenate([fixed, r[SUB:]], axis=0)


def _shift_up(cur, next8, s):
    tm = cur.shape[0]
    r = pltpu.roll(cur, tm - s, 0)
    p = pltpu.roll(next8, SUB - s, 0)
    row8 = lax.broadcasted_iota(jnp.int32, (SUB, 1), 0)
    fixed = jnp.where(row8 >= SUB - s, p, r[tm - SUB:])
    return jnp.concatenate([r[:tm - SUB], fixed], axis=0)


def _conv_pre(cur, prev8, w_ref, b_row, taps):
    acc = cur * w_ref[taps - 1:taps, :] + b_row
    for s in range(1, taps):
        acc = acc + _shift_down(cur, prev8, s) * w_ref[taps - 1 - s:taps - s, :]
    return acc


def _dot01_r(x, m01, parts=3):
    acc = None
    r = x
    for _ in range(parts):
        hi = r.astype(jnp.bfloat16)
        t = jnp.dot(hi, m01, preferred_element_type=F32)
        acc = t if acc is None else acc + t
        r = r - hi.astype(F32)
    return acc


def _dot01_l(m01, x, parts=3):
    acc = None
    r = x
    for _ in range(parts):
        hi = r.astype(jnp.bfloat16)
        t = jnp.dot(m01, hi, preferred_element_type=F32)
        acc = t if acc is None else acc + t
        r = r - hi.astype(F32)
    return acc


def _dot(a, b):
    return jnp.dot(a.astype(_MXU), b.astype(_MXU), preferred_element_type=F32)


def _dot_nt(a, b):
    return lax.dot_general(a.astype(_MXU), b.astype(_MXU), (((1,), (1,)), ((), ())), preferred_element_type=F32)


def _dot_tn(a, b):
    return lax.dot_general(a.astype(_MXU), b.astype(_MXU), (((0,), (0,)), ((), ())), preferred_element_type=F32)


def _sigmoid(x):
    return 1.0 / (1.0 + jnp.exp(-x))


def _half_masks():
    lane = lax.broadcasted_iota(jnp.int32, (1, LANE), 1)
    lo = (lane < 64).astype(F32)
    return lo, 1.0 - lo


_TC = 512


def _tm_rows(t):
    return min(256, t)


def _conv_silu_fwd(proj, colblk0, nblk, w8, b8, name):
    t = proj.shape[0]
    tm = _tm_rows(t)

    def body(c_ref, p_ref, w_ref, b_ref, o_ref):
        i = pl.program_id(1)
        prev8 = jnp.where(i > 0, p_ref[...], 0.0)
        pre = _conv_pre(c_ref[...], prev8, w_ref, b_ref[0:1, :], SSM_K)
        o_ref[...] = pre * _sigmoid(pre)

    return pl.pallas_call(
        body, name=name, out_shape=jax.ShapeDtypeStruct((t, nblk * _TC), F32),
        grid=(nblk, t // tm),
        in_specs=[pl.BlockSpec((tm, _TC), lambda j, i: (i, colblk0 + j)),
                  pl.BlockSpec((SUB, _TC), lambda j, i: (jnp.maximum(i * (tm // SUB) - 1, 0), colblk0 + j)),
                  pl.BlockSpec((SUB, _TC), lambda j, i: (0, j)),
                  pl.BlockSpec((SUB, _TC), lambda j, i: (0, j))],
        out_specs=pl.BlockSpec((tm, _TC), lambda j, i: (i, j)),
        compiler_params=_cp(("parallel", "parallel")),
    )(proj, proj, w8, b8)


def _conv_silu_bwd(proj, colblk0, nblk, w8, b8, d_out, name):
    t = proj.shape[0]
    tm = _tm_rows(t)

    def body(c_ref, p_ref, w_ref, b_ref, d_ref, dp_ref, acc_ref):
        i = pl.program_id(1)

        @pl.when(i == 0)
        def _():
            acc_ref[...] = jnp.zeros_like(acc_ref)

        cur = c_ref[...]
        prev8 = jnp.where(i > 0, p_ref[...], 0.0)
        pre = _conv_pre(cur, prev8, w_ref, b_ref[0:1, :], SSM_K)
        sg = _sigmoid(pre)
        dpre = d_ref[...] * (sg * (1.0 + pre * (1.0 - sg)))
        dp_ref[...] = dpre
        acc_ref[SSM_K:SSM_K + 1, :] += jnp.sum(dpre, axis=0, keepdims=True)
        acc_ref[SSM_K - 1:SSM_K, :] += jnp.sum(dpre * cur, axis=0, keepdims=True)
        for s in range(1, SSM_K):
            acc_ref[SSM_K - 1 - s:SSM_K - s, :] += jnp.sum(dpre * _shift_down(cur, prev8, s), axis=0, keepdims=True)

    c = nblk * _TC
    return pl.pallas_call(
        body, name=name,
        out_shape=(jax.ShapeDtypeStruct((t, c), F32), jax.ShapeDtypeStruct((SUB, c), F32)),
        grid=(nblk, t // tm),
        in_specs=[pl.BlockSpec((tm, _TC), lambda j, i: (i, colblk0 + j)),
                  pl.BlockSpec((SUB, _TC), lambda j, i: (jnp.maximum(i * (tm // SUB) - 1, 0), colblk0 + j)),
                  pl.BlockSpec((SUB, _TC), lambda j, i: (0, j)),
                  pl.BlockSpec((SUB, _TC), lambda j, i: (0, j)),
                  pl.BlockSpec((tm, _TC), lambda j, i: (i, j))],
        out_specs=(pl.BlockSpec((tm, _TC), lambda j, i: (i, j)),
                   pl.BlockSpec((SUB, _TC), lambda j, i: (0, j))),
        compiler_params=_cp(("parallel", "arbitrary")),
    )(proj, proj, w8, b8, d_out)


def _conv_transpose(d_pre, w8, taps, name):
    t, c = d_pre.shape
    tm = _tm_rows(t)
    nblk = c // _TC
    last8 = t // SUB - 1

    def body(c_ref, n_ref, w_ref, o_ref):
        i = pl.program_id(1)
        cur = c_ref[...]
        next8 = jnp.where(i < t // tm - 1, n_ref[...], 0.0)
        acc = cur * w_ref[taps - 1:taps, :]
        for s in range(1, taps):
            acc = acc + _shift_up(cur, next8, s) * w_ref[taps - 1 - s:taps - s, :]
        o_ref[...] = acc.astype(_ACT)

    return pl.pallas_call(
        body, name=name, out_shape=jax.ShapeDtypeStruct((t, c), _ACT),
        grid=(nblk, t // tm),
        in_specs=[pl.BlockSpec((tm, _TC), lambda j, i: (i, j)),
                  pl.BlockSpec((SUB, _TC), lambda j, i: (jnp.minimum((i + 1) * (tm // SUB), last8), j)),
                  pl.BlockSpec((SUB, _TC), lambda j, i: (0, j))],
        out_specs=pl.BlockSpec((tm, _TC), lambda j, i: (i, j)),
        compiler_params=_cp(("parallel", "parallel")),
    )(d_pre, d_pre, w8)


def _expand_consts():
    e = np.zeros((LANE, D_INNER), np.float32)
    for h in range(N_HEADS):
        e[h, h * HEAD_P:(h + 1) * HEAD_P] = 1.0
    return jnp.asarray(e, jnp.bfloat16), jnp.asarray(e.T.copy(), jnp.bfloat16)


def _ssd_common(dtr_ref, dtb_ref, alog_ref, e_ref):
    lane = lax.broadcasted_iota(jnp.int32, (1, LANE), 1)
    hm = lane < N_HEADS
    pre = dtr_ref[...] + dtb_ref[0:1, :]
    dt = jnp.where(hm, jnp.maximum(pre, 0.0) + jnp.log(1.0 + jnp.exp(-jnp.abs(pre))), 0.0)
    a_row = jnp.where(hm, -jnp.exp(alog_ref[0:1, :]), 0.0)
    adt = dt * a_row
    r = lax.broadcasted_iota(jnp.int32, (CHUNK, CHUNK), 0)
    c = lax.broadcasted_iota(jnp.int32, (CHUNK, CHUNK), 1)
    causal = r >= c
    acs = _dot01_l(causal.astype(jnp.bfloat16), adt)
    e = e_ref[...]
    acs_x = _dot01_r(acs, e)
    dt_x = _dot01_r(dt, e)
    return pre, dt, a_row, acs, acs_x, dt_x, causal, hm


def _decay(acs, acs_t, h, causal):
    seg = acs[:, h:h + 1] - acs_t[h:h + 1, :]
    return jnp.exp(jnp.where(causal, seg, NEG))


def _ssd_fwd(xs_c, bc_c, proj, dtb8, alog8, dsk8, nw8, name):
    t = xs_c.shape[0]
    nc = t // CHUNK
    e_bf, _ = _expand_consts()
    gw = D_INNER // N_GROUPS

    def body(xs_ref, bc_ref, dtr_ref, z_ref, dtb_ref, alog_ref, dsk_ref, nw_ref, e_ref,
             y_ref, ys_ref, hp_ref, h_ref):
        c_id = pl.program_id(0)

        @pl.when(c_id == 0)
        def _():
            h_ref[...] = jnp.zeros_like(h_ref)

        _, dt, a_row, acs, acs_x, dt_x, causal, _ = _ssd_common(dtr_ref, dtb_ref, alog_ref, e_ref)
        acs_t = acs.T
        xs = xs_ref[...]
        x_dt = xs * dt_x
        last_x = acs_x[CHUNK - 1:CHUNK, :]
        w_end = jnp.exp(last_x - acs_x)
        e_in = jnp.exp(acs_x)
        d_x = _dot01_r(dsk_ref[...], e_ref[...])[0:1, :]
        hprev = h_ref[...]
        hp_ref[...] = hprev
        lo, hi = _half_masks()
        for g in range(N_GROUPS):
            bg = bc_ref[:, g * N_STATE:(g + 1) * N_STATE]
            cg = bc_ref[:, N_GROUPS * N_STATE + g * N_STATE:N_GROUPS * N_STATE + (g + 1) * N_STATE]
            sl = slice(g * gw, (g + 1) * gw)
            gm = _dot_nt(cg, bg)
            st = _dot(bg.T, x_dt[:, sl] * w_end[:, sl])
            y_off = _dot(cg, hprev[:, sl]) * e_in[:, sl]
            for j in range(gw // LANE):
                h0 = g * (gw // HEAD_P) + 2 * j
                cs = slice(g * gw + j * LANE, g * gw + (j + 1) * LANE)
                xp = x_dt[:, cs]
                m0 = gm * _decay(acs, acs_t, h0, causal)
                m1 = gm * _decay(acs, acs_t, h0 + 1, causal)
                yd = _dot(m0, xp * lo) + _dot(m1, xp * hi)
                y_ref[:, cs] = yd + y_off[:, j * LANE:(j + 1) * LANE] + xs[:, cs] * d_x[:, cs]
            h_ref[:, sl] = hprev[:, sl] * jnp.exp(last_x[:, sl]) + st
        y = y_ref[...]
        z = z_ref[...]
        y2 = y * (z * _sigmoid(z))
        for g in range(N_GROUPS):
            sl = slice(g * gw, (g + 1) * gw)
            yg = y2[:, sl]
            rinv = lax.rsqrt(jnp.mean(yg * yg, axis=-1, keepdims=True) + RMS_EPS)
            ys_ref[:, sl] = (yg * rinv * nw_ref[0:1, sl]).astype(_ACT)

    small = pl.BlockSpec((SUB, LANE), lambda c: (0, 0))
    return pl.pallas_call(
        body, name=name,
        out_shape=(jax.ShapeDtypeStruct((t, D_INNER), F32), jax.ShapeDtypeStruct((t, D_INNER), _ACT),
                   jax.ShapeDtypeStruct((t, D_INNER), F32)),
        grid=(nc,),
        in_specs=[pl.BlockSpec((CHUNK, D_INNER), lambda c: (c, 0)),
                  pl.BlockSpec((CHUNK, 1024), lambda c: (c, 0)),
                  pl.BlockSpec((CHUNK, LANE), lambda c: (c, P_DT // LANE)),
                  pl.BlockSpec((CHUNK, D_INNER), lambda c: (c, P_Z // D_INNER)),
                  small, small, small,
                  pl.BlockSpec((SUB, D_INNER), lambda c: (0, 0)),
                  pl.BlockSpec((LANE, D_INNER), lambda c: (0, 0))],
        out_specs=(pl.BlockSpec((CHUNK, D_INNER), lambda c: (c, 0)),
                   pl.BlockSpec((CHUNK, D_INNER), lambda c: (c, 0)),
                   pl.BlockSpec((N_STATE, D_INNER), lambda c: (c, 0))),
        scratch_shapes=[pltpu.VMEM((N_STATE, D_INNER), F32)],
        compiler_params=_cp(("arbitrary",)),
    )(xs_c, bc_c, proj, proj, dtb8, alog8, dsk8, nw8, e_bf)


def _ssd_bwd(d_ys, y, xs_c, bc_c, proj, hprev_all, dtb8, alog8, dsk8, nw8, name):
    t = xs_c.shape[0]
    nc = t // CHUNK
    e_bf, et_bf = _expand_consts()
    gw = D_INNER // N_GROUPS

    def body(dys_ref, y_ref, xs_ref, bc_ref, dtr_ref, z_ref, hp_ref, dtb_ref, alog_ref, dsk_ref, nw_ref,
             e_ref, et_ref, dxs_ref, dbc_ref, dz_ref, ddt_ref, acc_ref, dnw_ref, dh_ref, dx_ref):
        step = pl.program_id(0)

        @pl.when(step == 0)
        def _():
            dh_ref[...] = jnp.zeros_like(dh_ref)
            acc_ref[...] = jnp.zeros_like(acc_ref)
            dnw_ref[...] = jnp.zeros_like(dnw_ref)

        pre, dt, a_row, acs, acs_x, dt_x, causal, hm = _ssd_common(dtr_ref, dtb_ref, alog_ref, e_ref)
        acs_t = acs.T
        et = et_ref[...]
        xs = xs_ref[...]
        x_dt = xs * dt_x
        last_x = acs_x[CHUNK - 1:CHUNK, :]
        w_end = jnp.exp(last_x - acs_x)
        e_in = jnp.exp(acs_x)
        e_last = jnp.exp(last_x)
        d_x = _dot01_r(dsk_ref[...], e_ref[...])[0:1, :]

        y = y_ref[...]
        z = z_ref[...]
        sz = _sigmoid(z)
        gz = z * sz
        y2 = y * gz
        dys = dys_ref[...]
        for g in range(N_GROUPS):
            sl = slice(g * gw, (g + 1) * gw)
            yg = y2[:, sl]
            rinv = lax.rsqrt(jnp.mean(yg * yg, axis=-1, keepdims=True) + RMS_EPS)
            nrm = yg * rinv
            dn = dys[:, sl] * nw_ref[0:1, sl]
            dnw_ref[0:1, sl] += jnp.sum(dys[:, sl] * nrm, axis=0, keepdims=True)
            dx_ref[:, sl] = rinv * (dn - nrm * jnp.mean(dn * nrm, axis=-1, keepdims=True))
        dy2 = dx_ref[...]
        dy = dy2 * gz
        dz_ref[...] = (dy2 * y * (sz * (1.0 + z * (1.0 - sz)))).astype(_ACT)

        dh_next = dh_ref[...]
        hprev = hp_ref[...]
        lo, hi = _half_masks()
        r = lax.broadcasted_iota(jnp.int32, (CHUNK, CHUNK), 0)
        c = lax.broadcasted_iota(jnp.int32, (CHUNK, CHUNK), 1)
        from_here = (c >= r).astype(jnp.bfloat16)
        before = c < r
        lane = lax.broadcasted_iota(jnp.int32, (1, LANE), 1)
        da_intra = jnp.zeros((CHUNK, LANE), F32)
        v_seg = jnp.zeros((CHUNK, LANE), F32)
        z_seg = jnp.zeros((CHUNK, LANE), F32)
        tail_parts = []
        for g in range(N_GROUPS):
            bg = bc_ref[:, g * N_STATE:(g + 1) * N_STATE]
            cg = bc_ref[:, N_GROUPS * N_STATE + g * N_STATE:N_GROUPS * N_STATE + (g + 1) * N_STATE]
            sl = slice(g * gw, (g + 1) * gw)
            et_g = et_ref[g * gw:(g + 1) * gw, :]
            gm = _dot_nt(cg, bg)
            dzg = e_in[:, sl] * dy[:, sl]
            dcg = _dot_nt(dzg, hprev[:, sl])
            dh_c = _dot(cg.T, dzg)
            q = _dot(bg, dh_next[:, sl])
            dbg = _dot_nt(x_dt[:, sl] * w_end[:, sl], dh_next[:, sl])
            y_off = _dot(cg, hprev[:, sl]) * e_in[:, sl]
            v_seg = v_seg + _dot01_r(dy[:, sl] * y_off, et_g)
            z_seg = z_seg + _dot01_r(w_end[:, sl] * q * x_dt[:, sl], et_g)
            dgm = jnp.zeros((CHUNK, CHUNK), F32)
            for j in range(gw // LANE):
                h0 = g * (gw // HEAD_P) + 2 * j
                cs = slice(g * gw + j * LANE, g * gw + (j + 1) * LANE)
                xp = x_dt[:, cs]
                dyp = dy[:, cs]
                dxd = jnp.zeros((CHUNK, LANE), F32)
                for half, msk in ((0, lo), (1, hi)):
                    lam = _decay(acs, acs_t, h0 + half, causal)
                    mm = gm * lam
                    dym = dyp * msk
                    dmm = _dot_nt(dym, xp)
                    dxd = dxd + _dot_tn(mm, dym)
                    dgm = dgm + dmm * lam
                    below = _dot(from_here, dmm * mm)
                    col = jnp.sum(jnp.where(before, below, 0.0), axis=-1, keepdims=True)
                    da_intra = da_intra + jnp.where(lane == h0 + half, col, 0.0)
                dx_ref[:, cs] = dxd + w_end[:, cs] * q[:, j * LANE:(j + 1) * LANE]
            dbc_ref[:, N_GROUPS * N_STATE + g * N_STATE:N_GROUPS * N_STATE + (g + 1) * N_STATE] = dcg + _dot(dgm, bg)
            dbc_ref[:, g * N_STATE:(g + 1) * N_STATE] = dbg + _dot_tn(dgm, cg)
            dh_ref[:, sl] = e_last[:, sl] * dh_next[:, sl] + dh_c
            tail_parts.append(e_last[:, sl] * jnp.sum(dh_next[:, sl] * hprev[:, sl], axis=0, keepdims=True))
        dxt = dx_ref[...]

        u_seg = _dot01_r(xs * dxt, et)
        q_full = jnp.concatenate(tail_parts, axis=1)
        t_row = _dot01_r(jnp.broadcast_to(q_full, (SUB, D_INNER)), et)[0:1, :]
        d_alpha = (da_intra + _dot01_l(from_here, v_seg) + _dot01_l(before.astype(jnp.bfloat16), z_seg) + t_row)
        d_dt = a_row * d_alpha + u_seg
        sgp = _sigmoid(pre)
        d_raw = jnp.where(hm, d_dt * sgp, 0.0)
        ddt_ref[...] = d_raw.astype(_ACT)
        acc_ref[0:1, :] += jnp.sum(d_raw, axis=0, keepdims=True)
        acc_ref[1:2, :] += jnp.sum(d_alpha * dt, axis=0, keepdims=True) * a_row
        dd_row = jnp.sum(dy * xs, axis=0, keepdims=True)
        acc_ref[2:3, :] += _dot01_r(jnp.broadcast_to(dd_row, (SUB, D_INNER)), et)[0:1, :]
        dxs_ref[...] = dy * d_x + dxt * dt_x

    rev = lambda c: (nc - 1 - c, 0)
    small = pl.BlockSpec((SUB, LANE), lambda c: (0, 0))
    return pl.pallas_call(
        body, name=name,
        out_shape=(jax.ShapeDtypeStruct((t, D_INNER), F32), jax.ShapeDtypeStruct((t, 1024), F32),
                   jax.ShapeDtypeStruct((t, D_INNER), _ACT), jax.ShapeDtypeStruct((t, LANE), _ACT),
                   jax.ShapeDtypeStruct((SUB, LANE), F32), jax.ShapeDtypeStruct((SUB, D_INNER), F32)),
        grid=(nc,),
        in_specs=[pl.BlockSpec((CHUNK, D_INNER), rev),
                  pl.BlockSpec((CHUNK, D_INNER), rev),
                  pl.BlockSpec((CHUNK, D_INNER), rev),
                  pl.BlockSpec((CHUNK, 1024), rev),
                  pl.BlockSpec((CHUNK, LANE), lambda c: (nc - 1 - c, P_DT // LANE)),
                  pl.BlockSpec((CHUNK, D_INNER), lambda c: (nc - 1 - c, P_Z // D_INNER)),
                  pl.BlockSpec((N_STATE, D_INNER), rev),
                  small, small, small,
                  pl.BlockSpec((SUB, D_INNER), lambda c: (0, 0)),
                  pl.BlockSpec((LANE, D_INNER), lambda c: (0, 0)),
                  pl.BlockSpec((D_INNER, LANE), lambda c: (0, 0))],
        out_specs=(pl.BlockSpec((CHUNK, D_INNER), rev),
                   pl.BlockSpec((CHUNK, 1024), rev),
                   pl.BlockSpec((CHUNK, D_INNER), rev),
                   pl.BlockSpec((CHUNK, LANE), rev),
                   small,
                   pl.BlockSpec((SUB, D_INNER), lambda c: (0, 0))),
        scratch_shapes=[pltpu.VMEM((N_STATE, D_INNER), F32), pltpu.VMEM((CHUNK, D_INNER), F32)],
        compiler_params=_cp(("arbitrary",), vmem_mb=56),
    )(d_ys, y, xs_c, bc_c, proj, proj, hprev_all, dtb8, alog8, dsk8, nw8, e_bf, et_bf)


def _rel_tables():
    qi = np.arange(WIN)[:, None] + WIN
    kj = np.arange(2 * WIN)[None, :]
    rel = qi - kj
    n = np.maximum(rel, 0)
    max_exact = REL_BUCKETS // 2
    nf = np.maximum(n, 1).astype(np.float32)
    large = max_exact + (np.log(nf / np.float32(max_exact)) / np.float32(math.log(WIN / max_exact))
                         * np.float32(REL_BUCKETS - max_exact)).astype(np.int32)
    large = np.minimum(large, REL_BUCKETS - 1)
    bucket = np.where(n < max_exact, n, large)
    valid = (rel >= 0) & (rel < WIN)
    onehot = np.zeros((REL_BUCKETS, WIN * 2 * WIN), np.float32)
    flat_b = bucket.reshape(-1)
    flat_v = valid.reshape(-1)
    idx = np.arange(WIN * 2 * WIN)
    onehot[flat_b[flat_v], idx[flat_v]] = 1.0
    return onehot, flat_v.astype(np.float32)[None, :]


def _bias_expand(rel_bias_t, name):
    onehot, valid = _rel_tables()

    def body(rb_ref, oh_ref, v_ref, o_ref):
        o_ref[...] = jnp.where(v_ref[...] > 0.5, _dot01_r(rb_ref[...], oh_ref[...]), NEG)

    return pl.pallas_call(
        body, name=name, out_shape=jax.ShapeDtypeStruct((A_HEADS, WIN * 2 * WIN), F32),
        compiler_params=_cp(None),
    )(rel_bias_t, jnp.asarray(onehot, jnp.bfloat16), jnp.asarray(valid, F32))


def _bias_reduce(dbias, name):
    onehot, _ = _rel_tables()

    def body(d_ref, oh_ref, o_ref):
        acc = None
        r = d_ref[...]
        for _ in range(3):
            hi = r.astype(jnp.bfloat16)
            tt = lax.dot_general(hi, oh_ref[...], (((1,), (1,)), ((), ())), preferred_element_type=F32)
            acc = tt if acc is None else acc + tt
            r = r - hi.astype(F32)
        o_ref[...] = acc

    return pl.pallas_call(
        body, name=name, out_shape=jax.ShapeDtypeStruct((A_HEADS, REL_BUCKETS), F32),
        compiler_params=_cp(None),
    )(dbias, jnp.asarray(onehot, jnp.bfloat16))


def _attn_bands(kc_ref, kp_ref, vc_ref, vp_ref, has_prev):
    lo, hi = _half_masks()
    kb = jnp.concatenate([jnp.where(has_prev, kp_ref[...], 0.0), kc_ref[...]], axis=0)
    vb = jnp.concatenate([jnp.where(has_prev, vp_ref[...], 0.0), vc_ref[...]], axis=0)
    kr = pltpu.roll(kb, 64, 1)
    vr = pltpu.roll(vb, 64, 1)
    kk = ((kb * lo, kr * hi), (kr * lo, kb * hi))
    vv = ((vb * lo, vr * hi), (vr * lo, vb * hi))
    return kk, vv


def _attn_probs(qp, k_half, bias_h, sink, col_ok):
    logits = _dot_nt(qp, k_half) * (A_DH ** -0.5) + bias_h
    logits = jnp.where(col_ok, logits, NEG)
    m = jnp.maximum(jnp.max(logits, axis=-1, keepdims=True), sink)
    p = jnp.exp(logits - m)
    es = jnp.exp(sink - m)
    inv = 1.0 / (jnp.sum(p, axis=-1, keepdims=True) + es)
    return p * inv, es * inv


def _attn_fwd(proj, bias, sinks8, name):
    t = proj.shape[0]
    nb = t // WIN

    def body(q_ref, kc_ref, kp_ref, vc_ref, vp_ref, b_ref, s_ref, o_ref):
        n = pl.program_id(0)
        has_prev = n > 0
        kk, vv = _attn_bands(kc_ref, kp_ref, vc_ref, vp_ref, has_prev)
        col = lax.broadcasted_iota(jnp.int32, (1, 2 * WIN), 1)
        col_ok = jnp.logical_or(has_prev, col >= WIN)
        for j in range(A_HEADS // 2):
            kv = (2 * j) // (A_HEADS // 2)
            qp = q_ref[:, j * LANE:(j + 1) * LANE]
            out = None
            for half in range(2):
                h = 2 * j + half
                probs, _ = _attn_probs(qp, kk[kv][half], b_ref[h], s_ref[0:1, h:h + 1], col_ok)
                o = _dot(probs, vv[kv][half])
                out = o if out is None else out + o
            o_ref[:, j * LANE:(j + 1) * LANE] = out.astype(_ACT)

    kvspec = lambda col, prev: pl.BlockSpec(
        (WIN, LANE), (lambda n: (jnp.maximum(n - 1, 0), col)) if prev else (lambda n: (n, col)))
    return pl.pallas_call(
        body, name=name, out_shape=jax.ShapeDtypeStruct((t, D_MODEL), _ACT),
        grid=(nb,),
        in_specs=[pl.BlockSpec((WIN, 1024), lambda n: (n, P_Q // 1024)),
                  kvspec(P_K // LANE, False), kvspec(P_K // LANE, True),
                  kvspec(P_V // LANE, False), kvspec(P_V // LANE, True),
                  pl.BlockSpec((A_HEADS, WIN, 2 * WIN), lambda n: (0, 0, 0)),
                  pl.BlockSpec((SUB, LANE), lambda n: (0, 0))],
        out_specs=pl.BlockSpec((WIN, 1024), lambda n: (n, 0)),
        compiler_params=_cp(("parallel",)),
    )(proj, proj, proj, proj, proj, bias, sinks8)


def _attn_bwd(proj, bias, sinks8, d_out, name):
    t = proj.shape[0]
    nb = t // WIN

    def body(q_ref, kc_ref, kp_ref, vc_ref, vp_ref, b_ref, s_ref, do_ref,
             dq_ref, dk_ref, dv_ref, db_ref, ds_ref, ck_ref, cv_ref, sacc_ref):
        n = pl.program_id(0)

        @pl.when(n == 0)
        def _():
            db_ref[...] = jnp.zeros_like(db_ref)
            sacc_ref[...] = jnp.zeros_like(sacc_ref)
            ck_ref[...] = jnp.zeros_like(ck_ref)
            cv_ref[...] = jnp.zeros_like(cv_ref)

        @pl.when(n < nb)
        def _():
            has_prev = n > 0
            kk, vv = _attn_bands(kc_ref, kp_ref, vc_ref, vp_ref, has_prev)
            lo, hi = _half_masks()
            col = lax.broadcasted_iota(jnp.int32, (1, 2 * WIN), 1)
            col_ok = jnp.logical_or(has_prev, col >= WIN)
            lane = lax.broadcasted_iota(jnp.int32, (1, LANE), 1)
            dk_acc = [[jnp.zeros((2 * WIN, LANE), F32) for _ in range(2)] for _ in range(2)]
            dv_acc = [[jnp.zeros((2 * WIN, LANE), F32) for _ in range(2)] for _ in range(2)]
            sink_add = jnp.zeros((WIN, LANE), F32)
            for j in range(A_HEADS // 2):
                kv = (2 * j) // (A_HEADS // 2)
                qp = q_ref[:, j * LANE:(j + 1) * LANE]
                dop = do_ref[:, j * LANE:(j + 1) * LANE]
                dq = None
                for half, msk in ((0, lo), (1, hi)):
                    h = 2 * j + half
                    probs, psink = _attn_probs(qp, kk[kv][half], b_ref[h], s_ref[0:1, h:h + 1], col_ok)
                    dprobs = _dot_nt(dop, vv[kv][half])
                    delta = jnp.sum(probs * dprobs, axis=-1, keepdims=True)
                    dlog = probs * (dprobs - delta)
                    db_ref[h] += dlog
                    sink_add = sink_add + jnp.where(lane == h, -psink * delta, 0.0)
                    dls = dlog * (A_DH ** -0.5)
                    dqh = _dot(dls, kk[kv][half])
                    dq = dqh if dq is None else dq + dqh
                    dk_acc[kv][half] = dk_acc[kv][half] + _dot_tn(dls, qp * msk)
                    dv_acc[kv][half] = dv_acc[kv][half] + _dot_tn(probs, dop * msk)
                dq_ref[:, j * LANE:(j + 1) * LANE] = dq.astype(_ACT)
            sacc_ref[...] += sink_add
            dkb = dk_acc[0][0] + pltpu.roll(dk_acc[0][1], 64, 1) + pltpu.roll(dk_acc[1][0], 64, 1) + dk_acc[1][1]
            dvb = dv_acc[0][0] + pltpu.roll(dv_acc[0][1], 64, 1) + pltpu.roll(dv_acc[1][0], 64, 1) + dv_acc[1][1]
            dk_ref[...] = (ck_ref[...] + dkb[0:WIN]).astype(_ACT)
            dv_ref[...] = (cv_ref[...] + dvb[0:WIN]).astype(_ACT)
            ck_ref[...] = dkb[WIN:]
            cv_ref[...] = dvb[WIN:]

        @pl.when(n == nb)
        def _():
            dk_ref[...] = ck_ref[...].astype(_ACT)
            dv_ref[...] = cv_ref[...].astype(_ACT)
            ds_ref[...] = jnp.zeros_like(ds_ref)
            ds_ref[0:1, :] = jnp.sum(sacc_ref[...], axis=0, keepdims=True)

    cur = lambda n: jnp.minimum(n, nb - 1)
    prv = lambda n: jnp.maximum(jnp.minimum(n, nb - 1) - 1, 0)
    kvspec = lambda col, prev: pl.BlockSpec(
        (WIN, LANE), (lambda n: (prv(n), col)) if prev else (lambda n: (cur(n), col)))
    return pl.pallas_call(
        body, name=name,
        out_shape=(jax.ShapeDtypeStruct((t, D_MODEL), _ACT), jax.ShapeDtypeStruct((t, LANE), _ACT),
                   jax.ShapeDtypeStruct((t, LANE), _ACT), jax.ShapeDtypeStruct((A_HEADS, WIN, 2 * WIN), F32),
                   jax.ShapeDtypeStruct((SUB, LANE), F32)),
        grid=(nb + 1,),
        in_specs=[pl.BlockSpec((WIN, 1024), lambda n: (cur(n), P_Q // 1024)),
                  kvspec(P_K // LANE, False), kvspec(P_K // LANE, True),
                  kvspec(P_V // LANE, False), kvspec(P_V // LANE, True),
                  pl.BlockSpec((A_HEADS, WIN, 2 * WIN), lambda n: (0, 0, 0)),
                  pl.BlockSpec((SUB, LANE), lambda n: (0, 0)),
                  pl.BlockSpec((WIN, 1024), lambda n: (cur(n), 0))],
        out_specs=(pl.BlockSpec((WIN, 1024), lambda n: (cur(n), 0)),
                   pl.BlockSpec((WIN, LANE), lambda n: (jnp.maximum(n - 1, 0), 0)),
                   pl.BlockSpec((WIN, LANE), lambda n: (jnp.maximum(n - 1, 0), 0)),
                   pl.BlockSpec((A_HEADS, WIN, 2 * WIN), lambda n: (0, 0, 0)),
                   pl.BlockSpec((SUB, LANE), lambda n: (0, 0))),
        scratch_shapes=[pltpu.VMEM((WIN, LANE), F32), pltpu.VMEM((WIN, LANE), F32), pltpu.VMEM((WIN, LANE), F32)],
        compiler_params=_cp(("arbitrary",)),
    )(proj, proj, proj, proj, proj, bias, sinks8, d_out)


def _merge_fwd(bs, ba, proj, bg8, name):
    t = bs.shape[0]
    tm = _tm_rows(t)

    def body(bs_ref, ba_ref, gs_ref, ga_ref, bgs_ref, bga_ref, o_ref):
        g_s = _sigmoid(gs_ref[...] + bgs_ref[0:1, :])
        g_a = _sigmoid(ga_ref[...] + bga_ref[0:1, :])
        o_ref[...] = (g_s * bs_ref[...] + g_a * ba_ref[...]).astype(_ACT)

    row = lambda col: pl.BlockSpec((tm, 1024), lambda i: (i, col))
    return pl.pallas_call(
        body, name=name, out_shape=jax.ShapeDtypeStruct((t, D_MODEL), _ACT), grid=(t // tm,),
        in_specs=[row(0), row(0), row(P_G // 1024), row(P_G // 1024 + 1),
                  pl.BlockSpec((SUB, 1024), lambda i: (0, 0)), pl.BlockSpec((SUB, 1024), lambda i: (0, 1))],
        out_specs=row(0), compiler_params=_cp(("parallel",)),
    )(bs, ba, proj, proj, bg8, bg8)


def _merge_bwd(d_merged, bs, ba, proj, bg8, name):
    t = bs.shape[0]
    tm = _tm_rows(t)

    def body(dm_ref, bs_ref, ba_ref, gs_ref, ga_ref, bgs_ref, bga_ref, dbs_ref, dba_ref, dg_ref, acc_ref):
        @pl.when(pl.program_id(0) == 0)
        def _():
            acc_ref[...] = jnp.zeros_like(acc_ref)

        dm = dm_ref[...]
        g_s = _sigmoid(gs_ref[...] + bgs_ref[0:1, :])
        g_a = _sigmoid(ga_ref[...] + bga_ref[0:1, :])
        dbs_ref[...] = (dm * g_s).astype(_ACT)
        dba_ref[...] = (dm * g_a).astype(_ACT)
        dgs = dm * bs_ref[...] * g_s * (1.0 - g_s)
        dga = dm * ba_ref[...] * g_a * (1.0 - g_a)
        dg_ref[:, 0:1024] = dgs.astype(_ACT)
        dg_ref[:, 1024:2048] = dga.astype(_ACT)
        acc_ref[0:1, 0:1024] += jnp.sum(dgs, axis=0, keepdims=True)
        acc_ref[0:1, 1024:2048] += jnp.sum(dga, axis=0, keepdims=True)

    row = lambda col: pl.BlockSpec((tm, 1024), lambda i: (i, col))
    return pl.pallas_call(
        body, name=name,
        out_shape=(jax.ShapeDtypeStruct((t, D_MODEL), _ACT), jax.ShapeDtypeStruct((t, D_MODEL), _ACT),
                   jax.ShapeDtypeStruct((t, 2048), _ACT), jax.ShapeDtypeStruct((SUB, 2048), F32)),
        grid=(t // tm,),
        in_specs=[row(0), row(0), row(0), row(P_G // 1024), row(P_G // 1024 + 1),
                  pl.BlockSpec((SUB, 1024), lambda i: (0, 0)), pl.BlockSpec((SUB, 1024), lambda i: (0, 1))],
        out_specs=(row(0), row(0), pl.BlockSpec((tm, 2048), lambda i: (i, 0)),
                   pl.BlockSpec((SUB, 2048), lambda i: (0, 0))),
        compiler_params=_cp(("arbitrary",)),
    )(d_merged, bs, ba, proj, proj, bg8, bg8)


def _ln_stats(r):
    mu = jnp.mean(r, axis=-1, keepdims=True)
    xc = r - mu
    var = jnp.mean(xc * xc, axis=-1, keepdims=True)
    rstd = lax.rsqrt(var + LN_EPS)
    return xc * rstd, rstd


def _ln_bwd(dxhat, xhat, rstd):
    return rstd * (dxhat - jnp.mean(dxhat, axis=-1, keepdims=True)
                   - xhat * jnp.mean(dxhat * xhat, axis=-1, keepdims=True))


def _ln1_fwd(x, mix, g8, b8, name):
    t = x.shape[0]
    tm = _tm_rows(t)

    def body(x_ref, m_ref, g_ref, b_ref, xh_ref, h_ref, rs_ref):
        xhat, rstd = _ln_stats(ALPHA * x_ref[...] + m_ref[...])
        xh_ref[...] = xhat
        h_ref[...] = (xhat * g_ref[0:1, :] + b_ref[0:1, :]).astype(_ACT)
        rs_ref[...] = rstd

    row = pl.BlockSpec((tm, D_MODEL), lambda i: (i, 0))
    par = pl.BlockSpec((SUB, D_MODEL), lambda i: (0, 0))
    return pl.pallas_call(
        body, name=name,
        out_shape=(jax.ShapeDtypeStruct((t, D_MODEL), F32), jax.ShapeDtypeStruct((t, D_MODEL), _ACT),
                   jax.ShapeDtypeStruct((t, 1), F32)),
        grid=(t // tm,), in_specs=[row, row, par, par],
        out_specs=(row, row, pl.BlockSpec((tm, 1), lambda i: (i, 0))),
        compiler_params=_cp(("parallel",)),
    )(x, mix, g8, b8)


def _ln2_loss(xhat1, ffn, target, g1_8, b1_8, g2_8, b2_8, name):
    t = xhat1.shape[0]
    tm = _tm_rows(t)

    def body(xh_ref, f_ref, t_ref, g1_ref, b1_ref, g2_ref, b2_ref, d_ref, db_ref, acc_ref):
        @pl.when(pl.program_id(0) == 0)
        def _():
            acc_ref[...] = jnp.zeros_like(acc_ref)

        h1 = xh_ref[...] * g1_ref[0:1, :] + b1_ref[0:1, :]
        xhat, rstd = _ln_stats(ALPHA * h1 + f_ref[...])
        diff = xhat * g2_ref[0:1, :] + b2_ref[0:1, :] - t_ref[...]
        dy = diff * (1.0 / D_MODEL)
        acc_ref[0:1, :] += jnp.sum(dy * xhat, axis=0, keepdims=True)
        acc_ref[1:2, :] += jnp.sum(dy, axis=0, keepdims=True)
        acc_ref[2:3, :] += jnp.sum(diff * diff, axis=0, keepdims=True)
        d = _ln_bwd(dy * g2_ref[0:1, :], xhat, rstd)
        d_ref[...] = d
        db_ref[...] = d.astype(_ACT)

    row = pl.BlockSpec((tm, D_MODEL), lambda i: (i, 0))
    par = pl.BlockSpec((SUB, D_MODEL), lambda i: (0, 0))
    return pl.pallas_call(
        body, name=name,
        out_shape=(jax.ShapeDtypeStruct((t, D_MODEL), F32), jax.ShapeDtypeStruct((t, D_MODEL), _ACT),
                   jax.ShapeDtypeStruct((SUB, D_MODEL), F32)),
        grid=(t // tm,), in_specs=[row, row, row, par, par, par, par],
        out_specs=(row, row, par), compiler_params=_cp(("arbitrary",)),
    )(xhat1, ffn, target, g1_8, b1_8, g2_8, b2_8)


def _ln1_bwd(d_r2, d_h1_ffn, xhat1, rstd1, g1_8, name):
    t = xhat1.shape[0]
    tm = _tm_rows(t)

    def body(d2_ref, df_ref, xh_ref, rs_ref, g_ref, d_ref, db_ref, acc_ref):
        @pl.when(pl.program_id(0) == 0)
        def _():
            acc_ref[...] = jnp.zeros_like(acc_ref)

        dh = ALPHA * d2_ref[...] + df_ref[...]
        xhat = xh_ref[...]
        acc_ref[0:1, :] += jnp.sum(dh * xhat, axis=0, keepdims=True)
        acc_ref[1:2, :] += jnp.sum(dh, axis=0, keepdims=True)
        d = _ln_bwd(dh * g_ref[0:1, :], xhat, rs_ref[...])
        d_ref[...] = d
        db_ref[...] = d.astype(_ACT)

    row = pl.BlockSpec((tm, D_MODEL), lambda i: (i, 0))
    par = pl.BlockSpec((SUB, D_MODEL), lambda i: (0, 0))
    return pl.pallas_call(
        body, name=name,
        out_shape=(jax.ShapeDtypeStruct((t, D_MODEL), F32), jax.ShapeDtypeStruct((t, D_MODEL), _ACT),
                   jax.ShapeDtypeStruct((SUB, D_MODEL), F32)),
        grid=(t // tm,), in_specs=[row, row, row, pl.BlockSpec((tm, 1), lambda i: (i, 0)), par],
        out_specs=(row, row, par), compiler_params=_cp(("arbitrary",)),
    )(d_r2, d_h1_ffn, xhat1, rstd1, g1_8)


def _ffn_tm(t):
    return min(128, t)


def _ffn_act_fwd(u0, cw8, cb8, name):
    t = u0.shape[0]
    tm = _ffn_tm(t)

    def body(g_ref, gp_ref, v_ref, vp_ref, wg_ref, wv_ref, bg_ref, bv_ref, o_ref):
        i = pl.program_id(0)
        gate = _conv_pre(g_ref[...], jnp.where(i > 0, gp_ref[...], 0.0), wg_ref, bg_ref[0:1, :], FFN_K)
        val = _conv_pre(v_ref[...], jnp.where(i > 0, vp_ref[...], 0.0), wv_ref, bv_ref[0:1, :], FFN_K)
        o_ref[...] = (gate * _sigmoid(gate) * val).astype(_ACT)

    cur = lambda col: pl.BlockSpec((tm, D_FF), lambda i: (i, col))
    prv = lambda col: pl.BlockSpec((SUB, D_FF), lambda i: (jnp.maximum(i * (tm // SUB) - 1, 0), col))
    par = lambda col: pl.BlockSpec((SUB, D_FF), lambda i: (0, col))
    return pl.pallas_call(
        body, name=name, out_shape=jax.ShapeDtypeStruct((t, D_FF), _ACT), grid=(t // tm,),
        in_specs=[cur(0), prv(0), cur(1), prv(1), par(0), par(1), par(0), par(1)],
        out_specs=pl.BlockSpec((tm, D_FF), lambda i: (i, 0)), compiler_params=_cp(("parallel",)),
    )(u0, u0, u0, u0, cw8, cw8, cb8, cb8)


def _ffn_act_bwd(u0, cw8, cb8, d_a, name):
    t = u0.shape[0]
    tm = _ffn_tm(t)

    def body(g_ref, gp_ref, v_ref, vp_ref, wg_ref, wv_ref, bg_ref, bv_ref, da_ref, du_ref, acc_ref):
        i = pl.program_id(0)

        @pl.when(i == 0)
        def _():
            acc_ref[...] = jnp.zeros_like(acc_ref)

        gcur, vcur = g_ref[...], v_ref[...]
        gprev = jnp.where(i > 0, gp_ref[...], 0.0)
        vprev = jnp.where(i > 0, vp_ref[...], 0.0)
        gate = _conv_pre(gcur, gprev, wg_ref, bg_ref[0:1, :], FFN_K)
        val = _conv_pre(vcur, vprev, wv_ref, bv_ref[0:1, :], FFN_K)
        sg = _sigmoid(gate)
        da = da_ref[...]
        dgate = da * val * (sg * (1.0 + gate * (1.0 - sg)))
        dval = da * gate * sg
        du_ref[:, 0:D_FF] = dgate
        du_ref[:, D_FF:2 * D_FF] = dval
        for d, cur, prev, off in ((dgate, gcur, gprev, 0), (dval, vcur, vprev, D_FF)):
            acc_ref[FFN_K:FFN_K + 1, off:off + D_FF] += jnp.sum(d, axis=0, keepdims=True)
            acc_ref[FFN_K - 1:FFN_K, off:off + D_FF] += jnp.sum(d * cur, axis=0, keepdims=True)
            for s in range(1, FFN_K):
                acc_ref[FFN_K - 1 - s:FFN_K - s, off:off + D_FF] += jnp.sum(
                    d * _shift_down(cur, prev, s), axis=0, keepdims=True)

    cur = lambda col: pl.BlockSpec((tm, D_FF), lambda i: (i, col))
    prv = lambda col: pl.BlockSpec((SUB, D_FF), lambda i: (jnp.maximum(i * (tm // SUB) - 1, 0), col))
    par = lambda col: pl.BlockSpec((SUB, D_FF), lambda i: (0, col))
    return pl.pallas_call(
        body, name=name,
        out_shape=(jax.ShapeDtypeStruct((t, 2 * D_FF), F32), jax.ShapeDtypeStruct((SUB, 2 * D_FF), F32)),
        grid=(t // tm,),
        in_specs=[cur(0), prv(0), cur(1), prv(1), par(0), par(1), par(0), par(1),
                  pl.BlockSpec((tm, D_FF), lambda i: (i, 0))],
        out_specs=(pl.BlockSpec((tm, 2 * D_FF), lambda i: (i, 0)),
                   pl.BlockSpec((SUB, 2 * D_FF), lambda i: (0, 0))),
        compiler_params=_cp(("arbitrary",)),
    )(u0, u0, u0, u0, cw8, cw8, cb8, cb8, d_a)


def _pack_w_in(w_in):
    order = sorted(_PIECES, key=lambda p: p[2])
    cols = []
    at = 0
    for o, w, pk in order:
        if pk > at:
            cols.append(jnp.zeros((w_in.shape[0], pk - at), w_in.dtype))
        cols.append(w_in[:, o:o + w])
        at = pk + w
    if at < P_W:
        cols.append(jnp.zeros((w_in.shape[0], P_W - at), w_in.dtype))
    return jnp.concatenate(cols, axis=1)


def _unpack_w_in(wp):
    return jnp.concatenate([wp[:, pk:pk + w] for o, w, pk in sorted(_PIECES)], axis=1)


def _local_step(x, target, wts):
    t = x.shape[0]
    wp = _pack_w_in(wts["w_in"])
    w_bs, w_ba, w_mix, w_up, w_dn = (wts[k] for k in ("w_branch_ssm", "w_branch_attn", "w_mix_out", "w_up", "w_down"))
    scw = wts["ssm_conv_w"]
    scb = wts["ssm_conv_b"]
    fcw8 = _rows8(wts["ffn_conv_w"])
    fcb8 = _rows8(wts["ffn_conv_b"])
    pad_lane = lambda p: jnp.concatenate([p.astype(F32), jnp.zeros((1, LANE - p.shape[1]), F32)], axis=1)
    dtb8 = _rows8(pad_lane(wts["ssm_dt_bias"]))
    alog8 = _rows8(pad_lane(wts["ssm_a_log"]))
    dsk8 = _rows8(pad_lane(wts["ssm_d"]))
    sinks8 = _rows8(pad_lane(wts["attn_sinks"]))
    nw8 = _rows8(wts["ssm_norm_w"])
    bg8 = _rows8(wts["b_gate"])
    g1_8, b1_8, g2_8, b2_8 = (_rows8(wts[k]) for k in ("ln1_g", "ln1_b", "ln2_g", "ln2_b"))
    xs_w8, xs_b8 = _rows8(scw[:, :D_INNER]), _rows8(scb[:, :D_INNER])
    bc_w8, bc_b8 = _rows8(scw[:, D_INNER:]), _rows8(scb[:, D_INNER:])

    x_bf = x.astype(_ACT)
    proj = _mm(x_bf, wp, "mm_in")
    xs_c = _conv_silu_fwd(proj, P_XS // _TC, D_INNER // _TC, xs_w8, xs_b8, "conv_xs_fwd")
    bc_c = _conv_silu_fwd(proj, P_BC // _TC, 1024 // _TC, bc_w8, bc_b8, "conv_bc_fwd")
    y_ssd, y_ssm, hprev = _ssd_fwd(xs_c, bc_c, proj, dtb8, alog8, dsk8, nw8, "ssd_fwd")
    bias = _bias_expand(wts["rel_bias"].T.astype(F32), "bias_expand").reshape(A_HEADS, WIN, 2 * WIN)
    y_attn = _attn_fwd(proj, bias, sinks8, "attn_fwd")
    bs = _mm(y_ssm, w_bs, "mm_bs")
    ba = _mm(y_attn, w_ba, "mm_ba")
    merged = _merge_fwd(bs, ba, proj, bg8, "merge_fwd")
    mix = _mm(merged, w_mix, "mm_mix")
    xhat1, h1_bf, rstd1 = _ln1_fwd(x, mix, g1_8, b1_8, "ln1_fwd")
    u0 = _mm(h1_bf, w_up, "mm_up")
    act = _ffn_act_fwd(u0, fcw8, fcb8, "ffn_act_fwd")
    ffn = _mm(act, w_dn, "mm_down")
    d_r2, d_r2_bf, acc_ln2 = _ln2_loss(xhat1, ffn, target, g1_8, b1_8, g2_8, b2_8, "ln2_loss")
    d_w_dn = _mm(act, d_r2_bf, "mm_dw_down", trans_a=True)
    d_act = _mm(d_r2_bf, w_dn.T, "mm_d_act")
    d_u, acc_ffn = _ffn_act_bwd(u0, fcw8, fcb8, d_act, "ffn_act_bwd")
    d_u0 = _conv_transpose(d_u, fcw8, FFN_K, "ffn_conv_t")
    d_w_up = _mm(h1_bf, d_u0, "mm_dw_up", trans_a=True)
    d_h1_ffn = _mm(d_u0, w_up.T, "mm_d_h1")
    d_r1, d_r1_bf, acc_ln1 = _ln1_bwd(d_r2, d_h1_ffn, xhat1, rstd1, g1_8, "ln1_bwd")
    d_w_mix = _mm(merged, d_r1_bf, "mm_dw_mix", trans_a=True)
    d_merged = _mm(d_r1_bf, w_mix.T, "mm_d_merged")
    d_bs, d_ba, d_gates, acc_bg = _merge_bwd(d_merged, bs, ba, proj, bg8, "merge_bwd")
    d_w_bs = _mm(y_ssm, d_bs, "mm_dw_bs", trans_a=True)
    d_w_ba = _mm(y_attn, d_ba, "mm_dw_ba", trans_a=True)
    d_y_ssm = _mm(d_bs, w_bs.T, "mm_d_yssm")
    d_y_attn = _mm(d_ba, w_ba.T, "mm_d_yattn")
    d_q, d_k, d_v, d_bias, d_sinks = _attn_bwd(proj, bias, sinks8, d_y_attn, "attn_bwd")
    d_rel_t = _bias_reduce(d_bias.reshape(A_HEADS, WIN * 2 * WIN), "bias_reduce")
    d_xs_c, d_bc_c, d_z, d_dt, acc_ssd, acc_nw = _ssd_bwd(
        d_y_ssm, y_ssd, xs_c, bc_c, proj, hprev, dtb8, alog8, dsk8, nw8, "ssd_bwd")
    d_xs_pre, acc_xs = _conv_silu_bwd(proj, P_XS // _TC, D_INNER // _TC, xs_w8, xs_b8, d_xs_c, "conv_xs_bwd")
    d_bc_pre, acc_bc = _conv_silu_bwd(proj, P_BC // _TC, 1024 // _TC, bc_w8, bc_b8, d_bc_c, "conv_bc_bwd")
    d_xs = _conv_transpose(d_xs_pre, xs_w8, SSM_K, "conv_xs_t")
    d_bc = _conv_transpose(d_bc_pre, bc_w8, SSM_K, "conv_bc_t")
    d_proj = jnp.concatenate([d_z, d_xs, d_gates, d_q, d_bc, d_k, d_v, d_dt,
                              jnp.zeros((t, P_W - P_DT - LANE), _ACT)], axis=1)
    d_wp = _mm(x_bf, d_proj, "mm_dw_in", trans_a=True)
    d_x = _mm(d_proj, wp.T, "mm_d_x", res=d_r1, res_scale=ALPHA)

    grads = {
        "w_in": _unpack_w_in(d_wp),
        "ssm_conv_w": jnp.concatenate([acc_xs[0:SSM_K], acc_bc[0:SSM_K]], axis=1),
        "w_branch_ssm": d_w_bs, "w_branch_attn": d_w_ba, "w_mix_out": d_w_mix,
        "w_up": d_w_up, "ffn_conv_w": acc_ffn[0:FFN_K], "w_down": d_w_dn,
    }
    small = {
        "rel_bias": d_rel_t.T,
        "b_gate": acc_bg[0:1],
        "ssm_conv_b": jnp.concatenate([acc_xs[SSM_K:SSM_K + 1], acc_bc[SSM_K:SSM_K + 1]], axis=1),
        "ssm_dt_bias": acc_ssd[0:1, 0:N_HEADS], "ssm_a_log": acc_ssd[1:2, 0:N_HEADS], "ssm_d": acc_ssd[2:3, 0:N_HEADS],
        "ssm_norm_w": acc_nw[0:1],
        "attn_sinks": d_sinks[0:1, 0:A_HEADS],
        "ln1_g": acc_ln1[0:1], "ln1_b": acc_ln1[1:2],
        "ffn_conv_b": acc_ffn[FFN_K:FFN_K + 1],
        "ln2_g": acc_ln2[0:1], "ln2_b": acc_ln2[1:2],
        "loss_lanes": acc_ln2[2:3],
    }
    return d_x, grads, small


_BIG = (("w_in", (1024, 2120), 1), ("ssm_conv_w", (4, 768), 1), ("w_branch_ssm", (512, 1024), 0),
        ("w_branch_attn", (256, 1024), 0), ("w_mix_out", (256, 1024), 0), ("w_up", (1024, 1408), 1),
        ("ffn_conv_w", (3, 1408), 1), ("w_down", (704, 1024), 0))
_ROW_ALIGN = 16
_BLK_ROWS = 1024


def _piece_rows(shape):
    rows = -(-(shape[0] * shape[1]) // LANE)
    return -(-rows // _ROW_ALIGN) * _ROW_ALIGN


_BIG_ROWS = -(-sum(_piece_rows(s) for _, s, _ in _BIG) // (2 * _BLK_ROWS)) * (2 * _BLK_ROWS)
_HALF_ROWS = _BIG_ROWS // 2
_F32_CONV = (("ssm_conv_w", (4, 768)), ("ffn_conv_w", (3, 1408)))
_AG_ROWS = _BIG_ROWS + 2 * _BLK_ROWS
_AG_HALF = _AG_ROWS // 2

_SMALL = (("rel_bias", (32, 16)), ("b_gate", (1, 2048)), ("ssm_conv_b", (1, 3072)), ("ssm_dt_bias", (1, 32)),
          ("ssm_a_log", (1, 32)), ("ssm_d", (1, 32)), ("ssm_norm_w", (1, 2048)), ("attn_sinks", (1, 16)),
          ("ln1_g", (1, 1024)), ("ln1_b", (1, 1024)), ("ffn_conv_b", (1, 5632)), ("ln2_g", (1, 1024)),
          ("ln2_b", (1, 1024)))
_LOSS_ROWS = SUB


def _small_rows(shape):
    rows = -(-(shape[0] * shape[1]) // LANE)
    return -(-rows // SUB) * SUB


_SMALL_ROWS = sum(_small_rows(s) for _, s in _SMALL) + _LOSS_ROWS


def _as_rows(a, rows, dtype):
    flat = a.reshape(-1).astype(dtype)
    flat = jnp.concatenate([flat, jnp.zeros((rows * LANE - flat.shape[0],), dtype)])
    return flat.reshape(rows, LANE)


def _pack_big(parts, dtype):
    blocks = [_as_rows(parts[n], _piece_rows(s), dtype) for n, s, _ in _BIG]
    used = sum(b.shape[0] for b in blocks)
    blocks.append(jnp.zeros((_BIG_ROWS - used, LANE), dtype))
    return jnp.concatenate(blocks, axis=0)


def _unpack_big(packed):
    out, at = {}, 0
    for n, s, _ in _BIG:
        rows = _piece_rows(s)
        out[n] = packed[at:at + rows].reshape(-1)[:s[0] * s[1]].reshape(s)
        at += rows
    return out


def _pack_small(parts, extra):
    blocks = [_as_rows(parts[n], _small_rows(s), F32) for n, s in _SMALL]
    blocks.append(_as_rows(extra, _LOSS_ROWS, F32))
    return jnp.concatenate(blocks, axis=0)


def _unpack_small(packed):
    out, at = {}, 0
    for n, s in _SMALL:
        rows = _small_rows(s)
        out[n] = packed[at:at + rows].reshape(-1)[:s[0] * s[1]].reshape(s)
        at += rows
    return out, packed[at:at + _LOSS_ROWS]


def _shard_of(full, shape, axis, j):
    return lax.slice_in_dim(full, j * shape[axis], (j + 1) * shape[axis], axis=axis)


_MESH = pl.DeviceIdType.MESH
_HBM = pl.BlockSpec(memory_space=pltpu.HBM)


def _position():
    return lax.axis_index("x"), lax.axis_index("y"), lax.axis_index("c")


def _other_chips(x, y):
    return ((1 - x, y), (x, 1 - y), (1 - x, 1 - y))


def _allgather_weights(shard):
    _, hr, _ = shard.shape

    def body(s_ref, o_ref, send_sems, recv_sems, local_sem):
        x, y, c = _position()
        me = 2 * x + y
        sib = (x, y, 1 - c)
        chips = _other_chips(x, y)
        mine = pltpu.make_async_copy(s_ref, o_ref.at[me], local_sem)
        mine.start()

        def copy(k, chip_idx, half, to, src=None):
            dst = o_ref.at[chip_idx, half]
            return pltpu.make_async_remote_copy(src_ref=dst if src is None else src, dst_ref=dst,
                                                send_sem=send_sems.at[k], recv_sem=recv_sems.at[k],
                                                device_id=to, device_id_type=_MESH)

        first = [copy(i, me, c, (cx, cy, c), src=s_ref.at[c]) for i, (cx, cy) in enumerate(chips)]
        for cp in first:
            cp.start()
        passed = [copy(3 + i, 2 * cx + cy, c, sib) for i, (cx, cy) in enumerate(chips)]
        for i, (cx, cy) in enumerate(chips):
            copy(i, 2 * cx + cy, c, sib).wait_recv()
            passed[i].start()
        for i, (cx, cy) in enumerate(chips):
            copy(3 + i, 2 * cx + cy, 1 - c, sib).wait_recv()
        for cp in first + passed:
            cp.wait_send()
        mine.wait()

    return pl.pallas_call(
        body, name="allgather_weights",
        out_shape=jax.ShapeDtypeStruct((N_CHIPS, 2, hr, LANE), shard.dtype),
        in_specs=[_HBM], out_specs=_HBM,
        scratch_shapes=[pltpu.SemaphoreType.DMA((6,)), pltpu.SemaphoreType.DMA((6,)), pltpu.SemaphoreType.DMA],
    )(shard)


def _swap_halves(g):
    nseg, _, hr, _ = g.shape

    def body(g_ref, o_ref, send_sems, recv_sems):
        x, y, c = _position()
        cps = [pltpu.make_async_remote_copy(src_ref=g_ref.at[j, 1 - c], dst_ref=o_ref.at[j],
                                            send_sem=send_sems.at[j], recv_sem=recv_sems.at[j],
                                            device_id=(x, y, 1 - c), device_id_type=_MESH) for j in range(nseg)]
        for cp in cps:
            cp.start()
        for cp in cps:
            cp.wait()

    return pl.pallas_call(
        body, name="swap_halves", out_shape=jax.ShapeDtypeStruct((nseg, hr, LANE), g.dtype),
        in_specs=[_HBM], out_specs=_HBM,
        scratch_shapes=[pltpu.SemaphoreType.DMA((nseg,)), pltpu.SemaphoreType.DMA((nseg,))],
    )(g)


def _scatter_chips(p):
    _, hr, _ = p.shape

    def body(p_ref, o_ref, send_sems, recv_sems):
        x, y, c = _position()
        cps = [pltpu.make_async_remote_copy(src_ref=p_ref.at[2 * cx + cy], dst_ref=o_ref.at[i],
                                            send_sem=send_sems.at[i], recv_sem=recv_sems.at[i],
                                            device_id=(cx, cy, c), device_id_type=_MESH)
               for i, (cx, cy) in enumerate(_other_chips(x, y))]
        for cp in cps:
            cp.start()
        for cp in cps:
            cp.wait()

    return pl.pallas_call(
        body, name="scatter_chips", out_shape=jax.ShapeDtypeStruct((N_CHIPS - 1, hr, LANE), p.dtype),
        in_specs=[_HBM], out_specs=_HBM,
        scratch_shapes=[pltpu.SemaphoreType.DMA((N_CHIPS - 1,)), pltpu.SemaphoreType.DMA((N_CHIPS - 1,))],
    )(p)


def _join_halves(red):
    hr, _ = red.shape

    def body(r_ref, o_ref, send_sem, recv_sem, local_sem):
        x, y, c = _position()
        mine = pltpu.make_async_copy(r_ref, o_ref.at[c], local_sem)
        mine.start()
        cp = pltpu.make_async_remote_copy(src_ref=r_ref, dst_ref=o_ref.at[c], send_sem=send_sem, recv_sem=recv_sem,
                                          device_id=(x, y, 1 - c), device_id_type=_MESH)
        cp.start()
        cp.wait()
        mine.wait()

    return pl.pallas_call(
        body, name="join_halves", out_shape=jax.ShapeDtypeStruct((2, hr, LANE), red.dtype),
        in_specs=[_HBM], out_specs=_HBM,
        scratch_shapes=[pltpu.SemaphoreType.DMA, pltpu.SemaphoreType.DMA, pltpu.SemaphoreType.DMA],
    )(red)


def _allgather_small(mine):
    m_per, n = mine.shape

    def body(x_ref, out_ref, send_sems, recv_sems, local_sem):
        x, y, c = _position()
        me, sibling = (x, y, c), (x, y, 1 - c)
        chips = _other_chips(x, y)

        def rows(px, py, pc):
            return out_ref.at[pl.ds((4 * px + 2 * py + pc) * m_per, m_per), :]

        def copy(k, block, to, src=None):
            return pltpu.make_async_remote_copy(src_ref=rows(*block) if src is None else src, dst_ref=rows(*block),
                                                send_sem=send_sems.at[k], recv_sem=recv_sems.at[k],
                                                device_id=to, device_id_type=_MESH)

        own = pltpu.make_async_copy(x_ref, rows(*me), local_sem)
        own.start()
        first = [copy(0, me, sibling, src=x_ref)]
        first += [copy(1 + j, me, (*chip, c), src=x_ref) for j, chip in enumerate(chips)]
        for cp in first:
            cp.start()
        passed = [copy(4 + j, (*chip, c), sibling) for j, chip in enumerate(chips)]
        for j, chip in enumerate(chips):
            copy(1 + j, (*chip, c), me).wait_recv()
            passed[j].start()
        copy(0, sibling, me).wait_recv()
        for j, chip in enumerate(chips):
            copy(4 + j, (*chip, 1 - c), me).wait_recv()
        for cp in first + passed:
            cp.wait_send()
        own.wait()

    return pl.pallas_call(
        body, name="allgather_small", out_shape=jax.ShapeDtypeStruct((N_DEV * m_per, n), mine.dtype),
        in_specs=[pl.BlockSpec(memory_space=pltpu.VMEM)], out_specs=pl.BlockSpec(memory_space=pltpu.VMEM),
        scratch_shapes=[pltpu.SemaphoreType.DMA((7,)), pltpu.SemaphoreType.DMA((7,)), pltpu.SemaphoreType.DMA],
    )(mine)


def _add_own_half(g, recv, c_idx):
    nseg, _, hr, _ = g.shape

    def body(c_ref, g_ref, r_ref, o_ref):
        o_ref[...] = g_ref[...] + r_ref[...]

    return pl.pallas_call(
        body, name="add_own_half", out_shape=jax.ShapeDtypeStruct((nseg, hr, LANE), F32),
        grid_spec=pltpu.PrefetchScalarGridSpec(
            num_scalar_prefetch=1, grid=(nseg, hr // _BLK_ROWS),
            in_specs=[pl.BlockSpec((None, None, _BLK_ROWS, LANE), lambda j, i, c_ref: (j, c_ref[0], i, 0)),
                      pl.BlockSpec((None, _BLK_ROWS, LANE), lambda j, i, c_ref: (j, i, 0))],
            out_specs=pl.BlockSpec((None, _BLK_ROWS, LANE), lambda j, i, c_ref: (j, i, 0))),
        compiler_params=_cp(("parallel", "parallel")),
    )(c_idx, g, recv)


def _add_chips(p, recv, chip_idx):
    _, hr, _ = p.shape

    def body(j_ref, p_ref, r_ref, o_ref):
        o_ref[...] = ((p_ref[...] + r_ref[0]) + r_ref[1]) + r_ref[2]

    return pl.pallas_call(
        body, name="add_chips", out_shape=jax.ShapeDtypeStruct((hr, LANE), F32),
        grid_spec=pltpu.PrefetchScalarGridSpec(
            num_scalar_prefetch=1, grid=(hr // _BLK_ROWS,),
            in_specs=[pl.BlockSpec((None, _BLK_ROWS, LANE), lambda i, j_ref: (j_ref[0], i, 0)),
                      pl.BlockSpec((N_CHIPS - 1, _BLK_ROWS, LANE), lambda i, j_ref: (0, i, 0))],
            out_specs=pl.BlockSpec((_BLK_ROWS, LANE), lambda i, j_ref: (i, 0))),
        compiler_params=_cp(("parallel",)),
    )(chip_idx, p, recv)


def _adam_math(w, g, m, v):
    m = ADAM_B1 * m + (1.0 - ADAM_B1) * g
    v = ADAM_B2 * v + (1.0 - ADAM_B2) * (g * g)
    m_hat = m / (1.0 - ADAM_B1 ** ADAM_STEP)
    v_hat = v / (1.0 - ADAM_B2 ** ADAM_STEP)
    delta = -ADAM_LR * (m_hat / (jnp.sqrt(v_hat) + ADAM_EPS) + ADAM_WD * w)
    return delta, m, v


def _adam_big(w, g, m, v):
    rows = w.shape[0]

    def body(w_ref, g_ref, m_ref, v_ref, d_ref, mo_ref, vo_ref):
        d_ref[...], mo_ref[...], vo_ref[...] = _adam_math(w_ref[...], g_ref[...], m_ref[...], v_ref[...])

    blk = pl.BlockSpec((_BLK_ROWS, LANE), lambda i: (i, 0))
    shp = jax.ShapeDtypeStruct((rows, LANE), F32)
    return pl.pallas_call(
        body, name="adam_big", out_shape=(shp, shp, shp), grid=(rows // _BLK_ROWS,),
        in_specs=[blk, blk, blk, blk], out_specs=(blk, blk, blk), compiler_params=_cp(("parallel",)),
    )(w, g, m, v)


def _adam_small(w, gathered, m, v):
    rows = w.shape[0]

    def body(w_ref, a_ref, m_ref, v_ref, g_ref, d_ref, mo_ref, vo_ref):
        g = a_ref[0:rows, :]
        for k in range(1, N_DEV):
            g = g + a_ref[k * rows:(k + 1) * rows, :]
        g_ref[...] = g
        d_ref[...], mo_ref[...], vo_ref[...] = _adam_math(w_ref[...], g, m_ref[...], v_ref[...])

    shp = jax.ShapeDtypeStruct((rows, LANE), F32)
    return pl.pallas_call(body, name="adam_small", out_shape=(shp, shp, shp, shp), compiler_params=_cp(None))(
        w, gathered, m, v)


_WEIGHTS = ("rel_bias", "w_in", "b_gate", "ssm_conv_w", "ssm_conv_b", "ssm_dt_bias", "ssm_a_log", "ssm_d",
            "ssm_norm_w", "attn_sinks", "w_branch_ssm", "w_branch_attn", "w_mix_out", "ln1_g", "ln1_b", "w_up",
            "ffn_conv_w", "ffn_conv_b", "w_down", "ln2_g", "ln2_b")
_BIG_NAMES = tuple(n for n, _, _ in _BIG)


def _step(x, target, w, m, v):
    xi, yi, ci = _position()
    chip = 2 * xi + yi

    f32_rows = jnp.concatenate(
        [_as_rows(lax.bitcast_convert_type(w[n].astype(F32), jnp.bfloat16), 2 * _piece_rows(s), jnp.bfloat16)
         for n, s in _F32_CONV], axis=0)
    f32_rows = jnp.concatenate(
        [f32_rows, jnp.zeros((_AG_ROWS - _BIG_ROWS - f32_rows.shape[0], LANE), jnp.bfloat16)], axis=0)
    shard = jnp.concatenate([_pack_big(w, jnp.bfloat16), f32_rows], axis=0).reshape(2, _AG_HALF, LANE)
    gathered = _allgather_weights(shard).reshape(N_CHIPS, _AG_ROWS, LANE)
    per_chip = [_unpack_big(gathered[j, :_BIG_ROWS]) for j in range(N_CHIPS)]
    full = {n: jnp.concatenate([per_chip[j][n] for j in range(N_CHIPS)], axis=ax) for n, _, ax in _BIG}
    at = _BIG_ROWS
    for n, s in _F32_CONV:
        rows = 2 * _piece_rows(s)
        pieces = [lax.bitcast_convert_type(gathered[j, at:at + rows].reshape(-1, 2), F32)[:s[0] * s[1]].reshape(s)
                  for j in range(N_CHIPS)]
        full[n] = jnp.concatenate(pieces, axis=1)
        at += rows
    wts = {n: (full[n] if n in full else w[n]) for n in _WEIGHTS}

    d_x, grads, small = _local_step(x, target, wts)

    g = jnp.stack([_pack_big({n: _shard_of(grads[n], s, ax, j) for n, s, ax in _BIG}, F32)
                   for j in range(N_CHIPS)]).reshape(N_CHIPS, 2, _HALF_ROWS, LANE)
    c_idx = jnp.reshape(ci, (1,)).astype(jnp.int32)
    chip_sum = _add_own_half(g, _swap_halves(g), c_idx)
    red = _add_chips(chip_sum, _scatter_chips(chip_sum), jnp.reshape(chip, (1,)).astype(jnp.int32))
    g_big = _join_halves(red).reshape(_BIG_ROWS, LANE)
    d_big, m_big, v_big = _adam_big(_pack_big(w, F32), g_big, _pack_big(m, F32), _pack_big(v, F32))
    outs = {"grad": _unpack_big(g_big), "delta": _unpack_big(d_big), "m": _unpack_big(m_big), "v": _unpack_big(v_big)}

    all_small = _allgather_small(_pack_small(small, small["loss_lanes"]))
    g_s, d_s, m_s, v_s = _adam_small(_pack_small(w, jnp.zeros((1, LANE), F32)), all_small,
                                     _pack_small(m, jnp.zeros((1, LANE), F32)),
                                     _pack_small(v, jnp.zeros((1, LANE), F32)))
    (gs, loss_rows), (ds, _), (ms, _), (vs, _) = (_unpack_small(a) for a in (g_s, d_s, m_s, v_s))
    for kind, part in (("grad", gs), ("delta", ds), ("m", ms), ("v", vs)):
        outs[kind].update(part)
    loss = (0.5 / D_MODEL) * jnp.sum(loss_rows)
    return loss, d_x, outs


def kernel(x, rel_bias, w_in, b_gate, ssm_conv_w, ssm_conv_b, ssm_dt_bias, ssm_a_log, ssm_d, ssm_norm_w, attn_sinks, w_branch_ssm, w_branch_attn, w_mix_out, ln1_g, ln1_b, w_up, ffn_conv_w, ffn_conv_b, w_down, ln2_g, ln2_b, loss_target, m_rel_bias, m_w_in, m_b_gate, m_ssm_conv_w, m_ssm_conv_b, m_ssm_dt_bias, m_ssm_a_log, m_ssm_d, m_ssm_norm_w, m_attn_sinks, m_w_branch_ssm, m_w_branch_attn, m_w_mix_out, m_ln1_g, m_ln1_b, m_w_up, m_ffn_conv_w, m_ffn_conv_b, m_w_down, m_ln2_g, m_ln2_b, v_rel_bias, v_w_in, v_b_gate, v_ssm_conv_w, v_ssm_conv_b, v_ssm_dt_bias, v_ssm_a_log, v_ssm_d, v_ssm_norm_w, v_attn_sinks, v_w_branch_ssm, v_w_branch_attn, v_w_mix_out, v_ln1_g, v_ln1_b, v_w_up, v_ffn_conv_w, v_ffn_conv_b, v_w_down, v_ln2_g, v_ln2_b):
    given = dict(locals())
    drop = lambda a, n: a if n == "rel_bias" or a.ndim == 2 else a[0]
    w = {n: drop(given[n], n) for n in _WEIGHTS}
    m = {n: drop(given["m_" + n], n) for n in _WEIGHTS}
    v = {n: drop(given["v_" + n], n) for n in _WEIGHTS}
    loss, d_x, outs = _step(x[0], loss_target[0], w, m, v)
    like = lambda a, n: a.reshape(given[n].shape)
    res = [loss, d_x[None]]
    for kind in ("grad", "delta", "m", "v"):
        res += [like(outs[kind][n], n) for n in _WEIGHTS]
    return tuple(res)
```

```python
import functools
import math

import numpy as np
import jax
import jax.numpy as jnp
from jax import lax
from jax.experimental import pallas as pl
from jax.experimental.pallas import tpu as pltpu

F32 = jnp.float32
_ACT = jnp.bfloat16
_MXU = jnp.bfloat16

D_MODEL = 1024
D_INNER = 2048
N_HEADS = 32
HEAD_P = 64
N_GROUPS = 4
N_STATE = 128
CHUNK = 128
CONV_DIM = 3072
SSM_K = 4
A_HEADS = 16
A_DH = 64
WIN = 128
REL_BUCKETS = 32
D_FF = 2816
FFN_K = 3
ALPHA = 2.0 ** 0.25
LN_EPS = 1e-5
RMS_EPS = 1e-5
IN_COLS = 8480
NEG = -1e30

ADAM_LR = 0.001
ADAM_B1 = 0.9
ADAM_B2 = 0.999
ADAM_EPS = 1e-08
ADAM_WD = 0.01
ADAM_STEP = 10

LANE = 128
SUB = 8

P_Z, P_XS, P_G, P_Q, P_BC, P_K, P_V, P_DT = 0, 2048, 4096, 6144, 7168, 8192, 8320, 8448
P_W = 8704
P_MAIN = 8192
T_K, T_V, T_DT = P_K - P_MAIN, P_V - P_MAIN, P_DT - P_MAIN
_PIECES = ((0, 2048, P_Z), (2048, 2048, P_XS), (4096, 1024, P_BC), (5120, 32, P_DT), (5152, 1024, P_Q),
           (6176, 128, P_K), (6304, 128, P_V), (6432, 2048, P_G))

N_CHIPS = 4
N_DEV = 8


def _cp(sem=None, vmem_mb=48):
    return pltpu.CompilerParams(dimension_semantics=sem, vmem_limit_bytes=vmem_mb * 1024 * 1024)


def _pick(n, cands):
    for c in cands:
        if n % c == 0:
            return c
    raise ValueError(f"no block size for {n}")


def _rows8(p):
    k, c = p.shape
    return jnp.concatenate([p.astype(F32), jnp.zeros((SUB - k, c), F32)], axis=0)


def _mm(a, b, name, *, trans_a=False, out_dtype=F32, res=None, res_scale=1.0):
    if trans_a:
        k_dim, m = a.shape
    else:
        m, k_dim = a.shape
    k2, n = b.shape
    assert k_dim == k2, (a.shape, b.shape)
    tm = _pick(m, (1024, 512, 256, 128))
    tn = _pick(n, (1024, 512, 256, 128))
    tk = _pick(k_dim, (1024, 512, 256, 128))
    nk = k_dim // tk
    dn = (((0,), (0,)), ((), ())) if trans_a else (((1,), (0,)), ((), ()))

    def body(*refs):
        if res is None:
            a_ref, b_ref, o_ref = refs[:3]
        else:
            a_ref, b_ref, r_ref, o_ref = refs[:4]

        def finish(r):
            if res is not None:
                r = r + res_scale * r_ref[...]
            o_ref[...] = r.astype(out_dtype)

        part = lax.dot_general(a_ref[...].astype(_MXU), b_ref[...].astype(_MXU), dn, preferred_element_type=F32)
        if nk == 1:
            finish(part)
            return
        acc = refs[-1]
        k = pl.program_id(2)

        @pl.when(k == 0)
        def _():
            acc[...] = part

        @pl.when(k > 0)
        def _():
            acc[...] += part

        @pl.when(k == nk - 1)
        def _():
            finish(acc[...])

    if trans_a:
        a_spec = pl.BlockSpec((tk, tm), lambda i, j, k: (k, i))
    else:
        a_spec = pl.BlockSpec((tm, tk), lambda i, j, k: (i, k))
    in_specs = [a_spec, pl.BlockSpec((tk, tn), lambda i, j, k: (k, j))]
    args = [a, b]
    if res is not None:
        in_specs.append(pl.BlockSpec((tm, tn), lambda i, j, k: (i, j)))
        args.append(res)
    return pl.pallas_call(
        body, name=name, out_shape=jax.ShapeDtypeStruct((m, n), out_dtype),
        grid=(m // tm, n // tn, nk), in_specs=in_specs,
        out_specs=pl.BlockSpec((tm, tn), lambda i, j, k: (i, j)),
        scratch_shapes=[pltpu.VMEM((tm, tn), F32)] if nk > 1 else [],
        compiler_params=_cp(("parallel", "parallel", "arbitrary")),
    )(*args)


def _shift_down(cur, prev8, s):
    r = pltpu.roll(cur, s, 0)
    p = pltpu.roll(prev8, s, 0)
    row8 = lax.broadcasted_iota(jnp.int32, (SUB, 1), 0)
    fixed = jnp.where(row8 < s, p, r[0:SUB])
    if cur.shape[0] == SUB:
        return fixed
    return jnp.concatenate([fixed, r[SUB:]], axis=0)


def _shift_up(cur, next8, s):
    tm = cur.shape[0]
    r = pltpu.roll(cur, tm - s, 0)
    p = pltpu.roll(next8, SUB - s, 0)
    row8 = lax.broadcasted_iota(jnp.int32, (SUB, 1), 0)
    fixed = jnp.where(row8 >= SUB - s, p, r[tm - SUB:])
    return jnp.concatenate([r[:tm - SUB], fixed], axis=0)


def _conv_pre(cur, prev8, w_ref, b_row, taps):
    acc = cur * w_ref[taps - 1:taps, :] + b_row
    for s in range(1, taps):
        acc = acc + _shift_down(cur, prev8, s) * w_ref[taps - 1 - s:taps - s, :]
    return acc


def _dot01_r(x, m01, parts=3):
    acc = None
    r = x
    for _ in range(parts):
        hi = r.astype(jnp.bfloat16)
        t = jnp.dot(hi, m01, preferred_element_type=F32)
        acc = t if acc is None else acc + t
        r = r - hi.astype(F32)
    return acc


def _dot01_l(m01, x, parts=3):
    acc = None
    r = x
    for _ in range(parts):
        hi = r.astype(jnp.bfloat16)
        t = jnp.dot(m01, hi, preferred_element_type=F32)
        acc = t if acc is None else acc + t
        r = r - hi.astype(F32)
    return acc


def _dot(a, b):
    return jnp.dot(a.astype(_MXU), b.astype(_MXU), preferred_element_type=F32)


def _dot_nt(a, b):
    return lax.dot_general(a.astype(_MXU), b.astype(_MXU), (((1,), (1,)), ((), ())), preferred_element_type=F32)


def _dot_tn(a, b):
    return lax.dot_general(a.astype(_MXU), b.astype(_MXU), (((0,), (0,)), ((), ())), preferred_element_type=F32)


def _sigmoid(x):
    return 1.0 / (1.0 + jnp.exp(-x))


def _half_masks():
    lane = lax.broadcasted_iota(jnp.int32, (1, LANE), 1)
    lo = (lane < 64).astype(F32)
    return lo, 1.0 - lo


_TC = 512


def _tm_rows(t):
    return min(256, t)


HALO = 16


def _prev_halo(tm, width, pos):
    def index(*ids):
        i, col = pos(*ids)
        return (jnp.maximum(i * (tm // HALO) - 1, 0), col)
    return pl.BlockSpec((HALO, width), index)


def _next_halo(tm, t, width, pos):
    def index(*ids):
        i, col = pos(*ids)
        return (jnp.minimum((i + 1) * (tm // HALO), t // HALO - 1), col)
    return pl.BlockSpec((HALO, width), index)


def _conv_silu_fwd(proj, colblk0, nblk, w8, b8, name):
    t = proj.shape[0]
    tm = _tm_rows(t)

    def body(c_ref, p_ref, w_ref, b_ref, o_ref):
        i = pl.program_id(1)
        prev8 = jnp.where(i > 0, p_ref[SUB:HALO, :].astype(F32), 0.0)
        pre = _conv_pre(c_ref[...].astype(F32), prev8, w_ref, b_ref[0:1, :], SSM_K)
        o_ref[...] = pre * _sigmoid(pre)

    return pl.pallas_call(
        body, name=name, out_shape=jax.ShapeDtypeStruct((t, nblk * _TC), F32),
        grid=(nblk, t // tm),
        in_specs=[pl.BlockSpec((tm, _TC), lambda j, i: (i, colblk0 + j)),
                  _prev_halo(tm, _TC, lambda j, i: (i, colblk0 + j)),
                  pl.BlockSpec((SUB, _TC), lambda j, i: (0, j)),
                  pl.BlockSpec((SUB, _TC), lambda j, i: (0, j))],
        out_specs=pl.BlockSpec((tm, _TC), lambda j, i: (i, j)),
        compiler_params=_cp(("parallel", "parallel")),
    )(proj, proj, w8, b8)


def _silu_grad(pre):
    sg = _sigmoid(pre)
    return sg * (1.0 + pre * (1.0 - sg))


def _conv_t(d_cur, d_next8, w_ref, taps):
    acc = d_cur * w_ref[taps - 1:taps, :]
    for s in range(1, taps):
        acc = acc + _shift_up(d_cur, d_next8, s) * w_ref[taps - 1 - s:taps - s, :]
    return acc


def _conv_silu_bwd(proj, colblk0, nblk, w8, b8, d_out, name):
    t = proj.shape[0]
    tm = _tm_rows(t)
    nt = t // tm

    def body(c_ref, p_ref, n_ref, w_ref, b_ref, d_ref, dn_ref, du_ref, acc_ref):
        i = pl.program_id(1)

        @pl.when(i == 0)
        def _():
            acc_ref[...] = jnp.zeros_like(acc_ref)

        cur = c_ref[...].astype(F32)
        prev8 = jnp.where(i > 0, p_ref[SUB:HALO, :].astype(F32), 0.0)
        b_row = b_ref[0:1, :]
        dpre = d_ref[...].astype(F32) * _silu_grad(_conv_pre(cur, prev8, w_ref, b_row, SSM_K))
        pre_n = _conv_pre(n_ref[0:SUB, :].astype(F32), cur[tm - SUB:], w_ref, b_row, SSM_K)
        dpre_n = jnp.where(i < nt - 1, dn_ref[0:SUB, :].astype(F32) * _silu_grad(pre_n), 0.0)
        du_ref[...] = _conv_t(dpre, dpre_n, w_ref, SSM_K).astype(_ACT)
        acc_ref[SSM_K:SSM_K + 1, :] += jnp.sum(dpre, axis=0, keepdims=True)
        acc_ref[SSM_K - 1:SSM_K, :] += jnp.sum(dpre * cur, axis=0, keepdims=True)
        for s in range(1, SSM_K):
            acc_ref[SSM_K - 1 - s:SSM_K - s, :] += jnp.sum(dpre * _shift_down(cur, prev8, s), axis=0, keepdims=True)

    c = nblk * _TC
    return pl.pallas_call(
        body, name=name,
        out_shape=(jax.ShapeDtypeStruct((t, c), _ACT), jax.ShapeDtypeStruct((SUB, c), F32)),
        grid=(nblk, nt),
        in_specs=[pl.BlockSpec((tm, _TC), lambda j, i: (i, colblk0 + j)),
                  _prev_halo(tm, _TC, lambda j, i: (i, colblk0 + j)),
                  _next_halo(tm, t, _TC, lambda j, i: (i, colblk0 + j)),
                  pl.BlockSpec((SUB, _TC), lambda j, i: (0, j)),
                  pl.BlockSpec((SUB, _TC), lambda j, i: (0, j)),
                  pl.BlockSpec((tm, _TC), lambda j, i: (i, j)),
                  _next_halo(tm, t, _TC, lambda j, i: (i, j))],
        out_specs=(pl.BlockSpec((tm, _TC), lambda j, i: (i, j)),
                   pl.BlockSpec((SUB, _TC), lambda j, i: (0, j))),
        compiler_params=_cp(("parallel", "arbitrary")),
    )(proj, proj, proj, w8, b8, d_out, d_out)


def _expand_consts():
    e = np.zeros((LANE, D_INNER), np.float32)
    for h in range(N_HEADS):
        e[h, h * HEAD_P:(h + 1) * HEAD_P] = 1.0
    return jnp.asarray(e, jnp.bfloat16), jnp.asarray(e.T.copy(), jnp.bfloat16)


def _ssd_common(dtr_ref, dtb_ref, alog_ref, e_ref):
    lane = lax.broadcasted_iota(jnp.int32, (1, LANE), 1)
    hm = lane < N_HEADS
    pre = dtr_ref[...] + dtb_ref[0:1, :]
    dt = jnp.where(hm, jnp.maximum(pre, 0.0) + jnp.log(1.0 + jnp.exp(-jnp.abs(pre))), 0.0)
    a_row = jnp.where(hm, -jnp.exp(alog_ref[0:1, :]), 0.0)
    adt = dt * a_row
    r = lax.broadcasted_iota(jnp.int32, (CHUNK, CHUNK), 0)
    c = lax.broadcasted_iota(jnp.int32, (CHUNK, CHUNK), 1)
    causal = r >= c
    acs = _dot01_l(causal.astype(jnp.bfloat16), adt)
    e = e_ref[...]
    acs_x = _dot01_r(acs, e)
    dt_x = _dot01_r(dt, e)
    return pre, dt, a_row, acs, acs_x, dt_x, causal, hm


def _decay(acs, acs_t, h, causal):
    seg = acs[:, h:h + 1] - acs_t[h:h + 1, :]
    return jnp.exp(jnp.where(causal, seg, NEG))


def _ssd_fwd(xs_c, bc_c, proj, tail, dtb8, alog8, dsk8, nw8, name):
    t = xs_c.shape[0]
    nc = t // CHUNK
    e_bf, _ = _expand_consts()
    gw = D_INNER // N_GROUPS

    def body(xs_ref, bc_ref, dtr_ref, z_ref, dtb_ref, alog_ref, dsk_ref, nw_ref, e_ref,
             y_ref, ys_ref, hp_ref, h_ref):
        c_id = pl.program_id(0)

        @pl.when(c_id == 0)
        def _():
            h_ref[...] = jnp.zeros_like(h_ref)

        _, dt, a_row, acs, acs_x, dt_x, causal, _ = _ssd_common(dtr_ref, dtb_ref, alog_ref, e_ref)
        acs_t = acs.T
        xs = xs_ref[...]
        x_dt = xs * dt_x
        last_x = acs_x[CHUNK - 1:CHUNK, :]
        w_end = jnp.exp(last_x - acs_x)
        e_in = jnp.exp(acs_x)
        d_x = _dot01_r(dsk_ref[...], e_ref[...])[0:1, :]
        hprev = h_ref[...]
        hp_ref[...] = hprev
        lo, hi = _half_masks()
        for g in range(N_GROUPS):
            bg = bc_ref[:, g * N_STATE:(g + 1) * N_STATE]
            cg = bc_ref[:, N_GROUPS * N_STATE + g * N_STATE:N_GROUPS * N_STATE + (g + 1) * N_STATE]
            sl = slice(g * gw, (g + 1) * gw)
            gm = _dot_nt(cg, bg)
            st = _dot(bg.T, x_dt[:, sl] * w_end[:, sl])
            y_off = _dot(cg, hprev[:, sl]) * e_in[:, sl]
            for j in range(gw // LANE):
                h0 = g * (gw // HEAD_P) + 2 * j
                cs = slice(g * gw + j * LANE, g * gw + (j + 1) * LANE)
                xp = x_dt[:, cs]
                m0 = gm * _decay(acs, acs_t, h0, causal)
                m1 = gm * _decay(acs, acs_t, h0 + 1, causal)
                yd = _dot(m0, xp * lo) + _dot(m1, xp * hi)
                y_ref[:, cs] = yd + y_off[:, j * LANE:(j + 1) * LANE] + xs[:, cs] * d_x[:, cs]
            h_ref[:, sl] = hprev[:, sl] * jnp.exp(last_x[:, sl]) + st
        y = y_ref[...]
        z = z_ref[...].astype(F32)
        y2 = y * (z * _sigmoid(z))
        for g in range(N_GROUPS):
            sl = slice(g * gw, (g + 1) * gw)
            yg = y2[:, sl]
            rinv = lax.rsqrt(jnp.mean(yg * yg, axis=-1, keepdims=True) + RMS_EPS)
            ys_ref[:, sl] = (yg * rinv * nw_ref[0:1, sl]).astype(_ACT)

    small = pl.BlockSpec((SUB, LANE), lambda c: (0, 0))
    return pl.pallas_call(
        body, name=name,
        out_shape=(jax.ShapeDtypeStruct((t, D_INNER), F32), jax.ShapeDtypeStruct((t, D_INNER), _ACT),
                   jax.ShapeDtypeStruct((t, D_INNER), F32)),
        grid=(nc,),
        in_specs=[pl.BlockSpec((CHUNK, D_INNER), lambda c: (c, 0)),
                  pl.BlockSpec((CHUNK, 1024), lambda c: (c, 0)),
                  pl.BlockSpec((CHUNK, LANE), lambda c: (c, T_DT // LANE)),
                  pl.BlockSpec((CHUNK, D_INNER), lambda c: (c, P_Z // D_INNER)),
                  small, small, small,
                  pl.BlockSpec((SUB, D_INNER), lambda c: (0, 0)),
                  pl.BlockSpec((LANE, D_INNER), lambda c: (0, 0))],
        out_specs=(pl.BlockSpec((CHUNK, D_INNER), lambda c: (c, 0)),
                   pl.BlockSpec((CHUNK, D_INNER), lambda c: (c, 0)),
                   pl.BlockSpec((N_STATE, D_INNER), lambda c: (c, 0))),
        scratch_shapes=[pltpu.VMEM((N_STATE, D_INNER), F32)],
        compiler_params=_cp(("arbitrary",)),
    )(xs_c, bc_c, tail, proj, dtb8, alog8, dsk8, nw8, e_bf)


def _ssd_bwd(d_ys, y, xs_c, bc_c, proj, tail, hprev_all, dtb8, alog8, dsk8, nw8, name):
    t = xs_c.shape[0]
    nc = t // CHUNK
    e_bf, et_bf = _expand_consts()
    gw = D_INNER // N_GROUPS

    def body(dys_ref, y_ref, xs_ref, bc_ref, dtr_ref, z_ref, hp_ref, dtb_ref, alog_ref, dsk_ref, nw_ref,
             e_ref, et_ref, dxs_ref, dbc_ref, dz_ref, ddt_ref, acc_ref, dnw_ref, dh_ref, dx_ref):
        step = pl.program_id(0)

        @pl.when(step == 0)
        def _():
            dh_ref[...] = jnp.zeros_like(dh_ref)
            acc_ref[...] = jnp.zeros_like(acc_ref)
            dnw_ref[...] = jnp.zeros_like(dnw_ref)

        pre, dt, a_row, acs, acs_x, dt_x, causal, hm = _ssd_common(dtr_ref, dtb_ref, alog_ref, e_ref)
        acs_t = acs.T
        et = et_ref[...]
        xs = xs_ref[...]
        x_dt = xs * dt_x
        last_x = acs_x[CHUNK - 1:CHUNK, :]
        w_end = jnp.exp(last_x - acs_x)
        e_in = jnp.exp(acs_x)
        e_last = jnp.exp(last_x)
        d_x = _dot01_r(dsk_ref[...], e_ref[...])[0:1, :]

        y = y_ref[...]
        z = z_ref[...].astype(F32)
        sz = _sigmoid(z)
        gz = z * sz
        y2 = y * gz
        dys = dys_ref[...].astype(F32)
        for g in range(N_GROUPS):
            sl = slice(g * gw, (g + 1) * gw)
            yg = y2[:, sl]
            rinv = lax.rsqrt(jnp.mean(yg * yg, axis=-1, keepdims=True) + RMS_EPS)
            nrm = yg * rinv
            dn = dys[:, sl] * nw_ref[0:1, sl]
            dnw_ref[0:1, sl] += jnp.sum(dys[:, sl] * nrm, axis=0, keepdims=True)
            dx_ref[:, sl] = rinv * (dn - nrm * jnp.mean(dn * nrm, axis=-1, keepdims=True))
        dy2 = dx_ref[...]
        dy = dy2 * gz
        dz_ref[...] = (dy2 * y * (sz * (1.0 + z * (1.0 - sz)))).astype(_ACT)

        dh_next = dh_ref[...]
        hprev = hp_ref[...]
        lo, hi = _half_masks()
        r = lax.broadcasted_iota(jnp.int32, (CHUNK, CHUNK), 0)
        c = lax.broadcasted_iota(jnp.int32, (CHUNK, CHUNK), 1)
        from_here = (c >= r).astype(jnp.bfloat16)
        before = c < r
        lane = lax.broadcasted_iota(jnp.int32, (1, LANE), 1)
        da_intra = jnp.zeros((CHUNK, LANE), F32)
        v_seg = jnp.zeros((CHUNK, LANE), F32)
        z_seg = jnp.zeros((CHUNK, LANE), F32)
        tail_parts = []
        for g in range(N_GROUPS):
            bg = bc_ref[:, g * N_STATE:(g + 1) * N_STATE]
            cg = bc_ref[:, N_GROUPS * N_STATE + g * N_STATE:N_GROUPS * N_STATE + (g + 1) * N_STATE]
            sl = slice(g * gw, (g + 1) * gw)
            et_g = et_ref[g * gw:(g + 1) * gw, :]
            gm = _dot_nt(cg, bg)
            dzg = e_in[:, sl] * dy[:, sl]
            dcg = _dot_nt(dzg, hprev[:, sl])
            dh_c = _dot(cg.T, dzg)
            q = _dot(bg, dh_next[:, sl])
            dbg = _dot_nt(x_dt[:, sl] * w_end[:, sl], dh_next[:, sl])
            y_off = _dot(cg, hprev[:, sl]) * e_in[:, sl]
            v_seg = v_seg + _dot01_r(dy[:, sl] * y_off, et_g)
            z_seg = z_seg + _dot01_r(w_end[:, sl] * q * x_dt[:, sl], et_g)
            dgm = jnp.zeros((CHUNK, CHUNK), F32)
            for j in range(gw // LANE):
                h0 = g * (gw // HEAD_P) + 2 * j
                cs = slice(g * gw + j * LANE, g * gw + (j + 1) * LANE)
                xp = x_dt[:, cs]
                dyp = dy[:, cs]
                dxd = jnp.zeros((CHUNK, LANE), F32)
                for half, msk in ((0, lo), (1, hi)):
                    lam = _decay(acs, acs_t, h0 + half, causal)
                    mm = gm * lam
                    dym = dyp * msk
                    dmm = _dot_nt(dym, xp)
                    dxd = dxd + _dot_tn(mm, dym)
                    dgm = dgm + dmm * lam
                    below = _dot(from_here, dmm * mm)
                    col = jnp.sum(jnp.where(before, below, 0.0), axis=-1, keepdims=True)
                    da_intra = da_intra + jnp.where(lane == h0 + half, col, 0.0)
                dx_ref[:, cs] = dxd + w_end[:, cs] * q[:, j * LANE:(j + 1) * LANE]
            dbc_ref[:, N_GROUPS * N_STATE + g * N_STATE:N_GROUPS * N_STATE + (g + 1) * N_STATE] = dcg + _dot(dgm, bg)
            dbc_ref[:, g * N_STATE:(g + 1) * N_STATE] = dbg + _dot_tn(dgm, cg)
            dh_ref[:, sl] = e_last[:, sl] * dh_next[:, sl] + dh_c
            tail_parts.append(e_last[:, sl] * jnp.sum(dh_next[:, sl] * hprev[:, sl], axis=0, keepdims=True))
        dxt = dx_ref[...]

        u_seg = _dot01_r(xs * dxt, et)
        q_full = jnp.concatenate(tail_parts, axis=1)
        t_row = _dot01_r(jnp.broadcast_to(q_full, (SUB, D_INNER)), et)[0:1, :]
        d_alpha = (da_intra + _dot01_l(from_here, v_seg) + _dot01_l(before.astype(jnp.bfloat16), z_seg) + t_row)
        d_dt = a_row * d_alpha + u_seg
        sgp = _sigmoid(pre)
        d_raw = jnp.where(hm, d_dt * sgp, 0.0)
        ddt_ref[...] = d_raw.astype(_ACT)
        acc_ref[0:1, :] += jnp.sum(d_raw, axis=0, keepdims=True)
        acc_ref[1:2, :] += jnp.sum(d_alpha * dt, axis=0, keepdims=True) * a_row
        dd_row = jnp.sum(dy * xs, axis=0, keepdims=True)
        acc_ref[2:3, :] += _dot01_r(jnp.broadcast_to(dd_row, (SUB, D_INNER)), et)[0:1, :]
        dxs_ref[...] = dy * d_x + dxt * dt_x

    rev = lambda c: (nc - 1 - c, 0)
    small = pl.BlockSpec((SUB, LANE), lambda c: (0, 0))
    return pl.pallas_call(
        body, name=name,
        out_shape=(jax.ShapeDtypeStruct((t, D_INNER), F32), jax.ShapeDtypeStruct((t, 1024), F32),
                   jax.ShapeDtypeStruct((t, D_INNER), _ACT), jax.ShapeDtypeStruct((t, LANE), _ACT),
                   jax.ShapeDtypeStruct((SUB, LANE), F32), jax.ShapeDtypeStruct((SUB, D_INNER), F32)),
        grid=(nc,),
        in_specs=[pl.BlockSpec((CHUNK, D_INNER), rev),
                  pl.BlockSpec((CHUNK, D_INNER), rev),
                  pl.BlockSpec((CHUNK, D_INNER), rev),
                  pl.BlockSpec((CHUNK, 1024), rev),
                  pl.BlockSpec((CHUNK, LANE), lambda c: (nc - 1 - c, T_DT // LANE)),
                  pl.BlockSpec((CHUNK, D_INNER), lambda c: (nc - 1 - c, P_Z // D_INNER)),
                  pl.BlockSpec((N_STATE, D_INNER), rev),
                  small, small, small,
                  pl.BlockSpec((SUB, D_INNER), lambda c: (0, 0)),
                  pl.BlockSpec((LANE, D_INNER), lambda c: (0, 0)),
                  pl.BlockSpec((D_INNER, LANE), lambda c: (0, 0))],
        out_specs=(pl.BlockSpec((CHUNK, D_INNER), rev),
                   pl.BlockSpec((CHUNK, 1024), rev),
                   pl.BlockSpec((CHUNK, D_INNER), rev),
                   pl.BlockSpec((CHUNK, LANE), rev),
                   small,
                   pl.BlockSpec((SUB, D_INNER), lambda c: (0, 0))),
        scratch_shapes=[pltpu.VMEM((N_STATE, D_INNER), F32), pltpu.VMEM((CHUNK, D_INNER), F32)],
        compiler_params=_cp(("arbitrary",), vmem_mb=56),
    )(d_ys, y, xs_c, bc_c, tail, proj, hprev_all, dtb8, alog8, dsk8, nw8, e_bf, et_bf)


def _rel_tables():
    qi = np.arange(WIN)[:, None] + WIN
    kj = np.arange(2 * WIN)[None, :]
    rel = qi - kj
    n = np.maximum(rel, 0)
    max_exact = REL_BUCKETS // 2
    nf = np.maximum(n, 1).astype(np.float32)
    large = max_exact + (np.log(nf / np.float32(max_exact)) / np.float32(math.log(WIN / max_exact))
                         * np.float32(REL_BUCKETS - max_exact)).astype(np.int32)
    large = np.minimum(large, REL_BUCKETS - 1)
    bucket = np.where(n < max_exact, n, large)
    valid = (rel >= 0) & (rel < WIN)
    onehot = np.zeros((REL_BUCKETS, WIN * 2 * WIN), np.float32)
    flat_b = bucket.reshape(-1)
    flat_v = valid.reshape(-1)
    idx = np.arange(WIN * 2 * WIN)
    onehot[flat_b[flat_v], idx[flat_v]] = 1.0
    return onehot, flat_v.astype(np.float32)[None, :]


def _bias_expand(rel_bias_t, name):
    onehot, valid = _rel_tables()

    def body(rb_ref, oh_ref, v_ref, o_ref):
        o_ref[...] = jnp.where(v_ref[...] > 0.5, _dot01_r(rb_ref[...], oh_ref[...]), NEG)

    return pl.pallas_call(
        body, name=name, out_shape=jax.ShapeDtypeStruct((A_HEADS, WIN * 2 * WIN), F32),
        compiler_params=_cp(None),
    )(rel_bias_t, jnp.asarray(onehot, jnp.bfloat16), jnp.asarray(valid, F32))


def _bias_reduce(dbias, name):
    onehot, _ = _rel_tables()

    def body(d_ref, oh_ref, o_ref):
        acc = None
        r = d_ref[...]
        for _ in range(3):
            hi = r.astype(jnp.bfloat16)
            tt = lax.dot_general(hi, oh_ref[...], (((1,), (1,)), ((), ())), preferred_element_type=F32)
            acc = tt if acc is None else acc + tt
            r = r - hi.astype(F32)
        o_ref[...] = acc

    return pl.pallas_call(
        body, name=name, out_shape=jax.ShapeDtypeStruct((A_HEADS, REL_BUCKETS), F32),
        compiler_params=_cp(None),
    )(dbias, jnp.asarray(onehot, jnp.bfloat16))


def _attn_bands(kc_ref, kp_ref, vc_ref, vp_ref, has_prev):
    lo, hi = _half_masks()
    kb = jnp.concatenate([jnp.where(has_prev, kp_ref[...], 0.0), kc_ref[...]], axis=0)
    vb = jnp.concatenate([jnp.where(has_prev, vp_ref[...], 0.0), vc_ref[...]], axis=0)
    kr = pltpu.roll(kb, 64, 1)
    vr = pltpu.roll(vb, 64, 1)
    kk = ((kb * lo, kr * hi), (kr * lo, kb * hi))
    vv = ((vb * lo, vr * hi), (vr * lo, vb * hi))
    return kk, vv


def _attn_probs(qp, k_half, bias_h, sink, col_ok):
    logits = _dot_nt(qp, k_half) * (A_DH ** -0.5) + bias_h
    logits = jnp.where(col_ok, logits, NEG)
    m = jnp.maximum(jnp.max(logits, axis=-1, keepdims=True), sink)
    p = jnp.exp(logits - m)
    es = jnp.exp(sink - m)
    inv = 1.0 / (jnp.sum(p, axis=-1, keepdims=True) + es)
    return p * inv, es * inv


def _attn_fwd(proj, tail, bias, sinks8, name):
    t = proj.shape[0]
    nb = t // WIN

    def body(q_ref, kc_ref, kp_ref, vc_ref, vp_ref, b_ref, s_ref, o_ref):
        n = pl.program_id(0)
        has_prev = n > 0
        kk, vv = _attn_bands(kc_ref, kp_ref, vc_ref, vp_ref, has_prev)
        col = lax.broadcasted_iota(jnp.int32, (1, 2 * WIN), 1)
        col_ok = jnp.logical_or(has_prev, col >= WIN)
        for j in range(A_HEADS // 2):
            kv = (2 * j) // (A_HEADS // 2)
            qp = q_ref[:, j * LANE:(j + 1) * LANE]
            out = None
            for half in range(2):
                h = 2 * j + half
                probs, _ = _attn_probs(qp, kk[kv][half], b_ref[h], s_ref[0:1, h:h + 1], col_ok)
                o = _dot(probs, vv[kv][half])
                out = o if out is None else out + o
            o_ref[:, j * LANE:(j + 1) * LANE] = out.astype(_ACT)

    kvspec = lambda col, prev: pl.BlockSpec(
        (WIN, LANE), (lambda n: (jnp.maximum(n - 1, 0), col)) if prev else (lambda n: (n, col)))
    return pl.pallas_call(
        body, name=name, out_shape=jax.ShapeDtypeStruct((t, D_MODEL), _ACT),
        grid=(nb,),
        in_specs=[pl.BlockSpec((WIN, 1024), lambda n: (n, P_Q // 1024)),
                  kvspec(T_K // LANE, False), kvspec(T_K // LANE, True),
                  kvspec(T_V // LANE, False), kvspec(T_V // LANE, True),
                  pl.BlockSpec((A_HEADS, WIN, 2 * WIN), lambda n: (0, 0, 0)),
                  pl.BlockSpec((SUB, LANE), lambda n: (0, 0))],
        out_specs=pl.BlockSpec((WIN, 1024), lambda n: (n, 0)),
        compiler_params=_cp(("parallel",)),
    )(proj, tail, tail, tail, tail, bias, sinks8)


def _attn_bwd(proj, tail, bias, sinks8, d_out, name):
    t = proj.shape[0]
    nb = t // WIN

    def body(q_ref, kc_ref, kp_ref, vc_ref, vp_ref, b_ref, s_ref, do_ref,
             dq_ref, dk_ref, dv_ref, db_ref, ds_ref, ck_ref, cv_ref, sacc_ref):
        n = pl.program_id(0)

        @pl.when(n == 0)
        def _():
            db_ref[...] = jnp.zeros_like(db_ref)
            sacc_ref[...] = jnp.zeros_like(sacc_ref)
            ck_ref[...] = jnp.zeros_like(ck_ref)
            cv_ref[...] = jnp.zeros_like(cv_ref)

        @pl.when(n < nb)
        def _():
            has_prev = n > 0
            kk, vv = _attn_bands(kc_ref, kp_ref, vc_ref, vp_ref, has_prev)
            lo, hi = _half_masks()
            col = lax.broadcasted_iota(jnp.int32, (1, 2 * WIN), 1)
            col_ok = jnp.logical_or(has_prev, col >= WIN)
            lane = lax.broadcasted_iota(jnp.int32, (1, LANE), 1)
            dk_acc = [[jnp.zeros((2 * WIN, LANE), F32) for _ in range(2)] for _ in range(2)]
            dv_acc = [[jnp.zeros((2 * WIN, LANE), F32) for _ in range(2)] for _ in range(2)]
            sink_add = jnp.zeros((WIN, LANE), F32)
            for j in range(A_HEADS // 2):
                kv = (2 * j) // (A_HEADS // 2)
                qp = q_ref[:, j * LANE:(j + 1) * LANE]
                dop = do_ref[:, j * LANE:(j + 1) * LANE]
                dq = None
                for half, msk in ((0, lo), (1, hi)):
                    h = 2 * j + half
                    probs, psink = _attn_probs(qp, kk[kv][half], b_ref[h], s_ref[0:1, h:h + 1], col_ok)
                    dprobs = _dot_nt(dop, vv[kv][half])
                    delta = jnp.sum(probs * dprobs, axis=-1, keepdims=True)
                    dlog = probs * (dprobs - delta)
                    db_ref[h] += dlog
                    sink_add = sink_add + jnp.where(lane == h, -psink * delta, 0.0)
                    dls = dlog * (A_DH ** -0.5)
                    dqh = _dot(dls, kk[kv][half])
                    dq = dqh if dq is None else dq + dqh
                    dk_acc[kv][half] = dk_acc[kv][half] + _dot_tn(dls, qp * msk)
                    dv_acc[kv][half] = dv_acc[kv][half] + _dot_tn(probs, dop * msk)
                dq_ref[:, j * LANE:(j + 1) * LANE] = dq.astype(_ACT)
            sacc_ref[...] += sink_add
            dkb = dk_acc[0][0] + pltpu.roll(dk_acc[0][1], 64, 1) + pltpu.roll(dk_acc[1][0], 64, 1) + dk_acc[1][1]
            dvb = dv_acc[0][0] + pltpu.roll(dv_acc[0][1], 64, 1) + pltpu.roll(dv_acc[1][0], 64, 1) + dv_acc[1][1]
            dk_ref[...] = (ck_ref[...] + dkb[0:WIN]).astype(_ACT)
            dv_ref[...] = (cv_ref[...] + dvb[0:WIN]).astype(_ACT)
            ck_ref[...] = dkb[WIN:]
            cv_ref[...] = dvb[WIN:]

        @pl.when(n == nb)
        def _():
            dk_ref[...] = ck_ref[...].astype(_ACT)
            dv_ref[...] = cv_ref[...].astype(_ACT)
            ds_ref[...] = jnp.zeros_like(ds_ref)
            ds_ref[0:1, :] = jnp.sum(sacc_ref[...], axis=0, keepdims=True)

    cur = lambda n: jnp.minimum(n, nb - 1)
    prv = lambda n: jnp.maximum(jnp.minimum(n, nb - 1) - 1, 0)
    kvspec = lambda col, prev: pl.BlockSpec(
        (WIN, LANE), (lambda n: (prv(n), col)) if prev else (lambda n: (cur(n), col)))
    return pl.pallas_call(
        body, name=name,
        out_shape=(jax.ShapeDtypeStruct((t, D_MODEL), _ACT), jax.ShapeDtypeStruct((t, LANE), _ACT),
                   jax.ShapeDtypeStruct((t, LANE), _ACT), jax.ShapeDtypeStruct((A_HEADS, WIN, 2 * WIN), F32),
                   jax.ShapeDtypeStruct((SUB, LANE), F32)),
        grid=(nb + 1,),
        in_specs=[pl.BlockSpec((WIN, 1024), lambda n: (cur(n), P_Q // 1024)),
                  kvspec(T_K // LANE, False), kvspec(T_K // LANE, True),
                  kvspec(T_V // LANE, False), kvspec(T_V // LANE, True),
                  pl.BlockSpec((A_HEADS, WIN, 2 * WIN), lambda n: (0, 0, 0)),
                  pl.BlockSpec((SUB, LANE), lambda n: (0, 0)),
                  pl.BlockSpec((WIN, 1024), lambda n: (cur(n), 0))],
        out_specs=(pl.BlockSpec((WIN, 1024), lambda n: (cur(n), 0)),
                   pl.BlockSpec((WIN, LANE), lambda n: (jnp.maximum(n - 1, 0), 0)),
                   pl.BlockSpec((WIN, LANE), lambda n: (jnp.maximum(n - 1, 0), 0)),
                   pl.BlockSpec((A_HEADS, WIN, 2 * WIN), lambda n: (0, 0, 0)),
                   pl.BlockSpec((SUB, LANE), lambda n: (0, 0))),
        scratch_shapes=[pltpu.VMEM((WIN, LANE), F32), pltpu.VMEM((WIN, LANE), F32), pltpu.VMEM((WIN, LANE), F32)],
        compiler_params=_cp(("arbitrary",)),
    )(proj, tail, tail, tail, tail, bias, sinks8, d_out)


def _merge_fwd(bs, ba, proj, bg8, name):
    t = bs.shape[0]
    tm = _tm_rows(t)

    def body(bs_ref, ba_ref, gs_ref, ga_ref, bgs_ref, bga_ref, o_ref):
        g_s = _sigmoid(gs_ref[...] + bgs_ref[0:1, :])
        g_a = _sigmoid(ga_ref[...] + bga_ref[0:1, :])
        o_ref[...] = (g_s * bs_ref[...] + g_a * ba_ref[...]).astype(_ACT)

    row = lambda col: pl.BlockSpec((tm, 1024), lambda i: (i, col))
    return pl.pallas_call(
        body, name=name, out_shape=jax.ShapeDtypeStruct((t, D_MODEL), _ACT), grid=(t // tm,),
        in_specs=[row(0), row(0), row(P_G // 1024), row(P_G // 1024 + 1),
                  pl.BlockSpec((SUB, 1024), lambda i: (0, 0)), pl.BlockSpec((SUB, 1024), lambda i: (0, 1))],
        out_specs=row(0), compiler_params=_cp(("parallel",)),
    )(bs, ba, proj, proj, bg8, bg8)


def _merge_bwd(d_merged, bs, ba, proj, bg8, name):
    t = bs.shape[0]
    tm = _tm_rows(t)

    def body(dm_ref, bs_ref, ba_ref, gs_ref, ga_ref, bgs_ref, bga_ref, dbs_ref, dba_ref, dg_ref, acc_ref):
        @pl.when(pl.program_id(0) == 0)
        def _():
            acc_ref[...] = jnp.zeros_like(acc_ref)

        dm = dm_ref[...].astype(F32)
        g_s = _sigmoid(gs_ref[...] + bgs_ref[0:1, :])
        g_a = _sigmoid(ga_ref[...] + bga_ref[0:1, :])
        dbs_ref[...] = (dm * g_s).astype(_ACT)
        dba_ref[...] = (dm * g_a).astype(_ACT)
        dgs = dm * bs_ref[...].astype(F32) * g_s * (1.0 - g_s)
        dga = dm * ba_ref[...].astype(F32) * g_a * (1.0 - g_a)
        dg_ref[:, 0:1024] = dgs.astype(_ACT)
        dg_ref[:, 1024:2048] = dga.astype(_ACT)
        acc_ref[0:1, 0:1024] += jnp.sum(dgs, axis=0, keepdims=True)
        acc_ref[0:1, 1024:2048] += jnp.sum(dga, axis=0, keepdims=True)

    row = lambda col: pl.BlockSpec((tm, 1024), lambda i: (i, col))
    return pl.pallas_call(
        body, name=name,
        out_shape=(jax.ShapeDtypeStruct((t, D_MODEL), _ACT), jax.ShapeDtypeStruct((t, D_MODEL), _ACT),
                   jax.ShapeDtypeStruct((t, 2048), _ACT), jax.ShapeDtypeStruct((SUB, 2048), F32)),
        grid=(t // tm,),
        in_specs=[row(0), row(0), row(0), row(P_G // 1024), row(P_G // 1024 + 1),
                  pl.BlockSpec((SUB, 1024), lambda i: (0, 0)), pl.BlockSpec((SUB, 1024), lambda i: (0, 1))],
        out_specs=(row(0), row(0), pl.BlockSpec((tm, 2048), lambda i: (i, 0)),
                   pl.BlockSpec((SUB, 2048), lambda i: (0, 0))),
        compiler_params=_cp(("arbitrary",)),
    )(d_merged, bs, ba, proj, proj, bg8, bg8)


def _ln_stats(r):
    mu = jnp.mean(r, axis=-1, keepdims=True)
    xc = r - mu
    var = jnp.mean(xc * xc, axis=-1, keepdims=True)
    rstd = lax.rsqrt(var + LN_EPS)
    return xc * rstd, rstd


def _ln_bwd(dxhat, xhat, rstd):
    return rstd * (dxhat - jnp.mean(dxhat, axis=-1, keepdims=True)
                   - xhat * jnp.mean(dxhat * xhat, axis=-1, keepdims=True))


def _ln1_fwd(x, mix, g8, b8, name):
    t = x.shape[0]
    tm = _tm_rows(t)

    def body(x_ref, m_ref, g_ref, b_ref, xh_ref, h_ref, rs_ref):
        xhat, rstd = _ln_stats(ALPHA * x_ref[...] + m_ref[...])
        xh_ref[...] = xhat
        h_ref[...] = (xhat * g_ref[0:1, :] + b_ref[0:1, :]).astype(_ACT)
        rs_ref[...] = rstd

    row = pl.BlockSpec((tm, D_MODEL), lambda i: (i, 0))
    par = pl.BlockSpec((SUB, D_MODEL), lambda i: (0, 0))
    return pl.pallas_call(
        body, name=name,
        out_shape=(jax.ShapeDtypeStruct((t, D_MODEL), F32), jax.ShapeDtypeStruct((t, D_MODEL), _ACT),
                   jax.ShapeDtypeStruct((t, 1), F32)),
        grid=(t // tm,), in_specs=[row, row, par, par],
        out_specs=(row, row, pl.BlockSpec((tm, 1), lambda i: (i, 0))),
        compiler_params=_cp(("parallel",)),
    )(x, mix, g8, b8)


def _ln2_loss(xhat1, ffn, target, g1_8, b1_8, g2_8, b2_8, name):
    t = xhat1.shape[0]
    tm = _tm_rows(t)

    def body(xh_ref, f_ref, t_ref, g1_ref, b1_ref, g2_ref, b2_ref, d_ref, db_ref, acc_ref):
        @pl.when(pl.program_id(0) == 0)
        def _():
            acc_ref[...] = jnp.zeros_like(acc_ref)

        h1 = xh_ref[...] * g1_ref[0:1, :] + b1_ref[0:1, :]
        xhat, rstd = _ln_stats(ALPHA * h1 + f_ref[...])
        diff = xhat * g2_ref[0:1, :] + b2_ref[0:1, :] - t_ref[...]
        dy = diff * (1.0 / D_MODEL)
        acc_ref[0:1, :] += jnp.sum(dy * xhat, axis=0, keepdims=True)
        acc_ref[1:2, :] += jnp.sum(dy, axis=0, keepdims=True)
        acc_ref[2:3, :] += jnp.sum(diff * diff, axis=0, keepdims=True)
        d = _ln_bwd(dy * g2_ref[0:1, :], xhat, rstd)
        d_ref[...] = d
        db_ref[...] = d.astype(_ACT)

    row = pl.BlockSpec((tm, D_MODEL), lambda i: (i, 0))
    par = pl.BlockSpec((SUB, D_MODEL), lambda i: (0, 0))
    return pl.pallas_call(
        body, name=name,
        out_shape=(jax.ShapeDtypeStruct((t, D_MODEL), F32), jax.ShapeDtypeStruct((t, D_MODEL), _ACT),
                   jax.ShapeDtypeStruct((SUB, D_MODEL), F32)),
        grid=(t // tm,), in_specs=[row, row, row, par, par, par, par],
        out_specs=(row, row, par), compiler_params=_cp(("arbitrary",)),
    )(xhat1, ffn, target, g1_8, b1_8, g2_8, b2_8)


def _ln1_bwd(d_r2, d_h1_ffn, xhat1, rstd1, g1_8, name):
    t = xhat1.shape[0]
    tm = _tm_rows(t)

    def body(d2_ref, df_ref, xh_ref, rs_ref, g_ref, d_ref, db_ref, acc_ref):
        @pl.when(pl.program_id(0) == 0)
        def _():
            acc_ref[...] = jnp.zeros_like(acc_ref)

        dh = ALPHA * d2_ref[...] + df_ref[...]
        xhat = xh_ref[...]
        acc_ref[0:1, :] += jnp.sum(dh * xhat, axis=0, keepdims=True)
        acc_ref[1:2, :] += jnp.sum(dh, axis=0, keepdims=True)
        d = _ln_bwd(dh * g_ref[0:1, :], xhat, rs_ref[...])
        d_ref[...] = d
        db_ref[...] = d.astype(_ACT)

    row = pl.BlockSpec((tm, D_MODEL), lambda i: (i, 0))
    par = pl.BlockSpec((SUB, D_MODEL), lambda i: (0, 0))
    return pl.pallas_call(
        body, name=name,
        out_shape=(jax.ShapeDtypeStruct((t, D_MODEL), F32), jax.ShapeDtypeStruct((t, D_MODEL), _ACT),
                   jax.ShapeDtypeStruct((SUB, D_MODEL), F32)),
        grid=(t // tm,), in_specs=[row, row, row, pl.BlockSpec((tm, 1), lambda i: (i, 0)), par],
        out_specs=(row, row, par), compiler_params=_cp(("arbitrary",)),
    )(d_r2, d_h1_ffn, xhat1, rstd1, g1_8)


def _ffn_tm(t):
    return min(128, t)


def _ffn_act_fwd(u0, cw8, cb8, name):
    t = u0.shape[0]
    tm = _ffn_tm(t)

    def body(g_ref, gp_ref, v_ref, vp_ref, wg_ref, wv_ref, bg_ref, bv_ref, o_ref):
        i = pl.program_id(0)
        gprev = jnp.where(i > 0, gp_ref[SUB:HALO, :].astype(F32), 0.0)
        vprev = jnp.where(i > 0, vp_ref[SUB:HALO, :].astype(F32), 0.0)
        gate = _conv_pre(g_ref[...].astype(F32), gprev, wg_ref, bg_ref[0:1, :], FFN_K)
        val = _conv_pre(v_ref[...].astype(F32), vprev, wv_ref, bv_ref[0:1, :], FFN_K)
        o_ref[...] = (gate * _sigmoid(gate) * val).astype(_ACT)

    cur = lambda col: pl.BlockSpec((tm, D_FF), lambda i: (i, col))
    prv = lambda col: _prev_halo(tm, D_FF, lambda i: (i, col))
    par = lambda col: pl.BlockSpec((SUB, D_FF), lambda i: (0, col))
    return pl.pallas_call(
        body, name=name, out_shape=jax.ShapeDtypeStruct((t, D_FF), _ACT), grid=(t // tm,),
        in_specs=[cur(0), prv(0), cur(1), prv(1), par(0), par(1), par(0), par(1)],
        out_specs=pl.BlockSpec((tm, D_FF), lambda i: (i, 0)), compiler_params=_cp(("parallel",)),
    )(u0, u0, u0, u0, cw8, cw8, cb8, cb8)


def _ffn_act_bwd(u0, cw8, cb8, d_a, name):
    t = u0.shape[0]
    tm = _ffn_tm(t)
    nt = t // tm

    def body(g_ref, gp_ref, gn_ref, v_ref, vp_ref, vn_ref, wg_ref, wv_ref, bg_ref, bv_ref, da_ref, dan_ref,
             du_ref, acc_ref):
        i = pl.program_id(0)

        @pl.when(i == 0)
        def _():
            acc_ref[...] = jnp.zeros_like(acc_ref)

        def grads(gcur, gprev, vcur, vprev, da):
            gate = _conv_pre(gcur, gprev, wg_ref, bg_ref[0:1, :], FFN_K)
            val = _conv_pre(vcur, vprev, wv_ref, bv_ref[0:1, :], FFN_K)
            return da * val * _silu_grad(gate), da * gate * _sigmoid(gate)

        gcur, vcur = g_ref[...].astype(F32), v_ref[...].astype(F32)
        gprev = jnp.where(i > 0, gp_ref[SUB:HALO, :].astype(F32), 0.0)
        vprev = jnp.where(i > 0, vp_ref[SUB:HALO, :].astype(F32), 0.0)
        dgate, dval = grads(gcur, gprev, vcur, vprev, da_ref[...].astype(F32))
        dgate_n, dval_n = grads(gn_ref[0:SUB, :].astype(F32), gcur[tm - SUB:], vn_ref[0:SUB, :].astype(F32),
                                vcur[tm - SUB:], dan_ref[0:SUB, :].astype(F32))
        last = i == nt - 1
        du_ref[:, 0:D_FF] = _conv_t(dgate, jnp.where(last, 0.0, dgate_n), wg_ref, FFN_K).astype(_ACT)
        du_ref[:, D_FF:2 * D_FF] = _conv_t(dval, jnp.where(last, 0.0, dval_n), wv_ref, FFN_K).astype(_ACT)
        for d, cur, prev, off in ((dgate, gcur, gprev, 0), (dval, vcur, vprev, D_FF)):
            acc_ref[FFN_K:FFN_K + 1, off:off + D_FF] += jnp.sum(d, axis=0, keepdims=True)
            acc_ref[FFN_K - 1:FFN_K, off:off + D_FF] += jnp.sum(d * cur, axis=0, keepdims=True)
            for s in range(1, FFN_K):
                acc_ref[FFN_K - 1 - s:FFN_K - s, off:off + D_FF] += jnp.sum(
                    d * _shift_down(cur, prev, s), axis=0, keepdims=True)

    cur = lambda col: pl.BlockSpec((tm, D_FF), lambda i: (i, col))
    prv = lambda col: _prev_halo(tm, D_FF, lambda i: (i, col))
    nxt = lambda col: _next_halo(tm, t, D_FF, lambda i: (i, col))
    par = lambda col: pl.BlockSpec((SUB, D_FF), lambda i: (0, col))
    return pl.pallas_call(
        body, name=name,
        out_shape=(jax.ShapeDtypeStruct((t, 2 * D_FF), _ACT), jax.ShapeDtypeStruct((SUB, 2 * D_FF), F32)),
        grid=(nt,),
        in_specs=[cur(0), prv(0), nxt(0), cur(1), prv(1), nxt(1), par(0), par(1), par(0), par(1),
                  cur(0), nxt(0)],
        out_specs=(pl.BlockSpec((tm, 2 * D_FF), lambda i: (i, 0)),
                   pl.BlockSpec((SUB, 2 * D_FF), lambda i: (0, 0))),
        compiler_params=_cp(("arbitrary",)),
    )(u0, u0, u0, u0, u0, u0, cw8, cw8, cb8, cb8, d_a, d_a)


def _pack_w_in(w_in):
    order = sorted(_PIECES, key=lambda p: p[2])
    cols = []
    at = 0
    for o, w, pk in order:
        if pk > at:
            cols.append(jnp.zeros((w_in.shape[0], pk - at), w_in.dtype))
        cols.append(w_in[:, o:o + w])
        at = pk + w
    if at < P_W:
        cols.append(jnp.zeros((w_in.shape[0], P_W - at), w_in.dtype))
    return jnp.concatenate(cols, axis=1)


def _unpack_w_in(wp):
    return jnp.concatenate([wp[:, pk:pk + w] for o, w, pk in sorted(_PIECES)], axis=1)


def _local_step(x, target, wts):
    t = x.shape[0]
    wp = _pack_w_in(wts["w_in"])
    w_bs, w_ba, w_mix, w_up, w_dn = (wts[k] for k in ("w_branch_ssm", "w_branch_attn", "w_mix_out", "w_up", "w_down"))
    scw = wts["ssm_conv_w"]
    scb = wts["ssm_conv_b"]
    fcw8 = _rows8(wts["ffn_conv_w"])
    fcb8 = _rows8(wts["ffn_conv_b"])
    pad_lane = lambda p: jnp.concatenate([p.astype(F32), jnp.zeros((1, LANE - p.shape[1]), F32)], axis=1)
    dtb8 = _rows8(pad_lane(wts["ssm_dt_bias"]))
    alog8 = _rows8(pad_lane(wts["ssm_a_log"]))
    dsk8 = _rows8(pad_lane(wts["ssm_d"]))
    sinks8 = _rows8(pad_lane(wts["attn_sinks"]))
    nw8 = _rows8(wts["ssm_norm_w"])
    bg8 = _rows8(wts["b_gate"])
    g1_8, b1_8, g2_8, b2_8 = (_rows8(wts[k]) for k in ("ln1_g", "ln1_b", "ln2_g", "ln2_b"))
    xs_w8, xs_b8 = _rows8(scw[:, :D_INNER]), _rows8(scb[:, :D_INNER])
    bc_w8, bc_b8 = _rows8(scw[:, D_INNER:]), _rows8(scb[:, D_INNER:])

    x_bf = x.astype(_ACT)
    proj = _mm(x_bf, wp[:, :P_MAIN], "mm_in", out_dtype=_ACT)
    tail = _mm(x_bf, wp[:, P_MAIN:], "mm_in_tail")
    xs_c = _conv_silu_fwd(proj, P_XS // _TC, D_INNER // _TC, xs_w8, xs_b8, "conv_xs_fwd")
    bc_c = _conv_silu_fwd(proj, P_BC // _TC, 1024 // _TC, bc_w8, bc_b8, "conv_bc_fwd")
    y_ssd, y_ssm, hprev = _ssd_fwd(xs_c, bc_c, proj, tail, dtb8, alog8, dsk8, nw8, "ssd_fwd")
    bias = _bias_expand(wts["rel_bias"].T.astype(F32), "bias_expand").reshape(A_HEADS, WIN, 2 * WIN)
    y_attn = _attn_fwd(proj, tail, bias, sinks8, "attn_fwd")
    bs = _mm(y_ssm, w_bs, "mm_bs", out_dtype=_ACT)
    ba = _mm(y_attn, w_ba, "mm_ba", out_dtype=_ACT)
    merged = _merge_fwd(bs, ba, proj, bg8, "merge_fwd")
    mix = _mm(merged, w_mix, "mm_mix", out_dtype=_ACT)
    xhat1, h1_bf, rstd1 = _ln1_fwd(x, mix, g1_8, b1_8, "ln1_fwd")
    u0 = _mm(h1_bf, w_up, "mm_up", out_dtype=_ACT)
    act = _ffn_act_fwd(u0, fcw8, fcb8, "ffn_act_fwd")
    ffn = _mm(act, w_dn, "mm_down", out_dtype=_ACT)
    d_r2, d_r2_bf, acc_ln2 = _ln2_loss(xhat1, ffn, target, g1_8, b1_8, g2_8, b2_8, "ln2_loss")
    d_w_dn = _mm(act, d_r2_bf, "mm_dw_down", trans_a=True)
    d_act = _mm(d_r2_bf, w_dn.T, "mm_d_act", out_dtype=_ACT)
    d_u0, acc_ffn = _ffn_act_bwd(u0, fcw8, fcb8, d_act, "ffn_act_bwd")
    d_w_up = _mm(h1_bf, d_u0, "mm_dw_up", trans_a=True)
    d_h1_ffn = _mm(d_u0, w_up.T, "mm_d_h1", out_dtype=_ACT)
    d_r1, d_r1_bf, acc_ln1 = _ln1_bwd(d_r2, d_h1_ffn, xhat1, rstd1, g1_8, "ln1_bwd")
    d_w_mix = _mm(merged, d_r1_bf, "mm_dw_mix", trans_a=True)
    d_merged = _mm(d_r1_bf, w_mix.T, "mm_d_merged", out_dtype=_ACT)
    d_bs, d_ba, d_gates, acc_bg = _merge_bwd(d_merged, bs, ba, proj, bg8, "merge_bwd")
    d_w_bs = _mm(y_ssm, d_bs, "mm_dw_bs", trans_a=True)
    d_w_ba = _mm(y_attn, d_ba, "mm_dw_ba", trans_a=True)
    d_y_ssm = _mm(d_bs, w_bs.T, "mm_d_yssm", out_dtype=_ACT)
    d_y_attn = _mm(d_ba, w_ba.T, "mm_d_yattn", out_dtype=_ACT)
    d_q, d_k, d_v, d_bias, d_sinks = _attn_bwd(proj, tail, bias, sinks8, d_y_attn, "attn_bwd")
    d_rel_t = _bias_reduce(d_bias.reshape(A_HEADS, WIN * 2 * WIN), "bias_reduce")
    d_xs_c, d_bc_c, d_z, d_dt, acc_ssd, acc_nw = _ssd_bwd(
        d_y_ssm, y_ssd, xs_c, bc_c, proj, tail, hprev, dtb8, alog8, dsk8, nw8, "ssd_bwd")
    d_xs, acc_xs = _conv_silu_bwd(proj, P_XS // _TC, D_INNER // _TC, xs_w8, xs_b8, d_xs_c, "conv_xs_bwd")
    d_bc, acc_bc = _conv_silu_bwd(proj, P_BC // _TC, 1024 // _TC, bc_w8, bc_b8, d_bc_c, "conv_bc_bwd")
    d_proj = jnp.concatenate([d_z, d_xs, d_gates, d_q, d_bc, d_k, d_v, d_dt,
                              jnp.zeros((t, P_W - P_DT - LANE), _ACT)], axis=1)
    d_wp = _mm(x_bf, d_proj, "mm_dw_in", trans_a=True)
    d_x = _mm(d_proj, wp.T, "mm_d_x", res=d_r1, res_scale=ALPHA)

    grads = {
        "w_in": _unpack_w_in(d_wp),
        "ssm_conv_w": jnp.concatenate([acc_xs[0:SSM_K], acc_bc[0:SSM_K]], axis=1),
        "w_branch_ssm": d_w_bs, "w_branch_attn": d_w_ba, "w_mix_out": d_w_mix,
        "w_up": d_w_up, "ffn_conv_w": acc_ffn[0:FFN_K], "w_down": d_w_dn,
    }
    small = {
        "rel_bias": d_rel_t.T,
        "b_gate": acc_bg[0:1],
        "ssm_conv_b": jnp.concatenate([acc_xs[SSM_K:SSM_K + 1], acc_bc[SSM_K:SSM_K + 1]], axis=1),
        "ssm_dt_bias": acc_ssd[0:1, 0:N_HEADS], "ssm_a_log": acc_ssd[1:2, 0:N_HEADS], "ssm_d": acc_ssd[2:3, 0:N_HEADS],
        "ssm_norm_w": acc_nw[0:1],
        "attn_sinks": d_sinks[0:1, 0:A_HEADS],
        "ln1_g": acc_ln1[0:1], "ln1_b": acc_ln1[1:2],
        "ffn_conv_b": acc_ffn[FFN_K:FFN_K + 1],
        "ln2_g": acc_ln2[0:1], "ln2_b": acc_ln2[1:2],
        "loss_lanes": acc_ln2[2:3],
    }
    return d_x, grads, small


_BIG = (("w_in", (1024, 2120), 1), ("ssm_conv_w", (4, 768), 1), ("w_branch_ssm", (512, 1024), 0),
        ("w_branch_attn", (256, 1024), 0), ("w_mix_out", (256, 1024), 0), ("w_up", (1024, 1408), 1),
        ("ffn_conv_w", (3, 1408), 1), ("w_down", (704, 1024), 0))
_ROW_ALIGN = 16
_BLK_ROWS = 1024


def _piece_rows(shape):
    rows = -(-(shape[0] * shape[1]) // LANE)
    return -(-rows // _ROW_ALIGN) * _ROW_ALIGN


_BIG_ROWS = -(-sum(_piece_rows(s) for _, s, _ in _BIG) // (2 * _BLK_ROWS)) * (2 * _BLK_ROWS)
_HALF_ROWS = _BIG_ROWS // 2
_F32_CONV = (("ssm_conv_w", (4, 768)), ("ffn_conv_w", (3, 1408)))
_AG_ROWS = _BIG_ROWS + 2 * _BLK_ROWS
_AG_HALF = _AG_ROWS // 2

_SMALL = (("rel_bias", (32, 16)), ("b_gate", (1, 2048)), ("ssm_conv_b", (1, 3072)), ("ssm_dt_bias", (1, 32)),
          ("ssm_a_log", (1, 32)), ("ssm_d", (1, 32)), ("ssm_norm_w", (1, 2048)), ("attn_sinks", (1, 16)),
          ("ln1_g", (1, 1024)), ("ln1_b", (1, 1024)), ("ffn_conv_b", (1, 5632)), ("ln2_g", (1, 1024)),
          ("ln2_b", (1, 1024)))
_LOSS_ROWS = SUB


def _small_rows(shape):
    rows = -(-(shape[0] * shape[1]) // LANE)
    return -(-rows // SUB) * SUB


_SMALL_ROWS = sum(_small_rows(s) for _, s in _SMALL) + _LOSS_ROWS


def _as_rows(a, rows, dtype):
    flat = a.reshape(-1).astype(dtype)
    flat = jnp.concatenate([flat, jnp.zeros((rows * LANE - flat.shape[0],), dtype)])
    return flat.reshape(rows, LANE)


def _pack_big(parts, dtype):
    blocks = [_as_rows(parts[n], _piece_rows(s), dtype) for n, s, _ in _BIG]
    used = sum(b.shape[0] for b in blocks)
    blocks.append(jnp.zeros((_BIG_ROWS - used, LANE), dtype))
    return jnp.concatenate(blocks, axis=0)


def _unpack_big(packed):
    out, at = {}, 0
    for n, s, _ in _BIG:
        rows = _piece_rows(s)
        out[n] = packed[at:at + rows].reshape(-1)[:s[0] * s[1]].reshape(s)
        at += rows
    return out


def _pack_small(parts, extra):
    blocks = [_as_rows(parts[n], _small_rows(s), F32) for n, s in _SMALL]
    blocks.append(_as_rows(extra, _LOSS_ROWS, F32))
    return jnp.concatenate(blocks, axis=0)


def _unpack_small(packed):
    out, at = {}, 0
    for n, s in _SMALL:
        rows = _small_rows(s)
        out[n] = packed[at:at + rows].reshape(-1)[:s[0] * s[1]].reshape(s)
        at += rows
    return out, packed[at:at + _LOSS_ROWS]


def _shard_of(full, shape, axis, j):
    return lax.slice_in_dim(full, j * shape[axis], (j + 1) * shape[axis], axis=axis)


_MESH = pl.DeviceIdType.MESH
_HBM = pl.BlockSpec(memory_space=pltpu.HBM)


def _position():
    return lax.axis_index("x"), lax.axis_index("y"), lax.axis_index("c")


def _other_chips(x, y):
    return ((1 - x, y), (x, 1 - y), (1 - x, 1 - y))


def _allgather_weights(shard):
    _, hr, _ = shard.shape

    def body(s_ref, o_ref, send_sems, recv_sems, local_sem):
        x, y, c = _position()
        me = 2 * x + y
        sib = (x, y, 1 - c)
        chips = _other_chips(x, y)
        mine = pltpu.make_async_copy(s_ref, o_ref.at[me], local_sem)
        mine.start()

        def copy(k, chip_idx, half, to, src=None):
            dst = o_ref.at[chip_idx, half]
            return pltpu.make_async_remote_copy(src_ref=dst if src is None else src, dst_ref=dst,
                                                send_sem=send_sems.at[k], recv_sem=recv_sems.at[k],
                                                device_id=to, device_id_type=_MESH)

        first = [copy(i, me, c, (cx, cy, c), src=s_ref.at[c]) for i, (cx, cy) in enumerate(chips)]
        for cp in first:
            cp.start()
        passed = [copy(3 + i, 2 * cx + cy, c, sib) for i, (cx, cy) in enumerate(chips)]
        for i, (cx, cy) in enumerate(chips):
            copy(i, 2 * cx + cy, c, sib).wait_recv()
            passed[i].start()
        for i, (cx, cy) in enumerate(chips):
            copy(3 + i, 2 * cx + cy, 1 - c, sib).wait_recv()
        for cp in first + passed:
            cp.wait_send()
        mine.wait()

    return pl.pallas_call(
        body, name="allgather_weights",
        out_shape=jax.ShapeDtypeStruct((N_CHIPS, 2, hr, LANE), shard.dtype),
        in_specs=[_HBM], out_specs=_HBM,
        scratch_shapes=[pltpu.SemaphoreType.DMA((6,)), pltpu.SemaphoreType.DMA((6,)), pltpu.SemaphoreType.DMA],
    )(shard)


def _swap_halves(g):
    nseg, _, hr, _ = g.shape

    def body(g_ref, o_ref, send_sems, recv_sems):
        x, y, c = _position()
        cps = [pltpu.make_async_remote_copy(src_ref=g_ref.at[j, 1 - c], dst_ref=o_ref.at[j],
                                            send_sem=send_sems.at[j], recv_sem=recv_sems.at[j],
                                            device_id=(x, y, 1 - c), device_id_type=_MESH) for j in range(nseg)]
        for cp in cps:
            cp.start()
        for cp in cps:
            cp.wait()

    return pl.pallas_call(
        body, name="swap_halves", out_shape=jax.ShapeDtypeStruct((nseg, hr, LANE), g.dtype),
        in_specs=[_HBM], out_specs=_HBM,
        scratch_shapes=[pltpu.SemaphoreType.DMA((nseg,)), pltpu.SemaphoreType.DMA((nseg,))],
    )(g)


def _scatter_chips(p):
    _, hr, _ = p.shape

    def body(p_ref, o_ref, send_sems, recv_sems):
        x, y, c = _position()
        cps = [pltpu.make_async_remote_copy(src_ref=p_ref.at[2 * cx + cy], dst_ref=o_ref.at[i],
                                            send_sem=send_sems.at[i], recv_sem=recv_sems.at[i],
                                            device_id=(cx, cy, c), device_id_type=_MESH)
               for i, (cx, cy) in enumerate(_other_chips(x, y))]
        for cp in cps:
            cp.start()
        for cp in cps:
            cp.wait()

    return pl.pallas_call(
        body, name="scatter_chips", out_shape=jax.ShapeDtypeStruct((N_CHIPS - 1, hr, LANE), p.dtype),
        in_specs=[_HBM], out_specs=_HBM,
        scratch_shapes=[pltpu.SemaphoreType.DMA((N_CHIPS - 1,)), pltpu.SemaphoreType.DMA((N_CHIPS - 1,))],
    )(p)


def _join_halves(red):
    hr, _ = red.shape

    def body(r_ref, o_ref, send_sem, recv_sem, local_sem):
        x, y, c = _position()
        mine = pltpu.make_async_copy(r_ref, o_ref.at[c], local_sem)
        mine.start()
        cp = pltpu.make_async_remote_copy(src_ref=r_ref, dst_ref=o_ref.at[c], send_sem=send_sem, recv_sem=recv_sem,
                                          device_id=(x, y, 1 - c), device_id_type=_MESH)
        cp.start()
        cp.wait()
        mine.wait()

    return pl.pallas_call(
        body, name="join_halves", out_shape=jax.ShapeDtypeStruct((2, hr, LANE), red.dtype),
        in_specs=[_HBM], out_specs=_HBM,
        scratch_shapes=[pltpu.SemaphoreType.DMA, pltpu.SemaphoreType.DMA, pltpu.SemaphoreType.DMA],
    )(red)


def _allgather_small(mine):
    m_per, n = mine.shape

    def body(x_ref, out_ref, send_sems, recv_sems, local_sem):
        x, y, c = _position()
        me, sibling = (x, y, c), (x, y, 1 - c)
        chips = _other_chips(x, y)

        def rows(px, py, pc):
            return out_ref.at[pl.ds((4 * px + 2 * py + pc) * m_per, m_per), :]

        def copy(k, block, to, src=None):
            return pltpu.make_async_remote_copy(src_ref=rows(*block) if src is None else src, dst_ref=rows(*block),
                                                send_sem=send_sems.at[k], recv_sem=recv_sems.at[k],
                                                device_id=to, device_id_type=_MESH)

        own = pltpu.make_async_copy(x_ref, rows(*me), local_sem)
        own.start()
        first = [copy(0, me, sibling, src=x_ref)]
        first += [copy(1 + j, me, (*chip, c), src=x_ref) for j, chip in enumerate(chips)]
        for cp in first:
            cp.start()
        passed = [copy(4 + j, (*chip, c), sibling) for j, chip in enumerate(chips)]
        for j, chip in enumerate(chips):
            copy(1 + j, (*chip, c), me).wait_recv()
            passed[j].start()
        copy(0, sibling, me).wait_recv()
        for j, chip in enumerate(chips):
            copy(4 + j, (*chip, 1 - c), me).wait_recv()
        for cp in first + passed:
            cp.wait_send()
        own.wait()

    return pl.pallas_call(
        body, name="allgather_small", out_shape=jax.ShapeDtypeStruct((N_DEV * m_per, n), mine.dtype),
        in_specs=[pl.BlockSpec(memory_space=pltpu.VMEM)], out_specs=pl.BlockSpec(memory_space=pltpu.VMEM),
        scratch_shapes=[pltpu.SemaphoreType.DMA((7,)), pltpu.SemaphoreType.DMA((7,)), pltpu.SemaphoreType.DMA],
    )(mine)


def _add_own_half(g, recv, c_idx):
    nseg, _, hr, _ = g.shape

    def body(c_ref, g_ref, r_ref, o_ref, ob_ref):
        s = g_ref[...] + r_ref[...]
        o_ref[...] = s
        ob_ref[...] = s.astype(jnp.bfloat16)

    blk = pl.BlockSpec((None, _BLK_ROWS, LANE), lambda j, i, c_ref: (j, i, 0))
    return pl.pallas_call(
        body, name="add_own_half",
        out_shape=(jax.ShapeDtypeStruct((nseg, hr, LANE), F32), jax.ShapeDtypeStruct((nseg, hr, LANE), jnp.bfloat16)),
        grid_spec=pltpu.PrefetchScalarGridSpec(
            num_scalar_prefetch=1, grid=(nseg, hr // _BLK_ROWS),
            in_specs=[pl.BlockSpec((None, None, _BLK_ROWS, LANE), lambda j, i, c_ref: (j, c_ref[0], i, 0)), blk],
            out_specs=(blk, blk)),
        compiler_params=_cp(("parallel", "parallel")),
    )(c_idx, g, recv)


def _add_chips(p, recv, chip_idx):
    _, hr, _ = p.shape

    def body(j_ref, p_ref, r_ref, o_ref):
        o_ref[...] = ((p_ref[...] + r_ref[0].astype(F32)) + r_ref[1].astype(F32)) + r_ref[2].astype(F32)

    return pl.pallas_call(
        body, name="add_chips", out_shape=jax.ShapeDtypeStruct((hr, LANE), F32),
        grid_spec=pltpu.PrefetchScalarGridSpec(
            num_scalar_prefetch=1, grid=(hr // _BLK_ROWS,),
            in_specs=[pl.BlockSpec((None, _BLK_ROWS, LANE), lambda i, j_ref: (j_ref[0], i, 0)),
                      pl.BlockSpec((N_CHIPS - 1, _BLK_ROWS, LANE), lambda i, j_ref: (0, i, 0))],
            out_specs=pl.BlockSpec((_BLK_ROWS, LANE), lambda i, j_ref: (i, 0))),
        compiler_params=_cp(("parallel",)),
    )(chip_idx, p, recv)


def _adam_math(w, g, m, v):
    m = ADAM_B1 * m + (1.0 - ADAM_B1) * g
    v = ADAM_B2 * v + (1.0 - ADAM_B2) * (g * g)
    m_hat = m / (1.0 - ADAM_B1 ** ADAM_STEP)
    v_hat = v / (1.0 - ADAM_B2 ** ADAM_STEP)
    delta = -ADAM_LR * (m_hat / (jnp.sqrt(v_hat) + ADAM_EPS) + ADAM_WD * w)
    return delta, m, v


def _adam_big(w, g, m, v):
    rows = w.shape[0]

    def body(w_ref, g_ref, m_ref, v_ref, d_ref, mo_ref, vo_ref):
        d_ref[...], mo_ref[...], vo_ref[...] = _adam_math(w_ref[...], g_ref[...], m_ref[...], v_ref[...])

    blk = pl.BlockSpec((_BLK_ROWS, LANE), lambda i: (i, 0))
    shp = jax.ShapeDtypeStruct((rows, LANE), F32)
    return pl.pallas_call(
        body, name="adam_big", out_shape=(shp, shp, shp), grid=(rows // _BLK_ROWS,),
        in_specs=[blk, blk, blk, blk], out_specs=(blk, blk, blk), compiler_params=_cp(("parallel",)),
    )(w, g, m, v)


def _adam_small(w, gathered, m, v):
    rows = w.shape[0]

    def body(w_ref, a_ref, m_ref, v_ref, g_ref, d_ref, mo_ref, vo_ref):
        g = a_ref[0:rows, :]
        for k in range(1, N_DEV):
            g = g + a_ref[k * rows:(k + 1) * rows, :]
        g_ref[...] = g
        d_ref[...], mo_ref[...], vo_ref[...] = _adam_math(w_ref[...], g, m_ref[...], v_ref[...])

    shp = jax.ShapeDtypeStruct((rows, LANE), F32)
    return pl.pallas_call(body, name="adam_small", out_shape=(shp, shp, shp, shp), compiler_params=_cp(None))(
        w, gathered, m, v)


_WEIGHTS = ("rel_bias", "w_in", "b_gate", "ssm_conv_w", "ssm_conv_b", "ssm_dt_bias", "ssm_a_log", "ssm_d",
            "ssm_norm_w", "attn_sinks", "w_branch_ssm", "w_branch_attn", "w_mix_out", "ln1_g", "ln1_b", "w_up",
            "ffn_conv_w", "ffn_conv_b", "w_down", "ln2_g", "ln2_b")
_BIG_NAMES = tuple(n for n, _, _ in _BIG)


def _step(x, target, w, m, v):
    xi, yi, ci = _position()
    chip = 2 * xi + yi

    f32_rows = jnp.concatenate(
        [_as_rows(lax.bitcast_convert_type(w[n].astype(F32), jnp.bfloat16), 2 * _piece_rows(s), jnp.bfloat16)
         for n, s in _F32_CONV], axis=0)
    f32_rows = jnp.concatenate(
        [f32_rows, jnp.zeros((_AG_ROWS - _BIG_ROWS - f32_rows.shape[0], LANE), jnp.bfloat16)], axis=0)
    shard = jnp.concatenate([_pack_big(w, jnp.bfloat16), f32_rows], axis=0).reshape(2, _AG_HALF, LANE)
    gathered = _allgather_weights(shard).reshape(N_CHIPS, _AG_ROWS, LANE)
    per_chip = [_unpack_big(gathered[j, :_BIG_ROWS]) for j in range(N_CHIPS)]
    full = {n: jnp.concatenate([per_chip[j][n] for j in range(N_CHIPS)], axis=ax) for n, _, ax in _BIG}
    at = _BIG_ROWS
    for n, s in _F32_CONV:
        rows = 2 * _piece_rows(s)
        pieces = [lax.bitcast_convert_type(gathered[j, at:at + rows].reshape(-1, 2), F32)[:s[0] * s[1]].reshape(s)
                  for j in range(N_CHIPS)]
        full[n] = jnp.concatenate(pieces, axis=1)
        at += rows
    wts = {n: (full[n] if n in full else w[n]) for n in _WEIGHTS}

    d_x, grads, small = _local_step(x, target, wts)

    g = jnp.stack([_pack_big({n: _shard_of(grads[n], s, ax, j) for n, s, ax in _BIG}, F32)
                   for j in range(N_CHIPS)]).reshape(N_CHIPS, 2, _HALF_ROWS, LANE)
    c_idx = jnp.reshape(ci, (1,)).astype(jnp.int32)
    chip_sum, chip_sum_bf = _add_own_half(g, _swap_halves(g), c_idx)
    red = _add_chips(chip_sum, _scatter_chips(chip_sum_bf), jnp.reshape(chip, (1,)).astype(jnp.int32))
    g_big = _join_halves(red).reshape(_BIG_ROWS, LANE)
    d_big, m_big, v_big = _adam_big(_pack_big(w, F32), g_big, _pack_big(m, F32), _pack_big(v, F32))
    outs = {"grad": _unpack_big(g_big), "delta": _unpack_big(d_big), "m": _unpack_big(m_big), "v": _unpack_big(v_big)}

    all_small = _allgather_small(_pack_small(small, small["loss_lanes"]))
    g_s, d_s, m_s, v_s = _adam_small(_pack_small(w, jnp.zeros((1, LANE), F32)), all_small,
                                     _pack_small(m, jnp.zeros((1, LANE), F32)),
                                     _pack_small(v, jnp.zeros((1, LANE), F32)))
    (gs, loss_rows), (ds, _), (ms, _), (vs, _) = (_unpack_small(a) for a in (g_s, d_s, m_s, v_s))
    for kind, part in (("grad", gs), ("delta", ds), ("m", ms), ("v", vs)):
        outs[kind].update(part)
    loss = (0.5 / D_MODEL) * jnp.sum(loss_rows)
    return loss, d_x, outs


def kernel(x, rel_bias, w_in, b_gate, ssm_conv_w, ssm_conv_b, ssm_dt_bias, ssm_a_log, ssm_d, ssm_norm_w, attn_sinks, w_branch_ssm, w_branch_attn, w_mix_out, ln1_g, ln1_b, w_up, ffn_conv_w, ffn_conv_b, w_down, ln2_g, ln2_b, loss_target, m_rel_bias, m_w_in, m_b_gate, m_ssm_conv_w, m_ssm_conv_b, m_ssm_dt_bias, m_ssm_a_log, m_ssm_d, m_ssm_norm_w, m_attn_sinks, m_w_branch_ssm, m_w_branch_attn, m_w_mix_out, m_ln1_g, m_ln1_b, m_w_up, m_ffn_conv_w, m_ffn_conv_b, m_w_down, m_ln2_g, m_ln2_b, v_rel_bias, v_w_in, v_b_gate, v_ssm_conv_w, v_ssm_conv_b, v_ssm_dt_bias, v_ssm_a_log, v_ssm_d, v_ssm_norm_w, v_attn_sinks, v_w_branch_ssm, v_w_branch_attn, v_w_mix_out, v_ln1_g, v_ln1_b, v_w_up, v_ffn_conv_w, v_ffn_conv_b, v_w_down, v_ln2_g, v_ln2_b):
    given = dict(locals())
    drop = lambda a, n: a if n == "rel_bias" or a.ndim == 2 else a[0]
    w = {n: drop(given[n], n) for n in _WEIGHTS}
    m = {n: drop(given["m_" + n], n) for n in _WEIGHTS}
    v = {n: drop(given["v_" + n], n) for n in _WEIGHTS}
    loss, d_x, outs = _step(x[0], loss_target[0], w, m, v)
    like = lambda a, n: a.reshape(given[n].shape)
    res = [loss, d_x[None]]
    for kind in ("grad", "delta", "m", "v"):
        res += [like(outs[kind][n], n) for n in _WEIGHTS]
    return tuple(res)
```

```python
import functools
import math

import numpy as np
import jax
import jax.numpy as jnp
from jax import lax
from jax.experimental import pallas as pl
from jax.experimental.pallas import tpu as pltpu

F32 = jnp.float32
_ACT = jnp.bfloat16
_MXU = jnp.bfloat16

D_MODEL = 1024
D_INNER = 2048
N_HEADS = 32
HEAD_P = 64
N_GROUPS = 4
N_STATE = 128
CHUNK = 128
CONV_DIM = 3072
SSM_K = 4
A_HEADS = 16
A_DH = 64
WIN = 128
REL_BUCKETS = 32
BIAS_ROWS = 64
D_FF = 2816
FFN_K = 3
ALPHA = 2.0 ** 0.25
LN_EPS = 1e-5
RMS_EPS = 1e-5
IN_COLS = 8480
NEG = -1e30

ADAM_LR = 0.001
ADAM_B1 = 0.9
ADAM_B2 = 0.999
ADAM_EPS = 1e-08
ADAM_WD = 0.01
ADAM_STEP = 10

LANE = 128
SUB = 8

P_Z, P_XS, P_G, P_Q, P_BC, P_K, P_V, P_DT = 0, 2048, 4096, 6144, 7168, 8192, 8320, 8448
P_W = 8704
P_MAIN = 8192
T_K, T_V, T_DT = P_K - P_MAIN, P_V - P_MAIN, P_DT - P_MAIN
_PIECES = ((0, 2048, P_Z), (2048, 2048, P_XS), (4096, 1024, P_BC), (5120, 32, P_DT), (5152, 1024, P_Q),
           (6176, 128, P_K), (6304, 128, P_V), (6432, 2048, P_G))

N_CHIPS = 4
N_DEV = 8


def _cp(sem=None, vmem_mb=48):
    return pltpu.CompilerParams(dimension_semantics=sem, vmem_limit_bytes=vmem_mb * 1024 * 1024)


def _pick(n, cands):
    for c in cands:
        if n % c == 0:
            return c
    raise ValueError(f"no block size for {n}")


def _rows8(p):
    k, c = p.shape
    return jnp.concatenate([p.astype(F32), jnp.zeros((SUB - k, c), F32)], axis=0)


def _mm(a, b, name, *, trans_a=False, out_dtype=F32, res=None, res_scale=1.0):
    if trans_a:
        k_dim, m = a.shape
    else:
        m, k_dim = a.shape
    k2, n = b.shape
    assert k_dim == k2, (a.shape, b.shape)
    tm = _pick(m, (1024, 512, 256, 128))
    tn = _pick(n, (1024, 512, 256, 128))
    tk = _pick(k_dim, (2816, 2176, 2048, 1024, 512, 256, 128))
    nk = k_dim // tk
    dn = (((0,), (0,)), ((), ())) if trans_a else (((1,), (0,)), ((), ()))

    def body(*refs):
        if res is None:
            a_ref, b_ref, o_ref = refs[:3]
        else:
            a_ref, b_ref, r_ref, o_ref = refs[:4]

        def finish(r):
            if res is not None:
                r = r + res_scale * r_ref[...]
            o_ref[...] = r.astype(out_dtype)

        part = lax.dot_general(a_ref[...].astype(_MXU), b_ref[...].astype(_MXU), dn, preferred_element_type=F32)
        if nk == 1:
            finish(part)
            return
        acc = refs[-1]
        k = pl.program_id(2)

        @pl.when(k == 0)
        def _():
            acc[...] = part

        @pl.when(k > 0)
        def _():
            acc[...] += part

        @pl.when(k == nk - 1)
        def _():
            finish(acc[...])

    if trans_a:
        a_spec = pl.BlockSpec((tk, tm), lambda i, j, k: (k, i))
    else:
        a_spec = pl.BlockSpec((tm, tk), lambda i, j, k: (i, k))
    in_specs = [a_spec, pl.BlockSpec((tk, tn), lambda i, j, k: (k, j))]
    args = [a, b]
    if res is not None:
        in_specs.append(pl.BlockSpec((tm, tn), lambda i, j, k: (i, j)))
        args.append(res)
    return pl.pallas_call(
        body, name=name, out_shape=jax.ShapeDtypeStruct((m, n), out_dtype),
        grid=(m // tm, n // tn, nk), in_specs=in_specs,
        out_specs=pl.BlockSpec((tm, tn), lambda i, j, k: (i, j)),
        scratch_shapes=[pltpu.VMEM((tm, tn), F32)] if nk > 1 else [],
        compiler_params=_cp(("parallel", "parallel", "arbitrary")),
    )(*args)


def _shift_down(cur, prev8, s):
    r = pltpu.roll(cur, s, 0)
    p = pltpu.roll(prev8, s, 0)
    row8 = lax.broadcasted_iota(jnp.int32, (SUB, 1), 0)
    fixed = jnp.where(row8 < s, p, r[0:SUB])
    if cur.shape[0] == SUB:
        return fixed
    return jnp.concatenate([fixed, r[SUB:]], axis=0)


def _shift_up(cur, next8, s):
    tm = cur.shape[0]
    r = pltpu.roll(cur, tm - s, 0)
    p = pltpu.roll(next8, SUB - s, 0)
    row8 = lax.broadcasted_iota(jnp.int32, (SUB, 1), 0)
    fixed = jnp.where(row8 >= SUB - s, p, r[tm - SUB:])
    return jnp.concatenate([r[:tm - SUB], fixed], axis=0)


def _conv_pre(cur, prev8, w_ref, b_row, taps):
    acc = cur * w_ref[taps - 1:taps, :] + b_row
    for s in range(1, taps):
        acc = acc + _shift_down(cur, prev8, s) * w_ref[taps - 1 - s:taps - s, :]
    return acc


def _dot01_r(x, m01, parts=3):
    acc = None
    r = x
    for _ in range(parts):
        hi = r.astype(jnp.bfloat16)
        t = jnp.dot(hi, m01, preferred_element_type=F32)
        acc = t if acc is None else acc + t
        r = r - hi.astype(F32)
    return acc


def _dot01_l(m01, x, parts=3):
    acc = None
    r = x
    for _ in range(parts):
        hi = r.astype(jnp.bfloat16)
        t = jnp.dot(m01, hi, preferred_element_type=F32)
        acc = t if acc is None else acc + t
        r = r - hi.astype(F32)
    return acc


def _dot(a, b):
    return jnp.dot(a.astype(_MXU), b.astype(_MXU), preferred_element_type=F32)


def _dot_nt(a, b):
    return lax.dot_general(a.astype(_MXU), b.astype(_MXU), (((1,), (1,)), ((), ())), preferred_element_type=F32)


def _dot_tn(a, b):
    return lax.dot_general(a.astype(_MXU), b.astype(_MXU), (((0,), (0,)), ((), ())), preferred_element_type=F32)


def _sigmoid(x):
    return 1.0 / (1.0 + jnp.exp(-x))


def _half_masks():
    lane = lax.broadcasted_iota(jnp.int32, (1, LANE), 1)
    lo = (lane < 64).astype(F32)
    return lo, 1.0 - lo


_TC = 512


def _tm_rows(t):
    return min(256, t)


HALO = 16


def _prev_halo(tm, width, pos):
    def index(*ids):
        i, col = pos(*ids)
        return (jnp.maximum(i * (tm // HALO) - 1, 0), col)
    return pl.BlockSpec((HALO, width), index)


def _next_halo(tm, t, width, pos):
    def index(*ids):
        i, col = pos(*ids)
        return (jnp.minimum((i + 1) * (tm // HALO), t // HALO - 1), col)
    return pl.BlockSpec((HALO, width), index)


def _conv_silu_fwd(proj, colblk0, nblk, w8, b8, name):
    t = proj.shape[0]
    tm = _tm_rows(t)

    def body(c_ref, p_ref, w_ref, b_ref, o_ref):
        i = pl.program_id(1)
        prev8 = jnp.where(i > 0, p_ref[SUB:HALO, :].astype(F32), 0.0)
        pre = _conv_pre(c_ref[...].astype(F32), prev8, w_ref, b_ref[0:1, :], SSM_K)
        o_ref[...] = pre * _sigmoid(pre)

    return pl.pallas_call(
        body, name=name, out_shape=jax.ShapeDtypeStruct((t, nblk * _TC), F32),
        grid=(nblk, t // tm),
        in_specs=[pl.BlockSpec((tm, _TC), lambda j, i: (i, colblk0 + j)),
                  _prev_halo(tm, _TC, lambda j, i: (i, colblk0 + j)),
                  pl.BlockSpec((SUB, _TC), lambda j, i: (0, j)),
                  pl.BlockSpec((SUB, _TC), lambda j, i: (0, j))],
        out_specs=pl.BlockSpec((tm, _TC), lambda j, i: (i, j)),
        compiler_params=_cp(("parallel", "parallel")),
    )(proj, proj, w8, b8)


def _silu_grad(pre):
    sg = _sigmoid(pre)
    return sg * (1.0 + pre * (1.0 - sg))


def _conv_t(d_cur, d_next8, w_ref, taps):
    acc = d_cur * w_ref[taps - 1:taps, :]
    for s in range(1, taps):
        acc = acc + _shift_up(d_cur, d_next8, s) * w_ref[taps - 1 - s:taps - s, :]
    return acc


def _conv_silu_bwd(proj, colblk0, nblk, w8, b8, d_out, name):
    t = proj.shape[0]
    tm = _tm_rows(t)
    nt = t // tm

    def body(c_ref, p_ref, n_ref, w_ref, b_ref, d_ref, dn_ref, du_ref, acc_ref):
        i = pl.program_id(1)

        @pl.when(i == 0)
        def _():
            acc_ref[...] = jnp.zeros_like(acc_ref)

        cur = c_ref[...].astype(F32)
        prev8 = jnp.where(i > 0, p_ref[SUB:HALO, :].astype(F32), 0.0)
        b_row = b_ref[0:1, :]
        dpre = d_ref[...].astype(F32) * _silu_grad(_conv_pre(cur, prev8, w_ref, b_row, SSM_K))
        pre_n = _conv_pre(n_ref[0:SUB, :].astype(F32), cur[tm - SUB:], w_ref, b_row, SSM_K)
        dpre_n = jnp.where(i < nt - 1, dn_ref[0:SUB, :].astype(F32) * _silu_grad(pre_n), 0.0)
        du_ref[...] = _conv_t(dpre, dpre_n, w_ref, SSM_K).astype(_ACT)
        acc_ref[SSM_K:SSM_K + 1, :] += jnp.sum(dpre, axis=0, keepdims=True)
        acc_ref[SSM_K - 1:SSM_K, :] += jnp.sum(dpre * cur, axis=0, keepdims=True)
        for s in range(1, SSM_K):
            acc_ref[SSM_K - 1 - s:SSM_K - s, :] += jnp.sum(dpre * _shift_down(cur, prev8, s), axis=0, keepdims=True)

    c = nblk * _TC
    return pl.pallas_call(
        body, name=name,
        out_shape=(jax.ShapeDtypeStruct((t, c), _ACT), jax.ShapeDtypeStruct((SUB, c), F32)),
        grid=(nblk, nt),
        in_specs=[pl.BlockSpec((tm, _TC), lambda j, i: (i, colblk0 + j)),
                  _prev_halo(tm, _TC, lambda j, i: (i, colblk0 + j)),
                  _next_halo(tm, t, _TC, lambda j, i: (i, colblk0 + j)),
                  pl.BlockSpec((SUB, _TC), lambda j, i: (0, j)),
                  pl.BlockSpec((SUB, _TC), lambda j, i: (0, j)),
                  pl.BlockSpec((tm, _TC), lambda j, i: (i, j)),
                  _next_halo(tm, t, _TC, lambda j, i: (i, j))],
        out_specs=(pl.BlockSpec((tm, _TC), lambda j, i: (i, j)),
                   pl.BlockSpec((SUB, _TC), lambda j, i: (0, j))),
        compiler_params=_cp(("parallel", "arbitrary")),
    )(proj, proj, proj, w8, b8, d_out, d_out)


def _expand_consts():
    e = np.zeros((LANE, D_INNER), np.float32)
    for h in range(N_HEADS):
        e[h, h * HEAD_P:(h + 1) * HEAD_P] = 1.0
    return jnp.asarray(e, jnp.bfloat16), jnp.asarray(e.T.copy(), jnp.bfloat16)


def _ssd_common(dtr_ref, dtb_ref, alog_ref, e_ref):
    lane = lax.broadcasted_iota(jnp.int32, (1, LANE), 1)
    hm = lane < N_HEADS
    pre = dtr_ref[...] + dtb_ref[0:1, :]
    dt = jnp.where(hm, jnp.maximum(pre, 0.0) + jnp.log(1.0 + jnp.exp(-jnp.abs(pre))), 0.0)
    a_row = jnp.where(hm, -jnp.exp(alog_ref[0:1, :]), 0.0)
    adt = dt * a_row
    r = lax.broadcasted_iota(jnp.int32, (CHUNK, CHUNK), 0)
    c = lax.broadcasted_iota(jnp.int32, (CHUNK, CHUNK), 1)
    causal = r >= c
    acs = _dot01_l(causal.astype(jnp.bfloat16), adt)
    e = e_ref[...]
    acs_x = _dot01_r(acs, e, parts=2)
    dt_x = _dot01_r(dt, e, parts=2)
    return pre, dt, a_row, acs, acs_x, dt_x, causal, hm


def _decay(acs, acs_t, h, causal):
    seg = acs[:, h:h + 1] - acs_t[h:h + 1, :]
    return jnp.exp(jnp.where(causal, seg, NEG))


def _ssd_fwd(xs_c, bc_c, proj, tail, dtb8, alog8, dsk8, nw8, name):
    t = xs_c.shape[0]
    nc = t // CHUNK
    e_bf, _ = _expand_consts()
    gw = D_INNER // N_GROUPS

    def body(xs_ref, bc_ref, dtr_ref, z_ref, dtb_ref, alog_ref, dsk_ref, nw_ref, e_ref,
             y_ref, ys_ref, hp_ref, h_ref):
        c_id = pl.program_id(0)

        @pl.when(c_id == 0)
        def _():
            h_ref[...] = jnp.zeros_like(h_ref)

        _, dt, a_row, acs, acs_x, dt_x, causal, _ = _ssd_common(dtr_ref, dtb_ref, alog_ref, e_ref)
        acs_t = acs.T
        xs = xs_ref[...]
        x_dt = xs * dt_x
        last_x = acs_x[CHUNK - 1:CHUNK, :]
        w_end = jnp.exp(last_x - acs_x)
        e_in = jnp.exp(acs_x)
        d_x = _dot01_r(dsk_ref[...], e_ref[...])[0:1, :]
        hprev = h_ref[...]
        hp_ref[...] = hprev
        lo, hi = _half_masks()
        for g in range(N_GROUPS):
            bg = bc_ref[:, g * N_STATE:(g + 1) * N_STATE]
            cg = bc_ref[:, N_GROUPS * N_STATE + g * N_STATE:N_GROUPS * N_STATE + (g + 1) * N_STATE]
            sl = slice(g * gw, (g + 1) * gw)
            gm = _dot_nt(cg, bg)
            st = _dot(bg.T, x_dt[:, sl] * w_end[:, sl])
            y_off = _dot(cg, hprev[:, sl]) * e_in[:, sl]
            for j in range(gw // LANE):
                h0 = g * (gw // HEAD_P) + 2 * j
                cs = slice(g * gw + j * LANE, g * gw + (j + 1) * LANE)
                xp = x_dt[:, cs]
                m0 = gm * _decay(acs, acs_t, h0, causal)
                m1 = gm * _decay(acs, acs_t, h0 + 1, causal)
                yd = _dot(m0, xp * lo) + _dot(m1, xp * hi)
                y_ref[:, cs] = yd + y_off[:, j * LANE:(j + 1) * LANE] + xs[:, cs] * d_x[:, cs]
            h_ref[:, sl] = hprev[:, sl] * jnp.exp(last_x[:, sl]) + st
        y = y_ref[...]
        z = z_ref[...].astype(F32)
        y2 = y * (z * _sigmoid(z))
        for g in range(N_GROUPS):
            sl = slice(g * gw, (g + 1) * gw)
            yg = y2[:, sl]
            rinv = lax.rsqrt(jnp.mean(yg * yg, axis=-1, keepdims=True) + RMS_EPS)
            ys_ref[:, sl] = (yg * rinv * nw_ref[0:1, sl]).astype(_ACT)

    small = pl.BlockSpec((SUB, LANE), lambda c: (0, 0))
    return pl.pallas_call(
        body, name=name,
        out_shape=(jax.ShapeDtypeStruct((t, D_INNER), F32), jax.ShapeDtypeStruct((t, D_INNER), _ACT),
                   jax.ShapeDtypeStruct((t, D_INNER), F32)),
        grid=(nc,),
        in_specs=[pl.BlockSpec((CHUNK, D_INNER), lambda c: (c, 0)),
                  pl.BlockSpec((CHUNK, 1024), lambda c: (c, 0)),
                  pl.BlockSpec((CHUNK, LANE), lambda c: (c, T_DT // LANE)),
                  pl.BlockSpec((CHUNK, D_INNER), lambda c: (c, P_Z // D_INNER)),
                  small, small, small,
                  pl.BlockSpec((SUB, D_INNER), lambda c: (0, 0)),
                  pl.BlockSpec((LANE, D_INNER), lambda c: (0, 0))],
        out_specs=(pl.BlockSpec((CHUNK, D_INNER), lambda c: (c, 0)),
                   pl.BlockSpec((CHUNK, D_INNER), lambda c: (c, 0)),
                   pl.BlockSpec((N_STATE, D_INNER), lambda c: (c, 0))),
        scratch_shapes=[pltpu.VMEM((N_STATE, D_INNER), F32)],
        compiler_params=_cp(("arbitrary",)),
    )(xs_c, bc_c, tail, proj, dtb8, alog8, dsk8, nw8, e_bf)


def _ssd_bwd(d_ys, y, xs_c, bc_c, proj, tail, hprev_all, dtb8, alog8, dsk8, nw8, name):
    t = xs_c.shape[0]
    nc = t // CHUNK
    e_bf, et_bf = _expand_consts()
    gw = D_INNER // N_GROUPS

    def body(dys_ref, y_ref, xs_ref, bc_ref, dtr_ref, z_ref, hp_ref, dtb_ref, alog_ref, dsk_ref, nw_ref,
             e_ref, et_ref, dxs_ref, dbc_ref, dz_ref, ddt_ref, acc_ref, dnw_ref, dh_ref, dx_ref):
        step = pl.program_id(0)

        @pl.when(step == 0)
        def _():
            dh_ref[...] = jnp.zeros_like(dh_ref)
            acc_ref[...] = jnp.zeros_like(acc_ref)
            dnw_ref[...] = jnp.zeros_like(dnw_ref)

        pre, dt, a_row, acs, acs_x, dt_x, causal, hm = _ssd_common(dtr_ref, dtb_ref, alog_ref, e_ref)
        acs_t = acs.T
        et = et_ref[...]
        xs = xs_ref[...]
        x_dt = xs * dt_x
        last_x = acs_x[CHUNK - 1:CHUNK, :]
        w_end = jnp.exp(last_x - acs_x)
        e_in = jnp.exp(acs_x)
        e_last = jnp.exp(last_x)
        d_x = _dot01_r(dsk_ref[...], e_ref[...])[0:1, :]

        y = y_ref[...]
        z = z_ref[...].astype(F32)
        sz = _sigmoid(z)
        gz = z * sz
        y2 = y * gz
        dys = dys_ref[...].astype(F32)
        for g in range(N_GROUPS):
            sl = slice(g * gw, (g + 1) * gw)
            yg = y2[:, sl]
            rinv = lax.rsqrt(jnp.mean(yg * yg, axis=-1, keepdims=True) + RMS_EPS)
            nrm = yg * rinv
            dn = dys[:, sl] * nw_ref[0:1, sl]
            dnw_ref[0:1, sl] += jnp.sum(dys[:, sl] * nrm, axis=0, keepdims=True)
            dx_ref[:, sl] = rinv * (dn - nrm * jnp.mean(dn * nrm, axis=-1, keepdims=True))
        dy2 = dx_ref[...]
        dy = dy2 * gz
        dz_ref[...] = (dy2 * y * (sz * (1.0 + z * (1.0 - sz)))).astype(_ACT)

        dh_next = dh_ref[...]
        hprev = hp_ref[...]
        lo, hi = _half_masks()
        r = lax.broadcasted_iota(jnp.int32, (CHUNK, CHUNK), 0)
        c = lax.broadcasted_iota(jnp.int32, (CHUNK, CHUNK), 1)
        from_here = (c >= r).astype(jnp.bfloat16)
        before = c < r
        lane = lax.broadcasted_iota(jnp.int32, (1, LANE), 1)
        da_intra = jnp.zeros((CHUNK, LANE), F32)
        v_seg = jnp.zeros((CHUNK, LANE), F32)
        z_seg = jnp.zeros((CHUNK, LANE), F32)
        tail_parts = []
        for g in range(N_GROUPS):
            bg = bc_ref[:, g * N_STATE:(g + 1) * N_STATE]
            cg = bc_ref[:, N_GROUPS * N_STATE + g * N_STATE:N_GROUPS * N_STATE + (g + 1) * N_STATE]
            sl = slice(g * gw, (g + 1) * gw)
            et_g = et_ref[g * gw:(g + 1) * gw, :]
            gm = _dot_nt(cg, bg)
            dzg = e_in[:, sl] * dy[:, sl]
            dcg = _dot_nt(dzg, hprev[:, sl])
            dh_c = _dot(cg.T, dzg)
            q = _dot(bg, dh_next[:, sl])
            dbg = _dot_nt(x_dt[:, sl] * w_end[:, sl], dh_next[:, sl])
            y_off = _dot(cg, hprev[:, sl]) * e_in[:, sl]
            v_seg = v_seg + _dot01_r(dy[:, sl] * y_off, et_g, parts=2)
            z_seg = z_seg + _dot01_r(w_end[:, sl] * q * x_dt[:, sl], et_g, parts=2)
            dgm = jnp.zeros((CHUNK, CHUNK), F32)
            for j in range(gw // LANE):
                h0 = g * (gw // HEAD_P) + 2 * j
                cs = slice(g * gw + j * LANE, g * gw + (j + 1) * LANE)
                xp = x_dt[:, cs]
                dyp = dy[:, cs]
                dxd = jnp.zeros((CHUNK, LANE), F32)
                for half, msk in ((0, lo), (1, hi)):
                    lam = _decay(acs, acs_t, h0 + half, causal)
                    mm = gm * lam
                    dym = dyp * msk
                    dmm = _dot_nt(dym, xp)
                    dxd = dxd + _dot_tn(mm, dym)
                    dgm = dgm + dmm * lam
                    below = _dot(from_here, dmm * mm)
                    col = jnp.sum(jnp.where(before, below, 0.0), axis=-1, keepdims=True)
                    da_intra = da_intra + jnp.where(lane == h0 + half, col, 0.0)
                dx_ref[:, cs] = dxd + w_end[:, cs] * q[:, j * LANE:(j + 1) * LANE]
            dbc_ref[:, N_GROUPS * N_STATE + g * N_STATE:N_GROUPS * N_STATE + (g + 1) * N_STATE] = dcg + _dot(dgm, bg)
            dbc_ref[:, g * N_STATE:(g + 1) * N_STATE] = dbg + _dot_tn(dgm, cg)
            dh_ref[:, sl] = e_last[:, sl] * dh_next[:, sl] + dh_c
            tail_parts.append(e_last[:, sl] * jnp.sum(dh_next[:, sl] * hprev[:, sl], axis=0, keepdims=True))
        dxt = dx_ref[...]

        u_seg = _dot01_r(xs * dxt, et, parts=2)
        q_full = jnp.concatenate(tail_parts, axis=1)
        t_row = _dot01_r(jnp.broadcast_to(q_full, (SUB, D_INNER)), et)[0:1, :]
        d_alpha = (da_intra + _dot01_l(from_here, v_seg) + _dot01_l(before.astype(jnp.bfloat16), z_seg) + t_row)
        d_dt = a_row * d_alpha + u_seg
        sgp = _sigmoid(pre)
        d_raw = jnp.where(hm, d_dt * sgp, 0.0)
        ddt_ref[...] = d_raw.astype(_ACT)
        acc_ref[0:1, :] += jnp.sum(d_raw, axis=0, keepdims=True)
        acc_ref[1:2, :] += jnp.sum(d_alpha * dt, axis=0, keepdims=True) * a_row
        dd_row = jnp.sum(dy * xs, axis=0, keepdims=True)
        acc_ref[2:3, :] += _dot01_r(jnp.broadcast_to(dd_row, (SUB, D_INNER)), et)[0:1, :]
        dxs_ref[...] = dy * d_x + dxt * dt_x

    rev = lambda c: (nc - 1 - c, 0)
    small = pl.BlockSpec((SUB, LANE), lambda c: (0, 0))
    return pl.pallas_call(
        body, name=name,
        out_shape=(jax.ShapeDtypeStruct((t, D_INNER), F32), jax.ShapeDtypeStruct((t, 1024), F32),
                   jax.ShapeDtypeStruct((t, D_INNER), _ACT), jax.ShapeDtypeStruct((t, LANE), _ACT),
                   jax.ShapeDtypeStruct((SUB, LANE), F32), jax.ShapeDtypeStruct((SUB, D_INNER), F32)),
        grid=(nc,),
        in_specs=[pl.BlockSpec((CHUNK, D_INNER), rev),
                  pl.BlockSpec((CHUNK, D_INNER), rev),
                  pl.BlockSpec((CHUNK, D_INNER), rev),
                  pl.BlockSpec((CHUNK, 1024), rev),
                  pl.BlockSpec((CHUNK, LANE), lambda c: (nc - 1 - c, T_DT // LANE)),
                  pl.BlockSpec((CHUNK, D_INNER), lambda c: (nc - 1 - c, P_Z // D_INNER)),
                  pl.BlockSpec((N_STATE, D_INNER), rev),
                  small, small, small,
                  pl.BlockSpec((SUB, D_INNER), lambda c: (0, 0)),
                  pl.BlockSpec((LANE, D_INNER), lambda c: (0, 0)),
                  pl.BlockSpec((D_INNER, LANE), lambda c: (0, 0))],
        out_specs=(pl.BlockSpec((CHUNK, D_INNER), rev),
                   pl.BlockSpec((CHUNK, 1024), rev),
                   pl.BlockSpec((CHUNK, D_INNER), rev),
                   pl.BlockSpec((CHUNK, LANE), rev),
                   small,
                   pl.BlockSpec((SUB, D_INNER), lambda c: (0, 0))),
        scratch_shapes=[pltpu.VMEM((N_STATE, D_INNER), F32), pltpu.VMEM((CHUNK, D_INNER), F32)],
        compiler_params=_cp(("arbitrary",), vmem_mb=56),
    )(d_ys, y, xs_c, bc_c, tail, proj, hprev_all, dtb8, alog8, dsk8, nw8, e_bf, et_bf)


def _rel_tables():
    qi = np.arange(WIN)[:, None] + WIN
    kj = np.arange(2 * WIN)[None, :]
    rel = qi - kj
    n = np.maximum(rel, 0)
    max_exact = REL_BUCKETS // 2
    nf = np.maximum(n, 1).astype(np.float32)
    large = max_exact + (np.log(nf / np.float32(max_exact)) / np.float32(math.log(WIN / max_exact))
                         * np.float32(REL_BUCKETS - max_exact)).astype(np.int32)
    large = np.minimum(large, REL_BUCKETS - 1)
    bucket = np.where(n < max_exact, n, large)
    valid = (rel >= 0) & (rel < WIN)
    sink_col = np.broadcast_to(kj == 0, rel.shape)
    onehot = np.zeros((BIAS_ROWS, WIN * 2 * WIN), np.float32)
    flat_b = np.where(sink_col, REL_BUCKETS, bucket).reshape(-1)
    flat_v = (valid | sink_col).reshape(-1)
    first_v = ((valid & (kj >= WIN)) | sink_col).reshape(-1)
    idx = np.arange(WIN * 2 * WIN)
    onehot[flat_b[flat_v], idx[flat_v]] = 1.0
    return onehot, np.stack([first_v, flat_v]).astype(np.float32)


def _bias_expand(table_t, name):
    onehot, valid = _rel_tables()

    def body(rb_ref, oh_ref, v_ref, o_ref):
        full = _dot01_r(rb_ref[...], oh_ref[...])
        o_ref[0] = jnp.where(v_ref[0:1, :] > 0.5, full, NEG)
        o_ref[1] = jnp.where(v_ref[1:2, :] > 0.5, full, NEG)

    return pl.pallas_call(
        body, name=name, out_shape=jax.ShapeDtypeStruct((2, A_HEADS, WIN * 2 * WIN), F32),
        compiler_params=_cp(None),
    )(table_t, jnp.asarray(onehot, jnp.bfloat16), jnp.asarray(valid, F32))


def _bias_reduce(dbias, name):
    onehot, _ = _rel_tables()

    def body(d_ref, oh_ref, o_ref):
        acc = None
        r = d_ref[...]
        for _ in range(3):
            hi = r.astype(jnp.bfloat16)
            tt = lax.dot_general(hi, oh_ref[...], (((1,), (1,)), ((), ())), preferred_element_type=F32)
            acc = tt if acc is None else acc + tt
            r = r - hi.astype(F32)
        o_ref[...] = acc

    return pl.pallas_call(
        body, name=name, out_shape=jax.ShapeDtypeStruct((A_HEADS, BIAS_ROWS), F32),
        compiler_params=_cp(None),
    )(dbias, jnp.asarray(onehot, jnp.bfloat16))


def _attn_bands(kc_ref, kp_ref, vc_ref, vp_ref, has_prev):
    lo, hi = _half_masks()
    row = lax.broadcasted_iota(jnp.int32, (2 * WIN, 1), 0)
    keep = (row > 0).astype(F32)
    kb = jnp.concatenate([jnp.where(has_prev, kp_ref[...], 0.0), kc_ref[...]], axis=0) * (keep * (A_DH ** -0.5))
    vb = jnp.concatenate([jnp.where(has_prev, vp_ref[...], 0.0), vc_ref[...]], axis=0) * keep
    kr = pltpu.roll(kb, 64, 1)
    vr = pltpu.roll(vb, 64, 1)
    kk = ((kb * lo, kr * hi), (kr * lo, kb * hi))
    vv = ((vb * lo, vr * hi), (vr * lo, vb * hi))
    return kk, vv, (hi, lo)


def _attn_logits(q_ref, kk, lg_ref):
    for h in range(A_HEADS):
        j, half, kv = h // 2, h % 2, h // (A_HEADS // 2)
        lg_ref[h] = _dot_nt(q_ref[:, j * LANE:(j + 1) * LANE], kk[kv][half])


def _attn_fwd(proj, tail, bias, name):
    t = proj.shape[0]
    nb = t // WIN

    def body(q_ref, kc_ref, kp_ref, vc_ref, vp_ref, b_ref, o_ref, lg_ref, p_ref):
        n = pl.program_id(0)
        kk, vv, ones = _attn_bands(kc_ref, kp_ref, vc_ref, vp_ref, n > 0)
        _attn_logits(q_ref, kk, lg_ref)
        for h in range(A_HEADS):
            logits = lg_ref[h] + b_ref[h]
            p_ref[h] = jnp.exp(logits - jnp.max(logits, axis=-1, keepdims=True)).astype(_MXU)
        lane = lax.broadcasted_iota(jnp.int32, (1, LANE), 1)
        for j in range(A_HEADS // 2):
            kv = (2 * j) // (A_HEADS // 2)
            outs = []
            for half in range(2):
                o = jnp.dot(p_ref[2 * j + half], (vv[kv][half] + ones[half]).astype(_MXU), preferred_element_type=F32)
                outs.append(o / pltpu.roll(o, 64, 1))
            o_ref[:, j * LANE:(j + 1) * LANE] = jnp.where(lane < 64, outs[0], outs[1]).astype(_ACT)

    kvspec = lambda col, prev: pl.BlockSpec(
        (WIN, LANE), (lambda n: (jnp.maximum(n - 1, 0), col)) if prev else (lambda n: (n, col)))
    return pl.pallas_call(
        body, name=name, out_shape=jax.ShapeDtypeStruct((t, D_MODEL), _ACT),
        grid=(nb,),
        in_specs=[pl.BlockSpec((WIN, 1024), lambda n: (n, P_Q // 1024)),
                  kvspec(T_K // LANE, False), kvspec(T_K // LANE, True),
                  kvspec(T_V // LANE, False), kvspec(T_V // LANE, True),
                  pl.BlockSpec((None, A_HEADS, WIN, 2 * WIN), lambda n: (jnp.minimum(n, 1), 0, 0, 0))],
        out_specs=pl.BlockSpec((WIN, 1024), lambda n: (n, 0)),
        scratch_shapes=[pltpu.VMEM((A_HEADS, WIN, 2 * WIN), F32), pltpu.VMEM((A_HEADS, WIN, 2 * WIN), _MXU)],
        compiler_params=_cp(("parallel",)),
    )(proj, tail, tail, tail, tail, bias)


def _attn_bwd(proj, tail, bias, y_attn, d_out, name):
    t = proj.shape[0]
    nb = t // WIN

    def body(q_ref, kc_ref, kp_ref, vc_ref, vp_ref, b_ref, y_ref, do_ref,
             dq_ref, dk_ref, dv_ref, db_ref, ck_ref, cv_ref, lg_ref, dl_ref, p_ref):
        n = pl.program_id(0)

        @pl.when(n == 0)
        def _():
            db_ref[...] = jnp.zeros_like(db_ref)
            ck_ref[...] = jnp.zeros_like(ck_ref)
            cv_ref[...] = jnp.zeros_like(cv_ref)

        @pl.when(n < nb)
        def _():
            kk, vv, _ = _attn_bands(kc_ref, kp_ref, vc_ref, vp_ref, n > 0)
            lo, hi = _half_masks()
            ones_k = jnp.ones((2 * WIN, LANE), jnp.bfloat16)
            ones_d = jnp.ones((LANE, LANE), jnp.bfloat16)
            _attn_logits(q_ref, kk, lg_ref)
            for h in range(A_HEADS):
                j, half, kv = h // 2, h % 2, h // (A_HEADS // 2)
                msk = hi if half else lo
                logits = lg_ref[h] + b_ref[h]
                p = jnp.exp(logits - jnp.max(logits, axis=-1, keepdims=True))
                den = jnp.dot(p.astype(_MXU), ones_k.astype(_MXU), preferred_element_type=F32)
                dop = do_ref[:, j * LANE:(j + 1) * LANE].astype(F32)
                delta = _dot01_r(dop * y_ref[:, j * LANE:(j + 1) * LANE].astype(F32) * msk, ones_d, parts=2)
                inv = 1.0 / den
                probs = p * jnp.concatenate([inv, inv], axis=1)
                dprobs = _dot_nt(dop, vv[kv][half])
                dlog = probs * (dprobs - jnp.concatenate([delta, delta], axis=1))
                db_ref[h] += dlog
                dl_ref[h] = dlog.astype(_MXU)
                p_ref[h] = probs.astype(_MXU)
            dk_t = [[None, None], [None, None]]
            dv_t = [[None, None], [None, None]]
            for j in range(A_HEADS // 2):
                kv = (2 * j) // (A_HEADS // 2)
                qs = q_ref[:, j * LANE:(j + 1) * LANE].astype(F32) * (A_DH ** -0.5)
                dop = do_ref[:, j * LANE:(j + 1) * LANE].astype(F32)
                dq = None
                for half, msk in ((0, lo), (1, hi)):
                    h = 2 * j + half
                    dqh = jnp.dot(dl_ref[h], kk[kv][half].astype(_MXU), preferred_element_type=F32)
                    dq = dqh if dq is None else dq + dqh
                    dkh = lax.dot_general((qs * msk).astype(_MXU), dl_ref[h], (((0,), (0,)), ((), ())),
                                          preferred_element_type=F32)
                    dvh = lax.dot_general((dop * msk).astype(_MXU), p_ref[h], (((0,), (0,)), ((), ())),
                                          preferred_element_type=F32)
                    dk_t[kv][half] = dkh if dk_t[kv][half] is None else dk_t[kv][half] + dkh
                    dv_t[kv][half] = dvh if dv_t[kv][half] is None else dv_t[kv][half] + dvh
                dq_ref[:, j * LANE:(j + 1) * LANE] = dq.astype(_ACT)
            row = lax.broadcasted_iota(jnp.int32, (2 * WIN, 1), 0)

            def band(acc):
                a = (acc[0][0] + pltpu.roll(acc[0][1], 64, 0)) + (pltpu.roll(acc[1][0], 64, 0) + acc[1][1])
                return jnp.where(row > 0, a.T, 0.0)

            dkb = band(dk_t)
            dvb = band(dv_t)
            dk_ref[...] = (ck_ref[...] + dkb[0:WIN]).astype(_ACT)
            dv_ref[...] = (cv_ref[...] + dvb[0:WIN]).astype(_ACT)
            ck_ref[...] = dkb[WIN:]
            cv_ref[...] = dvb[WIN:]

        @pl.when(n == nb)
        def _():
            dk_ref[...] = ck_ref[...].astype(_ACT)
            dv_ref[...] = cv_ref[...].astype(_ACT)

    cur = lambda n: jnp.minimum(n, nb - 1)
    prv = lambda n: jnp.maximum(jnp.minimum(n, nb - 1) - 1, 0)
    kvspec = lambda col, prev: pl.BlockSpec(
        (WIN, LANE), (lambda n: (prv(n), col)) if prev else (lambda n: (cur(n), col)))
    band_shape = (A_HEADS, WIN, 2 * WIN)
    return pl.pallas_call(
        body, name=name,
        out_shape=(jax.ShapeDtypeStruct((t, D_MODEL), _ACT), jax.ShapeDtypeStruct((t, LANE), _ACT),
                   jax.ShapeDtypeStruct((t, LANE), _ACT), jax.ShapeDtypeStruct(band_shape, F32)),
        grid=(nb + 1,),
        in_specs=[pl.BlockSpec((WIN, 1024), lambda n: (cur(n), P_Q // 1024)),
                  kvspec(T_K // LANE, False), kvspec(T_K // LANE, True),
                  kvspec(T_V // LANE, False), kvspec(T_V // LANE, True),
                  pl.BlockSpec((None,) + band_shape, lambda n: (jnp.minimum(n, 1), 0, 0, 0)),
                  pl.BlockSpec((WIN, 1024), lambda n: (cur(n), 0)),
                  pl.BlockSpec((WIN, 1024), lambda n: (cur(n), 0))],
        out_specs=(pl.BlockSpec((WIN, 1024), lambda n: (cur(n), 0)),
                   pl.BlockSpec((WIN, LANE), lambda n: (jnp.maximum(n - 1, 0), 0)),
                   pl.BlockSpec((WIN, LANE), lambda n: (jnp.maximum(n - 1, 0), 0)),
                   pl.BlockSpec(band_shape, lambda n: (0, 0, 0))),
        scratch_shapes=[pltpu.VMEM((WIN, LANE), F32), pltpu.VMEM((WIN, LANE), F32),
                        pltpu.VMEM(band_shape, F32), pltpu.VMEM(band_shape, _MXU), pltpu.VMEM(band_shape, _MXU)],
        compiler_params=_cp(("arbitrary",)),
    )(proj, tail, tail, tail, tail, bias, y_attn, d_out)


def _merge_fwd(bs, ba, proj, bg8, name):
    t = bs.shape[0]
    tm = _tm_rows(t)

    def body(bs_ref, ba_ref, gs_ref, ga_ref, bgs_ref, bga_ref, o_ref):
        g_s = _sigmoid(gs_ref[...] + bgs_ref[0:1, :])
        g_a = _sigmoid(ga_ref[...] + bga_ref[0:1, :])
        o_ref[...] = (g_s * bs_ref[...] + g_a * ba_ref[...]).astype(_ACT)

    row = lambda col: pl.BlockSpec((tm, 1024), lambda i: (i, col))
    return pl.pallas_call(
        body, name=name, out_shape=jax.ShapeDtypeStruct((t, D_MODEL), _ACT), grid=(t // tm,),
        in_specs=[row(0), row(0), row(P_G // 1024), row(P_G // 1024 + 1),
                  pl.BlockSpec((SUB, 1024), lambda i: (0, 0)), pl.BlockSpec((SUB, 1024), lambda i: (0, 1))],
        out_specs=row(0), compiler_params=_cp(("parallel",)),
    )(bs, ba, proj, proj, bg8, bg8)


def _merge_bwd(d_merged, bs, ba, proj, bg8, name):
    t = bs.shape[0]
    tm = _tm_rows(t)

    def body(dm_ref, bs_ref, ba_ref, gs_ref, ga_ref, bgs_ref, bga_ref, dbs_ref, dba_ref, dg_ref, acc_ref):
        @pl.when(pl.program_id(0) == 0)
        def _():
            acc_ref[...] = jnp.zeros_like(acc_ref)

        dm = dm_ref[...].astype(F32)
        g_s = _sigmoid(gs_ref[...] + bgs_ref[0:1, :])
        g_a = _sigmoid(ga_ref[...] + bga_ref[0:1, :])
        dbs_ref[...] = (dm * g_s).astype(_ACT)
        dba_ref[...] = (dm * g_a).astype(_ACT)
        dgs = dm * bs_ref[...].astype(F32) * g_s * (1.0 - g_s)
        dga = dm * ba_ref[...].astype(F32) * g_a * (1.0 - g_a)
        dg_ref[:, 0:1024] = dgs.astype(_ACT)
        dg_ref[:, 1024:2048] = dga.astype(_ACT)
        acc_ref[0:1, 0:1024] += jnp.sum(dgs, axis=0, keepdims=True)
        acc_ref[0:1, 1024:2048] += jnp.sum(dga, axis=0, keepdims=True)

    row = lambda col: pl.BlockSpec((tm, 1024), lambda i: (i, col))
    return pl.pallas_call(
        body, name=name,
        out_shape=(jax.ShapeDtypeStruct((t, D_MODEL), _ACT), jax.ShapeDtypeStruct((t, D_MODEL), _ACT),
                   jax.ShapeDtypeStruct((t, 2048), _ACT), jax.ShapeDtypeStruct((SUB, 2048), F32)),
        grid=(t // tm,),
        in_specs=[row(0), row(0), row(0), row(P_G // 1024), row(P_G // 1024 + 1),
                  pl.BlockSpec((SUB, 1024), lambda i: (0, 0)), pl.BlockSpec((SUB, 1024), lambda i: (0, 1))],
        out_specs=(row(0), row(0), pl.BlockSpec((tm, 2048), lambda i: (i, 0)),
                   pl.BlockSpec((SUB, 2048), lambda i: (0, 0))),
        compiler_params=_cp(("arbitrary",)),
    )(d_merged, bs, ba, proj, proj, bg8, bg8)


def _ln_stats(r):
    mu = jnp.mean(r, axis=-1, keepdims=True)
    xc = r - mu
    var = jnp.mean(xc * xc, axis=-1, keepdims=True)
    rstd = lax.rsqrt(var + LN_EPS)
    return xc * rstd, rstd


def _ln_bwd(dxhat, xhat, rstd):
    return rstd * (dxhat - jnp.mean(dxhat, axis=-1, keepdims=True)
                   - xhat * jnp.mean(dxhat * xhat, axis=-1, keepdims=True))


def _ln1_fwd(x, mix, g8, b8, name):
    t = x.shape[0]
    tm = _tm_rows(t)

    def body(x_ref, m_ref, g_ref, b_ref, xh_ref, h_ref, rs_ref):
        xhat, rstd = _ln_stats(ALPHA * x_ref[...] + m_ref[...])
        xh_ref[...] = xhat
        h_ref[...] = (xhat * g_ref[0:1, :] + b_ref[0:1, :]).astype(_ACT)
        rs_ref[...] = rstd

    row = pl.BlockSpec((tm, D_MODEL), lambda i: (i, 0))
    par = pl.BlockSpec((SUB, D_MODEL), lambda i: (0, 0))
    return pl.pallas_call(
        body, name=name,
        out_shape=(jax.ShapeDtypeStruct((t, D_MODEL), F32), jax.ShapeDtypeStruct((t, D_MODEL), _ACT),
                   jax.ShapeDtypeStruct((t, 1), F32)),
        grid=(t // tm,), in_specs=[row, row, par, par],
        out_specs=(row, row, pl.BlockSpec((tm, 1), lambda i: (i, 0))),
        compiler_params=_cp(("parallel",)),
    )(x, mix, g8, b8)


def _ln2_loss(xhat1, ffn, target, g1_8, b1_8, g2_8, b2_8, name):
    t = xhat1.shape[0]
    tm = _tm_rows(t)

    def body(xh_ref, f_ref, t_ref, g1_ref, b1_ref, g2_ref, b2_ref, d_ref, db_ref, acc_ref):
        @pl.when(pl.program_id(0) == 0)
        def _():
            acc_ref[...] = jnp.zeros_like(acc_ref)

        h1 = xh_ref[...] * g1_ref[0:1, :] + b1_ref[0:1, :]
        xhat, rstd = _ln_stats(ALPHA * h1 + f_ref[...])
        diff = xhat * g2_ref[0:1, :] + b2_ref[0:1, :] - t_ref[...]
        dy = diff * (1.0 / D_MODEL)
        acc_ref[0:1, :] += jnp.sum(dy * xhat, axis=0, keepdims=True)
        acc_ref[1:2, :] += jnp.sum(dy, axis=0, keepdims=True)
        acc_ref[2:3, :] += jnp.sum(diff * diff, axis=0, keepdims=True)
        d = _ln_bwd(dy * g2_ref[0:1, :], xhat, rstd)
        d_ref[...] = d
        db_ref[...] = d.astype(_ACT)

    row = pl.BlockSpec((tm, D_MODEL), lambda i: (i, 0))
    par = pl.BlockSpec((SUB, D_MODEL), lambda i: (0, 0))
    return pl.pallas_call(
        body, name=name,
        out_shape=(jax.ShapeDtypeStruct((t, D_MODEL), F32), jax.ShapeDtypeStruct((t, D_MODEL), _ACT),
                   jax.ShapeDtypeStruct((SUB, D_MODEL), F32)),
        grid=(t // tm,), in_specs=[row, row, row, par, par, par, par],
        out_specs=(row, row, par), compiler_params=_cp(("arbitrary",)),
    )(xhat1, ffn, target, g1_8, b1_8, g2_8, b2_8)


def _ln1_bwd(d_r2, d_h1_ffn, xhat1, rstd1, g1_8, name):
    t = xhat1.shape[0]
    tm = _tm_rows(t)

    def body(d2_ref, df_ref, xh_ref, rs_ref, g_ref, d_ref, db_ref, acc_ref):
        @pl.when(pl.program_id(0) == 0)
        def _():
            acc_ref[...] = jnp.zeros_like(acc_ref)

        dh = ALPHA * d2_ref[...] + df_ref[...]
        xhat = xh_ref[...]
        acc_ref[0:1, :] += jnp.sum(dh * xhat, axis=0, keepdims=True)
        acc_ref[1:2, :] += jnp.sum(dh, axis=0, keepdims=True)
        d = _ln_bwd(dh * g_ref[0:1, :], xhat, rs_ref[...])
        d_ref[...] = d
        db_ref[...] = d.astype(_ACT)

    row = pl.BlockSpec((tm, D_MODEL), lambda i: (i, 0))
    par = pl.BlockSpec((SUB, D_MODEL), lambda i: (0, 0))
    return pl.pallas_call(
        body, name=name,
        out_shape=(jax.ShapeDtypeStruct((t, D_MODEL), F32), jax.ShapeDtypeStruct((t, D_MODEL), _ACT),
                   jax.ShapeDtypeStruct((SUB, D_MODEL), F32)),
        grid=(t // tm,), in_specs=[row, row, row, pl.BlockSpec((tm, 1), lambda i: (i, 0)), par],
        out_specs=(row, row, par), compiler_params=_cp(("arbitrary",)),
    )(d_r2, d_h1_ffn, xhat1, rstd1, g1_8)


def _ffn_tm(t):
    return min(128, t)


def _ffn_act_fwd(u0, cw8, cb8, name):
    t = u0.shape[0]
    tm = _ffn_tm(t)

    def body(g_ref, gp_ref, v_ref, vp_ref, wg_ref, wv_ref, bg_ref, bv_ref, o_ref):
        i = pl.program_id(0)
        gprev = jnp.where(i > 0, gp_ref[SUB:HALO, :].astype(F32), 0.0)
        vprev = jnp.where(i > 0, vp_ref[SUB:HALO, :].astype(F32), 0.0)
        gate = _conv_pre(g_ref[...].astype(F32), gprev, wg_ref, bg_ref[0:1, :], FFN_K)
        val = _conv_pre(v_ref[...].astype(F32), vprev, wv_ref, bv_ref[0:1, :], FFN_K)
        o_ref[...] = (gate * _sigmoid(gate) * val).astype(_ACT)

    cur = lambda col: pl.BlockSpec((tm, D_FF), lambda i: (i, col))
    prv = lambda col: _prev_halo(tm, D_FF, lambda i: (i, col))
    par = lambda col: pl.BlockSpec((SUB, D_FF), lambda i: (0, col))
    return pl.pallas_call(
        body, name=name, out_shape=jax.ShapeDtypeStruct((t, D_FF), _ACT), grid=(t // tm,),
        in_specs=[cur(0), prv(0), cur(1), prv(1), par(0), par(1), par(0), par(1)],
        out_specs=pl.BlockSpec((tm, D_FF), lambda i: (i, 0)), compiler_params=_cp(("parallel",)),
    )(u0, u0, u0, u0, cw8, cw8, cb8, cb8)


def _ffn_act_bwd(u0, cw8, cb8, d_a, name):
    t = u0.shape[0]
    tm = _ffn_tm(t)
    nt = t // tm

    def body(g_ref, gp_ref, gn_ref, v_ref, vp_ref, vn_ref, wg_ref, wv_ref, bg_ref, bv_ref, da_ref, dan_ref,
             du_ref, acc_ref):
        i = pl.program_id(0)

        @pl.when(i == 0)
        def _():
            acc_ref[...] = jnp.zeros_like(acc_ref)

        def grads(gcur, gprev, vcur, vprev, da):
            gate = _conv_pre(gcur, gprev, wg_ref, bg_ref[0:1, :], FFN_K)
            val = _conv_pre(vcur, vprev, wv_ref, bv_ref[0:1, :], FFN_K)
            return da * val * _silu_grad(gate), da * gate * _sigmoid(gate)

        gcur, vcur = g_ref[...].astype(F32), v_ref[...].astype(F32)
        gprev = jnp.where(i > 0, gp_ref[SUB:HALO, :].astype(F32), 0.0)
        vprev = jnp.where(i > 0, vp_ref[SUB:HALO, :].astype(F32), 0.0)
        dgate, dval = grads(gcur, gprev, vcur, vprev, da_ref[...].astype(F32))
        dgate_n, dval_n = grads(gn_ref[0:SUB, :].astype(F32), gcur[tm - SUB:], vn_ref[0:SUB, :].astype(F32),
                                vcur[tm - SUB:], dan_ref[0:SUB, :].astype(F32))
        last = i == nt - 1
        du_ref[:, 0:D_FF] = _conv_t(dgate, jnp.where(last, 0.0, dgate_n), wg_ref, FFN_K).astype(_ACT)
        du_ref[:, D_FF:2 * D_FF] = _conv_t(dval, jnp.where(last, 0.0, dval_n), wv_ref, FFN_K).astype(_ACT)
        for d, cur, prev, off in ((dgate, gcur, gprev, 0), (dval, vcur, vprev, D_FF)):
            acc_ref[FFN_K:FFN_K + 1, off:off + D_FF] += jnp.sum(d, axis=0, keepdims=True)
            acc_ref[FFN_K - 1:FFN_K, off:off + D_FF] += jnp.sum(d * cur, axis=0, keepdims=True)
            for s in range(1, FFN_K):
                acc_ref[FFN_K - 1 - s:FFN_K - s, off:off + D_FF] += jnp.sum(
                    d * _shift_down(cur, prev, s), axis=0, keepdims=True)

    cur = lambda col: pl.BlockSpec((tm, D_FF), lambda i: (i, col))
    prv = lambda col: _prev_halo(tm, D_FF, lambda i: (i, col))
    nxt = lambda col: _next_halo(tm, t, D_FF, lambda i: (i, col))
    par = lambda col: pl.BlockSpec((SUB, D_FF), lambda i: (0, col))
    return pl.pallas_call(
        body, name=name,
        out_shape=(jax.ShapeDtypeStruct((t, 2 * D_FF), _ACT), jax.ShapeDtypeStruct((SUB, 2 * D_FF), F32)),
        grid=(nt,),
        in_specs=[cur(0), prv(0), nxt(0), cur(1), prv(1), nxt(1), par(0), par(1), par(0), par(1),
                  cur(0), nxt(0)],
        out_specs=(pl.BlockSpec((tm, 2 * D_FF), lambda i: (i, 0)),
                   pl.BlockSpec((SUB, 2 * D_FF), lambda i: (0, 0))),
        compiler_params=_cp(("arbitrary",)),
    )(u0, u0, u0, u0, u0, u0, cw8, cw8, cb8, cb8, d_a, d_a)


def _pack_w_in(w_in):
    order = sorted(_PIECES, key=lambda p: p[2])
    cols = []
    at = 0
    for o, w, pk in order:
        if pk > at:
            cols.append(jnp.zeros((w_in.shape[0], pk - at), w_in.dtype))
        cols.append(w_in[:, o:o + w])
        at = pk + w
    if at < P_W:
        cols.append(jnp.zeros((w_in.shape[0], P_W - at), w_in.dtype))
    return jnp.concatenate(cols, axis=1)


def _unpack_w_in(wp):
    return jnp.concatenate([wp[:, pk:pk + w] for o, w, pk in sorted(_PIECES)], axis=1)


def _local_step(x, target, wts):
    t = x.shape[0]
    wp = _pack_w_in(wts["w_in"])
    w_bs, w_ba, w_mix, w_up, w_dn = (wts[k] for k in ("w_branch_ssm", "w_branch_attn", "w_mix_out", "w_up", "w_down"))
    scw = wts["ssm_conv_w"]
    scb = wts["ssm_conv_b"]
    fcw8 = _rows8(wts["ffn_conv_w"])
    fcb8 = _rows8(wts["ffn_conv_b"])
    pad_lane = lambda p: jnp.concatenate([p.astype(F32), jnp.zeros((1, LANE - p.shape[1]), F32)], axis=1)
    dtb8 = _rows8(pad_lane(wts["ssm_dt_bias"]))
    alog8 = _rows8(pad_lane(wts["ssm_a_log"]))
    dsk8 = _rows8(pad_lane(wts["ssm_d"]))
    bias_table = jnp.concatenate([wts["rel_bias"].T.astype(F32), wts["attn_sinks"].T.astype(F32),
                                  jnp.zeros((A_HEADS, BIAS_ROWS - REL_BUCKETS - 1), F32)], axis=1)
    nw8 = _rows8(wts["ssm_norm_w"])
    bg8 = _rows8(wts["b_gate"])
    g1_8, b1_8, g2_8, b2_8 = (_rows8(wts[k]) for k in ("ln1_g", "ln1_b", "ln2_g", "ln2_b"))
    xs_w8, xs_b8 = _rows8(scw[:, :D_INNER]), _rows8(scb[:, :D_INNER])
    bc_w8, bc_b8 = _rows8(scw[:, D_INNER:]), _rows8(scb[:, D_INNER:])

    x_bf = x.astype(_ACT)
    proj = _mm(x_bf, wp[:, :P_MAIN], "mm_in", out_dtype=_ACT)
    tail = _mm(x_bf, wp[:, P_MAIN:], "mm_in_tail")
    xs_c = _conv_silu_fwd(proj, P_XS // _TC, D_INNER // _TC, xs_w8, xs_b8, "conv_xs_fwd")
    bc_c = _conv_silu_fwd(proj, P_BC // _TC, 1024 // _TC, bc_w8, bc_b8, "conv_bc_fwd")
    y_ssd, y_ssm, hprev = _ssd_fwd(xs_c, bc_c, proj, tail, dtb8, alog8, dsk8, nw8, "ssd_fwd")
    bias = _bias_expand(bias_table, "bias_expand").reshape(2, A_HEADS, WIN, 2 * WIN)
    y_attn = _attn_fwd(proj, tail, bias, "attn_fwd")
    bs = _mm(y_ssm, w_bs, "mm_bs", out_dtype=_ACT)
    ba = _mm(y_attn, w_ba, "mm_ba", out_dtype=_ACT)
    merged = _merge_fwd(bs, ba, proj, bg8, "merge_fwd")
    mix = _mm(merged, w_mix, "mm_mix", out_dtype=_ACT)
    xhat1, h1_bf, rstd1 = _ln1_fwd(x, mix, g1_8, b1_8, "ln1_fwd")
    u0 = _mm(h1_bf, w_up, "mm_up", out_dtype=_ACT)
    act = _ffn_act_fwd(u0, fcw8, fcb8, "ffn_act_fwd")
    ffn = _mm(act, w_dn, "mm_down", out_dtype=_ACT)
    d_r2, d_r2_bf, acc_ln2 = _ln2_loss(xhat1, ffn, target, g1_8, b1_8, g2_8, b2_8, "ln2_loss")
    d_w_dn = _mm(act, d_r2_bf, "mm_dw_down", trans_a=True)
    d_act = _mm(d_r2_bf, w_dn.T, "mm_d_act", out_dtype=_ACT)
    d_u0, acc_ffn = _ffn_act_bwd(u0, fcw8, fcb8, d_act, "ffn_act_bwd")
    d_w_up = _mm(h1_bf, d_u0, "mm_dw_up", trans_a=True)
    d_h1_ffn = _mm(d_u0, w_up.T, "mm_d_h1", out_dtype=_ACT)
    d_r1, d_r1_bf, acc_ln1 = _ln1_bwd(d_r2, d_h1_ffn, xhat1, rstd1, g1_8, "ln1_bwd")
    d_w_mix = _mm(merged, d_r1_bf, "mm_dw_mix", trans_a=True)
    d_merged = _mm(d_r1_bf, w_mix.T, "mm_d_merged", out_dtype=_ACT)
    d_bs, d_ba, d_gates, acc_bg = _merge_bwd(d_merged, bs, ba, proj, bg8, "merge_bwd")
    d_w_bs = _mm(y_ssm, d_bs, "mm_dw_bs", trans_a=True)
    d_w_ba = _mm(y_attn, d_ba, "mm_dw_ba", trans_a=True)
    d_y_ssm = _mm(d_bs, w_bs.T, "mm_d_yssm", out_dtype=_ACT)
    d_y_attn = _mm(d_ba, w_ba.T, "mm_d_yattn", out_dtype=_ACT)
    d_q, d_k, d_v, d_bias = _attn_bwd(proj, tail, bias, y_attn, d_y_attn, "attn_bwd")
    d_table = _bias_reduce(d_bias.reshape(A_HEADS, WIN * 2 * WIN), "bias_reduce")
    d_xs_c, d_bc_c, d_z, d_dt, acc_ssd, acc_nw = _ssd_bwd(
        d_y_ssm, y_ssd, xs_c, bc_c, proj, tail, hprev, dtb8, alog8, dsk8, nw8, "ssd_bwd")
    d_xs, acc_xs = _conv_silu_bwd(proj, P_XS // _TC, D_INNER // _TC, xs_w8, xs_b8, d_xs_c, "conv_xs_bwd")
    d_bc, acc_bc = _conv_silu_bwd(proj, P_BC // _TC, 1024 // _TC, bc_w8, bc_b8, d_bc_c, "conv_bc_bwd")
    d_proj = jnp.concatenate([d_z, d_xs, d_gates, d_q, d_bc, d_k, d_v, d_dt,
                              jnp.zeros((t, P_W - P_DT - LANE), _ACT)], axis=1)
    d_wp = _mm(x_bf, d_proj, "mm_dw_in", trans_a=True)
    d_x = _mm(d_proj, wp.T, "mm_d_x", res=d_r1, res_scale=ALPHA)

    grads = {
        "w_in": _unpack_w_in(d_wp),
        "ssm_conv_w": jnp.concatenate([acc_xs[0:SSM_K], acc_bc[0:SSM_K]], axis=1),
        "w_branch_ssm": d_w_bs, "w_branch_attn": d_w_ba, "w_mix_out": d_w_mix,
        "w_up": d_w_up, "ffn_conv_w": acc_ffn[0:FFN_K], "w_down": d_w_dn,
    }
    small = {
        "rel_bias": d_table[:, 0:REL_BUCKETS].T,
        "b_gate": acc_bg[0:1],
        "ssm_conv_b": jnp.concatenate([acc_xs[SSM_K:SSM_K + 1], acc_bc[SSM_K:SSM_K + 1]], axis=1),
        "ssm_dt_bias": acc_ssd[0:1, 0:N_HEADS], "ssm_a_log": acc_ssd[1:2, 0:N_HEADS], "ssm_d": acc_ssd[2:3, 0:N_HEADS],
        "ssm_norm_w": acc_nw[0:1],
        "attn_sinks": d_table[:, REL_BUCKETS:REL_BUCKETS + 1].T,
        "ln1_g": acc_ln1[0:1], "ln1_b": acc_ln1[1:2],
        "ffn_conv_b": acc_ffn[FFN_K:FFN_K + 1],
        "ln2_g": acc_ln2[0:1], "ln2_b": acc_ln2[1:2],
        "loss_lanes": acc_ln2[2:3],
    }
    return d_x, grads, small


_BIG = (("w_in", (1024, 2120), 1), ("ssm_conv_w", (4, 768), 1), ("w_branch_ssm", (512, 1024), 0),
        ("w_branch_attn", (256, 1024), 0), ("w_mix_out", (256, 1024), 0), ("w_up", (1024, 1408), 1),
        ("ffn_conv_w", (3, 1408), 1), ("w_down", (704, 1024), 0))
_ROW_ALIGN = 16
_BLK_ROWS = 1024


def _piece_rows(shape):
    rows = -(-(shape[0] * shape[1]) // LANE)
    return -(-rows // _ROW_ALIGN) * _ROW_ALIGN


_BIG_ROWS = -(-sum(_piece_rows(s) for _, s, _ in _BIG) // (2 * _BLK_ROWS)) * (2 * _BLK_ROWS)
_HALF_ROWS = _BIG_ROWS // 2
_F32_CONV = (("ssm_conv_w", (4, 768)), ("ffn_conv_w", (3, 1408)))
_AG_ROWS = _BIG_ROWS + 2 * _BLK_ROWS
_AG_HALF = _AG_ROWS // 2

_SMALL = (("rel_bias", (32, 16)), ("b_gate", (1, 2048)), ("ssm_conv_b", (1, 3072)), ("ssm_dt_bias", (1, 32)),
          ("ssm_a_log", (1, 32)), ("ssm_d", (1, 32)), ("ssm_norm_w", (1, 2048)), ("attn_sinks", (1, 16)),
          ("ln1_g", (1, 1024)), ("ln1_b", (1, 1024)), ("ffn_conv_b", (1, 5632)), ("ln2_g", (1, 1024)),
          ("ln2_b", (1, 1024)))
_LOSS_ROWS = SUB


def _small_rows(shape):
    rows = -(-(shape[0] * shape[1]) // LANE)
    return -(-rows // SUB) * SUB


_SMALL_ROWS = sum(_small_rows(s) for _, s in _SMALL) + _LOSS_ROWS


def _as_rows(a, rows, dtype):
    flat = a.reshape(-1).astype(dtype)
    flat = jnp.concatenate([flat, jnp.zeros((rows * LANE - flat.shape[0],), dtype)])
    return flat.reshape(rows, LANE)


def _pack_big(parts, dtype):
    blocks = [_as_rows(parts[n], _piece_rows(s), dtype) for n, s, _ in _BIG]
    used = sum(b.shape[0] for b in blocks)
    blocks.append(jnp.zeros((_BIG_ROWS - used, LANE), dtype))
    return jnp.concatenate(blocks, axis=0)


def _unpack_big(packed):
    out, at = {}, 0
    for n, s, _ in _BIG:
        rows = _piece_rows(s)
        out[n] = packed[at:at + rows].reshape(-1)[:s[0] * s[1]].reshape(s)
        at += rows
    return out


def _pack_small(parts, extra):
    blocks = [_as_rows(parts[n], _small_rows(s), F32) for n, s in _SMALL]
    blocks.append(_as_rows(extra, _LOSS_ROWS, F32))
    return jnp.concatenate(blocks, axis=0)


def _unpack_small(packed):
    out, at = {}, 0
    for n, s in _SMALL:
        rows = _small_rows(s)
        out[n] = packed[at:at + rows].reshape(-1)[:s[0] * s[1]].reshape(s)
        at += rows
    return out, packed[at:at + _LOSS_ROWS]


def _shard_of(full, shape, axis, j):
    return lax.slice_in_dim(full, j * shape[axis], (j + 1) * shape[axis], axis=axis)


_MESH = pl.DeviceIdType.MESH
_HBM = pl.BlockSpec(memory_space=pltpu.HBM)


def _position():
    return lax.axis_index("x"), lax.axis_index("y"), lax.axis_index("c")


def _other_chips(x, y):
    return ((1 - x, y), (x, 1 - y), (1 - x, 1 - y))


def _allgather_weights(shard):
    _, hr, _ = shard.shape

    def body(s_ref, o_ref, send_sems, recv_sems, local_sem):
        x, y, c = _position()
        me = 2 * x + y
        sib = (x, y, 1 - c)
        chips = _other_chips(x, y)
        mine = pltpu.make_async_copy(s_ref, o_ref.at[me], local_sem)
        mine.start()

        def copy(k, chip_idx, half, to, src=None):
            dst = o_ref.at[chip_idx, half]
            return pltpu.make_async_remote_copy(src_ref=dst if src is None else src, dst_ref=dst,
                                                send_sem=send_sems.at[k], recv_sem=recv_sems.at[k],
                                                device_id=to, device_id_type=_MESH)

        first = [copy(i, me, c, (cx, cy, c), src=s_ref.at[c]) for i, (cx, cy) in enumerate(chips)]
        for cp in first:
            cp.start()
        passed = [copy(3 + i, 2 * cx + cy, c, sib) for i, (cx, cy) in enumerate(chips)]
        for i, (cx, cy) in enumerate(chips):
            copy(i, 2 * cx + cy, c, sib).wait_recv()
            passed[i].start()
        for i, (cx, cy) in enumerate(chips):
            copy(3 + i, 2 * cx + cy, 1 - c, sib).wait_recv()
        for cp in first + passed:
            cp.wait_send()
        mine.wait()

    return pl.pallas_call(
        body, name="allgather_weights",
        out_shape=jax.ShapeDtypeStruct((N_CHIPS, 2, hr, LANE), shard.dtype),
        in_specs=[_HBM], out_specs=_HBM,
        scratch_shapes=[pltpu.SemaphoreType.DMA((6,)), pltpu.SemaphoreType.DMA((6,)), pltpu.SemaphoreType.DMA],
    )(shard)


def _swap_halves(g):
    nseg, _, hr, _ = g.shape

    def body(g_ref, o_ref, send_sems, recv_sems):
        x, y, c = _position()
        cps = [pltpu.make_async_remote_copy(src_ref=g_ref.at[j, 1 - c], dst_ref=o_ref.at[j],
                                            send_sem=send_sems.at[j], recv_sem=recv_sems.at[j],
                                            device_id=(x, y, 1 - c), device_id_type=_MESH) for j in range(nseg)]
        for cp in cps:
            cp.start()
        for cp in cps:
            cp.wait()

    return pl.pallas_call(
        body, name="swap_halves", out_shape=jax.ShapeDtypeStruct((nseg, hr, LANE), g.dtype),
        in_specs=[_HBM], out_specs=_HBM,
        scratch_shapes=[pltpu.SemaphoreType.DMA((nseg,)), pltpu.SemaphoreType.DMA((nseg,))],
    )(g)


def _scatter_chips(p):
    _, hr, _ = p.shape

    def body(p_ref, o_ref, send_sems, recv_sems):
        x, y, c = _position()
        cps = [pltpu.make_async_remote_copy(src_ref=p_ref.at[2 * cx + cy], dst_ref=o_ref.at[i],
                                            send_sem=send_sems.at[i], recv_sem=recv_sems.at[i],
                                            device_id=(cx, cy, c), device_id_type=_MESH)
               for i, (cx, cy) in enumerate(_other_chips(x, y))]
        for cp in cps:
            cp.start()
        for cp in cps:
            cp.wait()

    return pl.pallas_call(
        body, name="scatter_chips", out_shape=jax.ShapeDtypeStruct((N_CHIPS - 1, hr, LANE), p.dtype),
        in_specs=[_HBM], out_specs=_HBM,
        scratch_shapes=[pltpu.SemaphoreType.DMA((N_CHIPS - 1,)), pltpu.SemaphoreType.DMA((N_CHIPS - 1,))],
    )(p)


def _join_halves(red):
    hr, _ = red.shape

    def body(r_ref, o_ref, send_sem, recv_sem, local_sem):
        x, y, c = _position()
        mine = pltpu.make_async_copy(r_ref, o_ref.at[c], local_sem)
        mine.start()
        cp = pltpu.make_async_remote_copy(src_ref=r_ref, dst_ref=o_ref.at[c], send_sem=send_sem, recv_sem=recv_sem,
                                          device_id=(x, y, 1 - c), device_id_type=_MESH)
        cp.start()
        cp.wait()
        mine.wait()

    return pl.pallas_call(
        body, name="join_halves", out_shape=jax.ShapeDtypeStruct((2, hr, LANE), red.dtype),
        in_specs=[_HBM], out_specs=_HBM,
        scratch_shapes=[pltpu.SemaphoreType.DMA, pltpu.SemaphoreType.DMA, pltpu.SemaphoreType.DMA],
    )(red)


def _allgather_small(mine):
    m_per, n = mine.shape

    def body(x_ref, out_ref, send_sems, recv_sems, local_sem):
        x, y, c = _position()
        me, sibling = (x, y, c), (x, y, 1 - c)
        chips = _other_chips(x, y)

        def rows(px, py, pc):
            return out_ref.at[pl.ds((4 * px + 2 * py + pc) * m_per, m_per), :]

        def copy(k, block, to, src=None):
            return pltpu.make_async_remote_copy(src_ref=rows(*block) if src is None else src, dst_ref=rows(*block),
                                                send_sem=send_sems.at[k], recv_sem=recv_sems.at[k],
                                                device_id=to, device_id_type=_MESH)

        own = pltpu.make_async_copy(x_ref, rows(*me), local_sem)
        own.start()
        first = [copy(0, me, sibling, src=x_ref)]
        first += [copy(1 + j, me, (*chip, c), src=x_ref) for j, chip in enumerate(chips)]
        for cp in first:
            cp.start()
        passed = [copy(4 + j, (*chip, c), sibling) for j, chip in enumerate(chips)]
        for j, chip in enumerate(chips):
            copy(1 + j, (*chip, c), me).wait_recv()
            passed[j].start()
        copy(0, sibling, me).wait_recv()
        for j, chip in enumerate(chips):
            copy(4 + j, (*chip, 1 - c), me).wait_recv()
        for cp in first + passed:
            cp.wait_send()
        own.wait()

    return pl.pallas_call(
        body, name="allgather_small", out_shape=jax.ShapeDtypeStruct((N_DEV * m_per, n), mine.dtype),
        in_specs=[pl.BlockSpec(memory_space=pltpu.VMEM)], out_specs=pl.BlockSpec(memory_space=pltpu.VMEM),
        scratch_shapes=[pltpu.SemaphoreType.DMA((7,)), pltpu.SemaphoreType.DMA((7,)), pltpu.SemaphoreType.DMA],
    )(mine)


def _add_own_half(g, recv, c_idx):
    nseg, _, hr, _ = g.shape

    def body(c_ref, g_ref, r_ref, o_ref, ob_ref):
        s = g_ref[...] + r_ref[...]
        o_ref[...] = s
        ob_ref[...] = s.astype(jnp.bfloat16)

    blk = pl.BlockSpec((None, _BLK_ROWS, LANE), lambda j, i, c_ref: (j, i, 0))
    return pl.pallas_call(
        body, name="add_own_half",
        out_shape=(jax.ShapeDtypeStruct((nseg, hr, LANE), F32), jax.ShapeDtypeStruct((nseg, hr, LANE), jnp.bfloat16)),
        grid_spec=pltpu.PrefetchScalarGridSpec(
            num_scalar_prefetch=1, grid=(nseg, hr // _BLK_ROWS),
            in_specs=[pl.BlockSpec((None, None, _BLK_ROWS, LANE), lambda j, i, c_ref: (j, c_ref[0], i, 0)), blk],
            out_specs=(blk, blk)),
        compiler_params=_cp(("parallel", "parallel")),
    )(c_idx, g, recv)


def _add_chips(p, recv, chip_idx):
    _, hr, _ = p.shape

    def body(j_ref, p_ref, r_ref, o_ref):
        o_ref[...] = ((p_ref[...] + r_ref[0].astype(F32)) + r_ref[1].astype(F32)) + r_ref[2].astype(F32)

    return pl.pallas_call(
        body, name="add_chips", out_shape=jax.ShapeDtypeStruct((hr, LANE), F32),
        grid_spec=pltpu.PrefetchScalarGridSpec(
            num_scalar_prefetch=1, grid=(hr // _BLK_ROWS,),
            in_specs=[pl.BlockSpec((None, _BLK_ROWS, LANE), lambda i, j_ref: (j_ref[0], i, 0)),
                      pl.BlockSpec((N_CHIPS - 1, _BLK_ROWS, LANE), lambda i, j_ref: (0, i, 0))],
            out_specs=pl.BlockSpec((_BLK_ROWS, LANE), lambda i, j_ref: (i, 0))),
        compiler_params=_cp(("parallel",)),
    )(chip_idx, p, recv)


def _adam_math(w, g, m, v):
    m = ADAM_B1 * m + (1.0 - ADAM_B1) * g
    v = ADAM_B2 * v + (1.0 - ADAM_B2) * (g * g)
    m_hat = m / (1.0 - ADAM_B1 ** ADAM_STEP)
    v_hat = v / (1.0 - ADAM_B2 ** ADAM_STEP)
    delta = -ADAM_LR * (m_hat / (jnp.sqrt(v_hat) + ADAM_EPS) + ADAM_WD * w)
    return delta, m, v


def _adam_big(w, g, m, v, name):
    rows, cols = w.shape
    tr = _pick(rows, (256, 128, 64, 32, 16, 8)) if rows % SUB == 0 else rows

    def body(w_ref, g_ref, m_ref, v_ref, d_ref, mo_ref, vo_ref):
        d_ref[...], mo_ref[...], vo_ref[...] = _adam_math(w_ref[...], g_ref[...], m_ref[...], v_ref[...])

    blk = pl.BlockSpec((tr, cols), lambda i: (i, 0))
    shp = jax.ShapeDtypeStruct((rows, cols), F32)
    return pl.pallas_call(
        body, name=name, out_shape=(shp, shp, shp), grid=(rows // tr,),
        in_specs=[blk, blk, blk, blk], out_specs=(blk, blk, blk), compiler_params=_cp(("parallel",)),
    )(w, g, m, v)


def _adam_small(w, gathered, m, v):
    rows = w.shape[0]

    def body(w_ref, a_ref, m_ref, v_ref, g_ref, d_ref, mo_ref, vo_ref):
        g = a_ref[0:rows, :]
        for k in range(1, N_DEV):
            g = g + a_ref[k * rows:(k + 1) * rows, :]
        g_ref[...] = g
        d_ref[...], mo_ref[...], vo_ref[...] = _adam_math(w_ref[...], g, m_ref[...], v_ref[...])

    shp = jax.ShapeDtypeStruct((rows, LANE), F32)
    return pl.pallas_call(body, name="adam_small", out_shape=(shp, shp, shp, shp), compiler_params=_cp(None))(
        w, gathered, m, v)


_WEIGHTS = ("rel_bias", "w_in", "b_gate", "ssm_conv_w", "ssm_conv_b", "ssm_dt_bias", "ssm_a_log", "ssm_d",
            "ssm_norm_w", "attn_sinks", "w_branch_ssm", "w_branch_attn", "w_mix_out", "ln1_g", "ln1_b", "w_up",
            "ffn_conv_w", "ffn_conv_b", "w_down", "ln2_g", "ln2_b")
_BIG_NAMES = tuple(n for n, _, _ in _BIG)


def _step(x, target, w, m, v):
    xi, yi, ci = _position()
    chip = 2 * xi + yi

    f32_rows = jnp.concatenate(
        [_as_rows(lax.bitcast_convert_type(w[n].astype(F32), jnp.bfloat16), 2 * _piece_rows(s), jnp.bfloat16)
         for n, s in _F32_CONV], axis=0)
    f32_rows = jnp.concatenate(
        [f32_rows, jnp.zeros((_AG_ROWS - _BIG_ROWS - f32_rows.shape[0], LANE), jnp.bfloat16)], axis=0)
    shard = jnp.concatenate([_pack_big(w, jnp.bfloat16), f32_rows], axis=0).reshape(2, _AG_HALF, LANE)
    gathered = _allgather_weights(shard).reshape(N_CHIPS, _AG_ROWS, LANE)
    per_chip = [_unpack_big(gathered[j, :_BIG_ROWS]) for j in range(N_CHIPS)]
    full = {n: jnp.concatenate([per_chip[j][n] for j in range(N_CHIPS)], axis=ax) for n, _, ax in _BIG}
    at = _BIG_ROWS
    for n, s in _F32_CONV:
        rows = 2 * _piece_rows(s)
        pieces = [lax.bitcast_convert_type(gathered[j, at:at + rows].reshape(-1, 2), F32)[:s[0] * s[1]].reshape(s)
                  for j in range(N_CHIPS)]
        full[n] = jnp.concatenate(pieces, axis=1)
        at += rows
    wts = {n: (full[n] if n in full else w[n]) for n in _WEIGHTS}

    d_x, grads, small = _local_step(x, target, wts)

    g = jnp.stack([_pack_big({n: _shard_of(grads[n], s, ax, j) for n, s, ax in _BIG}, F32)
                   for j in range(N_CHIPS)]).reshape(N_CHIPS, 2, _HALF_ROWS, LANE)
    c_idx = jnp.reshape(ci, (1,)).astype(jnp.int32)
    chip_sum, chip_sum_bf = _add_own_half(g, _swap_halves(g), c_idx)
    red = _add_chips(chip_sum, _scatter_chips(chip_sum_bf), jnp.reshape(chip, (1,)).astype(jnp.int32))
    g_big = _join_halves(red).reshape(_BIG_ROWS, LANE)
    outs = {"grad": _unpack_big(g_big), "delta": {}, "m": {}, "v": {}}
    for n, _, _ in _BIG:
        outs["delta"][n], outs["m"][n], outs["v"][n] = _adam_big(
            w[n].astype(F32), outs["grad"][n], m[n].astype(F32), v[n].astype(F32), "adam_" + n)

    all_small = _allgather_small(_pack_small(small, small["loss_lanes"]))
    g_s, d_s, m_s, v_s = _adam_small(_pack_small(w, jnp.zeros((1, LANE), F32)), all_small,
                                     _pack_small(m, jnp.zeros((1, LANE), F32)),
                                     _pack_small(v, jnp.zeros((1, LANE), F32)))
    (gs, loss_rows), (ds, _), (ms, _), (vs, _) = (_unpack_small(a) for a in (g_s, d_s, m_s, v_s))
    for kind, part in (("grad", gs), ("delta", ds), ("m", ms), ("v", vs)):
        outs[kind].update(part)
    loss = (0.5 / D_MODEL) * jnp.sum(loss_rows)
    return loss, d_x, outs


def kernel(x, rel_bias, w_in, b_gate, ssm_conv_w, ssm_conv_b, ssm_dt_bias, ssm_a_log, ssm_d, ssm_norm_w, attn_sinks, w_branch_ssm, w_branch_attn, w_mix_out, ln1_g, ln1_b, w_up, ffn_conv_w, ffn_conv_b, w_down, ln2_g, ln2_b, loss_target, m_rel_bias, m_w_in, m_b_gate, m_ssm_conv_w, m_ssm_conv_b, m_ssm_dt_bias, m_ssm_a_log, m_ssm_d, m_ssm_norm_w, m_attn_sinks, m_w_branch_ssm, m_w_branch_attn, m_w_mix_out, m_ln1_g, m_ln1_b, m_w_up, m_ffn_conv_w, m_ffn_conv_b, m_w_down, m_ln2_g, m_ln2_b, v_rel_bias, v_w_in, v_b_gate, v_ssm_conv_w, v_ssm_conv_b, v_ssm_dt_bias, v_ssm_a_log, v_ssm_d, v_ssm_norm_w, v_attn_sinks, v_w_branch_ssm, v_w_branch_attn, v_w_mix_out, v_ln1_g, v_ln1_b, v_w_up, v_ffn_conv_w, v_ffn_conv_b, v_w_down, v_ln2_g, v_ln2_b):
    given = dict(locals())
    drop = lambda a, n: a if n == "rel_bias" or a.ndim == 2 else a[0]
    w = {n: drop(given[n], n) for n in _WEIGHTS}
    m = {n: drop(given["m_" + n], n) for n in _WEIGHTS}
    v = {n: drop(given["v_" + n], n) for n in _WEIGHTS}
    loss, d_x, outs = _step(x[0], loss_target[0], w, m, v)
    like = lambda a, n: a.reshape(given[n].shape)
    res = [loss, d_x[None]]
    for kind in ("grad", "delta", "m", "v"):
        res += [like(outs[kind][n], n) for n in _WEIGHTS]
    return tuple(res)
```

```python
import functools
import math

import numpy as np
import jax
import jax.numpy as jnp
from jax import lax
from jax.experimental import pallas as pl
from jax.experimental.pallas import tpu as pltpu

F32 = jnp.float32
_ACT = jnp.bfloat16
_MXU = jnp.bfloat16

D_MODEL = 1024
D_INNER = 2048
N_HEADS = 32
HEAD_P = 64
N_GROUPS = 4
N_STATE = 128
CHUNK = 128
CONV_DIM = 3072
SSM_K = 4
A_HEADS = 16
A_DH = 64
WIN = 128
REL_BUCKETS = 32
BIAS_ROWS = 64
D_FF = 2816
FFN_K = 3
ALPHA = 2.0 ** 0.25
LN_EPS = 1e-5
RMS_EPS = 1e-5
IN_COLS = 8480
NEG = -1e30

ADAM_LR = 0.001
ADAM_B1 = 0.9
ADAM_B2 = 0.999
ADAM_EPS = 1e-08
ADAM_WD = 0.01
ADAM_STEP = 10

LANE = 128
SUB = 8

P_Z, P_XS, P_G, P_Q, P_BC, P_K, P_V, P_DT = 0, 2048, 4096, 6144, 7168, 8192, 8320, 8448
P_W = 8704
P_MAIN = 8192
T_K, T_V, T_DT = P_K - P_MAIN, P_V - P_MAIN, P_DT - P_MAIN
_PIECES = ((0, 2048, P_Z), (2048, 2048, P_XS), (4096, 1024, P_BC), (5120, 32, P_DT), (5152, 1024, P_Q),
           (6176, 128, P_K), (6304, 128, P_V), (6432, 2048, P_G))

N_CHIPS = 4
N_DEV = 8


def _cp(sem=None, vmem_mb=48):
    return pltpu.CompilerParams(dimension_semantics=sem, vmem_limit_bytes=vmem_mb * 1024 * 1024)


def _pick(n, cands):
    for c in cands:
        if n % c == 0:
            return c
    raise ValueError(f"no block size for {n}")


def _rows8(p):
    k, c = p.shape
    return jnp.concatenate([p.astype(F32), jnp.zeros((SUB - k, c), F32)], axis=0)


def _mm(a, b, name, *, trans_a=False, out_dtype=F32, res=None, res_scale=1.0):
    if trans_a:
        k_dim, m = a.shape
    else:
        m, k_dim = a.shape
    k2, n = b.shape
    assert k_dim == k2, (a.shape, b.shape)
    tm = _pick(m, (1408, 1024, 512, 256, 128))
    tn = _pick(n, (1408, 1024, 512, 256, 128))
    tk = _pick(k_dim, (2816, 2176, 2048, 1024, 512, 256, 128))
    nk = k_dim // tk
    dn = (((0,), (0,)), ((), ())) if trans_a else (((1,), (0,)), ((), ()))

    def body(*refs):
        if res is None:
            a_ref, b_ref, o_ref = refs[:3]
        else:
            a_ref, b_ref, r_ref, o_ref = refs[:4]

        def finish(r):
            if res is not None:
                r = r + res_scale * r_ref[...]
            o_ref[...] = r.astype(out_dtype)

        part = lax.dot_general(a_ref[...].astype(_MXU), b_ref[...].astype(_MXU), dn, preferred_element_type=F32)
        if nk == 1:
            finish(part)
            return
        acc = refs[-1]
        k = pl.program_id(2)

        @pl.when(k == 0)
        def _():
            acc[...] = part

        @pl.when(k > 0)
        def _():
            acc[...] += part

        @pl.when(k == nk - 1)
        def _():
            finish(acc[...])

    if trans_a:
        a_spec = pl.BlockSpec((tk, tm), lambda i, j, k: (k, i))
    else:
        a_spec = pl.BlockSpec((tm, tk), lambda i, j, k: (i, k))
    in_specs = [a_spec, pl.BlockSpec((tk, tn), lambda i, j, k: (k, j))]
    args = [a, b]
    if res is not None:
        in_specs.append(pl.BlockSpec((tm, tn), lambda i, j, k: (i, j)))
        args.append(res)
    return pl.pallas_call(
        body, name=name, out_shape=jax.ShapeDtypeStruct((m, n), out_dtype),
        grid=(m // tm, n // tn, nk), in_specs=in_specs,
        out_specs=pl.BlockSpec((tm, tn), lambda i, j, k: (i, j)),
        scratch_shapes=[pltpu.VMEM((tm, tn), F32)] if nk > 1 else [],
        compiler_params=_cp(("parallel", "parallel", "arbitrary")),
    )(*args)


def _shift_down(cur, prev8, s):
    r = pltpu.roll(cur, s, 0)
    p = pltpu.roll(prev8, s, 0)
    row8 = lax.broadcasted_iota(jnp.int32, (SUB, 1), 0)
    fixed = jnp.where(row8 < s, p, r[0:SUB])
    if cur.shape[0] == SUB:
        return fixed
    return jnp.concatenate([fixed, r[SUB:]], axis=0)


def _shift_up(cur, next8, s):
    tm = cur.shape[0]
    r = pltpu.roll(cur, tm - s, 0)
    p = pltpu.roll(next8, SUB - s, 0)
    row8 = lax.broadcasted_iota(jnp.int32, (SUB, 1), 0)
    fixed = jnp.where(row8 >= SUB - s, p, r[tm - SUB:])
    return jnp.concatenate([r[:tm - SUB], fixed], axis=0)


def _conv_pre(cur, prev8, w_ref, b_row, taps):
    acc = cur * w_ref[taps - 1:taps, :] + b_row
    for s in range(1, taps):
        acc = acc + _shift_down(cur, prev8, s) * w_ref[taps - 1 - s:taps - s, :]
    return acc


def _dot01_r(x, m01, parts=3):
    acc = None
    r = x
    for _ in range(parts):
        hi = r.astype(jnp.bfloat16)
        t = jnp.dot(hi, m01, preferred_element_type=F32)
        acc = t if acc is None else acc + t
        r = r - hi.astype(F32)
    return acc


def _dot01_l(m01, x, parts=3):
    acc = None
    r = x
    for _ in range(parts):
        hi = r.astype(jnp.bfloat16)
        t = jnp.dot(m01, hi, preferred_element_type=F32)
        acc = t if acc is None else acc + t
        r = r - hi.astype(F32)
    return acc


def _dot(a, b):
    return jnp.dot(a.astype(_MXU), b.astype(_MXU), preferred_element_type=F32)


def _dot_nt(a, b):
    return lax.dot_general(a.astype(_MXU), b.astype(_MXU), (((1,), (1,)), ((), ())), preferred_element_type=F32)


def _dot_tn(a, b):
    return lax.dot_general(a.astype(_MXU), b.astype(_MXU), (((0,), (0,)), ((), ())), preferred_element_type=F32)


def _sigmoid(x):
    return 1.0 / (1.0 + jnp.exp(-x))


def _half_masks():
    lane = lax.broadcasted_iota(jnp.int32, (1, LANE), 1)
    lo = (lane < 64).astype(F32)
    return lo, 1.0 - lo


_TC = 512


def _tm_rows(t):
    return min(256, t)


HALO = 16


def _prev_halo(tm, width, pos):
    def index(*ids):
        i, col = pos(*ids)
        return (jnp.maximum(i * (tm // HALO) - 1, 0), col)
    return pl.BlockSpec((HALO, width), index)


def _next_halo(tm, t, width, pos):
    def index(*ids):
        i, col = pos(*ids)
        return (jnp.minimum((i + 1) * (tm // HALO), t // HALO - 1), col)
    return pl.BlockSpec((HALO, width), index)


def _conv_silu_fwd(proj, colblk0, nblk, w8, b8, name):
    t = proj.shape[0]
    tm = _tm_rows(t)

    def body(c_ref, p_ref, w_ref, b_ref, o_ref, pre_ref):
        i = pl.program_id(1)
        prev8 = jnp.where(i > 0, p_ref[SUB:HALO, :].astype(F32), 0.0)
        pre = _conv_pre(c_ref[...].astype(F32), prev8, w_ref, b_ref[0:1, :], SSM_K)
        o_ref[...] = pre * _sigmoid(pre)
        pre_ref[...] = pre.astype(_ACT)

    blk = pl.BlockSpec((tm, _TC), lambda j, i: (i, j))
    return pl.pallas_call(
        body, name=name,
        out_shape=(jax.ShapeDtypeStruct((t, nblk * _TC), F32), jax.ShapeDtypeStruct((t, nblk * _TC), _ACT)),
        grid=(nblk, t // tm),
        in_specs=[pl.BlockSpec((tm, _TC), lambda j, i: (i, colblk0 + j)),
                  _prev_halo(tm, _TC, lambda j, i: (i, colblk0 + j)),
                  pl.BlockSpec((SUB, _TC), lambda j, i: (0, j)),
                  pl.BlockSpec((SUB, _TC), lambda j, i: (0, j))],
        out_specs=(blk, blk),
        compiler_params=_cp(("parallel", "parallel")),
    )(proj, proj, w8, b8)


def _silu_grad(pre):
    sg = _sigmoid(pre)
    return sg * (1.0 + pre * (1.0 - sg))


def _conv_grads(d, d_next8, cur, w_ref, acc_ref, taps, cols=slice(None)):
    du = d * w_ref[taps - 1:taps, :]
    acc_ref[taps:taps + 1, cols] += jnp.sum(d, axis=0, keepdims=True)
    acc_ref[taps - 1:taps, cols] += jnp.sum(d * cur, axis=0, keepdims=True)
    for s in range(1, taps):
        up = _shift_up(d, d_next8, s)
        du = du + up * w_ref[taps - 1 - s:taps - s, :]
        acc_ref[taps - 1 - s:taps - s, cols] += jnp.sum(up * cur, axis=0, keepdims=True)
    return du


def _conv_silu_bwd(proj, colblk0, nblk, pre, w8, d_out, name):
    t = proj.shape[0]
    tm = _tm_rows(t)
    nt = t // tm

    def body(c_ref, pre_ref, pren_ref, w_ref, d_ref, dn_ref, du_ref, acc_ref):
        i = pl.program_id(1)

        @pl.when(i == 0)
        def _():
            acc_ref[...] = jnp.zeros_like(acc_ref)

        dpre = d_ref[...].astype(F32) * _silu_grad(pre_ref[...].astype(F32))
        dpre_n = jnp.where(i < nt - 1, dn_ref[0:SUB, :].astype(F32) * _silu_grad(pren_ref[0:SUB, :].astype(F32)), 0.0)
        du_ref[...] = _conv_grads(dpre, dpre_n, c_ref[...].astype(F32), w_ref, acc_ref, SSM_K).astype(_ACT)

    c = nblk * _TC
    blk = pl.BlockSpec((tm, _TC), lambda j, i: (i, j))
    nxt = _next_halo(tm, t, _TC, lambda j, i: (i, j))
    par = pl.BlockSpec((SUB, _TC), lambda j, i: (0, j))
    return pl.pallas_call(
        body, name=name,
        out_shape=(jax.ShapeDtypeStruct((t, c), _ACT), jax.ShapeDtypeStruct((SUB, c), F32)),
        grid=(nblk, nt),
        in_specs=[pl.BlockSpec((tm, _TC), lambda j, i: (i, colblk0 + j)), blk, nxt, par, blk, nxt],
        out_specs=(blk, par),
        compiler_params=_cp(("parallel", "arbitrary")),
    )(proj, pre, pre, w8, d_out, d_out)


def _expand_consts():
    e = np.zeros((LANE, D_INNER), np.float32)
    for h in range(N_HEADS):
        e[h, h * HEAD_P:(h + 1) * HEAD_P] = 1.0
    return jnp.asarray(e, jnp.bfloat16), jnp.asarray(e.T.copy(), jnp.bfloat16)


def _ssd_common(dtr_ref, dtb_ref, alog_ref, e_ref):
    lane = lax.broadcasted_iota(jnp.int32, (1, LANE), 1)
    hm = lane < N_HEADS
    pre = dtr_ref[...] + dtb_ref[0:1, :]
    dt = jnp.where(hm, jnp.maximum(pre, 0.0) + jnp.log(1.0 + jnp.exp(-jnp.abs(pre))), 0.0)
    a_row = jnp.where(hm, -jnp.exp(alog_ref[0:1, :]), 0.0)
    adt = dt * a_row
    r = lax.broadcasted_iota(jnp.int32, (CHUNK, CHUNK), 0)
    c = lax.broadcasted_iota(jnp.int32, (CHUNK, CHUNK), 1)
    causal = r >= c
    acs = _dot01_l(causal.astype(jnp.bfloat16), adt)
    e = e_ref[...]
    acs_x = _dot01_r(acs, e, parts=2)
    dt_x = _dot01_r(dt, e, parts=2)
    return pre, dt, a_row, acs, acs_x, dt_x, causal, hm


def _decay(acs, acs_t, h, causal):
    seg = acs[:, h:h + 1] - acs_t[h:h + 1, :]
    return jnp.exp(jnp.where(causal, seg, NEG))


def _ssd_fwd(xs_c, bc_c, proj, tail, dtb8, alog8, dsk8, nw8, name):
    t = xs_c.shape[0]
    nc = t // CHUNK
    e_bf, _ = _expand_consts()
    gw = D_INNER // N_GROUPS

    def body(xs_ref, bc_ref, dtr_ref, z_ref, dtb_ref, alog_ref, dsk_ref, nw_ref, e_ref,
             y_ref, ys_ref, hp_ref, h_ref):
        c_id = pl.program_id(0)

        @pl.when(c_id == 0)
        def _():
            h_ref[...] = jnp.zeros_like(h_ref)

        _, dt, a_row, acs, acs_x, dt_x, causal, _ = _ssd_common(dtr_ref, dtb_ref, alog_ref, e_ref)
        acs_t = acs.T
        xs = xs_ref[...]
        x_dt = xs * dt_x
        last_x = acs_x[CHUNK - 1:CHUNK, :]
        w_end = jnp.exp(last_x - acs_x)
        e_in = jnp.exp(acs_x)
        d_x = _dot01_r(dsk_ref[...], e_ref[...])[0:1, :]
        hprev = h_ref[...]
        hp_ref[...] = hprev
        lo, hi = _half_masks()
        for g in range(N_GROUPS):
            bg = bc_ref[:, g * N_STATE:(g + 1) * N_STATE]
            cg = bc_ref[:, N_GROUPS * N_STATE + g * N_STATE:N_GROUPS * N_STATE + (g + 1) * N_STATE]
            sl = slice(g * gw, (g + 1) * gw)
            gm = _dot_nt(cg, bg)
            st = _dot(bg.T, x_dt[:, sl] * w_end[:, sl])
            y_off = _dot(cg, hprev[:, sl]) * e_in[:, sl]
            for j in range(gw // LANE):
                h0 = g * (gw // HEAD_P) + 2 * j
                cs = slice(g * gw + j * LANE, g * gw + (j + 1) * LANE)
                xp = x_dt[:, cs]
                m0 = gm * _decay(acs, acs_t, h0, causal)
                m1 = gm * _decay(acs, acs_t, h0 + 1, causal)
                yd = _dot(m0, xp * lo) + _dot(m1, xp * hi)
                y_ref[:, cs] = yd + y_off[:, j * LANE:(j + 1) * LANE] + xs[:, cs] * d_x[:, cs]
            h_ref[:, sl] = hprev[:, sl] * jnp.exp(last_x[:, sl]) + st
        y = y_ref[...]
        z = z_ref[...].astype(F32)
        y2 = y * (z * _sigmoid(z))
        for g in range(N_GROUPS):
            sl = slice(g * gw, (g + 1) * gw)
            yg = y2[:, sl]
            rinv = lax.rsqrt(jnp.mean(yg * yg, axis=-1, keepdims=True) + RMS_EPS)
            ys_ref[:, sl] = (yg * rinv * nw_ref[0:1, sl]).astype(_ACT)

    small = pl.BlockSpec((SUB, LANE), lambda c: (0, 0))
    return pl.pallas_call(
        body, name=name,
        out_shape=(jax.ShapeDtypeStruct((t, D_INNER), F32), jax.ShapeDtypeStruct((t, D_INNER), _ACT),
                   jax.ShapeDtypeStruct((t, D_INNER), F32)),
        grid=(nc,),
        in_specs=[pl.BlockSpec((CHUNK, D_INNER), lambda c: (c, 0)),
                  pl.BlockSpec((CHUNK, 1024), lambda c: (c, 0)),
                  pl.BlockSpec((CHUNK, LANE), lambda c: (c, T_DT // LANE)),
                  pl.BlockSpec((CHUNK, D_INNER), lambda c: (c, P_Z // D_INNER)),
                  small, small, small,
                  pl.BlockSpec((SUB, D_INNER), lambda c: (0, 0)),
                  pl.BlockSpec((LANE, D_INNER), lambda c: (0, 0))],
        out_specs=(pl.BlockSpec((CHUNK, D_INNER), lambda c: (c, 0)),
                   pl.BlockSpec((CHUNK, D_INNER), lambda c: (c, 0)),
                   pl.BlockSpec((N_STATE, D_INNER), lambda c: (c, 0))),
        scratch_shapes=[pltpu.VMEM((N_STATE, D_INNER), F32)],
        compiler_params=_cp(("arbitrary",)),
    )(xs_c, bc_c, tail, proj, dtb8, alog8, dsk8, nw8, e_bf)


def _ssd_bwd(d_ys, y, xs_c, bc_c, proj, tail, hprev_all, dtb8, alog8, dsk8, nw8, name):
    t = xs_c.shape[0]
    nc = t // CHUNK
    e_bf, et_bf = _expand_consts()
    gw = D_INNER // N_GROUPS

    def body(dys_ref, y_ref, xs_ref, bc_ref, dtr_ref, z_ref, hp_ref, dtb_ref, alog_ref, dsk_ref, nw_ref,
             e_ref, et_ref, dxs_ref, dbc_ref, dz_ref, ddt_ref, acc_ref, dnw_ref, dh_ref, dx_ref):
        step = pl.program_id(0)

        @pl.when(step == 0)
        def _():
            dh_ref[...] = jnp.zeros_like(dh_ref)
            acc_ref[...] = jnp.zeros_like(acc_ref)
            dnw_ref[...] = jnp.zeros_like(dnw_ref)

        pre, dt, a_row, acs, acs_x, dt_x, causal, hm = _ssd_common(dtr_ref, dtb_ref, alog_ref, e_ref)
        acs_t = acs.T
        et = et_ref[...]
        xs = xs_ref[...]
        x_dt = xs * dt_x
        last_x = acs_x[CHUNK - 1:CHUNK, :]
        w_end = jnp.exp(last_x - acs_x)
        e_in = jnp.exp(acs_x)
        e_last = jnp.exp(last_x)
        d_x = _dot01_r(dsk_ref[...], e_ref[...])[0:1, :]

        y = y_ref[...]
        z = z_ref[...].astype(F32)
        sz = _sigmoid(z)
        gz = z * sz
        y2 = y * gz
        dys = dys_ref[...].astype(F32)
        for g in range(N_GROUPS):
            sl = slice(g * gw, (g + 1) * gw)
            yg = y2[:, sl]
            rinv = lax.rsqrt(jnp.mean(yg * yg, axis=-1, keepdims=True) + RMS_EPS)
            nrm = yg * rinv
            dn = dys[:, sl] * nw_ref[0:1, sl]
            dnw_ref[0:1, sl] += jnp.sum(dys[:, sl] * nrm, axis=0, keepdims=True)
            dx_ref[:, sl] = rinv * (dn - nrm * jnp.mean(dn * nrm, axis=-1, keepdims=True))
        dy2 = dx_ref[...]
        dy = dy2 * gz
        dz_ref[...] = (dy2 * y * (sz * (1.0 + z * (1.0 - sz)))).astype(_ACT)

        dh_next = dh_ref[...]
        hprev = hp_ref[...]
        lo, hi = _half_masks()
        r = lax.broadcasted_iota(jnp.int32, (CHUNK, CHUNK), 0)
        c = lax.broadcasted_iota(jnp.int32, (CHUNK, CHUNK), 1)
        from_here = (c >= r).astype(jnp.bfloat16)
        before = c < r
        lane = lax.broadcasted_iota(jnp.int32, (1, LANE), 1)
        da_intra = jnp.zeros((CHUNK, LANE), F32)
        v_seg = jnp.zeros((CHUNK, LANE), F32)
        z_seg = jnp.zeros((CHUNK, LANE), F32)
        tail_parts = []
        for g in range(N_GROUPS):
            bg = bc_ref[:, g * N_STATE:(g + 1) * N_STATE]
            cg = bc_ref[:, N_GROUPS * N_STATE + g * N_STATE:N_GROUPS * N_STATE + (g + 1) * N_STATE]
            sl = slice(g * gw, (g + 1) * gw)
            et_g = et_ref[g * gw:(g + 1) * gw, :]
            gm = _dot_nt(cg, bg)
            dzg = e_in[:, sl] * dy[:, sl]
            dcg = _dot_nt(dzg, hprev[:, sl])
            dh_c = _dot(cg.T, dzg)
            q = _dot(bg, dh_next[:, sl])
            dbg = _dot_nt(x_dt[:, sl] * w_end[:, sl], dh_next[:, sl])
            y_off = _dot(cg, hprev[:, sl]) * e_in[:, sl]
            v_seg = v_seg + _dot01_r(dy[:, sl] * y_off, et_g, parts=2)
            z_seg = z_seg + _dot01_r(w_end[:, sl] * q * x_dt[:, sl], et_g, parts=2)
            dgm = jnp.zeros((CHUNK, CHUNK), F32)
            for j in range(gw // LANE):
                h0 = g * (gw // HEAD_P) + 2 * j
                cs = slice(g * gw + j * LANE, g * gw + (j + 1) * LANE)
                xp = x_dt[:, cs]
                dyp = dy[:, cs]
                dxd = jnp.zeros((CHUNK, LANE), F32)
                for half, msk in ((0, lo), (1, hi)):
                    lam = _decay(acs, acs_t, h0 + half, causal)
                    mm = gm * lam
                    dym = dyp * msk
                    dmm = _dot_nt(dym, xp)
                    dxd = dxd + _dot_tn(mm, dym)
                    dgm = dgm + dmm * lam
                    below = _dot(from_here, dmm * mm)
                    col = jnp.sum(jnp.where(before, below, 0.0), axis=-1, keepdims=True)
                    da_intra = da_intra + jnp.where(lane == h0 + half, col, 0.0)
                dx_ref[:, cs] = dxd + w_end[:, cs] * q[:, j * LANE:(j + 1) * LANE]
            dbc_ref[:, N_GROUPS * N_STATE + g * N_STATE:N_GROUPS * N_STATE + (g + 1) * N_STATE] = dcg + _dot(dgm, bg)
            dbc_ref[:, g * N_STATE:(g + 1) * N_STATE] = dbg + _dot_tn(dgm, cg)
            dh_ref[:, sl] = e_last[:, sl] * dh_next[:, sl] + dh_c
            tail_parts.append(e_last[:, sl] * jnp.sum(dh_next[:, sl] * hprev[:, sl], axis=0, keepdims=True))
        dxt = dx_ref[...]

        u_seg = _dot01_r(xs * dxt, et, parts=2)
        q_full = jnp.concatenate(tail_parts, axis=1)
        t_row = _dot01_r(jnp.broadcast_to(q_full, (SUB, D_INNER)), et)[0:1, :]
        d_alpha = (da_intra + _dot01_l(from_here, v_seg) + _dot01_l(before.astype(jnp.bfloat16), z_seg) + t_row)
        d_dt = a_row * d_alpha + u_seg
        sgp = _sigmoid(pre)
        d_raw = jnp.where(hm, d_dt * sgp, 0.0)
        ddt_ref[...] = d_raw.astype(_ACT)
        acc_ref[0:1, :] += jnp.sum(d_raw, axis=0, keepdims=True)
        acc_ref[1:2, :] += jnp.sum(d_alpha * dt, axis=0, keepdims=True) * a_row
        dd_row = jnp.sum(dy * xs, axis=0, keepdims=True)
        acc_ref[2:3, :] += _dot01_r(jnp.broadcast_to(dd_row, (SUB, D_INNER)), et)[0:1, :]
        dxs_ref[...] = dy * d_x + dxt * dt_x

    rev = lambda c: (nc - 1 - c, 0)
    small = pl.BlockSpec((SUB, LANE), lambda c: (0, 0))
    return pl.pallas_call(
        body, name=name,
        out_shape=(jax.ShapeDtypeStruct((t, D_INNER), F32), jax.ShapeDtypeStruct((t, 1024), F32),
                   jax.ShapeDtypeStruct((t, D_INNER), _ACT), jax.ShapeDtypeStruct((t, LANE), _ACT),
                   jax.ShapeDtypeStruct((SUB, LANE), F32), jax.ShapeDtypeStruct((SUB, D_INNER), F32)),
        grid=(nc,),
        in_specs=[pl.BlockSpec((CHUNK, D_INNER), rev),
                  pl.BlockSpec((CHUNK, D_INNER), rev),
                  pl.BlockSpec((CHUNK, D_INNER), rev),
                  pl.BlockSpec((CHUNK, 1024), rev),
                  pl.BlockSpec((CHUNK, LANE), lambda c: (nc - 1 - c, T_DT // LANE)),
                  pl.BlockSpec((CHUNK, D_INNER), lambda c: (nc - 1 - c, P_Z // D_INNER)),
                  pl.BlockSpec((N_STATE, D_INNER), rev),
                  small, small, small,
                  pl.BlockSpec((SUB, D_INNER), lambda c: (0, 0)),
                  pl.BlockSpec((LANE, D_INNER), lambda c: (0, 0)),
                  pl.BlockSpec((D_INNER, LANE), lambda c: (0, 0))],
        out_specs=(pl.BlockSpec((CHUNK, D_INNER), rev),
                   pl.BlockSpec((CHUNK, 1024), rev),
                   pl.BlockSpec((CHUNK, D_INNER), rev),
                   pl.BlockSpec((CHUNK, LANE), rev),
                   small,
                   pl.BlockSpec((SUB, D_INNER), lambda c: (0, 0))),
        scratch_shapes=[pltpu.VMEM((N_STATE, D_INNER), F32), pltpu.VMEM((CHUNK, D_INNER), F32)],
        compiler_params=_cp(("arbitrary",), vmem_mb=56),
    )(d_ys, y, xs_c, bc_c, tail, proj, hprev_all, dtb8, alog8, dsk8, nw8, e_bf, et_bf)


def _rel_tables():
    qi = np.arange(WIN)[:, None] + WIN
    kj = np.arange(2 * WIN)[None, :]
    rel = qi - kj
    n = np.maximum(rel, 0)
    max_exact = REL_BUCKETS // 2
    nf = np.maximum(n, 1).astype(np.float32)
    large = max_exact + (np.log(nf / np.float32(max_exact)) / np.float32(math.log(WIN / max_exact))
                         * np.float32(REL_BUCKETS - max_exact)).astype(np.int32)
    large = np.minimum(large, REL_BUCKETS - 1)
    bucket = np.where(n < max_exact, n, large)
    valid = (rel >= 0) & (rel < WIN)
    sink_col = np.broadcast_to(kj == 0, rel.shape)
    onehot = np.zeros((BIAS_ROWS, WIN * 2 * WIN), np.float32)
    flat_b = np.where(sink_col, REL_BUCKETS, bucket).reshape(-1)
    flat_v = (valid | sink_col).reshape(-1)
    first_v = ((valid & (kj >= WIN)) | sink_col).reshape(-1)
    idx = np.arange(WIN * 2 * WIN)
    onehot[flat_b[flat_v], idx[flat_v]] = 1.0
    return onehot, np.stack([first_v, flat_v]).astype(np.float32)


def _bias_expand(table_t, name):
    onehot, valid = _rel_tables()

    def body(rb_ref, oh_ref, v_ref, o_ref):
        full = _dot01_r(rb_ref[...], oh_ref[...])
        o_ref[0] = jnp.where(v_ref[0:1, :] > 0.5, full, NEG)
        o_ref[1] = jnp.where(v_ref[1:2, :] > 0.5, full, NEG)

    return pl.pallas_call(
        body, name=name, out_shape=jax.ShapeDtypeStruct((2, A_HEADS, WIN * 2 * WIN), F32),
        compiler_params=_cp(None),
    )(table_t, jnp.asarray(onehot, jnp.bfloat16), jnp.asarray(valid, F32))


def _bias_reduce(dbias, name):
    onehot, _ = _rel_tables()

    def body(d_ref, oh_ref, o_ref):
        acc = None
        r = d_ref[...]
        for _ in range(3):
            hi = r.astype(jnp.bfloat16)
            tt = lax.dot_general(hi, oh_ref[...], (((1,), (1,)), ((), ())), preferred_element_type=F32)
            acc = tt if acc is None else acc + tt
            r = r - hi.astype(F32)
        o_ref[...] = acc

    return pl.pallas_call(
        body, name=name, out_shape=jax.ShapeDtypeStruct((A_HEADS, BIAS_ROWS), F32),
        compiler_params=_cp(None),
    )(dbias, jnp.asarray(onehot, jnp.bfloat16))


def _attn_bands(kc_ref, kp_ref, vc_ref, vp_ref, has_prev):
    lo, hi = _half_masks()
    row = lax.broadcasted_iota(jnp.int32, (2 * WIN, 1), 0)
    keep = (row > 0).astype(F32)
    kb = jnp.concatenate([jnp.where(has_prev, kp_ref[...], 0.0), kc_ref[...]], axis=0) * (keep * (A_DH ** -0.5))
    vb = jnp.concatenate([jnp.where(has_prev, vp_ref[...], 0.0), vc_ref[...]], axis=0) * keep
    kr = pltpu.roll(kb, 64, 1)
    vr = pltpu.roll(vb, 64, 1)
    kk = ((kb * lo, kr * hi), (kr * lo, kb * hi))
    vv = ((vb * lo, vr * hi), (vr * lo, vb * hi))
    return kk, vv, (hi, lo)


def _attn_logits(q_ref, kk, lg_ref):
    for h in range(A_HEADS):
        j, half, kv = h // 2, h % 2, h // (A_HEADS // 2)
        lg_ref[h] = _dot_nt(q_ref[:, j * LANE:(j + 1) * LANE], kk[kv][half])


def _attn_fwd(proj, tail, bias, name):
    t = proj.shape[0]
    nb = t // WIN

    def body(q_ref, kc_ref, kp_ref, vc_ref, vp_ref, b_ref, o_ref, lg_ref, p_ref):
        n = pl.program_id(0)
        kk, vv, ones = _attn_bands(kc_ref, kp_ref, vc_ref, vp_ref, n > 0)
        _attn_logits(q_ref, kk, lg_ref)
        for h in range(A_HEADS):
            logits = lg_ref[h] + b_ref[h]
            p_ref[h] = jnp.exp(logits - jnp.max(logits, axis=-1, keepdims=True)).astype(_MXU)
        lane = lax.broadcasted_iota(jnp.int32, (1, LANE), 1)
        for j in range(A_HEADS // 2):
            kv = (2 * j) // (A_HEADS // 2)
            outs = []
            for half in range(2):
                o = jnp.dot(p_ref[2 * j + half], (vv[kv][half] + ones[half]).astype(_MXU), preferred_element_type=F32)
                outs.append(o / pltpu.roll(o, 64, 1))
            o_ref[:, j * LANE:(j + 1) * LANE] = jnp.where(lane < 64, outs[0], outs[1]).astype(_ACT)

    kvspec = lambda col, prev: pl.BlockSpec(
        (WIN, LANE), (lambda n: (jnp.maximum(n - 1, 0), col)) if prev else (lambda n: (n, col)))
    return pl.pallas_call(
        body, name=name, out_shape=jax.ShapeDtypeStruct((t, D_MODEL), _ACT),
        grid=(nb,),
        in_specs=[pl.BlockSpec((WIN, 1024), lambda n: (n, P_Q // 1024)),
                  kvspec(T_K // LANE, False), kvspec(T_K // LANE, True),
                  kvspec(T_V // LANE, False), kvspec(T_V // LANE, True),
                  pl.BlockSpec((None, A_HEADS, WIN, 2 * WIN), lambda n: (jnp.minimum(n, 1), 0, 0, 0))],
        out_specs=pl.BlockSpec((WIN, 1024), lambda n: (n, 0)),
        scratch_shapes=[pltpu.VMEM((A_HEADS, WIN, 2 * WIN), F32), pltpu.VMEM((A_HEADS, WIN, 2 * WIN), _MXU)],
        compiler_params=_cp(("parallel",)),
    )(proj, tail, tail, tail, tail, bias)


def _attn_bwd(proj, tail, bias, y_attn, d_out, name):
    t = proj.shape[0]
    nb = t // WIN

    def body(q_ref, kc_ref, kp_ref, vc_ref, vp_ref, b_ref, y_ref, do_ref,
             dq_ref, dk_ref, dv_ref, db_ref, ck_ref, cv_ref, lg_ref, dl_ref, p_ref):
        n = pl.program_id(0)

        @pl.when(n == 0)
        def _():
            db_ref[...] = jnp.zeros_like(db_ref)
            ck_ref[...] = jnp.zeros_like(ck_ref)
            cv_ref[...] = jnp.zeros_like(cv_ref)

        @pl.when(n < nb)
        def _():
            kk, vv, _ = _attn_bands(kc_ref, kp_ref, vc_ref, vp_ref, n > 0)
            lo, hi = _half_masks()
            ones_k = jnp.ones((2 * WIN, LANE), jnp.bfloat16)
            ones_d = jnp.ones((LANE, LANE), jnp.bfloat16)
            _attn_logits(q_ref, kk, lg_ref)
            for h in range(A_HEADS):
                j, half, kv = h // 2, h % 2, h // (A_HEADS // 2)
                msk = hi if half else lo
                logits = lg_ref[h] + b_ref[h]
                p = jnp.exp(logits - jnp.max(logits, axis=-1, keepdims=True))
                den = jnp.dot(p.astype(_MXU), ones_k.astype(_MXU), preferred_element_type=F32)
                dop = do_ref[:, j * LANE:(j + 1) * LANE].astype(F32)
                delta = _dot01_r(dop * y_ref[:, j * LANE:(j + 1) * LANE].astype(F32) * msk, ones_d, parts=2)
                inv = 1.0 / den
                probs = p * jnp.concatenate([inv, inv], axis=1)
                dprobs = _dot_nt(dop, vv[kv][half])
                dlog = probs * (dprobs - jnp.concatenate([delta, delta], axis=1))
                db_ref[h] += dlog
                dl_ref[h] = dlog.astype(_MXU)
                p_ref[h] = probs.astype(_MXU)
            dk_t = [[None, None], [None, None]]
            dv_t = [[None, None], [None, None]]
            for j in range(A_HEADS // 2):
                kv = (2 * j) // (A_HEADS // 2)
                qs = q_ref[:, j * LANE:(j + 1) * LANE].astype(F32) * (A_DH ** -0.5)
                dop = do_ref[:, j * LANE:(j + 1) * LANE].astype(F32)
                dq = None
                for half, msk in ((0, lo), (1, hi)):
                    h = 2 * j + half
                    dqh = jnp.dot(dl_ref[h], kk[kv][half].astype(_MXU), preferred_element_type=F32)
                    dq = dqh if dq is None else dq + dqh
                    dkh = lax.dot_general((qs * msk).astype(_MXU), dl_ref[h], (((0,), (0,)), ((), ())),
                                          preferred_element_type=F32)
                    dvh = lax.dot_general((dop * msk).astype(_MXU), p_ref[h], (((0,), (0,)), ((), ())),
                                          preferred_element_type=F32)
                    dk_t[kv][half] = dkh if dk_t[kv][half] is None else dk_t[kv][half] + dkh
                    dv_t[kv][half] = dvh if dv_t[kv][half] is None else dv_t[kv][half] + dvh
                dq_ref[:, j * LANE:(j + 1) * LANE] = dq.astype(_ACT)
            row = lax.broadcasted_iota(jnp.int32, (2 * WIN, 1), 0)

            def band(acc):
                a = (acc[0][0] + pltpu.roll(acc[0][1], 64, 0)) + (pltpu.roll(acc[1][0], 64, 0) + acc[1][1])
                return jnp.where(row > 0, a.T, 0.0)

            dkb = band(dk_t)
            dvb = band(dv_t)
            dk_ref[...] = (ck_ref[...] + dkb[0:WIN]).astype(_ACT)
            dv_ref[...] = (cv_ref[...] + dvb[0:WIN]).astype(_ACT)
            ck_ref[...] = dkb[WIN:]
            cv_ref[...] = dvb[WIN:]

        @pl.when(n == nb)
        def _():
            dk_ref[...] = ck_ref[...].astype(_ACT)
            dv_ref[...] = cv_ref[...].astype(_ACT)

    cur = lambda n: jnp.minimum(n, nb - 1)
    prv = lambda n: jnp.maximum(jnp.minimum(n, nb - 1) - 1, 0)
    kvspec = lambda col, prev: pl.BlockSpec(
        (WIN, LANE), (lambda n: (prv(n), col)) if prev else (lambda n: (cur(n), col)))
    band_shape = (A_HEADS, WIN, 2 * WIN)
    return pl.pallas_call(
        body, name=name,
        out_shape=(jax.ShapeDtypeStruct((t, D_MODEL), _ACT), jax.ShapeDtypeStruct((t, LANE), _ACT),
                   jax.ShapeDtypeStruct((t, LANE), _ACT), jax.ShapeDtypeStruct(band_shape, F32)),
        grid=(nb + 1,),
        in_specs=[pl.BlockSpec((WIN, 1024), lambda n: (cur(n), P_Q // 1024)),
                  kvspec(T_K // LANE, False), kvspec(T_K // LANE, True),
                  kvspec(T_V // LANE, False), kvspec(T_V // LANE, True),
                  pl.BlockSpec((None,) + band_shape, lambda n: (jnp.minimum(n, 1), 0, 0, 0)),
                  pl.BlockSpec((WIN, 1024), lambda n: (cur(n), 0)),
                  pl.BlockSpec((WIN, 1024), lambda n: (cur(n), 0))],
        out_specs=(pl.BlockSpec((WIN, 1024), lambda n: (cur(n), 0)),
                   pl.BlockSpec((WIN, LANE), lambda n: (jnp.maximum(n - 1, 0), 0)),
                   pl.BlockSpec((WIN, LANE), lambda n: (jnp.maximum(n - 1, 0), 0)),
                   pl.BlockSpec(band_shape, lambda n: (0, 0, 0))),
        scratch_shapes=[pltpu.VMEM((WIN, LANE), F32), pltpu.VMEM((WIN, LANE), F32),
                        pltpu.VMEM(band_shape, F32), pltpu.VMEM(band_shape, _MXU), pltpu.VMEM(band_shape, _MXU)],
        compiler_params=_cp(("arbitrary",)),
    )(proj, tail, tail, tail, tail, bias, y_attn, d_out)


def _merge_fwd(bs, ba, proj, bg8, name):
    t = bs.shape[0]
    tm = _tm_rows(t)

    def body(bs_ref, ba_ref, gs_ref, ga_ref, bgs_ref, bga_ref, o_ref):
        g_s = _sigmoid(gs_ref[...] + bgs_ref[0:1, :])
        g_a = _sigmoid(ga_ref[...] + bga_ref[0:1, :])
        o_ref[...] = (g_s * bs_ref[...] + g_a * ba_ref[...]).astype(_ACT)

    row = lambda col: pl.BlockSpec((tm, 1024), lambda i: (i, col))
    return pl.pallas_call(
        body, name=name, out_shape=jax.ShapeDtypeStruct((t, D_MODEL), _ACT), grid=(t // tm,),
        in_specs=[row(0), row(0), row(P_G // 1024), row(P_G // 1024 + 1),
                  pl.BlockSpec((SUB, 1024), lambda i: (0, 0)), pl.BlockSpec((SUB, 1024), lambda i: (0, 1))],
        out_specs=row(0), compiler_params=_cp(("parallel",)),
    )(bs, ba, proj, proj, bg8, bg8)


def _merge_bwd(d_merged, bs, ba, proj, bg8, name):
    t = bs.shape[0]
    tm = _tm_rows(t)

    def body(dm_ref, bs_ref, ba_ref, gs_ref, ga_ref, bgs_ref, bga_ref, dbs_ref, dba_ref, dg_ref, acc_ref):
        @pl.when(pl.program_id(0) == 0)
        def _():
            acc_ref[...] = jnp.zeros_like(acc_ref)

        dm = dm_ref[...].astype(F32)
        g_s = _sigmoid(gs_ref[...] + bgs_ref[0:1, :])
        g_a = _sigmoid(ga_ref[...] + bga_ref[0:1, :])
        dbs_ref[...] = (dm * g_s).astype(_ACT)
        dba_ref[...] = (dm * g_a).astype(_ACT)
        dgs = dm * bs_ref[...].astype(F32) * g_s * (1.0 - g_s)
        dga = dm * ba_ref[...].astype(F32) * g_a * (1.0 - g_a)
        dg_ref[:, 0:1024] = dgs.astype(_ACT)
        dg_ref[:, 1024:2048] = dga.astype(_ACT)
        acc_ref[0:1, 0:1024] += jnp.sum(dgs, axis=0, keepdims=True)
        acc_ref[0:1, 1024:2048] += jnp.sum(dga, axis=0, keepdims=True)

    row = lambda col: pl.BlockSpec((tm, 1024), lambda i: (i, col))
    return pl.pallas_call(
        body, name=name,
        out_shape=(jax.ShapeDtypeStruct((t, D_MODEL), _ACT), jax.ShapeDtypeStruct((t, D_MODEL), _ACT),
                   jax.ShapeDtypeStruct((t, 2048), _ACT), jax.ShapeDtypeStruct((SUB, 2048), F32)),
        grid=(t // tm,),
        in_specs=[row(0), row(0), row(0), row(P_G // 1024), row(P_G // 1024 + 1),
                  pl.BlockSpec((SUB, 1024), lambda i: (0, 0)), pl.BlockSpec((SUB, 1024), lambda i: (0, 1))],
        out_specs=(row(0), row(0), pl.BlockSpec((tm, 2048), lambda i: (i, 0)),
                   pl.BlockSpec((SUB, 2048), lambda i: (0, 0))),
        compiler_params=_cp(("arbitrary",)),
    )(d_merged, bs, ba, proj, proj, bg8, bg8)


def _ln_stats(r):
    mu = jnp.mean(r, axis=-1, keepdims=True)
    xc = r - mu
    var = jnp.mean(xc * xc, axis=-1, keepdims=True)
    rstd = lax.rsqrt(var + LN_EPS)
    return xc * rstd, rstd


def _ln_bwd(dxhat, xhat, rstd):
    return rstd * (dxhat - jnp.mean(dxhat, axis=-1, keepdims=True)
                   - xhat * jnp.mean(dxhat * xhat, axis=-1, keepdims=True))


def _ln1_fwd(x, mix, g8, b8, name):
    t = x.shape[0]
    tm = _tm_rows(t)

    def body(x_ref, m_ref, g_ref, b_ref, xh_ref, h_ref, rs_ref):
        xhat, rstd = _ln_stats(ALPHA * x_ref[...] + m_ref[...])
        xh_ref[...] = xhat
        h_ref[...] = (xhat * g_ref[0:1, :] + b_ref[0:1, :]).astype(_ACT)
        rs_ref[...] = rstd

    row = pl.BlockSpec((tm, D_MODEL), lambda i: (i, 0))
    par = pl.BlockSpec((SUB, D_MODEL), lambda i: (0, 0))
    return pl.pallas_call(
        body, name=name,
        out_shape=(jax.ShapeDtypeStruct((t, D_MODEL), F32), jax.ShapeDtypeStruct((t, D_MODEL), _ACT),
                   jax.ShapeDtypeStruct((t, 1), F32)),
        grid=(t // tm,), in_specs=[row, row, par, par],
        out_specs=(row, row, pl.BlockSpec((tm, 1), lambda i: (i, 0))),
        compiler_params=_cp(("parallel",)),
    )(x, mix, g8, b8)


def _ln2_loss(xhat1, ffn, target, g1_8, b1_8, g2_8, b2_8, name):
    t = xhat1.shape[0]
    tm = _tm_rows(t)

    def body(xh_ref, f_ref, t_ref, g1_ref, b1_ref, g2_ref, b2_ref, d_ref, db_ref, acc_ref):
        @pl.when(pl.program_id(0) == 0)
        def _():
            acc_ref[...] = jnp.zeros_like(acc_ref)

        h1 = xh_ref[...] * g1_ref[0:1, :] + b1_ref[0:1, :]
        xhat, rstd = _ln_stats(ALPHA * h1 + f_ref[...])
        diff = xhat * g2_ref[0:1, :] + b2_ref[0:1, :] - t_ref[...]
        dy = diff * (1.0 / D_MODEL)
        acc_ref[0:1, :] += jnp.sum(dy * xhat, axis=0, keepdims=True)
        acc_ref[1:2, :] += jnp.sum(dy, axis=0, keepdims=True)
        acc_ref[2:3, :] += jnp.sum(diff * diff, axis=0, keepdims=True)
        d = _ln_bwd(dy * g2_ref[0:1, :], xhat, rstd)
        d_ref[...] = d
        db_ref[...] = d.astype(_ACT)

    row = pl.BlockSpec((tm, D_MODEL), lambda i: (i, 0))
    par = pl.BlockSpec((SUB, D_MODEL), lambda i: (0, 0))
    return pl.pallas_call(
        body, name=name,
        out_shape=(jax.ShapeDtypeStruct((t, D_MODEL), F32), jax.ShapeDtypeStruct((t, D_MODEL), _ACT),
                   jax.ShapeDtypeStruct((SUB, D_MODEL), F32)),
        grid=(t // tm,), in_specs=[row, row, row, par, par, par, par],
        out_specs=(row, row, par), compiler_params=_cp(("arbitrary",)),
    )(xhat1, ffn, target, g1_8, b1_8, g2_8, b2_8)


def _ln1_bwd(d_r2, d_h1_ffn, xhat1, rstd1, g1_8, name):
    t = xhat1.shape[0]
    tm = _tm_rows(t)

    def body(d2_ref, df_ref, xh_ref, rs_ref, g_ref, d_ref, db_ref, acc_ref):
        @pl.when(pl.program_id(0) == 0)
        def _():
            acc_ref[...] = jnp.zeros_like(acc_ref)

        dh = ALPHA * d2_ref[...] + df_ref[...]
        xhat = xh_ref[...]
        acc_ref[0:1, :] += jnp.sum(dh * xhat, axis=0, keepdims=True)
        acc_ref[1:2, :] += jnp.sum(dh, axis=0, keepdims=True)
        d = _ln_bwd(dh * g_ref[0:1, :], xhat, rs_ref[...])
        d_ref[...] = d
        db_ref[...] = d.astype(_ACT)

    row = pl.BlockSpec((tm, D_MODEL), lambda i: (i, 0))
    par = pl.BlockSpec((SUB, D_MODEL), lambda i: (0, 0))
    return pl.pallas_call(
        body, name=name,
        out_shape=(jax.ShapeDtypeStruct((t, D_MODEL), F32), jax.ShapeDtypeStruct((t, D_MODEL), _ACT),
                   jax.ShapeDtypeStruct((SUB, D_MODEL), F32)),
        grid=(t // tm,), in_specs=[row, row, row, pl.BlockSpec((tm, 1), lambda i: (i, 0)), par],
        out_specs=(row, row, par), compiler_params=_cp(("arbitrary",)),
    )(d_r2, d_h1_ffn, xhat1, rstd1, g1_8)


def _ffn_tm(t):
    return min(128, t)


def _ffn_act_fwd(u0, cw8, cb8, name):
    t = u0.shape[0]
    tm = _ffn_tm(t)

    def body(g_ref, gp_ref, v_ref, vp_ref, wg_ref, wv_ref, bg_ref, bv_ref, o_ref, u_ref):
        i = pl.program_id(0)
        gprev = jnp.where(i > 0, gp_ref[SUB:HALO, :].astype(F32), 0.0)
        vprev = jnp.where(i > 0, vp_ref[SUB:HALO, :].astype(F32), 0.0)
        gate = _conv_pre(g_ref[...].astype(F32), gprev, wg_ref, bg_ref[0:1, :], FFN_K)
        val = _conv_pre(v_ref[...].astype(F32), vprev, wv_ref, bv_ref[0:1, :], FFN_K)
        o_ref[...] = (gate * _sigmoid(gate) * val).astype(_ACT)
        u_ref[:, 0:D_FF] = gate.astype(_ACT)
        u_ref[:, D_FF:2 * D_FF] = val.astype(_ACT)

    cur = lambda col: pl.BlockSpec((tm, D_FF), lambda i: (i, col))
    prv = lambda col: _prev_halo(tm, D_FF, lambda i: (i, col))
    par = lambda col: pl.BlockSpec((SUB, D_FF), lambda i: (0, col))
    return pl.pallas_call(
        body, name=name,
        out_shape=(jax.ShapeDtypeStruct((t, D_FF), _ACT), jax.ShapeDtypeStruct((t, 2 * D_FF), _ACT)),
        grid=(t // tm,),
        in_specs=[cur(0), prv(0), cur(1), prv(1), par(0), par(1), par(0), par(1)],
        out_specs=(pl.BlockSpec((tm, D_FF), lambda i: (i, 0)), pl.BlockSpec((tm, 2 * D_FF), lambda i: (i, 0))),
        compiler_params=_cp(("parallel",)),
    )(u0, u0, u0, u0, cw8, cw8, cb8, cb8)


def _ffn_act_bwd(u0, u, cw8, d_a, name):
    t = u0.shape[0]
    tm = _ffn_tm(t)
    nt = t // tm

    def body(g0_ref, v0_ref, g_ref, gn_ref, v_ref, vn_ref, wg_ref, wv_ref, da_ref, dan_ref, du_ref, acc_ref):
        i = pl.program_id(0)

        @pl.when(i == 0)
        def _():
            acc_ref[...] = jnp.zeros_like(acc_ref)

        def grads(gate, val, da):
            return da * val * _silu_grad(gate), da * gate * _sigmoid(gate)

        dgate, dval = grads(g_ref[...].astype(F32), v_ref[...].astype(F32), da_ref[...].astype(F32))
        dgate_n, dval_n = grads(gn_ref[0:SUB, :].astype(F32), vn_ref[0:SUB, :].astype(F32),
                                dan_ref[0:SUB, :].astype(F32))
        last = i == nt - 1
        du_ref[:, 0:D_FF] = _conv_grads(dgate, jnp.where(last, 0.0, dgate_n), g0_ref[...].astype(F32), wg_ref,
                                        acc_ref, FFN_K, slice(0, D_FF)).astype(_ACT)
        du_ref[:, D_FF:2 * D_FF] = _conv_grads(dval, jnp.where(last, 0.0, dval_n), v0_ref[...].astype(F32), wv_ref,
                                               acc_ref, FFN_K, slice(D_FF, 2 * D_FF)).astype(_ACT)

    cur = lambda col: pl.BlockSpec((tm, D_FF), lambda i: (i, col))
    nxt = lambda col: _next_halo(tm, t, D_FF, lambda i: (i, col))
    par = lambda col: pl.BlockSpec((SUB, D_FF), lambda i: (0, col))
    return pl.pallas_call(
        body, name=name,
        out_shape=(jax.ShapeDtypeStruct((t, 2 * D_FF), _ACT), jax.ShapeDtypeStruct((SUB, 2 * D_FF), F32)),
        grid=(nt,),
        in_specs=[cur(0), cur(1), cur(0), nxt(0), cur(1), nxt(1), par(0), par(1), cur(0), nxt(0)],
        out_specs=(pl.BlockSpec((tm, 2 * D_FF), lambda i: (i, 0)),
                   pl.BlockSpec((SUB, 2 * D_FF), lambda i: (0, 0))),
        compiler_params=_cp(("arbitrary",)),
    )(u0, u0, u, u, u, u, cw8, cw8, d_a, d_a)


def _pack_w_in(w_in):
    order = sorted(_PIECES, key=lambda p: p[2])
    cols = []
    at = 0
    for o, w, pk in order:
        if pk > at:
            cols.append(jnp.zeros((w_in.shape[0], pk - at), w_in.dtype))
        cols.append(w_in[:, o:o + w])
        at = pk + w
    if at < P_W:
        cols.append(jnp.zeros((w_in.shape[0], P_W - at), w_in.dtype))
    return jnp.concatenate(cols, axis=1)


def _unpack_w_in(wp):
    return jnp.concatenate([wp[:, pk:pk + w] for o, w, pk in sorted(_PIECES)], axis=1)


def _local_step(x, target, wts):
    t = x.shape[0]
    wp = _pack_w_in(wts["w_in"])
    w_bs, w_ba, w_mix, w_up, w_dn = (wts[k] for k in ("w_branch_ssm", "w_branch_attn", "w_mix_out", "w_up", "w_down"))
    scw = wts["ssm_conv_w"]
    scb = wts["ssm_conv_b"]
    fcw8 = _rows8(wts["ffn_conv_w"])
    fcb8 = _rows8(wts["ffn_conv_b"])
    pad_lane = lambda p: jnp.concatenate([p.astype(F32), jnp.zeros((1, LANE - p.shape[1]), F32)], axis=1)
    dtb8 = _rows8(pad_lane(wts["ssm_dt_bias"]))
    alog8 = _rows8(pad_lane(wts["ssm_a_log"]))
    dsk8 = _rows8(pad_lane(wts["ssm_d"]))
    bias_table = jnp.concatenate([wts["rel_bias"].T.astype(F32), wts["attn_sinks"].T.astype(F32),
                                  jnp.zeros((A_HEADS, BIAS_ROWS - REL_BUCKETS - 1), F32)], axis=1)
    nw8 = _rows8(wts["ssm_norm_w"])
    bg8 = _rows8(wts["b_gate"])
    g1_8, b1_8, g2_8, b2_8 = (_rows8(wts[k]) for k in ("ln1_g", "ln1_b", "ln2_g", "ln2_b"))
    xs_w8, xs_b8 = _rows8(scw[:, :D_INNER]), _rows8(scb[:, :D_INNER])
    bc_w8, bc_b8 = _rows8(scw[:, D_INNER:]), _rows8(scb[:, D_INNER:])

    x_bf = x.astype(_ACT)
    proj = _mm(x_bf, wp[:, :P_MAIN], "mm_in", out_dtype=_ACT)
    tail = _mm(x_bf, wp[:, P_MAIN:], "mm_in_tail")
    xs_c, xs_pre = _conv_silu_fwd(proj, P_XS // _TC, D_INNER // _TC, xs_w8, xs_b8, "conv_xs_fwd")
    bc_c, bc_pre = _conv_silu_fwd(proj, P_BC // _TC, 1024 // _TC, bc_w8, bc_b8, "conv_bc_fwd")
    y_ssd, y_ssm, hprev = _ssd_fwd(xs_c, bc_c, proj, tail, dtb8, alog8, dsk8, nw8, "ssd_fwd")
    bias = _bias_expand(bias_table, "bias_expand").reshape(2, A_HEADS, WIN, 2 * WIN)
    y_attn = _attn_fwd(proj, tail, bias, "attn_fwd")
    bs = _mm(y_ssm, w_bs, "mm_bs", out_dtype=_ACT)
    ba = _mm(y_attn, w_ba, "mm_ba", out_dtype=_ACT)
    merged = _merge_fwd(bs, ba, proj, bg8, "merge_fwd")
    mix = _mm(merged, w_mix, "mm_mix", out_dtype=_ACT)
    xhat1, h1_bf, rstd1 = _ln1_fwd(x, mix, g1_8, b1_8, "ln1_fwd")
    u0 = _mm(h1_bf, w_up, "mm_up", out_dtype=_ACT)
    act, u_conv = _ffn_act_fwd(u0, fcw8, fcb8, "ffn_act_fwd")
    ffn = _mm(act, w_dn, "mm_down", out_dtype=_ACT)
    d_r2, d_r2_bf, acc_ln2 = _ln2_loss(xhat1, ffn, target, g1_8, b1_8, g2_8, b2_8, "ln2_loss")
    d_w_dn = _mm(act, d_r2_bf, "mm_dw_down", trans_a=True)
    d_act = _mm(d_r2_bf, w_dn.T, "mm_d_act", out_dtype=_ACT)
    d_u0, acc_ffn = _ffn_act_bwd(u0, u_conv, fcw8, d_act, "ffn_act_bwd")
    d_w_up = _mm(h1_bf, d_u0, "mm_dw_up", trans_a=True)
    d_h1_ffn = _mm(d_u0, w_up.T, "mm_d_h1", out_dtype=_ACT)
    d_r1, d_r1_bf, acc_ln1 = _ln1_bwd(d_r2, d_h1_ffn, xhat1, rstd1, g1_8, "ln1_bwd")
    d_w_mix = _mm(merged, d_r1_bf, "mm_dw_mix", trans_a=True)
    d_merged = _mm(d_r1_bf, w_mix.T, "mm_d_merged", out_dtype=_ACT)
    d_bs, d_ba, d_gates, acc_bg = _merge_bwd(d_merged, bs, ba, proj, bg8, "merge_bwd")
    d_w_bs = _mm(y_ssm, d_bs, "mm_dw_bs", trans_a=True)
    d_w_ba = _mm(y_attn, d_ba, "mm_dw_ba", trans_a=True)
    d_y_ssm = _mm(d_bs, w_bs.T, "mm_d_yssm", out_dtype=_ACT)
    d_y_attn = _mm(d_ba, w_ba.T, "mm_d_yattn", out_dtype=_ACT)
    d_q, d_k, d_v, d_bias = _attn_bwd(proj, tail, bias, y_attn, d_y_attn, "attn_bwd")
    d_table = _bias_reduce(d_bias.reshape(A_HEADS, WIN * 2 * WIN), "bias_reduce")
    d_xs_c, d_bc_c, d_z, d_dt, acc_ssd, acc_nw = _ssd_bwd(
        d_y_ssm, y_ssd, xs_c, bc_c, proj, tail, hprev, dtb8, alog8, dsk8, nw8, "ssd_bwd")
    d_xs, acc_xs = _conv_silu_bwd(proj, P_XS // _TC, D_INNER // _TC, xs_pre, xs_w8, d_xs_c, "conv_xs_bwd")
    d_bc, acc_bc = _conv_silu_bwd(proj, P_BC // _TC, 1024 // _TC, bc_pre, bc_w8, d_bc_c, "conv_bc_bwd")
    d_proj = jnp.concatenate([d_z, d_xs, d_gates, d_q, d_bc, d_k, d_v, d_dt,
                              jnp.zeros((t, P_W - P_DT - LANE), _ACT)], axis=1)
    d_wp = _mm(x_bf, d_proj, "mm_dw_in", trans_a=True)
    d_x = _mm(d_proj, wp.T, "mm_d_x", res=d_r1, res_scale=ALPHA)

    grads = {
        "w_in": _unpack_w_in(d_wp),
        "ssm_conv_w": jnp.concatenate([acc_xs[0:SSM_K], acc_bc[0:SSM_K]], axis=1),
        "w_branch_ssm": d_w_bs, "w_branch_attn": d_w_ba, "w_mix_out": d_w_mix,
        "w_up": d_w_up, "ffn_conv_w": acc_ffn[0:FFN_K], "w_down": d_w_dn,
    }
    small = {
        "rel_bias": d_table[:, 0:REL_BUCKETS].T,
        "b_gate": acc_bg[0:1],
        "ssm_conv_b": jnp.concatenate([acc_xs[SSM_K:SSM_K + 1], acc_bc[SSM_K:SSM_K + 1]], axis=1),
        "ssm_dt_bias": acc_ssd[0:1, 0:N_HEADS], "ssm_a_log": acc_ssd[1:2, 0:N_HEADS], "ssm_d": acc_ssd[2:3, 0:N_HEADS],
        "ssm_norm_w": acc_nw[0:1],
        "attn_sinks": d_table[:, REL_BUCKETS:REL_BUCKETS + 1].T,
        "ln1_g": acc_ln1[0:1], "ln1_b": acc_ln1[1:2],
        "ffn_conv_b": acc_ffn[FFN_K:FFN_K + 1],
        "ln2_g": acc_ln2[0:1], "ln2_b": acc_ln2[1:2],
        "loss_lanes": acc_ln2[2:3],
    }
    return d_x, grads, small


_BIG = (("w_in", (1024, 2120), 1), ("ssm_conv_w", (4, 768), 1), ("w_branch_ssm", (512, 1024), 0),
        ("w_branch_attn", (256, 1024), 0), ("w_mix_out", (256, 1024), 0), ("w_up", (1024, 1408), 1),
        ("ffn_conv_w", (3, 1408), 1), ("w_down", (704, 1024), 0))
_ROW_ALIGN = 16
_BLK_ROWS = 1024


def _piece_rows(shape):
    rows = -(-(shape[0] * shape[1]) // LANE)
    return -(-rows // _ROW_ALIGN) * _ROW_ALIGN


_BIG_ROWS = -(-sum(_piece_rows(s) for _, s, _ in _BIG) // (2 * _BLK_ROWS)) * (2 * _BLK_ROWS)
_HALF_ROWS = _BIG_ROWS // 2
_F32_CONV = (("ssm_conv_w", (4, 768)), ("ffn_conv_w", (3, 1408)))
_AG_ROWS = _BIG_ROWS + 2 * _BLK_ROWS
_AG_HALF = _AG_ROWS // 2

_SMALL = (("rel_bias", (32, 16)), ("b_gate", (1, 2048)), ("ssm_conv_b", (1, 3072)), ("ssm_dt_bias", (1, 32)),
          ("ssm_a_log", (1, 32)), ("ssm_d", (1, 32)), ("ssm_norm_w", (1, 2048)), ("attn_sinks", (1, 16)),
          ("ln1_g", (1, 1024)), ("ln1_b", (1, 1024)), ("ffn_conv_b", (1, 5632)), ("ln2_g", (1, 1024)),
          ("ln2_b", (1, 1024)))
_LOSS_ROWS = SUB


def _small_rows(shape):
    rows = -(-(shape[0] * shape[1]) // LANE)
    return -(-rows // SUB) * SUB


_SMALL_ROWS = sum(_small_rows(s) for _, s in _SMALL) + _LOSS_ROWS


def _as_rows(a, rows, dtype):
    flat = a.reshape(-1).astype(dtype)
    flat = jnp.concatenate([flat, jnp.zeros((rows * LANE - flat.shape[0],), dtype)])
    return flat.reshape(rows, LANE)


def _pack_big(parts, dtype):
    blocks = [_as_rows(parts[n], _piece_rows(s), dtype) for n, s, _ in _BIG]
    used = sum(b.shape[0] for b in blocks)
    blocks.append(jnp.zeros((_BIG_ROWS - used, LANE), dtype))
    return jnp.concatenate(blocks, axis=0)


def _unpack_big(packed):
    out, at = {}, 0
    for n, s, _ in _BIG:
        rows = _piece_rows(s)
        out[n] = packed[at:at + rows].reshape(-1)[:s[0] * s[1]].reshape(s)
        at += rows
    return out


def _pack_small(parts, extra):
    blocks = [_as_rows(parts[n], _small_rows(s), F32) for n, s in _SMALL]
    blocks.append(_as_rows(extra, _LOSS_ROWS, F32))
    return jnp.concatenate(blocks, axis=0)


def _unpack_small(packed):
    out, at = {}, 0
    for n, s in _SMALL:
        rows = _small_rows(s)
        out[n] = packed[at:at + rows].reshape(-1)[:s[0] * s[1]].reshape(s)
        at += rows
    return out, packed[at:at + _LOSS_ROWS]


def _shard_of(full, shape, axis, j):
    return lax.slice_in_dim(full, j * shape[axis], (j + 1) * shape[axis], axis=axis)


_MESH = pl.DeviceIdType.MESH
_HBM = pl.BlockSpec(memory_space=pltpu.HBM)


def _position():
    return lax.axis_index("x"), lax.axis_index("y"), lax.axis_index("c")


def _other_chips(x, y):
    return ((1 - x, y), (x, 1 - y), (1 - x, 1 - y))


def _allgather_weights(shard):
    _, hr, _ = shard.shape

    def body(s_ref, o_ref, send_sems, recv_sems, local_sem):
        x, y, c = _position()
        me = 2 * x + y
        sib = (x, y, 1 - c)
        chips = _other_chips(x, y)
        mine = pltpu.make_async_copy(s_ref, o_ref.at[me], local_sem)
        mine.start()

        def copy(k, chip_idx, half, to, src=None):
            dst = o_ref.at[chip_idx, half]
            return pltpu.make_async_remote_copy(src_ref=dst if src is None else src, dst_ref=dst,
                                                send_sem=send_sems.at[k], recv_sem=recv_sems.at[k],
                                                device_id=to, device_id_type=_MESH)

        first = [copy(i, me, c, (cx, cy, c), src=s_ref.at[c]) for i, (cx, cy) in enumerate(chips)]
        for cp in first:
            cp.start()
        passed = [copy(3 + i, 2 * cx + cy, c, sib) for i, (cx, cy) in enumerate(chips)]
        for i, (cx, cy) in enumerate(chips):
            copy(i, 2 * cx + cy, c, sib).wait_recv()
            passed[i].start()
        for i, (cx, cy) in enumerate(chips):
            copy(3 + i, 2 * cx + cy, 1 - c, sib).wait_recv()
        for cp in first + passed:
            cp.wait_send()
        mine.wait()

    return pl.pallas_call(
        body, name="allgather_weights",
        out_shape=jax.ShapeDtypeStruct((N_CHIPS, 2, hr, LANE), shard.dtype),
        in_specs=[_HBM], out_specs=_HBM,
        scratch_shapes=[pltpu.SemaphoreType.DMA((6,)), pltpu.SemaphoreType.DMA((6,)), pltpu.SemaphoreType.DMA],
    )(shard)


def _swap_halves(g):
    nseg, _, hr, _ = g.shape

    def body(g_ref, o_ref, send_sems, recv_sems):
        x, y, c = _position()
        cps = [pltpu.make_async_remote_copy(src_ref=g_ref.at[j, 1 - c], dst_ref=o_ref.at[j],
                                            send_sem=send_sems.at[j], recv_sem=recv_sems.at[j],
                                            device_id=(x, y, 1 - c), device_id_type=_MESH) for j in range(nseg)]
        for cp in cps:
            cp.start()
        for cp in cps:
            cp.wait()

    return pl.pallas_call(
        body, name="swap_halves", out_shape=jax.ShapeDtypeStruct((nseg, hr, LANE), g.dtype),
        in_specs=[_HBM], out_specs=_HBM,
        scratch_shapes=[pltpu.SemaphoreType.DMA((nseg,)), pltpu.SemaphoreType.DMA((nseg,))],
    )(g)


def _scatter_chips(p):
    _, hr, _ = p.shape

    def body(p_ref, o_ref, send_sems, recv_sems):
        x, y, c = _position()
        cps = [pltpu.make_async_remote_copy(src_ref=p_ref.at[2 * cx + cy], dst_ref=o_ref.at[i],
                                            send_sem=send_sems.at[i], recv_sem=recv_sems.at[i],
                                            device_id=(cx, cy, c), device_id_type=_MESH)
               for i, (cx, cy) in enumerate(_other_chips(x, y))]
        for cp in cps:
            cp.start()
        for cp in cps:
            cp.wait()

    return pl.pallas_call(
        body, name="scatter_chips", out_shape=jax.ShapeDtypeStruct((N_CHIPS - 1, hr, LANE), p.dtype),
        in_specs=[_HBM], out_specs=_HBM,
        scratch_shapes=[pltpu.SemaphoreType.DMA((N_CHIPS - 1,)), pltpu.SemaphoreType.DMA((N_CHIPS - 1,))],
    )(p)


def _join_halves(red):
    hr, _ = red.shape

    def body(r_ref, o_ref, send_sem, recv_sem, local_sem):
        x, y, c = _position()
        mine = pltpu.make_async_copy(r_ref, o_ref.at[c], local_sem)
        mine.start()
        cp = pltpu.make_async_remote_copy(src_ref=r_ref, dst_ref=o_ref.at[c], send_sem=send_sem, recv_sem=recv_sem,
                                          device_id=(x, y, 1 - c), device_id_type=_MESH)
        cp.start()
        cp.wait()
        mine.wait()

    return pl.pallas_call(
        body, name="join_halves", out_shape=jax.ShapeDtypeStruct((2, hr, LANE), red.dtype),
        in_specs=[_HBM], out_specs=_HBM,
        scratch_shapes=[pltpu.SemaphoreType.DMA, pltpu.SemaphoreType.DMA, pltpu.SemaphoreType.DMA],
    )(red)


def _allgather_small(mine):
    m_per, n = mine.shape

    def body(x_ref, out_ref, send_sems, recv_sems, local_sem):
        x, y, c = _position()
        me, sibling = (x, y, c), (x, y, 1 - c)
        chips = _other_chips(x, y)

        def rows(px, py, pc):
            return out_ref.at[pl.ds((4 * px + 2 * py + pc) * m_per, m_per), :]

        def copy(k, block, to, src=None):
            return pltpu.make_async_remote_copy(src_ref=rows(*block) if src is None else src, dst_ref=rows(*block),
                                                send_sem=send_sems.at[k], recv_sem=recv_sems.at[k],
                                                device_id=to, device_id_type=_MESH)

        own = pltpu.make_async_copy(x_ref, rows(*me), local_sem)
        own.start()
        first = [copy(0, me, sibling, src=x_ref)]
        first += [copy(1 + j, me, (*chip, c), src=x_ref) for j, chip in enumerate(chips)]
        for cp in first:
            cp.start()
        passed = [copy(4 + j, (*chip, c), sibling) for j, chip in enumerate(chips)]
        for j, chip in enumerate(chips):
            copy(1 + j, (*chip, c), me).wait_recv()
            passed[j].start()
        copy(0, sibling, me).wait_recv()
        for j, chip in enumerate(chips):
            copy(4 + j, (*chip, 1 - c), me).wait_recv()
        for cp in first + passed:
            cp.wait_send()
        own.wait()

    return pl.pallas_call(
        body, name="allgather_small", out_shape=jax.ShapeDtypeStruct((N_DEV * m_per, n), mine.dtype),
        in_specs=[pl.BlockSpec(memory_space=pltpu.VMEM)], out_specs=pl.BlockSpec(memory_space=pltpu.VMEM),
        scratch_shapes=[pltpu.SemaphoreType.DMA((7,)), pltpu.SemaphoreType.DMA((7,)), pltpu.SemaphoreType.DMA],
    )(mine)


def _add_own_half(g, recv, c_idx):
    nseg, _, hr, _ = g.shape

    def body(c_ref, g_ref, r_ref, o_ref, ob_ref):
        s = g_ref[...] + r_ref[...]
        o_ref[...] = s
        ob_ref[...] = s.astype(jnp.bfloat16)

    blk = pl.BlockSpec((None, _BLK_ROWS, LANE), lambda j, i, c_ref: (j, i, 0))
    return pl.pallas_call(
        body, name="add_own_half",
        out_shape=(jax.ShapeDtypeStruct((nseg, hr, LANE), F32), jax.ShapeDtypeStruct((nseg, hr, LANE), jnp.bfloat16)),
        grid_spec=pltpu.PrefetchScalarGridSpec(
            num_scalar_prefetch=1, grid=(nseg, hr // _BLK_ROWS),
            in_specs=[pl.BlockSpec((None, None, _BLK_ROWS, LANE), lambda j, i, c_ref: (j, c_ref[0], i, 0)), blk],
            out_specs=(blk, blk)),
        compiler_params=_cp(("parallel", "parallel")),
    )(c_idx, g, recv)


def _add_chips(p, recv, chip_idx):
    _, hr, _ = p.shape

    def body(j_ref, p_ref, r_ref, o_ref):
        o_ref[...] = ((p_ref[...] + r_ref[0].astype(F32)) + r_ref[1].astype(F32)) + r_ref[2].astype(F32)

    return pl.pallas_call(
        body, name="add_chips", out_shape=jax.ShapeDtypeStruct((hr, LANE), F32),
        grid_spec=pltpu.PrefetchScalarGridSpec(
            num_scalar_prefetch=1, grid=(hr // _BLK_ROWS,),
            in_specs=[pl.BlockSpec((None, _BLK_ROWS, LANE), lambda i, j_ref: (j_ref[0], i, 0)),
                      pl.BlockSpec((N_CHIPS - 1, _BLK_ROWS, LANE), lambda i, j_ref: (0, i, 0))],
            out_specs=pl.BlockSpec((_BLK_ROWS, LANE), lambda i, j_ref: (i, 0))),
        compiler_params=_cp(("parallel",)),
    )(chip_idx, p, recv)


def _adam_math(w, g, m, v):
    m = ADAM_B1 * m + (1.0 - ADAM_B1) * g
    v = ADAM_B2 * v + (1.0 - ADAM_B2) * (g * g)
    m_hat = m / (1.0 - ADAM_B1 ** ADAM_STEP)
    v_hat = v / (1.0 - ADAM_B2 ** ADAM_STEP)
    delta = -ADAM_LR * (m_hat / (jnp.sqrt(v_hat) + ADAM_EPS) + ADAM_WD * w)
    return delta, m, v


def _adam_big(w, g, m, v, name):
    rows, cols = w.shape
    tr = _pick(rows, (256, 128, 64, 32, 16, 8)) if rows % SUB == 0 else rows

    def body(w_ref, g_ref, m_ref, v_ref, d_ref, mo_ref, vo_ref):
        d_ref[...], mo_ref[...], vo_ref[...] = _adam_math(w_ref[...], g_ref[...], m_ref[...], v_ref[...])

    blk = pl.BlockSpec((tr, cols), lambda i: (i, 0))
    shp = jax.ShapeDtypeStruct((rows, cols), F32)
    return pl.pallas_call(
        body, name=name, out_shape=(shp, shp, shp), grid=(rows // tr,),
        in_specs=[blk, blk, blk, blk], out_specs=(blk, blk, blk), compiler_params=_cp(("parallel",)),
    )(w, g, m, v)


def _adam_small(w, gathered, m, v):
    rows = w.shape[0]

    def body(w_ref, a_ref, m_ref, v_ref, g_ref, d_ref, mo_ref, vo_ref):
        g = a_ref[0:rows, :]
        for k in range(1, N_DEV):
            g = g + a_ref[k * rows:(k + 1) * rows, :]
        g_ref[...] = g
        d_ref[...], mo_ref[...], vo_ref[...] = _adam_math(w_ref[...], g, m_ref[...], v_ref[...])

    shp = jax.ShapeDtypeStruct((rows, LANE), F32)
    return pl.pallas_call(body, name="adam_small", out_shape=(shp, shp, shp, shp), compiler_params=_cp(None))(
        w, gathered, m, v)


_WEIGHTS = ("rel_bias", "w_in", "b_gate", "ssm_conv_w", "ssm_conv_b", "ssm_dt_bias", "ssm_a_log", "ssm_d",
            "ssm_norm_w", "attn_sinks", "w_branch_ssm", "w_branch_attn", "w_mix_out", "ln1_g", "ln1_b", "w_up",
            "ffn_conv_w", "ffn_conv_b", "w_down", "ln2_g", "ln2_b")
_BIG_NAMES = tuple(n for n, _, _ in _BIG)


def _step(x, target, w, m, v):
    xi, yi, ci = _position()
    chip = 2 * xi + yi

    f32_rows = jnp.concatenate(
        [_as_rows(lax.bitcast_convert_type(w[n].astype(F32), jnp.bfloat16), 2 * _piece_rows(s), jnp.bfloat16)
         for n, s in _F32_CONV], axis=0)
    f32_rows = jnp.concatenate(
        [f32_rows, jnp.zeros((_AG_ROWS - _BIG_ROWS - f32_rows.shape[0], LANE), jnp.bfloat16)], axis=0)
    shard = jnp.concatenate([_pack_big(w, jnp.bfloat16), f32_rows], axis=0).reshape(2, _AG_HALF, LANE)
    gathered = _allgather_weights(shard).reshape(N_CHIPS, _AG_ROWS, LANE)
    per_chip = [_unpack_big(gathered[j, :_BIG_ROWS]) for j in range(N_CHIPS)]
    full = {n: jnp.concatenate([per_chip[j][n] for j in range(N_CHIPS)], axis=ax) for n, _, ax in _BIG}
    at = _BIG_ROWS
    for n, s in _F32_CONV:
        rows = 2 * _piece_rows(s)
        pieces = [lax.bitcast_convert_type(gathered[j, at:at + rows].reshape(-1, 2), F32)[:s[0] * s[1]].reshape(s)
                  for j in range(N_CHIPS)]
        full[n] = jnp.concatenate(pieces, axis=1)
        at += rows
    wts = {n: (full[n] if n in full else w[n]) for n in _WEIGHTS}

    d_x, grads, small = _local_step(x, target, wts)

    g = jnp.stack([_pack_big({n: _shard_of(grads[n], s, ax, j) for n, s, ax in _BIG}, F32)
                   for j in range(N_CHIPS)]).reshape(N_CHIPS, 2, _HALF_ROWS, LANE)
    c_idx = jnp.reshape(ci, (1,)).astype(jnp.int32)
    chip_sum, chip_sum_bf = _add_own_half(g, _swap_halves(g), c_idx)
    red = _add_chips(chip_sum, _scatter_chips(chip_sum_bf), jnp.reshape(chip, (1,)).astype(jnp.int32))
    g_big = _join_halves(red).reshape(_BIG_ROWS, LANE)
    outs = {"grad": _unpack_big(g_big), "delta": {}, "m": {}, "v": {}}
    for n, _, _ in _BIG:
        outs["delta"][n], outs["m"][n], outs["v"][n] = _adam_big(
            w[n].astype(F32), outs["grad"][n], m[n].astype(F32), v[n].astype(F32), "adam_" + n)

    all_small = _allgather_small(_pack_small(small, small["loss_lanes"]))
    g_s, d_s, m_s, v_s = _adam_small(_pack_small(w, jnp.zeros((1, LANE), F32)), all_small,
                                     _pack_small(m, jnp.zeros((1, LANE), F32)),
                                     _pack_small(v, jnp.zeros((1, LANE), F32)))
    (gs, loss_rows), (ds, _), (ms, _), (vs, _) = (_unpack_small(a) for a in (g_s, d_s, m_s, v_s))
    for kind, part in (("grad", gs), ("delta", ds), ("m", ms), ("v", vs)):
        outs[kind].update(part)
    loss = (0.5 / D_MODEL) * jnp.sum(loss_rows)
    return loss, d_x, outs


def kernel(x, rel_bias, w_in, b_gate, ssm_conv_w, ssm_conv_b, ssm_dt_bias, ssm_a_log, ssm_d, ssm_norm_w, attn_sinks, w_branch_ssm, w_branch_attn, w_mix_out, ln1_g, ln1_b, w_up, ffn_conv_w, ffn_conv_b, w_down, ln2_g, ln2_b, loss_target, m_rel_bias, m_w_in, m_b_gate, m_ssm_conv_w, m_ssm_conv_b, m_ssm_dt_bias, m_ssm_a_log, m_ssm_d, m_ssm_norm_w, m_attn_sinks, m_w_branch_ssm, m_w_branch_attn, m_w_mix_out, m_ln1_g, m_ln1_b, m_w_up, m_ffn_conv_w, m_ffn_conv_b, m_w_down, m_ln2_g, m_ln2_b, v_rel_bias, v_w_in, v_b_gate, v_ssm_conv_w, v_ssm_conv_b, v_ssm_dt_bias, v_ssm_a_log, v_ssm_d, v_ssm_norm_w, v_attn_sinks, v_w_branch_ssm, v_w_branch_attn, v_w_mix_out, v_ln1_g, v_ln1_b, v_w_up, v_ffn_conv_w, v_ffn_conv_b, v_w_down, v_ln2_g, v_ln2_b):
    given = dict(locals())
    drop = lambda a, n: a if n == "rel_bias" or a.ndim == 2 else a[0]
    w = {n: drop(given[n], n) for n in _WEIGHTS}
    m = {n: drop(given["m_" + n], n) for n in _WEIGHTS}
    v = {n: drop(given["v_" + n], n) for n in _WEIGHTS}
    loss, d_x, outs = _step(x[0], loss_target[0], w, m, v)
    like = lambda a, n: a.reshape(given[n].shape)
    res = [loss, d_x[None]]
    for kind in ("grad", "delta", "m", "v"):
        res += [like(outs[kind][n], n) for n in _WEIGHTS]
    return tuple(res)
```

```python
import functools
import math

import numpy as np
import jax
import jax.numpy as jnp
from jax import lax
from jax.experimental import pallas as pl
from jax.experimental.pallas import tpu as pltpu

F32 = jnp.float32
_ACT = jnp.bfloat16
_MXU = jnp.bfloat16

D_MODEL = 1024
D_INNER = 2048
N_HEADS = 32
HEAD_P = 64
N_GROUPS = 4
N_STATE = 128
CHUNK = 128
CONV_DIM = 3072
SSM_K = 4
A_HEADS = 16
A_DH = 64
WIN = 128
REL_BUCKETS = 32
BIAS_ROWS = 64
D_FF = 2816
FFN_K = 3
ALPHA = 2.0 ** 0.25
LN_EPS = 1e-5
RMS_EPS = 1e-5
IN_COLS = 8480
NEG = -1e30

ADAM_LR = 0.001
ADAM_B1 = 0.9
ADAM_B2 = 0.999
ADAM_EPS = 1e-08
ADAM_WD = 0.01
ADAM_STEP = 10

LANE = 128
SUB = 8

P_Z, P_XS, P_G, P_Q, P_BC, P_K, P_V, P_DT = 0, 2048, 4096, 6144, 7168, 8192, 8320, 8448
P_W = 8704
P_MAIN = 8192
T_K, T_V, T_DT = P_K - P_MAIN, P_V - P_MAIN, P_DT - P_MAIN
_PIECES = ((0, 2048, P_Z), (2048, 2048, P_XS), (4096, 1024, P_BC), (5120, 32, P_DT), (5152, 1024, P_Q),
           (6176, 128, P_K), (6304, 128, P_V), (6432, 2048, P_G))

N_CHIPS = 4
N_DEV = 8


def _cp(sem=None, vmem_mb=48):
    return pltpu.CompilerParams(dimension_semantics=sem, vmem_limit_bytes=vmem_mb * 1024 * 1024)


def _pick(n, cands):
    for c in cands:
        if n % c == 0:
            return c
    raise ValueError(f"no block size for {n}")


def _rows8(p):
    k, c = p.shape
    return jnp.concatenate([p.astype(F32), jnp.zeros((SUB - k, c), F32)], axis=0)


def _mm(a, b, name, *, trans_a=False, out_dtype=F32, res=None, res_scale=1.0):
    if trans_a:
        k_dim, m = a.shape
    else:
        m, k_dim = a.shape
    k2, n = b.shape
    assert k_dim == k2, (a.shape, b.shape)
    tm = _pick(m, (1408, 1024, 512, 256, 128))
    tn = _pick(n, (1408, 1024, 512, 256, 128))
    tk = _pick(k_dim, (2816, 2176, 2048, 1024, 512, 256, 128))
    nk = k_dim // tk
    dn = (((0,), (0,)), ((), ())) if trans_a else (((1,), (0,)), ((), ()))

    def body(*refs):
        if res is None:
            a_ref, b_ref, o_ref = refs[:3]
        else:
            a_ref, b_ref, r_ref, o_ref = refs[:4]

        def finish(r):
            if res is not None:
                r = r + res_scale * r_ref[...]
            o_ref[...] = r.astype(out_dtype)

        part = lax.dot_general(a_ref[...].astype(_MXU), b_ref[...].astype(_MXU), dn, preferred_element_type=F32)
        if nk == 1:
            finish(part)
            return
        acc = refs[-1]
        k = pl.program_id(2)

        @pl.when(k == 0)
        def _():
            acc[...] = part

        @pl.when(k > 0)
        def _():
            acc[...] += part

        @pl.when(k == nk - 1)
        def _():
            finish(acc[...])

    if trans_a:
        a_spec = pl.BlockSpec((tk, tm), lambda i, j, k: (k, i))
    else:
        a_spec = pl.BlockSpec((tm, tk), lambda i, j, k: (i, k))
    in_specs = [a_spec, pl.BlockSpec((tk, tn), lambda i, j, k: (k, j))]
    args = [a, b]
    if res is not None:
        in_specs.append(pl.BlockSpec((tm, tn), lambda i, j, k: (i, j)))
        args.append(res)
    return pl.pallas_call(
        body, name=name, out_shape=jax.ShapeDtypeStruct((m, n), out_dtype),
        grid=(m // tm, n // tn, nk), in_specs=in_specs,
        out_specs=pl.BlockSpec((tm, tn), lambda i, j, k: (i, j)),
        scratch_shapes=[pltpu.VMEM((tm, tn), F32)] if nk > 1 else [],
        compiler_params=_cp(("parallel", "parallel", "arbitrary")),
    )(*args)


def _shift_down(cur, prev8, s):
    r = pltpu.roll(cur, s, 0)
    p = pltpu.roll(prev8, s, 0)
    row8 = lax.broadcasted_iota(jnp.int32, (SUB, 1), 0)
    fixed = jnp.where(row8 < s, p, r[0:SUB])
    if cur.shape[0] == SUB:
        return fixed
    return jnp.concatenate([fixed, r[SUB:]], axis=0)


def _shift_up(cur, next8, s):
    tm = cur.shape[0]
    r = pltpu.roll(cur, tm - s, 0)
    p = pltpu.roll(next8, SUB - s, 0)
    row8 = lax.broadcasted_iota(jnp.int32, (SUB, 1), 0)
    fixed = jnp.where(row8 >= SUB - s, p, r[tm - SUB:])
    return jnp.concatenate([r[:tm - SUB], fixed], axis=0)


def _conv_pre(cur, prev8, w_ref, b_row, taps):
    acc = cur * w_ref[taps - 1:taps, :] + b_row
    for s in range(1, taps):
        acc = acc + _shift_down(cur, prev8, s) * w_ref[taps - 1 - s:taps - s, :]
    return acc


def _dot01_r(x, m01, parts=3):
    acc = None
    r = x
    for _ in range(parts):
        hi = r.astype(jnp.bfloat16)
        t = jnp.dot(hi, m01, preferred_element_type=F32)
        acc = t if acc is None else acc + t
        r = r - hi.astype(F32)
    return acc


def _dot01_l(m01, x, parts=3):
    acc = None
    r = x
    for _ in range(parts):
        hi = r.astype(jnp.bfloat16)
        t = jnp.dot(m01, hi, preferred_element_type=F32)
        acc = t if acc is None else acc + t
        r = r - hi.astype(F32)
    return acc


def _dot(a, b):
    return jnp.dot(a.astype(_MXU), b.astype(_MXU), preferred_element_type=F32)


def _dot_nt(a, b):
    return lax.dot_general(a.astype(_MXU), b.astype(_MXU), (((1,), (1,)), ((), ())), preferred_element_type=F32)


def _dot_tn(a, b):
    return lax.dot_general(a.astype(_MXU), b.astype(_MXU), (((0,), (0,)), ((), ())), preferred_element_type=F32)


def _sigmoid(x):
    return 1.0 / (1.0 + jnp.exp(-x))


def _half_masks():
    lane = lax.broadcasted_iota(jnp.int32, (1, LANE), 1)
    lo = (lane < 64).astype(F32)
    return lo, 1.0 - lo


_TC = 512


def _tm_rows(t):
    return min(256, t)


HALO = 16


def _prev_halo(tm, width, pos):
    def index(*ids):
        i, col = pos(*ids)
        return (jnp.maximum(i * (tm // HALO) - 1, 0), col)
    return pl.BlockSpec((HALO, width), index)


def _next_halo(tm, t, width, pos):
    def index(*ids):
        i, col = pos(*ids)
        return (jnp.minimum((i + 1) * (tm // HALO), t // HALO - 1), col)
    return pl.BlockSpec((HALO, width), index)


def _conv_silu_fwd(proj, colblk0, nblk, w8, b8, name):
    t = proj.shape[0]
    tm = _tm_rows(t)

    def body(c_ref, p_ref, w_ref, b_ref, o_ref, pre_ref):
        i = pl.program_id(1)
        prev8 = jnp.where(i > 0, p_ref[SUB:HALO, :].astype(F32), 0.0)
        pre = _conv_pre(c_ref[...].astype(F32), prev8, w_ref, b_ref[0:1, :], SSM_K)
        o_ref[...] = pre * _sigmoid(pre)
        pre_ref[...] = pre.astype(_ACT)

    blk = pl.BlockSpec((tm, _TC), lambda j, i: (i, j))
    return pl.pallas_call(
        body, name=name,
        out_shape=(jax.ShapeDtypeStruct((t, nblk * _TC), F32), jax.ShapeDtypeStruct((t, nblk * _TC), _ACT)),
        grid=(nblk, t // tm),
        in_specs=[pl.BlockSpec((tm, _TC), lambda j, i: (i, colblk0 + j)),
                  _prev_halo(tm, _TC, lambda j, i: (i, colblk0 + j)),
                  pl.BlockSpec((SUB, _TC), lambda j, i: (0, j)),
                  pl.BlockSpec((SUB, _TC), lambda j, i: (0, j))],
        out_specs=(blk, blk),
        compiler_params=_cp(("parallel", "parallel")),
    )(proj, proj, w8, b8)


def _silu_grad(pre):
    sg = _sigmoid(pre)
    return sg * (1.0 + pre * (1.0 - sg))


def _conv_grads(d, d_next8, cur, w_ref, acc_ref, taps, cols=slice(None)):
    du = d * w_ref[taps - 1:taps, :]
    acc_ref[taps:taps + 1, cols] += jnp.sum(d, axis=0, keepdims=True)
    acc_ref[taps - 1:taps, cols] += jnp.sum(d * cur, axis=0, keepdims=True)
    for s in range(1, taps):
        up = _shift_up(d, d_next8, s)
        du = du + up * w_ref[taps - 1 - s:taps - s, :]
        acc_ref[taps - 1 - s:taps - s, cols] += jnp.sum(up * cur, axis=0, keepdims=True)
    return du


def _conv_silu_bwd(proj, colblk0, nblk, pre, w8, d_out, name):
    t = proj.shape[0]
    tm = _tm_rows(t)
    nt = t // tm

    def body(c_ref, pre_ref, pren_ref, w_ref, d_ref, dn_ref, du_ref, acc_ref):
        i = pl.program_id(1)

        @pl.when(i == 0)
        def _():
            acc_ref[...] = jnp.zeros_like(acc_ref)

        dpre = d_ref[...].astype(F32) * _silu_grad(pre_ref[...].astype(F32))
        dpre_n = jnp.where(i < nt - 1, dn_ref[0:SUB, :].astype(F32) * _silu_grad(pren_ref[0:SUB, :].astype(F32)), 0.0)
        du_ref[...] = _conv_grads(dpre, dpre_n, c_ref[...].astype(F32), w_ref, acc_ref, SSM_K).astype(_ACT)

    c = nblk * _TC
    blk = pl.BlockSpec((tm, _TC), lambda j, i: (i, j))
    nxt = _next_halo(tm, t, _TC, lambda j, i: (i, j))
    par = pl.BlockSpec((SUB, _TC), lambda j, i: (0, j))
    return pl.pallas_call(
        body, name=name,
        out_shape=(jax.ShapeDtypeStruct((t, c), _ACT), jax.ShapeDtypeStruct((SUB, c), F32)),
        grid=(nblk, nt),
        in_specs=[pl.BlockSpec((tm, _TC), lambda j, i: (i, colblk0 + j)), blk, nxt, par, blk, nxt],
        out_specs=(blk, par),
        compiler_params=_cp(("parallel", "arbitrary")),
    )(proj, pre, pre, w8, d_out, d_out)


def _expand_consts():
    e = np.zeros((LANE, D_INNER), np.float32)
    for h in range(N_HEADS):
        e[h, h * HEAD_P:(h + 1) * HEAD_P] = 1.0
    return jnp.asarray(e, jnp.bfloat16), jnp.asarray(e.T.copy(), jnp.bfloat16)


def _ssd_common(dtr_ref, dtb_ref, alog_ref, e_ref):
    lane = lax.broadcasted_iota(jnp.int32, (1, LANE), 1)
    hm = lane < N_HEADS
    pre = dtr_ref[...] + dtb_ref[0:1, :]
    dt = jnp.where(hm, jnp.maximum(pre, 0.0) + jnp.log(1.0 + jnp.exp(-jnp.abs(pre))), 0.0)
    a_row = jnp.where(hm, -jnp.exp(alog_ref[0:1, :]), 0.0)
    adt = dt * a_row
    r = lax.broadcasted_iota(jnp.int32, (CHUNK, CHUNK), 0)
    c = lax.broadcasted_iota(jnp.int32, (CHUNK, CHUNK), 1)
    causal = r >= c
    acs = _dot01_l(causal.astype(jnp.bfloat16), adt)
    e = e_ref[...]
    acs_x = _dot01_r(acs, e, parts=2)
    dt_x = _dot01_r(dt, e, parts=2)
    return pre, dt, a_row, acs, acs_x, dt_x, causal, hm


def _decay(acs, acs_t, h, causal):
    seg = acs[:, h:h + 1] - acs_t[h:h + 1, :]
    return jnp.exp(jnp.where(causal, seg, NEG))


def _ssd_fwd(xs_c, bc_c, proj, tail, dtb8, alog8, dsk8, nw8, name):
    t = xs_c.shape[0]
    nc = t // CHUNK
    e_bf, _ = _expand_consts()
    gw = D_INNER // N_GROUPS

    def body(xs_ref, bc_ref, dtr_ref, z_ref, dtb_ref, alog_ref, dsk_ref, nw_ref, e_ref,
             y_ref, ys_ref, hp_ref, h_ref):
        c_id = pl.program_id(0)

        @pl.when(c_id == 0)
        def _():
            h_ref[...] = jnp.zeros_like(h_ref)

        _, dt, a_row, acs, acs_x, dt_x, causal, _ = _ssd_common(dtr_ref, dtb_ref, alog_ref, e_ref)
        acs_t = acs.T
        xs = xs_ref[...]
        x_dt = xs * dt_x
        last_x = acs_x[CHUNK - 1:CHUNK, :]
        w_end = jnp.exp(last_x - acs_x)
        e_in = jnp.exp(acs_x)
        d_x = _dot01_r(dsk_ref[...], e_ref[...])[0:1, :]
        hprev = h_ref[...]
        hp_ref[...] = hprev
        lo, hi = _half_masks()
        for g in range(N_GROUPS):
            bg = bc_ref[:, g * N_STATE:(g + 1) * N_STATE]
            cg = bc_ref[:, N_GROUPS * N_STATE + g * N_STATE:N_GROUPS * N_STATE + (g + 1) * N_STATE]
            sl = slice(g * gw, (g + 1) * gw)
            gm = _dot_nt(cg, bg)
            st = _dot(bg.T, x_dt[:, sl] * w_end[:, sl])
            y_off = _dot(cg, hprev[:, sl]) * e_in[:, sl]
            for j in range(gw // LANE):
                h0 = g * (gw // HEAD_P) + 2 * j
                cs = slice(g * gw + j * LANE, g * gw + (j + 1) * LANE)
                xp = x_dt[:, cs]
                m0 = gm * _decay(acs, acs_t, h0, causal)
                m1 = gm * _decay(acs, acs_t, h0 + 1, causal)
                yd = _dot(m0, xp * lo) + _dot(m1, xp * hi)
                y_ref[:, cs] = yd + y_off[:, j * LANE:(j + 1) * LANE] + xs[:, cs] * d_x[:, cs]
            h_ref[:, sl] = hprev[:, sl] * jnp.exp(last_x[:, sl]) + st
        y = y_ref[...]
        z = z_ref[...].astype(F32)
        y2 = y * (z * _sigmoid(z))
        for g in range(N_GROUPS):
            sl = slice(g * gw, (g + 1) * gw)
            yg = y2[:, sl]
            rinv = lax.rsqrt(jnp.mean(yg * yg, axis=-1, keepdims=True) + RMS_EPS)
            ys_ref[:, sl] = (yg * rinv * nw_ref[0:1, sl]).astype(_ACT)

    small = pl.BlockSpec((SUB, LANE), lambda c: (0, 0))
    return pl.pallas_call(
        body, name=name,
        out_shape=(jax.ShapeDtypeStruct((t, D_INNER), F32), jax.ShapeDtypeStruct((t, D_INNER), _ACT),
                   jax.ShapeDtypeStruct((t, D_INNER), F32)),
        grid=(nc,),
        in_specs=[pl.BlockSpec((CHUNK, D_INNER), lambda c: (c, 0)),
                  pl.BlockSpec((CHUNK, 1024), lambda c: (c, 0)),
                  pl.BlockSpec((CHUNK, LANE), lambda c: (c, T_DT // LANE)),
                  pl.BlockSpec((CHUNK, D_INNER), lambda c: (c, P_Z // D_INNER)),
                  small, small, small,
                  pl.BlockSpec((SUB, D_INNER), lambda c: (0, 0)),
                  pl.BlockSpec((LANE, D_INNER), lambda c: (0, 0))],
        out_specs=(pl.BlockSpec((CHUNK, D_INNER), lambda c: (c, 0)),
                   pl.BlockSpec((CHUNK, D_INNER), lambda c: (c, 0)),
                   pl.BlockSpec((N_STATE, D_INNER), lambda c: (c, 0))),
        scratch_shapes=[pltpu.VMEM((N_STATE, D_INNER), F32)],
        compiler_params=_cp(("arbitrary",)),
    )(xs_c, bc_c, tail, proj, dtb8, alog8, dsk8, nw8, e_bf)


def _ssd_bwd(d_ys, y, xs_c, bc_c, proj, tail, hprev_all, dtb8, alog8, dsk8, nw8, name):
    t = xs_c.shape[0]
    nc = t // CHUNK
    e_bf, et_bf = _expand_consts()
    gw = D_INNER // N_GROUPS

    def body(dys_ref, y_ref, xs_ref, bc_ref, dtr_ref, z_ref, hp_ref, dtb_ref, alog_ref, dsk_ref, nw_ref,
             e_ref, et_ref, dxs_ref, dbc_ref, dz_ref, ddt_ref, acc_ref, dnw_ref, dh_ref, dx_ref):
        step = pl.program_id(0)

        @pl.when(step == 0)
        def _():
            dh_ref[...] = jnp.zeros_like(dh_ref)
            acc_ref[...] = jnp.zeros_like(acc_ref)
            dnw_ref[...] = jnp.zeros_like(dnw_ref)

        pre, dt, a_row, acs, acs_x, dt_x, causal, hm = _ssd_common(dtr_ref, dtb_ref, alog_ref, e_ref)
        acs_t = acs.T
        et = et_ref[...]
        xs = xs_ref[...]
        x_dt = xs * dt_x
        last_x = acs_x[CHUNK - 1:CHUNK, :]
        w_end = jnp.exp(last_x - acs_x)
        e_in = jnp.exp(acs_x)
        e_last = jnp.exp(last_x)
        d_x = _dot01_r(dsk_ref[...], e_ref[...])[0:1, :]

        y = y_ref[...]
        z = z_ref[...].astype(F32)
        sz = _sigmoid(z)
        gz = z * sz
        y2 = y * gz
        dys = dys_ref[...].astype(F32)
        for g in range(N_GROUPS):
            sl = slice(g * gw, (g + 1) * gw)
            yg = y2[:, sl]
            rinv = lax.rsqrt(jnp.mean(yg * yg, axis=-1, keepdims=True) + RMS_EPS)
            nrm = yg * rinv
            dn = dys[:, sl] * nw_ref[0:1, sl]
            dnw_ref[0:1, sl] += jnp.sum(dys[:, sl] * nrm, axis=0, keepdims=True)
            dx_ref[:, sl] = rinv * (dn - nrm * jnp.mean(dn * nrm, axis=-1, keepdims=True))
        dy2 = dx_ref[...]
        dy = dy2 * gz
        dz_ref[...] = (dy2 * y * (sz * (1.0 + z * (1.0 - sz)))).astype(_ACT)

        dh_next = dh_ref[...]
        hprev = hp_ref[...]
        lo, hi = _half_masks()
        r = lax.broadcasted_iota(jnp.int32, (CHUNK, CHUNK), 0)
        c = lax.broadcasted_iota(jnp.int32, (CHUNK, CHUNK), 1)
        from_here = (c >= r).astype(jnp.bfloat16)
        before = c < r
        lane = lax.broadcasted_iota(jnp.int32, (1, LANE), 1)
        da_intra = jnp.zeros((CHUNK, LANE), F32)
        v_seg = jnp.zeros((CHUNK, LANE), F32)
        z_seg = jnp.zeros((CHUNK, LANE), F32)
        tail_parts = []
        for g in range(N_GROUPS):
            bg = bc_ref[:, g * N_STATE:(g + 1) * N_STATE]
            cg = bc_ref[:, N_GROUPS * N_STATE + g * N_STATE:N_GROUPS * N_STATE + (g + 1) * N_STATE]
            sl = slice(g * gw, (g + 1) * gw)
            et_g = et_ref[g * gw:(g + 1) * gw, :]
            gm = _dot_nt(cg, bg)
            dzg = e_in[:, sl] * dy[:, sl]
            dcg = _dot_nt(dzg, hprev[:, sl])
            dh_c = _dot(cg.T, dzg)
            q = _dot(bg, dh_next[:, sl])
            dbg = _dot_nt(x_dt[:, sl] * w_end[:, sl], dh_next[:, sl])
            y_off = _dot(cg, hprev[:, sl]) * e_in[:, sl]
            v_seg = v_seg + _dot01_r(dy[:, sl] * y_off, et_g, parts=2)
            z_seg = z_seg + _dot01_r(w_end[:, sl] * q * x_dt[:, sl], et_g, parts=2)
            dgm = jnp.zeros((CHUNK, CHUNK), F32)
            for j in range(gw // LANE):
                h0 = g * (gw // HEAD_P) + 2 * j
                cs = slice(g * gw + j * LANE, g * gw + (j + 1) * LANE)
                xp = x_dt[:, cs]
                dyp = dy[:, cs]
                dxd = jnp.zeros((CHUNK, LANE), F32)
                for half, msk in ((0, lo), (1, hi)):
                    lam = _decay(acs, acs_t, h0 + half, causal)
                    mm = gm * lam
                    dym = dyp * msk
                    dmm = _dot_nt(dym, xp)
                    dxd = dxd + _dot_tn(mm, dym)
                    dgm = dgm + dmm * lam
                    below = _dot(from_here, dmm * mm)
                    col = jnp.sum(jnp.where(before, below, 0.0), axis=-1, keepdims=True)
                    da_intra = da_intra + jnp.where(lane == h0 + half, col, 0.0)
                dx_ref[:, cs] = dxd + w_end[:, cs] * q[:, j * LANE:(j + 1) * LANE]
            dbc_ref[:, N_GROUPS * N_STATE + g * N_STATE:N_GROUPS * N_STATE + (g + 1) * N_STATE] = dcg + _dot(dgm, bg)
            dbc_ref[:, g * N_STATE:(g + 1) * N_STATE] = dbg + _dot_tn(dgm, cg)
            dh_ref[:, sl] = e_last[:, sl] * dh_next[:, sl] + dh_c
            tail_parts.append(e_last[:, sl] * jnp.sum(dh_next[:, sl] * hprev[:, sl], axis=0, keepdims=True))
        dxt = dx_ref[...]

        u_seg = _dot01_r(xs * dxt, et, parts=2)
        q_full = jnp.concatenate(tail_parts, axis=1)
        t_row = _dot01_r(jnp.broadcast_to(q_full, (SUB, D_INNER)), et)[0:1, :]
        d_alpha = (da_intra + _dot01_l(from_here, v_seg) + _dot01_l(before.astype(jnp.bfloat16), z_seg) + t_row)
        d_dt = a_row * d_alpha + u_seg
        sgp = _sigmoid(pre)
        d_raw = jnp.where(hm, d_dt * sgp, 0.0)
        ddt_ref[...] = d_raw.astype(_ACT)
        acc_ref[0:1, :] += jnp.sum(d_raw, axis=0, keepdims=True)
        acc_ref[1:2, :] += jnp.sum(d_alpha * dt, axis=0, keepdims=True) * a_row
        dd_row = jnp.sum(dy * xs, axis=0, keepdims=True)
        acc_ref[2:3, :] += _dot01_r(jnp.broadcast_to(dd_row, (SUB, D_INNER)), et)[0:1, :]
        dxs_ref[...] = dy * d_x + dxt * dt_x

    rev = lambda c: (nc - 1 - c, 0)
    small = pl.BlockSpec((SUB, LANE), lambda c: (0, 0))
    return pl.pallas_call(
        body, name=name,
        out_shape=(jax.ShapeDtypeStruct((t, D_INNER), F32), jax.ShapeDtypeStruct((t, 1024), F32),
                   jax.ShapeDtypeStruct((t, D_INNER), _ACT), jax.ShapeDtypeStruct((t, LANE), _ACT),
                   jax.ShapeDtypeStruct((SUB, LANE), F32), jax.ShapeDtypeStruct((SUB, D_INNER), F32)),
        grid=(nc,),
        in_specs=[pl.BlockSpec((CHUNK, D_INNER), rev),
                  pl.BlockSpec((CHUNK, D_INNER), rev),
                  pl.BlockSpec((CHUNK, D_INNER), rev),
                  pl.BlockSpec((CHUNK, 1024), rev),
                  pl.BlockSpec((CHUNK, LANE), lambda c: (nc - 1 - c, T_DT // LANE)),
                  pl.BlockSpec((CHUNK, D_INNER), lambda c: (nc - 1 - c, P_Z // D_INNER)),
                  pl.BlockSpec((N_STATE, D_INNER), rev),
                  small, small, small,
                  pl.BlockSpec((SUB, D_INNER), lambda c: (0, 0)),
                  pl.BlockSpec((LANE, D_INNER), lambda c: (0, 0)),
                  pl.BlockSpec((D_INNER, LANE), lambda c: (0, 0))],
        out_specs=(pl.BlockSpec((CHUNK, D_INNER), rev),
                   pl.BlockSpec((CHUNK, 1024), rev),
                   pl.BlockSpec((CHUNK, D_INNER), rev),
                   pl.BlockSpec((CHUNK, LANE), rev),
                   small,
                   pl.BlockSpec((SUB, D_INNER), lambda c: (0, 0))),
        scratch_shapes=[pltpu.VMEM((N_STATE, D_INNER), F32), pltpu.VMEM((CHUNK, D_INNER), F32)],
        compiler_params=_cp(("arbitrary",), vmem_mb=56),
    )(d_ys, y, xs_c, bc_c, tail, proj, hprev_all, dtb8, alog8, dsk8, nw8, e_bf, et_bf)


def _rel_tables():
    qi = np.arange(WIN)[:, None] + WIN
    kj = np.arange(2 * WIN)[None, :]
    rel = qi - kj
    n = np.maximum(rel, 0)
    max_exact = REL_BUCKETS // 2
    nf = np.maximum(n, 1).astype(np.float32)
    large = max_exact + (np.log(nf / np.float32(max_exact)) / np.float32(math.log(WIN / max_exact))
                         * np.float32(REL_BUCKETS - max_exact)).astype(np.int32)
    large = np.minimum(large, REL_BUCKETS - 1)
    bucket = np.where(n < max_exact, n, large)
    valid = (rel >= 0) & (rel < WIN)
    sink_col = np.broadcast_to(kj == 0, rel.shape)
    onehot = np.zeros((BIAS_ROWS, WIN * 2 * WIN), np.float32)
    flat_b = np.where(sink_col, REL_BUCKETS, bucket).reshape(-1)
    flat_v = (valid | sink_col).reshape(-1)
    first_v = ((valid & (kj >= WIN)) | sink_col).reshape(-1)
    idx = np.arange(WIN * 2 * WIN)
    onehot[flat_b[flat_v], idx[flat_v]] = 1.0
    return onehot, np.stack([first_v, flat_v]).astype(np.float32)


def _bias_expand(table_t, name):
    onehot, valid = _rel_tables()

    def body(rb_ref, oh_ref, v_ref, o_ref):
        full = _dot01_r(rb_ref[...], oh_ref[...])
        o_ref[0] = jnp.where(v_ref[0:1, :] > 0.5, full, NEG)
        o_ref[1] = jnp.where(v_ref[1:2, :] > 0.5, full, NEG)

    return pl.pallas_call(
        body, name=name, out_shape=jax.ShapeDtypeStruct((2, A_HEADS, WIN * 2 * WIN), F32),
        compiler_params=_cp(None),
    )(table_t, jnp.asarray(onehot, jnp.bfloat16), jnp.asarray(valid, F32))


def _bias_reduce(dbias, name):
    onehot, _ = _rel_tables()

    def body(d_ref, oh_ref, o_ref):
        acc = None
        r = d_ref[...]
        for _ in range(3):
            hi = r.astype(jnp.bfloat16)
            tt = lax.dot_general(hi, oh_ref[...], (((1,), (1,)), ((), ())), preferred_element_type=F32)
            acc = tt if acc is None else acc + tt
            r = r - hi.astype(F32)
        o_ref[...] = acc

    return pl.pallas_call(
        body, name=name, out_shape=jax.ShapeDtypeStruct((A_HEADS, BIAS_ROWS), F32),
        compiler_params=_cp(None),
    )(dbias, jnp.asarray(onehot, jnp.bfloat16))


def _attn_bands(kc_ref, kp_ref, vc_ref, vp_ref, has_prev):
    lo, hi = _half_masks()
    row = lax.broadcasted_iota(jnp.int32, (2 * WIN, 1), 0)
    keep = (row > 0).astype(F32)
    kb = jnp.concatenate([jnp.where(has_prev, kp_ref[...], 0.0), kc_ref[...]], axis=0) * (keep * (A_DH ** -0.5))
    vb = jnp.concatenate([jnp.where(has_prev, vp_ref[...], 0.0), vc_ref[...]], axis=0) * keep
    kr = pltpu.roll(kb, 64, 1)
    vr = pltpu.roll(vb, 64, 1)
    kk = ((kb * lo, kr * hi), (kr * lo, kb * hi))
    vv = ((vb * lo, vr * hi), (vr * lo, vb * hi))
    return kk, vv, (hi, lo)


def _attn_logits(q_ref, kk, lg_ref):
    for h in range(A_HEADS):
        j, half, kv = h // 2, h % 2, h // (A_HEADS // 2)
        lg_ref[h] = _dot_nt(q_ref[:, j * LANE:(j + 1) * LANE], kk[kv][half])


def _attn_fwd(proj, tail, bias, name):
    t = proj.shape[0]
    nb = t // WIN

    def body(q_ref, kc_ref, kp_ref, vc_ref, vp_ref, b_ref, o_ref, lg_ref, p_ref):
        n = pl.program_id(0)
        kk, vv, ones = _attn_bands(kc_ref, kp_ref, vc_ref, vp_ref, n > 0)
        _attn_logits(q_ref, kk, lg_ref)
        for h in range(A_HEADS):
            logits = lg_ref[h] + b_ref[h]
            p_ref[h] = jnp.exp(logits - jnp.max(logits, axis=-1, keepdims=True)).astype(_MXU)
        lane = lax.broadcasted_iota(jnp.int32, (1, LANE), 1)
        for j in range(A_HEADS // 2):
            kv = (2 * j) // (A_HEADS // 2)
            outs = []
            for half in range(2):
                o = jnp.dot(p_ref[2 * j + half], (vv[kv][half] + ones[half]).astype(_MXU), preferred_element_type=F32)
                outs.append(o / pltpu.roll(o, 64, 1))
            o_ref[:, j * LANE:(j + 1) * LANE] = jnp.where(lane < 64, outs[0], outs[1]).astype(_ACT)

    kvspec = lambda col, prev: pl.BlockSpec(
        (WIN, LANE), (lambda n: (jnp.maximum(n - 1, 0), col)) if prev else (lambda n: (n, col)))
    return pl.pallas_call(
        body, name=name, out_shape=jax.ShapeDtypeStruct((t, D_MODEL), _ACT),
        grid=(nb,),
        in_specs=[pl.BlockSpec((WIN, 1024), lambda n: (n, P_Q // 1024)),
                  kvspec(T_K // LANE, False), kvspec(T_K // LANE, True),
                  kvspec(T_V // LANE, False), kvspec(T_V // LANE, True),
                  pl.BlockSpec((None, A_HEADS, WIN, 2 * WIN), lambda n: (jnp.minimum(n, 1), 0, 0, 0))],
        out_specs=pl.BlockSpec((WIN, 1024), lambda n: (n, 0)),
        scratch_shapes=[pltpu.VMEM((A_HEADS, WIN, 2 * WIN), F32), pltpu.VMEM((A_HEADS, WIN, 2 * WIN), _MXU)],
        compiler_params=_cp(("parallel",)),
    )(proj, tail, tail, tail, tail, bias)


def _attn_bwd(proj, tail, bias, y_attn, d_out, name):
    t = proj.shape[0]
    nb = t // WIN

    def body(q_ref, kc_ref, kp_ref, vc_ref, vp_ref, b_ref, y_ref, do_ref,
             dq_ref, dk_ref, dv_ref, db_ref, ck_ref, cv_ref, lg_ref, dl_ref, p_ref):
        n = pl.program_id(0)

        @pl.when(n == 0)
        def _():
            db_ref[...] = jnp.zeros_like(db_ref)
            ck_ref[...] = jnp.zeros_like(ck_ref)
            cv_ref[...] = jnp.zeros_like(cv_ref)

        @pl.when(n < nb)
        def _():
            kk, vv, _ = _attn_bands(kc_ref, kp_ref, vc_ref, vp_ref, n > 0)
            lo, hi = _half_masks()
            ones_k = jnp.ones((2 * WIN, LANE), jnp.bfloat16)
            ones_d = jnp.ones((LANE, LANE), jnp.bfloat16)
            _attn_logits(q_ref, kk, lg_ref)
            for h in range(A_HEADS):
                j, half, kv = h // 2, h % 2, h // (A_HEADS // 2)
                msk = hi if half else lo
                logits = lg_ref[h] + b_ref[h]
                p = jnp.exp(logits - jnp.max(logits, axis=-1, keepdims=True))
                den = jnp.dot(p.astype(_MXU), ones_k.astype(_MXU), preferred_element_type=F32)
                dop = do_ref[:, j * LANE:(j + 1) * LANE].astype(F32)
                delta = _dot01_r(dop * y_ref[:, j * LANE:(j + 1) * LANE].astype(F32) * msk, ones_d, parts=2)
                inv = 1.0 / den
                probs = p * jnp.concatenate([inv, inv], axis=1)
                dprobs = _dot_nt(dop, vv[kv][half])
                dlog = probs * (dprobs - jnp.concatenate([delta, delta], axis=1))
                db_ref[h] += dlog
                dl_ref[h] = dlog.astype(_MXU)
                p_ref[h] = probs.astype(_MXU)
            dk_t = [[None, None], [None, None]]
            dv_t = [[None, None], [None, None]]
            for j in range(A_HEADS // 2):
                kv = (2 * j) // (A_HEADS // 2)
                qs = q_ref[:, j * LANE:(j + 1) * LANE].astype(F32) * (A_DH ** -0.5)
                dop = do_ref[:, j * LANE:(j + 1) * LANE].astype(F32)
                dq = None
                for half, msk in ((0, lo), (1, hi)):
                    h = 2 * j + half
                    dqh = jnp.dot(dl_ref[h], kk[kv][half].astype(_MXU), preferred_element_type=F32)
                    dq = dqh if dq is None else dq + dqh
                    dkh = lax.dot_general((qs * msk).astype(_MXU), dl_ref[h], (((0,), (0,)), ((), ())),
                                          preferred_element_type=F32)
                    dvh = lax.dot_general((dop * msk).astype(_MXU), p_ref[h], (((0,), (0,)), ((), ())),
                                          preferred_element_type=F32)
                    dk_t[kv][half] = dkh if dk_t[kv][half] is None else dk_t[kv][half] + dkh
                    dv_t[kv][half] = dvh if dv_t[kv][half] is None else dv_t[kv][half] + dvh
                dq_ref[:, j * LANE:(j + 1) * LANE] = dq.astype(_ACT)
            row = lax.broadcasted_iota(jnp.int32, (2 * WIN, 1), 0)

            def band(acc):
                a = (acc[0][0] + pltpu.roll(acc[0][1], 64, 0)) + (pltpu.roll(acc[1][0], 64, 0) + acc[1][1])
                return jnp.where(row > 0, a.T, 0.0)

            dkb = band(dk_t)
            dvb = band(dv_t)
            dk_ref[...] = (ck_ref[...] + dkb[0:WIN]).astype(_ACT)
            dv_ref[...] = (cv_ref[...] + dvb[0:WIN]).astype(_ACT)
            ck_ref[...] = dkb[WIN:]
            cv_ref[...] = dvb[WIN:]

        @pl.when(n == nb)
        def _():
            dk_ref[...] = ck_ref[...].astype(_ACT)
            dv_ref[...] = cv_ref[...].astype(_ACT)

    cur = lambda n: jnp.minimum(n, nb - 1)
    prv = lambda n: jnp.maximum(jnp.minimum(n, nb - 1) - 1, 0)
    kvspec = lambda col, prev: pl.BlockSpec(
        (WIN, LANE), (lambda n: (prv(n), col)) if prev else (lambda n: (cur(n), col)))
    band_shape = (A_HEADS, WIN, 2 * WIN)
    return pl.pallas_call(
        body, name=name,
        out_shape=(jax.ShapeDtypeStruct((t, D_MODEL), _ACT), jax.ShapeDtypeStruct((t, LANE), _ACT),
                   jax.ShapeDtypeStruct((t, LANE), _ACT), jax.ShapeDtypeStruct(band_shape, F32)),
        grid=(nb + 1,),
        in_specs=[pl.BlockSpec((WIN, 1024), lambda n: (cur(n), P_Q // 1024)),
                  kvspec(T_K // LANE, False), kvspec(T_K // LANE, True),
                  kvspec(T_V // LANE, False), kvspec(T_V // LANE, True),
                  pl.BlockSpec((None,) + band_shape, lambda n: (jnp.minimum(n, 1), 0, 0, 0)),
                  pl.BlockSpec((WIN, 1024), lambda n: (cur(n), 0)),
                  pl.BlockSpec((WIN, 1024), lambda n: (cur(n), 0))],
        out_specs=(pl.BlockSpec((WIN, 1024), lambda n: (cur(n), 0)),
                   pl.BlockSpec((WIN, LANE), lambda n: (jnp.maximum(n - 1, 0), 0)),
                   pl.BlockSpec((WIN, LANE), lambda n: (jnp.maximum(n - 1, 0), 0)),
                   pl.BlockSpec(band_shape, lambda n: (0, 0, 0))),
        scratch_shapes=[pltpu.VMEM((WIN, LANE), F32), pltpu.VMEM((WIN, LANE), F32),
                        pltpu.VMEM(band_shape, F32), pltpu.VMEM(band_shape, _MXU), pltpu.VMEM(band_shape, _MXU)],
        compiler_params=_cp(("arbitrary",)),
    )(proj, tail, tail, tail, tail, bias, y_attn, d_out)


def _merge_fwd(bs, ba, proj, bg8, name):
    t = bs.shape[0]
    tm = _tm_rows(t)

    def body(bs_ref, ba_ref, gs_ref, ga_ref, bgs_ref, bga_ref, o_ref):
        g_s = _sigmoid(gs_ref[...] + bgs_ref[0:1, :])
        g_a = _sigmoid(ga_ref[...] + bga_ref[0:1, :])
        o_ref[...] = (g_s * bs_ref[...] + g_a * ba_ref[...]).astype(_ACT)

    row = lambda col: pl.BlockSpec((tm, 1024), lambda i: (i, col))
    return pl.pallas_call(
        body, name=name, out_shape=jax.ShapeDtypeStruct((t, D_MODEL), _ACT), grid=(t // tm,),
        in_specs=[row(0), row(0), row(P_G // 1024), row(P_G // 1024 + 1),
                  pl.BlockSpec((SUB, 1024), lambda i: (0, 0)), pl.BlockSpec((SUB, 1024), lambda i: (0, 1))],
        out_specs=row(0), compiler_params=_cp(("parallel",)),
    )(bs, ba, proj, proj, bg8, bg8)


def _merge_bwd(d_merged, bs, ba, proj, bg8, name):
    t = bs.shape[0]
    tm = _tm_rows(t)

    def body(dm_ref, bs_ref, ba_ref, gs_ref, ga_ref, bgs_ref, bga_ref, dbs_ref, dba_ref, dg_ref, acc_ref):
        @pl.when(pl.program_id(0) == 0)
        def _():
            acc_ref[...] = jnp.zeros_like(acc_ref)

        dm = dm_ref[...].astype(F32)
        g_s = _sigmoid(gs_ref[...] + bgs_ref[0:1, :])
        g_a = _sigmoid(ga_ref[...] + bga_ref[0:1, :])
        dbs_ref[...] = (dm * g_s).astype(_ACT)
        dba_ref[...] = (dm * g_a).astype(_ACT)
        dgs = dm * bs_ref[...].astype(F32) * g_s * (1.0 - g_s)
        dga = dm * ba_ref[...].astype(F32) * g_a * (1.0 - g_a)
        dg_ref[:, 0:1024] = dgs.astype(_ACT)
        dg_ref[:, 1024:2048] = dga.astype(_ACT)
        acc_ref[0:1, 0:1024] += jnp.sum(dgs, axis=0, keepdims=True)
        acc_ref[0:1, 1024:2048] += jnp.sum(dga, axis=0, keepdims=True)

    row = lambda col: pl.BlockSpec((tm, 1024), lambda i: (i, col))
    return pl.pallas_call(
        body, name=name,
        out_shape=(jax.ShapeDtypeStruct((t, D_MODEL), _ACT), jax.ShapeDtypeStruct((t, D_MODEL), _ACT),
                   jax.ShapeDtypeStruct((t, 2048), _ACT), jax.ShapeDtypeStruct((SUB, 2048), F32)),
        grid=(t // tm,),
        in_specs=[row(0), row(0), row(0), row(P_G // 1024), row(P_G // 1024 + 1),
                  pl.BlockSpec((SUB, 1024), lambda i: (0, 0)), pl.BlockSpec((SUB, 1024), lambda i: (0, 1))],
        out_specs=(row(0), row(0), pl.BlockSpec((tm, 2048), lambda i: (i, 0)),
                   pl.BlockSpec((SUB, 2048), lambda i: (0, 0))),
        compiler_params=_cp(("arbitrary",)),
    )(d_merged, bs, ba, proj, proj, bg8, bg8)


def _ln_stats(r):
    mu = jnp.mean(r, axis=-1, keepdims=True)
    xc = r - mu
    var = jnp.mean(xc * xc, axis=-1, keepdims=True)
    rstd = lax.rsqrt(var + LN_EPS)
    return xc * rstd, rstd


def _ln_bwd(dxhat, xhat, rstd):
    return rstd * (dxhat - jnp.mean(dxhat, axis=-1, keepdims=True)
                   - xhat * jnp.mean(dxhat * xhat, axis=-1, keepdims=True))


def _ln1_fwd(x, mix, g8, b8, name):
    t = x.shape[0]
    tm = _tm_rows(t)

    def body(x_ref, m_ref, g_ref, b_ref, xh_ref, h_ref, rs_ref):
        xhat, rstd = _ln_stats(ALPHA * x_ref[...] + m_ref[...])
        xh_ref[...] = xhat
        h_ref[...] = (xhat * g_ref[0:1, :] + b_ref[0:1, :]).astype(_ACT)
        rs_ref[...] = rstd

    row = pl.BlockSpec((tm, D_MODEL), lambda i: (i, 0))
    par = pl.BlockSpec((SUB, D_MODEL), lambda i: (0, 0))
    return pl.pallas_call(
        body, name=name,
        out_shape=(jax.ShapeDtypeStruct((t, D_MODEL), F32), jax.ShapeDtypeStruct((t, D_MODEL), _ACT),
                   jax.ShapeDtypeStruct((t, 1), F32)),
        grid=(t // tm,), in_specs=[row, row, par, par],
        out_specs=(row, row, pl.BlockSpec((tm, 1), lambda i: (i, 0))),
        compiler_params=_cp(("parallel",)),
    )(x, mix, g8, b8)


def _ln2_loss(xhat1, ffn, target, g1_8, b1_8, g2_8, b2_8, name):
    t = xhat1.shape[0]
    tm = _tm_rows(t)

    def body(xh_ref, f_ref, t_ref, g1_ref, b1_ref, g2_ref, b2_ref, d_ref, db_ref, acc_ref):
        @pl.when(pl.program_id(0) == 0)
        def _():
            acc_ref[...] = jnp.zeros_like(acc_ref)

        h1 = xh_ref[...] * g1_ref[0:1, :] + b1_ref[0:1, :]
        xhat, rstd = _ln_stats(ALPHA * h1 + f_ref[...])
        diff = xhat * g2_ref[0:1, :] + b2_ref[0:1, :] - t_ref[...]
        dy = diff * (1.0 / D_MODEL)
        acc_ref[0:1, :] += jnp.sum(dy * xhat, axis=0, keepdims=True)
        acc_ref[1:2, :] += jnp.sum(dy, axis=0, keepdims=True)
        acc_ref[2:3, :] += jnp.sum(diff * diff, axis=0, keepdims=True)
        d = _ln_bwd(dy * g2_ref[0:1, :], xhat, rstd)
        d_ref[...] = d
        db_ref[...] = d.astype(_ACT)

    row = pl.BlockSpec((tm, D_MODEL), lambda i: (i, 0))
    par = pl.BlockSpec((SUB, D_MODEL), lambda i: (0, 0))
    return pl.pallas_call(
        body, name=name,
        out_shape=(jax.ShapeDtypeStruct((t, D_MODEL), F32), jax.ShapeDtypeStruct((t, D_MODEL), _ACT),
                   jax.ShapeDtypeStruct((SUB, D_MODEL), F32)),
        grid=(t // tm,), in_specs=[row, row, row, par, par, par, par],
        out_specs=(row, row, par), compiler_params=_cp(("arbitrary",)),
    )(xhat1, ffn, target, g1_8, b1_8, g2_8, b2_8)


def _ln1_bwd(d_r2, d_h1_ffn, xhat1, rstd1, g1_8, name):
    t = xhat1.shape[0]
    tm = _tm_rows(t)

    def body(d2_ref, df_ref, xh_ref, rs_ref, g_ref, d_ref, db_ref, acc_ref):
        @pl.when(pl.program_id(0) == 0)
        def _():
            acc_ref[...] = jnp.zeros_like(acc_ref)

        dh = ALPHA * d2_ref[...] + df_ref[...]
        xhat = xh_ref[...]
        acc_ref[0:1, :] += jnp.sum(dh * xhat, axis=0, keepdims=True)
        acc_ref[1:2, :] += jnp.sum(dh, axis=0, keepdims=True)
        d = _ln_bwd(dh * g_ref[0:1, :], xhat, rs_ref[...])
        d_ref[...] = d
        db_ref[...] = d.astype(_ACT)

    row = pl.BlockSpec((tm, D_MODEL), lambda i: (i, 0))
    par = pl.BlockSpec((SUB, D_MODEL), lambda i: (0, 0))
    return pl.pallas_call(
        body, name=name,
        out_shape=(jax.ShapeDtypeStruct((t, D_MODEL), F32), jax.ShapeDtypeStruct((t, D_MODEL), _ACT),
                   jax.ShapeDtypeStruct((SUB, D_MODEL), F32)),
        grid=(t // tm,), in_specs=[row, row, row, pl.BlockSpec((tm, 1), lambda i: (i, 0)), par],
        out_specs=(row, row, par), compiler_params=_cp(("arbitrary",)),
    )(d_r2, d_h1_ffn, xhat1, rstd1, g1_8)


def _ffn_tm(t):
    return min(128, t)


def _ffn_act_fwd(u0, cw8, cb8, name):
    t = u0.shape[0]
    tm = _ffn_tm(t)

    def body(g_ref, gp_ref, v_ref, vp_ref, wg_ref, wv_ref, bg_ref, bv_ref, o_ref, u_ref):
        i = pl.program_id(0)
        gprev = jnp.where(i > 0, gp_ref[SUB:HALO, :].astype(F32), 0.0)
        vprev = jnp.where(i > 0, vp_ref[SUB:HALO, :].astype(F32), 0.0)
        gate = _conv_pre(g_ref[...].astype(F32), gprev, wg_ref, bg_ref[0:1, :], FFN_K)
        val = _conv_pre(v_ref[...].astype(F32), vprev, wv_ref, bv_ref[0:1, :], FFN_K)
        o_ref[...] = (gate * _sigmoid(gate) * val).astype(_ACT)
        u_ref[:, 0:D_FF] = gate.astype(_ACT)
        u_ref[:, D_FF:2 * D_FF] = val.astype(_ACT)

    cur = lambda col: pl.BlockSpec((tm, D_FF), lambda i: (i, col))
    prv = lambda col: _prev_halo(tm, D_FF, lambda i: (i, col))
    par = lambda col: pl.BlockSpec((SUB, D_FF), lambda i: (0, col))
    return pl.pallas_call(
        body, name=name,
        out_shape=(jax.ShapeDtypeStruct((t, D_FF), _ACT), jax.ShapeDtypeStruct((t, 2 * D_FF), _ACT)),
        grid=(t // tm,),
        in_specs=[cur(0), prv(0), cur(1), prv(1), par(0), par(1), par(0), par(1)],
        out_specs=(pl.BlockSpec((tm, D_FF), lambda i: (i, 0)), pl.BlockSpec((tm, 2 * D_FF), lambda i: (i, 0))),
        compiler_params=_cp(("parallel",)),
    )(u0, u0, u0, u0, cw8, cw8, cb8, cb8)


def _ffn_act_bwd(u0, u, cw8, d_a, name):
    t = u0.shape[0]
    tm = _ffn_tm(t)
    nt = t // tm

    def body(g0_ref, v0_ref, g_ref, gn_ref, v_ref, vn_ref, wg_ref, wv_ref, da_ref, dan_ref, du_ref, acc_ref):
        i = pl.program_id(0)

        @pl.when(i == 0)
        def _():
            acc_ref[...] = jnp.zeros_like(acc_ref)

        def grads(gate, val, da):
            return da * val * _silu_grad(gate), da * gate * _sigmoid(gate)

        dgate, dval = grads(g_ref[...].astype(F32), v_ref[...].astype(F32), da_ref[...].astype(F32))
        dgate_n, dval_n = grads(gn_ref[0:SUB, :].astype(F32), vn_ref[0:SUB, :].astype(F32),
                                dan_ref[0:SUB, :].astype(F32))
        last = i == nt - 1
        du_ref[:, 0:D_FF] = _conv_grads(dgate, jnp.where(last, 0.0, dgate_n), g0_ref[...].astype(F32), wg_ref,
                                        acc_ref, FFN_K, slice(0, D_FF)).astype(_ACT)
        du_ref[:, D_FF:2 * D_FF] = _conv_grads(dval, jnp.where(last, 0.0, dval_n), v0_ref[...].astype(F32), wv_ref,
                                               acc_ref, FFN_K, slice(D_FF, 2 * D_FF)).astype(_ACT)

    cur = lambda col: pl.BlockSpec((tm, D_FF), lambda i: (i, col))
    nxt = lambda col: _next_halo(tm, t, D_FF, lambda i: (i, col))
    par = lambda col: pl.BlockSpec((SUB, D_FF), lambda i: (0, col))
    return pl.pallas_call(
        body, name=name,
        out_shape=(jax.ShapeDtypeStruct((t, 2 * D_FF), _ACT), jax.ShapeDtypeStruct((SUB, 2 * D_FF), F32)),
        grid=(nt,),
        in_specs=[cur(0), cur(1), cur(0), nxt(0), cur(1), nxt(1), par(0), par(1), cur(0), nxt(0)],
        out_specs=(pl.BlockSpec((tm, 2 * D_FF), lambda i: (i, 0)),
                   pl.BlockSpec((SUB, 2 * D_FF), lambda i: (0, 0))),
        compiler_params=_cp(("arbitrary",)),
    )(u0, u0, u, u, u, u, cw8, cw8, d_a, d_a)


def _local_step(x, target, wts):
    t = x.shape[0]
    wp = wts["wp"]
    w_bs, w_ba, w_mix, w_up, w_dn = (wts[k] for k in ("w_branch_ssm", "w_branch_attn", "w_mix_out", "w_up", "w_down"))
    scw = wts["ssm_conv_w"]
    scb = wts["ssm_conv_b"]
    fcw8 = _rows8(wts["ffn_conv_w"])
    fcb8 = _rows8(wts["ffn_conv_b"])
    pad_lane = lambda p: jnp.concatenate([p.astype(F32), jnp.zeros((1, LANE - p.shape[1]), F32)], axis=1)
    dtb8 = _rows8(pad_lane(wts["ssm_dt_bias"]))
    alog8 = _rows8(pad_lane(wts["ssm_a_log"]))
    dsk8 = _rows8(pad_lane(wts["ssm_d"]))
    bias_table = jnp.concatenate([wts["rel_bias"].T.astype(F32), wts["attn_sinks"].T.astype(F32),
                                  jnp.zeros((A_HEADS, BIAS_ROWS - REL_BUCKETS - 1), F32)], axis=1)
    nw8 = _rows8(wts["ssm_norm_w"])
    bg8 = _rows8(wts["b_gate"])
    g1_8, b1_8, g2_8, b2_8 = (_rows8(wts[k]) for k in ("ln1_g", "ln1_b", "ln2_g", "ln2_b"))
    xs_w8, xs_b8 = _rows8(scw[:, :D_INNER]), _rows8(scb[:, :D_INNER])
    bc_w8, bc_b8 = _rows8(scw[:, D_INNER:]), _rows8(scb[:, D_INNER:])

    x_bf = x.astype(_ACT)
    proj = _mm(x_bf, wp[:, :P_MAIN], "mm_in", out_dtype=_ACT)
    tail = _mm(x_bf, wp[:, P_MAIN:], "mm_in_tail")
    xs_c, xs_pre = _conv_silu_fwd(proj, P_XS // _TC, D_INNER // _TC, xs_w8, xs_b8, "conv_xs_fwd")
    bc_c, bc_pre = _conv_silu_fwd(proj, P_BC // _TC, 1024 // _TC, bc_w8, bc_b8, "conv_bc_fwd")
    y_ssd, y_ssm, hprev = _ssd_fwd(xs_c, bc_c, proj, tail, dtb8, alog8, dsk8, nw8, "ssd_fwd")
    bias = _bias_expand(bias_table, "bias_expand").reshape(2, A_HEADS, WIN, 2 * WIN)
    y_attn = _attn_fwd(proj, tail, bias, "attn_fwd")
    bs = _mm(y_ssm, w_bs, "mm_bs", out_dtype=_ACT)
    ba = _mm(y_attn, w_ba, "mm_ba", out_dtype=_ACT)
    merged = _merge_fwd(bs, ba, proj, bg8, "merge_fwd")
    mix = _mm(merged, w_mix, "mm_mix", out_dtype=_ACT)
    xhat1, h1_bf, rstd1 = _ln1_fwd(x, mix, g1_8, b1_8, "ln1_fwd")
    u0 = _mm(h1_bf, w_up, "mm_up", out_dtype=_ACT)
    act, u_conv = _ffn_act_fwd(u0, fcw8, fcb8, "ffn_act_fwd")
    ffn = _mm(act, w_dn, "mm_down", out_dtype=_ACT)
    d_r2, d_r2_bf, acc_ln2 = _ln2_loss(xhat1, ffn, target, g1_8, b1_8, g2_8, b2_8, "ln2_loss")
    d_w_dn = _mm(act, d_r2_bf, "mm_dw_down", trans_a=True)
    d_act = _mm(d_r2_bf, w_dn.T, "mm_d_act", out_dtype=_ACT)
    d_u0, acc_ffn = _ffn_act_bwd(u0, u_conv, fcw8, d_act, "ffn_act_bwd")
    d_w_up = _mm(h1_bf, d_u0, "mm_dw_up", trans_a=True)
    d_h1_ffn = _mm(d_u0, w_up.T, "mm_d_h1", out_dtype=_ACT)
    d_r1, d_r1_bf, acc_ln1 = _ln1_bwd(d_r2, d_h1_ffn, xhat1, rstd1, g1_8, "ln1_bwd")
    d_w_mix = _mm(merged, d_r1_bf, "mm_dw_mix", trans_a=True)
    d_merged = _mm(d_r1_bf, w_mix.T, "mm_d_merged", out_dtype=_ACT)
    d_bs, d_ba, d_gates, acc_bg = _merge_bwd(d_merged, bs, ba, proj, bg8, "merge_bwd")
    d_w_bs = _mm(y_ssm, d_bs, "mm_dw_bs", trans_a=True)
    d_w_ba = _mm(y_attn, d_ba, "mm_dw_ba", trans_a=True)
    d_y_ssm = _mm(d_bs, w_bs.T, "mm_d_yssm", out_dtype=_ACT)
    d_y_attn = _mm(d_ba, w_ba.T, "mm_d_yattn", out_dtype=_ACT)
    d_q, d_k, d_v, d_bias = _attn_bwd(proj, tail, bias, y_attn, d_y_attn, "attn_bwd")
    d_table = _bias_reduce(d_bias.reshape(A_HEADS, WIN * 2 * WIN), "bias_reduce")
    d_xs_c, d_bc_c, d_z, d_dt, acc_ssd, acc_nw = _ssd_bwd(
        d_y_ssm, y_ssd, xs_c, bc_c, proj, tail, hprev, dtb8, alog8, dsk8, nw8, "ssd_bwd")
    d_xs, acc_xs = _conv_silu_bwd(proj, P_XS // _TC, D_INNER // _TC, xs_pre, xs_w8, d_xs_c, "conv_xs_bwd")
    d_bc, acc_bc = _conv_silu_bwd(proj, P_BC // _TC, 1024 // _TC, bc_pre, bc_w8, d_bc_c, "conv_bc_bwd")
    d_proj = jnp.concatenate([d_z, d_xs, d_gates, d_q, d_bc, d_k, d_v, d_dt,
                              jnp.zeros((t, P_W - P_DT - LANE), _ACT)], axis=1)
    d_wp = _mm(x_bf, d_proj, "mm_dw_in", trans_a=True)
    d_x = _mm(d_proj, wp.T, "mm_d_x", res=d_r1, res_scale=ALPHA)

    grads = {
        "wp": d_wp,
        "ssm_conv_w": jnp.concatenate([acc_xs[0:SSM_K], acc_bc[0:SSM_K]], axis=1),
        "w_branch_ssm": d_w_bs, "w_branch_attn": d_w_ba, "w_mix_out": d_w_mix,
        "w_up": d_w_up, "ffn_conv_w": acc_ffn[0:FFN_K], "w_down": d_w_dn,
    }
    small = {
        "rel_bias": d_table[:, 0:REL_BUCKETS].T,
        "b_gate": acc_bg[0:1],
        "ssm_conv_b": jnp.concatenate([acc_xs[SSM_K:SSM_K + 1], acc_bc[SSM_K:SSM_K + 1]], axis=1),
        "ssm_dt_bias": acc_ssd[0:1, 0:N_HEADS], "ssm_a_log": acc_ssd[1:2, 0:N_HEADS], "ssm_d": acc_ssd[2:3, 0:N_HEADS],
        "ssm_norm_w": acc_nw[0:1],
        "attn_sinks": d_table[:, REL_BUCKETS:REL_BUCKETS + 1].T,
        "ln1_g": acc_ln1[0:1], "ln1_b": acc_ln1[1:2],
        "ffn_conv_b": acc_ffn[FFN_K:FFN_K + 1],
        "ln2_g": acc_ln2[0:1], "ln2_b": acc_ln2[1:2],
        "loss_lanes": acc_ln2[2:3],
    }
    return d_x, grads, small


_MATS = (("w_in", (1024, 2120), 1), ("w_branch_ssm", (512, 1024), 0), ("w_branch_attn", (256, 1024), 0),
         ("w_mix_out", (256, 1024), 0), ("w_up", (1024, 1408), 1), ("w_down", (704, 1024), 0))
_CONVS = (("ssm_conv_w", (4, 768)), ("ffn_conv_w", (3, 1408)))
_CONV_ROWS = 64

_SMALL = (("rel_bias", (32, 16)), ("b_gate", (1, 2048)), ("ssm_conv_b", (1, 3072)), ("ssm_dt_bias", (1, 32)),
          ("ssm_a_log", (1, 32)), ("ssm_d", (1, 32)), ("ssm_norm_w", (1, 2048)), ("attn_sinks", (1, 16)),
          ("ln1_g", (1, 1024)), ("ln1_b", (1, 1024)), ("ffn_conv_b", (1, 5632)), ("ln2_g", (1, 1024)),
          ("ln2_b", (1, 1024)), ("g_ssm_conv_w", (4, 3072)), ("g_ffn_conv_w", (3, 5632)), ("loss_lanes", (1, 1024)))


def _small_rows(shape):
    rows = -(-(shape[0] * shape[1]) // LANE)
    return -(-rows // SUB) * SUB


def _as_rows(a, rows, dtype):
    flat = a.reshape(-1).astype(dtype)
    flat = jnp.concatenate([flat, jnp.zeros((rows * LANE - flat.shape[0],), dtype)])
    return flat.reshape(rows, LANE)


def _pack_small(parts):
    blocks = [_as_rows(parts[n], _small_rows(s), F32) if n in parts else jnp.zeros((_small_rows(s), LANE), F32)
              for n, s in _SMALL]
    return jnp.concatenate(blocks, axis=0)


def _unpack_small(packed):
    out, at = {}, 0
    for n, s in _SMALL:
        rows = _small_rows(s)
        out[n] = packed[at:at + rows].reshape(-1)[:s[0] * s[1]].reshape(s)
        at += rows
    return out


def _to_stack(full, shape, axis):
    if axis == 0:
        return full.reshape((N_CHIPS,) + shape)
    return jnp.transpose(full.reshape(shape[0], N_CHIPS, shape[1]), (1, 0, 2))


def _from_stack(stack, axis):
    n, r, c = stack.shape
    if axis == 0:
        return stack.reshape(n * r, c)
    return jnp.transpose(stack, (1, 0, 2)).reshape(r, n * c)


_IN_SHARD = IN_COLS // N_CHIPS


def _cols_of_stack(stack, o, w):
    parts = []
    while w > 0:
        j, a = divmod(o, _IN_SHARD)
        n = min(w, _IN_SHARD - a)
        parts.append(stack[j][:, a:a + n])
        o, w = o + n, w - n
    return parts


def _pack_w_in_stack(stack):
    cols, at = [], 0
    for o, w, pk in sorted(_PIECES, key=lambda p: p[2]):
        if pk > at:
            cols.append(jnp.zeros((stack.shape[1], pk - at), stack.dtype))
        cols += _cols_of_stack(stack, o, w)
        at = pk + w
    cols.append(jnp.zeros((stack.shape[1], P_W - at), stack.dtype))
    return jnp.concatenate(cols, axis=1)


def _unpack_w_in_stack(wp):
    slabs = []
    for j in range(N_CHIPS):
        lo, hi = j * _IN_SHARD, (j + 1) * _IN_SHARD
        cols = []
        for o, w, pk in sorted(_PIECES):
            a, b = max(o, lo), min(o + w, hi)
            if a < b:
                cols.append(wp[:, pk + a - o:pk + b - o])
        slabs.append(jnp.concatenate(cols, axis=1))
    return jnp.stack(slabs)


_MESH = pl.DeviceIdType.MESH
_HBM = pl.BlockSpec(memory_space=pltpu.HBM)


def _position():
    return lax.axis_index("x"), lax.axis_index("y"), lax.axis_index("c")


def _other_chips(x, y):
    return ((1 - x, y), (x, 1 - y), (1 - x, 1 - y))


def _remote(src, dst, send_sem, recv_sem, to):
    return pltpu.make_async_remote_copy(src_ref=src, dst_ref=dst, send_sem=send_sem, recv_sem=recv_sem,
                                        device_id=to, device_id_type=_MESH)


def _allgather_weights(shards):
    n = len(shards)

    def body(*refs):
        s_refs, o_refs = refs[:n], refs[n:2 * n]
        send_sems, recv_sems, local_sems = refs[2 * n:]
        x, y, c = _position()
        me = 2 * x + y
        sib = (x, y, 1 - c)
        chips = _other_chips(x, y)
        local = [pltpu.make_async_copy(s_refs[m], o_refs[m].at[me], local_sems.at[m]) for m in range(n)]
        for cp in local:
            cp.start()

        def copy(m, k, chip_idx, half, to, src=None):
            dst = o_refs[m].at[chip_idx, half]
            return _remote(dst if src is None else src, dst, send_sems.at[6 * m + k], recv_sems.at[6 * m + k], to)

        first = [copy(m, i, me, c, (cx, cy, c), src=s_refs[m].at[c])
                 for i, (cx, cy) in enumerate(chips) for m in range(n)]
        for cp in first:
            cp.start()
        passed = []
        for i, (cx, cy) in enumerate(chips):
            for m in range(n):
                copy(m, i, 2 * cx + cy, c, sib).wait_recv()
                passed.append(copy(m, 3 + i, 2 * cx + cy, c, sib))
                passed[-1].start()
        for i, (cx, cy) in enumerate(chips):
            for m in range(n):
                copy(m, 3 + i, 2 * cx + cy, 1 - c, sib).wait_recv()
        for cp in first + passed:
            cp.wait_send()
        for cp in local:
            cp.wait()

    return pl.pallas_call(
        body, name="allgather_weights",
        out_shape=[jax.ShapeDtypeStruct((N_CHIPS,) + s.shape, s.dtype) for s in shards],
        in_specs=[_HBM] * n, out_specs=[_HBM] * n,
        scratch_shapes=[pltpu.SemaphoreType.DMA((6 * n,)), pltpu.SemaphoreType.DMA((6 * n,)),
                        pltpu.SemaphoreType.DMA((n,))],
    )(*shards)


def _swap_halves(gs):
    n = len(gs)

    def body(*refs):
        g_refs, o_refs = refs[:n], refs[n:2 * n]
        send_sems, recv_sems = refs[2 * n:]
        x, y, c = _position()
        cps = [_remote(g_refs[m].at[j, 1 - c], o_refs[m].at[j], send_sems.at[N_CHIPS * m + j],
                       recv_sems.at[N_CHIPS * m + j], (x, y, 1 - c)) for m in range(n) for j in range(N_CHIPS)]
        for cp in cps:
            cp.start()
        for cp in cps:
            cp.wait()

    return pl.pallas_call(
        body, name="swap_halves",
        out_shape=[jax.ShapeDtypeStruct((N_CHIPS,) + g.shape[2:], g.dtype) for g in gs],
        in_specs=[_HBM] * n, out_specs=[_HBM] * n,
        scratch_shapes=[pltpu.SemaphoreType.DMA((N_CHIPS * n,)), pltpu.SemaphoreType.DMA((N_CHIPS * n,))],
    )(*gs)


def _scatter_chips(ps):
    n = len(ps)

    def body(*refs):
        p_refs, o_refs = refs[:n], refs[n:2 * n]
        send_sems, recv_sems = refs[2 * n:]
        x, y, c = _position()
        cps = [_remote(p_refs[m].at[2 * cx + cy], o_refs[m].at[i], send_sems.at[3 * m + i], recv_sems.at[3 * m + i],
                       (cx, cy, c)) for i, (cx, cy) in enumerate(_other_chips(x, y)) for m in range(n)]
        for cp in cps:
            cp.start()
        for cp in cps:
            cp.wait()

    return pl.pallas_call(
        body, name="scatter_chips",
        out_shape=[jax.ShapeDtypeStruct((N_CHIPS - 1,) + p.shape[1:], p.dtype) for p in ps],
        in_specs=[_HBM] * n, out_specs=[_HBM] * n,
        scratch_shapes=[pltpu.SemaphoreType.DMA((3 * n,)), pltpu.SemaphoreType.DMA((3 * n,))],
    )(*ps)


def _join_halves(reds):
    n = len(reds)

    def body(*refs):
        r_refs, o_refs = refs[:n], refs[n:2 * n]
        send_sems, recv_sems, local_sems = refs[2 * n:]
        x, y, c = _position()
        local = [pltpu.make_async_copy(r_refs[m], o_refs[m].at[c], local_sems.at[m]) for m in range(n)]
        cps = [_remote(r_refs[m], o_refs[m].at[c], send_sems.at[m], recv_sems.at[m], (x, y, 1 - c)) for m in range(n)]
        for cp in local + cps:
            cp.start()
        for cp in cps + local:
            cp.wait()

    return pl.pallas_call(
        body, name="join_halves",
        out_shape=[jax.ShapeDtypeStruct((2,) + r.shape, r.dtype) for r in reds],
        in_specs=[_HBM] * n, out_specs=[_HBM] * n,
        scratch_shapes=[pltpu.SemaphoreType.DMA((n,)), pltpu.SemaphoreType.DMA((n,)), pltpu.SemaphoreType.DMA((n,))],
    )(*reds)


def _allgather_small(mine, name):
    m_per, n = mine.shape

    def body(x_ref, out_ref, send_sems, recv_sems, local_sem):
        x, y, c = _position()
        me, sibling = (x, y, c), (x, y, 1 - c)
        chips = _other_chips(x, y)

        def rows(px, py, pc):
            return out_ref.at[pl.ds((4 * px + 2 * py + pc) * m_per, m_per), :]

        def copy(k, block, to, src=None):
            return pltpu.make_async_remote_copy(src_ref=rows(*block) if src is None else src, dst_ref=rows(*block),
                                                send_sem=send_sems.at[k], recv_sem=recv_sems.at[k],
                                                device_id=to, device_id_type=_MESH)

        own = pltpu.make_async_copy(x_ref, rows(*me), local_sem)
        own.start()
        first = [copy(0, me, sibling, src=x_ref)]
        first += [copy(1 + j, me, (*chip, c), src=x_ref) for j, chip in enumerate(chips)]
        for cp in first:
            cp.start()
        passed = [copy(4 + j, (*chip, c), sibling) for j, chip in enumerate(chips)]
        for j, chip in enumerate(chips):
            copy(1 + j, (*chip, c), me).wait_recv()
            passed[j].start()
        copy(0, sibling, me).wait_recv()
        for j, chip in enumerate(chips):
            copy(4 + j, (*chip, 1 - c), me).wait_recv()
        for cp in first + passed:
            cp.wait_send()
        own.wait()

    return pl.pallas_call(
        body, name=name, out_shape=jax.ShapeDtypeStruct((N_DEV * m_per, n), mine.dtype),
        in_specs=[pl.BlockSpec(memory_space=pltpu.VMEM)], out_specs=pl.BlockSpec(memory_space=pltpu.VMEM),
        scratch_shapes=[pltpu.SemaphoreType.DMA((7,)), pltpu.SemaphoreType.DMA((7,)), pltpu.SemaphoreType.DMA],
    )(mine)


_ADD_BLOCK_BYTES = 3 << 20


def _add_rows(hr, cols):
    if hr * cols * 4 <= _ADD_BLOCK_BYTES:
        return hr
    return _pick(hr, (256, 128, 64, 32, 16))


def _add_own_half(g, recv, c_idx, name):
    nseg, _, hr, cols = g.shape
    tr = _add_rows(hr, cols)

    def body(c_ref, g_ref, r_ref, o_ref, ob_ref):
        s = g_ref[...] + r_ref[...]
        o_ref[...] = s
        ob_ref[...] = s.astype(jnp.bfloat16)

    blk = pl.BlockSpec((None, tr, cols), lambda j, i, c_ref: (j, i, 0))
    return pl.pallas_call(
        body, name=name,
        out_shape=(jax.ShapeDtypeStruct((nseg, hr, cols), F32), jax.ShapeDtypeStruct((nseg, hr, cols), jnp.bfloat16)),
        grid_spec=pltpu.PrefetchScalarGridSpec(
            num_scalar_prefetch=1, grid=(nseg, hr // tr),
            in_specs=[pl.BlockSpec((None, None, tr, cols), lambda j, i, c_ref: (j, c_ref[0], i, 0)), blk],
            out_specs=(blk, blk)),
        compiler_params=_cp(("parallel", "parallel")),
    )(c_idx, g, recv)


def _add_chips(p, recv, chip_idx, name):
    _, hr, cols = p.shape
    tr = _add_rows(hr, cols)

    def body(j_ref, p_ref, r_ref, o_ref):
        o_ref[...] = ((p_ref[...] + r_ref[0].astype(F32)) + r_ref[1].astype(F32)) + r_ref[2].astype(F32)

    return pl.pallas_call(
        body, name=name, out_shape=jax.ShapeDtypeStruct((hr, cols), F32),
        grid_spec=pltpu.PrefetchScalarGridSpec(
            num_scalar_prefetch=1, grid=(hr // tr,),
            in_specs=[pl.BlockSpec((None, tr, cols), lambda i, j_ref: (j_ref[0], i, 0)),
                      pl.BlockSpec((N_CHIPS - 1, tr, cols), lambda i, j_ref: (0, i, 0))],
            out_specs=pl.BlockSpec((tr, cols), lambda i, j_ref: (i, 0))),
        compiler_params=_cp(("parallel",)),
    )(chip_idx, p, recv)


def _adam_math(w, g, m, v):
    m = ADAM_B1 * m + (1.0 - ADAM_B1) * g
    v = ADAM_B2 * v + (1.0 - ADAM_B2) * (g * g)
    m_hat = m / (1.0 - ADAM_B1 ** ADAM_STEP)
    v_hat = v / (1.0 - ADAM_B2 ** ADAM_STEP)
    delta = -ADAM_LR * (m_hat / (jnp.sqrt(v_hat) + ADAM_EPS) + ADAM_WD * w)
    return delta, m, v


def _adam_big(w, g, m, v, name):
    rows, cols = w.shape
    tr = _pick(rows, (256, 128, 64, 32, 16, 8)) if rows % SUB == 0 else rows

    def body(w_ref, g_ref, m_ref, v_ref, d_ref, mo_ref, vo_ref):
        d_ref[...], mo_ref[...], vo_ref[...] = _adam_math(w_ref[...], g_ref[...], m_ref[...], v_ref[...])

    blk = pl.BlockSpec((tr, cols), lambda i: (i, 0))
    shp = jax.ShapeDtypeStruct((rows, cols), F32)
    return pl.pallas_call(
        body, name=name, out_shape=(shp, shp, shp), grid=(rows // tr,),
        in_specs=[blk, blk, blk, blk], out_specs=(blk, blk, blk), compiler_params=_cp(("parallel",)),
    )(w, g, m, v)


def _adam_small(w, gathered, m, v):
    rows = w.shape[0]

    def body(w_ref, a_ref, m_ref, v_ref, g_ref, d_ref, mo_ref, vo_ref):
        g = a_ref[0:rows, :]
        for k in range(1, N_DEV):
            g = g + a_ref[k * rows:(k + 1) * rows, :]
        g_ref[...] = g
        d_ref[...], mo_ref[...], vo_ref[...] = _adam_math(w_ref[...], g, m_ref[...], v_ref[...])

    shp = jax.ShapeDtypeStruct((rows, LANE), F32)
    return pl.pallas_call(body, name="adam_small", out_shape=(shp, shp, shp, shp), compiler_params=_cp(None))(
        w, gathered, m, v)


_WEIGHTS = ("rel_bias", "w_in", "b_gate", "ssm_conv_w", "ssm_conv_b", "ssm_dt_bias", "ssm_a_log", "ssm_d",
            "ssm_norm_w", "attn_sinks", "w_branch_ssm", "w_branch_attn", "w_mix_out", "ln1_g", "ln1_b", "w_up",
            "ffn_conv_w", "ffn_conv_b", "w_down", "ln2_g", "ln2_b")
_REPLICATED = tuple(n for n, _ in _SMALL[:13])


def _step(x, target, w, m, v):
    xi, yi, ci = _position()
    chip = 2 * xi + yi
    c_idx = jnp.reshape(ci, (1,)).astype(jnp.int32)
    chip_idx = jnp.reshape(chip, (1,)).astype(jnp.int32)

    halves = lambda a, s: a.reshape(2, s[0] // 2, s[1])
    stacks = _allgather_weights([halves(w[n].astype(jnp.bfloat16), s) for n, s, _ in _MATS])
    wts = {n: w[n] for n in _REPLICATED}
    for (n, s, ax), st in zip(_MATS, stacks):
        st = st.reshape((N_CHIPS,) + s)
        if n == "w_in":
            wts["wp"] = _pack_w_in_stack(st)
        else:
            wts[n] = _from_stack(st, ax)
    taps = jnp.concatenate([w[n].astype(F32).reshape(-1) for n, _ in _CONVS])
    taps = _allgather_small(_as_rows(taps, _CONV_ROWS, F32), "allgather_taps")
    taps = taps.reshape(N_CHIPS, 2, _CONV_ROWS * LANE)[:, 0]
    at = 0
    for n, s in _CONVS:
        wts[n] = _from_stack(taps[:, at:at + s[0] * s[1]].reshape((N_CHIPS,) + s), 1)
        at += s[0] * s[1]

    d_x, grads, small = _local_step(x, target, wts)

    g_stacks = []
    for n, s, ax in _MATS:
        st = _unpack_w_in_stack(grads["wp"]) if n == "w_in" else _to_stack(grads[n], s, ax)
        g_stacks.append(st.reshape(N_CHIPS, 2, s[0] // 2, s[1]))
    swapped = _swap_halves(g_stacks)
    sums = [_add_own_half(g, r, c_idx, "add_own_" + n) for (n, _, _), g, r in zip(_MATS, g_stacks, swapped)]
    landed = _scatter_chips([bf for _, bf in sums])
    reds = [_add_chips(f32, r, chip_idx, "add_chips_" + n) for (n, _, _), (f32, _), r in zip(_MATS, sums, landed)]
    joined = _join_halves(reds)
    outs = {"grad": {}, "delta": {}, "m": {}, "v": {}}
    for (n, s, _), g in zip(_MATS, joined):
        outs["grad"][n] = g.reshape(s)

    small = dict(small, g_ssm_conv_w=grads["ssm_conv_w"], g_ffn_conv_w=grads["ffn_conv_w"])
    all_small = _allgather_small(_pack_small(small), "allgather_small")
    packs = [_pack_small({n: d[n] for n in _REPLICATED}) for d in (w, m, v)]
    g_s, d_s, m_s, v_s = (_unpack_small(a) for a in _adam_small(packs[0], all_small, packs[1], packs[2]))
    for kind, part in (("grad", g_s), ("delta", d_s), ("m", m_s), ("v", v_s)):
        outs[kind].update({n: part[n] for n in _REPLICATED})
    for n, s in _CONVS:
        outs["grad"][n] = lax.dynamic_slice_in_dim(g_s["g_" + n], chip * s[1], s[1], axis=1)
    for n in [n for n, _, _ in _MATS] + [n for n, _ in _CONVS]:
        outs["delta"][n], outs["m"][n], outs["v"][n] = _adam_big(
            w[n].astype(F32), outs["grad"][n], m[n].astype(F32), v[n].astype(F32), "adam_" + n)
    loss = (0.5 / D_MODEL) * jnp.sum(g_s["loss_lanes"])
    return loss, d_x, outs


def kernel(x, rel_bias, w_in, b_gate, ssm_conv_w, ssm_conv_b, ssm_dt_bias, ssm_a_log, ssm_d, ssm_norm_w, attn_sinks, w_branch_ssm, w_branch_attn, w_mix_out, ln1_g, ln1_b, w_up, ffn_conv_w, ffn_conv_b, w_down, ln2_g, ln2_b, loss_target, m_rel_bias, m_w_in, m_b_gate, m_ssm_conv_w, m_ssm_conv_b, m_ssm_dt_bias, m_ssm_a_log, m_ssm_d, m_ssm_norm_w, m_attn_sinks, m_w_branch_ssm, m_w_branch_attn, m_w_mix_out, m_ln1_g, m_ln1_b, m_w_up, m_ffn_conv_w, m_ffn_conv_b, m_w_down, m_ln2_g, m_ln2_b, v_rel_bias, v_w_in, v_b_gate, v_ssm_conv_w, v_ssm_conv_b, v_ssm_dt_bias, v_ssm_a_log, v_ssm_d, v_ssm_norm_w, v_attn_sinks, v_w_branch_ssm, v_w_branch_attn, v_w_mix_out, v_ln1_g, v_ln1_b, v_w_up, v_ffn_conv_w, v_ffn_conv_b, v_w_down, v_ln2_g, v_ln2_b):
    given = dict(locals())
    drop = lambda a, n: a if n == "rel_bias" or a.ndim == 2 else a[0]
    w = {n: drop(given[n], n) for n in _WEIGHTS}
    m = {n: drop(given["m_" + n], n) for n in _WEIGHTS}
    v = {n: drop(given["v_" + n], n) for n in _WEIGHTS}
    loss, d_x, outs = _step(x[0], loss_target[0], w, m, v)
    like = lambda a, n: a.reshape(given[n].shape)
    res = [loss, d_x[None]]
    for kind in ("grad", "delta", "m", "v"):
        res += [like(outs[kind][n], n) for n in _WEIGHTS]
    return tuple(res)
```

```python
import math
from typing import NamedTuple

import numpy as np
import jax
import jax.numpy as jnp
from jax import lax
from jax.experimental import pallas as pl
from jax.experimental.pallas import tpu as pltpu

F32 = jnp.float32
_ACT = jnp.bfloat16
_MXU = jnp.bfloat16

D_MODEL = 1024
D_INNER = 2048
N_HEADS = 32
HEAD_P = 64
N_GROUPS = 4
N_STATE = 128
CHUNK = 128
CONV_DIM = 3072
SSM_K = 4
A_HEADS = 16
A_DH = 64
WIN = 128
REL_BUCKETS = 32
BIAS_ROWS = 64
D_FF = 2816
FFN_K = 3
ALPHA = 2.0 ** 0.25
LN_EPS = 1e-5
RMS_EPS = 1e-5
IN_COLS = 8480
NEG = -1e30

ADAM_LR = 0.001
ADAM_B1 = 0.9
ADAM_B2 = 0.999
ADAM_EPS = 1e-08
ADAM_WD = 0.01
ADAM_STEP = 10

LANE = 128
SUB = 8

P_Z, P_XS, P_G, P_Q, P_BC, P_K, P_V, P_DT = 0, 2048, 4096, 6144, 7168, 8192, 8320, 8448
P_W = 8704
P_MAIN = 8192
T_K, T_V, T_DT = P_K - P_MAIN, P_V - P_MAIN, P_DT - P_MAIN
_PIECES = ((0, 2048, P_Z), (2048, 2048, P_XS), (4096, 1024, P_BC), (5120, 32, P_DT), (5152, 1024, P_Q),
           (6176, 128, P_K), (6304, 128, P_V), (6432, 2048, P_G))

N_CHIPS = 4
N_DEV = 8


def _cp(sem=None, vmem_mb=48):
    return pltpu.CompilerParams(dimension_semantics=sem, vmem_limit_bytes=vmem_mb * 1024 * 1024)


def _pick(n, cands):
    for c in cands:
        if n % c == 0:
            return c
    raise ValueError(f"no block size for {n}")


def _rows8(p):
    k, c = p.shape
    return jnp.concatenate([p.astype(F32), jnp.zeros((SUB - k, c), F32)], axis=0)


class _SideJob(NamedTuple):
    inputs: list
    out_shape: list
    sems: list
    start: object
    finish: object


def _mm(a, b, name, *, trans_a=False, out_dtype=F32, res=None, res_scale=1.0, side=None):
    if trans_a:
        k_dim, m = a.shape
    else:
        m, k_dim = a.shape
    k2, n = b.shape
    assert k_dim == k2, (a.shape, b.shape)
    tm = _pick(m, (1408, 1024, 512, 256, 128))
    tn = _pick(n, (1408, 1024, 512, 256, 128))
    tk = _pick(k_dim, (2816, 2176, 2048, 1024, 512, 256, 128))
    nk = k_dim // tk
    grid = (m // tm, n // tn, nk)
    dn = (((0,), (0,)), ((), ())) if trans_a else (((1,), (0,)), ((), ()))
    n_in = 2 if res is None else 3
    ns_in = len(side.inputs) if side else 0
    ns_out = len(side.out_shape) if side else 0

    def body(*refs):
        a_ref, b_ref = refs[0], refs[1]
        o_ref = refs[n_in + ns_in]
        scratch = refs[n_in + ns_in + 1 + ns_out:]
        job_refs = (refs[n_in:n_in + ns_in], refs[n_in + ns_in + 1:n_in + ns_in + 1 + ns_out],
                    scratch[1:] if nk > 1 else scratch)
        i, j, k = pl.program_id(0), pl.program_id(1), pl.program_id(2)

        def finish(r):
            if res is not None:
                r = r + res_scale * refs[2][...]
            o_ref[...] = r.astype(out_dtype)

        if side:
            @pl.when(jnp.logical_and(jnp.logical_and(i == 0, j == 0), k == 0))
            def _():
                side.start(*job_refs)

        part = lax.dot_general(a_ref[...].astype(_MXU), b_ref[...].astype(_MXU), dn, preferred_element_type=F32)
        if nk == 1:
            finish(part)
        else:
            acc = scratch[0]

            @pl.when(k == 0)
            def _():
                acc[...] = part

            @pl.when(k > 0)
            def _():
                acc[...] += part

            @pl.when(k == nk - 1)
            def _():
                finish(acc[...])

        if side:
            @pl.when(jnp.logical_and(jnp.logical_and(i == grid[0] - 1, j == grid[1] - 1), k == nk - 1))
            def _():
                side.finish(*job_refs)

    if trans_a:
        a_spec = pl.BlockSpec((tk, tm), lambda i, j, k: (k, i))
    else:
        a_spec = pl.BlockSpec((tm, tk), lambda i, j, k: (i, k))
    in_specs = [a_spec, pl.BlockSpec((tk, tn), lambda i, j, k: (k, j))]
    args = [a, b]
    if res is not None:
        in_specs.append(pl.BlockSpec((tm, tn), lambda i, j, k: (i, j)))
        args.append(res)
    out_spec = pl.BlockSpec((tm, tn), lambda i, j, k: (i, j))
    out_shape = jax.ShapeDtypeStruct((m, n), out_dtype)
    scratch_shapes = [pltpu.VMEM((tm, tn), F32)] if nk > 1 else []
    if not side:
        return pl.pallas_call(
            body, name=name, out_shape=out_shape, grid=grid, in_specs=in_specs, out_specs=out_spec,
            scratch_shapes=scratch_shapes, compiler_params=_cp(("parallel", "parallel", "arbitrary")),
        )(*args)
    hbm = pl.BlockSpec(memory_space=pltpu.HBM)
    outs = pl.pallas_call(
        body, name=name, out_shape=[out_shape] + list(side.out_shape), grid=grid,
        in_specs=in_specs + [hbm] * ns_in, out_specs=[out_spec] + [hbm] * ns_out,
        scratch_shapes=scratch_shapes + list(side.sems),
        compiler_params=_cp(("arbitrary", "arbitrary", "arbitrary")),
    )(*args, *side.inputs)
    return outs[0], list(outs[1:])


def _shift_down(cur, prev8, s):
    r = pltpu.roll(cur, s, 0)
    p = pltpu.roll(prev8, s, 0)
    row8 = lax.broadcasted_iota(jnp.int32, (SUB, 1), 0)
    fixed = jnp.where(row8 < s, p, r[0:SUB])
    if cur.shape[0] == SUB:
        return fixed
    return jnp.concatenate([fixed, r[SUB:]], axis=0)


def _shift_up(cur, next8, s):
    tm = cur.shape[0]
    r = pltpu.roll(cur, tm - s, 0)
    p = pltpu.roll(next8, SUB - s, 0)
    row8 = lax.broadcasted_iota(jnp.int32, (SUB, 1), 0)
    fixed = jnp.where(row8 >= SUB - s, p, r[tm - SUB:])
    return jnp.concatenate([r[:tm - SUB], fixed], axis=0)


def _conv_pre(cur, prev8, w_ref, b_row, taps):
    acc = cur * w_ref[taps - 1:taps, :] + b_row
    for s in range(1, taps):
        acc = acc + _shift_down(cur, prev8, s) * w_ref[taps - 1 - s:taps - s, :]
    return acc


def _dot01_r(x, m01, parts=3):
    acc = None
    r = x
    for _ in range(parts):
        hi = r.astype(jnp.bfloat16)
        t = jnp.dot(hi, m01, preferred_element_type=F32)
        acc = t if acc is None else acc + t
        r = r - hi.astype(F32)
    return acc


def _dot01_l(m01, x, parts=3):
    acc = None
    r = x
    for _ in range(parts):
        hi = r.astype(jnp.bfloat16)
        t = jnp.dot(m01, hi, preferred_element_type=F32)
        acc = t if acc is None else acc + t
        r = r - hi.astype(F32)
    return acc


def _dot(a, b):
    return jnp.dot(a.astype(_MXU), b.astype(_MXU), preferred_element_type=F32)


def _dot_nt(a, b):
    return lax.dot_general(a.astype(_MXU), b.astype(_MXU), (((1,), (1,)), ((), ())), preferred_element_type=F32)


def _dot_tn(a, b):
    return lax.dot_general(a.astype(_MXU), b.astype(_MXU), (((0,), (0,)), ((), ())), preferred_element_type=F32)


def _sigmoid(x):
    return 1.0 / (1.0 + jnp.exp(-x))


def _half_masks():
    lane = lax.broadcasted_iota(jnp.int32, (1, LANE), 1)
    lo = (lane < 64).astype(F32)
    return lo, 1.0 - lo


_TC = 512


def _tm_rows(t):
    return min(256, t)


HALO = 16


def _prev_halo(tm, width, pos):
    def index(*ids):
        i, col = pos(*ids)
        return (jnp.maximum(i * (tm // HALO) - 1, 0), col)
    return pl.BlockSpec((HALO, width), index)


def _next_halo(tm, t, width, pos):
    def index(*ids):
        i, col = pos(*ids)
        return (jnp.minimum((i + 1) * (tm // HALO), t // HALO - 1), col)
    return pl.BlockSpec((HALO, width), index)


def _conv_silu_fwd(proj, colblk0, nblk, w8, b8, name):
    t = proj.shape[0]
    tm = _tm_rows(t)

    def body(c_ref, p_ref, w_ref, b_ref, o_ref, pre_ref):
        i = pl.program_id(1)
        prev8 = jnp.where(i > 0, p_ref[SUB:HALO, :].astype(F32), 0.0)
        pre = _conv_pre(c_ref[...].astype(F32), prev8, w_ref, b_ref[0:1, :], SSM_K)
        o_ref[...] = pre * _sigmoid(pre)
        pre_ref[...] = pre.astype(_ACT)

    blk = pl.BlockSpec((tm, _TC), lambda j, i: (i, j))
    return pl.pallas_call(
        body, name=name,
        out_shape=(jax.ShapeDtypeStruct((t, nblk * _TC), F32), jax.ShapeDtypeStruct((t, nblk * _TC), _ACT)),
        grid=(nblk, t // tm),
        in_specs=[pl.BlockSpec((tm, _TC), lambda j, i: (i, colblk0 + j)),
                  _prev_halo(tm, _TC, lambda j, i: (i, colblk0 + j)),
                  pl.BlockSpec((SUB, _TC), lambda j, i: (0, j)),
                  pl.BlockSpec((SUB, _TC), lambda j, i: (0, j))],
        out_specs=(blk, blk),
        compiler_params=_cp(("parallel", "parallel")),
    )(proj, proj, w8, b8)


def _silu_grad(pre):
    sg = _sigmoid(pre)
    return sg * (1.0 + pre * (1.0 - sg))


def _conv_grads(d, d_next8, cur, w_ref, acc_ref, taps, cols=slice(None)):
    du = d * w_ref[taps - 1:taps, :]
    acc_ref[taps:taps + 1, cols] += jnp.sum(d, axis=0, keepdims=True)
    acc_ref[taps - 1:taps, cols] += jnp.sum(d * cur, axis=0, keepdims=True)
    for s in range(1, taps):
        up = _shift_up(d, d_next8, s)
        du = du + up * w_ref[taps - 1 - s:taps - s, :]
        acc_ref[taps - 1 - s:taps - s, cols] += jnp.sum(up * cur, axis=0, keepdims=True)
    return du


def _conv_silu_bwd(proj, colblk0, nblk, pre, w8, d_out, name):
    t = proj.shape[0]
    tm = _tm_rows(t)
    nt = t // tm

    def body(c_ref, pre_ref, pren_ref, w_ref, d_ref, dn_ref, du_ref, acc_ref):
        i = pl.program_id(1)

        @pl.when(i == 0)
        def _():
            acc_ref[...] = jnp.zeros_like(acc_ref)

        dpre = d_ref[...].astype(F32) * _silu_grad(pre_ref[...].astype(F32))
        dpre_n = jnp.where(i < nt - 1, dn_ref[0:SUB, :].astype(F32) * _silu_grad(pren_ref[0:SUB, :].astype(F32)), 0.0)
        du_ref[...] = _conv_grads(dpre, dpre_n, c_ref[...].astype(F32), w_ref, acc_ref, SSM_K).astype(_ACT)

    c = nblk * _TC
    blk = pl.BlockSpec((tm, _TC), lambda j, i: (i, j))
    nxt = _next_halo(tm, t, _TC, lambda j, i: (i, j))
    par = pl.BlockSpec((SUB, _TC), lambda j, i: (0, j))
    return pl.pallas_call(
        body, name=name,
        out_shape=(jax.ShapeDtypeStruct((t, c), _ACT), jax.ShapeDtypeStruct((SUB, c), F32)),
        grid=(nblk, nt),
        in_specs=[pl.BlockSpec((tm, _TC), lambda j, i: (i, colblk0 + j)), blk, nxt, par, blk, nxt],
        out_specs=(blk, par),
        compiler_params=_cp(("parallel", "arbitrary")),
    )(proj, pre, pre, w8, d_out, d_out)


def _expand_consts():
    e = np.zeros((LANE, D_INNER), np.float32)
    for h in range(N_HEADS):
        e[h, h * HEAD_P:(h + 1) * HEAD_P] = 1.0
    return jnp.asarray(e, jnp.bfloat16), jnp.asarray(e.T.copy(), jnp.bfloat16)


def _ssd_common(dtr_ref, dtb_ref, alog_ref, e_ref):
    lane = lax.broadcasted_iota(jnp.int32, (1, LANE), 1)
    hm = lane < N_HEADS
    pre = dtr_ref[...] + dtb_ref[0:1, :]
    dt = jnp.where(hm, jnp.maximum(pre, 0.0) + jnp.log(1.0 + jnp.exp(-jnp.abs(pre))), 0.0)
    a_row = jnp.where(hm, -jnp.exp(alog_ref[0:1, :]), 0.0)
    adt = dt * a_row
    r = lax.broadcasted_iota(jnp.int32, (CHUNK, CHUNK), 0)
    c = lax.broadcasted_iota(jnp.int32, (CHUNK, CHUNK), 1)
    causal = r >= c
    acs = _dot01_l(causal.astype(jnp.bfloat16), adt)
    e = e_ref[...]
    acs_x = _dot01_r(acs, e, parts=2)
    dt_x = _dot01_r(dt, e, parts=2)
    return pre, dt, a_row, acs, acs_x, dt_x, causal, hm


def _decay(acs, acs_t, h, causal):
    seg = acs[:, h:h + 1] - acs_t[h:h + 1, :]
    return jnp.exp(jnp.where(causal, seg, NEG))


def _ssd_fwd(xs_c, bc_c, proj, tail, dtb8, alog8, dsk8, nw8, name):
    t = xs_c.shape[0]
    nc = t // CHUNK
    e_bf, _ = _expand_consts()
    gw = D_INNER // N_GROUPS

    def body(xs_ref, bc_ref, dtr_ref, z_ref, dtb_ref, alog_ref, dsk_ref, nw_ref, e_ref,
             y_ref, ys_ref, hp_ref, h_ref):
        c_id = pl.program_id(0)

        @pl.when(c_id == 0)
        def _():
            h_ref[...] = jnp.zeros_like(h_ref)

        _, dt, a_row, acs, acs_x, dt_x, causal, _ = _ssd_common(dtr_ref, dtb_ref, alog_ref, e_ref)
        acs_t = acs.T
        xs = xs_ref[...]
        x_dt = xs * dt_x
        last_x = acs_x[CHUNK - 1:CHUNK, :]
        w_end = jnp.exp(last_x - acs_x)
        e_in = jnp.exp(acs_x)
        d_x = _dot01_r(dsk_ref[...], e_ref[...])[0:1, :]
        hprev = h_ref[...]
        hp_ref[...] = hprev
        lo, hi = _half_masks()
        for g in range(N_GROUPS):
            bg = bc_ref[:, g * N_STATE:(g + 1) * N_STATE]
            cg = bc_ref[:, N_GROUPS * N_STATE + g * N_STATE:N_GROUPS * N_STATE + (g + 1) * N_STATE]
            sl = slice(g * gw, (g + 1) * gw)
            gm = _dot_nt(cg, bg)
            st = _dot(bg.T, x_dt[:, sl] * w_end[:, sl])
            y_off = _dot(cg, hprev[:, sl]) * e_in[:, sl]
            for j in range(gw // LANE):
                h0 = g * (gw // HEAD_P) + 2 * j
                cs = slice(g * gw + j * LANE, g * gw + (j + 1) * LANE)
                xp = x_dt[:, cs]
                m0 = gm * _decay(acs, acs_t, h0, causal)
                m1 = gm * _decay(acs, acs_t, h0 + 1, causal)
                yd = _dot(m0, xp * lo) + _dot(m1, xp * hi)
                y_ref[:, cs] = yd + y_off[:, j * LANE:(j + 1) * LANE] + xs[:, cs] * d_x[:, cs]
            h_ref[:, sl] = hprev[:, sl] * jnp.exp(last_x[:, sl]) + st
        y = y_ref[...]
        z = z_ref[...].astype(F32)
        y2 = y * (z * _sigmoid(z))
        for g in range(N_GROUPS):
            sl = slice(g * gw, (g + 1) * gw)
            yg = y2[:, sl]
            rinv = lax.rsqrt(jnp.mean(yg * yg, axis=-1, keepdims=True) + RMS_EPS)
            ys_ref[:, sl] = (yg * rinv * nw_ref[0:1, sl]).astype(_ACT)

    small = pl.BlockSpec((SUB, LANE), lambda c: (0, 0))
    return pl.pallas_call(
        body, name=name,
        out_shape=(jax.ShapeDtypeStruct((t, D_INNER), F32), jax.ShapeDtypeStruct((t, D_INNER), _ACT),
                   jax.ShapeDtypeStruct((t, D_INNER), F32)),
        grid=(nc,),
        in_specs=[pl.BlockSpec((CHUNK, D_INNER), lambda c: (c, 0)),
                  pl.BlockSpec((CHUNK, 1024), lambda c: (c, 0)),
                  pl.BlockSpec((CHUNK, LANE), lambda c: (c, T_DT // LANE)),
                  pl.BlockSpec((CHUNK, D_INNER), lambda c: (c, P_Z // D_INNER)),
                  small, small, small,
                  pl.BlockSpec((SUB, D_INNER), lambda c: (0, 0)),
                  pl.BlockSpec((LANE, D_INNER), lambda c: (0, 0))],
        out_specs=(pl.BlockSpec((CHUNK, D_INNER), lambda c: (c, 0)),
                   pl.BlockSpec((CHUNK, D_INNER), lambda c: (c, 0)),
                   pl.BlockSpec((N_STATE, D_INNER), lambda c: (c, 0))),
        scratch_shapes=[pltpu.VMEM((N_STATE, D_INNER), F32)],
        compiler_params=_cp(("arbitrary",)),
    )(xs_c, bc_c, tail, proj, dtb8, alog8, dsk8, nw8, e_bf)


def _ssd_bwd(d_ys, y, xs_c, bc_c, proj, tail, hprev_all, dtb8, alog8, dsk8, nw8, name):
    t = xs_c.shape[0]
    nc = t // CHUNK
    e_bf, et_bf = _expand_consts()
    gw = D_INNER // N_GROUPS

    def body(dys_ref, y_ref, xs_ref, bc_ref, dtr_ref, z_ref, hp_ref, dtb_ref, alog_ref, dsk_ref, nw_ref,
             e_ref, et_ref, dxs_ref, dbc_ref, dz_ref, ddt_ref, acc_ref, dnw_ref, dh_ref, dx_ref):
        step = pl.program_id(0)

        @pl.when(step == 0)
        def _():
            dh_ref[...] = jnp.zeros_like(dh_ref)
            acc_ref[...] = jnp.zeros_like(acc_ref)
            dnw_ref[...] = jnp.zeros_like(dnw_ref)

        pre, dt, a_row, acs, acs_x, dt_x, causal, hm = _ssd_common(dtr_ref, dtb_ref, alog_ref, e_ref)
        acs_t = acs.T
        et = et_ref[...]
        xs = xs_ref[...]
        x_dt = xs * dt_x
        last_x = acs_x[CHUNK - 1:CHUNK, :]
        w_end = jnp.exp(last_x - acs_x)
        e_in = jnp.exp(acs_x)
        e_last = jnp.exp(last_x)
        d_x = _dot01_r(dsk_ref[...], e_ref[...])[0:1, :]

        y = y_ref[...]
        z = z_ref[...].astype(F32)
        sz = _sigmoid(z)
        gz = z * sz
        y2 = y * gz
        dys = dys_ref[...].astype(F32)
        for g in range(N_GROUPS):
            sl = slice(g * gw, (g + 1) * gw)
            yg = y2[:, sl]
            rinv = lax.rsqrt(jnp.mean(yg * yg, axis=-1, keepdims=True) + RMS_EPS)
            nrm = yg * rinv
            dn = dys[:, sl] * nw_ref[0:1, sl]
            dnw_ref[0:1, sl] += jnp.sum(dys[:, sl] * nrm, axis=0, keepdims=True)
            dx_ref[:, sl] = rinv * (dn - nrm * jnp.mean(dn * nrm, axis=-1, keepdims=True))
        dy2 = dx_ref[...]
        dy = dy2 * gz
        dz_ref[...] = (dy2 * y * (sz * (1.0 + z * (1.0 - sz)))).astype(_ACT)

        dh_next = dh_ref[...]
        hprev = hp_ref[...]
        lo, hi = _half_masks()
        r = lax.broadcasted_iota(jnp.int32, (CHUNK, CHUNK), 0)
        c = lax.broadcasted_iota(jnp.int32, (CHUNK, CHUNK), 1)
        from_here = (c >= r).astype(jnp.bfloat16)
        before = c < r
        lane = lax.broadcasted_iota(jnp.int32, (1, LANE), 1)
        da_intra = jnp.zeros((CHUNK, LANE), F32)
        v_seg = jnp.zeros((CHUNK, LANE), F32)
        z_seg = jnp.zeros((CHUNK, LANE), F32)
        tail_parts = []
        for g in range(N_GROUPS):
            bg = bc_ref[:, g * N_STATE:(g + 1) * N_STATE]
            cg = bc_ref[:, N_GROUPS * N_STATE + g * N_STATE:N_GROUPS * N_STATE + (g + 1) * N_STATE]
            sl = slice(g * gw, (g + 1) * gw)
            et_g = et_ref[g * gw:(g + 1) * gw, :]
            gm = _dot_nt(cg, bg)
            dzg = e_in[:, sl] * dy[:, sl]
            dcg = _dot_nt(dzg, hprev[:, sl])
            dh_c = _dot(cg.T, dzg)
            q = _dot(bg, dh_next[:, sl])
            dbg = _dot_nt(x_dt[:, sl] * w_end[:, sl], dh_next[:, sl])
            y_off = _dot(cg, hprev[:, sl]) * e_in[:, sl]
            v_seg = v_seg + _dot01_r(dy[:, sl] * y_off, et_g, parts=2)
            z_seg = z_seg + _dot01_r(w_end[:, sl] * q * x_dt[:, sl], et_g, parts=2)
            dgm = jnp.zeros((CHUNK, CHUNK), F32)
            for j in range(gw // LANE):
                h0 = g * (gw // HEAD_P) + 2 * j
                cs = slice(g * gw + j * LANE, g * gw + (j + 1) * LANE)
                xp = x_dt[:, cs]
                dyp = dy[:, cs]
                dxd = jnp.zeros((CHUNK, LANE), F32)
                for half, msk in ((0, lo), (1, hi)):
                    lam = _decay(acs, acs_t, h0 + half, causal)
                    mm = gm * lam
                    dym = dyp * msk
                    dmm = _dot_nt(dym, xp)
                    dxd = dxd + _dot_tn(mm, dym)
                    dgm = dgm + dmm * lam
                    below = _dot(from_here, dmm * mm)
                    col = jnp.sum(jnp.where(before, below, 0.0), axis=-1, keepdims=True)
                    da_intra = da_intra + jnp.where(lane == h0 + half, col, 0.0)
                dx_ref[:, cs] = dxd + w_end[:, cs] * q[:, j * LANE:(j + 1) * LANE]
            dbc_ref[:, N_GROUPS * N_STATE + g * N_STATE:N_GROUPS * N_STATE + (g + 1) * N_STATE] = dcg + _dot(dgm, bg)
            dbc_ref[:, g * N_STATE:(g + 1) * N_STATE] = dbg + _dot_tn(dgm, cg)
            dh_ref[:, sl] = e_last[:, sl] * dh_next[:, sl] + dh_c
            tail_parts.append(e_last[:, sl] * jnp.sum(dh_next[:, sl] * hprev[:, sl], axis=0, keepdims=True))
        dxt = dx_ref[...]

        u_seg = _dot01_r(xs * dxt, et, parts=2)
        q_full = jnp.concatenate(tail_parts, axis=1)
        t_row = _dot01_r(jnp.broadcast_to(q_full, (SUB, D_INNER)), et)[0:1, :]
        d_alpha = (da_intra + _dot01_l(from_here, v_seg) + _dot01_l(before.astype(jnp.bfloat16), z_seg) + t_row)
        d_dt = a_row * d_alpha + u_seg
        sgp = _sigmoid(pre)
        d_raw = jnp.where(hm, d_dt * sgp, 0.0)
        ddt_ref[...] = d_raw.astype(_ACT)
        acc_ref[0:1, :] += jnp.sum(d_raw, axis=0, keepdims=True)
        acc_ref[1:2, :] += jnp.sum(d_alpha * dt, axis=0, keepdims=True) * a_row
        dd_row = jnp.sum(dy * xs, axis=0, keepdims=True)
        acc_ref[2:3, :] += _dot01_r(jnp.broadcast_to(dd_row, (SUB, D_INNER)), et)[0:1, :]
        dxs_ref[...] = dy * d_x + dxt * dt_x

    rev = lambda c: (nc - 1 - c, 0)
    small = pl.BlockSpec((SUB, LANE), lambda c: (0, 0))
    return pl.pallas_call(
        body, name=name,
        out_shape=(jax.ShapeDtypeStruct((t, D_INNER), F32), jax.ShapeDtypeStruct((t, 1024), F32),
                   jax.ShapeDtypeStruct((t, D_INNER), _ACT), jax.ShapeDtypeStruct((t, LANE), _ACT),
                   jax.ShapeDtypeStruct((SUB, LANE), F32), jax.ShapeDtypeStruct((SUB, D_INNER), F32)),
        grid=(nc,),
        in_specs=[pl.BlockSpec((CHUNK, D_INNER), rev),
                  pl.BlockSpec((CHUNK, D_INNER), rev),
                  pl.BlockSpec((CHUNK, D_INNER), rev),
                  pl.BlockSpec((CHUNK, 1024), rev),
                  pl.BlockSpec((CHUNK, LANE), lambda c: (nc - 1 - c, T_DT // LANE)),
                  pl.BlockSpec((CHUNK, D_INNER), lambda c: (nc - 1 - c, P_Z // D_INNER)),
                  pl.BlockSpec((N_STATE, D_INNER), rev),
                  small, small, small,
                  pl.BlockSpec((SUB, D_INNER), lambda c: (0, 0)),
                  pl.BlockSpec((LANE, D_INNER), lambda c: (0, 0)),
                  pl.BlockSpec((D_INNER, LANE), lambda c: (0, 0))],
        out_specs=(pl.BlockSpec((CHUNK, D_INNER), rev),
                   pl.BlockSpec((CHUNK, 1024), rev),
                   pl.BlockSpec((CHUNK, D_INNER), rev),
                   pl.BlockSpec((CHUNK, LANE), rev),
                   small,
                   pl.BlockSpec((SUB, D_INNER), lambda c: (0, 0))),
        scratch_shapes=[pltpu.VMEM((N_STATE, D_INNER), F32), pltpu.VMEM((CHUNK, D_INNER), F32)],
        compiler_params=_cp(("arbitrary",), vmem_mb=56),
    )(d_ys, y, xs_c, bc_c, tail, proj, hprev_all, dtb8, alog8, dsk8, nw8, e_bf, et_bf)


def _rel_tables():
    qi = np.arange(WIN)[:, None] + WIN
    kj = np.arange(2 * WIN)[None, :]
    rel = qi - kj
    n = np.maximum(rel, 0)
    max_exact = REL_BUCKETS // 2
    nf = np.maximum(n, 1).astype(np.float32)
    large = max_exact + (np.log(nf / np.float32(max_exact)) / np.float32(math.log(WIN / max_exact))
                         * np.float32(REL_BUCKETS - max_exact)).astype(np.int32)
    large = np.minimum(large, REL_BUCKETS - 1)
    bucket = np.where(n < max_exact, n, large)
    valid = (rel >= 0) & (rel < WIN)
    sink_col = np.broadcast_to(kj == 0, rel.shape)
    onehot = np.zeros((BIAS_ROWS, WIN * 2 * WIN), np.float32)
    flat_b = np.where(sink_col, REL_BUCKETS, bucket).reshape(-1)
    flat_v = (valid | sink_col).reshape(-1)
    first_v = ((valid & (kj >= WIN)) | sink_col).reshape(-1)
    idx = np.arange(WIN * 2 * WIN)
    onehot[flat_b[flat_v], idx[flat_v]] = 1.0
    return onehot, np.stack([first_v, flat_v]).astype(np.float32)


def _bias_expand(table_t, name):
    onehot, valid = _rel_tables()

    def body(rb_ref, oh_ref, v_ref, o_ref):
        full = _dot01_r(rb_ref[...], oh_ref[...])
        o_ref[0] = jnp.where(v_ref[0:1, :] > 0.5, full, NEG)
        o_ref[1] = jnp.where(v_ref[1:2, :] > 0.5, full, NEG)

    return pl.pallas_call(
        body, name=name, out_shape=jax.ShapeDtypeStruct((2, A_HEADS, WIN * 2 * WIN), F32),
        compiler_params=_cp(None),
    )(table_t, jnp.asarray(onehot, jnp.bfloat16), jnp.asarray(valid, F32))


def _bias_reduce(dbias, name):
    onehot, _ = _rel_tables()

    def body(d_ref, oh_ref, o_ref):
        acc = None
        r = d_ref[...]
        for _ in range(3):
            hi = r.astype(jnp.bfloat16)
            tt = lax.dot_general(hi, oh_ref[...], (((1,), (1,)), ((), ())), preferred_element_type=F32)
            acc = tt if acc is None else acc + tt
            r = r - hi.astype(F32)
        o_ref[...] = acc

    return pl.pallas_call(
        body, name=name, out_shape=jax.ShapeDtypeStruct((A_HEADS, BIAS_ROWS), F32),
        compiler_params=_cp(None),
    )(dbias, jnp.asarray(onehot, jnp.bfloat16))


def _attn_bands(kc_ref, kp_ref, vc_ref, vp_ref, has_prev):
    lo, hi = _half_masks()
    row = lax.broadcasted_iota(jnp.int32, (2 * WIN, 1), 0)
    keep = (row > 0).astype(F32)
    kb = jnp.concatenate([jnp.where(has_prev, kp_ref[...], 0.0), kc_ref[...]], axis=0) * (keep * (A_DH ** -0.5))
    vb = jnp.concatenate([jnp.where(has_prev, vp_ref[...], 0.0), vc_ref[...]], axis=0) * keep
    kr = pltpu.roll(kb, 64, 1)
    vr = pltpu.roll(vb, 64, 1)
    kk = ((kb * lo, kr * hi), (kr * lo, kb * hi))
    vv = ((vb * lo, vr * hi), (vr * lo, vb * hi))
    return kk, vv, (hi, lo)


def _attn_logits(q_ref, kk, lg_ref):
    for h in range(A_HEADS):
        j, half, kv = h // 2, h % 2, h // (A_HEADS // 2)
        lg_ref[h] = _dot_nt(q_ref[:, j * LANE:(j + 1) * LANE], kk[kv][half])


def _attn_fwd(proj, tail, bias, name):
    t = proj.shape[0]
    nb = t // WIN

    def body(q_ref, kc_ref, kp_ref, vc_ref, vp_ref, b_ref, o_ref, lg_ref, p_ref):
        n = pl.program_id(0)
        kk, vv, ones = _attn_bands(kc_ref, kp_ref, vc_ref, vp_ref, n > 0)
        _attn_logits(q_ref, kk, lg_ref)
        for h in range(A_HEADS):
            logits = lg_ref[h] + b_ref[h]
            p_ref[h] = jnp.exp(logits - jnp.max(logits, axis=-1, keepdims=True)).astype(_MXU)
        lane = lax.broadcasted_iota(jnp.int32, (1, LANE), 1)
        for j in range(A_HEADS // 2):
            kv = (2 * j) // (A_HEADS // 2)
            outs = []
            for half in range(2):
                o = jnp.dot(p_ref[2 * j + half], (vv[kv][half] + ones[half]).astype(_MXU), preferred_element_type=F32)
                outs.append(o / pltpu.roll(o, 64, 1))
            o_ref[:, j * LANE:(j + 1) * LANE] = jnp.where(lane < 64, outs[0], outs[1]).astype(_ACT)

    kvspec = lambda col, prev: pl.BlockSpec(
        (WIN, LANE), (lambda n: (jnp.maximum(n - 1, 0), col)) if prev else (lambda n: (n, col)))
    return pl.pallas_call(
        body, name=name, out_shape=jax.ShapeDtypeStruct((t, D_MODEL), _ACT),
        grid=(nb,),
        in_specs=[pl.BlockSpec((WIN, 1024), lambda n: (n, P_Q // 1024)),
                  kvspec(T_K // LANE, False), kvspec(T_K // LANE, True),
                  kvspec(T_V // LANE, False), kvspec(T_V // LANE, True),
                  pl.BlockSpec((None, A_HEADS, WIN, 2 * WIN), lambda n: (jnp.minimum(n, 1), 0, 0, 0))],
        out_specs=pl.BlockSpec((WIN, 1024), lambda n: (n, 0)),
        scratch_shapes=[pltpu.VMEM((A_HEADS, WIN, 2 * WIN), F32), pltpu.VMEM((A_HEADS, WIN, 2 * WIN), _MXU)],
        compiler_params=_cp(("parallel",)),
    )(proj, tail, tail, tail, tail, bias)


def _attn_bwd(proj, tail, bias, y_attn, d_out, name):
    t = proj.shape[0]
    nb = t // WIN

    def body(q_ref, kc_ref, kp_ref, vc_ref, vp_ref, b_ref, y_ref, do_ref,
             dq_ref, dk_ref, dv_ref, db_ref, ck_ref, cv_ref, lg_ref, dl_ref, p_ref):
        n = pl.program_id(0)

        @pl.when(n == 0)
        def _():
            db_ref[...] = jnp.zeros_like(db_ref)
            ck_ref[...] = jnp.zeros_like(ck_ref)
            cv_ref[...] = jnp.zeros_like(cv_ref)

        @pl.when(n < nb)
        def _():
            kk, vv, _ = _attn_bands(kc_ref, kp_ref, vc_ref, vp_ref, n > 0)
            lo, hi = _half_masks()
            ones_k = jnp.ones((2 * WIN, LANE), jnp.bfloat16)
            ones_d = jnp.ones((LANE, LANE), jnp.bfloat16)
            _attn_logits(q_ref, kk, lg_ref)
            for h in range(A_HEADS):
                j, half, kv = h // 2, h % 2, h // (A_HEADS // 2)
                msk = hi if half else lo
                logits = lg_ref[h] + b_ref[h]
                p = jnp.exp(logits - jnp.max(logits, axis=-1, keepdims=True))
                den = jnp.dot(p.astype(_MXU), ones_k.astype(_MXU), preferred_element_type=F32)
                dop = do_ref[:, j * LANE:(j + 1) * LANE].astype(F32)
                delta = _dot01_r(dop * y_ref[:, j * LANE:(j + 1) * LANE].astype(F32) * msk, ones_d, parts=2)
                inv = 1.0 / den
                probs = p * jnp.concatenate([inv, inv], axis=1)
                dprobs = _dot_nt(dop, vv[kv][half])
                dlog = probs * (dprobs - jnp.concatenate([delta, delta], axis=1))
                db_ref[h] += dlog
                dl_ref[h] = dlog.astype(_MXU)
                p_ref[h] = probs.astype(_MXU)
            dk_t = [[None, None], [None, None]]
            dv_t = [[None, None], [None, None]]
            for j in range(A_HEADS // 2):
                kv = (2 * j) // (A_HEADS // 2)
                qs = q_ref[:, j * LANE:(j + 1) * LANE].astype(F32) * (A_DH ** -0.5)
                dop = do_ref[:, j * LANE:(j + 1) * LANE].astype(F32)
                dq = None
                for half, msk in ((0, lo), (1, hi)):
                    h = 2 * j + half
                    dqh = jnp.dot(dl_ref[h], kk[kv][half].astype(_MXU), preferred_element_type=F32)
                    dq = dqh if dq is None else dq + dqh
                    dkh = lax.dot_general((qs * msk).astype(_MXU), dl_ref[h], (((0,), (0,)), ((), ())),
                                          preferred_element_type=F32)
                    dvh = lax.dot_general((dop * msk).astype(_MXU), p_ref[h], (((0,), (0,)), ((), ())),
                                          preferred_element_type=F32)
                    dk_t[kv][half] = dkh if dk_t[kv][half] is None else dk_t[kv][half] + dkh
                    dv_t[kv][half] = dvh if dv_t[kv][half] is None else dv_t[kv][half] + dvh
                dq_ref[:, j * LANE:(j + 1) * LANE] = dq.astype(_ACT)
            row = lax.broadcasted_iota(jnp.int32, (2 * WIN, 1), 0)

            def band(acc):
                a = (acc[0][0] + pltpu.roll(acc[0][1], 64, 0)) + (pltpu.roll(acc[1][0], 64, 0) + acc[1][1])
                return jnp.where(row > 0, a.T, 0.0)

            dkb = band(dk_t)
            dvb = band(dv_t)
            dk_ref[...] = (ck_ref[...] + dkb[0:WIN]).astype(_ACT)
            dv_ref[...] = (cv_ref[...] + dvb[0:WIN]).astype(_ACT)
            ck_ref[...] = dkb[WIN:]
            cv_ref[...] = dvb[WIN:]

        @pl.when(n == nb)
        def _():
            dk_ref[...] = ck_ref[...].astype(_ACT)
            dv_ref[...] = cv_ref[...].astype(_ACT)

    cur = lambda n: jnp.minimum(n, nb - 1)
    prv = lambda n: jnp.maximum(jnp.minimum(n, nb - 1) - 1, 0)
    kvspec = lambda col, prev: pl.BlockSpec(
        (WIN, LANE), (lambda n: (prv(n), col)) if prev else (lambda n: (cur(n), col)))
    band_shape = (A_HEADS, WIN, 2 * WIN)
    return pl.pallas_call(
        body, name=name,
        out_shape=(jax.ShapeDtypeStruct((t, D_MODEL), _ACT), jax.ShapeDtypeStruct((t, LANE), _ACT),
                   jax.ShapeDtypeStruct((t, LANE), _ACT), jax.ShapeDtypeStruct(band_shape, F32)),
        grid=(nb + 1,),
        in_specs=[pl.BlockSpec((WIN, 1024), lambda n: (cur(n), P_Q // 1024)),
                  kvspec(T_K // LANE, False), kvspec(T_K // LANE, True),
                  kvspec(T_V // LANE, False), kvspec(T_V // LANE, True),
                  pl.BlockSpec((None,) + band_shape, lambda n: (jnp.minimum(n, 1), 0, 0, 0)),
                  pl.BlockSpec((WIN, 1024), lambda n: (cur(n), 0)),
                  pl.BlockSpec((WIN, 1024), lambda n: (cur(n), 0))],
        out_specs=(pl.BlockSpec((WIN, 1024), lambda n: (cur(n), 0)),
                   pl.BlockSpec((WIN, LANE), lambda n: (jnp.maximum(n - 1, 0), 0)),
                   pl.BlockSpec((WIN, LANE), lambda n: (jnp.maximum(n - 1, 0), 0)),
                   pl.BlockSpec(band_shape, lambda n: (0, 0, 0))),
        scratch_shapes=[pltpu.VMEM((WIN, LANE), F32), pltpu.VMEM((WIN, LANE), F32),
                        pltpu.VMEM(band_shape, F32), pltpu.VMEM(band_shape, _MXU), pltpu.VMEM(band_shape, _MXU)],
        compiler_params=_cp(("arbitrary",)),
    )(proj, tail, tail, tail, tail, bias, y_attn, d_out)


def _merge_fwd(bs, ba, proj, bg8, name):
    t = bs.shape[0]
    tm = _tm_rows(t)

    def body(bs_ref, ba_ref, gs_ref, ga_ref, bgs_ref, bga_ref, o_ref):
        g_s = _sigmoid(gs_ref[...] + bgs_ref[0:1, :])
        g_a = _sigmoid(ga_ref[...] + bga_ref[0:1, :])
        o_ref[...] = (g_s * bs_ref[...] + g_a * ba_ref[...]).astype(_ACT)

    row = lambda col: pl.BlockSpec((tm, 1024), lambda i: (i, col))
    return pl.pallas_call(
        body, name=name, out_shape=jax.ShapeDtypeStruct((t, D_MODEL), _ACT), grid=(t // tm,),
        in_specs=[row(0), row(0), row(P_G // 1024), row(P_G // 1024 + 1),
                  pl.BlockSpec((SUB, 1024), lambda i: (0, 0)), pl.BlockSpec((SUB, 1024), lambda i: (0, 1))],
        out_specs=row(0), compiler_params=_cp(("parallel",)),
    )(bs, ba, proj, proj, bg8, bg8)


def _merge_bwd(d_merged, bs, ba, proj, bg8, name):
    t = bs.shape[0]
    tm = _tm_rows(t)

    def body(dm_ref, bs_ref, ba_ref, gs_ref, ga_ref, bgs_ref, bga_ref, dbs_ref, dba_ref, dg_ref, acc_ref):
        @pl.when(pl.program_id(0) == 0)
        def _():
            acc_ref[...] = jnp.zeros_like(acc_ref)

        dm = dm_ref[...].astype(F32)
        g_s = _sigmoid(gs_ref[...] + bgs_ref[0:1, :])
        g_a = _sigmoid(ga_ref[...] + bga_ref[0:1, :])
        dbs_ref[...] = (dm * g_s).astype(_ACT)
        dba_ref[...] = (dm * g_a).astype(_ACT)
        dgs = dm * bs_ref[...].astype(F32) * g_s * (1.0 - g_s)
        dga = dm * ba_ref[...].astype(F32) * g_a * (1.0 - g_a)
        dg_ref[:, 0:1024] = dgs.astype(_ACT)
        dg_ref[:, 1024:2048] = dga.astype(_ACT)
        acc_ref[0:1, 0:1024] += jnp.sum(dgs, axis=0, keepdims=True)
        acc_ref[0:1, 1024:2048] += jnp.sum(dga, axis=0, keepdims=True)

    row = lambda col: pl.BlockSpec((tm, 1024), lambda i: (i, col))
    return pl.pallas_call(
        body, name=name,
        out_shape=(jax.ShapeDtypeStruct((t, D_MODEL), _ACT), jax.ShapeDtypeStruct((t, D_MODEL), _ACT),
                   jax.ShapeDtypeStruct((t, 2048), _ACT), jax.ShapeDtypeStruct((SUB, 2048), F32)),
        grid=(t // tm,),
        in_specs=[row(0), row(0), row(0), row(P_G // 1024), row(P_G // 1024 + 1),
                  pl.BlockSpec((SUB, 1024), lambda i: (0, 0)), pl.BlockSpec((SUB, 1024), lambda i: (0, 1))],
        out_specs=(row(0), row(0), pl.BlockSpec((tm, 2048), lambda i: (i, 0)),
                   pl.BlockSpec((SUB, 2048), lambda i: (0, 0))),
        compiler_params=_cp(("arbitrary",)),
    )(d_merged, bs, ba, proj, proj, bg8, bg8)


def _ln_stats(r):
    mu = jnp.mean(r, axis=-1, keepdims=True)
    xc = r - mu
    var = jnp.mean(xc * xc, axis=-1, keepdims=True)
    rstd = lax.rsqrt(var + LN_EPS)
    return xc * rstd, rstd


def _ln_bwd(dxhat, xhat, rstd):
    return rstd * (dxhat - jnp.mean(dxhat, axis=-1, keepdims=True)
                   - xhat * jnp.mean(dxhat * xhat, axis=-1, keepdims=True))


def _ln1_fwd(x, mix, g8, b8, name):
    t = x.shape[0]
    tm = _tm_rows(t)

    def body(x_ref, m_ref, g_ref, b_ref, xh_ref, h_ref, rs_ref):
        xhat, rstd = _ln_stats(ALPHA * x_ref[...] + m_ref[...])
        xh_ref[...] = xhat
        h_ref[...] = (xhat * g_ref[0:1, :] + b_ref[0:1, :]).astype(_ACT)
        rs_ref[...] = rstd

    row = pl.BlockSpec((tm, D_MODEL), lambda i: (i, 0))
    par = pl.BlockSpec((SUB, D_MODEL), lambda i: (0, 0))
    return pl.pallas_call(
        body, name=name,
        out_shape=(jax.ShapeDtypeStruct((t, D_MODEL), F32), jax.ShapeDtypeStruct((t, D_MODEL), _ACT),
                   jax.ShapeDtypeStruct((t, 1), F32)),
        grid=(t // tm,), in_specs=[row, row, par, par],
        out_specs=(row, row, pl.BlockSpec((tm, 1), lambda i: (i, 0))),
        compiler_params=_cp(("parallel",)),
    )(x, mix, g8, b8)


def _ln2_loss(xhat1, ffn, target, g1_8, b1_8, g2_8, b2_8, name):
    t = xhat1.shape[0]
    tm = _tm_rows(t)

    def body(xh_ref, f_ref, t_ref, g1_ref, b1_ref, g2_ref, b2_ref, d_ref, db_ref, acc_ref):
        @pl.when(pl.program_id(0) == 0)
        def _():
            acc_ref[...] = jnp.zeros_like(acc_ref)

        h1 = xh_ref[...] * g1_ref[0:1, :] + b1_ref[0:1, :]
        xhat, rstd = _ln_stats(ALPHA * h1 + f_ref[...])
        diff = xhat * g2_ref[0:1, :] + b2_ref[0:1, :] - t_ref[...]
        dy = diff * (1.0 / D_MODEL)
        acc_ref[0:1, :] += jnp.sum(dy * xhat, axis=0, keepdims=True)
        acc_ref[1:2, :] += jnp.sum(dy, axis=0, keepdims=True)
        acc_ref[2:3, :] += jnp.sum(diff * diff, axis=0, keepdims=True)
        d = _ln_bwd(dy * g2_ref[0:1, :], xhat, rstd)
        d_ref[...] = d
        db_ref[...] = d.astype(_ACT)

    row = pl.BlockSpec((tm, D_MODEL), lambda i: (i, 0))
    par = pl.BlockSpec((SUB, D_MODEL), lambda i: (0, 0))
    return pl.pallas_call(
        body, name=name,
        out_shape=(jax.ShapeDtypeStruct((t, D_MODEL), F32), jax.ShapeDtypeStruct((t, D_MODEL), _ACT),
                   jax.ShapeDtypeStruct((SUB, D_MODEL), F32)),
        grid=(t // tm,), in_specs=[row, row, row, par, par, par, par],
        out_specs=(row, row, par), compiler_params=_cp(("arbitrary",)),
    )(xhat1, ffn, target, g1_8, b1_8, g2_8, b2_8)


def _ln1_bwd(d_r2, d_h1_ffn, xhat1, rstd1, g1_8, name):
    t = xhat1.shape[0]
    tm = _tm_rows(t)

    def body(d2_ref, df_ref, xh_ref, rs_ref, g_ref, d_ref, db_ref, acc_ref):
        @pl.when(pl.program_id(0) == 0)
        def _():
            acc_ref[...] = jnp.zeros_like(acc_ref)

        dh = ALPHA * d2_ref[...] + df_ref[...]
        xhat = xh_ref[...]
        acc_ref[0:1, :] += jnp.sum(dh * xhat, axis=0, keepdims=True)
        acc_ref[1:2, :] += jnp.sum(dh, axis=0, keepdims=True)
        d = _ln_bwd(dh * g_ref[0:1, :], xhat, rs_ref[...])
        d_ref[...] = d
        db_ref[...] = d.astype(_ACT)

    row = pl.BlockSpec((tm, D_MODEL), lambda i: (i, 0))
    par = pl.BlockSpec((SUB, D_MODEL), lambda i: (0, 0))
    return pl.pallas_call(
        body, name=name,
        out_shape=(jax.ShapeDtypeStruct((t, D_MODEL), F32), jax.ShapeDtypeStruct((t, D_MODEL), _ACT),
                   jax.ShapeDtypeStruct((SUB, D_MODEL), F32)),
        grid=(t // tm,), in_specs=[row, row, row, pl.BlockSpec((tm, 1), lambda i: (i, 0)), par],
        out_specs=(row, row, par), compiler_params=_cp(("arbitrary",)),
    )(d_r2, d_h1_ffn, xhat1, rstd1, g1_8)


def _ffn_tm(t):
    return min(128, t)


def _ffn_act_fwd(u0, cw8, cb8, name):
    t = u0.shape[0]
    tm = _ffn_tm(t)

    def body(g_ref, gp_ref, v_ref, vp_ref, wg_ref, wv_ref, bg_ref, bv_ref, o_ref, u_ref):
        i = pl.program_id(0)
        gprev = jnp.where(i > 0, gp_ref[SUB:HALO, :].astype(F32), 0.0)
        vprev = jnp.where(i > 0, vp_ref[SUB:HALO, :].astype(F32), 0.0)
        gate = _conv_pre(g_ref[...].astype(F32), gprev, wg_ref, bg_ref[0:1, :], FFN_K)
        val = _conv_pre(v_ref[...].astype(F32), vprev, wv_ref, bv_ref[0:1, :], FFN_K)
        o_ref[...] = (gate * _sigmoid(gate) * val).astype(_ACT)
        u_ref[:, 0:D_FF] = gate.astype(_ACT)
        u_ref[:, D_FF:2 * D_FF] = val.astype(_ACT)

    cur = lambda col: pl.BlockSpec((tm, D_FF), lambda i: (i, col))
    prv = lambda col: _prev_halo(tm, D_FF, lambda i: (i, col))
    par = lambda col: pl.BlockSpec((SUB, D_FF), lambda i: (0, col))
    return pl.pallas_call(
        body, name=name,
        out_shape=(jax.ShapeDtypeStruct((t, D_FF), _ACT), jax.ShapeDtypeStruct((t, 2 * D_FF), _ACT)),
        grid=(t // tm,),
        in_specs=[cur(0), prv(0), cur(1), prv(1), par(0), par(1), par(0), par(1)],
        out_specs=(pl.BlockSpec((tm, D_FF), lambda i: (i, 0)), pl.BlockSpec((tm, 2 * D_FF), lambda i: (i, 0))),
        compiler_params=_cp(("parallel",)),
    )(u0, u0, u0, u0, cw8, cw8, cb8, cb8)


def _ffn_act_bwd(u0, u, cw8, d_a, name):
    t = u0.shape[0]
    tm = _ffn_tm(t)
    nt = t // tm

    def body(g0_ref, v0_ref, g_ref, gn_ref, v_ref, vn_ref, wg_ref, wv_ref, da_ref, dan_ref, du_ref, acc_ref):
        i = pl.program_id(0)

        @pl.when(i == 0)
        def _():
            acc_ref[...] = jnp.zeros_like(acc_ref)

        def grads(gate, val, da):
            return da * val * _silu_grad(gate), da * gate * _sigmoid(gate)

        dgate, dval = grads(g_ref[...].astype(F32), v_ref[...].astype(F32), da_ref[...].astype(F32))
        dgate_n, dval_n = grads(gn_ref[0:SUB, :].astype(F32), vn_ref[0:SUB, :].astype(F32),
                                dan_ref[0:SUB, :].astype(F32))
        last = i == nt - 1
        du_ref[:, 0:D_FF] = _conv_grads(dgate, jnp.where(last, 0.0, dgate_n), g0_ref[...].astype(F32), wg_ref,
                                        acc_ref, FFN_K, slice(0, D_FF)).astype(_ACT)
        du_ref[:, D_FF:2 * D_FF] = _conv_grads(dval, jnp.where(last, 0.0, dval_n), v0_ref[...].astype(F32), wv_ref,
                                               acc_ref, FFN_K, slice(D_FF, 2 * D_FF)).astype(_ACT)

    cur = lambda col: pl.BlockSpec((tm, D_FF), lambda i: (i, col))
    nxt = lambda col: _next_halo(tm, t, D_FF, lambda i: (i, col))
    par = lambda col: pl.BlockSpec((SUB, D_FF), lambda i: (0, col))
    return pl.pallas_call(
        body, name=name,
        out_shape=(jax.ShapeDtypeStruct((t, 2 * D_FF), _ACT), jax.ShapeDtypeStruct((SUB, 2 * D_FF), F32)),
        grid=(nt,),
        in_specs=[cur(0), cur(1), cur(0), nxt(0), cur(1), nxt(1), par(0), par(1), cur(0), nxt(0)],
        out_specs=(pl.BlockSpec((tm, 2 * D_FF), lambda i: (i, 0)),
                   pl.BlockSpec((SUB, 2 * D_FF), lambda i: (0, 0))),
        compiler_params=_cp(("arbitrary",)),
    )(u0, u0, u, u, u, u, cw8, cw8, d_a, d_a)


_REST = ("w_branch_ssm", "w_branch_attn", "w_mix_out", "w_up", "w_down")


def _mm_side(*args, side, **kw):
    if side is None:
        return _mm(*args, **kw), []
    return _mm(*args, side=side, **kw)


def _local_step(x, target, wts, ex):
    t = x.shape[0]
    wp = wts["wp"]
    scw = wts["ssm_conv_w"]
    scb = wts["ssm_conv_b"]
    fcw8 = _rows8(wts["ffn_conv_w"])
    fcb8 = _rows8(wts["ffn_conv_b"])
    pad_lane = lambda p: jnp.concatenate([p.astype(F32), jnp.zeros((1, LANE - p.shape[1]), F32)], axis=1)
    dtb8 = _rows8(pad_lane(wts["ssm_dt_bias"]))
    alog8 = _rows8(pad_lane(wts["ssm_a_log"]))
    dsk8 = _rows8(pad_lane(wts["ssm_d"]))
    bias_table = jnp.concatenate([wts["rel_bias"].T.astype(F32), wts["attn_sinks"].T.astype(F32),
                                  jnp.zeros((A_HEADS, BIAS_ROWS - REL_BUCKETS - 1), F32)], axis=1)
    nw8 = _rows8(wts["ssm_norm_w"])
    bg8 = _rows8(wts["b_gate"])
    g1_8, b1_8, g2_8, b2_8 = (_rows8(wts[k]) for k in ("ln1_g", "ln1_b", "ln2_g", "ln2_b"))
    xs_w8, xs_b8 = _rows8(scw[:, :D_INNER]), _rows8(scb[:, :D_INNER])
    bc_w8, bc_b8 = _rows8(scw[:, D_INNER:]), _rows8(scb[:, D_INNER:])

    x_bf = x.astype(_ACT)
    proj, stacks = _mm_side(x_bf, wp[:, :P_MAIN], "mm_in", out_dtype=_ACT, side=ex.gather_rest())
    wts = dict(wts, **ex.rest_weights(stacks))
    w_bs, w_ba, w_mix, w_up, w_dn = (wts[k] for k in _REST)
    tail = _mm(x_bf, wp[:, P_MAIN:], "mm_in_tail")
    xs_c, xs_pre = _conv_silu_fwd(proj, P_XS // _TC, D_INNER // _TC, xs_w8, xs_b8, "conv_xs_fwd")
    bc_c, bc_pre = _conv_silu_fwd(proj, P_BC // _TC, 1024 // _TC, bc_w8, bc_b8, "conv_bc_fwd")
    y_ssd, y_ssm, hprev = _ssd_fwd(xs_c, bc_c, proj, tail, dtb8, alog8, dsk8, nw8, "ssd_fwd")
    bias = _bias_expand(bias_table, "bias_expand").reshape(2, A_HEADS, WIN, 2 * WIN)
    y_attn = _attn_fwd(proj, tail, bias, "attn_fwd")
    bs = _mm(y_ssm, w_bs, "mm_bs", out_dtype=_ACT)
    ba = _mm(y_attn, w_ba, "mm_ba", out_dtype=_ACT)
    merged = _merge_fwd(bs, ba, proj, bg8, "merge_fwd")
    mix = _mm(merged, w_mix, "mm_mix", out_dtype=_ACT)
    xhat1, h1_bf, rstd1 = _ln1_fwd(x, mix, g1_8, b1_8, "ln1_fwd")
    u0 = _mm(h1_bf, w_up, "mm_up", out_dtype=_ACT)
    act, u_conv = _ffn_act_fwd(u0, fcw8, fcb8, "ffn_act_fwd")
    ffn = _mm(act, w_dn, "mm_down", out_dtype=_ACT)
    d_r2, d_r2_bf, acc_ln2 = _ln2_loss(xhat1, ffn, target, g1_8, b1_8, g2_8, b2_8, "ln2_loss")
    d_w_dn = _mm(act, d_r2_bf, "mm_dw_down", trans_a=True)
    d_act = _mm(d_r2_bf, w_dn.T, "mm_d_act", out_dtype=_ACT)
    d_u0, acc_ffn = _ffn_act_bwd(u0, u_conv, fcw8, d_act, "ffn_act_bwd")
    d_w_up = _mm(h1_bf, d_u0, "mm_dw_up", trans_a=True)
    d_h1_ffn = _mm(d_u0, w_up.T, "mm_d_h1", out_dtype=_ACT)
    d_r1, d_r1_bf, acc_ln1 = _ln1_bwd(d_r2, d_h1_ffn, xhat1, rstd1, g1_8, "ln1_bwd")
    d_w_mix = _mm(merged, d_r1_bf, "mm_dw_mix", trans_a=True)
    d_merged = _mm(d_r1_bf, w_mix.T, "mm_d_merged", out_dtype=_ACT)
    d_bs, d_ba, d_gates, acc_bg = _merge_bwd(d_merged, bs, ba, proj, bg8, "merge_bwd")
    d_w_bs = _mm(y_ssm, d_bs, "mm_dw_bs", trans_a=True)
    d_w_ba = _mm(y_attn, d_ba, "mm_dw_ba", trans_a=True)
    d_y_ssm = _mm(d_bs, w_bs.T, "mm_d_yssm", out_dtype=_ACT)
    d_y_attn = _mm(d_ba, w_ba.T, "mm_d_yattn", out_dtype=_ACT)
    d_q, d_k, d_v, d_bias = _attn_bwd(proj, tail, bias, y_attn, d_y_attn, "attn_bwd")
    d_table = _bias_reduce(d_bias.reshape(A_HEADS, WIN * 2 * WIN), "bias_reduce")
    d_xs_c, d_bc_c, d_z, d_dt, acc_ssd, acc_nw = _ssd_bwd(
        d_y_ssm, y_ssd, xs_c, bc_c, proj, tail, hprev, dtb8, alog8, dsk8, nw8, "ssd_bwd")
    d_xs, acc_xs = _conv_silu_bwd(proj, P_XS // _TC, D_INNER // _TC, xs_pre, xs_w8, d_xs_c, "conv_xs_bwd")
    d_bc, acc_bc = _conv_silu_bwd(proj, P_BC // _TC, 1024 // _TC, bc_pre, bc_w8, d_bc_c, "conv_bc_bwd")
    d_proj = jnp.concatenate([d_z, d_xs, d_gates, d_q, d_bc, d_k, d_v, d_dt,
                              jnp.zeros((t, P_W - P_DT - LANE), _ACT)], axis=1)
    grads = {"w_branch_ssm": d_w_bs, "w_branch_attn": d_w_ba, "w_mix_out": d_w_mix, "w_up": d_w_up, "w_down": d_w_dn}
    d_wp, landed_rest = _mm_side(x_bf, d_proj, "mm_dw_in", trans_a=True, side=ex.reduce_job(grads))
    d_x, landed_in = _mm_side(d_proj, wp.T, "mm_d_x", res=d_r1, res_scale=ALPHA, side=ex.reduce_job({"wp": d_wp}))
    grads.update({
        "wp": d_wp,
        "ssm_conv_w": jnp.concatenate([acc_xs[0:SSM_K], acc_bc[0:SSM_K]], axis=1),
        "ffn_conv_w": acc_ffn[0:FFN_K],
    })
    small = {
        "rel_bias": d_table[:, 0:REL_BUCKETS].T,
        "b_gate": acc_bg[0:1],
        "ssm_conv_b": jnp.concatenate([acc_xs[SSM_K:SSM_K + 1], acc_bc[SSM_K:SSM_K + 1]], axis=1),
        "ssm_dt_bias": acc_ssd[0:1, 0:N_HEADS], "ssm_a_log": acc_ssd[1:2, 0:N_HEADS], "ssm_d": acc_ssd[2:3, 0:N_HEADS],
        "ssm_norm_w": acc_nw[0:1],
        "attn_sinks": d_table[:, REL_BUCKETS:REL_BUCKETS + 1].T,
        "ln1_g": acc_ln1[0:1], "ln1_b": acc_ln1[1:2],
        "ffn_conv_b": acc_ffn[FFN_K:FFN_K + 1],
        "ln2_g": acc_ln2[0:1], "ln2_b": acc_ln2[1:2],
        "loss_lanes": acc_ln2[2:3],
    }
    return d_x, grads, small, landed_rest + landed_in


_MATS = (("w_in", (1024, 2120), 1), ("w_branch_ssm", (512, 1024), 0), ("w_branch_attn", (256, 1024), 0),
         ("w_mix_out", (256, 1024), 0), ("w_up", (1024, 1408), 1), ("w_down", (704, 1024), 0))
_CONVS = (("ssm_conv_w", (4, 768)), ("ffn_conv_w", (3, 1408)))
_CONV_ROWS = 64

_SMALL = (("rel_bias", (32, 16)), ("b_gate", (1, 2048)), ("ssm_conv_b", (1, 3072)), ("ssm_dt_bias", (1, 32)),
          ("ssm_a_log", (1, 32)), ("ssm_d", (1, 32)), ("ssm_norm_w", (1, 2048)), ("attn_sinks", (1, 16)),
          ("ln1_g", (1, 1024)), ("ln1_b", (1, 1024)), ("ffn_conv_b", (1, 5632)), ("ln2_g", (1, 1024)),
          ("ln2_b", (1, 1024)), ("g_ssm_conv_w", (4, 3072)), ("g_ffn_conv_w", (3, 5632)), ("loss_lanes", (1, 1024)))


def _small_rows(shape):
    rows = -(-(shape[0] * shape[1]) // LANE)
    return -(-rows // SUB) * SUB


def _as_rows(a, rows, dtype):
    flat = a.reshape(-1).astype(dtype)
    flat = jnp.concatenate([flat, jnp.zeros((rows * LANE - flat.shape[0],), dtype)])
    return flat.reshape(rows, LANE)


def _pack_small(parts):
    blocks = [_as_rows(parts[n], _small_rows(s), F32) if n in parts else jnp.zeros((_small_rows(s), LANE), F32)
              for n, s in _SMALL]
    return jnp.concatenate(blocks, axis=0)


def _unpack_small(packed):
    out, at = {}, 0
    for n, s in _SMALL:
        rows = _small_rows(s)
        out[n] = packed[at:at + rows].reshape(-1)[:s[0] * s[1]].reshape(s)
        at += rows
    return out


def _to_stack(full, shape, axis):
    if axis == 0:
        return full.reshape((N_CHIPS,) + shape)
    return jnp.transpose(full.reshape(shape[0], N_CHIPS, shape[1]), (1, 0, 2))


def _from_stack(stack, axis):
    n, r, c = stack.shape
    if axis == 0:
        return stack.reshape(n * r, c)
    return jnp.transpose(stack, (1, 0, 2)).reshape(r, n * c)


_IN_SHARD = IN_COLS // N_CHIPS


def _cols_of_stack(stack, o, w):
    parts = []
    while w > 0:
        j, a = divmod(o, _IN_SHARD)
        n = min(w, _IN_SHARD - a)
        parts.append(stack[j][:, a:a + n])
        o, w = o + n, w - n
    return parts


def _pack_w_in_stack(stack):
    cols, at = [], 0
    for o, w, pk in sorted(_PIECES, key=lambda p: p[2]):
        if pk > at:
            cols.append(jnp.zeros((stack.shape[1], pk - at), stack.dtype))
        cols += _cols_of_stack(stack, o, w)
        at = pk + w
    cols.append(jnp.zeros((stack.shape[1], P_W - at), stack.dtype))
    return jnp.concatenate(cols, axis=1)


def _unpack_w_in_stack(wp):
    slabs = []
    for j in range(N_CHIPS):
        lo, hi = j * _IN_SHARD, (j + 1) * _IN_SHARD
        cols = []
        for o, w, pk in sorted(_PIECES):
            a, b = max(o, lo), min(o + w, hi)
            if a < b:
                cols.append(wp[:, pk + a - o:pk + b - o])
        slabs.append(jnp.concatenate(cols, axis=1))
    return jnp.stack(slabs)


_MESH = pl.DeviceIdType.MESH
_HBM = pl.BlockSpec(memory_space=pltpu.HBM)


def _position():
    return lax.axis_index("x"), lax.axis_index("y"), lax.axis_index("c")


def _other_chips(x, y):
    return ((1 - x, y), (x, 1 - y), (1 - x, 1 - y))


def _remote(src, dst, send_sem, recv_sem, to):
    return pltpu.make_async_remote_copy(src_ref=src, dst_ref=dst, send_sem=send_sem, recv_sem=recv_sem,
                                        device_id=to, device_id_type=_MESH)


def _run_job(job, name):
    n_in, n_out = len(job.inputs), len(job.out_shape)

    def body(*refs):
        parts = (refs[:n_in], refs[n_in:n_in + n_out], refs[n_in + n_out:])
        job.start(*parts)
        job.finish(*parts)

    return pl.pallas_call(body, name=name, out_shape=list(job.out_shape), in_specs=[_HBM] * n_in,
                          out_specs=[_HBM] * n_out, scratch_shapes=list(job.sems))(*job.inputs)


def _gather_job(shards):
    n = len(shards)

    def plan(s_refs, o_refs, sems):
        send_sems, recv_sems, local_sems = sems
        x, y, c = _position()
        me = 2 * x + y
        sib = (x, y, 1 - c)
        chips = _other_chips(x, y)

        def copy(m, k, chip_idx, half, to, src=None):
            dst = o_refs[m].at[chip_idx, half]
            return _remote(dst if src is None else src, dst, send_sems.at[6 * m + k], recv_sems.at[6 * m + k], to)

        local = [pltpu.make_async_copy(s_refs[m], o_refs[m].at[me], local_sems.at[m]) for m in range(n)]
        first = [copy(m, i, me, c, (cx, cy, c), src=s_refs[m].at[c])
                 for i, (cx, cy) in enumerate(chips) for m in range(n)]
        return c, sib, chips, copy, local, first

    def start(s_refs, o_refs, sems):
        _, _, _, _, local, first = plan(s_refs, o_refs, sems)
        for cp in local + first:
            cp.start()

    def finish(s_refs, o_refs, sems):
        c, sib, chips, copy, local, first = plan(s_refs, o_refs, sems)
        passed = []
        for i, (cx, cy) in enumerate(chips):
            for m in range(n):
                copy(m, i, 2 * cx + cy, c, sib).wait_recv()
                passed.append(copy(m, 3 + i, 2 * cx + cy, c, sib))
                passed[-1].start()
        for i, (cx, cy) in enumerate(chips):
            for m in range(n):
                copy(m, 3 + i, 2 * cx + cy, 1 - c, sib).wait_recv()
        for cp in first + passed:
            cp.wait_send()
        for cp in local:
            cp.wait()

    return _SideJob(
        inputs=list(shards), out_shape=[jax.ShapeDtypeStruct((N_CHIPS,) + s.shape, s.dtype) for s in shards],
        sems=[pltpu.SemaphoreType.DMA((6 * n,)), pltpu.SemaphoreType.DMA((6 * n,)), pltpu.SemaphoreType.DMA((n,))],
        start=start, finish=finish)


def _swap_halves(gs, name):
    n = len(gs)

    def body(*refs):
        g_refs, o_refs = refs[:n], refs[n:2 * n]
        send_sems, recv_sems = refs[2 * n:]
        x, y, c = _position()
        cps = [_remote(g_refs[m].at[j, 1 - c], o_refs[m].at[j], send_sems.at[N_CHIPS * m + j],
                       recv_sems.at[N_CHIPS * m + j], (x, y, 1 - c)) for m in range(n) for j in range(N_CHIPS)]
        for cp in cps:
            cp.start()
        for cp in cps:
            cp.wait()

    return pl.pallas_call(
        body, name=name,
        out_shape=[jax.ShapeDtypeStruct((N_CHIPS,) + g.shape[2:], g.dtype) for g in gs],
        in_specs=[_HBM] * n, out_specs=[_HBM] * n,
        scratch_shapes=[pltpu.SemaphoreType.DMA((N_CHIPS * n,)), pltpu.SemaphoreType.DMA((N_CHIPS * n,))],
    )(*gs)


def _scatter_job(ps):
    n = len(ps)

    def copies(p_refs, o_refs, sems):
        send_sems, recv_sems = sems
        x, y, c = _position()
        return [_remote(p_refs[m].at[2 * cx + cy], o_refs[m].at[i], send_sems.at[3 * m + i], recv_sems.at[3 * m + i],
                        (cx, cy, c)) for i, (cx, cy) in enumerate(_other_chips(x, y)) for m in range(n)]

    def start(*parts):
        for cp in copies(*parts):
            cp.start()

    def finish(*parts):
        for cp in copies(*parts):
            cp.wait()

    return _SideJob(
        inputs=list(ps), out_shape=[jax.ShapeDtypeStruct((N_CHIPS - 1,) + p.shape[1:], p.dtype) for p in ps],
        sems=[pltpu.SemaphoreType.DMA((3 * n,)), pltpu.SemaphoreType.DMA((3 * n,))], start=start, finish=finish)


def _join_halves(reds):
    n = len(reds)

    def body(*refs):
        r_refs, o_refs = refs[:n], refs[n:2 * n]
        send_sems, recv_sems, local_sems = refs[2 * n:]
        x, y, c = _position()
        local = [pltpu.make_async_copy(r_refs[m], o_refs[m].at[c], local_sems.at[m]) for m in range(n)]
        cps = [_remote(r_refs[m], o_refs[m].at[c], send_sems.at[m], recv_sems.at[m], (x, y, 1 - c)) for m in range(n)]
        for cp in local + cps:
            cp.start()
        for cp in cps + local:
            cp.wait()

    return pl.pallas_call(
        body, name="join_halves",
        out_shape=[jax.ShapeDtypeStruct((2,) + r.shape, r.dtype) for r in reds],
        in_specs=[_HBM] * n, out_specs=[_HBM] * n,
        scratch_shapes=[pltpu.SemaphoreType.DMA((n,)), pltpu.SemaphoreType.DMA((n,)), pltpu.SemaphoreType.DMA((n,))],
    )(*reds)


def _allgather_small(mine, name):
    m_per, n = mine.shape

    def body(x_ref, out_ref, send_sems, recv_sems, local_sem):
        x, y, c = _position()
        me, sibling = (x, y, c), (x, y, 1 - c)
        chips = _other_chips(x, y)

        def rows(px, py, pc):
            return out_ref.at[pl.ds((4 * px + 2 * py + pc) * m_per, m_per), :]

        def copy(k, block, to, src=None):
            return pltpu.make_async_remote_copy(src_ref=rows(*block) if src is None else src, dst_ref=rows(*block),
                                                send_sem=send_sems.at[k], recv_sem=recv_sems.at[k],
                                                device_id=to, device_id_type=_MESH)

        own = pltpu.make_async_copy(x_ref, rows(*me), local_sem)
        own.start()
        first = [copy(0, me, sibling, src=x_ref)]
        first += [copy(1 + j, me, (*chip, c), src=x_ref) for j, chip in enumerate(chips)]
        for cp in first:
            cp.start()
        passed = [copy(4 + j, (*chip, c), sibling) for j, chip in enumerate(chips)]
        for j, chip in enumerate(chips):
            copy(1 + j, (*chip, c), me).wait_recv()
            passed[j].start()
        copy(0, sibling, me).wait_recv()
        for j, chip in enumerate(chips):
            copy(4 + j, (*chip, 1 - c), me).wait_recv()
        for cp in first + passed:
            cp.wait_send()
        own.wait()

    return pl.pallas_call(
        body, name=name, out_shape=jax.ShapeDtypeStruct((N_DEV * m_per, n), mine.dtype),
        in_specs=[pl.BlockSpec(memory_space=pltpu.VMEM)], out_specs=pl.BlockSpec(memory_space=pltpu.VMEM),
        scratch_shapes=[pltpu.SemaphoreType.DMA((7,)), pltpu.SemaphoreType.DMA((7,)), pltpu.SemaphoreType.DMA],
    )(mine)


_ADD_BLOCK_BYTES = 3 << 20


def _add_rows(hr, cols):
    if hr * cols * 4 <= _ADD_BLOCK_BYTES:
        return hr
    return _pick(hr, (256, 128, 64, 32, 16))


def _add_own_half(g, recv, c_idx, name):
    nseg, _, hr, cols = g.shape
    tr = _add_rows(hr, cols)

    def body(c_ref, g_ref, r_ref, o_ref, ob_ref):
        s = g_ref[...] + r_ref[...]
        o_ref[...] = s
        ob_ref[...] = s.astype(jnp.bfloat16)

    blk = pl.BlockSpec((None, tr, cols), lambda j, i, c_ref: (j, i, 0))
    return pl.pallas_call(
        body, name=name,
        out_shape=(jax.ShapeDtypeStruct((nseg, hr, cols), F32), jax.ShapeDtypeStruct((nseg, hr, cols), jnp.bfloat16)),
        grid_spec=pltpu.PrefetchScalarGridSpec(
            num_scalar_prefetch=1, grid=(nseg, hr // tr),
            in_specs=[pl.BlockSpec((None, None, tr, cols), lambda j, i, c_ref: (j, c_ref[0], i, 0)), blk],
            out_specs=(blk, blk)),
        compiler_params=_cp(("parallel", "parallel")),
    )(c_idx, g, recv)


def _add_chips(p, recv, chip_idx, name):
    _, hr, cols = p.shape
    tr = _add_rows(hr, cols)

    def body(j_ref, p_ref, r_ref, o_ref):
        o_ref[...] = ((p_ref[...] + r_ref[0].astype(F32)) + r_ref[1].astype(F32)) + r_ref[2].astype(F32)

    return pl.pallas_call(
        body, name=name, out_shape=jax.ShapeDtypeStruct((hr, cols), F32),
        grid_spec=pltpu.PrefetchScalarGridSpec(
            num_scalar_prefetch=1, grid=(hr // tr,),
            in_specs=[pl.BlockSpec((None, tr, cols), lambda i, j_ref: (j_ref[0], i, 0)),
                      pl.BlockSpec((N_CHIPS - 1, tr, cols), lambda i, j_ref: (0, i, 0))],
            out_specs=pl.BlockSpec((tr, cols), lambda i, j_ref: (i, 0))),
        compiler_params=_cp(("parallel",)),
    )(chip_idx, p, recv)


def _adam_math(w, g, m, v):
    m = ADAM_B1 * m + (1.0 - ADAM_B1) * g
    v = ADAM_B2 * v + (1.0 - ADAM_B2) * (g * g)
    m_hat = m / (1.0 - ADAM_B1 ** ADAM_STEP)
    v_hat = v / (1.0 - ADAM_B2 ** ADAM_STEP)
    delta = -ADAM_LR * (m_hat / (jnp.sqrt(v_hat) + ADAM_EPS) + ADAM_WD * w)
    return delta, m, v


def _adam_big(w, g, m, v, name):
    rows, cols = w.shape
    tr = _pick(rows, (256, 128, 64, 32, 16, 8)) if rows % SUB == 0 else rows

    def body(w_ref, g_ref, m_ref, v_ref, d_ref, mo_ref, vo_ref):
        d_ref[...], mo_ref[...], vo_ref[...] = _adam_math(w_ref[...], g_ref[...], m_ref[...], v_ref[...])

    blk = pl.BlockSpec((tr, cols), lambda i: (i, 0))
    shp = jax.ShapeDtypeStruct((rows, cols), F32)
    return pl.pallas_call(
        body, name=name, out_shape=(shp, shp, shp), grid=(rows // tr,),
        in_specs=[blk, blk, blk, blk], out_specs=(blk, blk, blk), compiler_params=_cp(("parallel",)),
    )(w, g, m, v)


def _adam_small(w, gathered, m, v):
    rows = w.shape[0]

    def body(w_ref, a_ref, m_ref, v_ref, g_ref, d_ref, mo_ref, vo_ref):
        g = a_ref[0:rows, :]
        for k in range(1, N_DEV):
            g = g + a_ref[k * rows:(k + 1) * rows, :]
        g_ref[...] = g
        d_ref[...], mo_ref[...], vo_ref[...] = _adam_math(w_ref[...], g, m_ref[...], v_ref[...])

    shp = jax.ShapeDtypeStruct((rows, LANE), F32)
    return pl.pallas_call(body, name="adam_small", out_shape=(shp, shp, shp, shp), compiler_params=_cp(None))(
        w, gathered, m, v)


_WEIGHTS = ("rel_bias", "w_in", "b_gate", "ssm_conv_w", "ssm_conv_b", "ssm_dt_bias", "ssm_a_log", "ssm_d",
            "ssm_norm_w", "attn_sinks", "w_branch_ssm", "w_branch_attn", "w_mix_out", "ln1_g", "ln1_b", "w_up",
            "ffn_conv_w", "ffn_conv_b", "w_down", "ln2_g", "ln2_b")
_REPLICATED = tuple(n for n, _ in _SMALL[:13])


class _Exchange:
    def __init__(self, w, c_idx, chip_idx):
        self.c_idx, self.chip_idx = c_idx, chip_idx
        self.shards = {n: w[n].astype(jnp.bfloat16).reshape(2, s[0] // 2, s[1]) for n, s, _ in _MATS}
        self.spec = {n: (s, ax) for n, s, ax in _MATS}
        self.sums = {}

    def w_in_packed(self):
        (stack,) = _run_job(_gather_job([self.shards["w_in"]]), "allgather_w_in")
        return _pack_w_in_stack(stack.reshape((N_CHIPS,) + self.spec["w_in"][0]))

    def gather_rest(self):
        return _gather_job([self.shards[n] for n in _REST])

    def rest_weights(self, stacks):
        return {n: _from_stack(st.reshape((N_CHIPS,) + self.spec[n][0]), self.spec[n][1])
                for n, st in zip(_REST, stacks)}

    def reduce_job(self, grads):
        names, stacks = [], []
        for n, g in grads.items():
            name = "w_in" if n == "wp" else n
            s, ax = self.spec[name]
            st = _unpack_w_in_stack(g) if n == "wp" else _to_stack(g, s, ax)
            names.append(name)
            stacks.append(st.reshape(N_CHIPS, 2, s[0] // 2, s[1]))
        swapped = _swap_halves(stacks, "swap_" + names[0])
        halves = []
        for n, g, r in zip(names, stacks, swapped):
            self.sums[n], bf = _add_own_half(g, r, self.c_idx, "add_own_" + n)
            halves.append(bf)
        return _scatter_job(halves)

    def reduced(self, landed):
        names = list(self.sums)
        reds = [_add_chips(self.sums[n], r, self.chip_idx, "add_chips_" + n) for n, r in zip(names, landed)]
        return {n: g.reshape(self.spec[n][0]) for n, g in zip(names, _join_halves(reds))}


def _step(x, target, w, m, v):
    xi, yi, ci = _position()
    chip = 2 * xi + yi
    ex = _Exchange(w, jnp.reshape(ci, (1,)).astype(jnp.int32), jnp.reshape(chip, (1,)).astype(jnp.int32))

    wts = {n: w[n] for n in _REPLICATED}
    wts["wp"] = ex.w_in_packed()
    taps = jnp.concatenate([w[n].astype(F32).reshape(-1) for n, _ in _CONVS])
    taps = _allgather_small(_as_rows(taps, _CONV_ROWS, F32), "allgather_taps")
    taps = taps.reshape(N_CHIPS, 2, _CONV_ROWS * LANE)[:, 0]
    at = 0
    for n, s in _CONVS:
        wts[n] = _from_stack(taps[:, at:at + s[0] * s[1]].reshape((N_CHIPS,) + s), 1)
        at += s[0] * s[1]

    d_x, grads, small, landed = _local_step(x, target, wts, ex)

    outs = {"grad": ex.reduced(landed), "delta": {}, "m": {}, "v": {}}

    small = dict(small, g_ssm_conv_w=grads["ssm_conv_w"], g_ffn_conv_w=grads["ffn_conv_w"])
    all_small = _allgather_small(_pack_small(small), "allgather_small")
    packs = [_pack_small({n: d[n] for n in _REPLICATED}) for d in (w, m, v)]
    g_s, d_s, m_s, v_s = (_unpack_small(a) for a in _adam_small(packs[0], all_small, packs[1], packs[2]))
    for kind, part in (("grad", g_s), ("delta", d_s), ("m", m_s), ("v", v_s)):
        outs[kind].update({n: part[n] for n in _REPLICATED})
    for n, s in _CONVS:
        outs["grad"][n] = lax.dynamic_slice_in_dim(g_s["g_" + n], chip * s[1], s[1], axis=1)
    for n in [n for n, _, _ in _MATS] + [n for n, _ in _CONVS]:
        outs["delta"][n], outs["m"][n], outs["v"][n] = _adam_big(
            w[n].astype(F32), outs["grad"][n], m[n].astype(F32), v[n].astype(F32), "adam_" + n)
    loss = (0.5 / D_MODEL) * jnp.sum(g_s["loss_lanes"])
    return loss, d_x, outs


def kernel(x, rel_bias, w_in, b_gate, ssm_conv_w, ssm_conv_b, ssm_dt_bias, ssm_a_log, ssm_d, ssm_norm_w, attn_sinks, w_branch_ssm, w_branch_attn, w_mix_out, ln1_g, ln1_b, w_up, ffn_conv_w, ffn_conv_b, w_down, ln2_g, ln2_b, loss_target, m_rel_bias, m_w_in, m_b_gate, m_ssm_conv_w, m_ssm_conv_b, m_ssm_dt_bias, m_ssm_a_log, m_ssm_d, m_ssm_norm_w, m_attn_sinks, m_w_branch_ssm, m_w_branch_attn, m_w_mix_out, m_ln1_g, m_ln1_b, m_w_up, m_ffn_conv_w, m_ffn_conv_b, m_w_down, m_ln2_g, m_ln2_b, v_rel_bias, v_w_in, v_b_gate, v_ssm_conv_w, v_ssm_conv_b, v_ssm_dt_bias, v_ssm_a_log, v_ssm_d, v_ssm_norm_w, v_attn_sinks, v_w_branch_ssm, v_w_branch_attn, v_w_mix_out, v_ln1_g, v_ln1_b, v_w_up, v_ffn_conv_w, v_ffn_conv_b, v_w_down, v_ln2_g, v_ln2_b):
    given = dict(locals())
    drop = lambda a, n: a if n == "rel_bias" or a.ndim == 2 else a[0]
    w = {n: drop(given[n], n) for n in _WEIGHTS}
    m = {n: drop(given["m_" + n], n) for n in _WEIGHTS}
    v = {n: drop(given["v_" + n], n) for n in _WEIGHTS}
    loss, d_x, outs = _step(x[0], loss_target[0], w, m, v)
    like = lambda a, n: a.reshape(given[n].shape)
    res = [loss, d_x[None]]
    for kind in ("grad", "delta", "m", "v"):
        res += [like(outs[kind][n], n) for n in _WEIGHTS]
    return tuple(res)
```

```python
import math
from typing import NamedTuple

import numpy as np
import jax
import jax.numpy as jnp
from jax import lax
from jax.experimental import pallas as pl
from jax.experimental.pallas import tpu as pltpu

F32 = jnp.float32
_ACT = jnp.bfloat16
_MXU = jnp.bfloat16

D_MODEL = 1024
D_INNER = 2048
N_HEADS = 32
HEAD_P = 64
N_GROUPS = 4
N_STATE = 128
CHUNK = 128
CONV_DIM = 3072
SSM_K = 4
A_HEADS = 16
A_DH = 64
WIN = 128
REL_BUCKETS = 32
BIAS_ROWS = 64
D_FF = 2816
FFN_K = 3
ALPHA = 2.0 ** 0.25
LN_EPS = 1e-5
RMS_EPS = 1e-5
IN_COLS = 8480
NEG = -1e30

ADAM_LR = 0.001
ADAM_B1 = 0.9
ADAM_B2 = 0.999
ADAM_EPS = 1e-08
ADAM_WD = 0.01
ADAM_STEP = 10

LANE = 128
SUB = 8

P_Z, P_XS, P_G, P_Q, P_BC, P_K, P_V, P_DT = 0, 2048, 4096, 6144, 7168, 8192, 8320, 8448
P_W = 8704
P_MAIN = 8192
T_K, T_V, T_DT = P_K - P_MAIN, P_V - P_MAIN, P_DT - P_MAIN
_PIECES = ((0, 2048, P_Z), (2048, 2048, P_XS), (4096, 1024, P_BC), (5120, 32, P_DT), (5152, 1024, P_Q),
           (6176, 128, P_K), (6304, 128, P_V), (6432, 2048, P_G))

N_CHIPS = 4
N_DEV = 8


def _cp(sem=None, vmem_mb=48):
    return pltpu.CompilerParams(dimension_semantics=sem, vmem_limit_bytes=vmem_mb * 1024 * 1024)


def _pick(n, cands):
    for c in cands:
        if n % c == 0:
            return c
    raise ValueError(f"no block size for {n}")


def _rows8(p):
    k, c = p.shape
    return jnp.concatenate([p.astype(F32), jnp.zeros((SUB - k, c), F32)], axis=0)


class _SideJob(NamedTuple):
    inputs: list
    out_shape: list
    sems: list
    start: object
    finish: object


def _mm(a, b, name, *, trans_a=False, out_dtype=F32, res=None, res_scale=1.0, side=None):
    if trans_a:
        k_dim, m = a.shape
    else:
        m, k_dim = a.shape
    k2, n = b.shape
    assert k_dim == k2, (a.shape, b.shape)
    tm = _pick(m, (1408, 1024, 512, 256, 128))
    tn = _pick(n, (1408, 1024, 512, 256, 128))
    tk = _pick(k_dim, (2816, 2176, 2048, 1024, 512, 256, 128))
    nk = k_dim // tk
    grid = (m // tm, n // tn, nk)
    dn = (((0,), (0,)), ((), ())) if trans_a else (((1,), (0,)), ((), ()))
    n_in = 2 if res is None else 3
    ns_in = len(side.inputs) if side else 0
    ns_out = len(side.out_shape) if side else 0

    def body(*refs):
        a_ref, b_ref = refs[0], refs[1]
        o_ref = refs[n_in + ns_in]
        scratch = refs[n_in + ns_in + 1 + ns_out:]
        job_refs = (refs[n_in:n_in + ns_in], refs[n_in + ns_in + 1:n_in + ns_in + 1 + ns_out],
                    scratch[1:] if nk > 1 else scratch)
        i, j, k = pl.program_id(0), pl.program_id(1), pl.program_id(2)

        def finish(r):
            if res is not None:
                r = r + res_scale * refs[2][...]
            o_ref[...] = r.astype(out_dtype)

        if side:
            @pl.when(jnp.logical_and(jnp.logical_and(i == 0, j == 0), k == 0))
            def _():
                side.start(*job_refs)

        part = lax.dot_general(a_ref[...].astype(_MXU), b_ref[...].astype(_MXU), dn, preferred_element_type=F32)
        if nk == 1:
            finish(part)
        else:
            acc = scratch[0]

            @pl.when(k == 0)
            def _():
                acc[...] = part

            @pl.when(k > 0)
            def _():
                acc[...] += part

            @pl.when(k == nk - 1)
            def _():
                finish(acc[...])

        if side:
            @pl.when(jnp.logical_and(jnp.logical_and(i == grid[0] - 1, j == grid[1] - 1), k == nk - 1))
            def _():
                side.finish(*job_refs)

    if trans_a:
        a_spec = pl.BlockSpec((tk, tm), lambda i, j, k: (k, i))
    else:
        a_spec = pl.BlockSpec((tm, tk), lambda i, j, k: (i, k))
    in_specs = [a_spec, pl.BlockSpec((tk, tn), lambda i, j, k: (k, j))]
    args = [a, b]
    if res is not None:
        in_specs.append(pl.BlockSpec((tm, tn), lambda i, j, k: (i, j)))
        args.append(res)
    out_spec = pl.BlockSpec((tm, tn), lambda i, j, k: (i, j))
    out_shape = jax.ShapeDtypeStruct((m, n), out_dtype)
    scratch_shapes = [pltpu.VMEM((tm, tn), F32)] if nk > 1 else []
    if not side:
        return pl.pallas_call(
            body, name=name, out_shape=out_shape, grid=grid, in_specs=in_specs, out_specs=out_spec,
            scratch_shapes=scratch_shapes, compiler_params=_cp(("parallel", "parallel", "arbitrary")),
        )(*args)
    hbm = pl.BlockSpec(memory_space=pltpu.HBM)
    outs = pl.pallas_call(
        body, name=name, out_shape=[out_shape] + list(side.out_shape), grid=grid,
        in_specs=in_specs + [hbm] * ns_in, out_specs=[out_spec] + [hbm] * ns_out,
        scratch_shapes=scratch_shapes + list(side.sems),
        compiler_params=_cp(("arbitrary", "arbitrary", "arbitrary")),
    )(*args, *side.inputs)
    return outs[0], list(outs[1:])


def _shift_down(cur, prev8, s):
    r = pltpu.roll(cur, s, 0)
    p = pltpu.roll(prev8, s, 0)
    row8 = lax.broadcasted_iota(jnp.int32, (SUB, 1), 0)
    fixed = jnp.where(row8 < s, p, r[0:SUB])
    if cur.shape[0] == SUB:
        return fixed
    return jnp.concatenate([fixed, r[SUB:]], axis=0)


def _shift_up(cur, next8, s):
    tm = cur.shape[0]
    r = pltpu.roll(cur, tm - s, 0)
    p = pltpu.roll(next8, SUB - s, 0)
    row8 = lax.broadcasted_iota(jnp.int32, (SUB, 1), 0)
    fixed = jnp.where(row8 >= SUB - s, p, r[tm - SUB:])
    return jnp.concatenate([r[:tm - SUB], fixed], axis=0)


def _conv_pre(cur, prev8, w_ref, b_row, taps):
    acc = cur * w_ref[taps - 1:taps, :] + b_row
    for s in range(1, taps):
        acc = acc + _shift_down(cur, prev8, s) * w_ref[taps - 1 - s:taps - s, :]
    return acc


def _dot01_r(x, m01, parts=3):
    acc = None
    r = x
    for _ in range(parts):
        hi = r.astype(jnp.bfloat16)
        t = jnp.dot(hi, m01, preferred_element_type=F32)
        acc = t if acc is None else acc + t
        r = r - hi.astype(F32)
    return acc


def _dot01_l(m01, x, parts=3):
    acc = None
    r = x
    for _ in range(parts):
        hi = r.astype(jnp.bfloat16)
        t = jnp.dot(m01, hi, preferred_element_type=F32)
        acc = t if acc is None else acc + t
        r = r - hi.astype(F32)
    return acc


def _dot(a, b):
    return jnp.dot(a.astype(_MXU), b.astype(_MXU), preferred_element_type=F32)


def _dot_nt(a, b):
    return lax.dot_general(a.astype(_MXU), b.astype(_MXU), (((1,), (1,)), ((), ())), preferred_element_type=F32)


def _dot_tn(a, b):
    return lax.dot_general(a.astype(_MXU), b.astype(_MXU), (((0,), (0,)), ((), ())), preferred_element_type=F32)


def _sigmoid(x):
    return 1.0 / (1.0 + jnp.exp(-x))


def _half_masks():
    lane = lax.broadcasted_iota(jnp.int32, (1, LANE), 1)
    lo = (lane < 64).astype(F32)
    return lo, 1.0 - lo


_TC = 512


def _tm_rows(t):
    return min(256, t)


HALO = 16


def _prev_halo(tm, width, pos):
    def index(*ids):
        i, col = pos(*ids)
        return (jnp.maximum(i * (tm // HALO) - 1, 0), col)
    return pl.BlockSpec((HALO, width), index)


def _next_halo(tm, t, width, pos):
    def index(*ids):
        i, col = pos(*ids)
        return (jnp.minimum((i + 1) * (tm // HALO), t // HALO - 1), col)
    return pl.BlockSpec((HALO, width), index)


def _conv_silu_fwd(proj, colblk0, nblk, w8, b8, name):
    t = proj.shape[0]
    tm = _tm_rows(t)

    def body(c_ref, p_ref, w_ref, b_ref, o_ref, pre_ref):
        i = pl.program_id(1)
        prev8 = jnp.where(i > 0, p_ref[SUB:HALO, :].astype(F32), 0.0)
        pre = _conv_pre(c_ref[...].astype(F32), prev8, w_ref, b_ref[0:1, :], SSM_K)
        o_ref[...] = pre * _sigmoid(pre)
        pre_ref[...] = pre.astype(_ACT)

    blk = pl.BlockSpec((tm, _TC), lambda j, i: (i, j))
    return pl.pallas_call(
        body, name=name,
        out_shape=(jax.ShapeDtypeStruct((t, nblk * _TC), F32), jax.ShapeDtypeStruct((t, nblk * _TC), _ACT)),
        grid=(nblk, t // tm),
        in_specs=[pl.BlockSpec((tm, _TC), lambda j, i: (i, colblk0 + j)),
                  _prev_halo(tm, _TC, lambda j, i: (i, colblk0 + j)),
                  pl.BlockSpec((SUB, _TC), lambda j, i: (0, j)),
                  pl.BlockSpec((SUB, _TC), lambda j, i: (0, j))],
        out_specs=(blk, blk),
        compiler_params=_cp(("parallel", "parallel")),
    )(proj, proj, w8, b8)


def _silu_grad(pre):
    sg = _sigmoid(pre)
    return sg * (1.0 + pre * (1.0 - sg))


def _conv_grads(d, d_next8, cur, w_ref, acc_ref, taps, cols=slice(None)):
    du = d * w_ref[taps - 1:taps, :]
    acc_ref[taps:taps + 1, cols] += jnp.sum(d, axis=0, keepdims=True)
    acc_ref[taps - 1:taps, cols] += jnp.sum(d * cur, axis=0, keepdims=True)
    for s in range(1, taps):
        up = _shift_up(d, d_next8, s)
        du = du + up * w_ref[taps - 1 - s:taps - s, :]
        acc_ref[taps - 1 - s:taps - s, cols] += jnp.sum(up * cur, axis=0, keepdims=True)
    return du


def _conv_silu_bwd(proj, colblk0, nblk, pre, w8, d_out, name):
    t = proj.shape[0]
    tm = _tm_rows(t)
    nt = t // tm

    def body(c_ref, pre_ref, pren_ref, w_ref, d_ref, dn_ref, du_ref, acc_ref):
        i = pl.program_id(1)

        @pl.when(i == 0)
        def _():
            acc_ref[...] = jnp.zeros_like(acc_ref)

        dpre = d_ref[...].astype(F32) * _silu_grad(pre_ref[...].astype(F32))
        dpre_n = jnp.where(i < nt - 1, dn_ref[0:SUB, :].astype(F32) * _silu_grad(pren_ref[0:SUB, :].astype(F32)), 0.0)
        du_ref[...] = _conv_grads(dpre, dpre_n, c_ref[...].astype(F32), w_ref, acc_ref, SSM_K).astype(_ACT)

    c = nblk * _TC
    blk = pl.BlockSpec((tm, _TC), lambda j, i: (i, j))
    nxt = _next_halo(tm, t, _TC, lambda j, i: (i, j))
    par = pl.BlockSpec((SUB, _TC), lambda j, i: (0, j))
    return pl.pallas_call(
        body, name=name,
        out_shape=(jax.ShapeDtypeStruct((t, c), _ACT), jax.ShapeDtypeStruct((SUB, c), F32)),
        grid=(nblk, nt),
        in_specs=[pl.BlockSpec((tm, _TC), lambda j, i: (i, colblk0 + j)), blk, nxt, par, blk, nxt],
        out_specs=(blk, par),
        compiler_params=_cp(("parallel", "arbitrary")),
    )(proj, pre, pre, w8, d_out, d_out)


def _expand_consts():
    e = np.zeros((LANE, D_INNER), np.float32)
    for h in range(N_HEADS):
        e[h, h * HEAD_P:(h + 1) * HEAD_P] = 1.0
    return jnp.asarray(e, jnp.bfloat16), jnp.asarray(e.T.copy(), jnp.bfloat16)


def _ssd_common(dtr_ref, dtb_ref, alog_ref, e_ref):
    lane = lax.broadcasted_iota(jnp.int32, (1, LANE), 1)
    hm = lane < N_HEADS
    pre = dtr_ref[...] + dtb_ref[0:1, :]
    dt = jnp.where(hm, jnp.maximum(pre, 0.0) + jnp.log(1.0 + jnp.exp(-jnp.abs(pre))), 0.0)
    a_row = jnp.where(hm, -jnp.exp(alog_ref[0:1, :]), 0.0)
    adt = dt * a_row
    r = lax.broadcasted_iota(jnp.int32, (CHUNK, CHUNK), 0)
    c = lax.broadcasted_iota(jnp.int32, (CHUNK, CHUNK), 1)
    causal = r >= c
    acs = _dot01_l(causal.astype(jnp.bfloat16), adt)
    e = e_ref[...]
    acs_x = _dot01_r(acs, e, parts=2)
    dt_x = _dot01_r(dt, e, parts=2)
    return pre, dt, a_row, acs, acs_x, dt_x, causal, hm


def _decay(acs, acs_t, h, causal):
    seg = acs[:, h:h + 1] - acs_t[h:h + 1, :]
    return jnp.exp(jnp.where(causal, seg, NEG))


def _ssd_fwd(xs_c, bc_c, proj, tail, dtb8, alog8, dsk8, nw8, name):
    t = xs_c.shape[0]
    nc = t // CHUNK
    e_bf, _ = _expand_consts()
    gw = D_INNER // N_GROUPS

    def body(xs_ref, bc_ref, dtr_ref, z_ref, dtb_ref, alog_ref, dsk_ref, nw_ref, e_ref,
             y_ref, ys_ref, hp_ref, h_ref):
        c_id = pl.program_id(0)

        @pl.when(c_id == 0)
        def _():
            h_ref[...] = jnp.zeros_like(h_ref)

        _, dt, a_row, acs, acs_x, dt_x, causal, _ = _ssd_common(dtr_ref, dtb_ref, alog_ref, e_ref)
        acs_t = acs.T
        xs = xs_ref[...]
        x_dt = xs * dt_x
        last_x = acs_x[CHUNK - 1:CHUNK, :]
        w_end = jnp.exp(last_x - acs_x)
        e_in = jnp.exp(acs_x)
        d_x = _dot01_r(dsk_ref[...], e_ref[...])[0:1, :]
        hprev = h_ref[...]
        hp_ref[...] = hprev
        lo, hi = _half_masks()
        for g in range(N_GROUPS):
            bg = bc_ref[:, g * N_STATE:(g + 1) * N_STATE]
            cg = bc_ref[:, N_GROUPS * N_STATE + g * N_STATE:N_GROUPS * N_STATE + (g + 1) * N_STATE]
            sl = slice(g * gw, (g + 1) * gw)
            gm = _dot_nt(cg, bg)
            st = _dot(bg.T, x_dt[:, sl] * w_end[:, sl])
            y_off = _dot(cg, hprev[:, sl]) * e_in[:, sl]
            for j in range(gw // LANE):
                h0 = g * (gw // HEAD_P) + 2 * j
                cs = slice(g * gw + j * LANE, g * gw + (j + 1) * LANE)
                xp = x_dt[:, cs]
                m0 = gm * _decay(acs, acs_t, h0, causal)
                m1 = gm * _decay(acs, acs_t, h0 + 1, causal)
                yd = _dot(m0, xp * lo) + _dot(m1, xp * hi)
                y_ref[:, cs] = yd + y_off[:, j * LANE:(j + 1) * LANE] + xs[:, cs] * d_x[:, cs]
            h_ref[:, sl] = hprev[:, sl] * jnp.exp(last_x[:, sl]) + st
        y = y_ref[...]
        z = z_ref[...].astype(F32)
        y2 = y * (z * _sigmoid(z))
        for g in range(N_GROUPS):
            sl = slice(g * gw, (g + 1) * gw)
            yg = y2[:, sl]
            rinv = lax.rsqrt(jnp.mean(yg * yg, axis=-1, keepdims=True) + RMS_EPS)
            ys_ref[:, sl] = (yg * rinv * nw_ref[0:1, sl]).astype(_ACT)

    small = pl.BlockSpec((SUB, LANE), lambda c: (0, 0))
    return pl.pallas_call(
        body, name=name,
        out_shape=(jax.ShapeDtypeStruct((t, D_INNER), F32), jax.ShapeDtypeStruct((t, D_INNER), _ACT),
                   jax.ShapeDtypeStruct((t, D_INNER), F32)),
        grid=(nc,),
        in_specs=[pl.BlockSpec((CHUNK, D_INNER), lambda c: (c, 0)),
                  pl.BlockSpec((CHUNK, 1024), lambda c: (c, 0)),
                  pl.BlockSpec((CHUNK, LANE), lambda c: (c, T_DT // LANE)),
                  pl.BlockSpec((CHUNK, D_INNER), lambda c: (c, P_Z // D_INNER)),
                  small, small, small,
                  pl.BlockSpec((SUB, D_INNER), lambda c: (0, 0)),
                  pl.BlockSpec((LANE, D_INNER), lambda c: (0, 0))],
        out_specs=(pl.BlockSpec((CHUNK, D_INNER), lambda c: (c, 0)),
                   pl.BlockSpec((CHUNK, D_INNER), lambda c: (c, 0)),
                   pl.BlockSpec((N_STATE, D_INNER), lambda c: (c, 0))),
        scratch_shapes=[pltpu.VMEM((N_STATE, D_INNER), F32)],
        compiler_params=_cp(("arbitrary",)),
    )(xs_c, bc_c, tail, proj, dtb8, alog8, dsk8, nw8, e_bf)


def _ssd_bwd(d_ys, y, xs_c, bc_c, proj, tail, hprev_all, dtb8, alog8, dsk8, nw8, name):
    t = xs_c.shape[0]
    nc = t // CHUNK
    e_bf, et_bf = _expand_consts()
    gw = D_INNER // N_GROUPS

    def body(dys_ref, y_ref, xs_ref, bc_ref, dtr_ref, z_ref, hp_ref, dtb_ref, alog_ref, dsk_ref, nw_ref,
             e_ref, et_ref, dxs_ref, dbc_ref, dz_ref, ddt_ref, acc_ref, dnw_ref, dh_ref, dx_ref):
        step = pl.program_id(0)

        @pl.when(step == 0)
        def _():
            dh_ref[...] = jnp.zeros_like(dh_ref)
            acc_ref[...] = jnp.zeros_like(acc_ref)
            dnw_ref[...] = jnp.zeros_like(dnw_ref)

        pre, dt, a_row, acs, acs_x, dt_x, causal, hm = _ssd_common(dtr_ref, dtb_ref, alog_ref, e_ref)
        acs_t = acs.T
        et = et_ref[...]
        xs = xs_ref[...]
        x_dt = xs * dt_x
        last_x = acs_x[CHUNK - 1:CHUNK, :]
        w_end = jnp.exp(last_x - acs_x)
        e_in = jnp.exp(acs_x)
        e_last = jnp.exp(last_x)
        d_x = _dot01_r(dsk_ref[...], e_ref[...])[0:1, :]

        y = y_ref[...]
        z = z_ref[...].astype(F32)
        sz = _sigmoid(z)
        gz = z * sz
        y2 = y * gz
        dys = dys_ref[...].astype(F32)
        for g in range(N_GROUPS):
            sl = slice(g * gw, (g + 1) * gw)
            yg = y2[:, sl]
            rinv = lax.rsqrt(jnp.mean(yg * yg, axis=-1, keepdims=True) + RMS_EPS)
            nrm = yg * rinv
            dn = dys[:, sl] * nw_ref[0:1, sl]
            dnw_ref[0:1, sl] += jnp.sum(dys[:, sl] * nrm, axis=0, keepdims=True)
            dx_ref[:, sl] = rinv * (dn - nrm * jnp.mean(dn * nrm, axis=-1, keepdims=True))
        dy2 = dx_ref[...]
        dy = dy2 * gz
        dz_ref[...] = (dy2 * y * (sz * (1.0 + z * (1.0 - sz)))).astype(_ACT)

        dh_next = dh_ref[...]
        hprev = hp_ref[...]
        lo, hi = _half_masks()
        r = lax.broadcasted_iota(jnp.int32, (CHUNK, CHUNK), 0)
        c = lax.broadcasted_iota(jnp.int32, (CHUNK, CHUNK), 1)
        from_here = (c >= r).astype(jnp.bfloat16)
        before = c < r
        lane = lax.broadcasted_iota(jnp.int32, (1, LANE), 1)
        da_intra = jnp.zeros((CHUNK, LANE), F32)
        v_seg = jnp.zeros((CHUNK, LANE), F32)
        z_seg = jnp.zeros((CHUNK, LANE), F32)
        tail_parts = []
        for g in range(N_GROUPS):
            bg = bc_ref[:, g * N_STATE:(g + 1) * N_STATE]
            cg = bc_ref[:, N_GROUPS * N_STATE + g * N_STATE:N_GROUPS * N_STATE + (g + 1) * N_STATE]
            sl = slice(g * gw, (g + 1) * gw)
            et_g = et_ref[g * gw:(g + 1) * gw, :]
            gm = _dot_nt(cg, bg)
            dzg = e_in[:, sl] * dy[:, sl]
            dcg = _dot_nt(dzg, hprev[:, sl])
            dh_c = _dot(cg.T, dzg)
            q = _dot(bg, dh_next[:, sl])
            dbg = _dot_nt(x_dt[:, sl] * w_end[:, sl], dh_next[:, sl])
            y_off = _dot(cg, hprev[:, sl]) * e_in[:, sl]
            v_seg = v_seg + _dot01_r(dy[:, sl] * y_off, et_g, parts=2)
            z_seg = z_seg + _dot01_r(w_end[:, sl] * q * x_dt[:, sl], et_g, parts=2)
            dgm = jnp.zeros((CHUNK, CHUNK), F32)
            for j in range(gw // LANE):
                h0 = g * (gw // HEAD_P) + 2 * j
                cs = slice(g * gw + j * LANE, g * gw + (j + 1) * LANE)
                xp = x_dt[:, cs]
                dyp = dy[:, cs]
                dxd = jnp.zeros((CHUNK, LANE), F32)
                for half, msk in ((0, lo), (1, hi)):
                    lam = _decay(acs, acs_t, h0 + half, causal)
                    mm = gm * lam
                    dym = dyp * msk
                    dmm = _dot_nt(dym, xp)
                    dxd = dxd + _dot_tn(mm, dym)
                    dgm = dgm + dmm * lam
                    below = _dot(from_here, dmm * mm)
                    col = jnp.sum(jnp.where(before, below, 0.0), axis=-1, keepdims=True)
                    da_intra = da_intra + jnp.where(lane == h0 + half, col, 0.0)
                dx_ref[:, cs] = dxd + w_end[:, cs] * q[:, j * LANE:(j + 1) * LANE]
            dbc_ref[:, N_GROUPS * N_STATE + g * N_STATE:N_GROUPS * N_STATE + (g + 1) * N_STATE] = dcg + _dot(dgm, bg)
            dbc_ref[:, g * N_STATE:(g + 1) * N_STATE] = dbg + _dot_tn(dgm, cg)
            dh_ref[:, sl] = e_last[:, sl] * dh_next[:, sl] + dh_c
            tail_parts.append(e_last[:, sl] * jnp.sum(dh_next[:, sl] * hprev[:, sl], axis=0, keepdims=True))
        dxt = dx_ref[...]

        u_seg = _dot01_r(xs * dxt, et, parts=2)
        q_full = jnp.concatenate(tail_parts, axis=1)
        t_row = _dot01_r(jnp.broadcast_to(q_full, (SUB, D_INNER)), et)[0:1, :]
        d_alpha = (da_intra + _dot01_l(from_here, v_seg) + _dot01_l(before.astype(jnp.bfloat16), z_seg) + t_row)
        d_dt = a_row * d_alpha + u_seg
        sgp = _sigmoid(pre)
        d_raw = jnp.where(hm, d_dt * sgp, 0.0)
        ddt_ref[...] = d_raw.astype(_ACT)
        acc_ref[0:1, :] += jnp.sum(d_raw, axis=0, keepdims=True)
        acc_ref[1:2, :] += jnp.sum(d_alpha * dt, axis=0, keepdims=True) * a_row
        dd_row = jnp.sum(dy * xs, axis=0, keepdims=True)
        acc_ref[2:3, :] += _dot01_r(jnp.broadcast_to(dd_row, (SUB, D_INNER)), et)[0:1, :]
        dxs_ref[...] = dy * d_x + dxt * dt_x

    rev = lambda c: (nc - 1 - c, 0)
    small = pl.BlockSpec((SUB, LANE), lambda c: (0, 0))
    return pl.pallas_call(
        body, name=name,
        out_shape=(jax.ShapeDtypeStruct((t, D_INNER), F32), jax.ShapeDtypeStruct((t, 1024), F32),
                   jax.ShapeDtypeStruct((t, D_INNER), _ACT), jax.ShapeDtypeStruct((t, LANE), _ACT),
                   jax.ShapeDtypeStruct((SUB, LANE), F32), jax.ShapeDtypeStruct((SUB, D_INNER), F32)),
        grid=(nc,),
        in_specs=[pl.BlockSpec((CHUNK, D_INNER), rev),
                  pl.BlockSpec((CHUNK, D_INNER), rev),
                  pl.BlockSpec((CHUNK, D_INNER), rev),
                  pl.BlockSpec((CHUNK, 1024), rev),
                  pl.BlockSpec((CHUNK, LANE), lambda c: (nc - 1 - c, T_DT // LANE)),
                  pl.BlockSpec((CHUNK, D_INNER), lambda c: (nc - 1 - c, P_Z // D_INNER)),
                  pl.BlockSpec((N_STATE, D_INNER), rev),
                  small, small, small,
                  pl.BlockSpec((SUB, D_INNER), lambda c: (0, 0)),
                  pl.BlockSpec((LANE, D_INNER), lambda c: (0, 0)),
                  pl.BlockSpec((D_INNER, LANE), lambda c: (0, 0))],
        out_specs=(pl.BlockSpec((CHUNK, D_INNER), rev),
                   pl.BlockSpec((CHUNK, 1024), rev),
                   pl.BlockSpec((CHUNK, D_INNER), rev),
                   pl.BlockSpec((CHUNK, LANE), rev),
                   small,
                   pl.BlockSpec((SUB, D_INNER), lambda c: (0, 0))),
        scratch_shapes=[pltpu.VMEM((N_STATE, D_INNER), F32), pltpu.VMEM((CHUNK, D_INNER), F32)],
        compiler_params=_cp(("arbitrary",), vmem_mb=56),
    )(d_ys, y, xs_c, bc_c, tail, proj, hprev_all, dtb8, alog8, dsk8, nw8, e_bf, et_bf)


def _rel_tables():
    qi = np.arange(WIN)[:, None] + WIN
    kj = np.arange(2 * WIN)[None, :]
    rel = qi - kj
    n = np.maximum(rel, 0)
    max_exact = REL_BUCKETS // 2
    nf = np.maximum(n, 1).astype(np.float32)
    large = max_exact + (np.log(nf / np.float32(max_exact)) / np.float32(math.log(WIN / max_exact))
                         * np.float32(REL_BUCKETS - max_exact)).astype(np.int32)
    large = np.minimum(large, REL_BUCKETS - 1)
    bucket = np.where(n < max_exact, n, large)
    valid = (rel >= 0) & (rel < WIN)
    sink_col = np.broadcast_to(kj == 0, rel.shape)
    onehot = np.zeros((BIAS_ROWS, WIN * 2 * WIN), np.float32)
    flat_b = np.where(sink_col, REL_BUCKETS, bucket).reshape(-1)
    flat_v = (valid | sink_col).reshape(-1)
    first_v = ((valid & (kj >= WIN)) | sink_col).reshape(-1)
    idx = np.arange(WIN * 2 * WIN)
    onehot[flat_b[flat_v], idx[flat_v]] = 1.0
    return onehot, np.stack([first_v, flat_v]).astype(np.float32)


def _bias_expand(table_t, name):
    onehot, valid = _rel_tables()

    def body(rb_ref, oh_ref, v_ref, o_ref):
        full = _dot01_r(rb_ref[...], oh_ref[...])
        o_ref[0] = jnp.where(v_ref[0:1, :] > 0.5, full, NEG)
        o_ref[1] = jnp.where(v_ref[1:2, :] > 0.5, full, NEG)

    return pl.pallas_call(
        body, name=name, out_shape=jax.ShapeDtypeStruct((2, A_HEADS, WIN * 2 * WIN), F32),
        compiler_params=_cp(None),
    )(table_t, jnp.asarray(onehot, jnp.bfloat16), jnp.asarray(valid, F32))


def _bias_reduce(dbias, name):
    onehot, _ = _rel_tables()

    def body(d_ref, oh_ref, o_ref):
        acc = None
        r = d_ref[...]
        for _ in range(3):
            hi = r.astype(jnp.bfloat16)
            tt = lax.dot_general(hi, oh_ref[...], (((1,), (1,)), ((), ())), preferred_element_type=F32)
            acc = tt if acc is None else acc + tt
            r = r - hi.astype(F32)
        o_ref[...] = acc

    return pl.pallas_call(
        body, name=name, out_shape=jax.ShapeDtypeStruct((A_HEADS, BIAS_ROWS), F32),
        compiler_params=_cp(None),
    )(dbias, jnp.asarray(onehot, jnp.bfloat16))


def _attn_bands(kc_ref, kp_ref, vc_ref, vp_ref, has_prev):
    lo, hi = _half_masks()
    row = lax.broadcasted_iota(jnp.int32, (2 * WIN, 1), 0)
    keep = (row > 0).astype(F32)
    kb = jnp.concatenate([jnp.where(has_prev, kp_ref[...], 0.0), kc_ref[...]], axis=0) * (keep * (A_DH ** -0.5))
    vb = jnp.concatenate([jnp.where(has_prev, vp_ref[...], 0.0), vc_ref[...]], axis=0) * keep
    kr = pltpu.roll(kb, 64, 1)
    vr = pltpu.roll(vb, 64, 1)
    kk = ((kb * lo, kr * hi), (kr * lo, kb * hi))
    vv = ((vb * lo, vr * hi), (vr * lo, vb * hi))
    return kk, vv, (hi, lo)


def _attn_logits(q_ref, kk, lg_ref):
    for h in range(A_HEADS):
        j, half, kv = h // 2, h % 2, h // (A_HEADS // 2)
        lg_ref[h] = _dot_nt(q_ref[:, j * LANE:(j + 1) * LANE], kk[kv][half])


def _attn_fwd(proj, tail, bias, name):
    t = proj.shape[0]
    nb = t // WIN

    def body(q_ref, kc_ref, kp_ref, vc_ref, vp_ref, b_ref, o_ref, lg_ref, p_ref):
        n = pl.program_id(0)
        kk, vv, ones = _attn_bands(kc_ref, kp_ref, vc_ref, vp_ref, n > 0)
        _attn_logits(q_ref, kk, lg_ref)
        for h in range(A_HEADS):
            logits = lg_ref[h] + b_ref[h]
            p_ref[h] = jnp.exp(logits - jnp.max(logits, axis=-1, keepdims=True)).astype(_MXU)
        lane = lax.broadcasted_iota(jnp.int32, (1, LANE), 1)
        for j in range(A_HEADS // 2):
            kv = (2 * j) // (A_HEADS // 2)
            outs = []
            for half in range(2):
                o = jnp.dot(p_ref[2 * j + half], (vv[kv][half] + ones[half]).astype(_MXU), preferred_element_type=F32)
                outs.append(o / pltpu.roll(o, 64, 1))
            o_ref[:, j * LANE:(j + 1) * LANE] = jnp.where(lane < 64, outs[0], outs[1]).astype(_ACT)

    kvspec = lambda col, prev: pl.BlockSpec(
        (WIN, LANE), (lambda n: (jnp.maximum(n - 1, 0), col)) if prev else (lambda n: (n, col)))
    return pl.pallas_call(
        body, name=name, out_shape=jax.ShapeDtypeStruct((t, D_MODEL), _ACT),
        grid=(nb,),
        in_specs=[pl.BlockSpec((WIN, 1024), lambda n: (n, P_Q // 1024)),
                  kvspec(T_K // LANE, False), kvspec(T_K // LANE, True),
                  kvspec(T_V // LANE, False), kvspec(T_V // LANE, True),
                  pl.BlockSpec((None, A_HEADS, WIN, 2 * WIN), lambda n: (jnp.minimum(n, 1), 0, 0, 0))],
        out_specs=pl.BlockSpec((WIN, 1024), lambda n: (n, 0)),
        scratch_shapes=[pltpu.VMEM((A_HEADS, WIN, 2 * WIN), F32), pltpu.VMEM((A_HEADS, WIN, 2 * WIN), _MXU)],
        compiler_params=_cp(("parallel",)),
    )(proj, tail, tail, tail, tail, bias)


def _attn_bwd(proj, tail, bias, y_attn, d_out, name):
    t = proj.shape[0]
    nb = t // WIN

    def body(q_ref, kc_ref, kp_ref, vc_ref, vp_ref, b_ref, y_ref, do_ref,
             dq_ref, dk_ref, dv_ref, db_ref, ck_ref, cv_ref, lg_ref, dl_ref, p_ref):
        n = pl.program_id(0)

        @pl.when(n == 0)
        def _():
            db_ref[...] = jnp.zeros_like(db_ref)
            ck_ref[...] = jnp.zeros_like(ck_ref)
            cv_ref[...] = jnp.zeros_like(cv_ref)

        @pl.when(n < nb)
        def _():
            kk, vv, _ = _attn_bands(kc_ref, kp_ref, vc_ref, vp_ref, n > 0)
            lo, hi = _half_masks()
            ones_k = jnp.ones((2 * WIN, LANE), jnp.bfloat16)
            ones_d = jnp.ones((LANE, LANE), jnp.bfloat16)
            _attn_logits(q_ref, kk, lg_ref)
            for h in range(A_HEADS):
                j, half, kv = h // 2, h % 2, h // (A_HEADS // 2)
                msk = hi if half else lo
                logits = lg_ref[h] + b_ref[h]
                p = jnp.exp(logits - jnp.max(logits, axis=-1, keepdims=True))
                den = jnp.dot(p.astype(_MXU), ones_k.astype(_MXU), preferred_element_type=F32)
                dop = do_ref[:, j * LANE:(j + 1) * LANE].astype(F32)
                delta = _dot01_r(dop * y_ref[:, j * LANE:(j + 1) * LANE].astype(F32) * msk, ones_d, parts=2)
                inv = 1.0 / den
                probs = p * jnp.concatenate([inv, inv], axis=1)
                dprobs = _dot_nt(dop, vv[kv][half])
                dlog = probs * (dprobs - jnp.concatenate([delta, delta], axis=1))
                db_ref[h] += dlog
                dl_ref[h] = dlog.astype(_MXU)
                p_ref[h] = probs.astype(_MXU)
            dk_t = [[None, None], [None, None]]
            dv_t = [[None, None], [None, None]]
            for j in range(A_HEADS // 2):
                kv = (2 * j) // (A_HEADS // 2)
                qs = q_ref[:, j * LANE:(j + 1) * LANE].astype(F32) * (A_DH ** -0.5)
                dop = do_ref[:, j * LANE:(j + 1) * LANE].astype(F32)
                dq = None
                for half, msk in ((0, lo), (1, hi)):
                    h = 2 * j + half
                    dqh = jnp.dot(dl_ref[h], kk[kv][half].astype(_MXU), preferred_element_type=F32)
                    dq = dqh if dq is None else dq + dqh
                    dkh = lax.dot_general((qs * msk).astype(_MXU), dl_ref[h], (((0,), (0,)), ((), ())),
                                          preferred_element_type=F32)
                    dvh = lax.dot_general((dop * msk).astype(_MXU), p_ref[h], (((0,), (0,)), ((), ())),
                                          preferred_element_type=F32)
                    dk_t[kv][half] = dkh if dk_t[kv][half] is None else dk_t[kv][half] + dkh
                    dv_t[kv][half] = dvh if dv_t[kv][half] is None else dv_t[kv][half] + dvh
                dq_ref[:, j * LANE:(j + 1) * LANE] = dq.astype(_ACT)
            row = lax.broadcasted_iota(jnp.int32, (2 * WIN, 1), 0)

            def band(acc):
                a = (acc[0][0] + pltpu.roll(acc[0][1], 64, 0)) + (pltpu.roll(acc[1][0], 64, 0) + acc[1][1])
                return jnp.where(row > 0, a.T, 0.0)

            dkb = band(dk_t)
            dvb = band(dv_t)
            dk_ref[...] = (ck_ref[...] + dkb[0:WIN]).astype(_ACT)
            dv_ref[...] = (cv_ref[...] + dvb[0:WIN]).astype(_ACT)
            ck_ref[...] = dkb[WIN:]
            cv_ref[...] = dvb[WIN:]

        @pl.when(n == nb)
        def _():
            dk_ref[...] = ck_ref[...].astype(_ACT)
            dv_ref[...] = cv_ref[...].astype(_ACT)

    cur = lambda n: jnp.minimum(n, nb - 1)
    prv = lambda n: jnp.maximum(jnp.minimum(n, nb - 1) - 1, 0)
    kvspec = lambda col, prev: pl.BlockSpec(
        (WIN, LANE), (lambda n: (prv(n), col)) if prev else (lambda n: (cur(n), col)))
    band_shape = (A_HEADS, WIN, 2 * WIN)
    return pl.pallas_call(
        body, name=name,
        out_shape=(jax.ShapeDtypeStruct((t, D_MODEL), _ACT), jax.ShapeDtypeStruct((t, LANE), _ACT),
                   jax.ShapeDtypeStruct((t, LANE), _ACT), jax.ShapeDtypeStruct(band_shape, F32)),
        grid=(nb + 1,),
        in_specs=[pl.BlockSpec((WIN, 1024), lambda n: (cur(n), P_Q // 1024)),
                  kvspec(T_K // LANE, False), kvspec(T_K // LANE, True),
                  kvspec(T_V // LANE, False), kvspec(T_V // LANE, True),
                  pl.BlockSpec((None,) + band_shape, lambda n: (jnp.minimum(n, 1), 0, 0, 0)),
                  pl.BlockSpec((WIN, 1024), lambda n: (cur(n), 0)),
                  pl.BlockSpec((WIN, 1024), lambda n: (cur(n), 0))],
        out_specs=(pl.BlockSpec((WIN, 1024), lambda n: (cur(n), 0)),
                   pl.BlockSpec((WIN, LANE), lambda n: (jnp.maximum(n - 1, 0), 0)),
                   pl.BlockSpec((WIN, LANE), lambda n: (jnp.maximum(n - 1, 0), 0)),
                   pl.BlockSpec(band_shape, lambda n: (0, 0, 0))),
        scratch_shapes=[pltpu.VMEM((WIN, LANE), F32), pltpu.VMEM((WIN, LANE), F32),
                        pltpu.VMEM(band_shape, F32), pltpu.VMEM(band_shape, _MXU), pltpu.VMEM(band_shape, _MXU)],
        compiler_params=_cp(("arbitrary",)),
    )(proj, tail, tail, tail, tail, bias, y_attn, d_out)


def _merge_fwd(bs, ba, proj, bg8, name):
    t = bs.shape[0]
    tm = _tm_rows(t)

    def body(bs_ref, ba_ref, gs_ref, ga_ref, bgs_ref, bga_ref, o_ref):
        g_s = _sigmoid(gs_ref[...] + bgs_ref[0:1, :])
        g_a = _sigmoid(ga_ref[...] + bga_ref[0:1, :])
        o_ref[...] = (g_s * bs_ref[...] + g_a * ba_ref[...]).astype(_ACT)

    row = lambda col: pl.BlockSpec((tm, 1024), lambda i: (i, col))
    return pl.pallas_call(
        body, name=name, out_shape=jax.ShapeDtypeStruct((t, D_MODEL), _ACT), grid=(t // tm,),
        in_specs=[row(0), row(0), row(P_G // 1024), row(P_G // 1024 + 1),
                  pl.BlockSpec((SUB, 1024), lambda i: (0, 0)), pl.BlockSpec((SUB, 1024), lambda i: (0, 1))],
        out_specs=row(0), compiler_params=_cp(("parallel",)),
    )(bs, ba, proj, proj, bg8, bg8)


def _merge_bwd(d_merged, bs, ba, proj, bg8, name):
    t = bs.shape[0]
    tm = _tm_rows(t)

    def body(dm_ref, bs_ref, ba_ref, gs_ref, ga_ref, bgs_ref, bga_ref, dbs_ref, dba_ref, dg_ref, acc_ref):
        @pl.when(pl.program_id(0) == 0)
        def _():
            acc_ref[...] = jnp.zeros_like(acc_ref)

        dm = dm_ref[...].astype(F32)
        g_s = _sigmoid(gs_ref[...] + bgs_ref[0:1, :])
        g_a = _sigmoid(ga_ref[...] + bga_ref[0:1, :])
        dbs_ref[...] = (dm * g_s).astype(_ACT)
        dba_ref[...] = (dm * g_a).astype(_ACT)
        dgs = dm * bs_ref[...].astype(F32) * g_s * (1.0 - g_s)
        dga = dm * ba_ref[...].astype(F32) * g_a * (1.0 - g_a)
        dg_ref[:, 0:1024] = dgs.astype(_ACT)
        dg_ref[:, 1024:2048] = dga.astype(_ACT)
        acc_ref[0:1, 0:1024] += jnp.sum(dgs, axis=0, keepdims=True)
        acc_ref[0:1, 1024:2048] += jnp.sum(dga, axis=0, keepdims=True)

    row = lambda col: pl.BlockSpec((tm, 1024), lambda i: (i, col))
    return pl.pallas_call(
        body, name=name,
        out_shape=(jax.ShapeDtypeStruct((t, D_MODEL), _ACT), jax.ShapeDtypeStruct((t, D_MODEL), _ACT),
                   jax.ShapeDtypeStruct((t, 2048), _ACT), jax.ShapeDtypeStruct((SUB, 2048), F32)),
        grid=(t // tm,),
        in_specs=[row(0), row(0), row(0), row(P_G // 1024), row(P_G // 1024 + 1),
                  pl.BlockSpec((SUB, 1024), lambda i: (0, 0)), pl.BlockSpec((SUB, 1024), lambda i: (0, 1))],
        out_specs=(row(0), row(0), pl.BlockSpec((tm, 2048), lambda i: (i, 0)),
                   pl.BlockSpec((SUB, 2048), lambda i: (0, 0))),
        compiler_params=_cp(("arbitrary",)),
    )(d_merged, bs, ba, proj, proj, bg8, bg8)


def _ln_stats(r):
    mu = jnp.mean(r, axis=-1, keepdims=True)
    xc = r - mu
    var = jnp.mean(xc * xc, axis=-1, keepdims=True)
    rstd = lax.rsqrt(var + LN_EPS)
    return xc * rstd, rstd


def _ln_bwd(dxhat, xhat, rstd):
    return rstd * (dxhat - jnp.mean(dxhat, axis=-1, keepdims=True)
                   - xhat * jnp.mean(dxhat * xhat, axis=-1, keepdims=True))


def _ln1_fwd(x, mix, g8, b8, name):
    t = x.shape[0]
    tm = _tm_rows(t)

    def body(x_ref, m_ref, g_ref, b_ref, xh_ref, h_ref, rs_ref):
        xhat, rstd = _ln_stats(ALPHA * x_ref[...] + m_ref[...])
        xh_ref[...] = xhat
        h_ref[...] = (xhat * g_ref[0:1, :] + b_ref[0:1, :]).astype(_ACT)
        rs_ref[...] = rstd

    row = pl.BlockSpec((tm, D_MODEL), lambda i: (i, 0))
    par = pl.BlockSpec((SUB, D_MODEL), lambda i: (0, 0))
    return pl.pallas_call(
        body, name=name,
        out_shape=(jax.ShapeDtypeStruct((t, D_MODEL), F32), jax.ShapeDtypeStruct((t, D_MODEL), _ACT),
                   jax.ShapeDtypeStruct((t, 1), F32)),
        grid=(t // tm,), in_specs=[row, row, par, par],
        out_specs=(row, row, pl.BlockSpec((tm, 1), lambda i: (i, 0))),
        compiler_params=_cp(("parallel",)),
    )(x, mix, g8, b8)


def _ln2_loss(xhat1, ffn, target, g1_8, b1_8, g2_8, b2_8, name):
    t = xhat1.shape[0]
    tm = _tm_rows(t)

    def body(xh_ref, f_ref, t_ref, g1_ref, b1_ref, g2_ref, b2_ref, d_ref, db_ref, acc_ref):
        @pl.when(pl.program_id(0) == 0)
        def _():
            acc_ref[...] = jnp.zeros_like(acc_ref)

        h1 = xh_ref[...] * g1_ref[0:1, :] + b1_ref[0:1, :]
        xhat, rstd = _ln_stats(ALPHA * h1 + f_ref[...])
        diff = xhat * g2_ref[0:1, :] + b2_ref[0:1, :] - t_ref[...]
        dy = diff * (1.0 / D_MODEL)
        acc_ref[0:1, :] += jnp.sum(dy * xhat, axis=0, keepdims=True)
        acc_ref[1:2, :] += jnp.sum(dy, axis=0, keepdims=True)
        acc_ref[2:3, :] += jnp.sum(diff * diff, axis=0, keepdims=True)
        d = _ln_bwd(dy * g2_ref[0:1, :], xhat, rstd)
        d_ref[...] = d
        db_ref[...] = d.astype(_ACT)

    row = pl.BlockSpec((tm, D_MODEL), lambda i: (i, 0))
    par = pl.BlockSpec((SUB, D_MODEL), lambda i: (0, 0))
    return pl.pallas_call(
        body, name=name,
        out_shape=(jax.ShapeDtypeStruct((t, D_MODEL), F32), jax.ShapeDtypeStruct((t, D_MODEL), _ACT),
                   jax.ShapeDtypeStruct((SUB, D_MODEL), F32)),
        grid=(t // tm,), in_specs=[row, row, row, par, par, par, par],
        out_specs=(row, row, par), compiler_params=_cp(("arbitrary",)),
    )(xhat1, ffn, target, g1_8, b1_8, g2_8, b2_8)


def _ln1_bwd(d_r2, d_h1_ffn, xhat1, rstd1, g1_8, name):
    t = xhat1.shape[0]
    tm = _tm_rows(t)

    def body(d2_ref, df_ref, xh_ref, rs_ref, g_ref, d_ref, db_ref, acc_ref):
        @pl.when(pl.program_id(0) == 0)
        def _():
            acc_ref[...] = jnp.zeros_like(acc_ref)

        dh = ALPHA * d2_ref[...] + df_ref[...]
        xhat = xh_ref[...]
        acc_ref[0:1, :] += jnp.sum(dh * xhat, axis=0, keepdims=True)
        acc_ref[1:2, :] += jnp.sum(dh, axis=0, keepdims=True)
        d = _ln_bwd(dh * g_ref[0:1, :], xhat, rs_ref[...])
        d_ref[...] = d
        db_ref[...] = d.astype(_ACT)

    row = pl.BlockSpec((tm, D_MODEL), lambda i: (i, 0))
    par = pl.BlockSpec((SUB, D_MODEL), lambda i: (0, 0))
    return pl.pallas_call(
        body, name=name,
        out_shape=(jax.ShapeDtypeStruct((t, D_MODEL), F32), jax.ShapeDtypeStruct((t, D_MODEL), _ACT),
                   jax.ShapeDtypeStruct((SUB, D_MODEL), F32)),
        grid=(t // tm,), in_specs=[row, row, row, pl.BlockSpec((tm, 1), lambda i: (i, 0)), par],
        out_specs=(row, row, par), compiler_params=_cp(("arbitrary",)),
    )(d_r2, d_h1_ffn, xhat1, rstd1, g1_8)


def _ffn_tm(t):
    return min(128, t)


def _ffn_act_fwd(u0, cw8, cb8, name):
    t = u0.shape[0]
    tm = _ffn_tm(t)

    def body(g_ref, gp_ref, v_ref, vp_ref, wg_ref, wv_ref, bg_ref, bv_ref, o_ref, u_ref):
        i = pl.program_id(0)
        gprev = jnp.where(i > 0, gp_ref[SUB:HALO, :].astype(F32), 0.0)
        vprev = jnp.where(i > 0, vp_ref[SUB:HALO, :].astype(F32), 0.0)
        gate = _conv_pre(g_ref[...].astype(F32), gprev, wg_ref, bg_ref[0:1, :], FFN_K)
        val = _conv_pre(v_ref[...].astype(F32), vprev, wv_ref, bv_ref[0:1, :], FFN_K)
        o_ref[...] = (gate * _sigmoid(gate) * val).astype(_ACT)
        u_ref[:, 0:D_FF] = gate.astype(_ACT)
        u_ref[:, D_FF:2 * D_FF] = val.astype(_ACT)

    cur = lambda col: pl.BlockSpec((tm, D_FF), lambda i: (i, col))
    prv = lambda col: _prev_halo(tm, D_FF, lambda i: (i, col))
    par = lambda col: pl.BlockSpec((SUB, D_FF), lambda i: (0, col))
    return pl.pallas_call(
        body, name=name,
        out_shape=(jax.ShapeDtypeStruct((t, D_FF), _ACT), jax.ShapeDtypeStruct((t, 2 * D_FF), _ACT)),
        grid=(t // tm,),
        in_specs=[cur(0), prv(0), cur(1), prv(1), par(0), par(1), par(0), par(1)],
        out_specs=(pl.BlockSpec((tm, D_FF), lambda i: (i, 0)), pl.BlockSpec((tm, 2 * D_FF), lambda i: (i, 0))),
        compiler_params=_cp(("parallel",)),
    )(u0, u0, u0, u0, cw8, cw8, cb8, cb8)


def _ffn_act_bwd(u0, u, cw8, d_a, name):
    t = u0.shape[0]
    tm = _ffn_tm(t)
    nt = t // tm

    def body(g0_ref, v0_ref, g_ref, gn_ref, v_ref, vn_ref, wg_ref, wv_ref, da_ref, dan_ref, du_ref, acc_ref):
        i = pl.program_id(0)

        @pl.when(i == 0)
        def _():
            acc_ref[...] = jnp.zeros_like(acc_ref)

        def grads(gate, val, da):
            return da * val * _silu_grad(gate), da * gate * _sigmoid(gate)

        dgate, dval = grads(g_ref[...].astype(F32), v_ref[...].astype(F32), da_ref[...].astype(F32))
        dgate_n, dval_n = grads(gn_ref[0:SUB, :].astype(F32), vn_ref[0:SUB, :].astype(F32),
                                dan_ref[0:SUB, :].astype(F32))
        last = i == nt - 1
        du_ref[:, 0:D_FF] = _conv_grads(dgate, jnp.where(last, 0.0, dgate_n), g0_ref[...].astype(F32), wg_ref,
                                        acc_ref, FFN_K, slice(0, D_FF)).astype(_ACT)
        du_ref[:, D_FF:2 * D_FF] = _conv_grads(dval, jnp.where(last, 0.0, dval_n), v0_ref[...].astype(F32), wv_ref,
                                               acc_ref, FFN_K, slice(D_FF, 2 * D_FF)).astype(_ACT)

    cur = lambda col: pl.BlockSpec((tm, D_FF), lambda i: (i, col))
    nxt = lambda col: _next_halo(tm, t, D_FF, lambda i: (i, col))
    par = lambda col: pl.BlockSpec((SUB, D_FF), lambda i: (0, col))
    return pl.pallas_call(
        body, name=name,
        out_shape=(jax.ShapeDtypeStruct((t, 2 * D_FF), _ACT), jax.ShapeDtypeStruct((SUB, 2 * D_FF), F32)),
        grid=(nt,),
        in_specs=[cur(0), cur(1), cur(0), nxt(0), cur(1), nxt(1), par(0), par(1), cur(0), nxt(0)],
        out_specs=(pl.BlockSpec((tm, 2 * D_FF), lambda i: (i, 0)),
                   pl.BlockSpec((SUB, 2 * D_FF), lambda i: (0, 0))),
        compiler_params=_cp(("arbitrary",)),
    )(u0, u0, u, u, u, u, cw8, cw8, d_a, d_a)


_REST = ("w_branch_ssm", "w_branch_attn", "w_mix_out", "w_up", "w_down")


def _mm_side(*args, side, **kw):
    if side is None:
        return _mm(*args, **kw), []
    return _mm(*args, side=side, **kw)


def _local_step(x, target, wts, ex):
    t = x.shape[0]
    wp = wts["wp"]
    scw = wts["ssm_conv_w"]
    scb = wts["ssm_conv_b"]
    fcw8 = _rows8(wts["ffn_conv_w"])
    fcb8 = _rows8(wts["ffn_conv_b"])
    pad_lane = lambda p: jnp.concatenate([p.astype(F32), jnp.zeros((1, LANE - p.shape[1]), F32)], axis=1)
    dtb8 = _rows8(pad_lane(wts["ssm_dt_bias"]))
    alog8 = _rows8(pad_lane(wts["ssm_a_log"]))
    dsk8 = _rows8(pad_lane(wts["ssm_d"]))
    bias_table = jnp.concatenate([wts["rel_bias"].T.astype(F32), wts["attn_sinks"].T.astype(F32),
                                  jnp.zeros((A_HEADS, BIAS_ROWS - REL_BUCKETS - 1), F32)], axis=1)
    nw8 = _rows8(wts["ssm_norm_w"])
    bg8 = _rows8(wts["b_gate"])
    g1_8, b1_8, g2_8, b2_8 = (_rows8(wts[k]) for k in ("ln1_g", "ln1_b", "ln2_g", "ln2_b"))
    xs_w8, xs_b8 = _rows8(scw[:, :D_INNER]), _rows8(scb[:, :D_INNER])
    bc_w8, bc_b8 = _rows8(scw[:, D_INNER:]), _rows8(scb[:, D_INNER:])

    x_bf = x.astype(_ACT)
    proj, stacks = _mm_side(x_bf, wp[:, :P_MAIN], "mm_in", out_dtype=_ACT, side=ex.gather_rest())
    wts = dict(wts, **ex.rest_weights(stacks))
    w_bs, w_ba, w_mix, w_up, w_dn = (wts[k] for k in _REST)
    tail = _mm(x_bf, wp[:, P_MAIN:], "mm_in_tail")
    xs_c, xs_pre = _conv_silu_fwd(proj, P_XS // _TC, D_INNER // _TC, xs_w8, xs_b8, "conv_xs_fwd")
    bc_c, bc_pre = _conv_silu_fwd(proj, P_BC // _TC, 1024 // _TC, bc_w8, bc_b8, "conv_bc_fwd")
    y_ssd, y_ssm, hprev = _ssd_fwd(xs_c, bc_c, proj, tail, dtb8, alog8, dsk8, nw8, "ssd_fwd")
    bias = _bias_expand(bias_table, "bias_expand").reshape(2, A_HEADS, WIN, 2 * WIN)
    y_attn = _attn_fwd(proj, tail, bias, "attn_fwd")
    bs = _mm(y_ssm, w_bs, "mm_bs", out_dtype=_ACT)
    ba = _mm(y_attn, w_ba, "mm_ba", out_dtype=_ACT)
    merged = _merge_fwd(bs, ba, proj, bg8, "merge_fwd")
    mix = _mm(merged, w_mix, "mm_mix", out_dtype=_ACT)
    xhat1, h1_bf, rstd1 = _ln1_fwd(x, mix, g1_8, b1_8, "ln1_fwd")
    u0 = _mm(h1_bf, w_up, "mm_up", out_dtype=_ACT)
    act, u_conv = _ffn_act_fwd(u0, fcw8, fcb8, "ffn_act_fwd")
    ffn = _mm(act, w_dn, "mm_down", out_dtype=_ACT)
    d_r2, d_r2_bf, acc_ln2 = _ln2_loss(xhat1, ffn, target, g1_8, b1_8, g2_8, b2_8, "ln2_loss")
    d_w_dn = _mm(act, d_r2_bf, "mm_dw_down", trans_a=True)
    d_act = _mm(d_r2_bf, w_dn.T, "mm_d_act", out_dtype=_ACT)
    d_u0, acc_ffn = _ffn_act_bwd(u0, u_conv, fcw8, d_act, "ffn_act_bwd")
    d_w_up = _mm(h1_bf, d_u0, "mm_dw_up", trans_a=True)
    d_h1_ffn = _mm(d_u0, w_up.T, "mm_d_h1", out_dtype=_ACT)
    d_r1, d_r1_bf, acc_ln1 = _ln1_bwd(d_r2, d_h1_ffn, xhat1, rstd1, g1_8, "ln1_bwd")
    d_w_mix = _mm(merged, d_r1_bf, "mm_dw_mix", trans_a=True)
    d_merged = _mm(d_r1_bf, w_mix.T, "mm_d_merged", out_dtype=_ACT)
    d_bs, d_ba, d_gates, acc_bg = _merge_bwd(d_merged, bs, ba, proj, bg8, "merge_bwd")
    d_w_bs = _mm(y_ssm, d_bs, "mm_dw_bs", trans_a=True)
    d_w_ba = _mm(y_attn, d_ba, "mm_dw_ba", trans_a=True)
    d_y_ssm = _mm(d_bs, w_bs.T, "mm_d_yssm", out_dtype=_ACT)
    d_y_attn = _mm(d_ba, w_ba.T, "mm_d_yattn", out_dtype=_ACT)
    d_q, d_k, d_v, d_bias = _attn_bwd(proj, tail, bias, y_attn, d_y_attn, "attn_bwd")
    d_table = _bias_reduce(d_bias.reshape(A_HEADS, WIN * 2 * WIN), "bias_reduce")
    d_xs_c, d_bc_c, d_z, d_dt, acc_ssd, acc_nw = _ssd_bwd(
        d_y_ssm, y_ssd, xs_c, bc_c, proj, tail, hprev, dtb8, alog8, dsk8, nw8, "ssd_bwd")
    d_xs, acc_xs = _conv_silu_bwd(proj, P_XS // _TC, D_INNER // _TC, xs_pre, xs_w8, d_xs_c, "conv_xs_bwd")
    d_bc, acc_bc = _conv_silu_bwd(proj, P_BC // _TC, 1024 // _TC, bc_pre, bc_w8, d_bc_c, "conv_bc_bwd")
    d_proj = jnp.concatenate([d_z, d_xs, d_gates, d_q, d_bc, d_k, d_v, d_dt,
                              jnp.zeros((t, P_W - P_DT - LANE), _ACT)], axis=1)
    grads = {"w_branch_ssm": d_w_bs, "w_branch_attn": d_w_ba, "w_mix_out": d_w_mix, "w_up": d_w_up, "w_down": d_w_dn}
    d_wp, landed_rest = _mm_side(x_bf, d_proj, "mm_dw_in", trans_a=True, side=ex.reduce_job(grads))
    d_x, landed_in = _mm_side(d_proj, wp.T, "mm_d_x", res=d_r1, res_scale=ALPHA, side=ex.reduce_job({"wp": d_wp}))
    grads.update({
        "wp": d_wp,
        "ssm_conv_w": jnp.concatenate([acc_xs[0:SSM_K], acc_bc[0:SSM_K]], axis=1),
        "ffn_conv_w": acc_ffn[0:FFN_K],
    })
    small = {
        "rel_bias": d_table[:, 0:REL_BUCKETS].T,
        "b_gate": acc_bg[0:1],
        "ssm_conv_b": jnp.concatenate([acc_xs[SSM_K:SSM_K + 1], acc_bc[SSM_K:SSM_K + 1]], axis=1),
        "ssm_dt_bias": acc_ssd[0:1, 0:N_HEADS], "ssm_a_log": acc_ssd[1:2, 0:N_HEADS], "ssm_d": acc_ssd[2:3, 0:N_HEADS],
        "ssm_norm_w": acc_nw[0:1],
        "attn_sinks": d_table[:, REL_BUCKETS:REL_BUCKETS + 1].T,
        "ln1_g": acc_ln1[0:1], "ln1_b": acc_ln1[1:2],
        "ffn_conv_b": acc_ffn[FFN_K:FFN_K + 1],
        "ln2_g": acc_ln2[0:1], "ln2_b": acc_ln2[1:2],
        "loss_lanes": acc_ln2[2:3],
    }
    return d_x, grads, small, landed_rest + landed_in


_MATS = (("w_in", (1024, 2120), 1), ("w_branch_ssm", (512, 1024), 0), ("w_branch_attn", (256, 1024), 0),
         ("w_mix_out", (256, 1024), 0), ("w_up", (1024, 1408), 1), ("w_down", (704, 1024), 0))
_CONVS = (("ssm_conv_w", (4, 768)), ("ffn_conv_w", (3, 1408)))
_CONV_ROWS = 64

_SMALL = (("rel_bias", (32, 16)), ("b_gate", (1, 2048)), ("ssm_conv_b", (1, 3072)), ("ssm_dt_bias", (1, 32)),
          ("ssm_a_log", (1, 32)), ("ssm_d", (1, 32)), ("ssm_norm_w", (1, 2048)), ("attn_sinks", (1, 16)),
          ("ln1_g", (1, 1024)), ("ln1_b", (1, 1024)), ("ffn_conv_b", (1, 5632)), ("ln2_g", (1, 1024)),
          ("ln2_b", (1, 1024)), ("g_ssm_conv_w", (4, 3072)), ("g_ffn_conv_w", (3, 5632)), ("loss_lanes", (1, 1024)))


def _small_rows(shape):
    rows = -(-(shape[0] * shape[1]) // LANE)
    return -(-rows // SUB) * SUB


def _as_rows(a, rows, dtype):
    flat = a.reshape(-1).astype(dtype)
    flat = jnp.concatenate([flat, jnp.zeros((rows * LANE - flat.shape[0],), dtype)])
    return flat.reshape(rows, LANE)


def _pack_small(parts):
    blocks = [_as_rows(parts[n], _small_rows(s), F32) if n in parts else jnp.zeros((_small_rows(s), LANE), F32)
              for n, s in _SMALL]
    return jnp.concatenate(blocks, axis=0)


def _unpack_small(packed):
    out, at = {}, 0
    for n, s in _SMALL:
        rows = _small_rows(s)
        out[n] = packed[at:at + rows].reshape(-1)[:s[0] * s[1]].reshape(s)
        at += rows
    return out


def _to_stack(full, shape, axis):
    if axis == 0:
        return full.reshape((N_CHIPS,) + shape)
    return jnp.transpose(full.reshape(shape[0], N_CHIPS, shape[1]), (1, 0, 2))


def _from_stack(stack, axis):
    n, r, c = stack.shape
    if axis == 0:
        return stack.reshape(n * r, c)
    return jnp.transpose(stack, (1, 0, 2)).reshape(r, n * c)


_IN_SHARD = IN_COLS // N_CHIPS


def _cols_of_stack(stack, o, w):
    parts = []
    while w > 0:
        j, a = divmod(o, _IN_SHARD)
        n = min(w, _IN_SHARD - a)
        parts.append(stack[j][:, a:a + n])
        o, w = o + n, w - n
    return parts


def _pack_w_in_stack(stack):
    cols, at = [], 0
    for o, w, pk in sorted(_PIECES, key=lambda p: p[2]):
        if pk > at:
            cols.append(jnp.zeros((stack.shape[1], pk - at), stack.dtype))
        cols += _cols_of_stack(stack, o, w)
        at = pk + w
    cols.append(jnp.zeros((stack.shape[1], P_W - at), stack.dtype))
    return jnp.concatenate(cols, axis=1)


def _unpack_w_in_stack(wp):
    slabs = []
    for j in range(N_CHIPS):
        lo, hi = j * _IN_SHARD, (j + 1) * _IN_SHARD
        cols = []
        for o, w, pk in sorted(_PIECES):
            a, b = max(o, lo), min(o + w, hi)
            if a < b:
                cols.append(wp[:, pk + a - o:pk + b - o])
        slabs.append(jnp.concatenate(cols, axis=1))
    return jnp.stack(slabs)


_MESH = pl.DeviceIdType.MESH
_HBM = pl.BlockSpec(memory_space=pltpu.HBM)


def _position():
    return lax.axis_index("x"), lax.axis_index("y"), lax.axis_index("c")


def _other_chips(x, y):
    return ((1 - x, y), (x, 1 - y), (1 - x, 1 - y))


def _remote(src, dst, send_sem, recv_sem, to):
    return pltpu.make_async_remote_copy(src_ref=src, dst_ref=dst, send_sem=send_sem, recv_sem=recv_sem,
                                        device_id=to, device_id_type=_MESH)


def _run_job(job, name):
    n_in, n_out = len(job.inputs), len(job.out_shape)

    def body(*refs):
        parts = (refs[:n_in], refs[n_in:n_in + n_out], refs[n_in + n_out:])
        job.start(*parts)
        job.finish(*parts)

    return pl.pallas_call(body, name=name, out_shape=list(job.out_shape), in_specs=[_HBM] * n_in,
                          out_specs=[_HBM] * n_out, scratch_shapes=list(job.sems))(*job.inputs)


def _gather_job(shards, copy_own=True):
    n = len(shards)

    def plan(s_refs, o_refs, sems):
        send_sems, recv_sems, local_sems = sems
        x, y, c = _position()
        me = 2 * x + y
        sib = (x, y, 1 - c)
        chips = _other_chips(x, y)

        def copy(m, k, chip_idx, half, to, src=None):
            dst = o_refs[m].at[chip_idx, half]
            return _remote(dst if src is None else src, dst, send_sems.at[6 * m + k], recv_sems.at[6 * m + k], to)

        local = [pltpu.make_async_copy(s_refs[m], o_refs[m].at[me], local_sems.at[m]) for m in range(n)]
        local = local if copy_own else []
        first = [copy(m, i, me, c, (cx, cy, c), src=s_refs[m].at[c])
                 for i, (cx, cy) in enumerate(chips) for m in range(n)]
        return c, sib, chips, copy, local, first

    def start(s_refs, o_refs, sems):
        _, _, _, _, local, first = plan(s_refs, o_refs, sems)
        for cp in local + first:
            cp.start()

    def finish(s_refs, o_refs, sems):
        c, sib, chips, copy, local, first = plan(s_refs, o_refs, sems)
        passed = []
        for i, (cx, cy) in enumerate(chips):
            for m in range(n):
                copy(m, i, 2 * cx + cy, c, sib).wait_recv()
                passed.append(copy(m, 3 + i, 2 * cx + cy, c, sib))
                passed[-1].start()
        for i, (cx, cy) in enumerate(chips):
            for m in range(n):
                copy(m, 3 + i, 2 * cx + cy, 1 - c, sib).wait_recv()
        for cp in first + passed:
            cp.wait_send()
        for cp in local:
            cp.wait()

    return _SideJob(
        inputs=list(shards), out_shape=[jax.ShapeDtypeStruct((N_CHIPS,) + s.shape, s.dtype) for s in shards],
        sems=[pltpu.SemaphoreType.DMA((6 * n,)), pltpu.SemaphoreType.DMA((6 * n,)), pltpu.SemaphoreType.DMA((n,))],
        start=start, finish=finish)


def _swap_halves(gs, name):
    n = len(gs)

    def body(*refs):
        g_refs, o_refs = refs[:n], refs[n:2 * n]
        send_sems, recv_sems = refs[2 * n:]
        x, y, c = _position()
        cps = [_remote(g_refs[m].at[j, 1 - c], o_refs[m].at[j], send_sems.at[N_CHIPS * m + j],
                       recv_sems.at[N_CHIPS * m + j], (x, y, 1 - c)) for m in range(n) for j in range(N_CHIPS)]
        for cp in cps:
            cp.start()
        for cp in cps:
            cp.wait()

    return pl.pallas_call(
        body, name=name,
        out_shape=[jax.ShapeDtypeStruct((N_CHIPS,) + g.shape[2:], g.dtype) for g in gs],
        in_specs=[_HBM] * n, out_specs=[_HBM] * n,
        scratch_shapes=[pltpu.SemaphoreType.DMA((N_CHIPS * n,)), pltpu.SemaphoreType.DMA((N_CHIPS * n,))],
    )(*gs)


def _scatter_job(ps):
    n = len(ps)

    def copies(p_refs, o_refs, sems):
        send_sems, recv_sems = sems
        x, y, c = _position()
        return [_remote(p_refs[m].at[2 * cx + cy], o_refs[m].at[i], send_sems.at[3 * m + i], recv_sems.at[3 * m + i],
                        (cx, cy, c)) for i, (cx, cy) in enumerate(_other_chips(x, y)) for m in range(n)]

    def start(*parts):
        for cp in copies(*parts):
            cp.start()

    def finish(*parts):
        for cp in copies(*parts):
            cp.wait()

    return _SideJob(
        inputs=list(ps), out_shape=[jax.ShapeDtypeStruct((N_CHIPS - 1,) + p.shape[1:], p.dtype) for p in ps],
        sems=[pltpu.SemaphoreType.DMA((3 * n,)), pltpu.SemaphoreType.DMA((3 * n,))], start=start, finish=finish)


def _join_halves(fulls):
    n = len(fulls)

    def body(*refs):
        o_refs = refs[n:2 * n]
        send_sems, recv_sems = refs[2 * n:]
        x, y, c = _position()
        cps = [_remote(o_refs[m].at[c], o_refs[m].at[c], send_sems.at[m], recv_sems.at[m], (x, y, 1 - c))
               for m in range(n)]
        for cp in cps:
            cp.start()
        for cp in cps:
            cp.wait()

    return pl.pallas_call(
        body, name="join_halves",
        out_shape=[jax.ShapeDtypeStruct(f.shape, f.dtype) for f in fulls],
        in_specs=[_HBM] * n, out_specs=[_HBM] * n, input_output_aliases={m: m for m in range(n)},
        scratch_shapes=[pltpu.SemaphoreType.DMA((n,)), pltpu.SemaphoreType.DMA((n,))],
    )(*fulls)


def _allgather_small(mine, name):
    m_per, n = mine.shape

    def body(x_ref, out_ref, send_sems, recv_sems, local_sem):
        x, y, c = _position()
        me, sibling = (x, y, c), (x, y, 1 - c)
        chips = _other_chips(x, y)

        def rows(px, py, pc):
            return out_ref.at[pl.ds((4 * px + 2 * py + pc) * m_per, m_per), :]

        def copy(k, block, to, src=None):
            return pltpu.make_async_remote_copy(src_ref=rows(*block) if src is None else src, dst_ref=rows(*block),
                                                send_sem=send_sems.at[k], recv_sem=recv_sems.at[k],
                                                device_id=to, device_id_type=_MESH)

        own = pltpu.make_async_copy(x_ref, rows(*me), local_sem)
        own.start()
        first = [copy(0, me, sibling, src=x_ref)]
        first += [copy(1 + j, me, (*chip, c), src=x_ref) for j, chip in enumerate(chips)]
        for cp in first:
            cp.start()
        passed = [copy(4 + j, (*chip, c), sibling) for j, chip in enumerate(chips)]
        for j, chip in enumerate(chips):
            copy(1 + j, (*chip, c), me).wait_recv()
            passed[j].start()
        copy(0, sibling, me).wait_recv()
        for j, chip in enumerate(chips):
            copy(4 + j, (*chip, 1 - c), me).wait_recv()
        for cp in first + passed:
            cp.wait_send()
        own.wait()

    return pl.pallas_call(
        body, name=name, out_shape=jax.ShapeDtypeStruct((N_DEV * m_per, n), mine.dtype),
        in_specs=[pl.BlockSpec(memory_space=pltpu.VMEM)], out_specs=pl.BlockSpec(memory_space=pltpu.VMEM),
        scratch_shapes=[pltpu.SemaphoreType.DMA((7,)), pltpu.SemaphoreType.DMA((7,)), pltpu.SemaphoreType.DMA],
    )(mine)


_ADD_BLOCK_BYTES = 3 << 20


def _add_rows(hr, cols):
    if hr * cols * 4 <= _ADD_BLOCK_BYTES:
        return hr
    return _pick(hr, (256, 128, 64, 32, 16))


def _add_own_half(g, recv, c_idx, name):
    nseg, _, hr, cols = g.shape
    tr = _add_rows(hr, cols)

    def body(c_ref, g_ref, r_ref, o_ref, ob_ref):
        s = g_ref[...] + r_ref[...]
        o_ref[...] = s
        ob_ref[...] = s.astype(jnp.bfloat16)

    blk = pl.BlockSpec((None, tr, cols), lambda j, i, c_ref: (j, i, 0))
    return pl.pallas_call(
        body, name=name,
        out_shape=(jax.ShapeDtypeStruct((nseg, hr, cols), F32), jax.ShapeDtypeStruct((nseg, hr, cols), jnp.bfloat16)),
        grid_spec=pltpu.PrefetchScalarGridSpec(
            num_scalar_prefetch=1, grid=(nseg, hr // tr),
            in_specs=[pl.BlockSpec((None, None, tr, cols), lambda j, i, c_ref: (j, c_ref[0], i, 0)), blk],
            out_specs=(blk, blk)),
        compiler_params=_cp(("parallel", "parallel")),
    )(c_idx, g, recv)


def _add_chips(p, recv, chip_idx, c_idx, name):
    _, hr, cols = p.shape
    tr = _add_rows(hr, cols)

    def body(j_ref, c_ref, p_ref, r_ref, o_ref):
        o_ref[...] = ((p_ref[...] + r_ref[0].astype(F32)) + r_ref[1].astype(F32)) + r_ref[2].astype(F32)

    return pl.pallas_call(
        body, name=name, out_shape=jax.ShapeDtypeStruct((2, hr, cols), F32),
        grid_spec=pltpu.PrefetchScalarGridSpec(
            num_scalar_prefetch=2, grid=(hr // tr,),
            in_specs=[pl.BlockSpec((None, tr, cols), lambda i, j_ref, c_ref: (j_ref[0], i, 0)),
                      pl.BlockSpec((N_CHIPS - 1, tr, cols), lambda i, j_ref, c_ref: (0, i, 0))],
            out_specs=pl.BlockSpec((None, tr, cols), lambda i, j_ref, c_ref: (c_ref[0], i, 0))),
        compiler_params=_cp(("parallel",)),
    )(chip_idx, c_idx, p, recv)


def _adam_math(w, g, m, v):
    m = ADAM_B1 * m + (1.0 - ADAM_B1) * g
    v = ADAM_B2 * v + (1.0 - ADAM_B2) * (g * g)
    m_hat = m / (1.0 - ADAM_B1 ** ADAM_STEP)
    v_hat = v / (1.0 - ADAM_B2 ** ADAM_STEP)
    delta = -ADAM_LR * (m_hat / (jnp.sqrt(v_hat) + ADAM_EPS) + ADAM_WD * w)
    return delta, m, v


def _adam_big(w, g, m, v, name):
    rows, cols = w.shape
    tr = _pick(rows, (256, 128, 64, 32, 16, 8)) if rows % SUB == 0 else rows

    def body(w_ref, g_ref, m_ref, v_ref, d_ref, mo_ref, vo_ref):
        d_ref[...], mo_ref[...], vo_ref[...] = _adam_math(w_ref[...], g_ref[...], m_ref[...], v_ref[...])

    blk = pl.BlockSpec((tr, cols), lambda i: (i, 0))
    shp = jax.ShapeDtypeStruct((rows, cols), F32)
    return pl.pallas_call(
        body, name=name, out_shape=(shp, shp, shp), grid=(rows // tr,),
        in_specs=[blk, blk, blk, blk], out_specs=(blk, blk, blk), compiler_params=_cp(("parallel",)),
    )(w, g, m, v)


def _adam_small(w, gathered, m, v):
    rows = w.shape[0]

    def body(w_ref, a_ref, m_ref, v_ref, g_ref, d_ref, mo_ref, vo_ref):
        g = a_ref[0:rows, :]
        for k in range(1, N_DEV):
            g = g + a_ref[k * rows:(k + 1) * rows, :]
        g_ref[...] = g
        d_ref[...], mo_ref[...], vo_ref[...] = _adam_math(w_ref[...], g, m_ref[...], v_ref[...])

    shp = jax.ShapeDtypeStruct((rows, LANE), F32)
    return pl.pallas_call(body, name="adam_small", out_shape=(shp, shp, shp, shp), compiler_params=_cp(None))(
        w, gathered, m, v)


_WEIGHTS = ("rel_bias", "w_in", "b_gate", "ssm_conv_w", "ssm_conv_b", "ssm_dt_bias", "ssm_a_log", "ssm_d",
            "ssm_norm_w", "attn_sinks", "w_branch_ssm", "w_branch_attn", "w_mix_out", "ln1_g", "ln1_b", "w_up",
            "ffn_conv_w", "ffn_conv_b", "w_down", "ln2_g", "ln2_b")
_REPLICATED = tuple(n for n, _ in _SMALL[:13])


class _Exchange:
    def __init__(self, w, chip, core):
        self.chip = chip
        self.c_idx = jnp.reshape(core, (1,)).astype(jnp.int32)
        self.chip_idx = jnp.reshape(chip, (1,)).astype(jnp.int32)
        self.shards = {n: w[n].astype(jnp.bfloat16).reshape(2, s[0] // 2, s[1]) for n, s, _ in _MATS}
        self.spec = {n: (s, ax) for n, s, ax in _MATS}
        self.sums = {}

    def _with_own(self, n, stack):
        shape = self.spec[n][0]
        slab = lax.broadcasted_iota(jnp.int32, (N_CHIPS, 1, 1), 0)
        return jnp.where(slab == self.chip, self.shards[n].reshape((1,) + shape), stack.reshape((N_CHIPS,) + shape))

    def w_in_packed(self):
        (stack,) = _run_job(_gather_job([self.shards["w_in"]], copy_own=False), "allgather_w_in")
        return _pack_w_in_stack(self._with_own("w_in", stack))

    def gather_rest(self):
        return _gather_job([self.shards[n] for n in _REST], copy_own=False)

    def rest_weights(self, stacks):
        return {n: _from_stack(self._with_own(n, st), self.spec[n][1]) for n, st in zip(_REST, stacks)}

    def reduce_job(self, grads):
        names, stacks = [], []
        for n, g in grads.items():
            name = "w_in" if n == "wp" else n
            s, ax = self.spec[name]
            st = _unpack_w_in_stack(g) if n == "wp" else _to_stack(g, s, ax)
            names.append(name)
            stacks.append(st.reshape(N_CHIPS, 2, s[0] // 2, s[1]))
        swapped = _swap_halves(stacks, "swap_" + names[0])
        halves = []
        for n, g, r in zip(names, stacks, swapped):
            self.sums[n], bf = _add_own_half(g, r, self.c_idx, "add_own_" + n)
            halves.append(bf)
        return _scatter_job(halves)

    def reduced(self, landed):
        names = list(self.sums)
        reds = [_add_chips(self.sums[n], r, self.chip_idx, self.c_idx, "add_chips_" + n)
                for n, r in zip(names, landed)]
        return {n: g.reshape(self.spec[n][0]) for n, g in zip(names, _join_halves(reds))}


def _step(x, target, w, m, v):
    xi, yi, ci = _position()
    chip = 2 * xi + yi
    ex = _Exchange(w, chip, ci)

    wts = {n: w[n] for n in _REPLICATED}
    wts["wp"] = ex.w_in_packed()
    taps = jnp.concatenate([w[n].astype(F32).reshape(-1) for n, _ in _CONVS])
    taps = _allgather_small(_as_rows(taps, _CONV_ROWS, F32), "allgather_taps")
    taps = taps.reshape(N_CHIPS, 2, _CONV_ROWS * LANE)[:, 0]
    at = 0
    for n, s in _CONVS:
        wts[n] = _from_stack(taps[:, at:at + s[0] * s[1]].reshape((N_CHIPS,) + s), 1)
        at += s[0] * s[1]

    d_x, grads, small, landed = _local_step(x, target, wts, ex)

    outs = {"grad": ex.reduced(landed), "delta": {}, "m": {}, "v": {}}

    small = dict(small, g_ssm_conv_w=grads["ssm_conv_w"], g_ffn_conv_w=grads["ffn_conv_w"])
    all_small = _allgather_small(_pack_small(small), "allgather_small")
    packs = [_pack_small({n: d[n] for n in _REPLICATED}) for d in (w, m, v)]
    g_s, d_s, m_s, v_s = (_unpack_small(a) for a in _adam_small(packs[0], all_small, packs[1], packs[2]))
    for kind, part in (("grad", g_s), ("delta", d_s), ("m", m_s), ("v", v_s)):
        outs[kind].update({n: part[n] for n in _REPLICATED})
    for n, s in _CONVS:
        outs["grad"][n] = lax.dynamic_slice_in_dim(g_s["g_" + n], chip * s[1], s[1], axis=1)
    for n in [n for n, _, _ in _MATS] + [n for n, _ in _CONVS]:
        outs["delta"][n], outs["m"][n], outs["v"][n] = _adam_big(
            w[n].astype(F32), outs["grad"][n], m[n].astype(F32), v[n].astype(F32), "adam_" + n)
    loss = (0.5 / D_MODEL) * jnp.sum(g_s["loss_lanes"])
    return loss, d_x, outs


def kernel(x, rel_bias, w_in, b_gate, ssm_conv_w, ssm_conv_b, ssm_dt_bias, ssm_a_log, ssm_d, ssm_norm_w, attn_sinks, w_branch_ssm, w_branch_attn, w_mix_out, ln1_g, ln1_b, w_up, ffn_conv_w, ffn_conv_b, w_down, ln2_g, ln2_b, loss_target, m_rel_bias, m_w_in, m_b_gate, m_ssm_conv_w, m_ssm_conv_b, m_ssm_dt_bias, m_ssm_a_log, m_ssm_d, m_ssm_norm_w, m_attn_sinks, m_w_branch_ssm, m_w_branch_attn, m_w_mix_out, m_ln1_g, m_ln1_b, m_w_up, m_ffn_conv_w, m_ffn_conv_b, m_w_down, m_ln2_g, m_ln2_b, v_rel_bias, v_w_in, v_b_gate, v_ssm_conv_w, v_ssm_conv_b, v_ssm_dt_bias, v_ssm_a_log, v_ssm_d, v_ssm_norm_w, v_attn_sinks, v_w_branch_ssm, v_w_branch_attn, v_w_mix_out, v_ln1_g, v_ln1_b, v_w_up, v_ffn_conv_w, v_ffn_conv_b, v_w_down, v_ln2_g, v_ln2_b):
    given = dict(locals())
    drop = lambda a, n: a if n == "rel_bias" or a.ndim == 2 else a[0]
    w = {n: drop(given[n], n) for n in _WEIGHTS}
    m = {n: drop(given["m_" + n], n) for n in _WEIGHTS}
    v = {n: drop(given["v_" + n], n) for n in _WEIGHTS}
    loss, d_x, outs = _step(x[0], loss_target[0], w, m, v)
    like = lambda a, n: a.reshape(given[n].shape)
    res = [loss, d_x[None]]
    for kind in ("grad", "delta", "m", "v"):
        res += [like(outs[kind][n], n) for n in _WEIGHTS]
    return tuple(res)
```

```python
import math
from typing import NamedTuple

import numpy as np
import jax
import jax.numpy as jnp
from jax import lax
from jax.experimental import pallas as pl
from jax.experimental.pallas import tpu as pltpu

F32 = jnp.float32
_ACT = jnp.bfloat16
_MXU = jnp.bfloat16

D_MODEL = 1024
D_INNER = 2048
N_HEADS = 32
HEAD_P = 64
N_GROUPS = 4
N_STATE = 128
CHUNK = 128
CONV_DIM = 3072
SSM_K = 4
A_HEADS = 16
A_DH = 64
WIN = 128
REL_BUCKETS = 32
BIAS_ROWS = 64
D_FF = 2816
FFN_K = 3
ALPHA = 2.0 ** 0.25
LN_EPS = 1e-5
RMS_EPS = 1e-5
IN_COLS = 8480
NEG = -1e30

ADAM_LR = 0.001
ADAM_B1 = 0.9
ADAM_B2 = 0.999
ADAM_EPS = 1e-08
ADAM_WD = 0.01
ADAM_STEP = 10

LANE = 128
SUB = 8

P_Z, P_XS, P_G, P_Q, P_BC, P_K, P_V, P_DT = 0, 2048, 4096, 6144, 7168, 8192, 8320, 8448
P_W = 8704
P_MAIN = 8192
T_K, T_V, T_DT = P_K - P_MAIN, P_V - P_MAIN, P_DT - P_MAIN
_PIECES = ((0, 2048, P_Z), (2048, 2048, P_XS), (4096, 1024, P_BC), (5120, 32, P_DT), (5152, 1024, P_Q),
           (6176, 128, P_K), (6304, 128, P_V), (6432, 2048, P_G))

N_CHIPS = 4
N_DEV = 8


def _cp(sem=None, vmem_mb=48):
    return pltpu.CompilerParams(dimension_semantics=sem, vmem_limit_bytes=vmem_mb * 1024 * 1024)


def _pick(n, cands):
    for c in cands:
        if n % c == 0:
            return c
    raise ValueError(f"no block size for {n}")


def _rows8(p):
    k, c = p.shape
    return jnp.concatenate([p.astype(F32), jnp.zeros((SUB - k, c), F32)], axis=0)


class _SideJob(NamedTuple):
    inputs: list
    out_shape: list
    sems: list
    start: object
    finish: object


def _mm(a, b, name, *, trans_a=False, out_dtype=F32, res=None, res_scale=1.0, side=None):
    if trans_a:
        k_dim, m = a.shape
    else:
        m, k_dim = a.shape
    k2, n = b.shape
    assert k_dim == k2, (a.shape, b.shape)
    tm = _pick(m, (1408, 1024, 512, 256, 128))
    tn = _pick(n, (1408, 1024, 512, 256, 128))
    tk = _pick(k_dim, (2816, 2176, 2048, 1024, 512, 256, 128))
    nk = k_dim // tk
    grid = (m // tm, n // tn, nk)
    dn = (((0,), (0,)), ((), ())) if trans_a else (((1,), (0,)), ((), ()))
    n_in = 2 if res is None else 3
    ns_in = len(side.inputs) if side else 0
    ns_out = len(side.out_shape) if side else 0

    def body(*refs):
        a_ref, b_ref = refs[0], refs[1]
        o_ref = refs[n_in + ns_in]
        scratch = refs[n_in + ns_in + 1 + ns_out:]
        job_refs = (refs[n_in:n_in + ns_in], refs[n_in + ns_in + 1:n_in + ns_in + 1 + ns_out],
                    scratch[1:] if nk > 1 else scratch)
        i, j, k = pl.program_id(0), pl.program_id(1), pl.program_id(2)

        def finish(r):
            if res is not None:
                r = r + res_scale * refs[2][...]
            o_ref[...] = r.astype(out_dtype)

        if side:
            @pl.when(jnp.logical_and(jnp.logical_and(i == 0, j == 0), k == 0))
            def _():
                side.start(*job_refs)

        part = lax.dot_general(a_ref[...].astype(_MXU), b_ref[...].astype(_MXU), dn, preferred_element_type=F32)
        if nk == 1:
            finish(part)
        else:
            acc = scratch[0]

            @pl.when(k == 0)
            def _():
                acc[...] = part

            @pl.when(k > 0)
            def _():
                acc[...] += part

            @pl.when(k == nk - 1)
            def _():
                finish(acc[...])

        if side:
            @pl.when(jnp.logical_and(jnp.logical_and(i == grid[0] - 1, j == grid[1] - 1), k == nk - 1))
            def _():
                side.finish(*job_refs)

    if trans_a:
        a_spec = pl.BlockSpec((tk, tm), lambda i, j, k: (k, i))
    else:
        a_spec = pl.BlockSpec((tm, tk), lambda i, j, k: (i, k))
    in_specs = [a_spec, pl.BlockSpec((tk, tn), lambda i, j, k: (k, j))]
    args = [a, b]
    if res is not None:
        in_specs.append(pl.BlockSpec((tm, tn), lambda i, j, k: (i, j)))
        args.append(res)
    out_spec = pl.BlockSpec((tm, tn), lambda i, j, k: (i, j))
    out_shape = jax.ShapeDtypeStruct((m, n), out_dtype)
    scratch_shapes = [pltpu.VMEM((tm, tn), F32)] if nk > 1 else []
    if not side:
        return pl.pallas_call(
            body, name=name, out_shape=out_shape, grid=grid, in_specs=in_specs, out_specs=out_spec,
            scratch_shapes=scratch_shapes, compiler_params=_cp(("parallel", "parallel", "arbitrary")),
        )(*args)
    hbm = pl.BlockSpec(memory_space=pltpu.HBM)
    outs = pl.pallas_call(
        body, name=name, out_shape=[out_shape] + list(side.out_shape), grid=grid,
        in_specs=in_specs + [hbm] * ns_in, out_specs=[out_spec] + [hbm] * ns_out,
        scratch_shapes=scratch_shapes + list(side.sems),
        compiler_params=_cp(("arbitrary", "arbitrary", "arbitrary")),
    )(*args, *side.inputs)
    return outs[0], list(outs[1:])


def _shift_down(cur, prev8, s):
    r = pltpu.roll(cur, s, 0)
    p = pltpu.roll(prev8, s, 0)
    row8 = lax.broadcasted_iota(jnp.int32, (SUB, 1), 0)
    fixed = jnp.where(row8 < s, p, r[0:SUB])
    if cur.shape[0] == SUB:
        return fixed
    return jnp.concatenate([fixed, r[SUB:]], axis=0)


def _shift_up(cur, next8, s):
    tm = cur.shape[0]
    r = pltpu.roll(cur, tm - s, 0)
    p = pltpu.roll(next8, SUB - s, 0)
    row8 = lax.broadcasted_iota(jnp.int32, (SUB, 1), 0)
    fixed = jnp.where(row8 >= SUB - s, p, r[tm - SUB:])
    return jnp.concatenate([r[:tm - SUB], fixed], axis=0)


def _conv_pre(cur, prev8, w_ref, b_row, taps):
    acc = cur * w_ref[taps - 1:taps, :] + b_row
    for s in range(1, taps):
        acc = acc + _shift_down(cur, prev8, s) * w_ref[taps - 1 - s:taps - s, :]
    return acc


def _dot01_r(x, m01, parts=3):
    acc = None
    r = x
    for _ in range(parts):
        hi = r.astype(jnp.bfloat16)
        t = jnp.dot(hi, m01, preferred_element_type=F32)
        acc = t if acc is None else acc + t
        r = r - hi.astype(F32)
    return acc


def _dot01_l(m01, x, parts=3):
    acc = None
    r = x
    for _ in range(parts):
        hi = r.astype(jnp.bfloat16)
        t = jnp.dot(m01, hi, preferred_element_type=F32)
        acc = t if acc is None else acc + t
        r = r - hi.astype(F32)
    return acc


def _dot(a, b):
    return jnp.dot(a.astype(_MXU), b.astype(_MXU), preferred_element_type=F32)


def _dot_nt(a, b):
    return lax.dot_general(a.astype(_MXU), b.astype(_MXU), (((1,), (1,)), ((), ())), preferred_element_type=F32)


def _dot_tn(a, b):
    return lax.dot_general(a.astype(_MXU), b.astype(_MXU), (((0,), (0,)), ((), ())), preferred_element_type=F32)


def _sigmoid(x):
    return 1.0 / (1.0 + jnp.exp(-x))


def _half_masks():
    lane = lax.broadcasted_iota(jnp.int32, (1, LANE), 1)
    lo = (lane < 64).astype(F32)
    return lo, 1.0 - lo


_TC = 1024


def _tm_rows(t):
    return min(512, t)


HALO = 16


def _prev_halo(tm, width, pos):
    def index(*ids):
        i, col = pos(*ids)
        return (jnp.maximum(i * (tm // HALO) - 1, 0), col)
    return pl.BlockSpec((HALO, width), index)


def _next_halo(tm, t, width, pos):
    def index(*ids):
        i, col = pos(*ids)
        return (jnp.minimum((i + 1) * (tm // HALO), t // HALO - 1), col)
    return pl.BlockSpec((HALO, width), index)


def _conv_silu_fwd(proj, colblk0, nblk, w8, b8, name):
    t = proj.shape[0]
    tm = _tm_rows(t)

    def body(c_ref, p_ref, w_ref, b_ref, o_ref, pre_ref):
        i = pl.program_id(1)
        prev8 = jnp.where(i > 0, p_ref[SUB:HALO, :].astype(F32), 0.0)
        pre = _conv_pre(c_ref[...].astype(F32), prev8, w_ref, b_ref[0:1, :], SSM_K)
        o_ref[...] = pre * _sigmoid(pre)
        pre_ref[...] = pre.astype(_ACT)

    blk = pl.BlockSpec((tm, _TC), lambda j, i: (i, j))
    return pl.pallas_call(
        body, name=name,
        out_shape=(jax.ShapeDtypeStruct((t, nblk * _TC), F32), jax.ShapeDtypeStruct((t, nblk * _TC), _ACT)),
        grid=(nblk, t // tm),
        in_specs=[pl.BlockSpec((tm, _TC), lambda j, i: (i, colblk0 + j)),
                  _prev_halo(tm, _TC, lambda j, i: (i, colblk0 + j)),
                  pl.BlockSpec((SUB, _TC), lambda j, i: (0, j)),
                  pl.BlockSpec((SUB, _TC), lambda j, i: (0, j))],
        out_specs=(blk, blk),
        compiler_params=_cp(("parallel", "parallel")),
    )(proj, proj, w8, b8)


def _silu_grad(pre):
    sg = _sigmoid(pre)
    return sg * (1.0 + pre * (1.0 - sg))


def _conv_grads(d, d_next8, cur, w_ref, acc_ref, taps, cols=slice(None)):
    du = d * w_ref[taps - 1:taps, :]
    acc_ref[taps:taps + 1, cols] += jnp.sum(d, axis=0, keepdims=True)
    acc_ref[taps - 1:taps, cols] += jnp.sum(d * cur, axis=0, keepdims=True)
    for s in range(1, taps):
        up = _shift_up(d, d_next8, s)
        du = du + up * w_ref[taps - 1 - s:taps - s, :]
        acc_ref[taps - 1 - s:taps - s, cols] += jnp.sum(up * cur, axis=0, keepdims=True)
    return du


def _conv_silu_bwd(proj, colblk0, nblk, pre, w8, d_out, name):
    t = proj.shape[0]
    tm = _tm_rows(t)
    nt = t // tm

    def body(c_ref, pre_ref, pren_ref, w_ref, d_ref, dn_ref, du_ref, acc_ref):
        i = pl.program_id(1)

        @pl.when(i == 0)
        def _():
            acc_ref[...] = jnp.zeros_like(acc_ref)

        dpre = d_ref[...].astype(F32) * _silu_grad(pre_ref[...].astype(F32))
        dpre_n = jnp.where(i < nt - 1, dn_ref[0:SUB, :].astype(F32) * _silu_grad(pren_ref[0:SUB, :].astype(F32)), 0.0)
        du_ref[...] = _conv_grads(dpre, dpre_n, c_ref[...].astype(F32), w_ref, acc_ref, SSM_K).astype(_ACT)

    c = nblk * _TC
    blk = pl.BlockSpec((tm, _TC), lambda j, i: (i, j))
    nxt = _next_halo(tm, t, _TC, lambda j, i: (i, j))
    par = pl.BlockSpec((SUB, _TC), lambda j, i: (0, j))
    return pl.pallas_call(
        body, name=name,
        out_shape=(jax.ShapeDtypeStruct((t, c), _ACT), jax.ShapeDtypeStruct((SUB, c), F32)),
        grid=(nblk, nt),
        in_specs=[pl.BlockSpec((tm, _TC), lambda j, i: (i, colblk0 + j)), blk, nxt, par, blk, nxt],
        out_specs=(blk, par),
        compiler_params=_cp(("parallel", "arbitrary")),
    )(proj, pre, pre, w8, d_out, d_out)


def _expand_consts():
    e = np.zeros((LANE, D_INNER), np.float32)
    for h in range(N_HEADS):
        e[h, h * HEAD_P:(h + 1) * HEAD_P] = 1.0
    return jnp.asarray(e, jnp.bfloat16), jnp.asarray(e.T.copy(), jnp.bfloat16)


def _ssd_common(dtr_ref, dtb_ref, alog_ref, e_ref):
    lane = lax.broadcasted_iota(jnp.int32, (1, LANE), 1)
    hm = lane < N_HEADS
    pre = dtr_ref[...] + dtb_ref[0:1, :]
    dt = jnp.where(hm, jnp.maximum(pre, 0.0) + jnp.log(1.0 + jnp.exp(-jnp.abs(pre))), 0.0)
    a_row = jnp.where(hm, -jnp.exp(alog_ref[0:1, :]), 0.0)
    adt = dt * a_row
    r = lax.broadcasted_iota(jnp.int32, (CHUNK, CHUNK), 0)
    c = lax.broadcasted_iota(jnp.int32, (CHUNK, CHUNK), 1)
    causal = r >= c
    acs = _dot01_l(causal.astype(jnp.bfloat16), adt)
    e = e_ref[...]
    acs_x = _dot01_r(acs, e, parts=2)
    dt_x = _dot01_r(dt, e, parts=2)
    return pre, dt, a_row, acs, acs_x, dt_x, causal, hm


def _decay(acs, acs_t, h, causal):
    seg = acs[:, h:h + 1] - acs_t[h:h + 1, :]
    return jnp.exp(jnp.where(causal, seg, NEG))


def _ssd_fwd(xs_c, bc_c, proj, tail, dtb8, alog8, dsk8, nw8, name):
    t = xs_c.shape[0]
    nc = t // CHUNK
    e_bf, _ = _expand_consts()
    gw = D_INNER // N_GROUPS

    def body(xs_ref, bc_ref, dtr_ref, z_ref, dtb_ref, alog_ref, dsk_ref, nw_ref, e_ref,
             y_ref, ys_ref, hp_ref, h_ref):
        c_id = pl.program_id(0)

        @pl.when(c_id == 0)
        def _():
            h_ref[...] = jnp.zeros_like(h_ref)

        _, dt, a_row, acs, acs_x, dt_x, causal, _ = _ssd_common(dtr_ref, dtb_ref, alog_ref, e_ref)
        acs_t = acs.T
        xs = xs_ref[...]
        x_dt = xs * dt_x
        last_x = acs_x[CHUNK - 1:CHUNK, :]
        w_end = jnp.exp(last_x - acs_x)
        e_in = jnp.exp(acs_x)
        d_x = _dot01_r(dsk_ref[...], e_ref[...])[0:1, :]
        hprev = h_ref[...]
        hp_ref[...] = hprev
        lo, hi = _half_masks()
        for g in range(N_GROUPS):
            bg = bc_ref[:, g * N_STATE:(g + 1) * N_STATE]
            cg = bc_ref[:, N_GROUPS * N_STATE + g * N_STATE:N_GROUPS * N_STATE + (g + 1) * N_STATE]
            sl = slice(g * gw, (g + 1) * gw)
            gm = _dot_nt(cg, bg)
            st = _dot(bg.T, x_dt[:, sl] * w_end[:, sl])
            y_off = _dot(cg, hprev[:, sl]) * e_in[:, sl]
            for j in range(gw // LANE):
                h0 = g * (gw // HEAD_P) + 2 * j
                cs = slice(g * gw + j * LANE, g * gw + (j + 1) * LANE)
                xp = x_dt[:, cs]
                m0 = gm * _decay(acs, acs_t, h0, causal)
                m1 = gm * _decay(acs, acs_t, h0 + 1, causal)
                yd = _dot(m0, xp * lo) + _dot(m1, xp * hi)
                y_ref[:, cs] = yd + y_off[:, j * LANE:(j + 1) * LANE] + xs[:, cs] * d_x[:, cs]
            h_ref[:, sl] = hprev[:, sl] * jnp.exp(last_x[:, sl]) + st
        y = y_ref[...]
        z = z_ref[...].astype(F32)
        y2 = y * (z * _sigmoid(z))
        for g in range(N_GROUPS):
            sl = slice(g * gw, (g + 1) * gw)
            yg = y2[:, sl]
            rinv = lax.rsqrt(jnp.mean(yg * yg, axis=-1, keepdims=True) + RMS_EPS)
            ys_ref[:, sl] = (yg * rinv * nw_ref[0:1, sl]).astype(_ACT)

    small = pl.BlockSpec((SUB, LANE), lambda c: (0, 0))
    return pl.pallas_call(
        body, name=name,
        out_shape=(jax.ShapeDtypeStruct((t, D_INNER), F32), jax.ShapeDtypeStruct((t, D_INNER), _ACT),
                   jax.ShapeDtypeStruct((t, D_INNER), F32)),
        grid=(nc,),
        in_specs=[pl.BlockSpec((CHUNK, D_INNER), lambda c: (c, 0)),
                  pl.BlockSpec((CHUNK, 1024), lambda c: (c, 0)),
                  pl.BlockSpec((CHUNK, LANE), lambda c: (c, T_DT // LANE)),
                  pl.BlockSpec((CHUNK, D_INNER), lambda c: (c, P_Z // D_INNER)),
                  small, small, small,
                  pl.BlockSpec((SUB, D_INNER), lambda c: (0, 0)),
                  pl.BlockSpec((LANE, D_INNER), lambda c: (0, 0))],
        out_specs=(pl.BlockSpec((CHUNK, D_INNER), lambda c: (c, 0)),
                   pl.BlockSpec((CHUNK, D_INNER), lambda c: (c, 0)),
                   pl.BlockSpec((N_STATE, D_INNER), lambda c: (c, 0))),
        scratch_shapes=[pltpu.VMEM((N_STATE, D_INNER), F32)],
        compiler_params=_cp(("arbitrary",)),
    )(xs_c, bc_c, tail, proj, dtb8, alog8, dsk8, nw8, e_bf)


def _ssd_bwd(d_ys, y, xs_c, bc_c, proj, tail, hprev_all, dtb8, alog8, dsk8, nw8, name):
    t = xs_c.shape[0]
    nc = t // CHUNK
    e_bf, et_bf = _expand_consts()
    gw = D_INNER // N_GROUPS

    def body(dys_ref, y_ref, xs_ref, bc_ref, dtr_ref, z_ref, hp_ref, dtb_ref, alog_ref, dsk_ref, nw_ref,
             e_ref, et_ref, dxs_ref, dbc_ref, dz_ref, ddt_ref, acc_ref, dnw_ref, dh_ref, dx_ref):
        step = pl.program_id(0)

        @pl.when(step == 0)
        def _():
            dh_ref[...] = jnp.zeros_like(dh_ref)
            acc_ref[...] = jnp.zeros_like(acc_ref)
            dnw_ref[...] = jnp.zeros_like(dnw_ref)

        pre, dt, a_row, acs, acs_x, dt_x, causal, hm = _ssd_common(dtr_ref, dtb_ref, alog_ref, e_ref)
        acs_t = acs.T
        et = et_ref[...]
        xs = xs_ref[...]
        x_dt = xs * dt_x
        last_x = acs_x[CHUNK - 1:CHUNK, :]
        w_end = jnp.exp(last_x - acs_x)
        e_in = jnp.exp(acs_x)
        e_last = jnp.exp(last_x)
        d_x = _dot01_r(dsk_ref[...], e_ref[...])[0:1, :]

        y = y_ref[...]
        z = z_ref[...].astype(F32)
        sz = _sigmoid(z)
        gz = z * sz
        y2 = y * gz
        dys = dys_ref[...].astype(F32)
        for g in range(N_GROUPS):
            sl = slice(g * gw, (g + 1) * gw)
            yg = y2[:, sl]
            rinv = lax.rsqrt(jnp.mean(yg * yg, axis=-1, keepdims=True) + RMS_EPS)
            nrm = yg * rinv
            dn = dys[:, sl] * nw_ref[0:1, sl]
            dnw_ref[0:1, sl] += jnp.sum(dys[:, sl] * nrm, axis=0, keepdims=True)
            dx_ref[:, sl] = rinv * (dn - nrm * jnp.mean(dn * nrm, axis=-1, keepdims=True))
        dy2 = dx_ref[...]
        dy = dy2 * gz
        dz_ref[...] = (dy2 * y * (sz * (1.0 + z * (1.0 - sz)))).astype(_ACT)

        dh_next = dh_ref[...]
        hprev = hp_ref[...]
        lo, hi = _half_masks()
        r = lax.broadcasted_iota(jnp.int32, (CHUNK, CHUNK), 0)
        c = lax.broadcasted_iota(jnp.int32, (CHUNK, CHUNK), 1)
        from_here = (c >= r).astype(jnp.bfloat16)
        before = c < r
        lane = lax.broadcasted_iota(jnp.int32, (1, LANE), 1)
        da_intra = jnp.zeros((CHUNK, LANE), F32)
        v_seg = jnp.zeros((CHUNK, LANE), F32)
        z_seg = jnp.zeros((CHUNK, LANE), F32)
        tail_parts = []
        for g in range(N_GROUPS):
            bg = bc_ref[:, g * N_STATE:(g + 1) * N_STATE]
            cg = bc_ref[:, N_GROUPS * N_STATE + g * N_STATE:N_GROUPS * N_STATE + (g + 1) * N_STATE]
            sl = slice(g * gw, (g + 1) * gw)
            et_g = et_ref[g * gw:(g + 1) * gw, :]
            gm = _dot_nt(cg, bg)
            dzg = e_in[:, sl] * dy[:, sl]
            dcg = _dot_nt(dzg, hprev[:, sl])
            dh_c = _dot(cg.T, dzg)
            q = _dot(bg, dh_next[:, sl])
            dbg = _dot_nt(x_dt[:, sl] * w_end[:, sl], dh_next[:, sl])
            y_off = _dot(cg, hprev[:, sl]) * e_in[:, sl]
            v_seg = v_seg + _dot01_r(dy[:, sl] * y_off, et_g, parts=1)
            z_seg = z_seg + _dot01_r(w_end[:, sl] * q * x_dt[:, sl], et_g, parts=1)
            dgm = jnp.zeros((CHUNK, CHUNK), F32)
            for j in range(gw // LANE):
                h0 = g * (gw // HEAD_P) + 2 * j
                cs = slice(g * gw + j * LANE, g * gw + (j + 1) * LANE)
                xp = x_dt[:, cs]
                dyp = dy[:, cs]
                dxd = jnp.zeros((CHUNK, LANE), F32)
                for half, msk in ((0, lo), (1, hi)):
                    lam = _decay(acs, acs_t, h0 + half, causal)
                    mm = gm * lam
                    dym = dyp * msk
                    dmm = _dot_nt(dym, xp)
                    dxd = dxd + _dot_tn(mm, dym)
                    dgm = dgm + dmm * lam
                    below = _dot(from_here, dmm * mm)
                    col = jnp.sum(jnp.where(before, below, 0.0), axis=-1, keepdims=True)
                    da_intra = da_intra + jnp.where(lane == h0 + half, col, 0.0)
                dx_ref[:, cs] = dxd + w_end[:, cs] * q[:, j * LANE:(j + 1) * LANE]
            dbc_ref[:, N_GROUPS * N_STATE + g * N_STATE:N_GROUPS * N_STATE + (g + 1) * N_STATE] = dcg + _dot(dgm, bg)
            dbc_ref[:, g * N_STATE:(g + 1) * N_STATE] = dbg + _dot_tn(dgm, cg)
            dh_ref[:, sl] = e_last[:, sl] * dh_next[:, sl] + dh_c
            tail_parts.append(e_last[:, sl] * jnp.sum(dh_next[:, sl] * hprev[:, sl], axis=0, keepdims=True))
        dxt = dx_ref[...]

        u_seg = _dot01_r(xs * dxt, et, parts=1)
        q_full = jnp.concatenate(tail_parts, axis=1)
        t_row = _dot01_r(jnp.broadcast_to(q_full, (SUB, D_INNER)), et)[0:1, :]
        d_alpha = (da_intra + _dot01_l(from_here, v_seg) + _dot01_l(before.astype(jnp.bfloat16), z_seg) + t_row)
        d_dt = a_row * d_alpha + u_seg
        sgp = _sigmoid(pre)
        d_raw = jnp.where(hm, d_dt * sgp, 0.0)
        ddt_ref[...] = d_raw.astype(_ACT)
        acc_ref[0:1, :] += jnp.sum(d_raw, axis=0, keepdims=True)
        acc_ref[1:2, :] += jnp.sum(d_alpha * dt, axis=0, keepdims=True) * a_row
        dd_row = jnp.sum(dy * xs, axis=0, keepdims=True)
        acc_ref[2:3, :] += _dot01_r(jnp.broadcast_to(dd_row, (SUB, D_INNER)), et)[0:1, :]
        dxs_ref[...] = dy * d_x + dxt * dt_x

    rev = lambda c: (nc - 1 - c, 0)
    small = pl.BlockSpec((SUB, LANE), lambda c: (0, 0))
    return pl.pallas_call(
        body, name=name,
        out_shape=(jax.ShapeDtypeStruct((t, D_INNER), F32), jax.ShapeDtypeStruct((t, 1024), F32),
                   jax.ShapeDtypeStruct((t, D_INNER), _ACT), jax.ShapeDtypeStruct((t, LANE), _ACT),
                   jax.ShapeDtypeStruct((SUB, LANE), F32), jax.ShapeDtypeStruct((SUB, D_INNER), F32)),
        grid=(nc,),
        in_specs=[pl.BlockSpec((CHUNK, D_INNER), rev),
                  pl.BlockSpec((CHUNK, D_INNER), rev),
                  pl.BlockSpec((CHUNK, D_INNER), rev),
                  pl.BlockSpec((CHUNK, 1024), rev),
                  pl.BlockSpec((CHUNK, LANE), lambda c: (nc - 1 - c, T_DT // LANE)),
                  pl.BlockSpec((CHUNK, D_INNER), lambda c: (nc - 1 - c, P_Z // D_INNER)),
                  pl.BlockSpec((N_STATE, D_INNER), rev),
                  small, small, small,
                  pl.BlockSpec((SUB, D_INNER), lambda c: (0, 0)),
                  pl.BlockSpec((LANE, D_INNER), lambda c: (0, 0)),
                  pl.BlockSpec((D_INNER, LANE), lambda c: (0, 0))],
        out_specs=(pl.BlockSpec((CHUNK, D_INNER), rev),
                   pl.BlockSpec((CHUNK, 1024), rev),
                   pl.BlockSpec((CHUNK, D_INNER), rev),
                   pl.BlockSpec((CHUNK, LANE), rev),
                   small,
                   pl.BlockSpec((SUB, D_INNER), lambda c: (0, 0))),
        scratch_shapes=[pltpu.VMEM((N_STATE, D_INNER), F32), pltpu.VMEM((CHUNK, D_INNER), F32)],
        compiler_params=_cp(("arbitrary",), vmem_mb=56),
    )(d_ys, y, xs_c, bc_c, tail, proj, hprev_all, dtb8, alog8, dsk8, nw8, e_bf, et_bf)


def _rel_tables():
    qi = np.arange(WIN)[:, None] + WIN
    kj = np.arange(2 * WIN)[None, :]
    rel = qi - kj
    n = np.maximum(rel, 0)
    max_exact = REL_BUCKETS // 2
    nf = np.maximum(n, 1).astype(np.float32)
    large = max_exact + (np.log(nf / np.float32(max_exact)) / np.float32(math.log(WIN / max_exact))
                         * np.float32(REL_BUCKETS - max_exact)).astype(np.int32)
    large = np.minimum(large, REL_BUCKETS - 1)
    bucket = np.where(n < max_exact, n, large)
    valid = (rel >= 0) & (rel < WIN)
    sink_col = np.broadcast_to(kj == 0, rel.shape)
    onehot = np.zeros((BIAS_ROWS, WIN * 2 * WIN), np.float32)
    flat_b = np.where(sink_col, REL_BUCKETS, bucket).reshape(-1)
    flat_v = (valid | sink_col).reshape(-1)
    first_v = ((valid & (kj >= WIN)) | sink_col).reshape(-1)
    idx = np.arange(WIN * 2 * WIN)
    onehot[flat_b[flat_v], idx[flat_v]] = 1.0
    return onehot, np.stack([first_v, flat_v]).astype(np.float32)


def _bias_expand(table_t, name):
    onehot, valid = _rel_tables()

    def body(rb_ref, oh_ref, v_ref, o_ref):
        full = _dot01_r(rb_ref[...], oh_ref[...])
        o_ref[0] = jnp.where(v_ref[0:1, :] > 0.5, full, NEG)
        o_ref[1] = jnp.where(v_ref[1:2, :] > 0.5, full, NEG)

    return pl.pallas_call(
        body, name=name, out_shape=jax.ShapeDtypeStruct((2, A_HEADS, WIN * 2 * WIN), F32),
        compiler_params=_cp(None),
    )(table_t, jnp.asarray(onehot, jnp.bfloat16), jnp.asarray(valid, F32))


def _bias_reduce(dbias, name):
    onehot, _ = _rel_tables()

    def body(d_ref, oh_ref, o_ref):
        acc = None
        r = d_ref[...]
        for _ in range(3):
            hi = r.astype(jnp.bfloat16)
            tt = lax.dot_general(hi, oh_ref[...], (((1,), (1,)), ((), ())), preferred_element_type=F32)
            acc = tt if acc is None else acc + tt
            r = r - hi.astype(F32)
        o_ref[...] = acc

    return pl.pallas_call(
        body, name=name, out_shape=jax.ShapeDtypeStruct((A_HEADS, BIAS_ROWS), F32),
        compiler_params=_cp(None),
    )(dbias, jnp.asarray(onehot, jnp.bfloat16))


def _attn_bands(kc_ref, kp_ref, vc_ref, vp_ref, has_prev):
    lo, hi = _half_masks()
    row = lax.broadcasted_iota(jnp.int32, (2 * WIN, 1), 0)
    keep = (row > 0).astype(F32)
    kb = jnp.concatenate([jnp.where(has_prev, kp_ref[...], 0.0), kc_ref[...]], axis=0) * (keep * (A_DH ** -0.5))
    vb = jnp.concatenate([jnp.where(has_prev, vp_ref[...], 0.0), vc_ref[...]], axis=0) * keep
    kr = pltpu.roll(kb, 64, 1)
    vr = pltpu.roll(vb, 64, 1)
    kk = ((kb * lo, kr * hi), (kr * lo, kb * hi))
    vv = ((vb * lo, vr * hi), (vr * lo, vb * hi))
    return kk, vv, (hi, lo)


def _attn_logits(q_ref, kk, lg_ref):
    for h in range(A_HEADS):
        j, half, kv = h // 2, h % 2, h // (A_HEADS // 2)
        lg_ref[h] = _dot_nt(q_ref[:, j * LANE:(j + 1) * LANE], kk[kv][half])


def _attn_fwd(proj, tail, bias, name):
    t = proj.shape[0]
    nb = t // WIN

    def body(q_ref, kc_ref, kp_ref, vc_ref, vp_ref, b_ref, o_ref, lg_ref, p_ref):
        n = pl.program_id(0)
        kk, vv, ones = _attn_bands(kc_ref, kp_ref, vc_ref, vp_ref, n > 0)
        _attn_logits(q_ref, kk, lg_ref)
        for h in range(A_HEADS):
            logits = lg_ref[h] + b_ref[h]
            p_ref[h] = jnp.exp(logits - jnp.max(logits, axis=-1, keepdims=True)).astype(_MXU)
        lane = lax.broadcasted_iota(jnp.int32, (1, LANE), 1)
        for j in range(A_HEADS // 2):
            kv = (2 * j) // (A_HEADS // 2)
            outs = []
            for half in range(2):
                o = jnp.dot(p_ref[2 * j + half], (vv[kv][half] + ones[half]).astype(_MXU), preferred_element_type=F32)
                outs.append(o / pltpu.roll(o, 64, 1))
            o_ref[:, j * LANE:(j + 1) * LANE] = jnp.where(lane < 64, outs[0], outs[1]).astype(_ACT)

    kvspec = lambda col, prev: pl.BlockSpec(
        (WIN, LANE), (lambda n: (jnp.maximum(n - 1, 0), col)) if prev else (lambda n: (n, col)))
    return pl.pallas_call(
        body, name=name, out_shape=jax.ShapeDtypeStruct((t, D_MODEL), _ACT),
        grid=(nb,),
        in_specs=[pl.BlockSpec((WIN, 1024), lambda n: (n, P_Q // 1024)),
                  kvspec(T_K // LANE, False), kvspec(T_K // LANE, True),
                  kvspec(T_V // LANE, False), kvspec(T_V // LANE, True),
                  pl.BlockSpec((None, A_HEADS, WIN, 2 * WIN), lambda n: (jnp.minimum(n, 1), 0, 0, 0))],
        out_specs=pl.BlockSpec((WIN, 1024), lambda n: (n, 0)),
        scratch_shapes=[pltpu.VMEM((A_HEADS, WIN, 2 * WIN), F32), pltpu.VMEM((A_HEADS, WIN, 2 * WIN), _MXU)],
        compiler_params=_cp(("parallel",)),
    )(proj, tail, tail, tail, tail, bias)


def _attn_bwd(proj, tail, bias, y_attn, d_out, name):
    t = proj.shape[0]
    nb = t // WIN

    def body(q_ref, kc_ref, kp_ref, vc_ref, vp_ref, b_ref, y_ref, do_ref,
             dq_ref, dk_ref, dv_ref, db_ref, ck_ref, cv_ref, lg_ref, dl_ref, p_ref):
        n = pl.program_id(0)

        @pl.when(n == 0)
        def _():
            db_ref[...] = jnp.zeros_like(db_ref)
            ck_ref[...] = jnp.zeros_like(ck_ref)
            cv_ref[...] = jnp.zeros_like(cv_ref)

        @pl.when(n < nb)
        def _():
            kk, vv, _ = _attn_bands(kc_ref, kp_ref, vc_ref, vp_ref, n > 0)
            lo, hi = _half_masks()
            ones_k = jnp.ones((2 * WIN, LANE), jnp.bfloat16)
            ones_d = jnp.ones((LANE, LANE), jnp.bfloat16)
            _attn_logits(q_ref, kk, lg_ref)
            for h in range(A_HEADS):
                j, half, kv = h // 2, h % 2, h // (A_HEADS // 2)
                msk = hi if half else lo
                logits = lg_ref[h] + b_ref[h]
                p = jnp.exp(logits - jnp.max(logits, axis=-1, keepdims=True))
                den = jnp.dot(p.astype(_MXU), ones_k.astype(_MXU), preferred_element_type=F32)
                dop = do_ref[:, j * LANE:(j + 1) * LANE].astype(F32)
                delta = _dot01_r(dop * y_ref[:, j * LANE:(j + 1) * LANE].astype(F32) * msk, ones_d, parts=2)
                inv = 1.0 / den
                probs = p * jnp.concatenate([inv, inv], axis=1)
                dprobs = _dot_nt(dop, vv[kv][half])
                dlog = probs * (dprobs - jnp.concatenate([delta, delta], axis=1))
                db_ref[h] += dlog
                dl_ref[h] = dlog.astype(_MXU)
                p_ref[h] = probs.astype(_MXU)
            dk_t = [[None, None], [None, None]]
            dv_t = [[None, None], [None, None]]
            for j in range(A_HEADS // 2):
                kv = (2 * j) // (A_HEADS // 2)
                qs = q_ref[:, j * LANE:(j + 1) * LANE].astype(F32) * (A_DH ** -0.5)
                dop = do_ref[:, j * LANE:(j + 1) * LANE].astype(F32)
                dq = None
                for half, msk in ((0, lo), (1, hi)):
                    h = 2 * j + half
                    dqh = jnp.dot(dl_ref[h], kk[kv][half].astype(_MXU), preferred_element_type=F32)
                    dq = dqh if dq is None else dq + dqh
                    dkh = lax.dot_general((qs * msk).astype(_MXU), dl_ref[h], (((0,), (0,)), ((), ())),
                                          preferred_element_type=F32)
                    dvh = lax.dot_general((dop * msk).astype(_MXU), p_ref[h], (((0,), (0,)), ((), ())),
                                          preferred_element_type=F32)
                    dk_t[kv][half] = dkh if dk_t[kv][half] is None else dk_t[kv][half] + dkh
                    dv_t[kv][half] = dvh if dv_t[kv][half] is None else dv_t[kv][half] + dvh
                dq_ref[:, j * LANE:(j + 1) * LANE] = dq.astype(_ACT)
            row = lax.broadcasted_iota(jnp.int32, (2 * WIN, 1), 0)

            def band(acc):
                a = (acc[0][0] + pltpu.roll(acc[0][1], 64, 0)) + (pltpu.roll(acc[1][0], 64, 0) + acc[1][1])
                return jnp.where(row > 0, a.T, 0.0)

            dkb = band(dk_t)
            dvb = band(dv_t)
            dk_ref[...] = (ck_ref[...] + dkb[0:WIN]).astype(_ACT)
            dv_ref[...] = (cv_ref[...] + dvb[0:WIN]).astype(_ACT)
            ck_ref[...] = dkb[WIN:]
            cv_ref[...] = dvb[WIN:]

        @pl.when(n == nb)
        def _():
            dk_ref[...] = ck_ref[...].astype(_ACT)
            dv_ref[...] = cv_ref[...].astype(_ACT)

    cur = lambda n: jnp.minimum(n, nb - 1)
    prv = lambda n: jnp.maximum(jnp.minimum(n, nb - 1) - 1, 0)
    kvspec = lambda col, prev: pl.BlockSpec(
        (WIN, LANE), (lambda n: (prv(n), col)) if prev else (lambda n: (cur(n), col)))
    band_shape = (A_HEADS, WIN, 2 * WIN)
    return pl.pallas_call(
        body, name=name,
        out_shape=(jax.ShapeDtypeStruct((t, D_MODEL), _ACT), jax.ShapeDtypeStruct((t, LANE), _ACT),
                   jax.ShapeDtypeStruct((t, LANE), _ACT), jax.ShapeDtypeStruct(band_shape, F32)),
        grid=(nb + 1,),
        in_specs=[pl.BlockSpec((WIN, 1024), lambda n: (cur(n), P_Q // 1024)),
                  kvspec(T_K // LANE, False), kvspec(T_K // LANE, True),
                  kvspec(T_V // LANE, False), kvspec(T_V // LANE, True),
                  pl.BlockSpec((None,) + band_shape, lambda n: (jnp.minimum(n, 1), 0, 0, 0)),
                  pl.BlockSpec((WIN, 1024), lambda n: (cur(n), 0)),
                  pl.BlockSpec((WIN, 1024), lambda n: (cur(n), 0))],
        out_specs=(pl.BlockSpec((WIN, 1024), lambda n: (cur(n), 0)),
                   pl.BlockSpec((WIN, LANE), lambda n: (jnp.maximum(n - 1, 0), 0)),
                   pl.BlockSpec((WIN, LANE), lambda n: (jnp.maximum(n - 1, 0), 0)),
                   pl.BlockSpec(band_shape, lambda n: (0, 0, 0))),
        scratch_shapes=[pltpu.VMEM((WIN, LANE), F32), pltpu.VMEM((WIN, LANE), F32),
                        pltpu.VMEM(band_shape, F32), pltpu.VMEM(band_shape, _MXU), pltpu.VMEM(band_shape, _MXU)],
        compiler_params=_cp(("arbitrary",)),
    )(proj, tail, tail, tail, tail, bias, y_attn, d_out)


def _merge_fwd(bs, ba, proj, bg8, name):
    t = bs.shape[0]
    tm = _tm_rows(t)

    def body(bs_ref, ba_ref, gs_ref, ga_ref, bgs_ref, bga_ref, o_ref):
        g_s = _sigmoid(gs_ref[...] + bgs_ref[0:1, :])
        g_a = _sigmoid(ga_ref[...] + bga_ref[0:1, :])
        o_ref[...] = (g_s * bs_ref[...] + g_a * ba_ref[...]).astype(_ACT)

    row = lambda col: pl.BlockSpec((tm, 1024), lambda i: (i, col))
    return pl.pallas_call(
        body, name=name, out_shape=jax.ShapeDtypeStruct((t, D_MODEL), _ACT), grid=(t // tm,),
        in_specs=[row(0), row(0), row(P_G // 1024), row(P_G // 1024 + 1),
                  pl.BlockSpec((SUB, 1024), lambda i: (0, 0)), pl.BlockSpec((SUB, 1024), lambda i: (0, 1))],
        out_specs=row(0), compiler_params=_cp(("parallel",)),
    )(bs, ba, proj, proj, bg8, bg8)


def _merge_bwd(d_merged, bs, ba, proj, bg8, name):
    t = bs.shape[0]
    tm = _tm_rows(t)

    def body(dm_ref, bs_ref, ba_ref, gs_ref, ga_ref, bgs_ref, bga_ref, dbs_ref, dba_ref, dg_ref, acc_ref):
        @pl.when(pl.program_id(0) == 0)
        def _():
            acc_ref[...] = jnp.zeros_like(acc_ref)

        dm = dm_ref[...].astype(F32)
        g_s = _sigmoid(gs_ref[...] + bgs_ref[0:1, :])
        g_a = _sigmoid(ga_ref[...] + bga_ref[0:1, :])
        dbs_ref[...] = (dm * g_s).astype(_ACT)
        dba_ref[...] = (dm * g_a).astype(_ACT)
        dgs = dm * bs_ref[...].astype(F32) * g_s * (1.0 - g_s)
        dga = dm * ba_ref[...].astype(F32) * g_a * (1.0 - g_a)
        dg_ref[:, 0:1024] = dgs.astype(_ACT)
        dg_ref[:, 1024:2048] = dga.astype(_ACT)
        acc_ref[0:1, 0:1024] += jnp.sum(dgs, axis=0, keepdims=True)
        acc_ref[0:1, 1024:2048] += jnp.sum(dga, axis=0, keepdims=True)

    row = lambda col: pl.BlockSpec((tm, 1024), lambda i: (i, col))
    return pl.pallas_call(
        body, name=name,
        out_shape=(jax.ShapeDtypeStruct((t, D_MODEL), _ACT), jax.ShapeDtypeStruct((t, D_MODEL), _ACT),
                   jax.ShapeDtypeStruct((t, 2048), _ACT), jax.ShapeDtypeStruct((SUB, 2048), F32)),
        grid=(t // tm,),
        in_specs=[row(0), row(0), row(0), row(P_G // 1024), row(P_G // 1024 + 1),
                  pl.BlockSpec((SUB, 1024), lambda i: (0, 0)), pl.BlockSpec((SUB, 1024), lambda i: (0, 1))],
        out_specs=(row(0), row(0), pl.BlockSpec((tm, 2048), lambda i: (i, 0)),
                   pl.BlockSpec((SUB, 2048), lambda i: (0, 0))),
        compiler_params=_cp(("arbitrary",)),
    )(d_merged, bs, ba, proj, proj, bg8, bg8)


def _ln_stats(r):
    mu = jnp.mean(r, axis=-1, keepdims=True)
    xc = r - mu
    var = jnp.mean(xc * xc, axis=-1, keepdims=True)
    rstd = lax.rsqrt(var + LN_EPS)
    return xc * rstd, rstd


def _ln_bwd(dxhat, xhat, rstd):
    return rstd * (dxhat - jnp.mean(dxhat, axis=-1, keepdims=True)
                   - xhat * jnp.mean(dxhat * xhat, axis=-1, keepdims=True))


def _ln1_fwd(x, mix, g8, b8, name):
    t = x.shape[0]
    tm = _tm_rows(t)

    def body(x_ref, m_ref, g_ref, b_ref, xh_ref, h_ref, rs_ref):
        xhat, rstd = _ln_stats(ALPHA * x_ref[...] + m_ref[...])
        xh_ref[...] = xhat
        h_ref[...] = (xhat * g_ref[0:1, :] + b_ref[0:1, :]).astype(_ACT)
        rs_ref[...] = rstd

    row = pl.BlockSpec((tm, D_MODEL), lambda i: (i, 0))
    par = pl.BlockSpec((SUB, D_MODEL), lambda i: (0, 0))
    return pl.pallas_call(
        body, name=name,
        out_shape=(jax.ShapeDtypeStruct((t, D_MODEL), F32), jax.ShapeDtypeStruct((t, D_MODEL), _ACT),
                   jax.ShapeDtypeStruct((t, 1), F32)),
        grid=(t // tm,), in_specs=[row, row, par, par],
        out_specs=(row, row, pl.BlockSpec((tm, 1), lambda i: (i, 0))),
        compiler_params=_cp(("parallel",)),
    )(x, mix, g8, b8)


def _ln2_loss(xhat1, ffn, target, g1_8, b1_8, g2_8, b2_8, name):
    t = xhat1.shape[0]
    tm = _tm_rows(t)

    def body(xh_ref, f_ref, t_ref, g1_ref, b1_ref, g2_ref, b2_ref, d_ref, db_ref, acc_ref):
        @pl.when(pl.program_id(0) == 0)
        def _():
            acc_ref[...] = jnp.zeros_like(acc_ref)

        h1 = xh_ref[...] * g1_ref[0:1, :] + b1_ref[0:1, :]
        xhat, rstd = _ln_stats(ALPHA * h1 + f_ref[...])
        diff = xhat * g2_ref[0:1, :] + b2_ref[0:1, :] - t_ref[...]
        dy = diff * (1.0 / D_MODEL)
        acc_ref[0:1, :] += jnp.sum(dy * xhat, axis=0, keepdims=True)
        acc_ref[1:2, :] += jnp.sum(dy, axis=0, keepdims=True)
        acc_ref[2:3, :] += jnp.sum(diff * diff, axis=0, keepdims=True)
        d = _ln_bwd(dy * g2_ref[0:1, :], xhat, rstd)
        d_ref[...] = d
        db_ref[...] = d.astype(_ACT)

    row = pl.BlockSpec((tm, D_MODEL), lambda i: (i, 0))
    par = pl.BlockSpec((SUB, D_MODEL), lambda i: (0, 0))
    return pl.pallas_call(
        body, name=name,
        out_shape=(jax.ShapeDtypeStruct((t, D_MODEL), F32), jax.ShapeDtypeStruct((t, D_MODEL), _ACT),
                   jax.ShapeDtypeStruct((SUB, D_MODEL), F32)),
        grid=(t // tm,), in_specs=[row, row, row, par, par, par, par],
        out_specs=(row, row, par), compiler_params=_cp(("arbitrary",)),
    )(xhat1, ffn, target, g1_8, b1_8, g2_8, b2_8)


def _ln1_bwd(d_r2, d_h1_ffn, xhat1, rstd1, g1_8, name):
    t = xhat1.shape[0]
    tm = _tm_rows(t)

    def body(d2_ref, df_ref, xh_ref, rs_ref, g_ref, d_ref, db_ref, acc_ref):
        @pl.when(pl.program_id(0) == 0)
        def _():
            acc_ref[...] = jnp.zeros_like(acc_ref)

        dh = ALPHA * d2_ref[...] + df_ref[...]
        xhat = xh_ref[...]
        acc_ref[0:1, :] += jnp.sum(dh * xhat, axis=0, keepdims=True)
        acc_ref[1:2, :] += jnp.sum(dh, axis=0, keepdims=True)
        d = _ln_bwd(dh * g_ref[0:1, :], xhat, rs_ref[...])
        d_ref[...] = d
        db_ref[...] = d.astype(_ACT)

    row = pl.BlockSpec((tm, D_MODEL), lambda i: (i, 0))
    par = pl.BlockSpec((SUB, D_MODEL), lambda i: (0, 0))
    return pl.pallas_call(
        body, name=name,
        out_shape=(jax.ShapeDtypeStruct((t, D_MODEL), F32), jax.ShapeDtypeStruct((t, D_MODEL), _ACT),
                   jax.ShapeDtypeStruct((SUB, D_MODEL), F32)),
        grid=(t // tm,), in_specs=[row, row, row, pl.BlockSpec((tm, 1), lambda i: (i, 0)), par],
        out_specs=(row, row, par), compiler_params=_cp(("arbitrary",)),
    )(d_r2, d_h1_ffn, xhat1, rstd1, g1_8)


def _ffn_tm(t):
    return min(128, t)


def _ffn_act_fwd(u0, cw8, cb8, name):
    t = u0.shape[0]
    tm = _ffn_tm(t)

    def body(g_ref, gp_ref, v_ref, vp_ref, wg_ref, wv_ref, bg_ref, bv_ref, o_ref, u_ref):
        i = pl.program_id(0)
        gprev = jnp.where(i > 0, gp_ref[SUB:HALO, :].astype(F32), 0.0)
        vprev = jnp.where(i > 0, vp_ref[SUB:HALO, :].astype(F32), 0.0)
        gate = _conv_pre(g_ref[...].astype(F32), gprev, wg_ref, bg_ref[0:1, :], FFN_K)
        val = _conv_pre(v_ref[...].astype(F32), vprev, wv_ref, bv_ref[0:1, :], FFN_K)
        o_ref[...] = (gate * _sigmoid(gate) * val).astype(_ACT)
        u_ref[:, 0:D_FF] = gate.astype(_ACT)
        u_ref[:, D_FF:2 * D_FF] = val.astype(_ACT)

    cur = lambda col: pl.BlockSpec((tm, D_FF), lambda i: (i, col))
    prv = lambda col: _prev_halo(tm, D_FF, lambda i: (i, col))
    par = lambda col: pl.BlockSpec((SUB, D_FF), lambda i: (0, col))
    return pl.pallas_call(
        body, name=name,
        out_shape=(jax.ShapeDtypeStruct((t, D_FF), _ACT), jax.ShapeDtypeStruct((t, 2 * D_FF), _ACT)),
        grid=(t // tm,),
        in_specs=[cur(0), prv(0), cur(1), prv(1), par(0), par(1), par(0), par(1)],
        out_specs=(pl.BlockSpec((tm, D_FF), lambda i: (i, 0)), pl.BlockSpec((tm, 2 * D_FF), lambda i: (i, 0))),
        compiler_params=_cp(("parallel",)),
    )(u0, u0, u0, u0, cw8, cw8, cb8, cb8)


def _ffn_act_bwd(u0, u, cw8, d_a, name):
    t = u0.shape[0]
    tm = _ffn_tm(t)
    nt = t // tm

    def body(g0_ref, v0_ref, g_ref, gn_ref, v_ref, vn_ref, wg_ref, wv_ref, da_ref, dan_ref, du_ref, acc_ref):
        i = pl.program_id(0)

        @pl.when(i == 0)
        def _():
            acc_ref[...] = jnp.zeros_like(acc_ref)

        def grads(gate, val, da):
            return da * val * _silu_grad(gate), da * gate * _sigmoid(gate)

        dgate, dval = grads(g_ref[...].astype(F32), v_ref[...].astype(F32), da_ref[...].astype(F32))
        dgate_n, dval_n = grads(gn_ref[0:SUB, :].astype(F32), vn_ref[0:SUB, :].astype(F32),
                                dan_ref[0:SUB, :].astype(F32))
        last = i == nt - 1
        du_ref[:, 0:D_FF] = _conv_grads(dgate, jnp.where(last, 0.0, dgate_n), g0_ref[...].astype(F32), wg_ref,
                                        acc_ref, FFN_K, slice(0, D_FF)).astype(_ACT)
        du_ref[:, D_FF:2 * D_FF] = _conv_grads(dval, jnp.where(last, 0.0, dval_n), v0_ref[...].astype(F32), wv_ref,
                                               acc_ref, FFN_K, slice(D_FF, 2 * D_FF)).astype(_ACT)

    cur = lambda col: pl.BlockSpec((tm, D_FF), lambda i: (i, col))
    nxt = lambda col: _next_halo(tm, t, D_FF, lambda i: (i, col))
    par = lambda col: pl.BlockSpec((SUB, D_FF), lambda i: (0, col))
    return pl.pallas_call(
        body, name=name,
        out_shape=(jax.ShapeDtypeStruct((t, 2 * D_FF), _ACT), jax.ShapeDtypeStruct((SUB, 2 * D_FF), F32)),
        grid=(nt,),
        in_specs=[cur(0), cur(1), cur(0), nxt(0), cur(1), nxt(1), par(0), par(1), cur(0), nxt(0)],
        out_specs=(pl.BlockSpec((tm, 2 * D_FF), lambda i: (i, 0)),
                   pl.BlockSpec((SUB, 2 * D_FF), lambda i: (0, 0))),
        compiler_params=_cp(("arbitrary",)),
    )(u0, u0, u, u, u, u, cw8, cw8, d_a, d_a)


_REST = ("w_branch_ssm", "w_branch_attn", "w_mix_out", "w_up", "w_down")


def _mm_side(*args, side, **kw):
    if side is None:
        return _mm(*args, **kw), []
    return _mm(*args, side=side, **kw)


def _local_step(x, target, wts, ex):
    t = x.shape[0]
    wp = wts["wp"]
    scw = wts["ssm_conv_w"]
    scb = wts["ssm_conv_b"]
    fcw8 = _rows8(wts["ffn_conv_w"])
    fcb8 = _rows8(wts["ffn_conv_b"])
    pad_lane = lambda p: jnp.concatenate([p.astype(F32), jnp.zeros((1, LANE - p.shape[1]), F32)], axis=1)
    dtb8 = _rows8(pad_lane(wts["ssm_dt_bias"]))
    alog8 = _rows8(pad_lane(wts["ssm_a_log"]))
    dsk8 = _rows8(pad_lane(wts["ssm_d"]))
    bias_table = jnp.concatenate([wts["rel_bias"].T.astype(F32), wts["attn_sinks"].T.astype(F32),
                                  jnp.zeros((A_HEADS, BIAS_ROWS - REL_BUCKETS - 1), F32)], axis=1)
    nw8 = _rows8(wts["ssm_norm_w"])
    bg8 = _rows8(wts["b_gate"])
    g1_8, b1_8, g2_8, b2_8 = (_rows8(wts[k]) for k in ("ln1_g", "ln1_b", "ln2_g", "ln2_b"))
    xs_w8, xs_b8 = _rows8(scw[:, :D_INNER]), _rows8(scb[:, :D_INNER])
    bc_w8, bc_b8 = _rows8(scw[:, D_INNER:]), _rows8(scb[:, D_INNER:])

    x_bf = x.astype(_ACT)
    proj, stacks = _mm_side(x_bf, wp[:, :P_MAIN], "mm_in", out_dtype=_ACT, side=ex.gather_rest())
    wts = dict(wts, **ex.rest_weights(stacks))
    w_bs, w_ba, w_mix, w_up, w_dn = (wts[k] for k in _REST)
    tail = _mm(x_bf, wp[:, P_MAIN:], "mm_in_tail")
    xs_c, xs_pre = _conv_silu_fwd(proj, P_XS // _TC, D_INNER // _TC, xs_w8, xs_b8, "conv_xs_fwd")
    bc_c, bc_pre = _conv_silu_fwd(proj, P_BC // _TC, 1024 // _TC, bc_w8, bc_b8, "conv_bc_fwd")
    y_ssd, y_ssm, hprev = _ssd_fwd(xs_c, bc_c, proj, tail, dtb8, alog8, dsk8, nw8, "ssd_fwd")
    bias = _bias_expand(bias_table, "bias_expand").reshape(2, A_HEADS, WIN, 2 * WIN)
    y_attn = _attn_fwd(proj, tail, bias, "attn_fwd")
    bs = _mm(y_ssm, w_bs, "mm_bs", out_dtype=_ACT)
    ba = _mm(y_attn, w_ba, "mm_ba", out_dtype=_ACT)
    merged = _merge_fwd(bs, ba, proj, bg8, "merge_fwd")
    mix = _mm(merged, w_mix, "mm_mix", out_dtype=_ACT)
    xhat1, h1_bf, rstd1 = _ln1_fwd(x, mix, g1_8, b1_8, "ln1_fwd")
    u0 = _mm(h1_bf, w_up, "mm_up", out_dtype=_ACT)
    act, u_conv = _ffn_act_fwd(u0, fcw8, fcb8, "ffn_act_fwd")
    ffn = _mm(act, w_dn, "mm_down", out_dtype=_ACT)
    d_r2, d_r2_bf, acc_ln2 = _ln2_loss(xhat1, ffn, target, g1_8, b1_8, g2_8, b2_8, "ln2_loss")
    d_w_dn = _mm(act, d_r2_bf, "mm_dw_down", trans_a=True)
    d_act = _mm(d_r2_bf, w_dn.T, "mm_d_act", out_dtype=_ACT)
    d_u0, acc_ffn = _ffn_act_bwd(u0, u_conv, fcw8, d_act, "ffn_act_bwd")
    d_w_up = _mm(h1_bf, d_u0, "mm_dw_up", trans_a=True)
    d_h1_ffn = _mm(d_u0, w_up.T, "mm_d_h1", out_dtype=_ACT)
    d_r1, d_r1_bf, acc_ln1 = _ln1_bwd(d_r2, d_h1_ffn, xhat1, rstd1, g1_8, "ln1_bwd")
    d_w_mix = _mm(merged, d_r1_bf, "mm_dw_mix", trans_a=True)
    d_merged = _mm(d_r1_bf, w_mix.T, "mm_d_merged", out_dtype=_ACT)
    d_bs, d_ba, d_gates, acc_bg = _merge_bwd(d_merged, bs, ba, proj, bg8, "merge_bwd")
    d_w_bs = _mm(y_ssm, d_bs, "mm_dw_bs", trans_a=True)
    d_w_ba = _mm(y_attn, d_ba, "mm_dw_ba", trans_a=True)
    d_y_ssm = _mm(d_bs, w_bs.T, "mm_d_yssm", out_dtype=_ACT)
    d_y_attn = _mm(d_ba, w_ba.T, "mm_d_yattn", out_dtype=_ACT)
    d_q, d_k, d_v, d_bias = _attn_bwd(proj, tail, bias, y_attn, d_y_attn, "attn_bwd")
    d_table = _bias_reduce(d_bias.reshape(A_HEADS, WIN * 2 * WIN), "bias_reduce")
    d_xs_c, d_bc_c, d_z, d_dt, acc_ssd, acc_nw = _ssd_bwd(
        d_y_ssm, y_ssd, xs_c, bc_c, proj, tail, hprev, dtb8, alog8, dsk8, nw8, "ssd_bwd")
    d_xs, acc_xs = _conv_silu_bwd(proj, P_XS // _TC, D_INNER // _TC, xs_pre, xs_w8, d_xs_c, "conv_xs_bwd")
    d_bc, acc_bc = _conv_silu_bwd(proj, P_BC // _TC, 1024 // _TC, bc_pre, bc_w8, d_bc_c, "conv_bc_bwd")
    d_proj = jnp.concatenate([d_z, d_xs, d_gates, d_q, d_bc, d_k, d_v, d_dt,
                              jnp.zeros((t, P_W - P_DT - LANE), _ACT)], axis=1)
    grads = {"w_branch_ssm": d_w_bs, "w_branch_attn": d_w_ba, "w_mix_out": d_w_mix, "w_up": d_w_up, "w_down": d_w_dn}
    d_wp, landed_rest = _mm_side(x_bf, d_proj, "mm_dw_in", trans_a=True, side=ex.reduce_job(grads))
    d_x, landed_in = _mm_side(d_proj, wp.T, "mm_d_x", res=d_r1, res_scale=ALPHA, side=ex.reduce_job({"wp": d_wp}))
    grads.update({
        "wp": d_wp,
        "ssm_conv_w": jnp.concatenate([acc_xs[0:SSM_K], acc_bc[0:SSM_K]], axis=1),
        "ffn_conv_w": acc_ffn[0:FFN_K],
    })
    small = {
        "rel_bias": d_table[:, 0:REL_BUCKETS].T,
        "b_gate": acc_bg[0:1],
        "ssm_conv_b": jnp.concatenate([acc_xs[SSM_K:SSM_K + 1], acc_bc[SSM_K:SSM_K + 1]], axis=1),
        "ssm_dt_bias": acc_ssd[0:1, 0:N_HEADS], "ssm_a_log": acc_ssd[1:2, 0:N_HEADS], "ssm_d": acc_ssd[2:3, 0:N_HEADS],
        "ssm_norm_w": acc_nw[0:1],
        "attn_sinks": d_table[:, REL_BUCKETS:REL_BUCKETS + 1].T,
        "ln1_g": acc_ln1[0:1], "ln1_b": acc_ln1[1:2],
        "ffn_conv_b": acc_ffn[FFN_K:FFN_K + 1],
        "ln2_g": acc_ln2[0:1], "ln2_b": acc_ln2[1:2],
        "loss_lanes": acc_ln2[2:3],
    }
    return d_x, grads, small, landed_rest + landed_in


_MATS = (("w_in", (1024, 2120), 1), ("w_branch_ssm", (512, 1024), 0), ("w_branch_attn", (256, 1024), 0),
         ("w_mix_out", (256, 1024), 0), ("w_up", (1024, 1408), 1), ("w_down", (704, 1024), 0))
_CONVS = (("ssm_conv_w", (4, 768)), ("ffn_conv_w", (3, 1408)))
_CONV_ROWS = 64

_SMALL = (("rel_bias", (32, 16)), ("b_gate", (1, 2048)), ("ssm_conv_b", (1, 3072)), ("ssm_dt_bias", (1, 32)),
          ("ssm_a_log", (1, 32)), ("ssm_d", (1, 32)), ("ssm_norm_w", (1, 2048)), ("attn_sinks", (1, 16)),
          ("ln1_g", (1, 1024)), ("ln1_b", (1, 1024)), ("ffn_conv_b", (1, 5632)), ("ln2_g", (1, 1024)),
          ("ln2_b", (1, 1024)), ("g_ssm_conv_w", (4, 3072)), ("g_ffn_conv_w", (3, 5632)), ("loss_lanes", (1, 1024)))


def _small_rows(shape):
    rows = -(-(shape[0] * shape[1]) // LANE)
    return -(-rows // SUB) * SUB


def _as_rows(a, rows, dtype):
    flat = a.reshape(-1).astype(dtype)
    flat = jnp.concatenate([flat, jnp.zeros((rows * LANE - flat.shape[0],), dtype)])
    return flat.reshape(rows, LANE)


def _pack_small(parts):
    blocks = [_as_rows(parts[n], _small_rows(s), F32) if n in parts else jnp.zeros((_small_rows(s), LANE), F32)
              for n, s in _SMALL]
    return jnp.concatenate(blocks, axis=0)


def _unpack_small(packed):
    out, at = {}, 0
    for n, s in _SMALL:
        rows = _small_rows(s)
        out[n] = packed[at:at + rows].reshape(-1)[:s[0] * s[1]].reshape(s)
        at += rows
    return out


def _to_stack(full, shape, axis):
    if axis == 0:
        return full.reshape((N_CHIPS,) + shape)
    return jnp.transpose(full.reshape(shape[0], N_CHIPS, shape[1]), (1, 0, 2))


def _from_stack(stack, axis):
    n, r, c = stack.shape
    if axis == 0:
        return stack.reshape(n * r, c)
    return jnp.transpose(stack, (1, 0, 2)).reshape(r, n * c)


_IN_SHARD = IN_COLS // N_CHIPS


def _cols_of_stack(stack, o, w):
    parts = []
    while w > 0:
        j, a = divmod(o, _IN_SHARD)
        n = min(w, _IN_SHARD - a)
        parts.append(stack[j][:, a:a + n])
        o, w = o + n, w - n
    return parts


def _pack_w_in_stack(stack):
    cols, at = [], 0
    for o, w, pk in sorted(_PIECES, key=lambda p: p[2]):
        if pk > at:
            cols.append(jnp.zeros((stack.shape[1], pk - at), stack.dtype))
        cols += _cols_of_stack(stack, o, w)
        at = pk + w
    cols.append(jnp.zeros((stack.shape[1], P_W - at), stack.dtype))
    return jnp.concatenate(cols, axis=1)


def _unpack_w_in_stack(wp):
    slabs = []
    for j in range(N_CHIPS):
        lo, hi = j * _IN_SHARD, (j + 1) * _IN_SHARD
        cols = []
        for o, w, pk in sorted(_PIECES):
            a, b = max(o, lo), min(o + w, hi)
            if a < b:
                cols.append(wp[:, pk + a - o:pk + b - o])
        slabs.append(jnp.concatenate(cols, axis=1))
    return jnp.stack(slabs)


_MESH = pl.DeviceIdType.MESH
_HBM = pl.BlockSpec(memory_space=pltpu.HBM)


def _position():
    return lax.axis_index("x"), lax.axis_index("y"), lax.axis_index("c")


def _other_chips(x, y):
    return ((1 - x, y), (x, 1 - y), (1 - x, 1 - y))


def _remote(src, dst, send_sem, recv_sem, to):
    return pltpu.make_async_remote_copy(src_ref=src, dst_ref=dst, send_sem=send_sem, recv_sem=recv_sem,
                                        device_id=to, device_id_type=_MESH)


def _run_job(job, name):
    n_in, n_out = len(job.inputs), len(job.out_shape)

    def body(*refs):
        parts = (refs[:n_in], refs[n_in:n_in + n_out], refs[n_in + n_out:])
        job.start(*parts)
        job.finish(*parts)

    return pl.pallas_call(body, name=name, out_shape=list(job.out_shape), in_specs=[_HBM] * n_in,
                          out_specs=[_HBM] * n_out, scratch_shapes=list(job.sems))(*job.inputs)


def _gather_job(shards, copy_own=True):
    n = len(shards)

    def plan(s_refs, o_refs, sems):
        send_sems, recv_sems, local_sems = sems
        x, y, c = _position()
        me = 2 * x + y
        sib = (x, y, 1 - c)
        chips = _other_chips(x, y)

        def copy(m, k, chip_idx, half, to, src=None):
            dst = o_refs[m].at[chip_idx, half]
            return _remote(dst if src is None else src, dst, send_sems.at[6 * m + k], recv_sems.at[6 * m + k], to)

        local = [pltpu.make_async_copy(s_refs[m], o_refs[m].at[me], local_sems.at[m]) for m in range(n)]
        local = local if copy_own else []
        first = [copy(m, i, me, c, (cx, cy, c), src=s_refs[m].at[c])
                 for i, (cx, cy) in enumerate(chips) for m in range(n)]
        return c, sib, chips, copy, local, first

    def start(s_refs, o_refs, sems):
        _, _, _, _, local, first = plan(s_refs, o_refs, sems)
        for cp in local + first:
            cp.start()

    def finish(s_refs, o_refs, sems):
        c, sib, chips, copy, local, first = plan(s_refs, o_refs, sems)
        passed = []
        for i, (cx, cy) in enumerate(chips):
            for m in range(n):
                copy(m, i, 2 * cx + cy, c, sib).wait_recv()
                passed.append(copy(m, 3 + i, 2 * cx + cy, c, sib))
                passed[-1].start()
        for i, (cx, cy) in enumerate(chips):
            for m in range(n):
                copy(m, 3 + i, 2 * cx + cy, 1 - c, sib).wait_recv()
        for cp in first + passed:
            cp.wait_send()
        for cp in local:
            cp.wait()

    return _SideJob(
        inputs=list(shards), out_shape=[jax.ShapeDtypeStruct((N_CHIPS,) + s.shape, s.dtype) for s in shards],
        sems=[pltpu.SemaphoreType.DMA((6 * n,)), pltpu.SemaphoreType.DMA((6 * n,)), pltpu.SemaphoreType.DMA((n,))],
        start=start, finish=finish)


def _swap_halves(gs, name):
    n = len(gs)

    def body(*refs):
        g_refs, o_refs = refs[:n], refs[n:2 * n]
        send_sems, recv_sems = refs[2 * n:]
        x, y, c = _position()
        cps = [_remote(g_refs[m].at[j, 1 - c], o_refs[m].at[j], send_sems.at[N_CHIPS * m + j],
                       recv_sems.at[N_CHIPS * m + j], (x, y, 1 - c)) for m in range(n) for j in range(N_CHIPS)]
        for cp in cps:
            cp.start()
        for cp in cps:
            cp.wait()

    return pl.pallas_call(
        body, name=name,
        out_shape=[jax.ShapeDtypeStruct((N_CHIPS,) + g.shape[2:], g.dtype) for g in gs],
        in_specs=[_HBM] * n, out_specs=[_HBM] * n,
        scratch_shapes=[pltpu.SemaphoreType.DMA((N_CHIPS * n,)), pltpu.SemaphoreType.DMA((N_CHIPS * n,))],
    )(*gs)


def _scatter_job(ps):
    n = len(ps)

    def copies(p_refs, o_refs, sems):
        send_sems, recv_sems = sems
        x, y, c = _position()
        return [_remote(p_refs[m].at[2 * cx + cy], o_refs[m].at[i], send_sems.at[3 * m + i], recv_sems.at[3 * m + i],
                        (cx, cy, c)) for i, (cx, cy) in enumerate(_other_chips(x, y)) for m in range(n)]

    def start(*parts):
        for cp in copies(*parts):
            cp.start()

    def finish(*parts):
        for cp in copies(*parts):
            cp.wait()

    return _SideJob(
        inputs=list(ps), out_shape=[jax.ShapeDtypeStruct((N_CHIPS - 1,) + p.shape[1:], p.dtype) for p in ps],
        sems=[pltpu.SemaphoreType.DMA((3 * n,)), pltpu.SemaphoreType.DMA((3 * n,))], start=start, finish=finish)


def _join_halves(fulls):
    n = len(fulls)

    def body(*refs):
        o_refs = refs[n:2 * n]
        send_sems, recv_sems = refs[2 * n:]
        x, y, c = _position()
        cps = [_remote(o_refs[m].at[c], o_refs[m].at[c], send_sems.at[m], recv_sems.at[m], (x, y, 1 - c))
               for m in range(n)]
        for cp in cps:
            cp.start()
        for cp in cps:
            cp.wait()

    return pl.pallas_call(
        body, name="join_halves",
        out_shape=[jax.ShapeDtypeStruct(f.shape, f.dtype) for f in fulls],
        in_specs=[_HBM] * n, out_specs=[_HBM] * n, input_output_aliases={m: m for m in range(n)},
        scratch_shapes=[pltpu.SemaphoreType.DMA((n,)), pltpu.SemaphoreType.DMA((n,))],
    )(*fulls)


def _allgather_small(mine, name):
    m_per, n = mine.shape

    def body(x_ref, out_ref, send_sems, recv_sems, local_sem):
        x, y, c = _position()
        me, sibling = (x, y, c), (x, y, 1 - c)
        chips = _other_chips(x, y)

        def rows(px, py, pc):
            return out_ref.at[pl.ds((4 * px + 2 * py + pc) * m_per, m_per), :]

        def copy(k, block, to, src=None):
            return pltpu.make_async_remote_copy(src_ref=rows(*block) if src is None else src, dst_ref=rows(*block),
                                                send_sem=send_sems.at[k], recv_sem=recv_sems.at[k],
                                                device_id=to, device_id_type=_MESH)

        own = pltpu.make_async_copy(x_ref, rows(*me), local_sem)
        own.start()
        first = [copy(0, me, sibling, src=x_ref)]
        first += [copy(1 + j, me, (*chip, c), src=x_ref) for j, chip in enumerate(chips)]
        for cp in first:
            cp.start()
        passed = [copy(4 + j, (*chip, c), sibling) for j, chip in enumerate(chips)]
        for j, chip in enumerate(chips):
            copy(1 + j, (*chip, c), me).wait_recv()
            passed[j].start()
        copy(0, sibling, me).wait_recv()
        for j, chip in enumerate(chips):
            copy(4 + j, (*chip, 1 - c), me).wait_recv()
        for cp in first + passed:
            cp.wait_send()
        own.wait()

    return pl.pallas_call(
        body, name=name, out_shape=jax.ShapeDtypeStruct((N_DEV * m_per, n), mine.dtype),
        in_specs=[pl.BlockSpec(memory_space=pltpu.VMEM)], out_specs=pl.BlockSpec(memory_space=pltpu.VMEM),
        scratch_shapes=[pltpu.SemaphoreType.DMA((7,)), pltpu.SemaphoreType.DMA((7,)), pltpu.SemaphoreType.DMA],
    )(mine)


_ADD_BLOCK_BYTES = 3 << 20


def _add_rows(hr, cols):
    if hr * cols * 4 <= _ADD_BLOCK_BYTES:
        return hr
    return _pick(hr, (256, 128, 64, 32, 16))


def _add_own_half(g, recv, c_idx, name):
    nseg, _, hr, cols = g.shape
    tr = _add_rows(hr, cols)

    def body(c_ref, g_ref, r_ref, o_ref, ob_ref):
        s = g_ref[...] + r_ref[...]
        o_ref[...] = s
        ob_ref[...] = s.astype(jnp.bfloat16)

    blk = pl.BlockSpec((None, tr, cols), lambda j, i, c_ref: (j, i, 0))
    return pl.pallas_call(
        body, name=name,
        out_shape=(jax.ShapeDtypeStruct((nseg, hr, cols), F32), jax.ShapeDtypeStruct((nseg, hr, cols), jnp.bfloat16)),
        grid_spec=pltpu.PrefetchScalarGridSpec(
            num_scalar_prefetch=1, grid=(nseg, hr // tr),
            in_specs=[pl.BlockSpec((None, None, tr, cols), lambda j, i, c_ref: (j, c_ref[0], i, 0)), blk],
            out_specs=(blk, blk)),
        compiler_params=_cp(("parallel", "parallel")),
    )(c_idx, g, recv)


def _add_chips(p, recv, chip_idx, c_idx, name):
    _, hr, cols = p.shape
    tr = _add_rows(hr, cols)

    def body(j_ref, c_ref, p_ref, r_ref, o_ref):
        o_ref[...] = ((p_ref[...] + r_ref[0].astype(F32)) + r_ref[1].astype(F32)) + r_ref[2].astype(F32)

    return pl.pallas_call(
        body, name=name, out_shape=jax.ShapeDtypeStruct((2, hr, cols), F32),
        grid_spec=pltpu.PrefetchScalarGridSpec(
            num_scalar_prefetch=2, grid=(hr // tr,),
            in_specs=[pl.BlockSpec((None, tr, cols), lambda i, j_ref, c_ref: (j_ref[0], i, 0)),
                      pl.BlockSpec((N_CHIPS - 1, tr, cols), lambda i, j_ref, c_ref: (0, i, 0))],
            out_specs=pl.BlockSpec((None, tr, cols), lambda i, j_ref, c_ref: (c_ref[0], i, 0))),
        compiler_params=_cp(("parallel",)),
    )(chip_idx, c_idx, p, recv)


def _adam_math(w, g, m, v):
    m = ADAM_B1 * m + (1.0 - ADAM_B1) * g
    v = ADAM_B2 * v + (1.0 - ADAM_B2) * (g * g)
    m_hat = m / (1.0 - ADAM_B1 ** ADAM_STEP)
    v_hat = v / (1.0 - ADAM_B2 ** ADAM_STEP)
    delta = -ADAM_LR * (m_hat / (jnp.sqrt(v_hat) + ADAM_EPS) + ADAM_WD * w)
    return delta, m, v


def _adam_big(w, g, m, v, name):
    rows, cols = w.shape
    tr = _pick(rows, (256, 128, 64, 32, 16, 8)) if rows % SUB == 0 else rows

    def body(w_ref, g_ref, m_ref, v_ref, d_ref, mo_ref, vo_ref):
        d_ref[...], mo_ref[...], vo_ref[...] = _adam_math(w_ref[...], g_ref[...], m_ref[...], v_ref[...])

    blk = pl.BlockSpec((tr, cols), lambda i: (i, 0))
    shp = jax.ShapeDtypeStruct((rows, cols), F32)
    return pl.pallas_call(
        body, name=name, out_shape=(shp, shp, shp), grid=(rows // tr,),
        in_specs=[blk, blk, blk, blk], out_specs=(blk, blk, blk), compiler_params=_cp(("parallel",)),
    )(w, g, m, v)


def _adam_small(w, gathered, m, v):
    rows = w.shape[0]

    def body(w_ref, a_ref, m_ref, v_ref, g_ref, d_ref, mo_ref, vo_ref):
        g = a_ref[0:rows, :]
        for k in range(1, N_DEV):
            g = g + a_ref[k * rows:(k + 1) * rows, :]
        g_ref[...] = g
        d_ref[...], mo_ref[...], vo_ref[...] = _adam_math(w_ref[...], g, m_ref[...], v_ref[...])

    shp = jax.ShapeDtypeStruct((rows, LANE), F32)
    return pl.pallas_call(body, name="adam_small", out_shape=(shp, shp, shp, shp), compiler_params=_cp(None))(
        w, gathered, m, v)


_WEIGHTS = ("rel_bias", "w_in", "b_gate", "ssm_conv_w", "ssm_conv_b", "ssm_dt_bias", "ssm_a_log", "ssm_d",
            "ssm_norm_w", "attn_sinks", "w_branch_ssm", "w_branch_attn", "w_mix_out", "ln1_g", "ln1_b", "w_up",
            "ffn_conv_w", "ffn_conv_b", "w_down", "ln2_g", "ln2_b")
_REPLICATED = tuple(n for n, _ in _SMALL[:13])


class _Exchange:
    def __init__(self, w, chip, core):
        self.chip = chip
        self.c_idx = jnp.reshape(core, (1,)).astype(jnp.int32)
        self.chip_idx = jnp.reshape(chip, (1,)).astype(jnp.int32)
        self.shards = {n: w[n].astype(jnp.bfloat16).reshape(2, s[0] // 2, s[1]) for n, s, _ in _MATS}
        self.spec = {n: (s, ax) for n, s, ax in _MATS}
        self.sums = {}

    def _with_own(self, n, stack):
        shape = self.spec[n][0]
        slab = lax.broadcasted_iota(jnp.int32, (N_CHIPS, 1, 1), 0)
        return jnp.where(slab == self.chip, self.shards[n].reshape((1,) + shape), stack.reshape((N_CHIPS,) + shape))

    def w_in_packed(self):
        (stack,) = _run_job(_gather_job([self.shards["w_in"]], copy_own=False), "allgather_w_in")
        return _pack_w_in_stack(self._with_own("w_in", stack))

    def gather_rest(self):
        return _gather_job([self.shards[n] for n in _REST], copy_own=False)

    def rest_weights(self, stacks):
        return {n: _from_stack(self._with_own(n, st), self.spec[n][1]) for n, st in zip(_REST, stacks)}

    def reduce_job(self, grads):
        names, stacks = [], []
        for n, g in grads.items():
            name = "w_in" if n == "wp" else n
            s, ax = self.spec[name]
            st = _unpack_w_in_stack(g) if n == "wp" else _to_stack(g, s, ax)
            names.append(name)
            stacks.append(st.reshape(N_CHIPS, 2, s[0] // 2, s[1]))
        swapped = _swap_halves(stacks, "swap_" + names[0])
        halves = []
        for n, g, r in zip(names, stacks, swapped):
            self.sums[n], bf = _add_own_half(g, r, self.c_idx, "add_own_" + n)
            halves.append(bf)
        return _scatter_job(halves)

    def reduced(self, landed):
        names = list(self.sums)
        reds = [_add_chips(self.sums[n], r, self.chip_idx, self.c_idx, "add_chips_" + n)
                for n, r in zip(names, landed)]
        return {n: g.reshape(self.spec[n][0]) for n, g in zip(names, _join_halves(reds))}


def _step(x, target, w, m, v):
    xi, yi, ci = _position()
    chip = 2 * xi + yi
    ex = _Exchange(w, chip, ci)

    wts = {n: w[n] for n in _REPLICATED}
    wts["wp"] = ex.w_in_packed()
    taps = jnp.concatenate([w[n].astype(F32).reshape(-1) for n, _ in _CONVS])
    taps = _allgather_small(_as_rows(taps, _CONV_ROWS, F32), "allgather_taps")
    taps = taps.reshape(N_CHIPS, 2, _CONV_ROWS * LANE)[:, 0]
    at = 0
    for n, s in _CONVS:
        wts[n] = _from_stack(taps[:, at:at + s[0] * s[1]].reshape((N_CHIPS,) + s), 1)
        at += s[0] * s[1]

    d_x, grads, small, landed = _local_step(x, target, wts, ex)

    outs = {"grad": ex.reduced(landed), "delta": {}, "m": {}, "v": {}}

    small = dict(small, g_ssm_conv_w=grads["ssm_conv_w"], g_ffn_conv_w=grads["ffn_conv_w"])
    all_small = _allgather_small(_pack_small(small), "allgather_small")
    packs = [_pack_small({n: d[n] for n in _REPLICATED}) for d in (w, m, v)]
    g_s, d_s, m_s, v_s = (_unpack_small(a) for a in _adam_small(packs[0], all_small, packs[1], packs[2]))
    for kind, part in (("grad", g_s), ("delta", d_s), ("m", m_s), ("v", v_s)):
        outs[kind].update({n: part[n] for n in _REPLICATED})
    for n, s in _CONVS:
        outs["grad"][n] = lax.dynamic_slice_in_dim(g_s["g_" + n], chip * s[1], s[1], axis=1)
    for n in [n for n, _, _ in _MATS] + [n for n, _ in _CONVS]:
        outs["delta"][n], outs["m"][n], outs["v"][n] = _adam_big(
            w[n].astype(F32), outs["grad"][n], m[n].astype(F32), v[n].astype(F32), "adam_" + n)
    loss = (0.5 / D_MODEL) * jnp.sum(g_s["loss_lanes"])
    return loss, d_x, outs


def kernel(x, rel_bias, w_in, b_gate, ssm_conv_w, ssm_conv_b, ssm_dt_bias, ssm_a_log, ssm_d, ssm_norm_w, attn_sinks, w_branch_ssm, w_branch_attn, w_mix_out, ln1_g, ln1_b, w_up, ffn_conv_w, ffn_conv_b, w_down, ln2_g, ln2_b, loss_target, m_rel_bias, m_w_in, m_b_gate, m_ssm_conv_w, m_ssm_conv_b, m_ssm_dt_bias, m_ssm_a_log, m_ssm_d, m_ssm_norm_w, m_attn_sinks, m_w_branch_ssm, m_w_branch_attn, m_w_mix_out, m_ln1_g, m_ln1_b, m_w_up, m_ffn_conv_w, m_ffn_conv_b, m_w_down, m_ln2_g, m_ln2_b, v_rel_bias, v_w_in, v_b_gate, v_ssm_conv_w, v_ssm_conv_b, v_ssm_dt_bias, v_ssm_a_log, v_ssm_d, v_ssm_norm_w, v_attn_sinks, v_w_branch_ssm, v_w_branch_attn, v_w_mix_out, v_ln1_g, v_ln1_b, v_w_up, v_ffn_conv_w, v_ffn_conv_b, v_w_down, v_ln2_g, v_ln2_b):
    given = dict(locals())
    drop = lambda a, n: a if n == "rel_bias" or a.ndim == 2 else a[0]
    w = {n: drop(given[n], n) for n in _WEIGHTS}
    m = {n: drop(given["m_" + n], n) for n in _WEIGHTS}
    v = {n: drop(given["v_" + n], n) for n in _WEIGHTS}
    loss, d_x, outs = _step(x[0], loss_target[0], w, m, v)
    like = lambda a, n: a.reshape(given[n].shape)
    res = [loss, d_x[None]]
    for kind in ("grad", "delta", "m", "v"):
        res += [like(outs[kind][n], n) for n in _WEIGHTS]
    return tuple(res)
```

```python
import math
from typing import NamedTuple

import numpy as np
import jax
import jax.numpy as jnp
from jax import lax
from jax.experimental import pallas as pl
from jax.experimental.pallas import tpu as pltpu

F32 = jnp.float32
_ACT = jnp.bfloat16
_MXU = jnp.bfloat16

D_MODEL = 1024
D_INNER = 2048
N_HEADS = 32
HEAD_P = 64
N_GROUPS = 4
N_STATE = 128
CHUNK = 128
CONV_DIM = 3072
SSM_K = 4
A_HEADS = 16
A_DH = 64
WIN = 128
REL_BUCKETS = 32
BIAS_ROWS = 64
D_FF = 2816
FFN_K = 3
ALPHA = 2.0 ** 0.25
LN_EPS = 1e-5
RMS_EPS = 1e-5
IN_COLS = 8480
NEG = -1e30

ADAM_LR = 0.001
ADAM_B1 = 0.9
ADAM_B2 = 0.999
ADAM_EPS = 1e-08
ADAM_WD = 0.01
ADAM_STEP = 10

LANE = 128
SUB = 8

P_Z, P_XS, P_G, P_Q, P_BC, P_K, P_V, P_DT = 0, 2048, 4096, 6144, 7168, 8192, 8320, 8448
P_W = 8704
P_MAIN = 8192
T_K, T_V, T_DT = P_K - P_MAIN, P_V - P_MAIN, P_DT - P_MAIN
_PIECES = ((0, 2048, P_Z), (2048, 2048, P_XS), (4096, 1024, P_BC), (5120, 32, P_DT), (5152, 1024, P_Q),
           (6176, 128, P_K), (6304, 128, P_V), (6432, 2048, P_G))

N_CHIPS = 4
N_DEV = 8


def _cp(sem=None, vmem_mb=48):
    return pltpu.CompilerParams(dimension_semantics=sem, vmem_limit_bytes=vmem_mb * 1024 * 1024)


def _pick(n, cands):
    for c in cands:
        if n % c == 0:
            return c
    raise ValueError(f"no block size for {n}")


def _rows8(p):
    k, c = p.shape
    return jnp.concatenate([p.astype(F32), jnp.zeros((SUB - k, c), F32)], axis=0)


class _SideJob(NamedTuple):
    inputs: list
    out_shape: list
    sems: list
    start: object
    finish: object


def _mm(a, b, name, *, trans_a=False, out_dtype=F32, res=None, res_scale=1.0, side=None):
    if trans_a:
        k_dim, m = a.shape
    else:
        m, k_dim = a.shape
    k2, n = b.shape
    assert k_dim == k2, (a.shape, b.shape)
    tm = _pick(m, (1408, 1024, 512, 256, 128))
    tn = _pick(n, (1408, 1024, 512, 256, 128))
    tk = _pick(k_dim, (2816, 2176, 2048, 1024, 512, 256, 128))
    nk = k_dim // tk
    grid = (m // tm, n // tn, nk)
    dn = (((0,), (0,)), ((), ())) if trans_a else (((1,), (0,)), ((), ()))
    n_in = 2 if res is None else 3
    ns_in = len(side.inputs) if side else 0
    ns_out = len(side.out_shape) if side else 0

    def body(*refs):
        a_ref, b_ref = refs[0], refs[1]
        o_ref = refs[n_in + ns_in]
        scratch = refs[n_in + ns_in + 1 + ns_out:]
        job_refs = (refs[n_in:n_in + ns_in], refs[n_in + ns_in + 1:n_in + ns_in + 1 + ns_out],
                    scratch[1:] if nk > 1 else scratch)
        i, j, k = pl.program_id(0), pl.program_id(1), pl.program_id(2)

        def finish(r):
            if res is not None:
                r = r + res_scale * refs[2][...]
            o_ref[...] = r.astype(out_dtype)

        if side:
            @pl.when(jnp.logical_and(jnp.logical_and(i == 0, j == 0), k == 0))
            def _():
                side.start(*job_refs)

        part = lax.dot_general(a_ref[...].astype(_MXU), b_ref[...].astype(_MXU), dn, preferred_element_type=F32)
        if nk == 1:
            finish(part)
        else:
            acc = scratch[0]

            @pl.when(k == 0)
            def _():
                acc[...] = part

            @pl.when(k > 0)
            def _():
                acc[...] += part

            @pl.when(k == nk - 1)
            def _():
                finish(acc[...])

        if side:
            @pl.when(jnp.logical_and(jnp.logical_and(i == grid[0] - 1, j == grid[1] - 1), k == nk - 1))
            def _():
                side.finish(*job_refs)

    if trans_a:
        a_spec = pl.BlockSpec((tk, tm), lambda i, j, k: (k, i))
    else:
        a_spec = pl.BlockSpec((tm, tk), lambda i, j, k: (i, k))
    in_specs = [a_spec, pl.BlockSpec((tk, tn), lambda i, j, k: (k, j))]
    args = [a, b]
    if res is not None:
        in_specs.append(pl.BlockSpec((tm, tn), lambda i, j, k: (i, j)))
        args.append(res)
    out_spec = pl.BlockSpec((tm, tn), lambda i, j, k: (i, j))
    out_shape = jax.ShapeDtypeStruct((m, n), out_dtype)
    scratch_shapes = [pltpu.VMEM((tm, tn), F32)] if nk > 1 else []
    if not side:
        return pl.pallas_call(
            body, name=name, out_shape=out_shape, grid=grid, in_specs=in_specs, out_specs=out_spec,
            scratch_shapes=scratch_shapes, compiler_params=_cp(("parallel", "parallel", "arbitrary")),
        )(*args)
    hbm = pl.BlockSpec(memory_space=pltpu.HBM)
    outs = pl.pallas_call(
        body, name=name, out_shape=[out_shape] + list(side.out_shape), grid=grid,
        in_specs=in_specs + [hbm] * ns_in, out_specs=[out_spec] + [hbm] * ns_out,
        scratch_shapes=scratch_shapes + list(side.sems),
        compiler_params=_cp(("arbitrary", "arbitrary", "arbitrary")),
    )(*args, *side.inputs)
    return outs[0], list(outs[1:])


def _shift_down(cur, prev8, s):
    r = pltpu.roll(cur, s, 0)
    p = pltpu.roll(prev8, s, 0)
    row8 = lax.broadcasted_iota(jnp.int32, (SUB, 1), 0)
    fixed = jnp.where(row8 < s, p, r[0:SUB])
    if cur.shape[0] == SUB:
        return fixed
    return jnp.concatenate([fixed, r[SUB:]], axis=0)


def _shift_up(cur, next8, s):
    tm = cur.shape[0]
    r = pltpu.roll(cur, tm - s, 0)
    p = pltpu.roll(next8, SUB - s, 0)
    row8 = lax.broadcasted_iota(jnp.int32, (SUB, 1), 0)
    fixed = jnp.where(row8 >= SUB - s, p, r[tm - SUB:])
    return jnp.concatenate([r[:tm - SUB], fixed], axis=0)


def _conv_pre(cur, prev8, w_ref, b_row, taps):
    acc = cur * w_ref[taps - 1:taps, :] + b_row
    for s in range(1, taps):
        acc = acc + _shift_down(cur, prev8, s) * w_ref[taps - 1 - s:taps - s, :]
    return acc


def _dot01_r(x, m01, parts=3):
    acc = None
    r = x
    for _ in range(parts):
        hi = r.astype(jnp.bfloat16)
        t = jnp.dot(hi, m01, preferred_element_type=F32)
        acc = t if acc is None else acc + t
        r = r - hi.astype(F32)
    return acc


def _dot01_l(m01, x, parts=3):
    acc = None
    r = x
    for _ in range(parts):
        hi = r.astype(jnp.bfloat16)
        t = jnp.dot(m01, hi, preferred_element_type=F32)
        acc = t if acc is None else acc + t
        r = r - hi.astype(F32)
    return acc


def _dot(a, b):
    return jnp.dot(a.astype(_MXU), b.astype(_MXU), preferred_element_type=F32)


def _dot_nt(a, b):
    return lax.dot_general(a.astype(_MXU), b.astype(_MXU), (((1,), (1,)), ((), ())), preferred_element_type=F32)


def _dot_tn(a, b):
    return lax.dot_general(a.astype(_MXU), b.astype(_MXU), (((0,), (0,)), ((), ())), preferred_element_type=F32)


def _sigmoid(x):
    return 1.0 / (1.0 + jnp.exp(-x))


def _half_masks():
    lane = lax.broadcasted_iota(jnp.int32, (1, LANE), 1)
    lo = (lane < 64).astype(F32)
    return lo, 1.0 - lo


_TC = 1024


def _tm_rows(t):
    return min(512, t)


HALO = 16


def _prev_halo(tm, width, pos):
    def index(*ids):
        i, col = pos(*ids)
        return (jnp.maximum(i * (tm // HALO) - 1, 0), col)
    return pl.BlockSpec((HALO, width), index)


def _next_halo(tm, t, width, pos):
    def index(*ids):
        i, col = pos(*ids)
        return (jnp.minimum((i + 1) * (tm // HALO), t // HALO - 1), col)
    return pl.BlockSpec((HALO, width), index)


def _conv_silu_fwd(proj, colblk0, nblk, w8, b8, name):
    t = proj.shape[0]
    tm = _tm_rows(t)

    def body(c_ref, p_ref, w_ref, b_ref, o_ref, pre_ref):
        i = pl.program_id(1)
        prev8 = jnp.where(i > 0, p_ref[SUB:HALO, :].astype(F32), 0.0)
        pre = _conv_pre(c_ref[...].astype(F32), prev8, w_ref, b_ref[0:1, :], SSM_K)
        o_ref[...] = pre * _sigmoid(pre)
        pre_ref[...] = pre.astype(_ACT)

    blk = pl.BlockSpec((tm, _TC), lambda j, i: (i, j))
    return pl.pallas_call(
        body, name=name,
        out_shape=(jax.ShapeDtypeStruct((t, nblk * _TC), F32), jax.ShapeDtypeStruct((t, nblk * _TC), _ACT)),
        grid=(nblk, t // tm),
        in_specs=[pl.BlockSpec((tm, _TC), lambda j, i: (i, colblk0 + j)),
                  _prev_halo(tm, _TC, lambda j, i: (i, colblk0 + j)),
                  pl.BlockSpec((SUB, _TC), lambda j, i: (0, j)),
                  pl.BlockSpec((SUB, _TC), lambda j, i: (0, j))],
        out_specs=(blk, blk),
        compiler_params=_cp(("parallel", "parallel")),
    )(proj, proj, w8, b8)


def _silu_grad(pre):
    sg = _sigmoid(pre)
    return sg * (1.0 + pre * (1.0 - sg))


def _conv_grads(d, d_next8, cur, w_ref, acc_ref, taps, cols=slice(None)):
    du = d * w_ref[taps - 1:taps, :]
    acc_ref[taps:taps + 1, cols] += jnp.sum(d, axis=0, keepdims=True)
    acc_ref[taps - 1:taps, cols] += jnp.sum(d * cur, axis=0, keepdims=True)
    for s in range(1, taps):
        up = _shift_up(d, d_next8, s)
        du = du + up * w_ref[taps - 1 - s:taps - s, :]
        acc_ref[taps - 1 - s:taps - s, cols] += jnp.sum(up * cur, axis=0, keepdims=True)
    return du


def _conv_silu_bwd(proj, colblk0, nblk, pre, w8, d_out, d_proj, name):
    t = proj.shape[0]
    tm = _tm_rows(t)
    nt = t // tm

    def body(c_ref, pre_ref, pren_ref, w_ref, d_ref, dn_ref, _, du_ref, acc_ref):
        i = pl.program_id(1)

        @pl.when(i == 0)
        def _():
            acc_ref[...] = jnp.zeros_like(acc_ref)

        dpre = d_ref[...].astype(F32) * _silu_grad(pre_ref[...].astype(F32))
        dpre_n = jnp.where(i < nt - 1, dn_ref[0:SUB, :].astype(F32) * _silu_grad(pren_ref[0:SUB, :].astype(F32)), 0.0)
        du_ref[...] = _conv_grads(dpre, dpre_n, c_ref[...].astype(F32), w_ref, acc_ref, SSM_K).astype(_ACT)

    c = nblk * _TC
    blk = pl.BlockSpec((tm, _TC), lambda j, i: (i, j))
    nxt = _next_halo(tm, t, _TC, lambda j, i: (i, j))
    par = pl.BlockSpec((SUB, _TC), lambda j, i: (0, j))
    return pl.pallas_call(
        body, name=name,
        out_shape=(jax.ShapeDtypeStruct(d_proj.shape, d_proj.dtype), jax.ShapeDtypeStruct((SUB, c), F32)),
        grid=(nblk, nt),
        in_specs=[pl.BlockSpec((tm, _TC), lambda j, i: (i, colblk0 + j)), blk, nxt, par, blk, nxt,
                  pl.BlockSpec(memory_space=pl.ANY)],
        out_specs=(pl.BlockSpec((tm, _TC), lambda j, i: (i, colblk0 + j)), par),
        input_output_aliases={6: 0},
        compiler_params=_cp(("parallel", "arbitrary")),
    )(proj, pre, pre, w8, d_out, d_out, d_proj)


def _expand_consts():
    e = np.zeros((LANE, D_INNER), np.float32)
    for h in range(N_HEADS):
        e[h, h * HEAD_P:(h + 1) * HEAD_P] = 1.0
    return jnp.asarray(e, jnp.bfloat16), jnp.asarray(e.T.copy(), jnp.bfloat16)


def _ssd_common(dtr_ref, dtb_ref, alog_ref, e_ref):
    lane = lax.broadcasted_iota(jnp.int32, (1, LANE), 1)
    hm = lane < N_HEADS
    pre = dtr_ref[...] + dtb_ref[0:1, :]
    dt = jnp.where(hm, jnp.maximum(pre, 0.0) + jnp.log(1.0 + jnp.exp(-jnp.abs(pre))), 0.0)
    a_row = jnp.where(hm, -jnp.exp(alog_ref[0:1, :]), 0.0)
    adt = dt * a_row
    r = lax.broadcasted_iota(jnp.int32, (CHUNK, CHUNK), 0)
    c = lax.broadcasted_iota(jnp.int32, (CHUNK, CHUNK), 1)
    causal = r >= c
    acs = _dot01_l(causal.astype(jnp.bfloat16), adt)
    e = e_ref[...]
    acs_x = _dot01_r(acs, e, parts=2)
    dt_x = _dot01_r(dt, e, parts=2)
    return pre, dt, a_row, acs, acs_x, dt_x, causal, hm


def _decay(acs, acs_t, h, causal):
    seg = acs[:, h:h + 1] - acs_t[h:h + 1, :]
    return jnp.exp(jnp.where(causal, seg, NEG))


def _ssd_fwd(xs_c, bc_c, proj, tail, dtb8, alog8, dsk8, nw8, name):
    t = xs_c.shape[0]
    nc = t // CHUNK
    e_bf, _ = _expand_consts()
    gw = D_INNER // N_GROUPS

    def body(xs_ref, bc_ref, dtr_ref, z_ref, dtb_ref, alog_ref, dsk_ref, nw_ref, e_ref,
             y_ref, ys_ref, hp_ref, h_ref):
        c_id = pl.program_id(0)

        @pl.when(c_id == 0)
        def _():
            h_ref[...] = jnp.zeros_like(h_ref)

        _, dt, a_row, acs, acs_x, dt_x, causal, _ = _ssd_common(dtr_ref, dtb_ref, alog_ref, e_ref)
        acs_t = acs.T
        xs = xs_ref[...]
        x_dt = xs * dt_x
        last_x = acs_x[CHUNK - 1:CHUNK, :]
        w_end = jnp.exp(last_x - acs_x)
        e_in = jnp.exp(acs_x)
        d_x = _dot01_r(dsk_ref[...], e_ref[...])[0:1, :]
        hprev = h_ref[...]
        hp_ref[...] = hprev
        lo, hi = _half_masks()
        for g in range(N_GROUPS):
            bg = bc_ref[:, g * N_STATE:(g + 1) * N_STATE]
            cg = bc_ref[:, N_GROUPS * N_STATE + g * N_STATE:N_GROUPS * N_STATE + (g + 1) * N_STATE]
            sl = slice(g * gw, (g + 1) * gw)
            gm = _dot_nt(cg, bg)
            st = _dot(bg.T, x_dt[:, sl] * w_end[:, sl])
            y_off = _dot(cg, hprev[:, sl]) * e_in[:, sl]
            for j in range(gw // LANE):
                h0 = g * (gw // HEAD_P) + 2 * j
                cs = slice(g * gw + j * LANE, g * gw + (j + 1) * LANE)
                xp = x_dt[:, cs]
                m0 = gm * _decay(acs, acs_t, h0, causal)
                m1 = gm * _decay(acs, acs_t, h0 + 1, causal)
                yd = _dot(m0, xp * lo) + _dot(m1, xp * hi)
                y_ref[:, cs] = yd + y_off[:, j * LANE:(j + 1) * LANE] + xs[:, cs] * d_x[:, cs]
            h_ref[:, sl] = hprev[:, sl] * jnp.exp(last_x[:, sl]) + st
        y = y_ref[...]
        z = z_ref[...].astype(F32)
        y2 = y * (z * _sigmoid(z))
        for g in range(N_GROUPS):
            sl = slice(g * gw, (g + 1) * gw)
            yg = y2[:, sl]
            rinv = lax.rsqrt(jnp.mean(yg * yg, axis=-1, keepdims=True) + RMS_EPS)
            ys_ref[:, sl] = (yg * rinv * nw_ref[0:1, sl]).astype(_ACT)

    small = pl.BlockSpec((SUB, LANE), lambda c: (0, 0))
    return pl.pallas_call(
        body, name=name,
        out_shape=(jax.ShapeDtypeStruct((t, D_INNER), F32), jax.ShapeDtypeStruct((t, D_INNER), _ACT),
                   jax.ShapeDtypeStruct((t, D_INNER), F32)),
        grid=(nc,),
        in_specs=[pl.BlockSpec((CHUNK, D_INNER), lambda c: (c, 0)),
                  pl.BlockSpec((CHUNK, 1024), lambda c: (c, 0)),
                  pl.BlockSpec((CHUNK, LANE), lambda c: (c, T_DT // LANE)),
                  pl.BlockSpec((CHUNK, D_INNER), lambda c: (c, P_Z // D_INNER)),
                  small, small, small,
                  pl.BlockSpec((SUB, D_INNER), lambda c: (0, 0)),
                  pl.BlockSpec((LANE, D_INNER), lambda c: (0, 0))],
        out_specs=(pl.BlockSpec((CHUNK, D_INNER), lambda c: (c, 0)),
                   pl.BlockSpec((CHUNK, D_INNER), lambda c: (c, 0)),
                   pl.BlockSpec((N_STATE, D_INNER), lambda c: (c, 0))),
        scratch_shapes=[pltpu.VMEM((N_STATE, D_INNER), F32)],
        compiler_params=_cp(("arbitrary",)),
    )(xs_c, bc_c, tail, proj, dtb8, alog8, dsk8, nw8, e_bf)


def _ssd_bwd(d_ys, y, xs_c, bc_c, proj, tail, hprev_all, dtb8, alog8, dsk8, nw8, d_proj, name):
    t = xs_c.shape[0]
    nc = t // CHUNK
    e_bf, et_bf = _expand_consts()
    gw = D_INNER // N_GROUPS

    def body(dys_ref, y_ref, xs_ref, bc_ref, dtr_ref, z_ref, hp_ref, dtb_ref, alog_ref, dsk_ref, nw_ref,
             e_ref, et_ref, _, dxs_ref, dbc_ref, dz_ref, ddt_ref, acc_ref, dnw_ref, dh_ref, dx_ref):
        step = pl.program_id(0)

        @pl.when(step == 0)
        def _():
            dh_ref[...] = jnp.zeros_like(dh_ref)
            acc_ref[...] = jnp.zeros_like(acc_ref)
            dnw_ref[...] = jnp.zeros_like(dnw_ref)

        pre, dt, a_row, acs, acs_x, dt_x, causal, hm = _ssd_common(dtr_ref, dtb_ref, alog_ref, e_ref)
        acs_t = acs.T
        et = et_ref[...]
        xs = xs_ref[...]
        x_dt = xs * dt_x
        last_x = acs_x[CHUNK - 1:CHUNK, :]
        w_end = jnp.exp(last_x - acs_x)
        e_in = jnp.exp(acs_x)
        e_last = jnp.exp(last_x)
        d_x = _dot01_r(dsk_ref[...], e_ref[...])[0:1, :]

        y = y_ref[...]
        z = z_ref[...].astype(F32)
        sz = _sigmoid(z)
        gz = z * sz
        y2 = y * gz
        dys = dys_ref[...].astype(F32)
        for g in range(N_GROUPS):
            sl = slice(g * gw, (g + 1) * gw)
            yg = y2[:, sl]
            rinv = lax.rsqrt(jnp.mean(yg * yg, axis=-1, keepdims=True) + RMS_EPS)
            nrm = yg * rinv
            dn = dys[:, sl] * nw_ref[0:1, sl]
            dnw_ref[0:1, sl] += jnp.sum(dys[:, sl] * nrm, axis=0, keepdims=True)
            dx_ref[:, sl] = rinv * (dn - nrm * jnp.mean(dn * nrm, axis=-1, keepdims=True))
        dy2 = dx_ref[...]
        dy = dy2 * gz
        dz_ref[...] = (dy2 * y * (sz * (1.0 + z * (1.0 - sz)))).astype(_ACT)

        dh_next = dh_ref[...]
        hprev = hp_ref[...]
        lo, hi = _half_masks()
        r = lax.broadcasted_iota(jnp.int32, (CHUNK, CHUNK), 0)
        c = lax.broadcasted_iota(jnp.int32, (CHUNK, CHUNK), 1)
        from_here = (c >= r).astype(jnp.bfloat16)
        before = c < r
        lane = lax.broadcasted_iota(jnp.int32, (1, LANE), 1)
        da_intra = jnp.zeros((CHUNK, LANE), F32)
        v_seg = jnp.zeros((CHUNK, LANE), F32)
        z_seg = jnp.zeros((CHUNK, LANE), F32)
        tail_parts = []
        for g in range(N_GROUPS):
            bg = bc_ref[:, g * N_STATE:(g + 1) * N_STATE]
            cg = bc_ref[:, N_GROUPS * N_STATE + g * N_STATE:N_GROUPS * N_STATE + (g + 1) * N_STATE]
            sl = slice(g * gw, (g + 1) * gw)
            et_g = et_ref[g * gw:(g + 1) * gw, :]
            gm = _dot_nt(cg, bg)
            dzg = e_in[:, sl] * dy[:, sl]
            dcg = _dot_nt(dzg, hprev[:, sl])
            dh_c = _dot(cg.T, dzg)
            q = _dot(bg, dh_next[:, sl])
            dbg = _dot_nt(x_dt[:, sl] * w_end[:, sl], dh_next[:, sl])
            y_off = _dot(cg, hprev[:, sl]) * e_in[:, sl]
            v_seg = v_seg + _dot01_r(dy[:, sl] * y_off, et_g, parts=1)
            z_seg = z_seg + _dot01_r(w_end[:, sl] * q * x_dt[:, sl], et_g, parts=1)
            dgm = jnp.zeros((CHUNK, CHUNK), F32)
            for j in range(gw // LANE):
                h0 = g * (gw // HEAD_P) + 2 * j
                cs = slice(g * gw + j * LANE, g * gw + (j + 1) * LANE)
                xp = x_dt[:, cs]
                dyp = dy[:, cs]
                dxd = jnp.zeros((CHUNK, LANE), F32)
                for half, msk in ((0, lo), (1, hi)):
                    lam = _decay(acs, acs_t, h0 + half, causal)
                    mm = gm * lam
                    dym = dyp * msk
                    dmm = _dot_nt(dym, xp)
                    dxd = dxd + _dot_tn(mm, dym)
                    dgm = dgm + dmm * lam
                    below = _dot(from_here, dmm * mm)
                    col = jnp.sum(jnp.where(before, below, 0.0), axis=-1, keepdims=True)
                    da_intra = da_intra + jnp.where(lane == h0 + half, col, 0.0)
                dx_ref[:, cs] = dxd + w_end[:, cs] * q[:, j * LANE:(j + 1) * LANE]
            dbc_ref[:, N_GROUPS * N_STATE + g * N_STATE:N_GROUPS * N_STATE + (g + 1) * N_STATE] = dcg + _dot(dgm, bg)
            dbc_ref[:, g * N_STATE:(g + 1) * N_STATE] = dbg + _dot_tn(dgm, cg)
            dh_ref[:, sl] = e_last[:, sl] * dh_next[:, sl] + dh_c
            tail_parts.append(e_last[:, sl] * jnp.sum(dh_next[:, sl] * hprev[:, sl], axis=0, keepdims=True))
        dxt = dx_ref[...]

        u_seg = _dot01_r(xs * dxt, et, parts=1)
        q_full = jnp.concatenate(tail_parts, axis=1)
        t_row = _dot01_r(jnp.broadcast_to(q_full, (SUB, D_INNER)), et)[0:1, :]
        d_alpha = (da_intra + _dot01_l(from_here, v_seg) + _dot01_l(before.astype(jnp.bfloat16), z_seg) + t_row)
        d_dt = a_row * d_alpha + u_seg
        sgp = _sigmoid(pre)
        d_raw = jnp.where(hm, d_dt * sgp, 0.0)
        ddt_ref[...] = d_raw.astype(_ACT)
        acc_ref[0:1, :] += jnp.sum(d_raw, axis=0, keepdims=True)
        acc_ref[1:2, :] += jnp.sum(d_alpha * dt, axis=0, keepdims=True) * a_row
        dd_row = jnp.sum(dy * xs, axis=0, keepdims=True)
        acc_ref[2:3, :] += _dot01_r(jnp.broadcast_to(dd_row, (SUB, D_INNER)), et)[0:1, :]
        dxs_ref[...] = dy * d_x + dxt * dt_x

    rev = lambda c: (nc - 1 - c, 0)
    small = pl.BlockSpec((SUB, LANE), lambda c: (0, 0))
    return pl.pallas_call(
        body, name=name,
        out_shape=(jax.ShapeDtypeStruct((t, D_INNER), F32), jax.ShapeDtypeStruct((t, 1024), F32),
                   jax.ShapeDtypeStruct(d_proj.shape, d_proj.dtype), jax.ShapeDtypeStruct((t, LANE), _ACT),
                   jax.ShapeDtypeStruct((SUB, LANE), F32), jax.ShapeDtypeStruct((SUB, D_INNER), F32)),
        grid=(nc,),
        in_specs=[pl.BlockSpec((CHUNK, D_INNER), rev),
                  pl.BlockSpec((CHUNK, D_INNER), rev),
                  pl.BlockSpec((CHUNK, D_INNER), rev),
                  pl.BlockSpec((CHUNK, 1024), rev),
                  pl.BlockSpec((CHUNK, LANE), lambda c: (nc - 1 - c, T_DT // LANE)),
                  pl.BlockSpec((CHUNK, D_INNER), lambda c: (nc - 1 - c, P_Z // D_INNER)),
                  pl.BlockSpec((N_STATE, D_INNER), rev),
                  small, small, small,
                  pl.BlockSpec((SUB, D_INNER), lambda c: (0, 0)),
                  pl.BlockSpec((LANE, D_INNER), lambda c: (0, 0)),
                  pl.BlockSpec((D_INNER, LANE), lambda c: (0, 0)),
                  pl.BlockSpec(memory_space=pl.ANY)],
        out_specs=(pl.BlockSpec((CHUNK, D_INNER), rev),
                   pl.BlockSpec((CHUNK, 1024), rev),
                   pl.BlockSpec((CHUNK, D_INNER), lambda c: (nc - 1 - c, P_Z // D_INNER)),
                   pl.BlockSpec((CHUNK, LANE), rev),
                   small,
                   pl.BlockSpec((SUB, D_INNER), lambda c: (0, 0))),
        input_output_aliases={13: 2},
        scratch_shapes=[pltpu.VMEM((N_STATE, D_INNER), F32), pltpu.VMEM((CHUNK, D_INNER), F32)],
        compiler_params=_cp(("arbitrary",), vmem_mb=56),
    )(d_ys, y, xs_c, bc_c, tail, proj, hprev_all, dtb8, alog8, dsk8, nw8, e_bf, et_bf, d_proj)


def _rel_tables():
    qi = np.arange(WIN)[:, None] + WIN
    kj = np.arange(2 * WIN)[None, :]
    rel = qi - kj
    n = np.maximum(rel, 0)
    max_exact = REL_BUCKETS // 2
    nf = np.maximum(n, 1).astype(np.float32)
    large = max_exact + (np.log(nf / np.float32(max_exact)) / np.float32(math.log(WIN / max_exact))
                         * np.float32(REL_BUCKETS - max_exact)).astype(np.int32)
    large = np.minimum(large, REL_BUCKETS - 1)
    bucket = np.where(n < max_exact, n, large)
    valid = (rel >= 0) & (rel < WIN)
    sink_col = np.broadcast_to(kj == 0, rel.shape)
    onehot = np.zeros((BIAS_ROWS, WIN * 2 * WIN), np.float32)
    flat_b = np.where(sink_col, REL_BUCKETS, bucket).reshape(-1)
    flat_v = (valid | sink_col).reshape(-1)
    first_v = ((valid & (kj >= WIN)) | sink_col).reshape(-1)
    idx = np.arange(WIN * 2 * WIN)
    onehot[flat_b[flat_v], idx[flat_v]] = 1.0
    return onehot, np.stack([first_v, flat_v]).astype(np.float32)


def _bias_expand(table_t, name):
    onehot, valid = _rel_tables()

    def body(rb_ref, oh_ref, v_ref, o_ref):
        full = _dot01_r(rb_ref[...], oh_ref[...])
        o_ref[0] = jnp.where(v_ref[0:1, :] > 0.5, full, NEG)
        o_ref[1] = jnp.where(v_ref[1:2, :] > 0.5, full, NEG)

    return pl.pallas_call(
        body, name=name, out_shape=jax.ShapeDtypeStruct((2, A_HEADS, WIN * 2 * WIN), F32),
        compiler_params=_cp(None),
    )(table_t, jnp.asarray(onehot, jnp.bfloat16), jnp.asarray(valid, F32))


def _bias_reduce(dbias, name):
    onehot, _ = _rel_tables()

    def body(d_ref, oh_ref, o_ref):
        acc = None
        r = d_ref[...]
        for _ in range(3):
            hi = r.astype(jnp.bfloat16)
            tt = lax.dot_general(hi, oh_ref[...], (((1,), (1,)), ((), ())), preferred_element_type=F32)
            acc = tt if acc is None else acc + tt
            r = r - hi.astype(F32)
        o_ref[...] = acc

    return pl.pallas_call(
        body, name=name, out_shape=jax.ShapeDtypeStruct((A_HEADS, BIAS_ROWS), F32),
        compiler_params=_cp(None),
    )(dbias, jnp.asarray(onehot, jnp.bfloat16))


def _attn_bands(kc_ref, kp_ref, vc_ref, vp_ref, has_prev):
    lo, hi = _half_masks()
    row = lax.broadcasted_iota(jnp.int32, (2 * WIN, 1), 0)
    keep = (row > 0).astype(F32)
    kb = jnp.concatenate([jnp.where(has_prev, kp_ref[...], 0.0), kc_ref[...]], axis=0) * (keep * (A_DH ** -0.5))
    vb = jnp.concatenate([jnp.where(has_prev, vp_ref[...], 0.0), vc_ref[...]], axis=0) * keep
    kr = pltpu.roll(kb, 64, 1)
    vr = pltpu.roll(vb, 64, 1)
    kk = ((kb * lo, kr * hi), (kr * lo, kb * hi))
    vv = ((vb * lo, vr * hi), (vr * lo, vb * hi))
    return kk, vv, (hi, lo)


def _attn_logits(q_ref, kk, lg_ref):
    for h in range(A_HEADS):
        j, half, kv = h // 2, h % 2, h // (A_HEADS // 2)
        lg_ref[h] = _dot_nt(q_ref[:, j * LANE:(j + 1) * LANE], kk[kv][half])


def _attn_fwd(proj, tail, bias, name):
    t = proj.shape[0]
    nb = t // WIN

    def body(q_ref, kc_ref, kp_ref, vc_ref, vp_ref, b_ref, o_ref, lg_ref, p_ref):
        n = pl.program_id(0)
        kk, vv, ones = _attn_bands(kc_ref, kp_ref, vc_ref, vp_ref, n > 0)
        _attn_logits(q_ref, kk, lg_ref)
        for h in range(A_HEADS):
            logits = lg_ref[h] + b_ref[h]
            p_ref[h] = jnp.exp(logits - jnp.max(logits, axis=-1, keepdims=True)).astype(_MXU)
        lane = lax.broadcasted_iota(jnp.int32, (1, LANE), 1)
        for j in range(A_HEADS // 2):
            kv = (2 * j) // (A_HEADS // 2)
            outs = []
            for half in range(2):
                o = jnp.dot(p_ref[2 * j + half], (vv[kv][half] + ones[half]).astype(_MXU), preferred_element_type=F32)
                outs.append(o / pltpu.roll(o, 64, 1))
            o_ref[:, j * LANE:(j + 1) * LANE] = jnp.where(lane < 64, outs[0], outs[1]).astype(_ACT)

    kvspec = lambda col, prev: pl.BlockSpec(
        (WIN, LANE), (lambda n: (jnp.maximum(n - 1, 0), col)) if prev else (lambda n: (n, col)))
    return pl.pallas_call(
        body, name=name, out_shape=jax.ShapeDtypeStruct((t, D_MODEL), _ACT),
        grid=(nb,),
        in_specs=[pl.BlockSpec((WIN, 1024), lambda n: (n, P_Q // 1024)),
                  kvspec(T_K // LANE, False), kvspec(T_K // LANE, True),
                  kvspec(T_V // LANE, False), kvspec(T_V // LANE, True),
                  pl.BlockSpec((None, A_HEADS, WIN, 2 * WIN), lambda n: (jnp.minimum(n, 1), 0, 0, 0))],
        out_specs=pl.BlockSpec((WIN, 1024), lambda n: (n, 0)),
        scratch_shapes=[pltpu.VMEM((A_HEADS, WIN, 2 * WIN), F32), pltpu.VMEM((A_HEADS, WIN, 2 * WIN), _MXU)],
        compiler_params=_cp(("parallel",)),
    )(proj, tail, tail, tail, tail, bias)


def _attn_bwd(proj, tail, bias, y_attn, d_out, d_proj, name):
    t = proj.shape[0]
    nb = t // WIN

    def body(q_ref, kc_ref, kp_ref, vc_ref, vp_ref, b_ref, y_ref, do_ref, _,
             dq_ref, dk_ref, dv_ref, db_ref, ck_ref, cv_ref, lg_ref, dl_ref, p_ref):
        n = pl.program_id(0)

        @pl.when(n == 0)
        def _():
            db_ref[...] = jnp.zeros_like(db_ref)
            ck_ref[...] = jnp.zeros_like(ck_ref)
            cv_ref[...] = jnp.zeros_like(cv_ref)

        @pl.when(n < nb)
        def _():
            kk, vv, _ = _attn_bands(kc_ref, kp_ref, vc_ref, vp_ref, n > 0)
            lo, hi = _half_masks()
            ones_k = jnp.ones((2 * WIN, LANE), jnp.bfloat16)
            ones_d = jnp.ones((LANE, LANE), jnp.bfloat16)
            _attn_logits(q_ref, kk, lg_ref)
            for h in range(A_HEADS):
                j, half, kv = h // 2, h % 2, h // (A_HEADS // 2)
                msk = hi if half else lo
                logits = lg_ref[h] + b_ref[h]
                p = jnp.exp(logits - jnp.max(logits, axis=-1, keepdims=True))
                den = jnp.dot(p.astype(_MXU), ones_k.astype(_MXU), preferred_element_type=F32)
                dop = do_ref[:, j * LANE:(j + 1) * LANE].astype(F32)
                delta = _dot01_r(dop * y_ref[:, j * LANE:(j + 1) * LANE].astype(F32) * msk, ones_d, parts=2)
                inv = 1.0 / den
                probs = p * jnp.concatenate([inv, inv], axis=1)
                dprobs = _dot_nt(dop, vv[kv][half])
                dlog = probs * (dprobs - jnp.concatenate([delta, delta], axis=1))
                db_ref[h] += dlog
                dl_ref[h] = dlog.astype(_MXU)
                p_ref[h] = probs.astype(_MXU)
            dk_t = [[None, None], [None, None]]
            dv_t = [[None, None], [None, None]]
            for j in range(A_HEADS // 2):
                kv = (2 * j) // (A_HEADS // 2)
                qs = q_ref[:, j * LANE:(j + 1) * LANE].astype(F32) * (A_DH ** -0.5)
                dop = do_ref[:, j * LANE:(j + 1) * LANE].astype(F32)
                dq = None
                for half, msk in ((0, lo), (1, hi)):
                    h = 2 * j + half
                    dqh = jnp.dot(dl_ref[h], kk[kv][half].astype(_MXU), preferred_element_type=F32)
                    dq = dqh if dq is None else dq + dqh
                    dkh = lax.dot_general((qs * msk).astype(_MXU), dl_ref[h], (((0,), (0,)), ((), ())),
                                          preferred_element_type=F32)
                    dvh = lax.dot_general((dop * msk).astype(_MXU), p_ref[h], (((0,), (0,)), ((), ())),
                                          preferred_element_type=F32)
                    dk_t[kv][half] = dkh if dk_t[kv][half] is None else dk_t[kv][half] + dkh
                    dv_t[kv][half] = dvh if dv_t[kv][half] is None else dv_t[kv][half] + dvh
                dq_ref[:, j * LANE:(j + 1) * LANE] = dq.astype(_ACT)
            row = lax.broadcasted_iota(jnp.int32, (2 * WIN, 1), 0)

            def band(acc):
                a = (acc[0][0] + pltpu.roll(acc[0][1], 64, 0)) + (pltpu.roll(acc[1][0], 64, 0) + acc[1][1])
                return jnp.where(row > 0, a.T, 0.0)

            dkb = band(dk_t)
            dvb = band(dv_t)
            dk_ref[...] = (ck_ref[...] + dkb[0:WIN]).astype(_ACT)
            dv_ref[...] = (cv_ref[...] + dvb[0:WIN]).astype(_ACT)
            ck_ref[...] = dkb[WIN:]
            cv_ref[...] = dvb[WIN:]

        @pl.when(n == nb)
        def _():
            dk_ref[...] = ck_ref[...].astype(_ACT)
            dv_ref[...] = cv_ref[...].astype(_ACT)

    cur = lambda n: jnp.minimum(n, nb - 1)
    prv = lambda n: jnp.maximum(jnp.minimum(n, nb - 1) - 1, 0)
    kvspec = lambda col, prev: pl.BlockSpec(
        (WIN, LANE), (lambda n: (prv(n), col)) if prev else (lambda n: (cur(n), col)))
    band_shape = (A_HEADS, WIN, 2 * WIN)
    return pl.pallas_call(
        body, name=name,
        out_shape=(jax.ShapeDtypeStruct(d_proj.shape, d_proj.dtype), jax.ShapeDtypeStruct((t, LANE), _ACT),
                   jax.ShapeDtypeStruct((t, LANE), _ACT), jax.ShapeDtypeStruct(band_shape, F32)),
        grid=(nb + 1,),
        in_specs=[pl.BlockSpec((WIN, 1024), lambda n: (cur(n), P_Q // 1024)),
                  kvspec(T_K // LANE, False), kvspec(T_K // LANE, True),
                  kvspec(T_V // LANE, False), kvspec(T_V // LANE, True),
                  pl.BlockSpec((None,) + band_shape, lambda n: (jnp.minimum(n, 1), 0, 0, 0)),
                  pl.BlockSpec((WIN, 1024), lambda n: (cur(n), 0)),
                  pl.BlockSpec((WIN, 1024), lambda n: (cur(n), 0)),
                  pl.BlockSpec(memory_space=pl.ANY)],
        out_specs=(pl.BlockSpec((WIN, 1024), lambda n: (cur(n), P_Q // 1024)),
                   pl.BlockSpec((WIN, LANE), lambda n: (jnp.maximum(n - 1, 0), 0)),
                   pl.BlockSpec((WIN, LANE), lambda n: (jnp.maximum(n - 1, 0), 0)),
                   pl.BlockSpec(band_shape, lambda n: (0, 0, 0))),
        input_output_aliases={8: 0},
        scratch_shapes=[pltpu.VMEM((WIN, LANE), F32), pltpu.VMEM((WIN, LANE), F32),
                        pltpu.VMEM(band_shape, F32), pltpu.VMEM(band_shape, _MXU), pltpu.VMEM(band_shape, _MXU)],
        compiler_params=_cp(("arbitrary",)),
    )(proj, tail, tail, tail, tail, bias, y_attn, d_out, d_proj)


def _merge_fwd(bs, ba, proj, bg8, name):
    t = bs.shape[0]
    tm = _tm_rows(t)

    def body(bs_ref, ba_ref, gs_ref, ga_ref, bgs_ref, bga_ref, o_ref):
        g_s = _sigmoid(gs_ref[...] + bgs_ref[0:1, :])
        g_a = _sigmoid(ga_ref[...] + bga_ref[0:1, :])
        o_ref[...] = (g_s * bs_ref[...] + g_a * ba_ref[...]).astype(_ACT)

    row = lambda col: pl.BlockSpec((tm, 1024), lambda i: (i, col))
    return pl.pallas_call(
        body, name=name, out_shape=jax.ShapeDtypeStruct((t, D_MODEL), _ACT), grid=(t // tm,),
        in_specs=[row(0), row(0), row(P_G // 1024), row(P_G // 1024 + 1),
                  pl.BlockSpec((SUB, 1024), lambda i: (0, 0)), pl.BlockSpec((SUB, 1024), lambda i: (0, 1))],
        out_specs=row(0), compiler_params=_cp(("parallel",)),
    )(bs, ba, proj, proj, bg8, bg8)


def _merge_bwd(d_merged, bs, ba, proj, bg8, name):
    t = bs.shape[0]
    tm = _tm_rows(t)

    def body(dm_ref, bs_ref, ba_ref, gs_ref, ga_ref, bgs_ref, bga_ref, dbs_ref, dba_ref, dg_ref, acc_ref):
        @pl.when(pl.program_id(0) == 0)
        def _():
            acc_ref[...] = jnp.zeros_like(acc_ref)

        dm = dm_ref[...].astype(F32)
        g_s = _sigmoid(gs_ref[...] + bgs_ref[0:1, :])
        g_a = _sigmoid(ga_ref[...] + bga_ref[0:1, :])
        dbs_ref[...] = (dm * g_s).astype(_ACT)
        dba_ref[...] = (dm * g_a).astype(_ACT)
        dgs = dm * bs_ref[...].astype(F32) * g_s * (1.0 - g_s)
        dga = dm * ba_ref[...].astype(F32) * g_a * (1.0 - g_a)
        dg_ref[:, 0:1024] = dgs.astype(_ACT)
        dg_ref[:, 1024:2048] = dga.astype(_ACT)
        acc_ref[0:1, 0:1024] += jnp.sum(dgs, axis=0, keepdims=True)
        acc_ref[0:1, 1024:2048] += jnp.sum(dga, axis=0, keepdims=True)

    row = lambda col: pl.BlockSpec((tm, 1024), lambda i: (i, col))
    return pl.pallas_call(
        body, name=name,
        out_shape=(jax.ShapeDtypeStruct((t, D_MODEL), _ACT), jax.ShapeDtypeStruct((t, D_MODEL), _ACT),
                   jax.ShapeDtypeStruct((t, P_W), _ACT), jax.ShapeDtypeStruct((SUB, 2048), F32)),
        grid=(t // tm,),
        in_specs=[row(0), row(0), row(0), row(P_G // 1024), row(P_G // 1024 + 1),
                  pl.BlockSpec((SUB, 1024), lambda i: (0, 0)), pl.BlockSpec((SUB, 1024), lambda i: (0, 1))],
        out_specs=(row(0), row(0), pl.BlockSpec((tm, 2048), lambda i: (i, P_G // 2048)),
                   pl.BlockSpec((SUB, 2048), lambda i: (0, 0))),
        compiler_params=_cp(("arbitrary",)),
    )(d_merged, bs, ba, proj, proj, bg8, bg8)


def _place_tail(d_k, d_v, d_dt, d_proj, name):
    t = d_k.shape[0]
    tm = _tm_rows(t)
    width = P_W - P_MAIN

    def body(k_ref, v_ref, dt_ref, _, o_ref):
        o_ref[:, T_K:T_K + LANE] = k_ref[...]
        o_ref[:, T_V:T_V + LANE] = v_ref[...]
        o_ref[:, T_DT:T_DT + LANE] = dt_ref[...]
        o_ref[:, T_DT + LANE:width] = jnp.zeros((tm, width - T_DT - LANE), o_ref.dtype)

    blk = pl.BlockSpec((tm, LANE), lambda i: (i, 0))
    return pl.pallas_call(
        body, name=name, out_shape=jax.ShapeDtypeStruct(d_proj.shape, d_proj.dtype), grid=(t // tm,),
        in_specs=[blk, blk, blk, pl.BlockSpec(memory_space=pl.ANY)],
        out_specs=pl.BlockSpec((tm, width), lambda i: (i, P_MAIN // width)),
        input_output_aliases={3: 0}, compiler_params=_cp(("parallel",)),
    )(d_k, d_v, d_dt, d_proj)


def _ln_stats(r):
    mu = jnp.mean(r, axis=-1, keepdims=True)
    xc = r - mu
    var = jnp.mean(xc * xc, axis=-1, keepdims=True)
    rstd = lax.rsqrt(var + LN_EPS)
    return xc * rstd, rstd


def _ln_bwd(dxhat, xhat, rstd):
    return rstd * (dxhat - jnp.mean(dxhat, axis=-1, keepdims=True)
                   - xhat * jnp.mean(dxhat * xhat, axis=-1, keepdims=True))


def _ln1_fwd(x, mix, g8, b8, name):
    t = x.shape[0]
    tm = _tm_rows(t)

    def body(x_ref, m_ref, g_ref, b_ref, xh_ref, h_ref, rs_ref):
        xhat, rstd = _ln_stats(ALPHA * x_ref[...] + m_ref[...])
        xh_ref[...] = xhat
        h_ref[...] = (xhat * g_ref[0:1, :] + b_ref[0:1, :]).astype(_ACT)
        rs_ref[...] = rstd

    row = pl.BlockSpec((tm, D_MODEL), lambda i: (i, 0))
    par = pl.BlockSpec((SUB, D_MODEL), lambda i: (0, 0))
    return pl.pallas_call(
        body, name=name,
        out_shape=(jax.ShapeDtypeStruct((t, D_MODEL), F32), jax.ShapeDtypeStruct((t, D_MODEL), _ACT),
                   jax.ShapeDtypeStruct((t, 1), F32)),
        grid=(t // tm,), in_specs=[row, row, par, par],
        out_specs=(row, row, pl.BlockSpec((tm, 1), lambda i: (i, 0))),
        compiler_params=_cp(("parallel",)),
    )(x, mix, g8, b8)


def _ln2_loss(xhat1, ffn, target, g1_8, b1_8, g2_8, b2_8, name):
    t = xhat1.shape[0]
    tm = _tm_rows(t)

    def body(xh_ref, f_ref, t_ref, g1_ref, b1_ref, g2_ref, b2_ref, d_ref, db_ref, acc_ref):
        @pl.when(pl.program_id(0) == 0)
        def _():
            acc_ref[...] = jnp.zeros_like(acc_ref)

        h1 = xh_ref[...] * g1_ref[0:1, :] + b1_ref[0:1, :]
        xhat, rstd = _ln_stats(ALPHA * h1 + f_ref[...])
        diff = xhat * g2_ref[0:1, :] + b2_ref[0:1, :] - t_ref[...]
        dy = diff * (1.0 / D_MODEL)
        acc_ref[0:1, :] += jnp.sum(dy * xhat, axis=0, keepdims=True)
        acc_ref[1:2, :] += jnp.sum(dy, axis=0, keepdims=True)
        acc_ref[2:3, :] += jnp.sum(diff * diff, axis=0, keepdims=True)
        d = _ln_bwd(dy * g2_ref[0:1, :], xhat, rstd)
        d_ref[...] = d
        db_ref[...] = d.astype(_ACT)

    row = pl.BlockSpec((tm, D_MODEL), lambda i: (i, 0))
    par = pl.BlockSpec((SUB, D_MODEL), lambda i: (0, 0))
    return pl.pallas_call(
        body, name=name,
        out_shape=(jax.ShapeDtypeStruct((t, D_MODEL), F32), jax.ShapeDtypeStruct((t, D_MODEL), _ACT),
                   jax.ShapeDtypeStruct((SUB, D_MODEL), F32)),
        grid=(t // tm,), in_specs=[row, row, row, par, par, par, par],
        out_specs=(row, row, par), compiler_params=_cp(("arbitrary",)),
    )(xhat1, ffn, target, g1_8, b1_8, g2_8, b2_8)


def _ln1_bwd(d_r2, d_h1_ffn, xhat1, rstd1, g1_8, name):
    t = xhat1.shape[0]
    tm = _tm_rows(t)

    def body(d2_ref, df_ref, xh_ref, rs_ref, g_ref, d_ref, db_ref, acc_ref):
        @pl.when(pl.program_id(0) == 0)
        def _():
            acc_ref[...] = jnp.zeros_like(acc_ref)

        dh = ALPHA * d2_ref[...] + df_ref[...]
        xhat = xh_ref[...]
        acc_ref[0:1, :] += jnp.sum(dh * xhat, axis=0, keepdims=True)
        acc_ref[1:2, :] += jnp.sum(dh, axis=0, keepdims=True)
        d = _ln_bwd(dh * g_ref[0:1, :], xhat, rs_ref[...])
        d_ref[...] = d
        db_ref[...] = d.astype(_ACT)

    row = pl.BlockSpec((tm, D_MODEL), lambda i: (i, 0))
    par = pl.BlockSpec((SUB, D_MODEL), lambda i: (0, 0))
    return pl.pallas_call(
        body, name=name,
        out_shape=(jax.ShapeDtypeStruct((t, D_MODEL), F32), jax.ShapeDtypeStruct((t, D_MODEL), _ACT),
                   jax.ShapeDtypeStruct((SUB, D_MODEL), F32)),
        grid=(t // tm,), in_specs=[row, row, row, pl.BlockSpec((tm, 1), lambda i: (i, 0)), par],
        out_specs=(row, row, par), compiler_params=_cp(("arbitrary",)),
    )(d_r2, d_h1_ffn, xhat1, rstd1, g1_8)


def _ffn_tm(t):
    return min(256, t)


def _ffn_act_fwd(u0, cw8, cb8, name):
    t = u0.shape[0]
    tm = _ffn_tm(t)

    def body(g_ref, gp_ref, v_ref, vp_ref, wg_ref, wv_ref, bg_ref, bv_ref, o_ref, u_ref):
        i = pl.program_id(0)
        gprev = jnp.where(i > 0, gp_ref[SUB:HALO, :].astype(F32), 0.0)
        vprev = jnp.where(i > 0, vp_ref[SUB:HALO, :].astype(F32), 0.0)
        gate = _conv_pre(g_ref[...].astype(F32), gprev, wg_ref, bg_ref[0:1, :], FFN_K)
        val = _conv_pre(v_ref[...].astype(F32), vprev, wv_ref, bv_ref[0:1, :], FFN_K)
        o_ref[...] = (gate * _sigmoid(gate) * val).astype(_ACT)
        u_ref[:, 0:D_FF] = gate.astype(_ACT)
        u_ref[:, D_FF:2 * D_FF] = val.astype(_ACT)

    cur = lambda col: pl.BlockSpec((tm, D_FF), lambda i: (i, col))
    prv = lambda col: _prev_halo(tm, D_FF, lambda i: (i, col))
    par = lambda col: pl.BlockSpec((SUB, D_FF), lambda i: (0, col))
    return pl.pallas_call(
        body, name=name,
        out_shape=(jax.ShapeDtypeStruct((t, D_FF), _ACT), jax.ShapeDtypeStruct((t, 2 * D_FF), _ACT)),
        grid=(t // tm,),
        in_specs=[cur(0), prv(0), cur(1), prv(1), par(0), par(1), par(0), par(1)],
        out_specs=(pl.BlockSpec((tm, D_FF), lambda i: (i, 0)), pl.BlockSpec((tm, 2 * D_FF), lambda i: (i, 0))),
        compiler_params=_cp(("parallel",)),
    )(u0, u0, u0, u0, cw8, cw8, cb8, cb8)


def _ffn_act_bwd(u0, u, cw8, d_a, name):
    t = u0.shape[0]
    tm = _ffn_tm(t)
    nt = t // tm

    def body(g0_ref, v0_ref, g_ref, gn_ref, v_ref, vn_ref, wg_ref, wv_ref, da_ref, dan_ref, du_ref, acc_ref):
        i = pl.program_id(0)

        @pl.when(i == 0)
        def _():
            acc_ref[...] = jnp.zeros_like(acc_ref)

        def grads(gate, val, da):
            return da * val * _silu_grad(gate), da * gate * _sigmoid(gate)

        dgate, dval = grads(g_ref[...].astype(F32), v_ref[...].astype(F32), da_ref[...].astype(F32))
        dgate_n, dval_n = grads(gn_ref[0:SUB, :].astype(F32), vn_ref[0:SUB, :].astype(F32),
                                dan_ref[0:SUB, :].astype(F32))
        last = i == nt - 1
        du_ref[:, 0:D_FF] = _conv_grads(dgate, jnp.where(last, 0.0, dgate_n), g0_ref[...].astype(F32), wg_ref,
                                        acc_ref, FFN_K, slice(0, D_FF)).astype(_ACT)
        du_ref[:, D_FF:2 * D_FF] = _conv_grads(dval, jnp.where(last, 0.0, dval_n), v0_ref[...].astype(F32), wv_ref,
                                               acc_ref, FFN_K, slice(D_FF, 2 * D_FF)).astype(_ACT)

    cur = lambda col: pl.BlockSpec((tm, D_FF), lambda i: (i, col))
    nxt = lambda col: _next_halo(tm, t, D_FF, lambda i: (i, col))
    par = lambda col: pl.BlockSpec((SUB, D_FF), lambda i: (0, col))
    return pl.pallas_call(
        body, name=name,
        out_shape=(jax.ShapeDtypeStruct((t, 2 * D_FF), _ACT), jax.ShapeDtypeStruct((SUB, 2 * D_FF), F32)),
        grid=(nt,),
        in_specs=[cur(0), cur(1), cur(0), nxt(0), cur(1), nxt(1), par(0), par(1), cur(0), nxt(0)],
        out_specs=(pl.BlockSpec((tm, 2 * D_FF), lambda i: (i, 0)),
                   pl.BlockSpec((SUB, 2 * D_FF), lambda i: (0, 0))),
        compiler_params=_cp(("arbitrary",)),
    )(u0, u0, u, u, u, u, cw8, cw8, d_a, d_a)


_REST = ("w_branch_ssm", "w_branch_attn", "w_mix_out", "w_up", "w_down")


def _mm_side(*args, side, **kw):
    if side is None:
        return _mm(*args, **kw), []
    return _mm(*args, side=side, **kw)


def _local_step(x, target, wts, ex):
    t = x.shape[0]
    wp = wts["wp"]
    scw = wts["ssm_conv_w"]
    scb = wts["ssm_conv_b"]
    fcw8 = _rows8(wts["ffn_conv_w"])
    fcb8 = _rows8(wts["ffn_conv_b"])
    pad_lane = lambda p: jnp.concatenate([p.astype(F32), jnp.zeros((1, LANE - p.shape[1]), F32)], axis=1)
    dtb8 = _rows8(pad_lane(wts["ssm_dt_bias"]))
    alog8 = _rows8(pad_lane(wts["ssm_a_log"]))
    dsk8 = _rows8(pad_lane(wts["ssm_d"]))
    bias_table = jnp.concatenate([wts["rel_bias"].T.astype(F32), wts["attn_sinks"].T.astype(F32),
                                  jnp.zeros((A_HEADS, BIAS_ROWS - REL_BUCKETS - 1), F32)], axis=1)
    nw8 = _rows8(wts["ssm_norm_w"])
    bg8 = _rows8(wts["b_gate"])
    g1_8, b1_8, g2_8, b2_8 = (_rows8(wts[k]) for k in ("ln1_g", "ln1_b", "ln2_g", "ln2_b"))
    xs_w8, xs_b8 = _rows8(scw[:, :D_INNER]), _rows8(scb[:, :D_INNER])
    bc_w8, bc_b8 = _rows8(scw[:, D_INNER:]), _rows8(scb[:, D_INNER:])

    x_bf = x.astype(_ACT)
    proj, stacks = _mm_side(x_bf, wp[:, :P_MAIN], "mm_in", out_dtype=_ACT, side=ex.gather_rest())
    wts = dict(wts, **ex.rest_weights(stacks))
    w_bs, w_ba, w_mix, w_up, w_dn = (wts[k] for k in _REST)
    tail = _mm(x_bf, wp[:, P_MAIN:], "mm_in_tail")
    xs_c, xs_pre = _conv_silu_fwd(proj, P_XS // _TC, D_INNER // _TC, xs_w8, xs_b8, "conv_xs_fwd")
    bc_c, bc_pre = _conv_silu_fwd(proj, P_BC // _TC, 1024 // _TC, bc_w8, bc_b8, "conv_bc_fwd")
    y_ssd, y_ssm, hprev = _ssd_fwd(xs_c, bc_c, proj, tail, dtb8, alog8, dsk8, nw8, "ssd_fwd")
    bias = _bias_expand(bias_table, "bias_expand").reshape(2, A_HEADS, WIN, 2 * WIN)
    y_attn = _attn_fwd(proj, tail, bias, "attn_fwd")
    bs = _mm(y_ssm, w_bs, "mm_bs", out_dtype=_ACT)
    ba = _mm(y_attn, w_ba, "mm_ba", out_dtype=_ACT)
    merged = _merge_fwd(bs, ba, proj, bg8, "merge_fwd")
    mix = _mm(merged, w_mix, "mm_mix", out_dtype=_ACT)
    xhat1, h1_bf, rstd1 = _ln1_fwd(x, mix, g1_8, b1_8, "ln1_fwd")
    u0 = _mm(h1_bf, w_up, "mm_up", out_dtype=_ACT)
    act, u_conv = _ffn_act_fwd(u0, fcw8, fcb8, "ffn_act_fwd")
    ffn = _mm(act, w_dn, "mm_down", out_dtype=_ACT)
    d_r2, d_r2_bf, acc_ln2 = _ln2_loss(xhat1, ffn, target, g1_8, b1_8, g2_8, b2_8, "ln2_loss")
    d_w_dn = _mm(act, d_r2_bf, "mm_dw_down", trans_a=True)
    d_act = _mm(d_r2_bf, w_dn.T, "mm_d_act", out_dtype=_ACT)
    d_u0, acc_ffn = _ffn_act_bwd(u0, u_conv, fcw8, d_act, "ffn_act_bwd")
    d_w_up = _mm(h1_bf, d_u0, "mm_dw_up", trans_a=True)
    d_h1_ffn = _mm(d_u0, w_up.T, "mm_d_h1", out_dtype=_ACT)
    d_r1, d_r1_bf, acc_ln1 = _ln1_bwd(d_r2, d_h1_ffn, xhat1, rstd1, g1_8, "ln1_bwd")
    d_w_mix = _mm(merged, d_r1_bf, "mm_dw_mix", trans_a=True)
    d_merged = _mm(d_r1_bf, w_mix.T, "mm_d_merged", out_dtype=_ACT)
    d_bs, d_ba, d_proj, acc_bg = _merge_bwd(d_merged, bs, ba, proj, bg8, "merge_bwd")
    d_w_bs = _mm(y_ssm, d_bs, "mm_dw_bs", trans_a=True)
    d_w_ba = _mm(y_attn, d_ba, "mm_dw_ba", trans_a=True)
    d_y_ssm = _mm(d_bs, w_bs.T, "mm_d_yssm", out_dtype=_ACT)
    d_y_attn = _mm(d_ba, w_ba.T, "mm_d_yattn", out_dtype=_ACT)
    d_proj, d_k, d_v, d_bias = _attn_bwd(proj, tail, bias, y_attn, d_y_attn, d_proj, "attn_bwd")
    d_table = _bias_reduce(d_bias.reshape(A_HEADS, WIN * 2 * WIN), "bias_reduce")
    d_xs_c, d_bc_c, d_proj, d_dt, acc_ssd, acc_nw = _ssd_bwd(
        d_y_ssm, y_ssd, xs_c, bc_c, proj, tail, hprev, dtb8, alog8, dsk8, nw8, d_proj, "ssd_bwd")
    d_proj, acc_xs = _conv_silu_bwd(proj, P_XS // _TC, D_INNER // _TC, xs_pre, xs_w8, d_xs_c, d_proj, "conv_xs_bwd")
    d_proj, acc_bc = _conv_silu_bwd(proj, P_BC // _TC, 1024 // _TC, bc_pre, bc_w8, d_bc_c, d_proj, "conv_bc_bwd")
    d_proj = _place_tail(d_k, d_v, d_dt, d_proj, "place_tail")
    grads = {"w_branch_ssm": d_w_bs, "w_branch_attn": d_w_ba, "w_mix_out": d_w_mix, "w_up": d_w_up, "w_down": d_w_dn}
    d_wp, landed_rest = _mm_side(x_bf, d_proj, "mm_dw_in", trans_a=True, side=ex.reduce_job(grads))
    d_x, landed_in = _mm_side(d_proj, wp.T, "mm_d_x", res=d_r1, res_scale=ALPHA, side=ex.reduce_job({"wp": d_wp}))
    grads.update({
        "wp": d_wp,
        "ssm_conv_w": jnp.concatenate([acc_xs[0:SSM_K], acc_bc[0:SSM_K]], axis=1),
        "ffn_conv_w": acc_ffn[0:FFN_K],
    })
    small = {
        "rel_bias": d_table[:, 0:REL_BUCKETS].T,
        "b_gate": acc_bg[0:1],
        "ssm_conv_b": jnp.concatenate([acc_xs[SSM_K:SSM_K + 1], acc_bc[SSM_K:SSM_K + 1]], axis=1),
        "ssm_dt_bias": acc_ssd[0:1, 0:N_HEADS], "ssm_a_log": acc_ssd[1:2, 0:N_HEADS], "ssm_d": acc_ssd[2:3, 0:N_HEADS],
        "ssm_norm_w": acc_nw[0:1],
        "attn_sinks": d_table[:, REL_BUCKETS:REL_BUCKETS + 1].T,
        "ln1_g": acc_ln1[0:1], "ln1_b": acc_ln1[1:2],
        "ffn_conv_b": acc_ffn[FFN_K:FFN_K + 1],
        "ln2_g": acc_ln2[0:1], "ln2_b": acc_ln2[1:2],
        "loss_lanes": acc_ln2[2:3],
    }
    return d_x, grads, small, landed_rest + landed_in


_MATS = (("w_in", (1024, 2120), 1), ("w_branch_ssm", (512, 1024), 0), ("w_branch_attn", (256, 1024), 0),
         ("w_mix_out", (256, 1024), 0), ("w_up", (1024, 1408), 1), ("w_down", (704, 1024), 0))
_CONVS = (("ssm_conv_w", (4, 768)), ("ffn_conv_w", (3, 1408)))
_CONV_ROWS = 64

_SMALL = (("rel_bias", (32, 16)), ("b_gate", (1, 2048)), ("ssm_conv_b", (1, 3072)), ("ssm_dt_bias", (1, 32)),
          ("ssm_a_log", (1, 32)), ("ssm_d", (1, 32)), ("ssm_norm_w", (1, 2048)), ("attn_sinks", (1, 16)),
          ("ln1_g", (1, 1024)), ("ln1_b", (1, 1024)), ("ffn_conv_b", (1, 5632)), ("ln2_g", (1, 1024)),
          ("ln2_b", (1, 1024)), ("g_ssm_conv_w", (4, 3072)), ("g_ffn_conv_w", (3, 5632)), ("loss_lanes", (1, 1024)))


def _small_rows(shape):
    rows = -(-(shape[0] * shape[1]) // LANE)
    return -(-rows // SUB) * SUB


def _as_rows(a, rows, dtype):
    flat = a.reshape(-1).astype(dtype)
    flat = jnp.concatenate([flat, jnp.zeros((rows * LANE - flat.shape[0],), dtype)])
    return flat.reshape(rows, LANE)


def _pack_small(parts):
    blocks = [_as_rows(parts[n], _small_rows(s), F32) if n in parts else jnp.zeros((_small_rows(s), LANE), F32)
              for n, s in _SMALL]
    return jnp.concatenate(blocks, axis=0)


def _unpack_small(packed):
    out, at = {}, 0
    for n, s in _SMALL:
        rows = _small_rows(s)
        out[n] = packed[at:at + rows].reshape(-1)[:s[0] * s[1]].reshape(s)
        at += rows
    return out


def _to_stack(full, shape, axis):
    if axis == 0:
        return full.reshape((N_CHIPS,) + shape)
    return jnp.transpose(full.reshape(shape[0], N_CHIPS, shape[1]), (1, 0, 2))


def _from_stack(stack, axis):
    n, r, c = stack.shape
    if axis == 0:
        return stack.reshape(n * r, c)
    return jnp.transpose(stack, (1, 0, 2)).reshape(r, n * c)


_IN_SHARD = IN_COLS // N_CHIPS


def _cols_of_stack(stack, o, w):
    parts = []
    while w > 0:
        j, a = divmod(o, _IN_SHARD)
        n = min(w, _IN_SHARD - a)
        parts.append(stack[j][:, a:a + n])
        o, w = o + n, w - n
    return parts


def _pack_w_in_stack(stack):
    cols, at = [], 0
    for o, w, pk in sorted(_PIECES, key=lambda p: p[2]):
        if pk > at:
            cols.append(jnp.zeros((stack.shape[1], pk - at), stack.dtype))
        cols += _cols_of_stack(stack, o, w)
        at = pk + w
    cols.append(jnp.zeros((stack.shape[1], P_W - at), stack.dtype))
    return jnp.concatenate(cols, axis=1)


def _unpack_w_in_stack(wp):
    slabs = []
    for j in range(N_CHIPS):
        lo, hi = j * _IN_SHARD, (j + 1) * _IN_SHARD
        cols = []
        for o, w, pk in sorted(_PIECES):
            a, b = max(o, lo), min(o + w, hi)
            if a < b:
                cols.append(wp[:, pk + a - o:pk + b - o])
        slabs.append(jnp.concatenate(cols, axis=1))
    return jnp.stack(slabs)


_MESH = pl.DeviceIdType.MESH
_HBM = pl.BlockSpec(memory_space=pltpu.HBM)


def _position():
    return lax.axis_index("x"), lax.axis_index("y"), lax.axis_index("c")


def _other_chips(x, y):
    return ((1 - x, y), (x, 1 - y), (1 - x, 1 - y))


def _remote(src, dst, send_sem, recv_sem, to):
    return pltpu.make_async_remote_copy(src_ref=src, dst_ref=dst, send_sem=send_sem, recv_sem=recv_sem,
                                        device_id=to, device_id_type=_MESH)


def _run_job(job, name):
    n_in, n_out = len(job.inputs), len(job.out_shape)

    def body(*refs):
        parts = (refs[:n_in], refs[n_in:n_in + n_out], refs[n_in + n_out:])
        job.start(*parts)
        job.finish(*parts)

    return pl.pallas_call(body, name=name, out_shape=list(job.out_shape), in_specs=[_HBM] * n_in,
                          out_specs=[_HBM] * n_out, scratch_shapes=list(job.sems))(*job.inputs)


def _gather_job(shards, copy_own=True):
    n = len(shards)

    def plan(s_refs, o_refs, sems):
        send_sems, recv_sems, local_sems = sems
        x, y, c = _position()
        me = 2 * x + y
        sib = (x, y, 1 - c)
        chips = _other_chips(x, y)

        def copy(m, k, chip_idx, half, to, src=None):
            dst = o_refs[m].at[chip_idx, half]
            return _remote(dst if src is None else src, dst, send_sems.at[6 * m + k], recv_sems.at[6 * m + k], to)

        local = [pltpu.make_async_copy(s_refs[m], o_refs[m].at[me], local_sems.at[m]) for m in range(n)]
        local = local if copy_own else []
        first = [copy(m, i, me, c, (cx, cy, c), src=s_refs[m].at[c])
                 for i, (cx, cy) in enumerate(chips) for m in range(n)]
        return c, sib, chips, copy, local, first

    def start(s_refs, o_refs, sems):
        _, _, _, _, local, first = plan(s_refs, o_refs, sems)
        for cp in local + first:
            cp.start()

    def finish(s_refs, o_refs, sems):
        c, sib, chips, copy, local, first = plan(s_refs, o_refs, sems)
        passed = []
        for i, (cx, cy) in enumerate(chips):
            for m in range(n):
                copy(m, i, 2 * cx + cy, c, sib).wait_recv()
                passed.append(copy(m, 3 + i, 2 * cx + cy, c, sib))
                passed[-1].start()
        for i, (cx, cy) in enumerate(chips):
            for m in range(n):
                copy(m, 3 + i, 2 * cx + cy, 1 - c, sib).wait_recv()
        for cp in first + passed:
            cp.wait_send()
        for cp in local:
            cp.wait()

    return _SideJob(
        inputs=list(shards), out_shape=[jax.ShapeDtypeStruct((N_CHIPS,) + s.shape, s.dtype) for s in shards],
        sems=[pltpu.SemaphoreType.DMA((6 * n,)), pltpu.SemaphoreType.DMA((6 * n,)), pltpu.SemaphoreType.DMA((n,))],
        start=start, finish=finish)


def _swap_halves(gs, name):
    n = len(gs)

    def body(*refs):
        g_refs, o_refs = refs[:n], refs[n:2 * n]
        send_sems, recv_sems = refs[2 * n:]
        x, y, c = _position()
        cps = [_remote(g_refs[m].at[j, 1 - c], o_refs[m].at[j], send_sems.at[N_CHIPS * m + j],
                       recv_sems.at[N_CHIPS * m + j], (x, y, 1 - c)) for m in range(n) for j in range(N_CHIPS)]
        for cp in cps:
            cp.start()
        for cp in cps:
            cp.wait()

    return pl.pallas_call(
        body, name=name,
        out_shape=[jax.ShapeDtypeStruct((N_CHIPS,) + g.shape[2:], g.dtype) for g in gs],
        in_specs=[_HBM] * n, out_specs=[_HBM] * n,
        scratch_shapes=[pltpu.SemaphoreType.DMA((N_CHIPS * n,)), pltpu.SemaphoreType.DMA((N_CHIPS * n,))],
    )(*gs)


def _scatter_job(ps):
    n = len(ps)

    def copies(p_refs, o_refs, sems):
        send_sems, recv_sems = sems
        x, y, c = _position()
        return [_remote(p_refs[m].at[2 * cx + cy], o_refs[m].at[i], send_sems.at[3 * m + i], recv_sems.at[3 * m + i],
                        (cx, cy, c)) for i, (cx, cy) in enumerate(_other_chips(x, y)) for m in range(n)]

    def start(*parts):
        for cp in copies(*parts):
            cp.start()

    def finish(*parts):
        for cp in copies(*parts):
            cp.wait()

    return _SideJob(
        inputs=list(ps), out_shape=[jax.ShapeDtypeStruct((N_CHIPS - 1,) + p.shape[1:], p.dtype) for p in ps],
        sems=[pltpu.SemaphoreType.DMA((3 * n,)), pltpu.SemaphoreType.DMA((3 * n,))], start=start, finish=finish)


def _join_halves(fulls):
    n = len(fulls)

    def body(*refs):
        o_refs = refs[n:2 * n]
        send_sems, recv_sems = refs[2 * n:]
        x, y, c = _position()
        cps = [_remote(o_refs[m].at[c], o_refs[m].at[c], send_sems.at[m], recv_sems.at[m], (x, y, 1 - c))
               for m in range(n)]
        for cp in cps:
            cp.start()
        for cp in cps:
            cp.wait()

    return pl.pallas_call(
        body, name="join_halves",
        out_shape=[jax.ShapeDtypeStruct(f.shape, f.dtype) for f in fulls],
        in_specs=[_HBM] * n, out_specs=[_HBM] * n, input_output_aliases={m: m for m in range(n)},
        scratch_shapes=[pltpu.SemaphoreType.DMA((n,)), pltpu.SemaphoreType.DMA((n,))],
    )(*fulls)


def _allgather_small(mine, name):
    m_per, n = mine.shape

    def body(x_ref, out_ref, send_sems, recv_sems, local_sem):
        x, y, c = _position()
        me, sibling = (x, y, c), (x, y, 1 - c)
        chips = _other_chips(x, y)

        def rows(px, py, pc):
            return out_ref.at[pl.ds((4 * px + 2 * py + pc) * m_per, m_per), :]

        def copy(k, block, to, src=None):
            return pltpu.make_async_remote_copy(src_ref=rows(*block) if src is None else src, dst_ref=rows(*block),
                                                send_sem=send_sems.at[k], recv_sem=recv_sems.at[k],
                                                device_id=to, device_id_type=_MESH)

        own = pltpu.make_async_copy(x_ref, rows(*me), local_sem)
        own.start()
        first = [copy(0, me, sibling, src=x_ref)]
        first += [copy(1 + j, me, (*chip, c), src=x_ref) for j, chip in enumerate(chips)]
        for cp in first:
            cp.start()
        passed = [copy(4 + j, (*chip, c), sibling) for j, chip in enumerate(chips)]
        for j, chip in enumerate(chips):
            copy(1 + j, (*chip, c), me).wait_recv()
            passed[j].start()
        copy(0, sibling, me).wait_recv()
        for j, chip in enumerate(chips):
            copy(4 + j, (*chip, 1 - c), me).wait_recv()
        for cp in first + passed:
            cp.wait_send()
        own.wait()

    return pl.pallas_call(
        body, name=name, out_shape=jax.ShapeDtypeStruct((N_DEV * m_per, n), mine.dtype),
        in_specs=[pl.BlockSpec(memory_space=pltpu.VMEM)], out_specs=pl.BlockSpec(memory_space=pltpu.VMEM),
        scratch_shapes=[pltpu.SemaphoreType.DMA((7,)), pltpu.SemaphoreType.DMA((7,)), pltpu.SemaphoreType.DMA],
    )(mine)


_ADD_BLOCK_BYTES = 3 << 20


def _add_rows(hr, cols):
    if hr * cols * 4 <= _ADD_BLOCK_BYTES:
        return hr
    return _pick(hr, (256, 128, 64, 32, 16))


def _add_own_half(g, recv, c_idx, name):
    nseg, _, hr, cols = g.shape
    tr = _add_rows(hr, cols)

    def body(c_ref, g_ref, r_ref, o_ref, ob_ref):
        s = g_ref[...] + r_ref[...]
        o_ref[...] = s
        ob_ref[...] = s.astype(jnp.bfloat16)

    blk = pl.BlockSpec((None, tr, cols), lambda j, i, c_ref: (j, i, 0))
    return pl.pallas_call(
        body, name=name,
        out_shape=(jax.ShapeDtypeStruct((nseg, hr, cols), F32), jax.ShapeDtypeStruct((nseg, hr, cols), jnp.bfloat16)),
        grid_spec=pltpu.PrefetchScalarGridSpec(
            num_scalar_prefetch=1, grid=(nseg, hr // tr),
            in_specs=[pl.BlockSpec((None, None, tr, cols), lambda j, i, c_ref: (j, c_ref[0], i, 0)), blk],
            out_specs=(blk, blk)),
        compiler_params=_cp(("parallel", "parallel")),
    )(c_idx, g, recv)


def _add_chips(p, recv, chip_idx, c_idx, name):
    _, hr, cols = p.shape
    tr = _add_rows(hr, cols)

    def body(j_ref, c_ref, p_ref, r_ref, o_ref):
        o_ref[...] = ((p_ref[...] + r_ref[0].astype(F32)) + r_ref[1].astype(F32)) + r_ref[2].astype(F32)

    return pl.pallas_call(
        body, name=name, out_shape=jax.ShapeDtypeStruct((2, hr, cols), F32),
        grid_spec=pltpu.PrefetchScalarGridSpec(
            num_scalar_prefetch=2, grid=(hr // tr,),
            in_specs=[pl.BlockSpec((None, tr, cols), lambda i, j_ref, c_ref: (j_ref[0], i, 0)),
                      pl.BlockSpec((N_CHIPS - 1, tr, cols), lambda i, j_ref, c_ref: (0, i, 0))],
            out_specs=pl.BlockSpec((None, tr, cols), lambda i, j_ref, c_ref: (c_ref[0], i, 0))),
        compiler_params=_cp(("parallel",)),
    )(chip_idx, c_idx, p, recv)


def _adam_math(w, g, m, v):
    m = ADAM_B1 * m + (1.0 - ADAM_B1) * g
    v = ADAM_B2 * v + (1.0 - ADAM_B2) * (g * g)
    m_hat = m / (1.0 - ADAM_B1 ** ADAM_STEP)
    v_hat = v / (1.0 - ADAM_B2 ** ADAM_STEP)
    delta = -ADAM_LR * (m_hat / (jnp.sqrt(v_hat) + ADAM_EPS) + ADAM_WD * w)
    return delta, m, v


def _adam_big(w, g, m, v, name):
    rows, cols = w.shape
    tr = _pick(rows, (256, 128, 64, 32, 16, 8)) if rows % SUB == 0 else rows

    def body(w_ref, g_ref, m_ref, v_ref, d_ref, mo_ref, vo_ref):
        d_ref[...], mo_ref[...], vo_ref[...] = _adam_math(w_ref[...], g_ref[...], m_ref[...], v_ref[...])

    blk = pl.BlockSpec((tr, cols), lambda i: (i, 0))
    shp = jax.ShapeDtypeStruct((rows, cols), F32)
    return pl.pallas_call(
        body, name=name, out_shape=(shp, shp, shp), grid=(rows // tr,),
        in_specs=[blk, blk, blk, blk], out_specs=(blk, blk, blk), compiler_params=_cp(("parallel",)),
    )(w, g, m, v)


def _adam_small(w, gathered, m, v):
    rows = w.shape[0]

    def body(w_ref, a_ref, m_ref, v_ref, g_ref, d_ref, mo_ref, vo_ref):
        g = a_ref[0:rows, :]
        for k in range(1, N_DEV):
            g = g + a_ref[k * rows:(k + 1) * rows, :]
        g_ref[...] = g
        d_ref[...], mo_ref[...], vo_ref[...] = _adam_math(w_ref[...], g, m_ref[...], v_ref[...])

    shp = jax.ShapeDtypeStruct((rows, LANE), F32)
    return pl.pallas_call(body, name="adam_small", out_shape=(shp, shp, shp, shp), compiler_params=_cp(None))(
        w, gathered, m, v)


_WEIGHTS = ("rel_bias", "w_in", "b_gate", "ssm_conv_w", "ssm_conv_b", "ssm_dt_bias", "ssm_a_log", "ssm_d",
            "ssm_norm_w", "attn_sinks", "w_branch_ssm", "w_branch_attn", "w_mix_out", "ln1_g", "ln1_b", "w_up",
            "ffn_conv_w", "ffn_conv_b", "w_down", "ln2_g", "ln2_b")
_REPLICATED = tuple(n for n, _ in _SMALL[:13])


class _Exchange:
    def __init__(self, w, chip, core):
        self.chip = chip
        self.c_idx = jnp.reshape(core, (1,)).astype(jnp.int32)
        self.chip_idx = jnp.reshape(chip, (1,)).astype(jnp.int32)
        self.shards = {n: w[n].astype(jnp.bfloat16).reshape(2, s[0] // 2, s[1]) for n, s, _ in _MATS}
        self.spec = {n: (s, ax) for n, s, ax in _MATS}
        self.sums = {}

    def _with_own(self, n, stack):
        shape = self.spec[n][0]
        slab = lax.broadcasted_iota(jnp.int32, (N_CHIPS, 1, 1), 0)
        return jnp.where(slab == self.chip, self.shards[n].reshape((1,) + shape), stack.reshape((N_CHIPS,) + shape))

    def w_in_packed(self):
        (stack,) = _run_job(_gather_job([self.shards["w_in"]], copy_own=False), "allgather_w_in")
        return _pack_w_in_stack(self._with_own("w_in", stack))

    def gather_rest(self):
        return _gather_job([self.shards[n] for n in _REST], copy_own=False)

    def rest_weights(self, stacks):
        return {n: _from_stack(self._with_own(n, st), self.spec[n][1]) for n, st in zip(_REST, stacks)}

    def reduce_job(self, grads):
        names, stacks = [], []
        for n, g in grads.items():
            name = "w_in" if n == "wp" else n
            s, ax = self.spec[name]
            st = _unpack_w_in_stack(g) if n == "wp" else _to_stack(g, s, ax)
            names.append(name)
            stacks.append(st.reshape(N_CHIPS, 2, s[0] // 2, s[1]))
        swapped = _swap_halves(stacks, "swap_" + names[0])
        halves = []
        for n, g, r in zip(names, stacks, swapped):
            self.sums[n], bf = _add_own_half(g, r, self.c_idx, "add_own_" + n)
            halves.append(bf)
        return _scatter_job(halves)

    def reduced(self, landed):
        names = list(self.sums)
        reds = [_add_chips(self.sums[n], r, self.chip_idx, self.c_idx, "add_chips_" + n)
                for n, r in zip(names, landed)]
        return {n: g.reshape(self.spec[n][0]) for n, g in zip(names, _join_halves(reds))}


def _step(x, target, w, m, v):
    xi, yi, ci = _position()
    chip = 2 * xi + yi
    ex = _Exchange(w, chip, ci)

    wts = {n: w[n] for n in _REPLICATED}
    wts["wp"] = ex.w_in_packed()
    taps = jnp.concatenate([w[n].astype(F32).reshape(-1) for n, _ in _CONVS])
    taps = _allgather_small(_as_rows(taps, _CONV_ROWS, F32), "allgather_taps")
    taps = taps.reshape(N_CHIPS, 2, _CONV_ROWS * LANE)[:, 0]
    at = 0
    for n, s in _CONVS:
        wts[n] = _from_stack(taps[:, at:at + s[0] * s[1]].reshape((N_CHIPS,) + s), 1)
        at += s[0] * s[1]

    d_x, grads, small, landed = _local_step(x, target, wts, ex)

    outs = {"grad": ex.reduced(landed), "delta": {}, "m": {}, "v": {}}

    small = dict(small, g_ssm_conv_w=grads["ssm_conv_w"], g_ffn_conv_w=grads["ffn_conv_w"])
    all_small = _allgather_small(_pack_small(small), "allgather_small")
    packs = [_pack_small({n: d[n] for n in _REPLICATED}) for d in (w, m, v)]
    g_s, d_s, m_s, v_s = (_unpack_small(a) for a in _adam_small(packs[0], all_small, packs[1], packs[2]))
    for kind, part in (("grad", g_s), ("delta", d_s), ("m", m_s), ("v", v_s)):
        outs[kind].update({n: part[n] for n in _REPLICATED})
    for n, s in _CONVS:
        outs["grad"][n] = lax.dynamic_slice_in_dim(g_s["g_" + n], chip * s[1], s[1], axis=1)
    for n in [n for n, _, _ in _MATS] + [n for n, _ in _CONVS]:
        outs["delta"][n], outs["m"][n], outs["v"][n] = _adam_big(
            w[n].astype(F32), outs["grad"][n], m[n].astype(F32), v[n].astype(F32), "adam_" + n)
    loss = (0.5 / D_MODEL) * jnp.sum(g_s["loss_lanes"])
    return loss, d_x, outs


def kernel(x, rel_bias, w_in, b_gate, ssm_conv_w, ssm_conv_b, ssm_dt_bias, ssm_a_log, ssm_d, ssm_norm_w, attn_sinks, w_branch_ssm, w_branch_attn, w_mix_out, ln1_g, ln1_b, w_up, ffn_conv_w, ffn_conv_b, w_down, ln2_g, ln2_b, loss_target, m_rel_bias, m_w_in, m_b_gate, m_ssm_conv_w, m_ssm_conv_b, m_ssm_dt_bias, m_ssm_a_log, m_ssm_d, m_ssm_norm_w, m_attn_sinks, m_w_branch_ssm, m_w_branch_attn, m_w_mix_out, m_ln1_g, m_ln1_b, m_w_up, m_ffn_conv_w, m_ffn_conv_b, m_w_down, m_ln2_g, m_ln2_b, v_rel_bias, v_w_in, v_b_gate, v_ssm_conv_w, v_ssm_conv_b, v_ssm_dt_bias, v_ssm_a_log, v_ssm_d, v_ssm_norm_w, v_attn_sinks, v_w_branch_ssm, v_w_branch_attn, v_w_mix_out, v_ln1_g, v_ln1_b, v_w_up, v_ffn_conv_w, v_ffn_conv_b, v_w_down, v_ln2_g, v_ln2_b):
    given = dict(locals())
    drop = lambda a, n: a if n == "rel_bias" or a.ndim == 2 else a[0]
    w = {n: drop(given[n], n) for n in _WEIGHTS}
    m = {n: drop(given["m_" + n], n) for n in _WEIGHTS}
    v = {n: drop(given["v_" + n], n) for n in _WEIGHTS}
    loss, d_x, outs = _step(x[0], loss_target[0], w, m, v)
    like = lambda a, n: a.reshape(given[n].shape)
    res = [loss, d_x[None]]
    for kind in ("grad", "delta", "m", "v"):
        res += [like(outs[kind][n], n) for n in _WEIGHTS]
    return tuple(res)
```

```python
import math
from typing import NamedTuple

import numpy as np
import jax
import jax.numpy as jnp
from jax import lax
from jax.experimental import pallas as pl
from jax.experimental.pallas import tpu as pltpu

F32 = jnp.float32
_ACT = jnp.bfloat16
_MXU = jnp.bfloat16

D_MODEL = 1024
D_INNER = 2048
N_HEADS = 32
HEAD_P = 64
N_GROUPS = 4
N_STATE = 128
CHUNK = 128
CONV_DIM = 3072
SSM_K = 4
A_HEADS = 16
A_DH = 64
WIN = 128
REL_BUCKETS = 32
BIAS_ROWS = 64
D_FF = 2816
FFN_K = 3
ALPHA = 2.0 ** 0.25
LN_EPS = 1e-5
RMS_EPS = 1e-5
IN_COLS = 8480
NEG = -1e30

ADAM_LR = 0.001
ADAM_B1 = 0.9
ADAM_B2 = 0.999
ADAM_EPS = 1e-08
ADAM_WD = 0.01
ADAM_STEP = 10

LANE = 128
SUB = 8

P_Z, P_XS, P_G, P_Q, P_BC, P_K, P_V, P_DT = 0, 2048, 4096, 6144, 7168, 8192, 8320, 8448
P_W = 8704
P_MAIN = 8192
T_K, T_V, T_DT = P_K - P_MAIN, P_V - P_MAIN, P_DT - P_MAIN
_PIECES = ((0, 2048, P_Z), (2048, 2048, P_XS), (4096, 1024, P_BC), (5120, 32, P_DT), (5152, 1024, P_Q),
           (6176, 128, P_K), (6304, 128, P_V), (6432, 2048, P_G))

N_CHIPS = 4
N_DEV = 8


def _cp(sem=None, vmem_mb=48):
    return pltpu.CompilerParams(dimension_semantics=sem, vmem_limit_bytes=vmem_mb * 1024 * 1024)


def _pick(n, cands):
    for c in cands:
        if n % c == 0:
            return c
    raise ValueError(f"no block size for {n}")


def _rows8(p):
    k, c = p.shape
    return jnp.concatenate([p.astype(F32), jnp.zeros((SUB - k, c), F32)], axis=0)


class _SideJob(NamedTuple):
    inputs: list
    out_shape: list
    sems: list
    start: object
    finish: object


def _mm(a, b, name, *, trans_a=False, out_dtype=F32, res=None, res_scale=1.0, side=None):
    if trans_a:
        k_dim, m = a.shape
    else:
        m, k_dim = a.shape
    k2, n = b.shape
    assert k_dim == k2, (a.shape, b.shape)
    tm = _pick(m, (1408, 1024, 512, 256, 128))
    tn = _pick(n, (1408, 1024, 512, 256, 128))
    tk = _pick(k_dim, (2816, 2176, 2048, 1024, 512, 256, 128))
    nk = k_dim // tk
    grid = (m // tm, n // tn, nk)
    dn = (((0,), (0,)), ((), ())) if trans_a else (((1,), (0,)), ((), ()))
    n_in = 2 if res is None else 3
    ns_in = len(side.inputs) if side else 0
    ns_out = len(side.out_shape) if side else 0

    def body(*refs):
        a_ref, b_ref = refs[0], refs[1]
        o_ref = refs[n_in + ns_in]
        scratch = refs[n_in + ns_in + 1 + ns_out:]
        job_refs = (refs[n_in:n_in + ns_in], refs[n_in + ns_in + 1:n_in + ns_in + 1 + ns_out],
                    scratch[1:] if nk > 1 else scratch)
        i, j, k = pl.program_id(0), pl.program_id(1), pl.program_id(2)

        def finish(r):
            if res is not None:
                r = r + res_scale * refs[2][...]
            o_ref[...] = r.astype(out_dtype)

        if side:
            @pl.when(jnp.logical_and(jnp.logical_and(i == 0, j == 0), k == 0))
            def _():
                side.start(*job_refs)

        part = lax.dot_general(a_ref[...].astype(_MXU), b_ref[...].astype(_MXU), dn, preferred_element_type=F32)
        if nk == 1:
            finish(part)
        else:
            acc = scratch[0]

            @pl.when(k == 0)
            def _():
                acc[...] = part

            @pl.when(k > 0)
            def _():
                acc[...] += part

            @pl.when(k == nk - 1)
            def _():
                finish(acc[...])

        if side:
            @pl.when(jnp.logical_and(jnp.logical_and(i == grid[0] - 1, j == grid[1] - 1), k == nk - 1))
            def _():
                side.finish(*job_refs)

    if trans_a:
        a_spec = pl.BlockSpec((tk, tm), lambda i, j, k: (k, i))
    else:
        a_spec = pl.BlockSpec((tm, tk), lambda i, j, k: (i, k))
    in_specs = [a_spec, pl.BlockSpec((tk, tn), lambda i, j, k: (k, j))]
    args = [a, b]
    if res is not None:
        in_specs.append(pl.BlockSpec((tm, tn), lambda i, j, k: (i, j)))
        args.append(res)
    out_spec = pl.BlockSpec((tm, tn), lambda i, j, k: (i, j))
    out_shape = jax.ShapeDtypeStruct((m, n), out_dtype)
    scratch_shapes = [pltpu.VMEM((tm, tn), F32)] if nk > 1 else []
    if not side:
        return pl.pallas_call(
            body, name=name, out_shape=out_shape, grid=grid, in_specs=in_specs, out_specs=out_spec,
            scratch_shapes=scratch_shapes, compiler_params=_cp(("parallel", "parallel", "arbitrary")),
        )(*args)
    hbm = pl.BlockSpec(memory_space=pltpu.HBM)
    outs = pl.pallas_call(
        body, name=name, out_shape=[out_shape] + list(side.out_shape), grid=grid,
        in_specs=in_specs + [hbm] * ns_in, out_specs=[out_spec] + [hbm] * ns_out,
        scratch_shapes=scratch_shapes + list(side.sems),
        compiler_params=_cp(("arbitrary", "arbitrary", "arbitrary")),
    )(*args, *side.inputs)
    return outs[0], list(outs[1:])


def _shift_down(cur, prev8, s):
    r = pltpu.roll(cur, s, 0)
    p = pltpu.roll(prev8, s, 0)
    row8 = lax.broadcasted_iota(jnp.int32, (SUB, 1), 0)
    fixed = jnp.where(row8 < s, p, r[0:SUB])
    if cur.shape[0] == SUB:
        return fixed
    return jnp.concatenate([fixed, r[SUB:]], axis=0)


def _shift_up(cur, next8, s):
    tm = cur.shape[0]
    r = pltpu.roll(cur, tm - s, 0)
    p = pltpu.roll(next8, SUB - s, 0)
    row8 = lax.broadcasted_iota(jnp.int32, (SUB, 1), 0)
    fixed = jnp.where(row8 >= SUB - s, p, r[tm - SUB:])
    return jnp.concatenate([r[:tm - SUB], fixed], axis=0)


def _conv_pre(cur, prev8, w_ref, b_row, taps):
    acc = cur * w_ref[taps - 1:taps, :] + b_row
    for s in range(1, taps):
        acc = acc + _shift_down(cur, prev8, s) * w_ref[taps - 1 - s:taps - s, :]
    return acc


def _dot01_r(x, m01, parts=3):
    acc = None
    r = x
    for _ in range(parts):
        hi = r.astype(jnp.bfloat16)
        t = jnp.dot(hi, m01, preferred_element_type=F32)
        acc = t if acc is None else acc + t
        r = r - hi.astype(F32)
    return acc


def _dot01_l(m01, x, parts=3):
    acc = None
    r = x
    for _ in range(parts):
        hi = r.astype(jnp.bfloat16)
        t = jnp.dot(m01, hi, preferred_element_type=F32)
        acc = t if acc is None else acc + t
        r = r - hi.astype(F32)
    return acc


def _dot(a, b):
    return jnp.dot(a.astype(_MXU), b.astype(_MXU), preferred_element_type=F32)


def _dot_nt(a, b):
    return lax.dot_general(a.astype(_MXU), b.astype(_MXU), (((1,), (1,)), ((), ())), preferred_element_type=F32)


def _dot_tn(a, b):
    return lax.dot_general(a.astype(_MXU), b.astype(_MXU), (((0,), (0,)), ((), ())), preferred_element_type=F32)


def _sigmoid(x):
    return 1.0 / (1.0 + jnp.exp(-x))


def _half_masks():
    lane = lax.broadcasted_iota(jnp.int32, (1, LANE), 1)
    lo = (lane < 64).astype(F32)
    return lo, 1.0 - lo


_TC = 1024


def _tm_rows(t):
    return min(512, t)


HALO = 16


def _prev_halo(tm, width, pos):
    def index(*ids):
        i, col = pos(*ids)
        return (jnp.maximum(i * (tm // HALO) - 1, 0), col)
    return pl.BlockSpec((HALO, width), index)


def _next_halo(tm, t, width, pos):
    def index(*ids):
        i, col = pos(*ids)
        return (jnp.minimum((i + 1) * (tm // HALO), t // HALO - 1), col)
    return pl.BlockSpec((HALO, width), index)


def _conv_silu_fwd(proj, colblk0, nblk, w8, b8, name):
    t = proj.shape[0]
    tm = _tm_rows(t)

    def body(c_ref, p_ref, w_ref, b_ref, o_ref, pre_ref):
        i = pl.program_id(1)
        prev8 = jnp.where(i > 0, p_ref[SUB:HALO, :].astype(F32), 0.0)
        pre = _conv_pre(c_ref[...].astype(F32), prev8, w_ref, b_ref[0:1, :], SSM_K)
        o_ref[...] = pre * _sigmoid(pre)
        pre_ref[...] = pre.astype(_ACT)

    blk = pl.BlockSpec((tm, _TC), lambda j, i: (i, j))
    return pl.pallas_call(
        body, name=name,
        out_shape=(jax.ShapeDtypeStruct((t, nblk * _TC), F32), jax.ShapeDtypeStruct((t, nblk * _TC), _ACT)),
        grid=(nblk, t // tm),
        in_specs=[pl.BlockSpec((tm, _TC), lambda j, i: (i, colblk0 + j)),
                  _prev_halo(tm, _TC, lambda j, i: (i, colblk0 + j)),
                  pl.BlockSpec((SUB, _TC), lambda j, i: (0, j)),
                  pl.BlockSpec((SUB, _TC), lambda j, i: (0, j))],
        out_specs=(blk, blk),
        compiler_params=_cp(("parallel", "parallel")),
    )(proj, proj, w8, b8)


def _silu_grad(pre):
    sg = _sigmoid(pre)
    return sg * (1.0 + pre * (1.0 - sg))


def _conv_grads(d, d_next8, cur, w_ref, acc_ref, taps, cols=slice(None)):
    du = d * w_ref[taps - 1:taps, :]
    acc_ref[taps:taps + 1, cols] += jnp.sum(d, axis=0, keepdims=True)
    acc_ref[taps - 1:taps, cols] += jnp.sum(d * cur, axis=0, keepdims=True)
    for s in range(1, taps):
        up = _shift_up(d, d_next8, s)
        du = du + up * w_ref[taps - 1 - s:taps - s, :]
        acc_ref[taps - 1 - s:taps - s, cols] += jnp.sum(up * cur, axis=0, keepdims=True)
    return du


def _conv_silu_bwd(proj, colblk0, nblk, pre, w8, d_out, d_proj, name):
    t = proj.shape[0]
    tm = _tm_rows(t)
    nt = t // tm

    def body(c_ref, pre_ref, pren_ref, w_ref, d_ref, dn_ref, _, du_ref, acc_ref):
        i = pl.program_id(1)

        @pl.when(i == 0)
        def _():
            acc_ref[...] = jnp.zeros_like(acc_ref)

        dpre = d_ref[...].astype(F32) * _silu_grad(pre_ref[...].astype(F32))
        dpre_n = jnp.where(i < nt - 1, dn_ref[0:SUB, :].astype(F32) * _silu_grad(pren_ref[0:SUB, :].astype(F32)), 0.0)
        du_ref[...] = _conv_grads(dpre, dpre_n, c_ref[...].astype(F32), w_ref, acc_ref, SSM_K).astype(_ACT)

    c = nblk * _TC
    blk = pl.BlockSpec((tm, _TC), lambda j, i: (i, j))
    nxt = _next_halo(tm, t, _TC, lambda j, i: (i, j))
    par = pl.BlockSpec((SUB, _TC), lambda j, i: (0, j))
    return pl.pallas_call(
        body, name=name,
        out_shape=(jax.ShapeDtypeStruct(d_proj.shape, d_proj.dtype), jax.ShapeDtypeStruct((SUB, c), F32)),
        grid=(nblk, nt),
        in_specs=[pl.BlockSpec((tm, _TC), lambda j, i: (i, colblk0 + j)), blk, nxt, par, blk, nxt,
                  pl.BlockSpec(memory_space=pl.ANY)],
        out_specs=(pl.BlockSpec((tm, _TC), lambda j, i: (i, colblk0 + j)), par),
        input_output_aliases={6: 0},
        compiler_params=_cp(("parallel", "arbitrary")),
    )(proj, pre, pre, w8, d_out, d_out, d_proj)


def _expand_consts():
    e = np.zeros((LANE, D_INNER), np.float32)
    for h in range(N_HEADS):
        e[h, h * HEAD_P:(h + 1) * HEAD_P] = 1.0
    return jnp.asarray(e, jnp.bfloat16), jnp.asarray(e.T.copy(), jnp.bfloat16)


def _ssd_common(dtr_ref, dtb_ref, alog_ref, e_ref):
    lane = lax.broadcasted_iota(jnp.int32, (1, LANE), 1)
    hm = lane < N_HEADS
    pre = dtr_ref[...] + dtb_ref[0:1, :]
    dt = jnp.where(hm, jnp.maximum(pre, 0.0) + jnp.log(1.0 + jnp.exp(-jnp.abs(pre))), 0.0)
    a_row = jnp.where(hm, -jnp.exp(alog_ref[0:1, :]), 0.0)
    adt = dt * a_row
    r = lax.broadcasted_iota(jnp.int32, (CHUNK, CHUNK), 0)
    c = lax.broadcasted_iota(jnp.int32, (CHUNK, CHUNK), 1)
    causal = r >= c
    acs = _dot01_l(causal.astype(jnp.bfloat16), adt)
    e = e_ref[...]
    acs_x = _dot01_r(acs, e, parts=2)
    dt_x = _dot01_r(dt, e, parts=2)
    return pre, dt, a_row, acs, acs_x, dt_x, causal, hm


def _decay(acs, acs_t, h, causal):
    seg = acs[:, h:h + 1] - acs_t[h:h + 1, :]
    return jnp.exp(jnp.where(causal, seg, NEG))


def _ssd_fwd(xs_c, bc_c, proj, tail, dtb8, alog8, dsk8, nw8, name):
    t = xs_c.shape[0]
    nc = t // CHUNK
    e_bf, _ = _expand_consts()
    gw = D_INNER // N_GROUPS

    def body(xs_ref, bc_ref, dtr_ref, z_ref, dtb_ref, alog_ref, dsk_ref, nw_ref, e_ref,
             y_ref, ys_ref, hp_ref, h_ref):
        c_id = pl.program_id(0)

        @pl.when(c_id == 0)
        def _():
            h_ref[...] = jnp.zeros_like(h_ref)

        _, dt, a_row, acs, acs_x, dt_x, causal, _ = _ssd_common(dtr_ref, dtb_ref, alog_ref, e_ref)
        acs_t = acs.T
        xs = xs_ref[...]
        x_dt = xs * dt_x
        last_x = acs_x[CHUNK - 1:CHUNK, :]
        w_end = jnp.exp(last_x - acs_x)
        e_in = jnp.exp(acs_x)
        d_x = _dot01_r(dsk_ref[...], e_ref[...])[0:1, :]
        hprev = h_ref[...]
        hp_ref[...] = hprev
        lo, hi = _half_masks()
        for g in range(N_GROUPS):
            bg = bc_ref[:, g * N_STATE:(g + 1) * N_STATE]
            cg = bc_ref[:, N_GROUPS * N_STATE + g * N_STATE:N_GROUPS * N_STATE + (g + 1) * N_STATE]
            sl = slice(g * gw, (g + 1) * gw)
            gm = _dot_nt(cg, bg)
            st = _dot(bg.T, x_dt[:, sl] * w_end[:, sl])
            y_off = _dot(cg, hprev[:, sl]) * e_in[:, sl]
            for j in range(gw // LANE):
                h0 = g * (gw // HEAD_P) + 2 * j
                cs = slice(g * gw + j * LANE, g * gw + (j + 1) * LANE)
                xp = x_dt[:, cs]
                m0 = gm * _decay(acs, acs_t, h0, causal)
                m1 = gm * _decay(acs, acs_t, h0 + 1, causal)
                yd = _dot(m0, xp * lo) + _dot(m1, xp * hi)
                y_ref[:, cs] = yd + y_off[:, j * LANE:(j + 1) * LANE] + xs[:, cs] * d_x[:, cs]
            h_ref[:, sl] = hprev[:, sl] * jnp.exp(last_x[:, sl]) + st
        y = y_ref[...]
        z = z_ref[...].astype(F32)
        y2 = y * (z * _sigmoid(z))
        for g in range(N_GROUPS):
            sl = slice(g * gw, (g + 1) * gw)
            yg = y2[:, sl]
            rinv = lax.rsqrt(jnp.mean(yg * yg, axis=-1, keepdims=True) + RMS_EPS)
            ys_ref[:, sl] = (yg * rinv * nw_ref[0:1, sl]).astype(_ACT)

    small = pl.BlockSpec((SUB, LANE), lambda c: (0, 0))
    return pl.pallas_call(
        body, name=name,
        out_shape=(jax.ShapeDtypeStruct((t, D_INNER), F32), jax.ShapeDtypeStruct((t, D_INNER), _ACT),
                   jax.ShapeDtypeStruct((t, D_INNER), F32)),
        grid=(nc,),
        in_specs=[pl.BlockSpec((CHUNK, D_INNER), lambda c: (c, 0)),
                  pl.BlockSpec((CHUNK, 1024), lambda c: (c, 0)),
                  pl.BlockSpec((CHUNK, LANE), lambda c: (c, T_DT // LANE)),
                  pl.BlockSpec((CHUNK, D_INNER), lambda c: (c, P_Z // D_INNER)),
                  small, small, small,
                  pl.BlockSpec((SUB, D_INNER), lambda c: (0, 0)),
                  pl.BlockSpec((LANE, D_INNER), lambda c: (0, 0))],
        out_specs=(pl.BlockSpec((CHUNK, D_INNER), lambda c: (c, 0)),
                   pl.BlockSpec((CHUNK, D_INNER), lambda c: (c, 0)),
                   pl.BlockSpec((N_STATE, D_INNER), lambda c: (c, 0))),
        scratch_shapes=[pltpu.VMEM((N_STATE, D_INNER), F32)],
        compiler_params=_cp(("arbitrary",)),
    )(xs_c, bc_c, tail, proj, dtb8, alog8, dsk8, nw8, e_bf)


def _ssd_bwd(d_ys, y, xs_c, bc_c, proj, tail, hprev_all, dtb8, alog8, dsk8, nw8, d_proj, name):
    t = xs_c.shape[0]
    nc = t // CHUNK
    e_bf, et_bf = _expand_consts()
    gw = D_INNER // N_GROUPS

    def body(dys_ref, y_ref, xs_ref, bc_ref, dtr_ref, z_ref, hp_ref, dtb_ref, alog_ref, dsk_ref, nw_ref,
             e_ref, et_ref, _, dxs_ref, dbc_ref, dz_ref, ddt_ref, acc_ref, dnw_ref, dh_ref, dx_ref):
        step = pl.program_id(0)

        @pl.when(step == 0)
        def _():
            dh_ref[...] = jnp.zeros_like(dh_ref)
            acc_ref[...] = jnp.zeros_like(acc_ref)
            dnw_ref[...] = jnp.zeros_like(dnw_ref)

        pre, dt, a_row, acs, acs_x, dt_x, causal, hm = _ssd_common(dtr_ref, dtb_ref, alog_ref, e_ref)
        acs_t = acs.T
        et = et_ref[...]
        xs = xs_ref[...]
        x_dt = xs * dt_x
        last_x = acs_x[CHUNK - 1:CHUNK, :]
        w_end = jnp.exp(last_x - acs_x)
        e_in = jnp.exp(acs_x)
        e_last = jnp.exp(last_x)
        d_x = _dot01_r(dsk_ref[...], e_ref[...])[0:1, :]

        y = y_ref[...]
        z = z_ref[...].astype(F32)
        sz = _sigmoid(z)
        gz = z * sz
        y2 = y * gz
        dys = dys_ref[...].astype(F32)
        for g in range(N_GROUPS):
            sl = slice(g * gw, (g + 1) * gw)
            yg = y2[:, sl]
            rinv = lax.rsqrt(jnp.mean(yg * yg, axis=-1, keepdims=True) + RMS_EPS)
            nrm = yg * rinv
            dn = dys[:, sl] * nw_ref[0:1, sl]
            dnw_ref[0:1, sl] += jnp.sum(dys[:, sl] * nrm, axis=0, keepdims=True)
            dx_ref[:, sl] = rinv * (dn - nrm * jnp.mean(dn * nrm, axis=-1, keepdims=True))
        dy2 = dx_ref[...]
        dy = dy2 * gz
        dz_ref[...] = (dy2 * y * (sz * (1.0 + z * (1.0 - sz)))).astype(_ACT)

        dh_next = dh_ref[...]
        hprev = hp_ref[...]
        lo, hi = _half_masks()
        r = lax.broadcasted_iota(jnp.int32, (CHUNK, CHUNK), 0)
        c = lax.broadcasted_iota(jnp.int32, (CHUNK, CHUNK), 1)
        from_here = (c >= r).astype(jnp.bfloat16)
        before = c < r
        lane = lax.broadcasted_iota(jnp.int32, (1, LANE), 1)
        da_intra = jnp.zeros((CHUNK, LANE), F32)
        v_seg = jnp.zeros((CHUNK, LANE), F32)
        z_seg = jnp.zeros((CHUNK, LANE), F32)
        tail_parts = []
        for g in range(N_GROUPS):
            bg = bc_ref[:, g * N_STATE:(g + 1) * N_STATE]
            cg = bc_ref[:, N_GROUPS * N_STATE + g * N_STATE:N_GROUPS * N_STATE + (g + 1) * N_STATE]
            sl = slice(g * gw, (g + 1) * gw)
            et_g = et_ref[g * gw:(g + 1) * gw, :]
            gm = _dot_nt(cg, bg)
            dzg = e_in[:, sl] * dy[:, sl]
            dcg = _dot_nt(dzg, hprev[:, sl])
            dh_c = _dot(cg.T, dzg)
            q = _dot(bg, dh_next[:, sl])
            dbg = _dot_nt(x_dt[:, sl] * w_end[:, sl], dh_next[:, sl])
            y_off = _dot(cg, hprev[:, sl]) * e_in[:, sl]
            v_seg = v_seg + _dot01_r(dy[:, sl] * y_off, et_g, parts=1)
            z_seg = z_seg + _dot01_r(w_end[:, sl] * q * x_dt[:, sl], et_g, parts=1)
            dgm = jnp.zeros((CHUNK, CHUNK), F32)
            for j in range(gw // LANE):
                h0 = g * (gw // HEAD_P) + 2 * j
                cs = slice(g * gw + j * LANE, g * gw + (j + 1) * LANE)
                xp = x_dt[:, cs]
                dyp = dy[:, cs]
                dxd = jnp.zeros((CHUNK, LANE), F32)
                for half, msk in ((0, lo), (1, hi)):
                    lam = _decay(acs, acs_t, h0 + half, causal)
                    mm = gm * lam
                    dym = dyp * msk
                    dmm = _dot_nt(dym, xp)
                    dxd = dxd + _dot_tn(mm, dym)
                    dgm = dgm + dmm * lam
                    below = _dot(from_here, dmm * mm)
                    col = jnp.sum(jnp.where(before, below, 0.0), axis=-1, keepdims=True)
                    da_intra = da_intra + jnp.where(lane == h0 + half, col, 0.0)
                dx_ref[:, cs] = dxd + w_end[:, cs] * q[:, j * LANE:(j + 1) * LANE]
            dbc_ref[:, N_GROUPS * N_STATE + g * N_STATE:N_GROUPS * N_STATE + (g + 1) * N_STATE] = dcg + _dot(dgm, bg)
            dbc_ref[:, g * N_STATE:(g + 1) * N_STATE] = dbg + _dot_tn(dgm, cg)
            dh_ref[:, sl] = e_last[:, sl] * dh_next[:, sl] + dh_c
            tail_parts.append(e_last[:, sl] * jnp.sum(dh_next[:, sl] * hprev[:, sl], axis=0, keepdims=True))
        dxt = dx_ref[...]

        u_seg = _dot01_r(xs * dxt, et, parts=1)
        q_full = jnp.concatenate(tail_parts, axis=1)
        t_row = _dot01_r(jnp.broadcast_to(q_full, (SUB, D_INNER)), et)[0:1, :]
        d_alpha = (da_intra + _dot01_l(from_here, v_seg) + _dot01_l(before.astype(jnp.bfloat16), z_seg) + t_row)
        d_dt = a_row * d_alpha + u_seg
        sgp = _sigmoid(pre)
        d_raw = jnp.where(hm, d_dt * sgp, 0.0)
        ddt_ref[...] = d_raw.astype(_ACT)
        acc_ref[0:1, :] += jnp.sum(d_raw, axis=0, keepdims=True)
        acc_ref[1:2, :] += jnp.sum(d_alpha * dt, axis=0, keepdims=True) * a_row
        dd_row = jnp.sum(dy * xs, axis=0, keepdims=True)
        acc_ref[2:3, :] += _dot01_r(jnp.broadcast_to(dd_row, (SUB, D_INNER)), et)[0:1, :]
        dxs_ref[...] = dy * d_x + dxt * dt_x

    rev = lambda c: (nc - 1 - c, 0)
    small = pl.BlockSpec((SUB, LANE), lambda c: (0, 0))
    return pl.pallas_call(
        body, name=name,
        out_shape=(jax.ShapeDtypeStruct((t, D_INNER), F32), jax.ShapeDtypeStruct((t, 1024), F32),
                   jax.ShapeDtypeStruct(d_proj.shape, d_proj.dtype), jax.ShapeDtypeStruct((t, LANE), _ACT),
                   jax.ShapeDtypeStruct((SUB, LANE), F32), jax.ShapeDtypeStruct((SUB, D_INNER), F32)),
        grid=(nc,),
        in_specs=[pl.BlockSpec((CHUNK, D_INNER), rev),
                  pl.BlockSpec((CHUNK, D_INNER), rev),
                  pl.BlockSpec((CHUNK, D_INNER), rev),
                  pl.BlockSpec((CHUNK, 1024), rev),
                  pl.BlockSpec((CHUNK, LANE), lambda c: (nc - 1 - c, T_DT // LANE)),
                  pl.BlockSpec((CHUNK, D_INNER), lambda c: (nc - 1 - c, P_Z // D_INNER)),
                  pl.BlockSpec((N_STATE, D_INNER), rev),
                  small, small, small,
                  pl.BlockSpec((SUB, D_INNER), lambda c: (0, 0)),
                  pl.BlockSpec((LANE, D_INNER), lambda c: (0, 0)),
                  pl.BlockSpec((D_INNER, LANE), lambda c: (0, 0)),
                  pl.BlockSpec(memory_space=pl.ANY)],
        out_specs=(pl.BlockSpec((CHUNK, D_INNER), rev),
                   pl.BlockSpec((CHUNK, 1024), rev),
                   pl.BlockSpec((CHUNK, D_INNER), lambda c: (nc - 1 - c, P_Z // D_INNER)),
                   pl.BlockSpec((CHUNK, LANE), rev),
                   small,
                   pl.BlockSpec((SUB, D_INNER), lambda c: (0, 0))),
        input_output_aliases={13: 2},
        scratch_shapes=[pltpu.VMEM((N_STATE, D_INNER), F32), pltpu.VMEM((CHUNK, D_INNER), F32)],
        compiler_params=_cp(("arbitrary",), vmem_mb=56),
    )(d_ys, y, xs_c, bc_c, tail, proj, hprev_all, dtb8, alog8, dsk8, nw8, e_bf, et_bf, d_proj)


def _rel_tables():
    qi = np.arange(WIN)[:, None] + WIN
    kj = np.arange(2 * WIN)[None, :]
    rel = qi - kj
    n = np.maximum(rel, 0)
    max_exact = REL_BUCKETS // 2
    nf = np.maximum(n, 1).astype(np.float32)
    large = max_exact + (np.log(nf / np.float32(max_exact)) / np.float32(math.log(WIN / max_exact))
                         * np.float32(REL_BUCKETS - max_exact)).astype(np.int32)
    large = np.minimum(large, REL_BUCKETS - 1)
    bucket = np.where(n < max_exact, n, large)
    valid = (rel >= 0) & (rel < WIN)
    sink_col = np.broadcast_to(kj == 0, rel.shape)
    onehot = np.zeros((BIAS_ROWS, WIN * 2 * WIN), np.float32)
    flat_b = np.where(sink_col, REL_BUCKETS, bucket).reshape(-1)
    flat_v = (valid | sink_col).reshape(-1)
    first_v = ((valid & (kj >= WIN)) | sink_col).reshape(-1)
    idx = np.arange(WIN * 2 * WIN)
    onehot[flat_b[flat_v], idx[flat_v]] = 1.0
    return onehot, np.stack([first_v, flat_v]).astype(np.float32)


def _bias_expand(table_t, name):
    onehot, valid = _rel_tables()

    def body(rb_ref, oh_ref, v_ref, o_ref):
        full = _dot01_r(rb_ref[...], oh_ref[...])
        o_ref[0] = jnp.where(v_ref[0:1, :] > 0.5, full, NEG)
        o_ref[1] = jnp.where(v_ref[1:2, :] > 0.5, full, NEG)

    return pl.pallas_call(
        body, name=name, out_shape=jax.ShapeDtypeStruct((2, A_HEADS, WIN * 2 * WIN), F32),
        compiler_params=_cp(None),
    )(table_t, jnp.asarray(onehot, jnp.bfloat16), jnp.asarray(valid, F32))


def _bias_reduce(dbias, name):
    onehot, _ = _rel_tables()

    def body(d_ref, oh_ref, o_ref):
        acc = None
        r = d_ref[...]
        for _ in range(3):
            hi = r.astype(jnp.bfloat16)
            tt = lax.dot_general(hi, oh_ref[...], (((1,), (1,)), ((), ())), preferred_element_type=F32)
            acc = tt if acc is None else acc + tt
            r = r - hi.astype(F32)
        o_ref[...] = acc

    return pl.pallas_call(
        body, name=name, out_shape=jax.ShapeDtypeStruct((A_HEADS, BIAS_ROWS), F32),
        compiler_params=_cp(None),
    )(dbias, jnp.asarray(onehot, jnp.bfloat16))


def _attn_bands(kc_ref, kp_ref, vc_ref, vp_ref, has_prev):
    lo, hi = _half_masks()
    row = lax.broadcasted_iota(jnp.int32, (2 * WIN, 1), 0)
    keep = (row > 0).astype(F32)
    kb = jnp.concatenate([jnp.where(has_prev, kp_ref[...], 0.0), kc_ref[...]], axis=0) * (keep * (A_DH ** -0.5))
    vb = jnp.concatenate([jnp.where(has_prev, vp_ref[...], 0.0), vc_ref[...]], axis=0) * keep
    kr = pltpu.roll(kb, 64, 1)
    vr = pltpu.roll(vb, 64, 1)
    kk = ((kb * lo, kr * hi), (kr * lo, kb * hi))
    vv = ((vb * lo, vr * hi), (vr * lo, vb * hi))
    return kk, vv, (hi, lo)


def _attn_logits(q_ref, kk, lg_ref):
    for h in range(A_HEADS):
        j, half, kv = h // 2, h % 2, h // (A_HEADS // 2)
        lg_ref[h] = _dot_nt(q_ref[:, j * LANE:(j + 1) * LANE], kk[kv][half])


def _attn_fwd(proj, tail, bias, name):
    t = proj.shape[0]
    nb = t // WIN

    def body(q_ref, kc_ref, kp_ref, vc_ref, vp_ref, b_ref, o_ref, lg_ref, p_ref):
        n = pl.program_id(0)
        kk, vv, ones = _attn_bands(kc_ref, kp_ref, vc_ref, vp_ref, n > 0)
        _attn_logits(q_ref, kk, lg_ref)
        for h in range(A_HEADS):
            logits = lg_ref[h] + b_ref[h]
            p_ref[h] = jnp.exp(logits - jnp.max(logits, axis=-1, keepdims=True)).astype(_MXU)
        lane = lax.broadcasted_iota(jnp.int32, (1, LANE), 1)
        for j in range(A_HEADS // 2):
            kv = (2 * j) // (A_HEADS // 2)
            outs = []
            for half in range(2):
                o = jnp.dot(p_ref[2 * j + half], (vv[kv][half] + ones[half]).astype(_MXU), preferred_element_type=F32)
                outs.append(o / pltpu.roll(o, 64, 1))
            o_ref[:, j * LANE:(j + 1) * LANE] = jnp.where(lane < 64, outs[0], outs[1]).astype(_ACT)

    kvspec = lambda col, prev: pl.BlockSpec(
        (WIN, LANE), (lambda n: (jnp.maximum(n - 1, 0), col)) if prev else (lambda n: (n, col)))
    return pl.pallas_call(
        body, name=name, out_shape=jax.ShapeDtypeStruct((t, D_MODEL), _ACT),
        grid=(nb,),
        in_specs=[pl.BlockSpec((WIN, 1024), lambda n: (n, P_Q // 1024)),
                  kvspec(T_K // LANE, False), kvspec(T_K // LANE, True),
                  kvspec(T_V // LANE, False), kvspec(T_V // LANE, True),
                  pl.BlockSpec((None, A_HEADS, WIN, 2 * WIN), lambda n: (jnp.minimum(n, 1), 0, 0, 0))],
        out_specs=pl.BlockSpec((WIN, 1024), lambda n: (n, 0)),
        scratch_shapes=[pltpu.VMEM((A_HEADS, WIN, 2 * WIN), F32), pltpu.VMEM((A_HEADS, WIN, 2 * WIN), _MXU)],
        compiler_params=_cp(("parallel",)),
    )(proj, tail, tail, tail, tail, bias)


def _attn_bwd(proj, tail, bias, y_attn, d_out, d_proj, name):
    t = proj.shape[0]
    nb = t // WIN

    def body(q_ref, kc_ref, kp_ref, vc_ref, vp_ref, b_ref, y_ref, do_ref, _,
             dq_ref, dk_ref, dv_ref, db_ref, ck_ref, cv_ref, lg_ref, dl_ref, p_ref):
        n = pl.program_id(0)

        @pl.when(n == 0)
        def _():
            db_ref[...] = jnp.zeros_like(db_ref)
            ck_ref[...] = jnp.zeros_like(ck_ref)
            cv_ref[...] = jnp.zeros_like(cv_ref)

        @pl.when(n < nb)
        def _():
            kk, vv, _ = _attn_bands(kc_ref, kp_ref, vc_ref, vp_ref, n > 0)
            lo, hi = _half_masks()
            ones_k = jnp.ones((2 * WIN, LANE), jnp.bfloat16)
            ones_d = jnp.ones((LANE, LANE), jnp.bfloat16)
            _attn_logits(q_ref, kk, lg_ref)
            for h in range(A_HEADS):
                j, half, kv = h // 2, h % 2, h // (A_HEADS // 2)
                msk = hi if half else lo
                logits = lg_ref[h] + b_ref[h]
                p = jnp.exp(logits - jnp.max(logits, axis=-1, keepdims=True))
                den = jnp.dot(p.astype(_MXU), ones_k.astype(_MXU), preferred_element_type=F32)
                dop = do_ref[:, j * LANE:(j + 1) * LANE].astype(F32)
                delta = _dot01_r(dop * y_ref[:, j * LANE:(j + 1) * LANE].astype(F32) * msk, ones_d, parts=2)
                inv = 1.0 / den
                probs = p * jnp.concatenate([inv, inv], axis=1)
                dprobs = _dot_nt(dop, vv[kv][half])
                dlog = probs * (dprobs - jnp.concatenate([delta, delta], axis=1))
                db_ref[h] += dlog
                dl_ref[h] = dlog.astype(_MXU)
                p_ref[h] = probs.astype(_MXU)
            dk_t = [[None, None], [None, None]]
            dv_t = [[None, None], [None, None]]
            for j in range(A_HEADS // 2):
                kv = (2 * j) // (A_HEADS // 2)
                qs = q_ref[:, j * LANE:(j + 1) * LANE].astype(F32) * (A_DH ** -0.5)
                dop = do_ref[:, j * LANE:(j + 1) * LANE].astype(F32)
                dq = None
                for half, msk in ((0, lo), (1, hi)):
                    h = 2 * j + half
                    dqh = jnp.dot(dl_ref[h], kk[kv][half].astype(_MXU), preferred_element_type=F32)
                    dq = dqh if dq is None else dq + dqh
                    dkh = lax.dot_general((qs * msk).astype(_MXU), dl_ref[h], (((0,), (0,)), ((), ())),
                                          preferred_element_type=F32)
                    dvh = lax.dot_general((dop * msk).astype(_MXU), p_ref[h], (((0,), (0,)), ((), ())),
                                          preferred_element_type=F32)
                    dk_t[kv][half] = dkh if dk_t[kv][half] is None else dk_t[kv][half] + dkh
                    dv_t[kv][half] = dvh if dv_t[kv][half] is None else dv_t[kv][half] + dvh
                dq_ref[:, j * LANE:(j + 1) * LANE] = dq.astype(_ACT)
            row = lax.broadcasted_iota(jnp.int32, (2 * WIN, 1), 0)

            def band(acc):
                a = (acc[0][0] + pltpu.roll(acc[0][1], 64, 0)) + (pltpu.roll(acc[1][0], 64, 0) + acc[1][1])
                return jnp.where(row > 0, a.T, 0.0)

            dkb = band(dk_t)
            dvb = band(dv_t)
            dk_ref[...] = (ck_ref[...] + dkb[0:WIN]).astype(_ACT)
            dv_ref[...] = (cv_ref[...] + dvb[0:WIN]).astype(_ACT)
            ck_ref[...] = dkb[WIN:]
            cv_ref[...] = dvb[WIN:]

        @pl.when(n == nb)
        def _():
            dk_ref[...] = ck_ref[...].astype(_ACT)
            dv_ref[...] = cv_ref[...].astype(_ACT)

    cur = lambda n: jnp.minimum(n, nb - 1)
    prv = lambda n: jnp.maximum(jnp.minimum(n, nb - 1) - 1, 0)
    kvspec = lambda col, prev: pl.BlockSpec(
        (WIN, LANE), (lambda n: (prv(n), col)) if prev else (lambda n: (cur(n), col)))
    band_shape = (A_HEADS, WIN, 2 * WIN)
    return pl.pallas_call(
        body, name=name,
        out_shape=(jax.ShapeDtypeStruct(d_proj.shape, d_proj.dtype), jax.ShapeDtypeStruct((t, LANE), _ACT),
                   jax.ShapeDtypeStruct((t, LANE), _ACT), jax.ShapeDtypeStruct(band_shape, F32)),
        grid=(nb + 1,),
        in_specs=[pl.BlockSpec((WIN, 1024), lambda n: (cur(n), P_Q // 1024)),
                  kvspec(T_K // LANE, False), kvspec(T_K // LANE, True),
                  kvspec(T_V // LANE, False), kvspec(T_V // LANE, True),
                  pl.BlockSpec((None,) + band_shape, lambda n: (jnp.minimum(n, 1), 0, 0, 0)),
                  pl.BlockSpec((WIN, 1024), lambda n: (cur(n), 0)),
                  pl.BlockSpec((WIN, 1024), lambda n: (cur(n), 0)),
                  pl.BlockSpec(memory_space=pl.ANY)],
        out_specs=(pl.BlockSpec((WIN, 1024), lambda n: (cur(n), P_Q // 1024)),
                   pl.BlockSpec((WIN, LANE), lambda n: (jnp.maximum(n - 1, 0), 0)),
                   pl.BlockSpec((WIN, LANE), lambda n: (jnp.maximum(n - 1, 0), 0)),
                   pl.BlockSpec(band_shape, lambda n: (0, 0, 0))),
        input_output_aliases={8: 0},
        scratch_shapes=[pltpu.VMEM((WIN, LANE), F32), pltpu.VMEM((WIN, LANE), F32),
                        pltpu.VMEM(band_shape, F32), pltpu.VMEM(band_shape, _MXU), pltpu.VMEM(band_shape, _MXU)],
        compiler_params=_cp(("arbitrary",)),
    )(proj, tail, tail, tail, tail, bias, y_attn, d_out, d_proj)


def _merge_fwd(bs, ba, proj, bg8, name):
    t = bs.shape[0]
    tm = _tm_rows(t)

    def body(bs_ref, ba_ref, gs_ref, ga_ref, bgs_ref, bga_ref, o_ref):
        g_s = _sigmoid(gs_ref[...] + bgs_ref[0:1, :])
        g_a = _sigmoid(ga_ref[...] + bga_ref[0:1, :])
        o_ref[...] = (g_s * bs_ref[...] + g_a * ba_ref[...]).astype(_ACT)

    row = lambda col: pl.BlockSpec((tm, 1024), lambda i: (i, col))
    return pl.pallas_call(
        body, name=name, out_shape=jax.ShapeDtypeStruct((t, D_MODEL), _ACT), grid=(t // tm,),
        in_specs=[row(0), row(0), row(P_G // 1024), row(P_G // 1024 + 1),
                  pl.BlockSpec((SUB, 1024), lambda i: (0, 0)), pl.BlockSpec((SUB, 1024), lambda i: (0, 1))],
        out_specs=row(0), compiler_params=_cp(("parallel",)),
    )(bs, ba, proj, proj, bg8, bg8)


def _merge_bwd(d_merged, bs, ba, proj, bg8, name):
    t = bs.shape[0]
    tm = _tm_rows(t)

    def body(dm_ref, bs_ref, ba_ref, gs_ref, ga_ref, bgs_ref, bga_ref, dbs_ref, dba_ref, dg_ref, acc_ref):
        @pl.when(pl.program_id(0) == 0)
        def _():
            acc_ref[...] = jnp.zeros_like(acc_ref)

        dm = dm_ref[...].astype(F32)
        g_s = _sigmoid(gs_ref[...] + bgs_ref[0:1, :])
        g_a = _sigmoid(ga_ref[...] + bga_ref[0:1, :])
        dbs_ref[...] = (dm * g_s).astype(_ACT)
        dba_ref[...] = (dm * g_a).astype(_ACT)
        dgs = dm * bs_ref[...].astype(F32) * g_s * (1.0 - g_s)
        dga = dm * ba_ref[...].astype(F32) * g_a * (1.0 - g_a)
        dg_ref[:, 0:1024] = dgs.astype(_ACT)
        dg_ref[:, 1024:2048] = dga.astype(_ACT)
        acc_ref[0:1, 0:1024] += jnp.sum(dgs, axis=0, keepdims=True)
        acc_ref[0:1, 1024:2048] += jnp.sum(dga, axis=0, keepdims=True)

    row = lambda col: pl.BlockSpec((tm, 1024), lambda i: (i, col))
    return pl.pallas_call(
        body, name=name,
        out_shape=(jax.ShapeDtypeStruct((t, D_MODEL), _ACT), jax.ShapeDtypeStruct((t, D_MODEL), _ACT),
                   jax.ShapeDtypeStruct((t, P_W), _ACT), jax.ShapeDtypeStruct((SUB, 2048), F32)),
        grid=(t // tm,),
        in_specs=[row(0), row(0), row(0), row(P_G // 1024), row(P_G // 1024 + 1),
                  pl.BlockSpec((SUB, 1024), lambda i: (0, 0)), pl.BlockSpec((SUB, 1024), lambda i: (0, 1))],
        out_specs=(row(0), row(0), pl.BlockSpec((tm, 2048), lambda i: (i, P_G // 2048)),
                   pl.BlockSpec((SUB, 2048), lambda i: (0, 0))),
        compiler_params=_cp(("arbitrary",)),
    )(d_merged, bs, ba, proj, proj, bg8, bg8)


def _place_tail(d_k, d_v, d_dt, d_proj, name):
    t = d_k.shape[0]
    tm = _tm_rows(t)
    width = P_W - P_MAIN

    def body(k_ref, v_ref, dt_ref, _, o_ref):
        o_ref[:, T_K:T_K + LANE] = k_ref[...]
        o_ref[:, T_V:T_V + LANE] = v_ref[...]
        o_ref[:, T_DT:T_DT + LANE] = dt_ref[...]
        o_ref[:, T_DT + LANE:width] = jnp.zeros((tm, width - T_DT - LANE), o_ref.dtype)

    blk = pl.BlockSpec((tm, LANE), lambda i: (i, 0))
    return pl.pallas_call(
        body, name=name, out_shape=jax.ShapeDtypeStruct(d_proj.shape, d_proj.dtype), grid=(t // tm,),
        in_specs=[blk, blk, blk, pl.BlockSpec(memory_space=pl.ANY)],
        out_specs=pl.BlockSpec((tm, width), lambda i: (i, P_MAIN // width)),
        input_output_aliases={3: 0}, compiler_params=_cp(("parallel",)),
    )(d_k, d_v, d_dt, d_proj)


def _ln_stats(r):
    mu = jnp.mean(r, axis=-1, keepdims=True)
    xc = r - mu
    var = jnp.mean(xc * xc, axis=-1, keepdims=True)
    rstd = lax.rsqrt(var + LN_EPS)
    return xc * rstd, rstd


def _ln_bwd(dxhat, xhat, rstd):
    return rstd * (dxhat - jnp.mean(dxhat, axis=-1, keepdims=True)
                   - xhat * jnp.mean(dxhat * xhat, axis=-1, keepdims=True))


def _ln1_fwd(x, mix, g8, b8, name):
    t = x.shape[0]
    tm = _tm_rows(t)

    def body(x_ref, m_ref, g_ref, b_ref, xh_ref, h_ref, rs_ref):
        xhat, rstd = _ln_stats(ALPHA * x_ref[...] + m_ref[...])
        xh_ref[...] = xhat
        h_ref[...] = (xhat * g_ref[0:1, :] + b_ref[0:1, :]).astype(_ACT)
        rs_ref[...] = rstd

    row = pl.BlockSpec((tm, D_MODEL), lambda i: (i, 0))
    par = pl.BlockSpec((SUB, D_MODEL), lambda i: (0, 0))
    return pl.pallas_call(
        body, name=name,
        out_shape=(jax.ShapeDtypeStruct((t, D_MODEL), F32), jax.ShapeDtypeStruct((t, D_MODEL), _ACT),
                   jax.ShapeDtypeStruct((t, 1), F32)),
        grid=(t // tm,), in_specs=[row, row, par, par],
        out_specs=(row, row, pl.BlockSpec((tm, 1), lambda i: (i, 0))),
        compiler_params=_cp(("parallel",)),
    )(x, mix, g8, b8)


def _ln2_loss(xhat1, ffn, target, g1_8, b1_8, g2_8, b2_8, name):
    t = xhat1.shape[0]
    tm = _tm_rows(t)

    def body(xh_ref, f_ref, t_ref, g1_ref, b1_ref, g2_ref, b2_ref, d_ref, db_ref, acc_ref):
        @pl.when(pl.program_id(0) == 0)
        def _():
            acc_ref[...] = jnp.zeros_like(acc_ref)

        h1 = xh_ref[...] * g1_ref[0:1, :] + b1_ref[0:1, :]
        xhat, rstd = _ln_stats(ALPHA * h1 + f_ref[...])
        diff = xhat * g2_ref[0:1, :] + b2_ref[0:1, :] - t_ref[...]
        dy = diff * (1.0 / D_MODEL)
        acc_ref[0:1, :] += jnp.sum(dy * xhat, axis=0, keepdims=True)
        acc_ref[1:2, :] += jnp.sum(dy, axis=0, keepdims=True)
        acc_ref[2:3, :] += jnp.sum(diff * diff, axis=0, keepdims=True)
        d = _ln_bwd(dy * g2_ref[0:1, :], xhat, rstd)
        d_ref[...] = d
        db_ref[...] = d.astype(_ACT)

    row = pl.BlockSpec((tm, D_MODEL), lambda i: (i, 0))
    par = pl.BlockSpec((SUB, D_MODEL), lambda i: (0, 0))
    return pl.pallas_call(
        body, name=name,
        out_shape=(jax.ShapeDtypeStruct((t, D_MODEL), F32), jax.ShapeDtypeStruct((t, D_MODEL), _ACT),
                   jax.ShapeDtypeStruct((SUB, D_MODEL), F32)),
        grid=(t // tm,), in_specs=[row, row, row, par, par, par, par],
        out_specs=(row, row, par), compiler_params=_cp(("arbitrary",)),
    )(xhat1, ffn, target, g1_8, b1_8, g2_8, b2_8)


def _ln1_bwd(d_r2, d_h1_ffn, xhat1, rstd1, g1_8, name):
    t = xhat1.shape[0]
    tm = _tm_rows(t)

    def body(d2_ref, df_ref, xh_ref, rs_ref, g_ref, d_ref, db_ref, acc_ref):
        @pl.when(pl.program_id(0) == 0)
        def _():
            acc_ref[...] = jnp.zeros_like(acc_ref)

        dh = ALPHA * d2_ref[...] + df_ref[...]
        xhat = xh_ref[...]
        acc_ref[0:1, :] += jnp.sum(dh * xhat, axis=0, keepdims=True)
        acc_ref[1:2, :] += jnp.sum(dh, axis=0, keepdims=True)
        d = _ln_bwd(dh * g_ref[0:1, :], xhat, rs_ref[...])
        d_ref[...] = d
        db_ref[...] = d.astype(_ACT)

    row = pl.BlockSpec((tm, D_MODEL), lambda i: (i, 0))
    par = pl.BlockSpec((SUB, D_MODEL), lambda i: (0, 0))
    return pl.pallas_call(
        body, name=name,
        out_shape=(jax.ShapeDtypeStruct((t, D_MODEL), F32), jax.ShapeDtypeStruct((t, D_MODEL), _ACT),
                   jax.ShapeDtypeStruct((SUB, D_MODEL), F32)),
        grid=(t // tm,), in_specs=[row, row, row, pl.BlockSpec((tm, 1), lambda i: (i, 0)), par],
        out_specs=(row, row, par), compiler_params=_cp(("arbitrary",)),
    )(d_r2, d_h1_ffn, xhat1, rstd1, g1_8)


def _ffn_tm(t):
    return min(256, t)


def _ffn_act_fwd(u0, cw8, cb8, name):
    t = u0.shape[0]
    tm = _ffn_tm(t)

    def body(g_ref, gp_ref, v_ref, vp_ref, wg_ref, wv_ref, bg_ref, bv_ref, o_ref, u_ref):
        i = pl.program_id(0)
        gprev = jnp.where(i > 0, gp_ref[SUB:HALO, :].astype(F32), 0.0)
        vprev = jnp.where(i > 0, vp_ref[SUB:HALO, :].astype(F32), 0.0)
        gate = _conv_pre(g_ref[...].astype(F32), gprev, wg_ref, bg_ref[0:1, :], FFN_K)
        val = _conv_pre(v_ref[...].astype(F32), vprev, wv_ref, bv_ref[0:1, :], FFN_K)
        o_ref[...] = (gate * _sigmoid(gate) * val).astype(_ACT)
        u_ref[:, 0:D_FF] = gate.astype(_ACT)
        u_ref[:, D_FF:2 * D_FF] = val.astype(_ACT)

    cur = lambda col: pl.BlockSpec((tm, D_FF), lambda i: (i, col))
    prv = lambda col: _prev_halo(tm, D_FF, lambda i: (i, col))
    par = lambda col: pl.BlockSpec((SUB, D_FF), lambda i: (0, col))
    return pl.pallas_call(
        body, name=name,
        out_shape=(jax.ShapeDtypeStruct((t, D_FF), _ACT), jax.ShapeDtypeStruct((t, 2 * D_FF), _ACT)),
        grid=(t // tm,),
        in_specs=[cur(0), prv(0), cur(1), prv(1), par(0), par(1), par(0), par(1)],
        out_specs=(pl.BlockSpec((tm, D_FF), lambda i: (i, 0)), pl.BlockSpec((tm, 2 * D_FF), lambda i: (i, 0))),
        compiler_params=_cp(("parallel",)),
    )(u0, u0, u0, u0, cw8, cw8, cb8, cb8)


def _ffn_act_bwd(u0, u, cw8, d_a, name):
    t = u0.shape[0]
    tm = _ffn_tm(t)
    nt = t // tm

    def body(g0_ref, v0_ref, g_ref, gn_ref, v_ref, vn_ref, wg_ref, wv_ref, da_ref, dan_ref, du_ref, acc_ref):
        i = pl.program_id(0)

        @pl.when(i == 0)
        def _():
            acc_ref[...] = jnp.zeros_like(acc_ref)

        def grads(gate, val, da):
            return da * val * _silu_grad(gate), da * gate * _sigmoid(gate)

        dgate, dval = grads(g_ref[...].astype(F32), v_ref[...].astype(F32), da_ref[...].astype(F32))
        dgate_n, dval_n = grads(gn_ref[0:SUB, :].astype(F32), vn_ref[0:SUB, :].astype(F32),
                                dan_ref[0:SUB, :].astype(F32))
        last = i == nt - 1
        du_ref[:, 0:D_FF] = _conv_grads(dgate, jnp.where(last, 0.0, dgate_n), g0_ref[...].astype(F32), wg_ref,
                                        acc_ref, FFN_K, slice(0, D_FF)).astype(_ACT)
        du_ref[:, D_FF:2 * D_FF] = _conv_grads(dval, jnp.where(last, 0.0, dval_n), v0_ref[...].astype(F32), wv_ref,
                                               acc_ref, FFN_K, slice(D_FF, 2 * D_FF)).astype(_ACT)

    cur = lambda col: pl.BlockSpec((tm, D_FF), lambda i: (i, col))
    nxt = lambda col: _next_halo(tm, t, D_FF, lambda i: (i, col))
    par = lambda col: pl.BlockSpec((SUB, D_FF), lambda i: (0, col))
    return pl.pallas_call(
        body, name=name,
        out_shape=(jax.ShapeDtypeStruct((t, 2 * D_FF), _ACT), jax.ShapeDtypeStruct((SUB, 2 * D_FF), F32)),
        grid=(nt,),
        in_specs=[cur(0), cur(1), cur(0), nxt(0), cur(1), nxt(1), par(0), par(1), cur(0), nxt(0)],
        out_specs=(pl.BlockSpec((tm, 2 * D_FF), lambda i: (i, 0)),
                   pl.BlockSpec((SUB, 2 * D_FF), lambda i: (0, 0))),
        compiler_params=_cp(("arbitrary",)),
    )(u0, u0, u, u, u, u, cw8, cw8, d_a, d_a)


_REST = ("w_branch_ssm", "w_branch_attn", "w_mix_out", "w_up", "w_down")


def _mm_side(*args, side, **kw):
    if side is None:
        return _mm(*args, **kw), []
    return _mm(*args, side=side, **kw)


def _local_step(x, x_bf, target, wts, ex):
    t = x.shape[0]
    wp = wts["wp"]
    scw = wts["ssm_conv_w"]
    scb = wts["ssm_conv_b"]
    fcw8 = _rows8(wts["ffn_conv_w"])
    fcb8 = _rows8(wts["ffn_conv_b"])
    pad_lane = lambda p: jnp.concatenate([p.astype(F32), jnp.zeros((1, LANE - p.shape[1]), F32)], axis=1)
    dtb8 = _rows8(pad_lane(wts["ssm_dt_bias"]))
    alog8 = _rows8(pad_lane(wts["ssm_a_log"]))
    dsk8 = _rows8(pad_lane(wts["ssm_d"]))
    bias_table = jnp.concatenate([wts["rel_bias"].T.astype(F32), wts["attn_sinks"].T.astype(F32),
                                  jnp.zeros((A_HEADS, BIAS_ROWS - REL_BUCKETS - 1), F32)], axis=1)
    nw8 = _rows8(wts["ssm_norm_w"])
    bg8 = _rows8(wts["b_gate"])
    g1_8, b1_8, g2_8, b2_8 = (_rows8(wts[k]) for k in ("ln1_g", "ln1_b", "ln2_g", "ln2_b"))
    xs_w8, xs_b8 = _rows8(scw[:, :D_INNER]), _rows8(scb[:, :D_INNER])
    bc_w8, bc_b8 = _rows8(scw[:, D_INNER:]), _rows8(scb[:, D_INNER:])

    proj, stacks = _mm_side(x_bf, wp[:, :P_MAIN], "mm_in", out_dtype=_ACT, side=ex.gather_rest())
    wts = dict(wts, **ex.rest_weights(stacks))
    w_bs, w_ba, w_mix, w_up, w_dn = (wts[k] for k in _REST)
    tail = _mm(x_bf, wp[:, P_MAIN:], "mm_in_tail")
    xs_c, xs_pre = _conv_silu_fwd(proj, P_XS // _TC, D_INNER // _TC, xs_w8, xs_b8, "conv_xs_fwd")
    bc_c, bc_pre = _conv_silu_fwd(proj, P_BC // _TC, 1024 // _TC, bc_w8, bc_b8, "conv_bc_fwd")
    y_ssd, y_ssm, hprev = _ssd_fwd(xs_c, bc_c, proj, tail, dtb8, alog8, dsk8, nw8, "ssd_fwd")
    bias = _bias_expand(bias_table, "bias_expand").reshape(2, A_HEADS, WIN, 2 * WIN)
    y_attn = _attn_fwd(proj, tail, bias, "attn_fwd")
    bs = _mm(y_ssm, w_bs, "mm_bs", out_dtype=_ACT)
    ba = _mm(y_attn, w_ba, "mm_ba", out_dtype=_ACT)
    merged = _merge_fwd(bs, ba, proj, bg8, "merge_fwd")
    mix = _mm(merged, w_mix, "mm_mix", out_dtype=_ACT)
    xhat1, h1_bf, rstd1 = _ln1_fwd(x, mix, g1_8, b1_8, "ln1_fwd")
    u0 = _mm(h1_bf, w_up, "mm_up", out_dtype=_ACT)
    act, u_conv = _ffn_act_fwd(u0, fcw8, fcb8, "ffn_act_fwd")
    ffn = _mm(act, w_dn, "mm_down", out_dtype=_ACT)
    d_r2, d_r2_bf, acc_ln2 = _ln2_loss(xhat1, ffn, target, g1_8, b1_8, g2_8, b2_8, "ln2_loss")
    d_w_dn = _mm(act, d_r2_bf, "mm_dw_down", trans_a=True)
    d_act = _mm(d_r2_bf, w_dn.T, "mm_d_act", out_dtype=_ACT)
    d_u0, acc_ffn = _ffn_act_bwd(u0, u_conv, fcw8, d_act, "ffn_act_bwd")
    d_w_up = _mm(h1_bf, d_u0, "mm_dw_up", trans_a=True)
    d_h1_ffn = _mm(d_u0, w_up.T, "mm_d_h1", out_dtype=_ACT)
    d_r1, d_r1_bf, acc_ln1 = _ln1_bwd(d_r2, d_h1_ffn, xhat1, rstd1, g1_8, "ln1_bwd")
    d_w_mix = _mm(merged, d_r1_bf, "mm_dw_mix", trans_a=True)
    d_merged = _mm(d_r1_bf, w_mix.T, "mm_d_merged", out_dtype=_ACT)
    d_bs, d_ba, d_proj, acc_bg = _merge_bwd(d_merged, bs, ba, proj, bg8, "merge_bwd")
    d_w_bs = _mm(y_ssm, d_bs, "mm_dw_bs", trans_a=True)
    d_w_ba = _mm(y_attn, d_ba, "mm_dw_ba", trans_a=True)
    d_y_ssm = _mm(d_bs, w_bs.T, "mm_d_yssm", out_dtype=_ACT)
    d_y_attn = _mm(d_ba, w_ba.T, "mm_d_yattn", out_dtype=_ACT)
    d_proj, d_k, d_v, d_bias = _attn_bwd(proj, tail, bias, y_attn, d_y_attn, d_proj, "attn_bwd")
    d_table = _bias_reduce(d_bias.reshape(A_HEADS, WIN * 2 * WIN), "bias_reduce")
    d_xs_c, d_bc_c, d_proj, d_dt, acc_ssd, acc_nw = _ssd_bwd(
        d_y_ssm, y_ssd, xs_c, bc_c, proj, tail, hprev, dtb8, alog8, dsk8, nw8, d_proj, "ssd_bwd")
    d_proj, acc_xs = _conv_silu_bwd(proj, P_XS // _TC, D_INNER // _TC, xs_pre, xs_w8, d_xs_c, d_proj, "conv_xs_bwd")
    d_proj, acc_bc = _conv_silu_bwd(proj, P_BC // _TC, 1024 // _TC, bc_pre, bc_w8, d_bc_c, d_proj, "conv_bc_bwd")
    d_proj = _place_tail(d_k, d_v, d_dt, d_proj, "place_tail")
    grads = {"w_branch_ssm": d_w_bs, "w_branch_attn": d_w_ba, "w_mix_out": d_w_mix, "w_up": d_w_up, "w_down": d_w_dn}
    d_wp, landed_rest = _mm_side(x_bf, d_proj, "mm_dw_in", trans_a=True, side=ex.reduce_job(grads))
    d_x, landed_in = _mm_side(d_proj, wp.T, "mm_d_x", res=d_r1, res_scale=ALPHA, side=ex.reduce_job({"wp": d_wp}))
    grads.update({
        "wp": d_wp,
        "ssm_conv_w": jnp.concatenate([acc_xs[0:SSM_K], acc_bc[0:SSM_K]], axis=1),
        "ffn_conv_w": acc_ffn[0:FFN_K],
    })
    small = {
        "rel_bias": d_table[:, 0:REL_BUCKETS].T,
        "b_gate": acc_bg[0:1],
        "ssm_conv_b": jnp.concatenate([acc_xs[SSM_K:SSM_K + 1], acc_bc[SSM_K:SSM_K + 1]], axis=1),
        "ssm_dt_bias": acc_ssd[0:1, 0:N_HEADS], "ssm_a_log": acc_ssd[1:2, 0:N_HEADS], "ssm_d": acc_ssd[2:3, 0:N_HEADS],
        "ssm_norm_w": acc_nw[0:1],
        "attn_sinks": d_table[:, REL_BUCKETS:REL_BUCKETS + 1].T,
        "ln1_g": acc_ln1[0:1], "ln1_b": acc_ln1[1:2],
        "ffn_conv_b": acc_ffn[FFN_K:FFN_K + 1],
        "ln2_g": acc_ln2[0:1], "ln2_b": acc_ln2[1:2],
        "loss_lanes": acc_ln2[2:3],
    }
    return d_x, grads, small, landed_rest + landed_in


_MATS = (("w_in", (1024, 2120), 1), ("w_branch_ssm", (512, 1024), 0), ("w_branch_attn", (256, 1024), 0),
         ("w_mix_out", (256, 1024), 0), ("w_up", (1024, 1408), 1), ("w_down", (704, 1024), 0))
_CONVS = (("ssm_conv_w", (4, 768)), ("ffn_conv_w", (3, 1408)))
_CONV_ROWS = 64

_SMALL = (("rel_bias", (32, 16)), ("b_gate", (1, 2048)), ("ssm_conv_b", (1, 3072)), ("ssm_dt_bias", (1, 32)),
          ("ssm_a_log", (1, 32)), ("ssm_d", (1, 32)), ("ssm_norm_w", (1, 2048)), ("attn_sinks", (1, 16)),
          ("ln1_g", (1, 1024)), ("ln1_b", (1, 1024)), ("ffn_conv_b", (1, 5632)), ("ln2_g", (1, 1024)),
          ("ln2_b", (1, 1024)), ("g_ssm_conv_w", (4, 3072)), ("g_ffn_conv_w", (3, 5632)), ("loss_lanes", (1, 1024)))


def _small_rows(shape):
    rows = -(-(shape[0] * shape[1]) // LANE)
    return -(-rows // SUB) * SUB


def _as_rows(a, rows, dtype):
    flat = a.reshape(-1).astype(dtype)
    flat = jnp.concatenate([flat, jnp.zeros((rows * LANE - flat.shape[0],), dtype)])
    return flat.reshape(rows, LANE)


def _pack_small(parts):
    blocks = [_as_rows(parts[n], _small_rows(s), F32) if n in parts else jnp.zeros((_small_rows(s), LANE), F32)
              for n, s in _SMALL]
    return jnp.concatenate(blocks, axis=0)


def _unpack_small(packed):
    out, at = {}, 0
    for n, s in _SMALL:
        rows = _small_rows(s)
        out[n] = packed[at:at + rows].reshape(-1)[:s[0] * s[1]].reshape(s)
        at += rows
    return out


def _to_stack(full, shape, axis):
    if axis == 0:
        return full.reshape((N_CHIPS,) + shape)
    return jnp.transpose(full.reshape(shape[0], N_CHIPS, shape[1]), (1, 0, 2))


def _from_stack(stack, axis):
    n, r, c = stack.shape
    if axis == 0:
        return stack.reshape(n * r, c)
    return jnp.transpose(stack, (1, 0, 2)).reshape(r, n * c)


_IN_SHARD = IN_COLS // N_CHIPS


def _cols_of_stack(stack, o, w):
    parts = []
    while w > 0:
        j, a = divmod(o, _IN_SHARD)
        n = min(w, _IN_SHARD - a)
        parts.append(stack[j][:, a:a + n])
        o, w = o + n, w - n
    return parts


def _pack_w_in_stack(stack):
    cols, at = [], 0
    for o, w, pk in sorted(_PIECES, key=lambda p: p[2]):
        if pk > at:
            cols.append(jnp.zeros((stack.shape[1], pk - at), stack.dtype))
        cols += _cols_of_stack(stack, o, w)
        at = pk + w
    cols.append(jnp.zeros((stack.shape[1], P_W - at), stack.dtype))
    return jnp.concatenate(cols, axis=1)


def _unpack_w_in_stack(wp):
    slabs = []
    for j in range(N_CHIPS):
        lo, hi = j * _IN_SHARD, (j + 1) * _IN_SHARD
        cols = []
        for o, w, pk in sorted(_PIECES):
            a, b = max(o, lo), min(o + w, hi)
            if a < b:
                cols.append(wp[:, pk + a - o:pk + b - o])
        slabs.append(jnp.concatenate(cols, axis=1))
    return jnp.stack(slabs)


_MESH = pl.DeviceIdType.MESH
_HBM = pl.BlockSpec(memory_space=pltpu.HBM)


def _position():
    return lax.axis_index("x"), lax.axis_index("y"), lax.axis_index("c")


def _other_chips(x, y):
    return ((1 - x, y), (x, 1 - y), (1 - x, 1 - y))


def _remote(src, dst, send_sem, recv_sem, to):
    return pltpu.make_async_remote_copy(src_ref=src, dst_ref=dst, send_sem=send_sem, recv_sem=recv_sem,
                                        device_id=to, device_id_type=_MESH)


def _run_job(job, name):
    n_in, n_out = len(job.inputs), len(job.out_shape)

    def body(*refs):
        parts = (refs[:n_in], refs[n_in:n_in + n_out], refs[n_in + n_out:])
        job.start(*parts)
        job.finish(*parts)

    return pl.pallas_call(body, name=name, out_shape=list(job.out_shape), in_specs=[_HBM] * n_in,
                          out_specs=[_HBM] * n_out, scratch_shapes=list(job.sems))(*job.inputs)


def _cast_rows(x, dtype, name, side):
    t, cols = x.shape
    tm = min(1024, t)
    nt = t // tm
    ns_in, ns_out = len(side.inputs), len(side.out_shape)

    def body(*refs):
        x_ref, o_ref = refs[0], refs[1 + ns_in]
        job_refs = (refs[1:1 + ns_in], refs[2 + ns_in:2 + ns_in + ns_out], refs[2 + ns_in + ns_out:])
        i = pl.program_id(0)

        @pl.when(i == 0)
        def _():
            side.start(*job_refs)

        o_ref[...] = x_ref[...].astype(dtype)

        @pl.when(i == nt - 1)
        def _():
            side.finish(*job_refs)

    blk = pl.BlockSpec((tm, cols), lambda i: (i, 0))
    outs = pl.pallas_call(
        body, name=name, out_shape=[jax.ShapeDtypeStruct((t, cols), dtype)] + list(side.out_shape), grid=(nt,),
        in_specs=[blk] + [_HBM] * ns_in, out_specs=[blk] + [_HBM] * ns_out, scratch_shapes=list(side.sems),
        compiler_params=_cp(("arbitrary",)),
    )(x, *side.inputs)
    return outs[0], list(outs[1:])


def _gather_job(shards, copy_own=True):
    n = len(shards)

    def plan(s_refs, o_refs, sems):
        send_sems, recv_sems, local_sems = sems
        x, y, c = _position()
        me = 2 * x + y
        sib = (x, y, 1 - c)
        chips = _other_chips(x, y)

        def copy(m, k, chip_idx, half, to, src=None):
            dst = o_refs[m].at[chip_idx, half]
            return _remote(dst if src is None else src, dst, send_sems.at[6 * m + k], recv_sems.at[6 * m + k], to)

        local = [pltpu.make_async_copy(s_refs[m], o_refs[m].at[me], local_sems.at[m]) for m in range(n)]
        local = local if copy_own else []
        first = [copy(m, i, me, c, (cx, cy, c), src=s_refs[m].at[c])
                 for i, (cx, cy) in enumerate(chips) for m in range(n)]
        return c, sib, chips, copy, local, first

    def start(s_refs, o_refs, sems):
        _, _, _, _, local, first = plan(s_refs, o_refs, sems)
        for cp in local + first:
            cp.start()

    def finish(s_refs, o_refs, sems):
        c, sib, chips, copy, local, first = plan(s_refs, o_refs, sems)
        passed = []
        for i, (cx, cy) in enumerate(chips):
            for m in range(n):
                copy(m, i, 2 * cx + cy, c, sib).wait_recv()
                passed.append(copy(m, 3 + i, 2 * cx + cy, c, sib))
                passed[-1].start()
        for i, (cx, cy) in enumerate(chips):
            for m in range(n):
                copy(m, 3 + i, 2 * cx + cy, 1 - c, sib).wait_recv()
        for cp in first + passed:
            cp.wait_send()
        for cp in local:
            cp.wait()

    return _SideJob(
        inputs=list(shards), out_shape=[jax.ShapeDtypeStruct((N_CHIPS,) + s.shape, s.dtype) for s in shards],
        sems=[pltpu.SemaphoreType.DMA((6 * n,)), pltpu.SemaphoreType.DMA((6 * n,)), pltpu.SemaphoreType.DMA((n,))],
        start=start, finish=finish)


def _swap_halves(gs, name):
    n = len(gs)

    def body(*refs):
        g_refs, o_refs = refs[:n], refs[n:2 * n]
        send_sems, recv_sems = refs[2 * n:]
        x, y, c = _position()
        cps = [_remote(g_refs[m].at[j, 1 - c], o_refs[m].at[j], send_sems.at[N_CHIPS * m + j],
                       recv_sems.at[N_CHIPS * m + j], (x, y, 1 - c)) for m in range(n) for j in range(N_CHIPS)]
        for cp in cps:
            cp.start()
        for cp in cps:
            cp.wait()

    return pl.pallas_call(
        body, name=name,
        out_shape=[jax.ShapeDtypeStruct((N_CHIPS,) + g.shape[2:], g.dtype) for g in gs],
        in_specs=[_HBM] * n, out_specs=[_HBM] * n,
        scratch_shapes=[pltpu.SemaphoreType.DMA((N_CHIPS * n,)), pltpu.SemaphoreType.DMA((N_CHIPS * n,))],
    )(*gs)


def _scatter_job(ps):
    n = len(ps)

    def copies(p_refs, o_refs, sems):
        send_sems, recv_sems = sems
        x, y, c = _position()
        return [_remote(p_refs[m].at[2 * cx + cy], o_refs[m].at[i], send_sems.at[3 * m + i], recv_sems.at[3 * m + i],
                        (cx, cy, c)) for i, (cx, cy) in enumerate(_other_chips(x, y)) for m in range(n)]

    def start(*parts):
        for cp in copies(*parts):
            cp.start()

    def finish(*parts):
        for cp in copies(*parts):
            cp.wait()

    return _SideJob(
        inputs=list(ps), out_shape=[jax.ShapeDtypeStruct((N_CHIPS - 1,) + p.shape[1:], p.dtype) for p in ps],
        sems=[pltpu.SemaphoreType.DMA((3 * n,)), pltpu.SemaphoreType.DMA((3 * n,))], start=start, finish=finish)


def _join_halves(fulls):
    n = len(fulls)

    def body(*refs):
        o_refs = refs[n:2 * n]
        send_sems, recv_sems = refs[2 * n:]
        x, y, c = _position()
        cps = [_remote(o_refs[m].at[c], o_refs[m].at[c], send_sems.at[m], recv_sems.at[m], (x, y, 1 - c))
               for m in range(n)]
        for cp in cps:
            cp.start()
        for cp in cps:
            cp.wait()

    return pl.pallas_call(
        body, name="join_halves",
        out_shape=[jax.ShapeDtypeStruct(f.shape, f.dtype) for f in fulls],
        in_specs=[_HBM] * n, out_specs=[_HBM] * n, input_output_aliases={m: m for m in range(n)},
        scratch_shapes=[pltpu.SemaphoreType.DMA((n,)), pltpu.SemaphoreType.DMA((n,))],
    )(*fulls)


def _allgather_small(mine, name):
    m_per, n = mine.shape

    def body(x_ref, out_ref, send_sems, recv_sems, local_sem):
        x, y, c = _position()
        me, sibling = (x, y, c), (x, y, 1 - c)
        chips = _other_chips(x, y)

        def rows(px, py, pc):
            return out_ref.at[pl.ds((4 * px + 2 * py + pc) * m_per, m_per), :]

        def copy(k, block, to, src=None):
            return pltpu.make_async_remote_copy(src_ref=rows(*block) if src is None else src, dst_ref=rows(*block),
                                                send_sem=send_sems.at[k], recv_sem=recv_sems.at[k],
                                                device_id=to, device_id_type=_MESH)

        own = pltpu.make_async_copy(x_ref, rows(*me), local_sem)
        own.start()
        first = [copy(0, me, sibling, src=x_ref)]
        first += [copy(1 + j, me, (*chip, c), src=x_ref) for j, chip in enumerate(chips)]
        for cp in first:
            cp.start()
        passed = [copy(4 + j, (*chip, c), sibling) for j, chip in enumerate(chips)]
        for j, chip in enumerate(chips):
            copy(1 + j, (*chip, c), me).wait_recv()
            passed[j].start()
        copy(0, sibling, me).wait_recv()
        for j, chip in enumerate(chips):
            copy(4 + j, (*chip, 1 - c), me).wait_recv()
        for cp in first + passed:
            cp.wait_send()
        own.wait()

    return pl.pallas_call(
        body, name=name, out_shape=jax.ShapeDtypeStruct((N_DEV * m_per, n), mine.dtype),
        in_specs=[pl.BlockSpec(memory_space=pltpu.VMEM)], out_specs=pl.BlockSpec(memory_space=pltpu.VMEM),
        scratch_shapes=[pltpu.SemaphoreType.DMA((7,)), pltpu.SemaphoreType.DMA((7,)), pltpu.SemaphoreType.DMA],
    )(mine)


_ADD_BLOCK_BYTES = 3 << 20


def _add_rows(hr, cols):
    if hr * cols * 4 <= _ADD_BLOCK_BYTES:
        return hr
    return _pick(hr, (256, 128, 64, 32, 16))


def _add_own_half(g, recv, c_idx, name):
    nseg, _, hr, cols = g.shape
    tr = _add_rows(hr, cols)

    def body(c_ref, g_ref, r_ref, o_ref, ob_ref):
        s = g_ref[...] + r_ref[...]
        o_ref[...] = s
        ob_ref[...] = s.astype(jnp.bfloat16)

    blk = pl.BlockSpec((None, tr, cols), lambda j, i, c_ref: (j, i, 0))
    return pl.pallas_call(
        body, name=name,
        out_shape=(jax.ShapeDtypeStruct((nseg, hr, cols), F32), jax.ShapeDtypeStruct((nseg, hr, cols), jnp.bfloat16)),
        grid_spec=pltpu.PrefetchScalarGridSpec(
            num_scalar_prefetch=1, grid=(nseg, hr // tr),
            in_specs=[pl.BlockSpec((None, None, tr, cols), lambda j, i, c_ref: (j, c_ref[0], i, 0)), blk],
            out_specs=(blk, blk)),
        compiler_params=_cp(("parallel", "parallel")),
    )(c_idx, g, recv)


def _add_chips(p, recv, chip_idx, c_idx, name):
    _, hr, cols = p.shape
    tr = _add_rows(hr, cols)

    def body(j_ref, c_ref, p_ref, r_ref, o_ref):
        o_ref[...] = ((p_ref[...] + r_ref[0].astype(F32)) + r_ref[1].astype(F32)) + r_ref[2].astype(F32)

    return pl.pallas_call(
        body, name=name, out_shape=jax.ShapeDtypeStruct((2, hr, cols), F32),
        grid_spec=pltpu.PrefetchScalarGridSpec(
            num_scalar_prefetch=2, grid=(hr // tr,),
            in_specs=[pl.BlockSpec((None, tr, cols), lambda i, j_ref, c_ref: (j_ref[0], i, 0)),
                      pl.BlockSpec((N_CHIPS - 1, tr, cols), lambda i, j_ref, c_ref: (0, i, 0))],
            out_specs=pl.BlockSpec((None, tr, cols), lambda i, j_ref, c_ref: (c_ref[0], i, 0))),
        compiler_params=_cp(("parallel",)),
    )(chip_idx, c_idx, p, recv)


def _adam_math(w, g, m, v):
    m = ADAM_B1 * m + (1.0 - ADAM_B1) * g
    v = ADAM_B2 * v + (1.0 - ADAM_B2) * (g * g)
    m_hat = m / (1.0 - ADAM_B1 ** ADAM_STEP)
    v_hat = v / (1.0 - ADAM_B2 ** ADAM_STEP)
    delta = -ADAM_LR * (m_hat / (jnp.sqrt(v_hat) + ADAM_EPS) + ADAM_WD * w)
    return delta, m, v


def _adam_big(w, g, m, v, name):
    rows, cols = w.shape
    tr = _pick(rows, (256, 128, 64, 32, 16, 8)) if rows % SUB == 0 else rows

    def body(w_ref, g_ref, m_ref, v_ref, d_ref, mo_ref, vo_ref):
        d_ref[...], mo_ref[...], vo_ref[...] = _adam_math(w_ref[...], g_ref[...], m_ref[...], v_ref[...])

    blk = pl.BlockSpec((tr, cols), lambda i: (i, 0))
    shp = jax.ShapeDtypeStruct((rows, cols), F32)
    return pl.pallas_call(
        body, name=name, out_shape=(shp, shp, shp), grid=(rows // tr,),
        in_specs=[blk, blk, blk, blk], out_specs=(blk, blk, blk), compiler_params=_cp(("parallel",)),
    )(w, g, m, v)


def _adam_small(w, gathered, m, v):
    rows = w.shape[0]

    def body(w_ref, a_ref, m_ref, v_ref, g_ref, d_ref, mo_ref, vo_ref):
        g = a_ref[0:rows, :]
        for k in range(1, N_DEV):
            g = g + a_ref[k * rows:(k + 1) * rows, :]
        g_ref[...] = g
        d_ref[...], mo_ref[...], vo_ref[...] = _adam_math(w_ref[...], g, m_ref[...], v_ref[...])

    shp = jax.ShapeDtypeStruct((rows, LANE), F32)
    return pl.pallas_call(body, name="adam_small", out_shape=(shp, shp, shp, shp), compiler_params=_cp(None))(
        w, gathered, m, v)


_WEIGHTS = ("rel_bias", "w_in", "b_gate", "ssm_conv_w", "ssm_conv_b", "ssm_dt_bias", "ssm_a_log", "ssm_d",
            "ssm_norm_w", "attn_sinks", "w_branch_ssm", "w_branch_attn", "w_mix_out", "ln1_g", "ln1_b", "w_up",
            "ffn_conv_w", "ffn_conv_b", "w_down", "ln2_g", "ln2_b")
_REPLICATED = tuple(n for n, _ in _SMALL[:13])


class _Exchange:
    def __init__(self, w, chip, core):
        self.chip = chip
        self.c_idx = jnp.reshape(core, (1,)).astype(jnp.int32)
        self.chip_idx = jnp.reshape(chip, (1,)).astype(jnp.int32)
        self.shards = {n: w[n].astype(jnp.bfloat16).reshape(2, s[0] // 2, s[1]) for n, s, _ in _MATS}
        self.spec = {n: (s, ax) for n, s, ax in _MATS}
        self.sums = {}

    def _with_own(self, n, stack):
        shape = self.spec[n][0]
        slab = lax.broadcasted_iota(jnp.int32, (N_CHIPS, 1, 1), 0)
        return jnp.where(slab == self.chip, self.shards[n].reshape((1,) + shape), stack.reshape((N_CHIPS,) + shape))

    def cast_and_gather_w_in(self, x):
        x_act, (stack,) = _cast_rows(x, _ACT, "cast_x", _gather_job([self.shards["w_in"]], copy_own=False))
        return x_act, _pack_w_in_stack(self._with_own("w_in", stack))

    def gather_rest(self):
        return _gather_job([self.shards[n] for n in _REST], copy_own=False)

    def rest_weights(self, stacks):
        return {n: _from_stack(self._with_own(n, st), self.spec[n][1]) for n, st in zip(_REST, stacks)}

    def reduce_job(self, grads):
        names, stacks = [], []
        for n, g in grads.items():
            name = "w_in" if n == "wp" else n
            s, ax = self.spec[name]
            st = _unpack_w_in_stack(g) if n == "wp" else _to_stack(g, s, ax)
            names.append(name)
            stacks.append(st.reshape(N_CHIPS, 2, s[0] // 2, s[1]))
        swapped = _swap_halves(stacks, "swap_" + names[0])
        halves = []
        for n, g, r in zip(names, stacks, swapped):
            self.sums[n], bf = _add_own_half(g, r, self.c_idx, "add_own_" + n)
            halves.append(bf)
        return _scatter_job(halves)

    def reduced(self, landed):
        names = list(self.sums)
        reds = [_add_chips(self.sums[n], r, self.chip_idx, self.c_idx, "add_chips_" + n)
                for n, r in zip(names, landed)]
        return {n: g.reshape(self.spec[n][0]) for n, g in zip(names, _join_halves(reds))}


def _step(x, target, w, m, v):
    xi, yi, ci = _position()
    chip = 2 * xi + yi
    ex = _Exchange(w, chip, ci)

    wts = {n: w[n] for n in _REPLICATED}
    x_act, wts["wp"] = ex.cast_and_gather_w_in(x)
    taps = jnp.concatenate([w[n].astype(F32).reshape(-1) for n, _ in _CONVS])
    taps = _allgather_small(_as_rows(taps, _CONV_ROWS, F32), "allgather_taps")
    taps = taps.reshape(N_CHIPS, 2, _CONV_ROWS * LANE)[:, 0]
    at = 0
    for n, s in _CONVS:
        wts[n] = _from_stack(taps[:, at:at + s[0] * s[1]].reshape((N_CHIPS,) + s), 1)
        at += s[0] * s[1]

    d_x, grads, small, landed = _local_step(x, x_act, target, wts, ex)

    outs = {"grad": ex.reduced(landed), "delta": {}, "m": {}, "v": {}}

    small = dict(small, g_ssm_conv_w=grads["ssm_conv_w"], g_ffn_conv_w=grads["ffn_conv_w"])
    all_small = _allgather_small(_pack_small(small), "allgather_small")
    packs = [_pack_small({n: d[n] for n in _REPLICATED}) for d in (w, m, v)]
    g_s, d_s, m_s, v_s = (_unpack_small(a) for a in _adam_small(packs[0], all_small, packs[1], packs[2]))
    for kind, part in (("grad", g_s), ("delta", d_s), ("m", m_s), ("v", v_s)):
        outs[kind].update({n: part[n] for n in _REPLICATED})
    for n, s in _CONVS:
        outs["grad"][n] = lax.dynamic_slice_in_dim(g_s["g_" + n], chip * s[1], s[1], axis=1)
    for n in [n for n, _, _ in _MATS] + [n for n, _ in _CONVS]:
        outs["delta"][n], outs["m"][n], outs["v"][n] = _adam_big(
            w[n].astype(F32), outs["grad"][n], m[n].astype(F32), v[n].astype(F32), "adam_" + n)
    loss = (0.5 / D_MODEL) * jnp.sum(g_s["loss_lanes"])
    return loss, d_x, outs


def kernel(x, rel_bias, w_in, b_gate, ssm_conv_w, ssm_conv_b, ssm_dt_bias, ssm_a_log, ssm_d, ssm_norm_w, attn_sinks, w_branch_ssm, w_branch_attn, w_mix_out, ln1_g, ln1_b, w_up, ffn_conv_w, ffn_conv_b, w_down, ln2_g, ln2_b, loss_target, m_rel_bias, m_w_in, m_b_gate, m_ssm_conv_w, m_ssm_conv_b, m_ssm_dt_bias, m_ssm_a_log, m_ssm_d, m_ssm_norm_w, m_attn_sinks, m_w_branch_ssm, m_w_branch_attn, m_w_mix_out, m_ln1_g, m_ln1_b, m_w_up, m_ffn_conv_w, m_ffn_conv_b, m_w_down, m_ln2_g, m_ln2_b, v_rel_bias, v_w_in, v_b_gate, v_ssm_conv_w, v_ssm_conv_b, v_ssm_dt_bias, v_ssm_a_log, v_ssm_d, v_ssm_norm_w, v_attn_sinks, v_w_branch_ssm, v_w_branch_attn, v_w_mix_out, v_ln1_g, v_ln1_b, v_w_up, v_ffn_conv_w, v_ffn_conv_b, v_w_down, v_ln2_g, v_ln2_b):
    given = dict(locals())
    drop = lambda a, n: a if n == "rel_bias" or a.ndim == 2 else a[0]
    w = {n: drop(given[n], n) for n in _WEIGHTS}
    m = {n: drop(given["m_" + n], n) for n in _WEIGHTS}
    v = {n: drop(given["v_" + n], n) for n in _WEIGHTS}
    loss, d_x, outs = _step(x[0], loss_target[0], w, m, v)
    like = lambda a, n: a.reshape(given[n].shape)
    res = [loss, d_x[None]]
    for kind in ("grad", "delta", "m", "v"):
        res += [like(outs[kind][n], n) for n in _WEIGHTS]
    return tuple(res)
```

```python
import math
from typing import NamedTuple

import numpy as np
import jax
import jax.numpy as jnp
from jax import lax
from jax.experimental import pallas as pl
from jax.experimental.pallas import tpu as pltpu

F32 = jnp.float32
_ACT = jnp.bfloat16
_MXU = jnp.bfloat16

D_MODEL = 1024
D_INNER = 2048
N_HEADS = 32
HEAD_P = 64
N_GROUPS = 4
N_STATE = 128
CHUNK = 128
CONV_DIM = 3072
SSM_K = 4
A_HEADS = 16
A_DH = 64
WIN = 128
REL_BUCKETS = 32
BIAS_ROWS = 64
D_FF = 2816
FFN_K = 3
ALPHA = 2.0 ** 0.25
LN_EPS = 1e-5
RMS_EPS = 1e-5
IN_COLS = 8480
NEG = -1e30

ADAM_LR = 0.001
ADAM_B1 = 0.9
ADAM_B2 = 0.999
ADAM_EPS = 1e-08
ADAM_WD = 0.01
ADAM_STEP = 10

LANE = 128
SUB = 8

P_Z, P_XS, P_G, P_Q, P_BC, P_K, P_V, P_DT = 0, 2048, 4096, 6144, 7168, 8192, 8320, 8448
P_W = 8704
P_MAIN = 8192
T_K, T_V, T_DT = P_K - P_MAIN, P_V - P_MAIN, P_DT - P_MAIN
_PIECES = ((0, 2048, P_Z), (2048, 2048, P_XS), (4096, 1024, P_BC), (5120, 32, P_DT), (5152, 1024, P_Q),
           (6176, 128, P_K), (6304, 128, P_V), (6432, 2048, P_G))

N_CHIPS = 4
N_DEV = 8


def _cp(sem=None, vmem_mb=48):
    return pltpu.CompilerParams(dimension_semantics=sem, vmem_limit_bytes=vmem_mb * 1024 * 1024)


def _pick(n, cands):
    for c in cands:
        if n % c == 0:
            return c
    raise ValueError(f"no block size for {n}")


def _rows8(p):
    k, c = p.shape
    return jnp.concatenate([p.astype(F32), jnp.zeros((SUB - k, c), F32)], axis=0)


class _SideJob(NamedTuple):
    inputs: list
    out_shape: list
    sems: list
    start: object
    finish: object


def _mm(a, b, name, *, trans_a=False, out_dtype=F32, res=None, res_scale=1.0, side=None):
    if trans_a:
        k_dim, m = a.shape
    else:
        m, k_dim = a.shape
    k2, n = b.shape
    assert k_dim == k2, (a.shape, b.shape)
    tm = _pick(m, (1408, 1024, 512, 256, 128))
    tn = _pick(n, (1408, 1024, 512, 256, 128))
    tk = _pick(k_dim, (2816, 2176, 2048, 1024, 512, 256, 128))
    nk = k_dim // tk
    grid = (m // tm, n // tn, nk)
    dn = (((0,), (0,)), ((), ())) if trans_a else (((1,), (0,)), ((), ()))
    n_in = 2 if res is None else 3
    ns_in = len(side.inputs) if side else 0
    ns_out = len(side.out_shape) if side else 0

    def body(*refs):
        a_ref, b_ref = refs[0], refs[1]
        o_ref = refs[n_in + ns_in]
        scratch = refs[n_in + ns_in + 1 + ns_out:]
        job_refs = (refs[n_in:n_in + ns_in], refs[n_in + ns_in + 1:n_in + ns_in + 1 + ns_out],
                    scratch[1:] if nk > 1 else scratch)
        i, j, k = pl.program_id(0), pl.program_id(1), pl.program_id(2)

        def finish(r):
            if res is not None:
                r = r + res_scale * refs[2][...]
            o_ref[...] = r.astype(out_dtype)

        if side:
            @pl.when(jnp.logical_and(jnp.logical_and(i == 0, j == 0), k == 0))
            def _():
                side.start(*job_refs)

        part = lax.dot_general(a_ref[...].astype(_MXU), b_ref[...].astype(_MXU), dn, preferred_element_type=F32)
        if nk == 1:
            finish(part)
        else:
            acc = scratch[0]

            @pl.when(k == 0)
            def _():
                acc[...] = part

            @pl.when(k > 0)
            def _():
                acc[...] += part

            @pl.when(k == nk - 1)
            def _():
                finish(acc[...])

        if side:
            @pl.when(jnp.logical_and(jnp.logical_and(i == grid[0] - 1, j == grid[1] - 1), k == nk - 1))
            def _():
                side.finish(*job_refs)

    if trans_a:
        a_spec = pl.BlockSpec((tk, tm), lambda i, j, k: (k, i))
    else:
        a_spec = pl.BlockSpec((tm, tk), lambda i, j, k: (i, k))
    in_specs = [a_spec, pl.BlockSpec((tk, tn), lambda i, j, k: (k, j))]
    args = [a, b]
    if res is not None:
        in_specs.append(pl.BlockSpec((tm, tn), lambda i, j, k: (i, j)))
        args.append(res)
    out_spec = pl.BlockSpec((tm, tn), lambda i, j, k: (i, j))
    out_shape = jax.ShapeDtypeStruct((m, n), out_dtype)
    scratch_shapes = [pltpu.VMEM((tm, tn), F32)] if nk > 1 else []
    if not side:
        return pl.pallas_call(
            body, name=name, out_shape=out_shape, grid=grid, in_specs=in_specs, out_specs=out_spec,
            scratch_shapes=scratch_shapes, compiler_params=_cp(("parallel", "parallel", "arbitrary")),
        )(*args)
    hbm = pl.BlockSpec(memory_space=pltpu.HBM)
    outs = pl.pallas_call(
        body, name=name, out_shape=[out_shape] + list(side.out_shape), grid=grid,
        in_specs=in_specs + [hbm] * ns_in, out_specs=[out_spec] + [hbm] * ns_out,
        scratch_shapes=scratch_shapes + list(side.sems),
        compiler_params=_cp(("arbitrary", "arbitrary", "arbitrary")),
    )(*args, *side.inputs)
    return outs[0], list(outs[1:])


def _shift_down(cur, prev8, s):
    r = pltpu.roll(cur, s, 0)
    p = pltpu.roll(prev8, s, 0)
    row8 = lax.broadcasted_iota(jnp.int32, (SUB, 1), 0)
    fixed = jnp.where(row8 < s, p, r[0:SUB])
    if cur.shape[0] == SUB:
        return fixed
    return jnp.concatenate([fixed, r[SUB:]], axis=0)


def _shift_up(cur, next8, s):
    tm = cur.shape[0]
    r = pltpu.roll(cur, tm - s, 0)
    p = pltpu.roll(next8, SUB - s, 0)
    row8 = lax.broadcasted_iota(jnp.int32, (SUB, 1), 0)
    fixed = jnp.where(row8 >= SUB - s, p, r[tm - SUB:])
    return jnp.concatenate([r[:tm - SUB], fixed], axis=0)


def _conv_pre(cur, prev8, w_ref, b_row, taps):
    acc = cur * w_ref[taps - 1:taps, :] + b_row
    for s in range(1, taps):
        acc = acc + _shift_down(cur, prev8, s) * w_ref[taps - 1 - s:taps - s, :]
    return acc


def _dot01_r(x, m01, parts=3):
    acc = None
    r = x
    for _ in range(parts):
        hi = r.astype(jnp.bfloat16)
        t = jnp.dot(hi, m01, preferred_element_type=F32)
        acc = t if acc is None else acc + t
        r = r - hi.astype(F32)
    return acc


def _dot01_l(m01, x, parts=3):
    acc = None
    r = x
    for _ in range(parts):
        hi = r.astype(jnp.bfloat16)
        t = jnp.dot(m01, hi, preferred_element_type=F32)
        acc = t if acc is None else acc + t
        r = r - hi.astype(F32)
    return acc


def _dot(a, b):
    return jnp.dot(a.astype(_MXU), b.astype(_MXU), preferred_element_type=F32)


def _dot_nt(a, b):
    return lax.dot_general(a.astype(_MXU), b.astype(_MXU), (((1,), (1,)), ((), ())), preferred_element_type=F32)


def _dot_tn(a, b):
    return lax.dot_general(a.astype(_MXU), b.astype(_MXU), (((0,), (0,)), ((), ())), preferred_element_type=F32)


def _sigmoid(x):
    return 1.0 / (1.0 + jnp.exp(-x))


def _half_masks():
    lane = lax.broadcasted_iota(jnp.int32, (1, LANE), 1)
    lo = (lane < 64).astype(F32)
    return lo, 1.0 - lo


_TC = 1024


def _tm_rows(t):
    return min(512, t)


HALO = 16


def _prev_halo(tm, width, pos):
    def index(*ids):
        i, col = pos(*ids)
        return (jnp.maximum(i * (tm // HALO) - 1, 0), col)
    return pl.BlockSpec((HALO, width), index)


def _next_halo(tm, t, width, pos):
    def index(*ids):
        i, col = pos(*ids)
        return (jnp.minimum((i + 1) * (tm // HALO), t // HALO - 1), col)
    return pl.BlockSpec((HALO, width), index)


def _conv_silu_fwd(proj, colblk0, nblk, w8, b8, name):
    t = proj.shape[0]
    tm = _tm_rows(t)

    def body(c_ref, p_ref, w_ref, b_ref, o_ref, pre_ref):
        i = pl.program_id(1)
        prev8 = jnp.where(i > 0, p_ref[SUB:HALO, :].astype(F32), 0.0)
        pre = _conv_pre(c_ref[...].astype(F32), prev8, w_ref, b_ref[0:1, :], SSM_K)
        o_ref[...] = pre * _sigmoid(pre)
        pre_ref[...] = pre.astype(_ACT)

    blk = pl.BlockSpec((tm, _TC), lambda j, i: (i, j))
    return pl.pallas_call(
        body, name=name,
        out_shape=(jax.ShapeDtypeStruct((t, nblk * _TC), F32), jax.ShapeDtypeStruct((t, nblk * _TC), _ACT)),
        grid=(nblk, t // tm),
        in_specs=[pl.BlockSpec((tm, _TC), lambda j, i: (i, colblk0 + j)),
                  _prev_halo(tm, _TC, lambda j, i: (i, colblk0 + j)),
                  pl.BlockSpec((SUB, _TC), lambda j, i: (0, j)),
                  pl.BlockSpec((SUB, _TC), lambda j, i: (0, j))],
        out_specs=(blk, blk),
        compiler_params=_cp(("parallel", "parallel")),
    )(proj, proj, w8, b8)


def _silu_grad(pre):
    sg = _sigmoid(pre)
    return sg * (1.0 + pre * (1.0 - sg))


def _conv_grads(d, d_next8, cur, w_ref, acc_ref, taps, cols=slice(None)):
    du = d * w_ref[taps - 1:taps, :]
    acc_ref[taps:taps + 1, cols] += jnp.sum(d, axis=0, keepdims=True)
    acc_ref[taps - 1:taps, cols] += jnp.sum(d * cur, axis=0, keepdims=True)
    for s in range(1, taps):
        up = _shift_up(d, d_next8, s)
        du = du + up * w_ref[taps - 1 - s:taps - s, :]
        acc_ref[taps - 1 - s:taps - s, cols] += jnp.sum(up * cur, axis=0, keepdims=True)
    return du


def _conv_silu_bwd(proj, colblk0, nblk, pre, w8, d_out, d_proj, name):
    t = proj.shape[0]
    tm = _tm_rows(t)
    nt = t // tm

    def body(c_ref, pre_ref, pren_ref, w_ref, d_ref, dn_ref, _, du_ref, acc_ref):
        i = pl.program_id(1)

        @pl.when(i == 0)
        def _():
            acc_ref[...] = jnp.zeros_like(acc_ref)

        dpre = d_ref[...].astype(F32) * _silu_grad(pre_ref[...].astype(F32))
        dpre_n = jnp.where(i < nt - 1, dn_ref[0:SUB, :].astype(F32) * _silu_grad(pren_ref[0:SUB, :].astype(F32)), 0.0)
        du_ref[...] = _conv_grads(dpre, dpre_n, c_ref[...].astype(F32), w_ref, acc_ref, SSM_K).astype(_ACT)

    c = nblk * _TC
    blk = pl.BlockSpec((tm, _TC), lambda j, i: (i, j))
    nxt = _next_halo(tm, t, _TC, lambda j, i: (i, j))
    par = pl.BlockSpec((SUB, _TC), lambda j, i: (0, j))
    return pl.pallas_call(
        body, name=name,
        out_shape=(jax.ShapeDtypeStruct(d_proj.shape, d_proj.dtype), jax.ShapeDtypeStruct((SUB, c), F32)),
        grid=(nblk, nt),
        in_specs=[pl.BlockSpec((tm, _TC), lambda j, i: (i, colblk0 + j)), blk, nxt, par, blk, nxt,
                  pl.BlockSpec(memory_space=pl.ANY)],
        out_specs=(pl.BlockSpec((tm, _TC), lambda j, i: (i, colblk0 + j)), par),
        input_output_aliases={6: 0},
        compiler_params=_cp(("parallel", "arbitrary")),
    )(proj, pre, pre, w8, d_out, d_out, d_proj)


def _expand_consts():
    e = np.zeros((LANE, D_INNER), np.float32)
    for h in range(N_HEADS):
        e[h, h * HEAD_P:(h + 1) * HEAD_P] = 1.0
    return jnp.asarray(e, jnp.bfloat16), jnp.asarray(e.T.copy(), jnp.bfloat16)


def _ssd_common(dtr_ref, dtb_ref, alog_ref, e_ref):
    lane = lax.broadcasted_iota(jnp.int32, (1, LANE), 1)
    hm = lane < N_HEADS
    pre = dtr_ref[...] + dtb_ref[0:1, :]
    dt = jnp.where(hm, jnp.maximum(pre, 0.0) + jnp.log(1.0 + jnp.exp(-jnp.abs(pre))), 0.0)
    a_row = jnp.where(hm, -jnp.exp(alog_ref[0:1, :]), 0.0)
    adt = dt * a_row
    r = lax.broadcasted_iota(jnp.int32, (CHUNK, CHUNK), 0)
    c = lax.broadcasted_iota(jnp.int32, (CHUNK, CHUNK), 1)
    causal = r >= c
    acs = _dot01_l(causal.astype(jnp.bfloat16), adt)
    e = e_ref[...]
    acs_x = _dot01_r(acs, e, parts=2)
    dt_x = _dot01_r(dt, e, parts=2)
    return pre, dt, a_row, acs, acs_x, dt_x, causal, hm


def _decay(acs, acs_t, h, causal):
    seg = acs[:, h:h + 1] - acs_t[h:h + 1, :]
    return jnp.exp(jnp.where(causal, seg, NEG))


def _ssd_fwd(xs_c, bc_c, proj, tail, dtb8, alog8, dsk8, nw8, name):
    t = xs_c.shape[0]
    nc = t // CHUNK
    e_bf, _ = _expand_consts()
    gw = D_INNER // N_GROUPS

    def body(xs_ref, bc_ref, dtr_ref, z_ref, dtb_ref, alog_ref, dsk_ref, nw_ref, e_ref,
             y_ref, ys_ref, hp_ref, h_ref):
        c_id = pl.program_id(0)

        @pl.when(c_id == 0)
        def _():
            h_ref[...] = jnp.zeros_like(h_ref)

        _, dt, a_row, acs, acs_x, dt_x, causal, _ = _ssd_common(dtr_ref, dtb_ref, alog_ref, e_ref)
        acs_t = acs.T
        xs = xs_ref[...]
        x_dt = xs * dt_x
        last_x = acs_x[CHUNK - 1:CHUNK, :]
        w_end = jnp.exp(last_x - acs_x)
        e_in = jnp.exp(acs_x)
        d_x = _dot01_r(dsk_ref[...], e_ref[...])[0:1, :]
        hprev = h_ref[...]
        hp_ref[...] = hprev
        lo, hi = _half_masks()
        for g in range(N_GROUPS):
            bg = bc_ref[:, g * N_STATE:(g + 1) * N_STATE]
            cg = bc_ref[:, N_GROUPS * N_STATE + g * N_STATE:N_GROUPS * N_STATE + (g + 1) * N_STATE]
            sl = slice(g * gw, (g + 1) * gw)
            gm = _dot_nt(cg, bg)
            st = _dot(bg.T, x_dt[:, sl] * w_end[:, sl])
            y_off = _dot(cg, hprev[:, sl]) * e_in[:, sl]
            for j in range(gw // LANE):
                h0 = g * (gw // HEAD_P) + 2 * j
                cs = slice(g * gw + j * LANE, g * gw + (j + 1) * LANE)
                xp = x_dt[:, cs]
                m0 = gm * _decay(acs, acs_t, h0, causal)
                m1 = gm * _decay(acs, acs_t, h0 + 1, causal)
                yd = _dot(m0, xp * lo) + _dot(m1, xp * hi)
                y_ref[:, cs] = yd + y_off[:, j * LANE:(j + 1) * LANE] + xs[:, cs] * d_x[:, cs]
            h_ref[:, sl] = hprev[:, sl] * jnp.exp(last_x[:, sl]) + st
        y = y_ref[...]
        z = z_ref[...].astype(F32)
        y2 = y * (z * _sigmoid(z))
        for g in range(N_GROUPS):
            sl = slice(g * gw, (g + 1) * gw)
            yg = y2[:, sl]
            rinv = lax.rsqrt(jnp.mean(yg * yg, axis=-1, keepdims=True) + RMS_EPS)
            ys_ref[:, sl] = (yg * rinv * nw_ref[0:1, sl]).astype(_ACT)

    small = pl.BlockSpec((SUB, LANE), lambda c: (0, 0))
    return pl.pallas_call(
        body, name=name,
        out_shape=(jax.ShapeDtypeStruct((t, D_INNER), F32), jax.ShapeDtypeStruct((t, D_INNER), _ACT),
                   jax.ShapeDtypeStruct((t, D_INNER), F32)),
        grid=(nc,),
        in_specs=[pl.BlockSpec((CHUNK, D_INNER), lambda c: (c, 0)),
                  pl.BlockSpec((CHUNK, 1024), lambda c: (c, 0)),
                  pl.BlockSpec((CHUNK, LANE), lambda c: (c, T_DT // LANE)),
                  pl.BlockSpec((CHUNK, D_INNER), lambda c: (c, P_Z // D_INNER)),
                  small, small, small,
                  pl.BlockSpec((SUB, D_INNER), lambda c: (0, 0)),
                  pl.BlockSpec((LANE, D_INNER), lambda c: (0, 0))],
        out_specs=(pl.BlockSpec((CHUNK, D_INNER), lambda c: (c, 0)),
                   pl.BlockSpec((CHUNK, D_INNER), lambda c: (c, 0)),
                   pl.BlockSpec((N_STATE, D_INNER), lambda c: (c, 0))),
        scratch_shapes=[pltpu.VMEM((N_STATE, D_INNER), F32)],
        compiler_params=_cp(("arbitrary",)),
    )(xs_c, bc_c, tail, proj, dtb8, alog8, dsk8, nw8, e_bf)


def _ssd_bwd(d_ys, y, xs_c, bc_c, proj, tail, hprev_all, dtb8, alog8, dsk8, nw8, d_proj, name):
    t = xs_c.shape[0]
    nc = t // CHUNK
    e_bf, et_bf = _expand_consts()
    gw = D_INNER // N_GROUPS

    def body(dys_ref, y_ref, xs_ref, bc_ref, dtr_ref, z_ref, hp_ref, dtb_ref, alog_ref, dsk_ref, nw_ref,
             e_ref, et_ref, _, dxs_ref, dbc_ref, dz_ref, ddt_ref, acc_ref, dnw_ref, dh_ref, dx_ref):
        step = pl.program_id(0)

        @pl.when(step == 0)
        def _():
            dh_ref[...] = jnp.zeros_like(dh_ref)
            acc_ref[...] = jnp.zeros_like(acc_ref)
            dnw_ref[...] = jnp.zeros_like(dnw_ref)

        pre, dt, a_row, acs, acs_x, dt_x, causal, hm = _ssd_common(dtr_ref, dtb_ref, alog_ref, e_ref)
        acs_t = acs.T
        et = et_ref[...]
        xs = xs_ref[...]
        x_dt = xs * dt_x
        last_x = acs_x[CHUNK - 1:CHUNK, :]
        w_end = jnp.exp(last_x - acs_x)
        e_in = jnp.exp(acs_x)
        e_last = jnp.exp(last_x)
        d_x = _dot01_r(dsk_ref[...], e_ref[...])[0:1, :]

        y = y_ref[...]
        z = z_ref[...].astype(F32)
        sz = _sigmoid(z)
        gz = z * sz
        y2 = y * gz
        dys = dys_ref[...].astype(F32)
        for g in range(N_GROUPS):
            sl = slice(g * gw, (g + 1) * gw)
            yg = y2[:, sl]
            rinv = lax.rsqrt(jnp.mean(yg * yg, axis=-1, keepdims=True) + RMS_EPS)
            nrm = yg * rinv
            dn = dys[:, sl] * nw_ref[0:1, sl]
            dnw_ref[0:1, sl] += jnp.sum(dys[:, sl] * nrm, axis=0, keepdims=True)
            dx_ref[:, sl] = rinv * (dn - nrm * jnp.mean(dn * nrm, axis=-1, keepdims=True))
        dy2 = dx_ref[...]
        dy = dy2 * gz
        dz_ref[...] = (dy2 * y * (sz * (1.0 + z * (1.0 - sz)))).astype(_ACT)

        dh_next = dh_ref[...]
        hprev = hp_ref[...]
        lo, hi = _half_masks()
        r = lax.broadcasted_iota(jnp.int32, (CHUNK, CHUNK), 0)
        c = lax.broadcasted_iota(jnp.int32, (CHUNK, CHUNK), 1)
        from_here = (c >= r).astype(jnp.bfloat16)
        before = c < r
        lane = lax.broadcasted_iota(jnp.int32, (1, LANE), 1)
        da_intra = jnp.zeros((CHUNK, LANE), F32)
        v_seg = jnp.zeros((CHUNK, LANE), F32)
        z_seg = jnp.zeros((CHUNK, LANE), F32)
        tail_parts = []
        for g in range(N_GROUPS):
            bg = bc_ref[:, g * N_STATE:(g + 1) * N_STATE]
            cg = bc_ref[:, N_GROUPS * N_STATE + g * N_STATE:N_GROUPS * N_STATE + (g + 1) * N_STATE]
            sl = slice(g * gw, (g + 1) * gw)
            et_g = et_ref[g * gw:(g + 1) * gw, :]
            gm = _dot_nt(cg, bg)
            dzg = e_in[:, sl] * dy[:, sl]
            dcg = _dot_nt(dzg, hprev[:, sl])
            dh_c = _dot(cg.T, dzg)
            q = _dot(bg, dh_next[:, sl])
            dbg = _dot_nt(x_dt[:, sl] * w_end[:, sl], dh_next[:, sl])
            y_off = _dot(cg, hprev[:, sl]) * e_in[:, sl]
            v_seg = v_seg + _dot01_r(dy[:, sl] * y_off, et_g, parts=1)
            z_seg = z_seg + _dot01_r(w_end[:, sl] * q * x_dt[:, sl], et_g, parts=1)
            dgm = jnp.zeros((CHUNK, CHUNK), F32)
            for j in range(gw // LANE):
                h0 = g * (gw // HEAD_P) + 2 * j
                cs = slice(g * gw + j * LANE, g * gw + (j + 1) * LANE)
                xp = x_dt[:, cs]
                dyp = dy[:, cs]
                dxd = jnp.zeros((CHUNK, LANE), F32)
                for half, msk in ((0, lo), (1, hi)):
                    lam = _decay(acs, acs_t, h0 + half, causal)
                    mm = gm * lam
                    dym = dyp * msk
                    dmm = _dot_nt(dym, xp)
                    dxd = dxd + _dot_tn(mm, dym)
                    dgm = dgm + dmm * lam
                    below = _dot(from_here, dmm * mm)
                    col = jnp.sum(jnp.where(before, below, 0.0), axis=-1, keepdims=True)
                    da_intra = da_intra + jnp.where(lane == h0 + half, col, 0.0)
                dx_ref[:, cs] = dxd + w_end[:, cs] * q[:, j * LANE:(j + 1) * LANE]
            dbc_ref[:, N_GROUPS * N_STATE + g * N_STATE:N_GROUPS * N_STATE + (g + 1) * N_STATE] = dcg + _dot(dgm, bg)
            dbc_ref[:, g * N_STATE:(g + 1) * N_STATE] = dbg + _dot_tn(dgm, cg)
            dh_ref[:, sl] = e_last[:, sl] * dh_next[:, sl] + dh_c
            tail_parts.append(e_last[:, sl] * jnp.sum(dh_next[:, sl] * hprev[:, sl], axis=0, keepdims=True))
        dxt = dx_ref[...]

        u_seg = _dot01_r(xs * dxt, et, parts=1)
        q_full = jnp.concatenate(tail_parts, axis=1)
        t_row = _dot01_r(jnp.broadcast_to(q_full, (SUB, D_INNER)), et)[0:1, :]
        d_alpha = (da_intra + _dot01_l(from_here, v_seg) + _dot01_l(before.astype(jnp.bfloat16), z_seg) + t_row)
        d_dt = a_row * d_alpha + u_seg
        sgp = _sigmoid(pre)
        d_raw = jnp.where(hm, d_dt * sgp, 0.0)
        ddt_ref[...] = d_raw.astype(_ACT)
        acc_ref[0:1, :] += jnp.sum(d_raw, axis=0, keepdims=True)
        acc_ref[1:2, :] += jnp.sum(d_alpha * dt, axis=0, keepdims=True) * a_row
        dd_row = jnp.sum(dy * xs, axis=0, keepdims=True)
        acc_ref[2:3, :] += _dot01_r(jnp.broadcast_to(dd_row, (SUB, D_INNER)), et)[0:1, :]
        dxs_ref[...] = dy * d_x + dxt * dt_x

    rev = lambda c: (nc - 1 - c, 0)
    small = pl.BlockSpec((SUB, LANE), lambda c: (0, 0))
    return pl.pallas_call(
        body, name=name,
        out_shape=(jax.ShapeDtypeStruct((t, D_INNER), F32), jax.ShapeDtypeStruct((t, 1024), F32),
                   jax.ShapeDtypeStruct(d_proj.shape, d_proj.dtype), jax.ShapeDtypeStruct((t, LANE), _ACT),
                   jax.ShapeDtypeStruct((SUB, LANE), F32), jax.ShapeDtypeStruct((SUB, D_INNER), F32)),
        grid=(nc,),
        in_specs=[pl.BlockSpec((CHUNK, D_INNER), rev),
                  pl.BlockSpec((CHUNK, D_INNER), rev),
                  pl.BlockSpec((CHUNK, D_INNER), rev),
                  pl.BlockSpec((CHUNK, 1024), rev),
                  pl.BlockSpec((CHUNK, LANE), lambda c: (nc - 1 - c, T_DT // LANE)),
                  pl.BlockSpec((CHUNK, D_INNER), lambda c: (nc - 1 - c, P_Z // D_INNER)),
                  pl.BlockSpec((N_STATE, D_INNER), rev),
                  small, small, small,
                  pl.BlockSpec((SUB, D_INNER), lambda c: (0, 0)),
                  pl.BlockSpec((LANE, D_INNER), lambda c: (0, 0)),
                  pl.BlockSpec((D_INNER, LANE), lambda c: (0, 0)),
                  pl.BlockSpec(memory_space=pl.ANY)],
        out_specs=(pl.BlockSpec((CHUNK, D_INNER), rev),
                   pl.BlockSpec((CHUNK, 1024), rev),
                   pl.BlockSpec((CHUNK, D_INNER), lambda c: (nc - 1 - c, P_Z // D_INNER)),
                   pl.BlockSpec((CHUNK, LANE), rev),
                   small,
                   pl.BlockSpec((SUB, D_INNER), lambda c: (0, 0))),
        input_output_aliases={13: 2},
        scratch_shapes=[pltpu.VMEM((N_STATE, D_INNER), F32), pltpu.VMEM((CHUNK, D_INNER), F32)],
        compiler_params=_cp(("arbitrary",), vmem_mb=56),
    )(d_ys, y, xs_c, bc_c, tail, proj, hprev_all, dtb8, alog8, dsk8, nw8, e_bf, et_bf, d_proj)


def _rel_tables():
    qi = np.arange(WIN)[:, None] + WIN
    kj = np.arange(2 * WIN)[None, :]
    rel = qi - kj
    n = np.maximum(rel, 0)
    max_exact = REL_BUCKETS // 2
    nf = np.maximum(n, 1).astype(np.float32)
    large = max_exact + (np.log(nf / np.float32(max_exact)) / np.float32(math.log(WIN / max_exact))
                         * np.float32(REL_BUCKETS - max_exact)).astype(np.int32)
    large = np.minimum(large, REL_BUCKETS - 1)
    bucket = np.where(n < max_exact, n, large)
    valid = (rel >= 0) & (rel < WIN)
    sink_col = np.broadcast_to(kj == 0, rel.shape)
    onehot = np.zeros((BIAS_ROWS, WIN * 2 * WIN), np.float32)
    flat_b = np.where(sink_col, REL_BUCKETS, bucket).reshape(-1)
    flat_v = (valid | sink_col).reshape(-1)
    first_v = ((valid & (kj >= WIN)) | sink_col).reshape(-1)
    idx = np.arange(WIN * 2 * WIN)
    onehot[flat_b[flat_v], idx[flat_v]] = 1.0
    return onehot, np.stack([first_v, flat_v]).astype(np.float32)


def _bias_expand(table_t, name):
    onehot, valid = _rel_tables()

    def body(rb_ref, oh_ref, v_ref, o_ref):
        full = _dot01_r(rb_ref[...], oh_ref[...])
        o_ref[0] = jnp.where(v_ref[0:1, :] > 0.5, full, NEG)
        o_ref[1] = jnp.where(v_ref[1:2, :] > 0.5, full, NEG)

    return pl.pallas_call(
        body, name=name, out_shape=jax.ShapeDtypeStruct((2, A_HEADS, WIN * 2 * WIN), F32),
        compiler_params=_cp(None),
    )(table_t, jnp.asarray(onehot, jnp.bfloat16), jnp.asarray(valid, F32))


def _bias_reduce(dbias, name):
    onehot, _ = _rel_tables()

    def body(d_ref, oh_ref, o_ref):
        acc = None
        r = d_ref[...]
        for _ in range(3):
            hi = r.astype(jnp.bfloat16)
            tt = lax.dot_general(hi, oh_ref[...], (((1,), (1,)), ((), ())), preferred_element_type=F32)
            acc = tt if acc is None else acc + tt
            r = r - hi.astype(F32)
        o_ref[...] = acc

    return pl.pallas_call(
        body, name=name, out_shape=jax.ShapeDtypeStruct((A_HEADS, BIAS_ROWS), F32),
        compiler_params=_cp(None),
    )(dbias, jnp.asarray(onehot, jnp.bfloat16))


def _attn_bands(kc_ref, kp_ref, vc_ref, vp_ref, has_prev):
    lo, hi = _half_masks()
    row = lax.broadcasted_iota(jnp.int32, (2 * WIN, 1), 0)
    keep = (row > 0).astype(F32)
    kb = jnp.concatenate([jnp.where(has_prev, kp_ref[...], 0.0), kc_ref[...]], axis=0) * (keep * (A_DH ** -0.5))
    vb = jnp.concatenate([jnp.where(has_prev, vp_ref[...], 0.0), vc_ref[...]], axis=0) * keep
    kr = pltpu.roll(kb, 64, 1)
    vr = pltpu.roll(vb, 64, 1)
    kk = ((kb * lo, kr * hi), (kr * lo, kb * hi))
    vv = ((vb * lo, vr * hi), (vr * lo, vb * hi))
    return kk, vv, (hi, lo)


def _attn_logits(q_ref, kk, lg_ref):
    for h in range(A_HEADS):
        j, half, kv = h // 2, h % 2, h // (A_HEADS // 2)
        lg_ref[h] = _dot_nt(q_ref[:, j * LANE:(j + 1) * LANE], kk[kv][half])


def _attn_fwd(proj, tail, bias, name):
    t = proj.shape[0]
    nb = t // WIN

    def body(q_ref, kc_ref, kp_ref, vc_ref, vp_ref, b_ref, o_ref, lg_ref, p_ref):
        n = pl.program_id(0)
        kk, vv, ones = _attn_bands(kc_ref, kp_ref, vc_ref, vp_ref, n > 0)
        _attn_logits(q_ref, kk, lg_ref)
        for h in range(A_HEADS):
            logits = lg_ref[h] + b_ref[h]
            p_ref[h] = jnp.exp(logits - jnp.max(logits, axis=-1, keepdims=True)).astype(_MXU)
        lane = lax.broadcasted_iota(jnp.int32, (1, LANE), 1)
        for j in range(A_HEADS // 2):
            kv = (2 * j) // (A_HEADS // 2)
            outs = []
            for half in range(2):
                o = jnp.dot(p_ref[2 * j + half], (vv[kv][half] + ones[half]).astype(_MXU), preferred_element_type=F32)
                outs.append(o / pltpu.roll(o, 64, 1))
            o_ref[:, j * LANE:(j + 1) * LANE] = jnp.where(lane < 64, outs[0], outs[1]).astype(_ACT)

    kvspec = lambda col, prev: pl.BlockSpec(
        (WIN, LANE), (lambda n: (jnp.maximum(n - 1, 0), col)) if prev else (lambda n: (n, col)))
    return pl.pallas_call(
        body, name=name, out_shape=jax.ShapeDtypeStruct((t, D_MODEL), _ACT),
        grid=(nb,),
        in_specs=[pl.BlockSpec((WIN, 1024), lambda n: (n, P_Q // 1024)),
                  kvspec(T_K // LANE, False), kvspec(T_K // LANE, True),
                  kvspec(T_V // LANE, False), kvspec(T_V // LANE, True),
                  pl.BlockSpec((None, A_HEADS, WIN, 2 * WIN), lambda n: (jnp.minimum(n, 1), 0, 0, 0))],
        out_specs=pl.BlockSpec((WIN, 1024), lambda n: (n, 0)),
        scratch_shapes=[pltpu.VMEM((A_HEADS, WIN, 2 * WIN), F32), pltpu.VMEM((A_HEADS, WIN, 2 * WIN), _MXU)],
        compiler_params=_cp(("parallel",)),
    )(proj, tail, tail, tail, tail, bias)


def _attn_bwd(proj, tail, bias, y_attn, d_out, d_proj, name):
    t = proj.shape[0]
    nb = t // WIN

    def body(q_ref, kc_ref, kp_ref, vc_ref, vp_ref, b_ref, y_ref, do_ref, _,
             dq_ref, dk_ref, dv_ref, db_ref, ck_ref, cv_ref, lg_ref, dl_ref, p_ref):
        n = pl.program_id(0)

        @pl.when(n == 0)
        def _():
            db_ref[...] = jnp.zeros_like(db_ref)
            ck_ref[...] = jnp.zeros_like(ck_ref)
            cv_ref[...] = jnp.zeros_like(cv_ref)

        @pl.when(n < nb)
        def _():
            kk, vv, _ = _attn_bands(kc_ref, kp_ref, vc_ref, vp_ref, n > 0)
            lo, hi = _half_masks()
            ones_k = jnp.ones((2 * WIN, LANE), jnp.bfloat16)
            ones_d = jnp.ones((LANE, LANE), jnp.bfloat16)
            _attn_logits(q_ref, kk, lg_ref)
            for h in range(A_HEADS):
                j, half, kv = h // 2, h % 2, h // (A_HEADS // 2)
                msk = hi if half else lo
                logits = lg_ref[h] + b_ref[h]
                p = jnp.exp(logits - jnp.max(logits, axis=-1, keepdims=True))
                den = jnp.dot(p.astype(_MXU), ones_k.astype(_MXU), preferred_element_type=F32)
                dop = do_ref[:, j * LANE:(j + 1) * LANE].astype(F32)
                delta = _dot01_r(dop * y_ref[:, j * LANE:(j + 1) * LANE].astype(F32) * msk, ones_d, parts=2)
                inv = 1.0 / den
                probs = p * jnp.concatenate([inv, inv], axis=1)
                dprobs = _dot_nt(dop, vv[kv][half])
                dlog = probs * (dprobs - jnp.concatenate([delta, delta], axis=1))
                db_ref[h] += dlog
                dl_ref[h] = dlog.astype(_MXU)
                p_ref[h] = probs.astype(_MXU)
            dk_t = [[None, None], [None, None]]
            dv_t = [[None, None], [None, None]]
            for j in range(A_HEADS // 2):
                kv = (2 * j) // (A_HEADS // 2)
                qs = q_ref[:, j * LANE:(j + 1) * LANE].astype(F32) * (A_DH ** -0.5)
                dop = do_ref[:, j * LANE:(j + 1) * LANE].astype(F32)
                dq = None
                for half, msk in ((0, lo), (1, hi)):
                    h = 2 * j + half
                    dqh = jnp.dot(dl_ref[h], kk[kv][half].astype(_MXU), preferred_element_type=F32)
                    dq = dqh if dq is None else dq + dqh
                    dkh = lax.dot_general((qs * msk).astype(_MXU), dl_ref[h], (((0,), (0,)), ((), ())),
                                          preferred_element_type=F32)
                    dvh = lax.dot_general((dop * msk).astype(_MXU), p_ref[h], (((0,), (0,)), ((), ())),
                                          preferred_element_type=F32)
                    dk_t[kv][half] = dkh if dk_t[kv][half] is None else dk_t[kv][half] + dkh
                    dv_t[kv][half] = dvh if dv_t[kv][half] is None else dv_t[kv][half] + dvh
                dq_ref[:, j * LANE:(j + 1) * LANE] = dq.astype(_ACT)
            row = lax.broadcasted_iota(jnp.int32, (2 * WIN, 1), 0)

            def band(acc):
                a = (acc[0][0] + pltpu.roll(acc[0][1], 64, 0)) + (pltpu.roll(acc[1][0], 64, 0) + acc[1][1])
                return jnp.where(row > 0, a.T, 0.0)

            dkb = band(dk_t)
            dvb = band(dv_t)
            dk_ref[...] = (ck_ref[...] + dkb[0:WIN]).astype(_ACT)
            dv_ref[...] = (cv_ref[...] + dvb[0:WIN]).astype(_ACT)
            ck_ref[...] = dkb[WIN:]
            cv_ref[...] = dvb[WIN:]

        @pl.when(n == nb)
        def _():
            dk_ref[...] = ck_ref[...].astype(_ACT)
            dv_ref[...] = cv_ref[...].astype(_ACT)

    cur = lambda n: jnp.minimum(n, nb - 1)
    prv = lambda n: jnp.maximum(jnp.minimum(n, nb - 1) - 1, 0)
    kvspec = lambda col, prev: pl.BlockSpec(
        (WIN, LANE), (lambda n: (prv(n), col)) if prev else (lambda n: (cur(n), col)))
    band_shape = (A_HEADS, WIN, 2 * WIN)
    return pl.pallas_call(
        body, name=name,
        out_shape=(jax.ShapeDtypeStruct(d_proj.shape, d_proj.dtype), jax.ShapeDtypeStruct((t, LANE), _ACT),
                   jax.ShapeDtypeStruct((t, LANE), _ACT), jax.ShapeDtypeStruct(band_shape, F32)),
        grid=(nb + 1,),
        in_specs=[pl.BlockSpec((WIN, 1024), lambda n: (cur(n), P_Q // 1024)),
                  kvspec(T_K // LANE, False), kvspec(T_K // LANE, True),
                  kvspec(T_V // LANE, False), kvspec(T_V // LANE, True),
                  pl.BlockSpec((None,) + band_shape, lambda n: (jnp.minimum(n, 1), 0, 0, 0)),
                  pl.BlockSpec((WIN, 1024), lambda n: (cur(n), 0)),
                  pl.BlockSpec((WIN, 1024), lambda n: (cur(n), 0)),
                  pl.BlockSpec(memory_space=pl.ANY)],
        out_specs=(pl.BlockSpec((WIN, 1024), lambda n: (cur(n), P_Q // 1024)),
                   pl.BlockSpec((WIN, LANE), lambda n: (jnp.maximum(n - 1, 0), 0)),
                   pl.BlockSpec((WIN, LANE), lambda n: (jnp.maximum(n - 1, 0), 0)),
                   pl.BlockSpec(band_shape, lambda n: (0, 0, 0))),
        input_output_aliases={8: 0},
        scratch_shapes=[pltpu.VMEM((WIN, LANE), F32), pltpu.VMEM((WIN, LANE), F32),
                        pltpu.VMEM(band_shape, F32), pltpu.VMEM(band_shape, _MXU), pltpu.VMEM(band_shape, _MXU)],
        compiler_params=_cp(("arbitrary",)),
    )(proj, tail, tail, tail, tail, bias, y_attn, d_out, d_proj)


def _merge_fwd(bs, ba, proj, bg8, name):
    t = bs.shape[0]
    tm = _tm_rows(t)

    def body(bs_ref, ba_ref, gs_ref, ga_ref, bgs_ref, bga_ref, o_ref):
        g_s = _sigmoid(gs_ref[...] + bgs_ref[0:1, :])
        g_a = _sigmoid(ga_ref[...] + bga_ref[0:1, :])
        o_ref[...] = (g_s * bs_ref[...] + g_a * ba_ref[...]).astype(_ACT)

    row = lambda col: pl.BlockSpec((tm, 1024), lambda i: (i, col))
    return pl.pallas_call(
        body, name=name, out_shape=jax.ShapeDtypeStruct((t, D_MODEL), _ACT), grid=(t // tm,),
        in_specs=[row(0), row(0), row(P_G // 1024), row(P_G // 1024 + 1),
                  pl.BlockSpec((SUB, 1024), lambda i: (0, 0)), pl.BlockSpec((SUB, 1024), lambda i: (0, 1))],
        out_specs=row(0), compiler_params=_cp(("parallel",)),
    )(bs, ba, proj, proj, bg8, bg8)


def _merge_bwd(d_merged, bs, ba, proj, bg8, name):
    t = bs.shape[0]
    tm = _tm_rows(t)

    def body(dm_ref, bs_ref, ba_ref, gs_ref, ga_ref, bgs_ref, bga_ref, dbs_ref, dba_ref, dg_ref, acc_ref):
        @pl.when(pl.program_id(0) == 0)
        def _():
            acc_ref[...] = jnp.zeros_like(acc_ref)

        dm = dm_ref[...].astype(F32)
        g_s = _sigmoid(gs_ref[...] + bgs_ref[0:1, :])
        g_a = _sigmoid(ga_ref[...] + bga_ref[0:1, :])
        dbs_ref[...] = (dm * g_s).astype(_ACT)
        dba_ref[...] = (dm * g_a).astype(_ACT)
        dgs = dm * bs_ref[...].astype(F32) * g_s * (1.0 - g_s)
        dga = dm * ba_ref[...].astype(F32) * g_a * (1.0 - g_a)
        dg_ref[:, 0:1024] = dgs.astype(_ACT)
        dg_ref[:, 1024:2048] = dga.astype(_ACT)
        acc_ref[0:1, 0:1024] += jnp.sum(dgs, axis=0, keepdims=True)
        acc_ref[0:1, 1024:2048] += jnp.sum(dga, axis=0, keepdims=True)

    row = lambda col: pl.BlockSpec((tm, 1024), lambda i: (i, col))
    return pl.pallas_call(
        body, name=name,
        out_shape=(jax.ShapeDtypeStruct((t, D_MODEL), _ACT), jax.ShapeDtypeStruct((t, D_MODEL), _ACT),
                   jax.ShapeDtypeStruct((t, P_W), _ACT), jax.ShapeDtypeStruct((SUB, 2048), F32)),
        grid=(t // tm,),
        in_specs=[row(0), row(0), row(0), row(P_G // 1024), row(P_G // 1024 + 1),
                  pl.BlockSpec((SUB, 1024), lambda i: (0, 0)), pl.BlockSpec((SUB, 1024), lambda i: (0, 1))],
        out_specs=(row(0), row(0), pl.BlockSpec((tm, 2048), lambda i: (i, P_G // 2048)),
                   pl.BlockSpec((SUB, 2048), lambda i: (0, 0))),
        compiler_params=_cp(("arbitrary",)),
    )(d_merged, bs, ba, proj, proj, bg8, bg8)


def _place_tail(d_k, d_v, d_dt, d_proj, name):
    t = d_k.shape[0]
    tm = _tm_rows(t)
    width = P_W - P_MAIN

    def body(k_ref, v_ref, dt_ref, _, o_ref):
        o_ref[:, T_K:T_K + LANE] = k_ref[...]
        o_ref[:, T_V:T_V + LANE] = v_ref[...]
        o_ref[:, T_DT:T_DT + LANE] = dt_ref[...]
        o_ref[:, T_DT + LANE:width] = jnp.zeros((tm, width - T_DT - LANE), o_ref.dtype)

    blk = pl.BlockSpec((tm, LANE), lambda i: (i, 0))
    return pl.pallas_call(
        body, name=name, out_shape=jax.ShapeDtypeStruct(d_proj.shape, d_proj.dtype), grid=(t // tm,),
        in_specs=[blk, blk, blk, pl.BlockSpec(memory_space=pl.ANY)],
        out_specs=pl.BlockSpec((tm, width), lambda i: (i, P_MAIN // width)),
        input_output_aliases={3: 0}, compiler_params=_cp(("parallel",)),
    )(d_k, d_v, d_dt, d_proj)


def _ln_stats(r):
    mu = jnp.mean(r, axis=-1, keepdims=True)
    xc = r - mu
    var = jnp.mean(xc * xc, axis=-1, keepdims=True)
    rstd = lax.rsqrt(var + LN_EPS)
    return xc * rstd, rstd


def _ln_bwd(dxhat, xhat, rstd):
    return rstd * (dxhat - jnp.mean(dxhat, axis=-1, keepdims=True)
                   - xhat * jnp.mean(dxhat * xhat, axis=-1, keepdims=True))


def _ln1_fwd(x, mix, g8, b8, name):
    t = x.shape[0]
    tm = _tm_rows(t)

    def body(x_ref, m_ref, g_ref, b_ref, xh_ref, h_ref, rs_ref):
        xhat, rstd = _ln_stats(ALPHA * x_ref[...] + m_ref[...])
        xh_ref[...] = xhat
        h_ref[...] = (xhat * g_ref[0:1, :] + b_ref[0:1, :]).astype(_ACT)
        rs_ref[...] = rstd

    row = pl.BlockSpec((tm, D_MODEL), lambda i: (i, 0))
    par = pl.BlockSpec((SUB, D_MODEL), lambda i: (0, 0))
    return pl.pallas_call(
        body, name=name,
        out_shape=(jax.ShapeDtypeStruct((t, D_MODEL), F32), jax.ShapeDtypeStruct((t, D_MODEL), _ACT),
                   jax.ShapeDtypeStruct((t, 1), F32)),
        grid=(t // tm,), in_specs=[row, row, par, par],
        out_specs=(row, row, pl.BlockSpec((tm, 1), lambda i: (i, 0))),
        compiler_params=_cp(("parallel",)),
    )(x, mix, g8, b8)


def _ln2_loss(xhat1, ffn, target, g1_8, b1_8, g2_8, b2_8, name):
    t = xhat1.shape[0]
    tm = _tm_rows(t)

    def body(xh_ref, f_ref, t_ref, g1_ref, b1_ref, g2_ref, b2_ref, d_ref, db_ref, acc_ref):
        @pl.when(pl.program_id(0) == 0)
        def _():
            acc_ref[...] = jnp.zeros_like(acc_ref)

        h1 = xh_ref[...] * g1_ref[0:1, :] + b1_ref[0:1, :]
        xhat, rstd = _ln_stats(ALPHA * h1 + f_ref[...])
        diff = xhat * g2_ref[0:1, :] + b2_ref[0:1, :] - t_ref[...]
        dy = diff * (1.0 / D_MODEL)
        acc_ref[0:1, :] += jnp.sum(dy * xhat, axis=0, keepdims=True)
        acc_ref[1:2, :] += jnp.sum(dy, axis=0, keepdims=True)
        acc_ref[2:3, :] += jnp.sum(diff * diff, axis=0, keepdims=True)
        d = _ln_bwd(dy * g2_ref[0:1, :], xhat, rstd)
        d_ref[...] = d
        db_ref[...] = d.astype(_ACT)

    row = pl.BlockSpec((tm, D_MODEL), lambda i: (i, 0))
    par = pl.BlockSpec((SUB, D_MODEL), lambda i: (0, 0))
    return pl.pallas_call(
        body, name=name,
        out_shape=(jax.ShapeDtypeStruct((t, D_MODEL), F32), jax.ShapeDtypeStruct((t, D_MODEL), _ACT),
                   jax.ShapeDtypeStruct((SUB, D_MODEL), F32)),
        grid=(t // tm,), in_specs=[row, row, row, par, par, par, par],
        out_specs=(row, row, par), compiler_params=_cp(("arbitrary",)),
    )(xhat1, ffn, target, g1_8, b1_8, g2_8, b2_8)


def _ln1_bwd(d_r2, d_h1_ffn, xhat1, rstd1, g1_8, name):
    t = xhat1.shape[0]
    tm = _tm_rows(t)

    def body(d2_ref, df_ref, xh_ref, rs_ref, g_ref, d_ref, db_ref, acc_ref):
        @pl.when(pl.program_id(0) == 0)
        def _():
            acc_ref[...] = jnp.zeros_like(acc_ref)

        dh = ALPHA * d2_ref[...] + df_ref[...]
        xhat = xh_ref[...]
        acc_ref[0:1, :] += jnp.sum(dh * xhat, axis=0, keepdims=True)
        acc_ref[1:2, :] += jnp.sum(dh, axis=0, keepdims=True)
        d = _ln_bwd(dh * g_ref[0:1, :], xhat, rs_ref[...])
        d_ref[...] = d
        db_ref[...] = d.astype(_ACT)

    row = pl.BlockSpec((tm, D_MODEL), lambda i: (i, 0))
    par = pl.BlockSpec((SUB, D_MODEL), lambda i: (0, 0))
    return pl.pallas_call(
        body, name=name,
        out_shape=(jax.ShapeDtypeStruct((t, D_MODEL), F32), jax.ShapeDtypeStruct((t, D_MODEL), _ACT),
                   jax.ShapeDtypeStruct((SUB, D_MODEL), F32)),
        grid=(t // tm,), in_specs=[row, row, row, pl.BlockSpec((tm, 1), lambda i: (i, 0)), par],
        out_specs=(row, row, par), compiler_params=_cp(("arbitrary",)),
    )(d_r2, d_h1_ffn, xhat1, rstd1, g1_8)


def _ffn_tm(t):
    return min(256, t)


def _ffn_act_fwd(u0, cw8, cb8, name):
    t = u0.shape[0]
    tm = _ffn_tm(t)

    def body(g_ref, gp_ref, v_ref, vp_ref, wg_ref, wv_ref, bg_ref, bv_ref, o_ref, u_ref):
        i = pl.program_id(0)
        gprev = jnp.where(i > 0, gp_ref[SUB:HALO, :].astype(F32), 0.0)
        vprev = jnp.where(i > 0, vp_ref[SUB:HALO, :].astype(F32), 0.0)
        gate = _conv_pre(g_ref[...].astype(F32), gprev, wg_ref, bg_ref[0:1, :], FFN_K)
        val = _conv_pre(v_ref[...].astype(F32), vprev, wv_ref, bv_ref[0:1, :], FFN_K)
        o_ref[...] = (gate * _sigmoid(gate) * val).astype(_ACT)
        u_ref[:, 0:D_FF] = gate.astype(_ACT)
        u_ref[:, D_FF:2 * D_FF] = val.astype(_ACT)

    cur = lambda col: pl.BlockSpec((tm, D_FF), lambda i: (i, col))
    prv = lambda col: _prev_halo(tm, D_FF, lambda i: (i, col))
    par = lambda col: pl.BlockSpec((SUB, D_FF), lambda i: (0, col))
    return pl.pallas_call(
        body, name=name,
        out_shape=(jax.ShapeDtypeStruct((t, D_FF), _ACT), jax.ShapeDtypeStruct((t, 2 * D_FF), _ACT)),
        grid=(t // tm,),
        in_specs=[cur(0), prv(0), cur(1), prv(1), par(0), par(1), par(0), par(1)],
        out_specs=(pl.BlockSpec((tm, D_FF), lambda i: (i, 0)), pl.BlockSpec((tm, 2 * D_FF), lambda i: (i, 0))),
        compiler_params=_cp(("parallel",)),
    )(u0, u0, u0, u0, cw8, cw8, cb8, cb8)


def _ffn_act_bwd(u0, u, cw8, d_a, name):
    t = u0.shape[0]
    tm = _ffn_tm(t)
    nt = t // tm

    def body(g0_ref, v0_ref, g_ref, gn_ref, v_ref, vn_ref, wg_ref, wv_ref, da_ref, dan_ref, du_ref, acc_ref):
        i = pl.program_id(0)

        @pl.when(i == 0)
        def _():
            acc_ref[...] = jnp.zeros_like(acc_ref)

        def grads(gate, val, da):
            return da * val * _silu_grad(gate), da * gate * _sigmoid(gate)

        dgate, dval = grads(g_ref[...].astype(F32), v_ref[...].astype(F32), da_ref[...].astype(F32))
        dgate_n, dval_n = grads(gn_ref[0:SUB, :].astype(F32), vn_ref[0:SUB, :].astype(F32),
                                dan_ref[0:SUB, :].astype(F32))
        last = i == nt - 1
        du_ref[:, 0:D_FF] = _conv_grads(dgate, jnp.where(last, 0.0, dgate_n), g0_ref[...].astype(F32), wg_ref,
                                        acc_ref, FFN_K, slice(0, D_FF)).astype(_ACT)
        du_ref[:, D_FF:2 * D_FF] = _conv_grads(dval, jnp.where(last, 0.0, dval_n), v0_ref[...].astype(F32), wv_ref,
                                               acc_ref, FFN_K, slice(D_FF, 2 * D_FF)).astype(_ACT)

    cur = lambda col: pl.BlockSpec((tm, D_FF), lambda i: (i, col))
    nxt = lambda col: _next_halo(tm, t, D_FF, lambda i: (i, col))
    par = lambda col: pl.BlockSpec((SUB, D_FF), lambda i: (0, col))
    return pl.pallas_call(
        body, name=name,
        out_shape=(jax.ShapeDtypeStruct((t, 2 * D_FF), _ACT), jax.ShapeDtypeStruct((SUB, 2 * D_FF), F32)),
        grid=(nt,),
        in_specs=[cur(0), cur(1), cur(0), nxt(0), cur(1), nxt(1), par(0), par(1), cur(0), nxt(0)],
        out_specs=(pl.BlockSpec((tm, 2 * D_FF), lambda i: (i, 0)),
                   pl.BlockSpec((SUB, 2 * D_FF), lambda i: (0, 0))),
        compiler_params=_cp(("arbitrary",)),
    )(u0, u0, u, u, u, u, cw8, cw8, d_a, d_a)


_REST = ("w_branch_ssm", "w_branch_attn", "w_mix_out", "w_up", "w_down")


def _mm_side(*args, side, **kw):
    if side is None:
        return _mm(*args, **kw), []
    return _mm(*args, side=side, **kw)


def _local_step(x, x_bf, target, wts, ex):
    t = x.shape[0]
    wp = wts["wp"]
    scw = wts["ssm_conv_w"]
    scb = wts["ssm_conv_b"]
    fcw8 = _rows8(wts["ffn_conv_w"])
    fcb8 = _rows8(wts["ffn_conv_b"])
    pad_lane = lambda p: jnp.concatenate([p.astype(F32), jnp.zeros((1, LANE - p.shape[1]), F32)], axis=1)
    dtb8 = _rows8(pad_lane(wts["ssm_dt_bias"]))
    alog8 = _rows8(pad_lane(wts["ssm_a_log"]))
    dsk8 = _rows8(pad_lane(wts["ssm_d"]))
    bias_table = jnp.concatenate([wts["rel_bias"].T.astype(F32), wts["attn_sinks"].T.astype(F32),
                                  jnp.zeros((A_HEADS, BIAS_ROWS - REL_BUCKETS - 1), F32)], axis=1)
    nw8 = _rows8(wts["ssm_norm_w"])
    bg8 = _rows8(wts["b_gate"])
    g1_8, b1_8, g2_8, b2_8 = (_rows8(wts[k]) for k in ("ln1_g", "ln1_b", "ln2_g", "ln2_b"))
    xs_w8, xs_b8 = _rows8(scw[:, :D_INNER]), _rows8(scb[:, :D_INNER])
    bc_w8, bc_b8 = _rows8(scw[:, D_INNER:]), _rows8(scb[:, D_INNER:])

    proj, stacks = _mm_side(x_bf, wp[:, :P_MAIN], "mm_in", out_dtype=_ACT, side=ex.gather_rest())
    wts = dict(wts, **ex.rest_weights(stacks))
    w_bs, w_ba, w_mix, w_up, w_dn = (wts[k] for k in _REST)
    tail = _mm(x_bf, wp[:, P_MAIN:], "mm_in_tail")
    xs_c, xs_pre = _conv_silu_fwd(proj, P_XS // _TC, D_INNER // _TC, xs_w8, xs_b8, "conv_xs_fwd")
    bc_c, bc_pre = _conv_silu_fwd(proj, P_BC // _TC, 1024 // _TC, bc_w8, bc_b8, "conv_bc_fwd")
    y_ssd, y_ssm, hprev = _ssd_fwd(xs_c, bc_c, proj, tail, dtb8, alog8, dsk8, nw8, "ssd_fwd")
    bias = _bias_expand(bias_table, "bias_expand").reshape(2, A_HEADS, WIN, 2 * WIN)
    y_attn = _attn_fwd(proj, tail, bias, "attn_fwd")
    bs = _mm(y_ssm, w_bs, "mm_bs", out_dtype=_ACT)
    ba = _mm(y_attn, w_ba, "mm_ba", out_dtype=_ACT)
    merged = _merge_fwd(bs, ba, proj, bg8, "merge_fwd")
    mix = _mm(merged, w_mix, "mm_mix", out_dtype=_ACT)
    xhat1, h1_bf, rstd1 = _ln1_fwd(x, mix, g1_8, b1_8, "ln1_fwd")
    u0 = _mm(h1_bf, w_up, "mm_up", out_dtype=_ACT)
    act, u_conv = _ffn_act_fwd(u0, fcw8, fcb8, "ffn_act_fwd")
    ffn = _mm(act, w_dn, "mm_down", out_dtype=_ACT)
    d_r2, d_r2_bf, acc_ln2 = _ln2_loss(xhat1, ffn, target, g1_8, b1_8, g2_8, b2_8, "ln2_loss")
    d_w_dn = _mm(act, d_r2_bf, "mm_dw_down", trans_a=True)
    d_act = _mm(d_r2_bf, w_dn.T, "mm_d_act", out_dtype=_ACT)
    d_u0, acc_ffn = _ffn_act_bwd(u0, u_conv, fcw8, d_act, "ffn_act_bwd")
    d_w_up = _mm(h1_bf, d_u0, "mm_dw_up", trans_a=True)
    d_h1_ffn = _mm(d_u0, w_up.T, "mm_d_h1", out_dtype=_ACT)
    d_r1, d_r1_bf, acc_ln1 = _ln1_bwd(d_r2, d_h1_ffn, xhat1, rstd1, g1_8, "ln1_bwd")
    d_w_mix = _mm(merged, d_r1_bf, "mm_dw_mix", trans_a=True)
    d_merged = _mm(d_r1_bf, w_mix.T, "mm_d_merged", out_dtype=_ACT)
    d_bs, d_ba, d_proj, acc_bg = _merge_bwd(d_merged, bs, ba, proj, bg8, "merge_bwd")
    d_w_bs = _mm(y_ssm, d_bs, "mm_dw_bs", trans_a=True)
    d_w_ba = _mm(y_attn, d_ba, "mm_dw_ba", trans_a=True)
    grads = {"w_branch_ssm": d_w_bs, "w_branch_attn": d_w_ba, "w_mix_out": d_w_mix, "w_up": d_w_up, "w_down": d_w_dn}
    d_y_ssm, swapped = _mm_side(d_bs, w_bs.T, "mm_d_yssm", out_dtype=_ACT, side=ex.swap_job(grads))
    d_y_attn = _mm(d_ba, w_ba.T, "mm_d_yattn", out_dtype=_ACT)
    d_proj, d_k, d_v, d_bias = _attn_bwd(proj, tail, bias, y_attn, d_y_attn, d_proj, "attn_bwd")
    d_table = _bias_reduce(d_bias.reshape(A_HEADS, WIN * 2 * WIN), "bias_reduce")
    d_xs_c, d_bc_c, d_proj, d_dt, acc_ssd, acc_nw = _ssd_bwd(
        d_y_ssm, y_ssd, xs_c, bc_c, proj, tail, hprev, dtb8, alog8, dsk8, nw8, d_proj, "ssd_bwd")
    d_proj, acc_xs = _conv_silu_bwd(proj, P_XS // _TC, D_INNER // _TC, xs_pre, xs_w8, d_xs_c, d_proj, "conv_xs_bwd")
    d_proj, acc_bc = _conv_silu_bwd(proj, P_BC // _TC, 1024 // _TC, bc_pre, bc_w8, d_bc_c, d_proj, "conv_bc_bwd")
    d_proj = _place_tail(d_k, d_v, d_dt, d_proj, "place_tail")
    d_wp, landed_rest = _mm_side(x_bf, d_proj, "mm_dw_in", trans_a=True, side=ex.scatter_job(swapped))
    d_x, landed_in = _mm_side(d_proj, wp.T, "mm_d_x", res=d_r1, res_scale=ALPHA, side=ex.reduce_job({"wp": d_wp}))
    grads.update({
        "wp": d_wp,
        "ssm_conv_w": jnp.concatenate([acc_xs[0:SSM_K], acc_bc[0:SSM_K]], axis=1),
        "ffn_conv_w": acc_ffn[0:FFN_K],
    })
    small = {
        "rel_bias": d_table[:, 0:REL_BUCKETS].T,
        "b_gate": acc_bg[0:1],
        "ssm_conv_b": jnp.concatenate([acc_xs[SSM_K:SSM_K + 1], acc_bc[SSM_K:SSM_K + 1]], axis=1),
        "ssm_dt_bias": acc_ssd[0:1, 0:N_HEADS], "ssm_a_log": acc_ssd[1:2, 0:N_HEADS], "ssm_d": acc_ssd[2:3, 0:N_HEADS],
        "ssm_norm_w": acc_nw[0:1],
        "attn_sinks": d_table[:, REL_BUCKETS:REL_BUCKETS + 1].T,
        "ln1_g": acc_ln1[0:1], "ln1_b": acc_ln1[1:2],
        "ffn_conv_b": acc_ffn[FFN_K:FFN_K + 1],
        "ln2_g": acc_ln2[0:1], "ln2_b": acc_ln2[1:2],
        "loss_lanes": acc_ln2[2:3],
    }
    return d_x, grads, small, landed_rest + landed_in


_MATS = (("w_in", (1024, 2120), 1), ("w_branch_ssm", (512, 1024), 0), ("w_branch_attn", (256, 1024), 0),
         ("w_mix_out", (256, 1024), 0), ("w_up", (1024, 1408), 1), ("w_down", (704, 1024), 0))
_CONVS = (("ssm_conv_w", (4, 768)), ("ffn_conv_w", (3, 1408)))
_CONV_ROWS = 64

_SMALL = (("rel_bias", (32, 16)), ("b_gate", (1, 2048)), ("ssm_conv_b", (1, 3072)), ("ssm_dt_bias", (1, 32)),
          ("ssm_a_log", (1, 32)), ("ssm_d", (1, 32)), ("ssm_norm_w", (1, 2048)), ("attn_sinks", (1, 16)),
          ("ln1_g", (1, 1024)), ("ln1_b", (1, 1024)), ("ffn_conv_b", (1, 5632)), ("ln2_g", (1, 1024)),
          ("ln2_b", (1, 1024)), ("g_ssm_conv_w", (4, 3072)), ("g_ffn_conv_w", (3, 5632)), ("loss_lanes", (1, 1024)))


def _small_rows(shape):
    rows = -(-(shape[0] * shape[1]) // LANE)
    return -(-rows // SUB) * SUB


def _as_rows(a, rows, dtype):
    flat = a.reshape(-1).astype(dtype)
    flat = jnp.concatenate([flat, jnp.zeros((rows * LANE - flat.shape[0],), dtype)])
    return flat.reshape(rows, LANE)


def _pack_small(parts):
    blocks = [_as_rows(parts[n], _small_rows(s), F32) if n in parts else jnp.zeros((_small_rows(s), LANE), F32)
              for n, s in _SMALL]
    return jnp.concatenate(blocks, axis=0)


def _unpack_small(packed):
    out, at = {}, 0
    for n, s in _SMALL:
        rows = _small_rows(s)
        out[n] = packed[at:at + rows].reshape(-1)[:s[0] * s[1]].reshape(s)
        at += rows
    return out


def _to_stack(full, shape, axis):
    if axis == 0:
        return full.reshape((N_CHIPS,) + shape)
    return jnp.transpose(full.reshape(shape[0], N_CHIPS, shape[1]), (1, 0, 2))


def _from_stack(stack, axis):
    n, r, c = stack.shape
    if axis == 0:
        return stack.reshape(n * r, c)
    return jnp.transpose(stack, (1, 0, 2)).reshape(r, n * c)


_IN_SHARD = IN_COLS // N_CHIPS


def _cols_of_stack(stack, o, w):
    parts = []
    while w > 0:
        j, a = divmod(o, _IN_SHARD)
        n = min(w, _IN_SHARD - a)
        parts.append(stack[j][:, a:a + n])
        o, w = o + n, w - n
    return parts


def _pack_w_in_stack(stack):
    cols, at = [], 0
    for o, w, pk in sorted(_PIECES, key=lambda p: p[2]):
        if pk > at:
            cols.append(jnp.zeros((stack.shape[1], pk - at), stack.dtype))
        cols += _cols_of_stack(stack, o, w)
        at = pk + w
    cols.append(jnp.zeros((stack.shape[1], P_W - at), stack.dtype))
    return jnp.concatenate(cols, axis=1)


def _unpack_w_in_stack(wp):
    slabs = []
    for j in range(N_CHIPS):
        lo, hi = j * _IN_SHARD, (j + 1) * _IN_SHARD
        cols = []
        for o, w, pk in sorted(_PIECES):
            a, b = max(o, lo), min(o + w, hi)
            if a < b:
                cols.append(wp[:, pk + a - o:pk + b - o])
        slabs.append(jnp.concatenate(cols, axis=1))
    return jnp.stack(slabs)


_MESH = pl.DeviceIdType.MESH
_HBM = pl.BlockSpec(memory_space=pltpu.HBM)


def _position():
    return lax.axis_index("x"), lax.axis_index("y"), lax.axis_index("c")


def _other_chips(x, y):
    return ((1 - x, y), (x, 1 - y), (1 - x, 1 - y))


def _remote(src, dst, send_sem, recv_sem, to):
    return pltpu.make_async_remote_copy(src_ref=src, dst_ref=dst, send_sem=send_sem, recv_sem=recv_sem,
                                        device_id=to, device_id_type=_MESH)


def _run_job(job, name):
    n_in, n_out = len(job.inputs), len(job.out_shape)

    def body(*refs):
        parts = (refs[:n_in], refs[n_in:n_in + n_out], refs[n_in + n_out:])
        job.start(*parts)
        job.finish(*parts)

    return pl.pallas_call(body, name=name, out_shape=list(job.out_shape), in_specs=[_HBM] * n_in,
                          out_specs=[_HBM] * n_out, scratch_shapes=list(job.sems))(*job.inputs)


def _cast_rows(x, dtype, name, side):
    t, cols = x.shape
    tm = min(1024, t)
    nt = t // tm
    ns_in, ns_out = len(side.inputs), len(side.out_shape)

    def body(*refs):
        x_ref, o_ref = refs[0], refs[1 + ns_in]
        job_refs = (refs[1:1 + ns_in], refs[2 + ns_in:2 + ns_in + ns_out], refs[2 + ns_in + ns_out:])
        i = pl.program_id(0)

        @pl.when(i == 0)
        def _():
            side.start(*job_refs)

        o_ref[...] = x_ref[...].astype(dtype)

        @pl.when(i == nt - 1)
        def _():
            side.finish(*job_refs)

    blk = pl.BlockSpec((tm, cols), lambda i: (i, 0))
    outs = pl.pallas_call(
        body, name=name, out_shape=[jax.ShapeDtypeStruct((t, cols), dtype)] + list(side.out_shape), grid=(nt,),
        in_specs=[blk] + [_HBM] * ns_in, out_specs=[blk] + [_HBM] * ns_out, scratch_shapes=list(side.sems),
        compiler_params=_cp(("arbitrary",)),
    )(x, *side.inputs)
    return outs[0], list(outs[1:])


def _gather_job(shards, copy_own=True):
    n = len(shards)

    def plan(s_refs, o_refs, sems):
        send_sems, recv_sems, local_sems = sems
        x, y, c = _position()
        me = 2 * x + y
        sib = (x, y, 1 - c)
        chips = _other_chips(x, y)

        def copy(m, k, chip_idx, half, to, src=None):
            dst = o_refs[m].at[chip_idx, half]
            return _remote(dst if src is None else src, dst, send_sems.at[6 * m + k], recv_sems.at[6 * m + k], to)

        local = [pltpu.make_async_copy(s_refs[m], o_refs[m].at[me], local_sems.at[m]) for m in range(n)]
        local = local if copy_own else []
        first = [copy(m, i, me, c, (cx, cy, c), src=s_refs[m].at[c])
                 for i, (cx, cy) in enumerate(chips) for m in range(n)]
        return c, sib, chips, copy, local, first

    def start(s_refs, o_refs, sems):
        _, _, _, _, local, first = plan(s_refs, o_refs, sems)
        for cp in local + first:
            cp.start()

    def finish(s_refs, o_refs, sems):
        c, sib, chips, copy, local, first = plan(s_refs, o_refs, sems)
        passed = []
        for i, (cx, cy) in enumerate(chips):
            for m in range(n):
                copy(m, i, 2 * cx + cy, c, sib).wait_recv()
                passed.append(copy(m, 3 + i, 2 * cx + cy, c, sib))
                passed[-1].start()
        for i, (cx, cy) in enumerate(chips):
            for m in range(n):
                copy(m, 3 + i, 2 * cx + cy, 1 - c, sib).wait_recv()
        for cp in first + passed:
            cp.wait_send()
        for cp in local:
            cp.wait()

    return _SideJob(
        inputs=list(shards), out_shape=[jax.ShapeDtypeStruct((N_CHIPS,) + s.shape, s.dtype) for s in shards],
        sems=[pltpu.SemaphoreType.DMA((6 * n,)), pltpu.SemaphoreType.DMA((6 * n,)), pltpu.SemaphoreType.DMA((n,))],
        start=start, finish=finish)


def _swap_job(gs):
    n = len(gs)

    def copies(g_refs, o_refs, sems):
        send_sems, recv_sems = sems
        x, y, c = _position()
        return [_remote(g_refs[m].at[j, 1 - c], o_refs[m].at[j], send_sems.at[N_CHIPS * m + j],
                        recv_sems.at[N_CHIPS * m + j], (x, y, 1 - c)) for m in range(n) for j in range(N_CHIPS)]

    def start(*parts):
        for cp in copies(*parts):
            cp.start()

    def finish(*parts):
        for cp in copies(*parts):
            cp.wait()

    return _SideJob(
        inputs=list(gs), out_shape=[jax.ShapeDtypeStruct((N_CHIPS,) + g.shape[2:], g.dtype) for g in gs],
        sems=[pltpu.SemaphoreType.DMA((N_CHIPS * n,)), pltpu.SemaphoreType.DMA((N_CHIPS * n,))],
        start=start, finish=finish)


def _scatter_job(ps):
    n = len(ps)

    def copies(p_refs, o_refs, sems):
        send_sems, recv_sems = sems
        x, y, c = _position()
        return [_remote(p_refs[m].at[2 * cx + cy], o_refs[m].at[i], send_sems.at[3 * m + i], recv_sems.at[3 * m + i],
                        (cx, cy, c)) for i, (cx, cy) in enumerate(_other_chips(x, y)) for m in range(n)]

    def start(*parts):
        for cp in copies(*parts):
            cp.start()

    def finish(*parts):
        for cp in copies(*parts):
            cp.wait()

    return _SideJob(
        inputs=list(ps), out_shape=[jax.ShapeDtypeStruct((N_CHIPS - 1,) + p.shape[1:], p.dtype) for p in ps],
        sems=[pltpu.SemaphoreType.DMA((3 * n,)), pltpu.SemaphoreType.DMA((3 * n,))], start=start, finish=finish)


def _join_halves(fulls):
    n = len(fulls)

    def body(*refs):
        o_refs = refs[n:2 * n]
        send_sems, recv_sems = refs[2 * n:]
        x, y, c = _position()
        cps = [_remote(o_refs[m].at[c], o_refs[m].at[c], send_sems.at[m], recv_sems.at[m], (x, y, 1 - c))
               for m in range(n)]
        for cp in cps:
            cp.start()
        for cp in cps:
            cp.wait()

    return pl.pallas_call(
        body, name="join_halves",
        out_shape=[jax.ShapeDtypeStruct(f.shape, f.dtype) for f in fulls],
        in_specs=[_HBM] * n, out_specs=[_HBM] * n, input_output_aliases={m: m for m in range(n)},
        scratch_shapes=[pltpu.SemaphoreType.DMA((n,)), pltpu.SemaphoreType.DMA((n,))],
    )(*fulls)


def _allgather_small(mine, name):
    m_per, n = mine.shape

    def body(x_ref, out_ref, send_sems, recv_sems, local_sem):
        x, y, c = _position()
        me, sibling = (x, y, c), (x, y, 1 - c)
        chips = _other_chips(x, y)

        def rows(px, py, pc):
            return out_ref.at[pl.ds((4 * px + 2 * py + pc) * m_per, m_per), :]

        def copy(k, block, to, src=None):
            return pltpu.make_async_remote_copy(src_ref=rows(*block) if src is None else src, dst_ref=rows(*block),
                                                send_sem=send_sems.at[k], recv_sem=recv_sems.at[k],
                                                device_id=to, device_id_type=_MESH)

        own = pltpu.make_async_copy(x_ref, rows(*me), local_sem)
        own.start()
        first = [copy(0, me, sibling, src=x_ref)]
        first += [copy(1 + j, me, (*chip, c), src=x_ref) for j, chip in enumerate(chips)]
        for cp in first:
            cp.start()
        passed = [copy(4 + j, (*chip, c), sibling) for j, chip in enumerate(chips)]
        for j, chip in enumerate(chips):
            copy(1 + j, (*chip, c), me).wait_recv()
            passed[j].start()
        copy(0, sibling, me).wait_recv()
        for j, chip in enumerate(chips):
            copy(4 + j, (*chip, 1 - c), me).wait_recv()
        for cp in first + passed:
            cp.wait_send()
        own.wait()

    return pl.pallas_call(
        body, name=name, out_shape=jax.ShapeDtypeStruct((N_DEV * m_per, n), mine.dtype),
        in_specs=[pl.BlockSpec(memory_space=pltpu.VMEM)], out_specs=pl.BlockSpec(memory_space=pltpu.VMEM),
        scratch_shapes=[pltpu.SemaphoreType.DMA((7,)), pltpu.SemaphoreType.DMA((7,)), pltpu.SemaphoreType.DMA],
    )(mine)


_ADD_BLOCK_BYTES = 3 << 20


def _add_rows(hr, cols):
    if hr * cols * 4 <= _ADD_BLOCK_BYTES:
        return hr
    return _pick(hr, (256, 128, 64, 32, 16))


def _add_own_half(g, recv, c_idx, name):
    nseg, _, hr, cols = g.shape
    tr = _add_rows(hr, cols)

    def body(c_ref, g_ref, r_ref, o_ref, ob_ref):
        s = g_ref[...] + r_ref[...]
        o_ref[...] = s
        ob_ref[...] = s.astype(jnp.bfloat16)

    blk = pl.BlockSpec((None, tr, cols), lambda j, i, c_ref: (j, i, 0))
    return pl.pallas_call(
        body, name=name,
        out_shape=(jax.ShapeDtypeStruct((nseg, hr, cols), F32), jax.ShapeDtypeStruct((nseg, hr, cols), jnp.bfloat16)),
        grid_spec=pltpu.PrefetchScalarGridSpec(
            num_scalar_prefetch=1, grid=(nseg, hr // tr),
            in_specs=[pl.BlockSpec((None, None, tr, cols), lambda j, i, c_ref: (j, c_ref[0], i, 0)), blk],
            out_specs=(blk, blk)),
        compiler_params=_cp(("parallel", "parallel")),
    )(c_idx, g, recv)


def _add_chips(p, recv, chip_idx, c_idx, name):
    _, hr, cols = p.shape
    tr = _add_rows(hr, cols)

    def body(j_ref, c_ref, p_ref, r_ref, o_ref):
        o_ref[...] = ((p_ref[...] + r_ref[0].astype(F32)) + r_ref[1].astype(F32)) + r_ref[2].astype(F32)

    return pl.pallas_call(
        body, name=name, out_shape=jax.ShapeDtypeStruct((2, hr, cols), F32),
        grid_spec=pltpu.PrefetchScalarGridSpec(
            num_scalar_prefetch=2, grid=(hr // tr,),
            in_specs=[pl.BlockSpec((None, tr, cols), lambda i, j_ref, c_ref: (j_ref[0], i, 0)),
                      pl.BlockSpec((N_CHIPS - 1, tr, cols), lambda i, j_ref, c_ref: (0, i, 0))],
            out_specs=pl.BlockSpec((None, tr, cols), lambda i, j_ref, c_ref: (c_ref[0], i, 0))),
        compiler_params=_cp(("parallel",)),
    )(chip_idx, c_idx, p, recv)


def _adam_math(w, g, m, v):
    m = ADAM_B1 * m + (1.0 - ADAM_B1) * g
    v = ADAM_B2 * v + (1.0 - ADAM_B2) * (g * g)
    m_hat = m / (1.0 - ADAM_B1 ** ADAM_STEP)
    v_hat = v / (1.0 - ADAM_B2 ** ADAM_STEP)
    delta = -ADAM_LR * (m_hat / (jnp.sqrt(v_hat) + ADAM_EPS) + ADAM_WD * w)
    return delta, m, v


def _adam_big(w, g, m, v, name):
    rows, cols = w.shape
    tr = _pick(rows, (256, 128, 64, 32, 16, 8)) if rows % SUB == 0 else rows

    def body(w_ref, g_ref, m_ref, v_ref, d_ref, mo_ref, vo_ref):
        d_ref[...], mo_ref[...], vo_ref[...] = _adam_math(w_ref[...], g_ref[...], m_ref[...], v_ref[...])

    blk = pl.BlockSpec((tr, cols), lambda i: (i, 0))
    shp = jax.ShapeDtypeStruct((rows, cols), F32)
    return pl.pallas_call(
        body, name=name, out_shape=(shp, shp, shp), grid=(rows // tr,),
        in_specs=[blk, blk, blk, blk], out_specs=(blk, blk, blk), compiler_params=_cp(("parallel",)),
    )(w, g, m, v)


def _adam_small(w, gathered, m, v):
    rows = w.shape[0]

    def body(w_ref, a_ref, m_ref, v_ref, g_ref, d_ref, mo_ref, vo_ref):
        g = a_ref[0:rows, :]
        for k in range(1, N_DEV):
            g = g + a_ref[k * rows:(k + 1) * rows, :]
        g_ref[...] = g
        d_ref[...], mo_ref[...], vo_ref[...] = _adam_math(w_ref[...], g, m_ref[...], v_ref[...])

    shp = jax.ShapeDtypeStruct((rows, LANE), F32)
    return pl.pallas_call(body, name="adam_small", out_shape=(shp, shp, shp, shp), compiler_params=_cp(None))(
        w, gathered, m, v)


_WEIGHTS = ("rel_bias", "w_in", "b_gate", "ssm_conv_w", "ssm_conv_b", "ssm_dt_bias", "ssm_a_log", "ssm_d",
            "ssm_norm_w", "attn_sinks", "w_branch_ssm", "w_branch_attn", "w_mix_out", "ln1_g", "ln1_b", "w_up",
            "ffn_conv_w", "ffn_conv_b", "w_down", "ln2_g", "ln2_b")
_REPLICATED = tuple(n for n, _ in _SMALL[:13])


class _Exchange:
    def __init__(self, w, chip, core):
        self.chip = chip
        self.c_idx = jnp.reshape(core, (1,)).astype(jnp.int32)
        self.chip_idx = jnp.reshape(chip, (1,)).astype(jnp.int32)
        self.shards = {n: w[n].astype(jnp.bfloat16).reshape(2, s[0] // 2, s[1]) for n, s, _ in _MATS}
        self.spec = {n: (s, ax) for n, s, ax in _MATS}
        self.sums = {}

    def _with_own(self, n, stack):
        shape = self.spec[n][0]
        slab = lax.broadcasted_iota(jnp.int32, (N_CHIPS, 1, 1), 0)
        return jnp.where(slab == self.chip, self.shards[n].reshape((1,) + shape), stack.reshape((N_CHIPS,) + shape))

    def cast_and_gather_w_in(self, x):
        x_act, (stack,) = _cast_rows(x, _ACT, "cast_x", _gather_job([self.shards["w_in"]], copy_own=False))
        return x_act, _pack_w_in_stack(self._with_own("w_in", stack))

    def gather_rest(self):
        return _gather_job([self.shards[n] for n in _REST], copy_own=False)

    def rest_weights(self, stacks):
        return {n: _from_stack(self._with_own(n, st), self.spec[n][1]) for n, st in zip(_REST, stacks)}

    def swap_job(self, grads):
        self.names, self.stacks = [], []
        for n, g in grads.items():
            name = "w_in" if n == "wp" else n
            s, ax = self.spec[name]
            st = _unpack_w_in_stack(g) if n == "wp" else _to_stack(g, s, ax)
            self.names.append(name)
            self.stacks.append(st.reshape(N_CHIPS, 2, s[0] // 2, s[1]))
        return _swap_job(self.stacks)

    def scatter_job(self, swapped):
        halves = []
        for n, g, r in zip(self.names, self.stacks, swapped):
            self.sums[n], bf = _add_own_half(g, r, self.c_idx, "add_own_" + n)
            halves.append(bf)
        return _scatter_job(halves)

    def reduce_job(self, grads):
        return self.scatter_job(_run_job(self.swap_job(grads), "swap_" + "_".join(grads)))

    def reduced(self, landed):
        names = list(self.sums)
        reds = [_add_chips(self.sums[n], r, self.chip_idx, self.c_idx, "add_chips_" + n)
                for n, r in zip(names, landed)]
        return {n: g.reshape(self.spec[n][0]) for n, g in zip(names, _join_halves(reds))}


def _step(x, target, w, m, v):
    xi, yi, ci = _position()
    chip = 2 * xi + yi
    ex = _Exchange(w, chip, ci)

    wts = {n: w[n] for n in _REPLICATED}
    x_act, wts["wp"] = ex.cast_and_gather_w_in(x)
    taps = jnp.concatenate([w[n].astype(F32).reshape(-1) for n, _ in _CONVS])
    taps = _allgather_small(_as_rows(taps, _CONV_ROWS, F32), "allgather_taps")
    taps = taps.reshape(N_CHIPS, 2, _CONV_ROWS * LANE)[:, 0]
    at = 0
    for n, s in _CONVS:
        wts[n] = _from_stack(taps[:, at:at + s[0] * s[1]].reshape((N_CHIPS,) + s), 1)
        at += s[0] * s[1]

    d_x, grads, small, landed = _local_step(x, x_act, target, wts, ex)

    outs = {"grad": ex.reduced(landed), "delta": {}, "m": {}, "v": {}}

    small = dict(small, g_ssm_conv_w=grads["ssm_conv_w"], g_ffn_conv_w=grads["ffn_conv_w"])
    all_small = _allgather_small(_pack_small(small), "allgather_small")
    packs = [_pack_small({n: d[n] for n in _REPLICATED}) for d in (w, m, v)]
    g_s, d_s, m_s, v_s = (_unpack_small(a) for a in _adam_small(packs[0], all_small, packs[1], packs[2]))
    for kind, part in (("grad", g_s), ("delta", d_s), ("m", m_s), ("v", v_s)):
        outs[kind].update({n: part[n] for n in _REPLICATED})
    for n, s in _CONVS:
        outs["grad"][n] = lax.dynamic_slice_in_dim(g_s["g_" + n], chip * s[1], s[1], axis=1)
    for n in [n for n, _, _ in _MATS] + [n for n, _ in _CONVS]:
        outs["delta"][n], outs["m"][n], outs["v"][n] = _adam_big(
            w[n].astype(F32), outs["grad"][n], m[n].astype(F32), v[n].astype(F32), "adam_" + n)
    loss = (0.5 / D_MODEL) * jnp.sum(g_s["loss_lanes"])
    return loss, d_x, outs


def kernel(x, rel_bias, w_in, b_gate, ssm_conv_w, ssm_conv_b, ssm_dt_bias, ssm_a_log, ssm_d, ssm_norm_w, attn_sinks, w_branch_ssm, w_branch_attn, w_mix_out, ln1_g, ln1_b, w_up, ffn_conv_w, ffn_conv_b, w_down, ln2_g, ln2_b, loss_target, m_rel_bias, m_w_in, m_b_gate, m_ssm_conv_w, m_ssm_conv_b, m_ssm_dt_bias, m_ssm_a_log, m_ssm_d, m_ssm_norm_w, m_attn_sinks, m_w_branch_ssm, m_w_branch_attn, m_w_mix_out, m_ln1_g, m_ln1_b, m_w_up, m_ffn_conv_w, m_ffn_conv_b, m_w_down, m_ln2_g, m_ln2_b, v_rel_bias, v_w_in, v_b_gate, v_ssm_conv_w, v_ssm_conv_b, v_ssm_dt_bias, v_ssm_a_log, v_ssm_d, v_ssm_norm_w, v_attn_sinks, v_w_branch_ssm, v_w_branch_attn, v_w_mix_out, v_ln1_g, v_ln1_b, v_w_up, v_ffn_conv_w, v_ffn_conv_b, v_w_down, v_ln2_g, v_ln2_b):
    given = dict(locals())
    drop = lambda a, n: a if n == "rel_bias" or a.ndim == 2 else a[0]
    w = {n: drop(given[n], n) for n in _WEIGHTS}
    m = {n: drop(given["m_" + n], n) for n in _WEIGHTS}
    v = {n: drop(given["v_" + n], n) for n in _WEIGHTS}
    loss, d_x, outs = _step(x[0], loss_target[0], w, m, v)
    like = lambda a, n: a.reshape(given[n].shape)
    res = [loss, d_x[None]]
    for kind in ("grad", "delta", "m", "v"):
        res += [like(outs[kind][n], n) for n in _WEIGHTS]
    return tuple(res)
```

```python
import math
from typing import NamedTuple

import numpy as np
import jax
import jax.numpy as jnp
from jax import lax
from jax.experimental import pallas as pl
from jax.experimental.pallas import tpu as pltpu

F32 = jnp.float32
_ACT = jnp.bfloat16
_MXU = jnp.bfloat16

D_MODEL = 1024
D_INNER = 2048
N_HEADS = 32
HEAD_P = 64
N_GROUPS = 4
N_STATE = 128
CHUNK = 128
SSM_K = 4
A_HEADS = 16
A_DH = 64
WIN = 128
REL_BUCKETS = 32
BIAS_ROWS = 64
D_FF = 2816
FFN_K = 3
ALPHA = 2.0 ** 0.25
LN_EPS = 1e-5
RMS_EPS = 1e-5
IN_COLS = 8480
NEG = -1e30

ADAM_LR = 0.001
ADAM_B1 = 0.9
ADAM_B2 = 0.999
ADAM_EPS = 1e-08
ADAM_WD = 0.01
ADAM_STEP = 10

LANE = 128
SUB = 8

P_Z, P_XS, P_G, P_Q, P_BC, P_K, P_V, P_DT = 0, 2048, 4096, 6144, 7168, 8192, 8320, 8448
P_W = 8704
P_MAIN = 8192
T_K, T_V, T_DT = P_K - P_MAIN, P_V - P_MAIN, P_DT - P_MAIN
_PIECES = ((0, 2048, P_Z), (2048, 2048, P_XS), (4096, 1024, P_BC), (5120, 32, P_DT), (5152, 1024, P_Q),
           (6176, 128, P_K), (6304, 128, P_V), (6432, 2048, P_G))

N_CHIPS = 4
N_DEV = 8


def _cp(sem=None, vmem_mb=48):
    return pltpu.CompilerParams(dimension_semantics=sem, vmem_limit_bytes=vmem_mb * 1024 * 1024)


def _pick(n, cands):
    for c in cands:
        if n % c == 0:
            return c
    raise ValueError(f"no block size for {n}")


def _rows8(p):
    k, c = p.shape
    return jnp.concatenate([p.astype(F32), jnp.zeros((SUB - k, c), F32)], axis=0)


class _SideJob(NamedTuple):
    inputs: list
    out_shape: list
    sems: list
    start: object
    finish: object


def _mm(a, b, name, *, trans_a=False, out_dtype=F32, res=None, res_scale=1.0, side=None):
    if trans_a:
        k_dim, m = a.shape
    else:
        m, k_dim = a.shape
    k2, n = b.shape
    assert k_dim == k2, (a.shape, b.shape)
    tm = _pick(m, (1408, 1024, 512, 256, 128))
    tn = _pick(n, (1408, 1024, 512, 256, 128))
    tk = _pick(k_dim, (2816, 2176, 2048, 1024, 512, 256, 128))
    nk = k_dim // tk
    grid = (m // tm, n // tn, nk)
    dn = (((0,), (0,)), ((), ())) if trans_a else (((1,), (0,)), ((), ()))
    n_in = 2 if res is None else 3
    ns_in = len(side.inputs) if side else 0
    ns_out = len(side.out_shape) if side else 0

    def body(*refs):
        a_ref, b_ref = refs[0], refs[1]
        o_ref = refs[n_in + ns_in]
        scratch = refs[n_in + ns_in + 1 + ns_out:]
        job_refs = (refs[n_in:n_in + ns_in], refs[n_in + ns_in + 1:n_in + ns_in + 1 + ns_out],
                    scratch[1:] if nk > 1 else scratch)
        i, j, k = pl.program_id(0), pl.program_id(1), pl.program_id(2)

        def finish(r):
            if res is not None:
                r = r + res_scale * refs[2][...]
            o_ref[...] = r.astype(out_dtype)

        if side:
            @pl.when(jnp.logical_and(jnp.logical_and(i == 0, j == 0), k == 0))
            def _():
                side.start(*job_refs)

        part = lax.dot_general(a_ref[...].astype(_MXU), b_ref[...].astype(_MXU), dn, preferred_element_type=F32)
        if nk == 1:
            finish(part)
        else:
            acc = scratch[0]

            @pl.when(k == 0)
            def _():
                acc[...] = part

            @pl.when(k > 0)
            def _():
                acc[...] += part

            @pl.when(k == nk - 1)
            def _():
                finish(acc[...])

        if side:
            @pl.when(jnp.logical_and(jnp.logical_and(i == grid[0] - 1, j == grid[1] - 1), k == nk - 1))
            def _():
                side.finish(*job_refs)

    if trans_a:
        a_spec = pl.BlockSpec((tk, tm), lambda i, j, k: (k, i))
    else:
        a_spec = pl.BlockSpec((tm, tk), lambda i, j, k: (i, k))
    in_specs = [a_spec, pl.BlockSpec((tk, tn), lambda i, j, k: (k, j))]
    args = [a, b]
    if res is not None:
        in_specs.append(pl.BlockSpec((tm, tn), lambda i, j, k: (i, j)))
        args.append(res)
    out_spec = pl.BlockSpec((tm, tn), lambda i, j, k: (i, j))
    out_shape = jax.ShapeDtypeStruct((m, n), out_dtype)
    scratch_shapes = [pltpu.VMEM((tm, tn), F32)] if nk > 1 else []
    if not side:
        return pl.pallas_call(
            body, name=name, out_shape=out_shape, grid=grid, in_specs=in_specs, out_specs=out_spec,
            scratch_shapes=scratch_shapes, compiler_params=_cp(("parallel", "parallel", "arbitrary")),
        )(*args)
    hbm = pl.BlockSpec(memory_space=pltpu.HBM)
    outs = pl.pallas_call(
        body, name=name, out_shape=[out_shape] + list(side.out_shape), grid=grid,
        in_specs=in_specs + [hbm] * ns_in, out_specs=[out_spec] + [hbm] * ns_out,
        scratch_shapes=scratch_shapes + list(side.sems),
        compiler_params=_cp(("arbitrary", "arbitrary", "arbitrary")),
    )(*args, *side.inputs)
    return outs[0], list(outs[1:])


def _shift_down(cur, prev8, s):
    r = pltpu.roll(cur, s, 0)
    p = pltpu.roll(prev8, s, 0)
    row8 = lax.broadcasted_iota(jnp.int32, (SUB, 1), 0)
    fixed = jnp.where(row8 < s, p, r[0:SUB])
    if cur.shape[0] == SUB:
        return fixed
    return jnp.concatenate([fixed, r[SUB:]], axis=0)


def _shift_up(cur, next8, s):
    tm = cur.shape[0]
    r = pltpu.roll(cur, tm - s, 0)
    p = pltpu.roll(next8, SUB - s, 0)
    row8 = lax.broadcasted_iota(jnp.int32, (SUB, 1), 0)
    fixed = jnp.where(row8 >= SUB - s, p, r[tm - SUB:])
    return jnp.concatenate([r[:tm - SUB], fixed], axis=0)


def _conv_pre(cur, prev8, w_ref, b_row, taps):
    acc = cur * w_ref[taps - 1:taps, :] + b_row
    for s in range(1, taps):
        acc = acc + _shift_down(cur, prev8, s) * w_ref[taps - 1 - s:taps - s, :]
    return acc


def _dot01_r(x, m01, parts=3):
    acc = None
    r = x
    for _ in range(parts):
        hi = r.astype(jnp.bfloat16)
        t = jnp.dot(hi, m01, preferred_element_type=F32)
        acc = t if acc is None else acc + t
        r = r - hi.astype(F32)
    return acc


def _dot01_l(m01, x, parts=3):
    acc = None
    r = x
    for _ in range(parts):
        hi = r.astype(jnp.bfloat16)
        t = jnp.dot(m01, hi, preferred_element_type=F32)
        acc = t if acc is None else acc + t
        r = r - hi.astype(F32)
    return acc


def _dot(a, b):
    return jnp.dot(a.astype(_MXU), b.astype(_MXU), preferred_element_type=F32)


def _dot_nt(a, b):
    return lax.dot_general(a.astype(_MXU), b.astype(_MXU), (((1,), (1,)), ((), ())), preferred_element_type=F32)


def _dot_tn(a, b):
    return lax.dot_general(a.astype(_MXU), b.astype(_MXU), (((0,), (0,)), ((), ())), preferred_element_type=F32)


def _sigmoid(x):
    return 1.0 / (1.0 + jnp.exp(-x))


def _half_masks():
    lane = lax.broadcasted_iota(jnp.int32, (1, LANE), 1)
    lo = (lane < 64).astype(F32)
    return lo, 1.0 - lo


_TC = 1024


def _tm_rows(t):
    return min(512, t)


HALO = 16


def _prev_halo(tm, width, pos):
    def index(*ids):
        i, col = pos(*ids)
        return (jnp.maximum(i * (tm // HALO) - 1, 0), col)
    return pl.BlockSpec((HALO, width), index)


def _next_halo(tm, t, width, pos):
    def index(*ids):
        i, col = pos(*ids)
        return (jnp.minimum((i + 1) * (tm // HALO), t // HALO - 1), col)
    return pl.BlockSpec((HALO, width), index)


def _conv_silu_fwd(proj, colblk0, nblk, w8, b8, name):
    t = proj.shape[0]
    tm = _tm_rows(t)

    def body(c_ref, p_ref, w_ref, b_ref, o_ref, pre_ref):
        i = pl.program_id(1)
        prev8 = jnp.where(i > 0, p_ref[SUB:HALO, :].astype(F32), 0.0)
        pre = _conv_pre(c_ref[...].astype(F32), prev8, w_ref, b_ref[0:1, :], SSM_K)
        o_ref[...] = pre * _sigmoid(pre)
        pre_ref[...] = pre.astype(_ACT)

    blk = pl.BlockSpec((tm, _TC), lambda j, i: (i, j))
    return pl.pallas_call(
        body, name=name,
        out_shape=(jax.ShapeDtypeStruct((t, nblk * _TC), F32), jax.ShapeDtypeStruct((t, nblk * _TC), _ACT)),
        grid=(nblk, t // tm),
        in_specs=[pl.BlockSpec((tm, _TC), lambda j, i: (i, colblk0 + j)),
                  _prev_halo(tm, _TC, lambda j, i: (i, colblk0 + j)),
                  pl.BlockSpec((SUB, _TC), lambda j, i: (0, j)),
                  pl.BlockSpec((SUB, _TC), lambda j, i: (0, j))],
        out_specs=(blk, blk),
        compiler_params=_cp(("parallel", "parallel")),
    )(proj, proj, w8, b8)


def _silu_grad(pre):
    sg = _sigmoid(pre)
    return sg * (1.0 + pre * (1.0 - sg))


def _conv_grads(d, d_next8, cur, w_ref, acc_ref, taps, cols=slice(None)):
    du = d * w_ref[taps - 1:taps, :]
    acc_ref[taps:taps + 1, cols] += jnp.sum(d, axis=0, keepdims=True)
    acc_ref[taps - 1:taps, cols] += jnp.sum(d * cur, axis=0, keepdims=True)
    for s in range(1, taps):
        up = _shift_up(d, d_next8, s)
        du = du + up * w_ref[taps - 1 - s:taps - s, :]
        acc_ref[taps - 1 - s:taps - s, cols] += jnp.sum(up * cur, axis=0, keepdims=True)
    return du


def _conv_silu_bwd(proj, colblk0, nblk, pre, w8, d_out, d_proj, name):
    t = proj.shape[0]
    tm = _tm_rows(t)
    nt = t // tm

    def body(c_ref, pre_ref, pren_ref, w_ref, d_ref, dn_ref, _, du_ref, acc_ref):
        i = pl.program_id(1)

        @pl.when(i == 0)
        def _():
            acc_ref[...] = jnp.zeros_like(acc_ref)

        dpre = d_ref[...].astype(F32) * _silu_grad(pre_ref[...].astype(F32))
        dpre_n = jnp.where(i < nt - 1, dn_ref[0:SUB, :].astype(F32) * _silu_grad(pren_ref[0:SUB, :].astype(F32)), 0.0)
        du_ref[...] = _conv_grads(dpre, dpre_n, c_ref[...].astype(F32), w_ref, acc_ref, SSM_K).astype(_ACT)

    c = nblk * _TC
    blk = pl.BlockSpec((tm, _TC), lambda j, i: (i, j))
    nxt = _next_halo(tm, t, _TC, lambda j, i: (i, j))
    par = pl.BlockSpec((SUB, _TC), lambda j, i: (0, j))
    return pl.pallas_call(
        body, name=name,
        out_shape=(jax.ShapeDtypeStruct(d_proj.shape, d_proj.dtype), jax.ShapeDtypeStruct((SUB, c), F32)),
        grid=(nblk, nt),
        in_specs=[pl.BlockSpec((tm, _TC), lambda j, i: (i, colblk0 + j)), blk, nxt, par, blk, nxt,
                  pl.BlockSpec(memory_space=pl.ANY)],
        out_specs=(pl.BlockSpec((tm, _TC), lambda j, i: (i, colblk0 + j)), par),
        input_output_aliases={6: 0},
        compiler_params=_cp(("parallel", "arbitrary")),
    )(proj, pre, pre, w8, d_out, d_out, d_proj)


def _expand_consts():
    e = np.zeros((LANE, D_INNER), np.float32)
    for h in range(N_HEADS):
        e[h, h * HEAD_P:(h + 1) * HEAD_P] = 1.0
    return jnp.asarray(e, jnp.bfloat16), jnp.asarray(e.T.copy(), jnp.bfloat16)


def _ssd_common(dtr_ref, dtb_ref, alog_ref, e_ref):
    lane = lax.broadcasted_iota(jnp.int32, (1, LANE), 1)
    hm = lane < N_HEADS
    pre = dtr_ref[...] + dtb_ref[0:1, :]
    dt = jnp.where(hm, jnp.maximum(pre, 0.0) + jnp.log(1.0 + jnp.exp(-jnp.abs(pre))), 0.0)
    a_row = jnp.where(hm, -jnp.exp(alog_ref[0:1, :]), 0.0)
    adt = dt * a_row
    r = lax.broadcasted_iota(jnp.int32, (CHUNK, CHUNK), 0)
    c = lax.broadcasted_iota(jnp.int32, (CHUNK, CHUNK), 1)
    causal = r >= c
    acs = _dot01_l(causal.astype(jnp.bfloat16), adt)
    e = e_ref[...]
    acs_x = _dot01_r(acs, e, parts=2)
    dt_x = _dot01_r(dt, e, parts=2)
    return pre, dt, a_row, acs, acs_x, dt_x, causal, hm


def _decay(acs, acs_t, h, causal):
    seg = acs[:, h:h + 1] - acs_t[h:h + 1, :]
    return jnp.exp(jnp.where(causal, seg, NEG))


def _ssd_fwd(xs_c, bc_c, proj, tail, dtb8, alog8, dsk8, nw8, name):
    t = xs_c.shape[0]
    nc = t // CHUNK
    e_bf, _ = _expand_consts()
    gw = D_INNER // N_GROUPS

    def body(xs_ref, bc_ref, dtr_ref, z_ref, dtb_ref, alog_ref, dsk_ref, nw_ref, e_ref,
             y_ref, ys_ref, hp_ref, h_ref):
        c_id = pl.program_id(0)

        @pl.when(c_id == 0)
        def _():
            h_ref[...] = jnp.zeros_like(h_ref)

        _, dt, a_row, acs, acs_x, dt_x, causal, _ = _ssd_common(dtr_ref, dtb_ref, alog_ref, e_ref)
        acs_t = acs.T
        xs = xs_ref[...]
        x_dt = xs * dt_x
        last_x = acs_x[CHUNK - 1:CHUNK, :]
        w_end = jnp.exp(last_x - acs_x)
        e_in = jnp.exp(acs_x)
        d_x = _dot01_r(dsk_ref[...], e_ref[...])[0:1, :]
        hprev = h_ref[...]
        hp_ref[...] = hprev
        lo, hi = _half_masks()
        for g in range(N_GROUPS):
            bg = bc_ref[:, g * N_STATE:(g + 1) * N_STATE]
            cg = bc_ref[:, N_GROUPS * N_STATE + g * N_STATE:N_GROUPS * N_STATE + (g + 1) * N_STATE]
            sl = slice(g * gw, (g + 1) * gw)
            gm = _dot_nt(cg, bg)
            st = _dot(bg.T, x_dt[:, sl] * w_end[:, sl])
            y_off = _dot(cg, hprev[:, sl]) * e_in[:, sl]
            for j in range(gw // LANE):
                h0 = g * (gw // HEAD_P) + 2 * j
                cs = slice(g * gw + j * LANE, g * gw + (j + 1) * LANE)
                xp = x_dt[:, cs]
                m0 = gm * _decay(acs, acs_t, h0, causal)
                m1 = gm * _decay(acs, acs_t, h0 + 1, causal)
                yd = _dot(m0, xp * lo) + _dot(m1, xp * hi)
                y_ref[:, cs] = yd + y_off[:, j * LANE:(j + 1) * LANE] + xs[:, cs] * d_x[:, cs]
            h_ref[:, sl] = hprev[:, sl] * jnp.exp(last_x[:, sl]) + st
        y = y_ref[...]
        z = z_ref[...].astype(F32)
        y2 = y * (z * _sigmoid(z))
        for g in range(N_GROUPS):
            sl = slice(g * gw, (g + 1) * gw)
            yg = y2[:, sl]
            rinv = lax.rsqrt(jnp.mean(yg * yg, axis=-1, keepdims=True) + RMS_EPS)
            ys_ref[:, sl] = (yg * rinv * nw_ref[0:1, sl]).astype(_ACT)

    small = pl.BlockSpec((SUB, LANE), lambda c: (0, 0))
    return pl.pallas_call(
        body, name=name,
        out_shape=(jax.ShapeDtypeStruct((t, D_INNER), F32), jax.ShapeDtypeStruct((t, D_INNER), _ACT),
                   jax.ShapeDtypeStruct((t, D_INNER), F32)),
        grid=(nc,),
        in_specs=[pl.BlockSpec((CHUNK, D_INNER), lambda c: (c, 0)),
                  pl.BlockSpec((CHUNK, 1024), lambda c: (c, 0)),
                  pl.BlockSpec((CHUNK, LANE), lambda c: (c, T_DT // LANE)),
                  pl.BlockSpec((CHUNK, D_INNER), lambda c: (c, P_Z // D_INNER)),
                  small, small, small,
                  pl.BlockSpec((SUB, D_INNER), lambda c: (0, 0)),
                  pl.BlockSpec((LANE, D_INNER), lambda c: (0, 0))],
        out_specs=(pl.BlockSpec((CHUNK, D_INNER), lambda c: (c, 0)),
                   pl.BlockSpec((CHUNK, D_INNER), lambda c: (c, 0)),
                   pl.BlockSpec((N_STATE, D_INNER), lambda c: (c, 0))),
        scratch_shapes=[pltpu.VMEM((N_STATE, D_INNER), F32)],
        compiler_params=_cp(("arbitrary",)),
    )(xs_c, bc_c, tail, proj, dtb8, alog8, dsk8, nw8, e_bf)


def _ssd_bwd(d_ys, y, xs_c, bc_c, proj, tail, hprev_all, dtb8, alog8, dsk8, nw8, d_proj, name):
    t = xs_c.shape[0]
    nc = t // CHUNK
    e_bf, et_bf = _expand_consts()
    gw = D_INNER // N_GROUPS

    def body(dys_ref, y_ref, xs_ref, bc_ref, dtr_ref, z_ref, hp_ref, dtb_ref, alog_ref, dsk_ref, nw_ref,
             e_ref, et_ref, _, dxs_ref, dbc_ref, dz_ref, ddt_ref, acc_ref, dnw_ref, dh_ref, dx_ref):
        step = pl.program_id(0)

        @pl.when(step == 0)
        def _():
            dh_ref[...] = jnp.zeros_like(dh_ref)
            acc_ref[...] = jnp.zeros_like(acc_ref)
            dnw_ref[...] = jnp.zeros_like(dnw_ref)

        pre, dt, a_row, acs, acs_x, dt_x, causal, hm = _ssd_common(dtr_ref, dtb_ref, alog_ref, e_ref)
        acs_t = acs.T
        et = et_ref[...]
        xs = xs_ref[...]
        x_dt = xs * dt_x
        last_x = acs_x[CHUNK - 1:CHUNK, :]
        w_end = jnp.exp(last_x - acs_x)
        e_in = jnp.exp(acs_x)
        e_last = jnp.exp(last_x)
        d_x = _dot01_r(dsk_ref[...], e_ref[...])[0:1, :]

        y = y_ref[...]
        z = z_ref[...].astype(F32)
        sz = _sigmoid(z)
        gz = z * sz
        y2 = y * gz
        dys = dys_ref[...].astype(F32)
        for g in range(N_GROUPS):
            sl = slice(g * gw, (g + 1) * gw)
            yg = y2[:, sl]
            rinv = lax.rsqrt(jnp.mean(yg * yg, axis=-1, keepdims=True) + RMS_EPS)
            nrm = yg * rinv
            dn = dys[:, sl] * nw_ref[0:1, sl]
            dnw_ref[0:1, sl] += jnp.sum(dys[:, sl] * nrm, axis=0, keepdims=True)
            dx_ref[:, sl] = rinv * (dn - nrm * jnp.mean(dn * nrm, axis=-1, keepdims=True))
        dy2 = dx_ref[...]
        dy = dy2 * gz
        dz_ref[...] = (dy2 * y * (sz * (1.0 + z * (1.0 - sz)))).astype(_ACT)

        dh_next = dh_ref[...]
        hprev = hp_ref[...]
        lo, hi = _half_masks()
        r = lax.broadcasted_iota(jnp.int32, (CHUNK, CHUNK), 0)
        c = lax.broadcasted_iota(jnp.int32, (CHUNK, CHUNK), 1)
        from_here = (c >= r).astype(jnp.bfloat16)
        before = c < r
        lane = lax.broadcasted_iota(jnp.int32, (1, LANE), 1)
        da_intra = jnp.zeros((CHUNK, LANE), F32)
        v_seg = jnp.zeros((CHUNK, LANE), F32)
        z_seg = jnp.zeros((CHUNK, LANE), F32)
        tail_parts = []
        for g in range(N_GROUPS):
            bg = bc_ref[:, g * N_STATE:(g + 1) * N_STATE]
            cg = bc_ref[:, N_GROUPS * N_STATE + g * N_STATE:N_GROUPS * N_STATE + (g + 1) * N_STATE]
            sl = slice(g * gw, (g + 1) * gw)
            et_g = et_ref[g * gw:(g + 1) * gw, :]
            gm = _dot_nt(cg, bg)
            dzg = e_in[:, sl] * dy[:, sl]
            dcg = _dot_nt(dzg, hprev[:, sl])
            dh_c = _dot(cg.T, dzg)
            q = _dot(bg, dh_next[:, sl])
            dbg = _dot_nt(x_dt[:, sl] * w_end[:, sl], dh_next[:, sl])
            y_off = _dot(cg, hprev[:, sl]) * e_in[:, sl]
            v_seg = v_seg + _dot01_r(dy[:, sl] * y_off, et_g, parts=1)
            z_seg = z_seg + _dot01_r(w_end[:, sl] * q * x_dt[:, sl], et_g, parts=1)
            dgm = jnp.zeros((CHUNK, CHUNK), F32)
            for j in range(gw // LANE):
                h0 = g * (gw // HEAD_P) + 2 * j
                cs = slice(g * gw + j * LANE, g * gw + (j + 1) * LANE)
                xp = x_dt[:, cs]
                dyp = dy[:, cs]
                dxd = jnp.zeros((CHUNK, LANE), F32)
                for half, msk in ((0, lo), (1, hi)):
                    lam = _decay(acs, acs_t, h0 + half, causal)
                    mm = gm * lam
                    dym = dyp * msk
                    dmm = _dot_nt(dym, xp)
                    dxd = dxd + _dot_tn(mm, dym)
                    dgm = dgm + dmm * lam
                    below = _dot(from_here, dmm * mm)
                    col = jnp.sum(jnp.where(before, below, 0.0), axis=-1, keepdims=True)
                    da_intra = da_intra + jnp.where(lane == h0 + half, col, 0.0)
                dx_ref[:, cs] = dxd + w_end[:, cs] * q[:, j * LANE:(j + 1) * LANE]
            dbc_ref[:, N_GROUPS * N_STATE + g * N_STATE:N_GROUPS * N_STATE + (g + 1) * N_STATE] = dcg + _dot(dgm, bg)
            dbc_ref[:, g * N_STATE:(g + 1) * N_STATE] = dbg + _dot_tn(dgm, cg)
            dh_ref[:, sl] = e_last[:, sl] * dh_next[:, sl] + dh_c
            tail_parts.append(e_last[:, sl] * jnp.sum(dh_next[:, sl] * hprev[:, sl], axis=0, keepdims=True))
        dxt = dx_ref[...]

        u_seg = _dot01_r(xs * dxt, et, parts=1)
        q_full = jnp.concatenate(tail_parts, axis=1)
        t_row = _dot01_r(jnp.broadcast_to(q_full, (SUB, D_INNER)), et)[0:1, :]
        d_alpha = (da_intra + _dot01_l(from_here, v_seg) + _dot01_l(before.astype(jnp.bfloat16), z_seg) + t_row)
        d_dt = a_row * d_alpha + u_seg
        sgp = _sigmoid(pre)
        d_raw = jnp.where(hm, d_dt * sgp, 0.0)
        ddt_ref[...] = d_raw.astype(_ACT)
        acc_ref[0:1, :] += jnp.sum(d_raw, axis=0, keepdims=True)
        acc_ref[1:2, :] += jnp.sum(d_alpha * dt, axis=0, keepdims=True) * a_row
        dd_row = jnp.sum(dy * xs, axis=0, keepdims=True)
        acc_ref[2:3, :] += _dot01_r(jnp.broadcast_to(dd_row, (SUB, D_INNER)), et)[0:1, :]
        dxs_ref[...] = dy * d_x + dxt * dt_x

    rev = lambda c: (nc - 1 - c, 0)
    small = pl.BlockSpec((SUB, LANE), lambda c: (0, 0))
    return pl.pallas_call(
        body, name=name,
        out_shape=(jax.ShapeDtypeStruct((t, D_INNER), F32), jax.ShapeDtypeStruct((t, 1024), F32),
                   jax.ShapeDtypeStruct(d_proj.shape, d_proj.dtype), jax.ShapeDtypeStruct((t, LANE), _ACT),
                   jax.ShapeDtypeStruct((SUB, LANE), F32), jax.ShapeDtypeStruct((SUB, D_INNER), F32)),
        grid=(nc,),
        in_specs=[pl.BlockSpec((CHUNK, D_INNER), rev),
                  pl.BlockSpec((CHUNK, D_INNER), rev),
                  pl.BlockSpec((CHUNK, D_INNER), rev),
                  pl.BlockSpec((CHUNK, 1024), rev),
                  pl.BlockSpec((CHUNK, LANE), lambda c: (nc - 1 - c, T_DT // LANE)),
                  pl.BlockSpec((CHUNK, D_INNER), lambda c: (nc - 1 - c, P_Z // D_INNER)),
                  pl.BlockSpec((N_STATE, D_INNER), rev),
                  small, small, small,
                  pl.BlockSpec((SUB, D_INNER), lambda c: (0, 0)),
                  pl.BlockSpec((LANE, D_INNER), lambda c: (0, 0)),
                  pl.BlockSpec((D_INNER, LANE), lambda c: (0, 0)),
                  pl.BlockSpec(memory_space=pl.ANY)],
        out_specs=(pl.BlockSpec((CHUNK, D_INNER), rev),
                   pl.BlockSpec((CHUNK, 1024), rev),
                   pl.BlockSpec((CHUNK, D_INNER), lambda c: (nc - 1 - c, P_Z // D_INNER)),
                   pl.BlockSpec((CHUNK, LANE), rev),
                   small,
                   pl.BlockSpec((SUB, D_INNER), lambda c: (0, 0))),
        input_output_aliases={13: 2},
        scratch_shapes=[pltpu.VMEM((N_STATE, D_INNER), F32), pltpu.VMEM((CHUNK, D_INNER), F32)],
        compiler_params=_cp(("arbitrary",), vmem_mb=56),
    )(d_ys, y, xs_c, bc_c, tail, proj, hprev_all, dtb8, alog8, dsk8, nw8, e_bf, et_bf, d_proj)


def _rel_tables():
    qi = np.arange(WIN)[:, None] + WIN
    kj = np.arange(2 * WIN)[None, :]
    rel = qi - kj
    n = np.maximum(rel, 0)
    max_exact = REL_BUCKETS // 2
    nf = np.maximum(n, 1).astype(np.float32)
    large = max_exact + (np.log(nf / np.float32(max_exact)) / np.float32(math.log(WIN / max_exact))
                         * np.float32(REL_BUCKETS - max_exact)).astype(np.int32)
    large = np.minimum(large, REL_BUCKETS - 1)
    bucket = np.where(n < max_exact, n, large)
    valid = (rel >= 0) & (rel < WIN)
    sink_col = np.broadcast_to(kj == 0, rel.shape)
    onehot = np.zeros((BIAS_ROWS, WIN * 2 * WIN), np.float32)
    flat_b = np.where(sink_col, REL_BUCKETS, bucket).reshape(-1)
    flat_v = (valid | sink_col).reshape(-1)
    first_v = ((valid & (kj >= WIN)) | sink_col).reshape(-1)
    idx = np.arange(WIN * 2 * WIN)
    onehot[flat_b[flat_v], idx[flat_v]] = 1.0
    return onehot, np.stack([first_v, flat_v]).astype(np.float32)


def _bias_expand(table_t, name):
    onehot, valid = _rel_tables()

    def body(rb_ref, oh_ref, v_ref, o_ref):
        full = _dot01_r(rb_ref[...], oh_ref[...])
        o_ref[0] = jnp.where(v_ref[0:1, :] > 0.5, full, NEG)
        o_ref[1] = jnp.where(v_ref[1:2, :] > 0.5, full, NEG)

    return pl.pallas_call(
        body, name=name, out_shape=jax.ShapeDtypeStruct((2, A_HEADS, WIN * 2 * WIN), F32),
        compiler_params=_cp(None),
    )(table_t, jnp.asarray(onehot, jnp.bfloat16), jnp.asarray(valid, F32))


def _bias_reduce(dbias, name):
    onehot, _ = _rel_tables()

    def body(d_ref, oh_ref, o_ref):
        acc = None
        r = d_ref[...]
        for _ in range(3):
            hi = r.astype(jnp.bfloat16)
            tt = lax.dot_general(hi, oh_ref[...], (((1,), (1,)), ((), ())), preferred_element_type=F32)
            acc = tt if acc is None else acc + tt
            r = r - hi.astype(F32)
        o_ref[...] = acc

    return pl.pallas_call(
        body, name=name, out_shape=jax.ShapeDtypeStruct((A_HEADS, BIAS_ROWS), F32),
        compiler_params=_cp(None),
    )(dbias, jnp.asarray(onehot, jnp.bfloat16))


def _attn_bands(kc_ref, kp_ref, vc_ref, vp_ref, has_prev):
    lo, hi = _half_masks()
    row = lax.broadcasted_iota(jnp.int32, (2 * WIN, 1), 0)
    keep = (row > 0).astype(F32)
    kb = jnp.concatenate([jnp.where(has_prev, kp_ref[...], 0.0), kc_ref[...]], axis=0) * (keep * (A_DH ** -0.5))
    vb = jnp.concatenate([jnp.where(has_prev, vp_ref[...], 0.0), vc_ref[...]], axis=0) * keep
    kr = pltpu.roll(kb, 64, 1)
    vr = pltpu.roll(vb, 64, 1)
    kk = ((kb * lo, kr * hi), (kr * lo, kb * hi))
    vv = ((vb * lo, vr * hi), (vr * lo, vb * hi))
    return kk, vv, (hi, lo)


def _attn_logits(q_ref, kk, lg_ref):
    for h in range(A_HEADS):
        j, half, kv = h // 2, h % 2, h // (A_HEADS // 2)
        lg_ref[h] = _dot_nt(q_ref[:, j * LANE:(j + 1) * LANE], kk[kv][half])


def _attn_fwd(proj, tail, bias, name):
    t = proj.shape[0]
    nb = t // WIN

    def body(q_ref, kc_ref, kp_ref, vc_ref, vp_ref, b_ref, o_ref, lg_ref, p_ref):
        n = pl.program_id(0)
        kk, vv, ones = _attn_bands(kc_ref, kp_ref, vc_ref, vp_ref, n > 0)
        _attn_logits(q_ref, kk, lg_ref)
        for h in range(A_HEADS):
            logits = lg_ref[h] + b_ref[h]
            p_ref[h] = jnp.exp(logits - jnp.max(logits, axis=-1, keepdims=True)).astype(_MXU)
        lane = lax.broadcasted_iota(jnp.int32, (1, LANE), 1)
        for j in range(A_HEADS // 2):
            kv = (2 * j) // (A_HEADS // 2)
            outs = []
            for half in range(2):
                o = jnp.dot(p_ref[2 * j + half], (vv[kv][half] + ones[half]).astype(_MXU), preferred_element_type=F32)
                outs.append(o / pltpu.roll(o, 64, 1))
            o_ref[:, j * LANE:(j + 1) * LANE] = jnp.where(lane < 64, outs[0], outs[1]).astype(_ACT)

    kvspec = lambda col, prev: pl.BlockSpec(
        (WIN, LANE), (lambda n: (jnp.maximum(n - 1, 0), col)) if prev else (lambda n: (n, col)))
    return pl.pallas_call(
        body, name=name, out_shape=jax.ShapeDtypeStruct((t, D_MODEL), _ACT),
        grid=(nb,),
        in_specs=[pl.BlockSpec((WIN, 1024), lambda n: (n, P_Q // 1024)),
                  kvspec(T_K // LANE, False), kvspec(T_K // LANE, True),
                  kvspec(T_V // LANE, False), kvspec(T_V // LANE, True),
                  pl.BlockSpec((None, A_HEADS, WIN, 2 * WIN), lambda n: (jnp.minimum(n, 1), 0, 0, 0))],
        out_specs=pl.BlockSpec((WIN, 1024), lambda n: (n, 0)),
        scratch_shapes=[pltpu.VMEM((A_HEADS, WIN, 2 * WIN), F32), pltpu.VMEM((A_HEADS, WIN, 2 * WIN), _MXU)],
        compiler_params=_cp(("parallel",)),
    )(proj, tail, tail, tail, tail, bias)


def _attn_bwd(proj, tail, bias, y_attn, d_out, d_proj, name):
    t = proj.shape[0]
    nb = t // WIN

    def body(q_ref, kc_ref, kp_ref, vc_ref, vp_ref, b_ref, y_ref, do_ref, _,
             dq_ref, dk_ref, dv_ref, db_ref, ck_ref, cv_ref, lg_ref, dl_ref, p_ref):
        n = pl.program_id(0)

        @pl.when(n == 0)
        def _():
            db_ref[...] = jnp.zeros_like(db_ref)
            ck_ref[...] = jnp.zeros_like(ck_ref)
            cv_ref[...] = jnp.zeros_like(cv_ref)

        @pl.when(n < nb)
        def _():
            kk, vv, _ = _attn_bands(kc_ref, kp_ref, vc_ref, vp_ref, n > 0)
            lo, hi = _half_masks()
            ones_k = jnp.ones((2 * WIN, LANE), jnp.bfloat16)
            ones_d = jnp.ones((LANE, LANE), jnp.bfloat16)
            _attn_logits(q_ref, kk, lg_ref)
            for h in range(A_HEADS):
                j, half, kv = h // 2, h % 2, h // (A_HEADS // 2)
                msk = hi if half else lo
                logits = lg_ref[h] + b_ref[h]
                p = jnp.exp(logits - jnp.max(logits, axis=-1, keepdims=True))
                den = jnp.dot(p.astype(_MXU), ones_k.astype(_MXU), preferred_element_type=F32)
                dop = do_ref[:, j * LANE:(j + 1) * LANE].astype(F32)
                delta = _dot01_r(dop * y_ref[:, j * LANE:(j + 1) * LANE].astype(F32) * msk, ones_d, parts=2)
                inv = 1.0 / den
                probs = p * jnp.concatenate([inv, inv], axis=1)
                dprobs = _dot_nt(dop, vv[kv][half])
                dlog = probs * (dprobs - jnp.concatenate([delta, delta], axis=1))
                db_ref[h] += dlog
                dl_ref[h] = dlog.astype(_MXU)
                p_ref[h] = probs.astype(_MXU)
            dk_t = [[None, None], [None, None]]
            dv_t = [[None, None], [None, None]]
            for j in range(A_HEADS // 2):
                kv = (2 * j) // (A_HEADS // 2)
                qs = q_ref[:, j * LANE:(j + 1) * LANE].astype(F32) * (A_DH ** -0.5)
                dop = do_ref[:, j * LANE:(j + 1) * LANE].astype(F32)
                dq = None
                for half, msk in ((0, lo), (1, hi)):
                    h = 2 * j + half
                    dqh = jnp.dot(dl_ref[h], kk[kv][half].astype(_MXU), preferred_element_type=F32)
                    dq = dqh if dq is None else dq + dqh
                    dkh = lax.dot_general((qs * msk).astype(_MXU), dl_ref[h], (((0,), (0,)), ((), ())),
                                          preferred_element_type=F32)
                    dvh = lax.dot_general((dop * msk).astype(_MXU), p_ref[h], (((0,), (0,)), ((), ())),
                                          preferred_element_type=F32)
                    dk_t[kv][half] = dkh if dk_t[kv][half] is None else dk_t[kv][half] + dkh
                    dv_t[kv][half] = dvh if dv_t[kv][half] is None else dv_t[kv][half] + dvh
                dq_ref[:, j * LANE:(j + 1) * LANE] = dq.astype(_ACT)
            row = lax.broadcasted_iota(jnp.int32, (2 * WIN, 1), 0)

            def band(acc):
                a = (acc[0][0] + pltpu.roll(acc[0][1], 64, 0)) + (pltpu.roll(acc[1][0], 64, 0) + acc[1][1])
                return jnp.where(row > 0, a.T, 0.0)

            dkb = band(dk_t)
            dvb = band(dv_t)
            dk_ref[...] = (ck_ref[...] + dkb[0:WIN]).astype(_ACT)
            dv_ref[...] = (cv_ref[...] + dvb[0:WIN]).astype(_ACT)
            ck_ref[...] = dkb[WIN:]
            cv_ref[...] = dvb[WIN:]

        @pl.when(n == nb)
        def _():
            dk_ref[...] = ck_ref[...].astype(_ACT)
            dv_ref[...] = cv_ref[...].astype(_ACT)

    cur = lambda n: jnp.minimum(n, nb - 1)
    prv = lambda n: jnp.maximum(jnp.minimum(n, nb - 1) - 1, 0)
    kvspec = lambda col, prev: pl.BlockSpec(
        (WIN, LANE), (lambda n: (prv(n), col)) if prev else (lambda n: (cur(n), col)))
    band_shape = (A_HEADS, WIN, 2 * WIN)
    return pl.pallas_call(
        body, name=name,
        out_shape=(jax.ShapeDtypeStruct(d_proj.shape, d_proj.dtype), jax.ShapeDtypeStruct((t, LANE), _ACT),
                   jax.ShapeDtypeStruct((t, LANE), _ACT), jax.ShapeDtypeStruct(band_shape, F32)),
        grid=(nb + 1,),
        in_specs=[pl.BlockSpec((WIN, 1024), lambda n: (cur(n), P_Q // 1024)),
                  kvspec(T_K // LANE, False), kvspec(T_K // LANE, True),
                  kvspec(T_V // LANE, False), kvspec(T_V // LANE, True),
                  pl.BlockSpec((None,) + band_shape, lambda n: (jnp.minimum(n, 1), 0, 0, 0)),
                  pl.BlockSpec((WIN, 1024), lambda n: (cur(n), 0)),
                  pl.BlockSpec((WIN, 1024), lambda n: (cur(n), 0)),
                  pl.BlockSpec(memory_space=pl.ANY)],
        out_specs=(pl.BlockSpec((WIN, 1024), lambda n: (cur(n), P_Q // 1024)),
                   pl.BlockSpec((WIN, LANE), lambda n: (jnp.maximum(n - 1, 0), 0)),
                   pl.BlockSpec((WIN, LANE), lambda n: (jnp.maximum(n - 1, 0), 0)),
                   pl.BlockSpec(band_shape, lambda n: (0, 0, 0))),
        input_output_aliases={8: 0},
        scratch_shapes=[pltpu.VMEM((WIN, LANE), F32), pltpu.VMEM((WIN, LANE), F32),
                        pltpu.VMEM(band_shape, F32), pltpu.VMEM(band_shape, _MXU), pltpu.VMEM(band_shape, _MXU)],
        compiler_params=_cp(("arbitrary",)),
    )(proj, tail, tail, tail, tail, bias, y_attn, d_out, d_proj)


def _merge_fwd(bs, ba, proj, bg8, name):
    t = bs.shape[0]
    tm = _tm_rows(t)

    def body(bs_ref, ba_ref, gs_ref, ga_ref, bgs_ref, bga_ref, o_ref):
        g_s = _sigmoid(gs_ref[...] + bgs_ref[0:1, :])
        g_a = _sigmoid(ga_ref[...] + bga_ref[0:1, :])
        o_ref[...] = (g_s * bs_ref[...] + g_a * ba_ref[...]).astype(_ACT)

    row = lambda col: pl.BlockSpec((tm, 1024), lambda i: (i, col))
    return pl.pallas_call(
        body, name=name, out_shape=jax.ShapeDtypeStruct((t, D_MODEL), _ACT), grid=(t // tm,),
        in_specs=[row(0), row(0), row(P_G // 1024), row(P_G // 1024 + 1),
                  pl.BlockSpec((SUB, 1024), lambda i: (0, 0)), pl.BlockSpec((SUB, 1024), lambda i: (0, 1))],
        out_specs=row(0), compiler_params=_cp(("parallel",)),
    )(bs, ba, proj, proj, bg8, bg8)


def _merge_bwd(d_merged, bs, ba, proj, bg8, name):
    t = bs.shape[0]
    tm = _tm_rows(t)

    def body(dm_ref, bs_ref, ba_ref, gs_ref, ga_ref, bgs_ref, bga_ref, dbs_ref, dba_ref, dg_ref, acc_ref):
        @pl.when(pl.program_id(0) == 0)
        def _():
            acc_ref[...] = jnp.zeros_like(acc_ref)

        dm = dm_ref[...].astype(F32)
        g_s = _sigmoid(gs_ref[...] + bgs_ref[0:1, :])
        g_a = _sigmoid(ga_ref[...] + bga_ref[0:1, :])
        dbs_ref[...] = (dm * g_s).astype(_ACT)
        dba_ref[...] = (dm * g_a).astype(_ACT)
        dgs = dm * bs_ref[...].astype(F32) * g_s * (1.0 - g_s)
        dga = dm * ba_ref[...].astype(F32) * g_a * (1.0 - g_a)
        dg_ref[:, 0:1024] = dgs.astype(_ACT)
        dg_ref[:, 1024:2048] = dga.astype(_ACT)
        acc_ref[0:1, 0:1024] += jnp.sum(dgs, axis=0, keepdims=True)
        acc_ref[0:1, 1024:2048] += jnp.sum(dga, axis=0, keepdims=True)

    row = lambda col: pl.BlockSpec((tm, 1024), lambda i: (i, col))
    return pl.pallas_call(
        body, name=name,
        out_shape=(jax.ShapeDtypeStruct((t, D_MODEL), _ACT), jax.ShapeDtypeStruct((t, D_MODEL), _ACT),
                   jax.ShapeDtypeStruct((t, P_W), _ACT), jax.ShapeDtypeStruct((SUB, 2048), F32)),
        grid=(t // tm,),
        in_specs=[row(0), row(0), row(0), row(P_G // 1024), row(P_G // 1024 + 1),
                  pl.BlockSpec((SUB, 1024), lambda i: (0, 0)), pl.BlockSpec((SUB, 1024), lambda i: (0, 1))],
        out_specs=(row(0), row(0), pl.BlockSpec((tm, 2048), lambda i: (i, P_G // 2048)),
                   pl.BlockSpec((SUB, 2048), lambda i: (0, 0))),
        compiler_params=_cp(("arbitrary",)),
    )(d_merged, bs, ba, proj, proj, bg8, bg8)


def _place_tail(d_k, d_v, d_dt, d_proj, name):
    t = d_k.shape[0]
    tm = _tm_rows(t)
    width = P_W - P_MAIN

    def body(k_ref, v_ref, dt_ref, _, o_ref):
        o_ref[:, T_K:T_K + LANE] = k_ref[...]
        o_ref[:, T_V:T_V + LANE] = v_ref[...]
        o_ref[:, T_DT:T_DT + LANE] = dt_ref[...]
        o_ref[:, T_DT + LANE:width] = jnp.zeros((tm, width - T_DT - LANE), o_ref.dtype)

    blk = pl.BlockSpec((tm, LANE), lambda i: (i, 0))
    return pl.pallas_call(
        body, name=name, out_shape=jax.ShapeDtypeStruct(d_proj.shape, d_proj.dtype), grid=(t // tm,),
        in_specs=[blk, blk, blk, pl.BlockSpec(memory_space=pl.ANY)],
        out_specs=pl.BlockSpec((tm, width), lambda i: (i, P_MAIN // width)),
        input_output_aliases={3: 0}, compiler_params=_cp(("parallel",)),
    )(d_k, d_v, d_dt, d_proj)


def _ln_stats(r):
    mu = jnp.mean(r, axis=-1, keepdims=True)
    xc = r - mu
    var = jnp.mean(xc * xc, axis=-1, keepdims=True)
    rstd = lax.rsqrt(var + LN_EPS)
    return xc * rstd, rstd


def _ln_bwd(dxhat, xhat, rstd):
    return rstd * (dxhat - jnp.mean(dxhat, axis=-1, keepdims=True)
                   - xhat * jnp.mean(dxhat * xhat, axis=-1, keepdims=True))


def _ln1_fwd(x, mix, g8, b8, name):
    t = x.shape[0]
    tm = _tm_rows(t)

    def body(x_ref, m_ref, g_ref, b_ref, xh_ref, h_ref, rs_ref):
        xhat, rstd = _ln_stats(ALPHA * x_ref[...] + m_ref[...])
        xh_ref[...] = xhat
        h_ref[...] = (xhat * g_ref[0:1, :] + b_ref[0:1, :]).astype(_ACT)
        rs_ref[...] = rstd

    row = pl.BlockSpec((tm, D_MODEL), lambda i: (i, 0))
    par = pl.BlockSpec((SUB, D_MODEL), lambda i: (0, 0))
    return pl.pallas_call(
        body, name=name,
        out_shape=(jax.ShapeDtypeStruct((t, D_MODEL), F32), jax.ShapeDtypeStruct((t, D_MODEL), _ACT),
                   jax.ShapeDtypeStruct((t, 1), F32)),
        grid=(t // tm,), in_specs=[row, row, par, par],
        out_specs=(row, row, pl.BlockSpec((tm, 1), lambda i: (i, 0))),
        compiler_params=_cp(("parallel",)),
    )(x, mix, g8, b8)


def _ln2_loss(xhat1, ffn, target, g1_8, b1_8, g2_8, b2_8, name):
    t = xhat1.shape[0]
    tm = _tm_rows(t)

    def body(xh_ref, f_ref, t_ref, g1_ref, b1_ref, g2_ref, b2_ref, d_ref, db_ref, acc_ref):
        @pl.when(pl.program_id(0) == 0)
        def _():
            acc_ref[...] = jnp.zeros_like(acc_ref)

        h1 = xh_ref[...] * g1_ref[0:1, :] + b1_ref[0:1, :]
        xhat, rstd = _ln_stats(ALPHA * h1 + f_ref[...])
        diff = xhat * g2_ref[0:1, :] + b2_ref[0:1, :] - t_ref[...]
        dy = diff * (1.0 / D_MODEL)
        acc_ref[0:1, :] += jnp.sum(dy * xhat, axis=0, keepdims=True)
        acc_ref[1:2, :] += jnp.sum(dy, axis=0, keepdims=True)
        acc_ref[2:3, :] += jnp.sum(diff * diff, axis=0, keepdims=True)
        d = _ln_bwd(dy * g2_ref[0:1, :], xhat, rstd)
        d_ref[...] = d
        db_ref[...] = d.astype(_ACT)

    row = pl.BlockSpec((tm, D_MODEL), lambda i: (i, 0))
    par = pl.BlockSpec((SUB, D_MODEL), lambda i: (0, 0))
    return pl.pallas_call(
        body, name=name,
        out_shape=(jax.ShapeDtypeStruct((t, D_MODEL), F32), jax.ShapeDtypeStruct((t, D_MODEL), _ACT),
                   jax.ShapeDtypeStruct((SUB, D_MODEL), F32)),
        grid=(t // tm,), in_specs=[row, row, row, par, par, par, par],
        out_specs=(row, row, par), compiler_params=_cp(("arbitrary",)),
    )(xhat1, ffn, target, g1_8, b1_8, g2_8, b2_8)


def _ln1_bwd(d_r2, d_h1_ffn, xhat1, rstd1, g1_8, name):
    t = xhat1.shape[0]
    tm = _tm_rows(t)

    def body(d2_ref, df_ref, xh_ref, rs_ref, g_ref, d_ref, db_ref, acc_ref):
        @pl.when(pl.program_id(0) == 0)
        def _():
            acc_ref[...] = jnp.zeros_like(acc_ref)

        dh = ALPHA * d2_ref[...] + df_ref[...]
        xhat = xh_ref[...]
        acc_ref[0:1, :] += jnp.sum(dh * xhat, axis=0, keepdims=True)
        acc_ref[1:2, :] += jnp.sum(dh, axis=0, keepdims=True)
        d = _ln_bwd(dh * g_ref[0:1, :], xhat, rs_ref[...])
        d_ref[...] = d
        db_ref[...] = d.astype(_ACT)

    row = pl.BlockSpec((tm, D_MODEL), lambda i: (i, 0))
    par = pl.BlockSpec((SUB, D_MODEL), lambda i: (0, 0))
    return pl.pallas_call(
        body, name=name,
        out_shape=(jax.ShapeDtypeStruct((t, D_MODEL), F32), jax.ShapeDtypeStruct((t, D_MODEL), _ACT),
                   jax.ShapeDtypeStruct((SUB, D_MODEL), F32)),
        grid=(t // tm,), in_specs=[row, row, row, pl.BlockSpec((tm, 1), lambda i: (i, 0)), par],
        out_specs=(row, row, par), compiler_params=_cp(("arbitrary",)),
    )(d_r2, d_h1_ffn, xhat1, rstd1, g1_8)


def _ffn_tm(t):
    return min(256, t)


def _ffn_act_fwd(u0, cw8, cb8, name):
    t = u0.shape[0]
    tm = _ffn_tm(t)

    def body(g_ref, gp_ref, v_ref, vp_ref, wg_ref, wv_ref, bg_ref, bv_ref, o_ref, u_ref):
        i = pl.program_id(0)
        gprev = jnp.where(i > 0, gp_ref[SUB:HALO, :].astype(F32), 0.0)
        vprev = jnp.where(i > 0, vp_ref[SUB:HALO, :].astype(F32), 0.0)
        gate = _conv_pre(g_ref[...].astype(F32), gprev, wg_ref, bg_ref[0:1, :], FFN_K)
        val = _conv_pre(v_ref[...].astype(F32), vprev, wv_ref, bv_ref[0:1, :], FFN_K)
        o_ref[...] = (gate * _sigmoid(gate) * val).astype(_ACT)
        u_ref[:, 0:D_FF] = gate.astype(_ACT)
        u_ref[:, D_FF:2 * D_FF] = val.astype(_ACT)

    cur = lambda col: pl.BlockSpec((tm, D_FF), lambda i: (i, col))
    prv = lambda col: _prev_halo(tm, D_FF, lambda i: (i, col))
    par = lambda col: pl.BlockSpec((SUB, D_FF), lambda i: (0, col))
    return pl.pallas_call(
        body, name=name,
        out_shape=(jax.ShapeDtypeStruct((t, D_FF), _ACT), jax.ShapeDtypeStruct((t, 2 * D_FF), _ACT)),
        grid=(t // tm,),
        in_specs=[cur(0), prv(0), cur(1), prv(1), par(0), par(1), par(0), par(1)],
        out_specs=(pl.BlockSpec((tm, D_FF), lambda i: (i, 0)), pl.BlockSpec((tm, 2 * D_FF), lambda i: (i, 0))),
        compiler_params=_cp(("parallel",)),
    )(u0, u0, u0, u0, cw8, cw8, cb8, cb8)


def _ffn_act_bwd(u0, u, cw8, d_a, name):
    t = u0.shape[0]
    tm = _ffn_tm(t)
    nt = t // tm

    def body(g0_ref, v0_ref, g_ref, gn_ref, v_ref, vn_ref, wg_ref, wv_ref, da_ref, dan_ref, du_ref, acc_ref):
        i = pl.program_id(0)

        @pl.when(i == 0)
        def _():
            acc_ref[...] = jnp.zeros_like(acc_ref)

        def grads(gate, val, da):
            return da * val * _silu_grad(gate), da * gate * _sigmoid(gate)

        dgate, dval = grads(g_ref[...].astype(F32), v_ref[...].astype(F32), da_ref[...].astype(F32))
        dgate_n, dval_n = grads(gn_ref[0:SUB, :].astype(F32), vn_ref[0:SUB, :].astype(F32),
                                dan_ref[0:SUB, :].astype(F32))
        last = i == nt - 1
        du_ref[:, 0:D_FF] = _conv_grads(dgate, jnp.where(last, 0.0, dgate_n), g0_ref[...].astype(F32), wg_ref,
                                        acc_ref, FFN_K, slice(0, D_FF)).astype(_ACT)
        du_ref[:, D_FF:2 * D_FF] = _conv_grads(dval, jnp.where(last, 0.0, dval_n), v0_ref[...].astype(F32), wv_ref,
                                               acc_ref, FFN_K, slice(D_FF, 2 * D_FF)).astype(_ACT)

    cur = lambda col: pl.BlockSpec((tm, D_FF), lambda i: (i, col))
    nxt = lambda col: _next_halo(tm, t, D_FF, lambda i: (i, col))
    par = lambda col: pl.BlockSpec((SUB, D_FF), lambda i: (0, col))
    return pl.pallas_call(
        body, name=name,
        out_shape=(jax.ShapeDtypeStruct((t, 2 * D_FF), _ACT), jax.ShapeDtypeStruct((SUB, 2 * D_FF), F32)),
        grid=(nt,),
        in_specs=[cur(0), cur(1), cur(0), nxt(0), cur(1), nxt(1), par(0), par(1), cur(0), nxt(0)],
        out_specs=(pl.BlockSpec((tm, 2 * D_FF), lambda i: (i, 0)),
                   pl.BlockSpec((SUB, 2 * D_FF), lambda i: (0, 0))),
        compiler_params=_cp(("arbitrary",)),
    )(u0, u0, u, u, u, u, cw8, cw8, d_a, d_a)


_REST = ("w_branch_ssm", "w_branch_attn", "w_mix_out", "w_up", "w_down")


def _mm_side(*args, side, **kw):
    if side is None:
        return _mm(*args, **kw), []
    return _mm(*args, side=side, **kw)


def _local_step(x, x_bf, target, wts, ex):
    t = x.shape[0]
    wp = wts["wp"]
    scw = wts["ssm_conv_w"]
    scb = wts["ssm_conv_b"]
    fcw8 = _rows8(wts["ffn_conv_w"])
    fcb8 = _rows8(wts["ffn_conv_b"])
    pad_lane = lambda p: jnp.concatenate([p.astype(F32), jnp.zeros((1, LANE - p.shape[1]), F32)], axis=1)
    dtb8 = _rows8(pad_lane(wts["ssm_dt_bias"]))
    alog8 = _rows8(pad_lane(wts["ssm_a_log"]))
    dsk8 = _rows8(pad_lane(wts["ssm_d"]))
    bias_table = jnp.concatenate([wts["rel_bias"].T.astype(F32), wts["attn_sinks"].T.astype(F32),
                                  jnp.zeros((A_HEADS, BIAS_ROWS - REL_BUCKETS - 1), F32)], axis=1)
    nw8 = _rows8(wts["ssm_norm_w"])
    bg8 = _rows8(wts["b_gate"])
    g1_8, b1_8, g2_8, b2_8 = (_rows8(wts[k]) for k in ("ln1_g", "ln1_b", "ln2_g", "ln2_b"))
    xs_w8, xs_b8 = _rows8(scw[:, :D_INNER]), _rows8(scb[:, :D_INNER])
    bc_w8, bc_b8 = _rows8(scw[:, D_INNER:]), _rows8(scb[:, D_INNER:])

    proj, stacks = _mm_side(x_bf, wp[:, :P_MAIN], "mm_in", out_dtype=_ACT, side=ex.gather_rest())
    wts = dict(wts, **ex.rest_weights(stacks))
    w_bs, w_ba, w_mix, w_up, w_dn = (wts[k] for k in _REST)
    tail = _mm(x_bf, wp[:, P_MAIN:], "mm_in_tail")
    xs_c, xs_pre = _conv_silu_fwd(proj, P_XS // _TC, D_INNER // _TC, xs_w8, xs_b8, "conv_xs_fwd")
    bc_c, bc_pre = _conv_silu_fwd(proj, P_BC // _TC, 1024 // _TC, bc_w8, bc_b8, "conv_bc_fwd")
    y_ssd, y_ssm, hprev = _ssd_fwd(xs_c, bc_c, proj, tail, dtb8, alog8, dsk8, nw8, "ssd_fwd")
    bias = _bias_expand(bias_table, "bias_expand").reshape(2, A_HEADS, WIN, 2 * WIN)
    y_attn = _attn_fwd(proj, tail, bias, "attn_fwd")
    bs = _mm(y_ssm, w_bs, "mm_bs", out_dtype=_ACT)
    ba = _mm(y_attn, w_ba, "mm_ba", out_dtype=_ACT)
    merged = _merge_fwd(bs, ba, proj, bg8, "merge_fwd")
    mix = _mm(merged, w_mix, "mm_mix", out_dtype=_ACT)
    xhat1, h1_bf, rstd1 = _ln1_fwd(x, mix, g1_8, b1_8, "ln1_fwd")
    u0 = _mm(h1_bf, w_up, "mm_up", out_dtype=_ACT)
    act, u_conv = _ffn_act_fwd(u0, fcw8, fcb8, "ffn_act_fwd")
    ffn = _mm(act, w_dn, "mm_down", out_dtype=_ACT)
    d_r2, d_r2_bf, acc_ln2 = _ln2_loss(xhat1, ffn, target, g1_8, b1_8, g2_8, b2_8, "ln2_loss")
    d_w_dn = _mm(act, d_r2_bf, "mm_dw_down", trans_a=True)
    d_act = _mm(d_r2_bf, w_dn.T, "mm_d_act", out_dtype=_ACT)
    d_u0, acc_ffn = _ffn_act_bwd(u0, u_conv, fcw8, d_act, "ffn_act_bwd")
    d_w_up = _mm(h1_bf, d_u0, "mm_dw_up", trans_a=True)
    d_h1_ffn = _mm(d_u0, w_up.T, "mm_d_h1", out_dtype=_ACT)
    d_r1, d_r1_bf, acc_ln1 = _ln1_bwd(d_r2, d_h1_ffn, xhat1, rstd1, g1_8, "ln1_bwd")
    d_w_mix = _mm(merged, d_r1_bf, "mm_dw_mix", trans_a=True)
    d_merged = _mm(d_r1_bf, w_mix.T, "mm_d_merged", out_dtype=_ACT)
    d_bs, d_ba, d_proj, acc_bg = _merge_bwd(d_merged, bs, ba, proj, bg8, "merge_bwd")
    d_w_bs = _mm(y_ssm, d_bs, "mm_dw_bs", trans_a=True)
    d_w_ba = _mm(y_attn, d_ba, "mm_dw_ba", trans_a=True)
    d_y_ssm = _mm(d_bs, w_bs.T, "mm_d_yssm", out_dtype=_ACT)
    d_y_attn = _mm(d_ba, w_ba.T, "mm_d_yattn", out_dtype=_ACT)
    d_proj, d_k, d_v, d_bias = _attn_bwd(proj, tail, bias, y_attn, d_y_attn, d_proj, "attn_bwd")
    d_table = _bias_reduce(d_bias.reshape(A_HEADS, WIN * 2 * WIN), "bias_reduce")
    d_xs_c, d_bc_c, d_proj, d_dt, acc_ssd, acc_nw = _ssd_bwd(
        d_y_ssm, y_ssd, xs_c, bc_c, proj, tail, hprev, dtb8, alog8, dsk8, nw8, d_proj, "ssd_bwd")
    d_proj, acc_xs = _conv_silu_bwd(proj, P_XS // _TC, D_INNER // _TC, xs_pre, xs_w8, d_xs_c, d_proj, "conv_xs_bwd")
    d_proj, acc_bc = _conv_silu_bwd(proj, P_BC // _TC, 1024 // _TC, bc_pre, bc_w8, d_bc_c, d_proj, "conv_bc_bwd")
    d_proj = _place_tail(d_k, d_v, d_dt, d_proj, "place_tail")
    grads = {"w_branch_ssm": d_w_bs, "w_branch_attn": d_w_ba, "w_mix_out": d_w_mix, "w_up": d_w_up, "w_down": d_w_dn}
    d_wp, landed_rest = _mm_side(x_bf, d_proj, "mm_dw_in", trans_a=True, side=ex.reduce_job(grads))
    d_x, landed_in = _mm_side(d_proj, wp.T, "mm_d_x", res=d_r1, res_scale=ALPHA, side=ex.reduce_job({"wp": d_wp}))
    grads.update({
        "wp": d_wp,
        "ssm_conv_w": jnp.concatenate([acc_xs[0:SSM_K], acc_bc[0:SSM_K]], axis=1),
        "ffn_conv_w": acc_ffn[0:FFN_K],
    })
    small = {
        "rel_bias": d_table[:, 0:REL_BUCKETS].T,
        "b_gate": acc_bg[0:1],
        "ssm_conv_b": jnp.concatenate([acc_xs[SSM_K:SSM_K + 1], acc_bc[SSM_K:SSM_K + 1]], axis=1),
        "ssm_dt_bias": acc_ssd[0:1, 0:N_HEADS], "ssm_a_log": acc_ssd[1:2, 0:N_HEADS], "ssm_d": acc_ssd[2:3, 0:N_HEADS],
        "ssm_norm_w": acc_nw[0:1],
        "attn_sinks": d_table[:, REL_BUCKETS:REL_BUCKETS + 1].T,
        "ln1_g": acc_ln1[0:1], "ln1_b": acc_ln1[1:2],
        "ffn_conv_b": acc_ffn[FFN_K:FFN_K + 1],
        "ln2_g": acc_ln2[0:1], "ln2_b": acc_ln2[1:2],
        "loss_lanes": acc_ln2[2:3],
    }
    return d_x, grads, small, landed_rest + landed_in


_MATS = (("w_in", (1024, 2120), 1), ("w_branch_ssm", (512, 1024), 0), ("w_branch_attn", (256, 1024), 0),
         ("w_mix_out", (256, 1024), 0), ("w_up", (1024, 1408), 1), ("w_down", (704, 1024), 0))
_CONVS = (("ssm_conv_w", (4, 768)), ("ffn_conv_w", (3, 1408)))
_CONV_ROWS = 64

_SMALL = (("rel_bias", (32, 16)), ("b_gate", (1, 2048)), ("ssm_conv_b", (1, 3072)), ("ssm_dt_bias", (1, 32)),
          ("ssm_a_log", (1, 32)), ("ssm_d", (1, 32)), ("ssm_norm_w", (1, 2048)), ("attn_sinks", (1, 16)),
          ("ln1_g", (1, 1024)), ("ln1_b", (1, 1024)), ("ffn_conv_b", (1, 5632)), ("ln2_g", (1, 1024)),
          ("ln2_b", (1, 1024)), ("g_ssm_conv_w", (4, 3072)), ("g_ffn_conv_w", (3, 5632)), ("loss_lanes", (1, 1024)))


def _small_rows(shape):
    rows = -(-(shape[0] * shape[1]) // LANE)
    return -(-rows // SUB) * SUB


def _as_rows(a, rows, dtype):
    flat = a.reshape(-1).astype(dtype)
    flat = jnp.concatenate([flat, jnp.zeros((rows * LANE - flat.shape[0],), dtype)])
    return flat.reshape(rows, LANE)


def _pack_small(parts):
    blocks = [_as_rows(parts[n], _small_rows(s), F32) if n in parts else jnp.zeros((_small_rows(s), LANE), F32)
              for n, s in _SMALL]
    return jnp.concatenate(blocks, axis=0)


def _unpack_small(packed):
    out, at = {}, 0
    for n, s in _SMALL:
        rows = _small_rows(s)
        out[n] = packed[at:at + rows].reshape(-1)[:s[0] * s[1]].reshape(s)
        at += rows
    return out


def _to_stack(full, shape, axis):
    if axis == 0:
        return full.reshape((N_CHIPS,) + shape)
    return jnp.transpose(full.reshape(shape[0], N_CHIPS, shape[1]), (1, 0, 2))


def _from_stack(stack, axis):
    n, r, c = stack.shape
    if axis == 0:
        return stack.reshape(n * r, c)
    return jnp.transpose(stack, (1, 0, 2)).reshape(r, n * c)


_IN_SHARD = IN_COLS // N_CHIPS


def _cols_of_stack(stack, o, w):
    parts = []
    while w > 0:
        j, a = divmod(o, _IN_SHARD)
        n = min(w, _IN_SHARD - a)
        parts.append(stack[j][:, a:a + n])
        o, w = o + n, w - n
    return parts


def _pack_w_in_stack(stack):
    cols, at = [], 0
    for o, w, pk in sorted(_PIECES, key=lambda p: p[2]):
        if pk > at:
            cols.append(jnp.zeros((stack.shape[1], pk - at), stack.dtype))
        cols += _cols_of_stack(stack, o, w)
        at = pk + w
    cols.append(jnp.zeros((stack.shape[1], P_W - at), stack.dtype))
    return jnp.concatenate(cols, axis=1)


def _unpack_w_in_stack(wp):
    slabs = []
    for j in range(N_CHIPS):
        lo, hi = j * _IN_SHARD, (j + 1) * _IN_SHARD
        cols = []
        for o, w, pk in sorted(_PIECES):
            a, b = max(o, lo), min(o + w, hi)
            if a < b:
                cols.append(wp[:, pk + a - o:pk + b - o])
        slabs.append(jnp.concatenate(cols, axis=1))
    return jnp.stack(slabs)


_MESH = pl.DeviceIdType.MESH
_HBM = pl.BlockSpec(memory_space=pltpu.HBM)


def _position():
    return lax.axis_index("x"), lax.axis_index("y"), lax.axis_index("c")


def _other_chips(x, y):
    return ((1 - x, y), (x, 1 - y), (1 - x, 1 - y))


def _remote(src, dst, send_sem, recv_sem, to):
    return pltpu.make_async_remote_copy(src_ref=src, dst_ref=dst, send_sem=send_sem, recv_sem=recv_sem,
                                        device_id=to, device_id_type=_MESH)


def _cast_rows(x, dtype, name, side):
    t, cols = x.shape
    tm = min(1024, t)
    nt = t // tm
    ns_in, ns_out = len(side.inputs), len(side.out_shape)

    def body(*refs):
        x_ref, o_ref = refs[0], refs[1 + ns_in]
        job_refs = (refs[1:1 + ns_in], refs[2 + ns_in:2 + ns_in + ns_out], refs[2 + ns_in + ns_out:])
        i = pl.program_id(0)

        @pl.when(i == 0)
        def _():
            side.start(*job_refs)

        o_ref[...] = x_ref[...].astype(dtype)

        @pl.when(i == nt - 1)
        def _():
            side.finish(*job_refs)

    blk = pl.BlockSpec((tm, cols), lambda i: (i, 0))
    outs = pl.pallas_call(
        body, name=name, out_shape=[jax.ShapeDtypeStruct((t, cols), dtype)] + list(side.out_shape), grid=(nt,),
        in_specs=[blk] + [_HBM] * ns_in, out_specs=[blk] + [_HBM] * ns_out, scratch_shapes=list(side.sems),
        compiler_params=_cp(("arbitrary",)),
    )(x, *side.inputs)
    return outs[0], list(outs[1:])


_GATHER_COPIES = 7


def _gather_job(shards):
    n = len(shards)

    def plan(s_refs, o_refs, sems):
        send_sems, recv_sems = sems
        x, y, c = _position()
        me = 2 * x + y
        sib = (x, y, 1 - c)
        chips = _other_chips(x, y)

        def copy(m, k, dst, to, src=None):
            at = _GATHER_COPIES * m + k
            return _remote(dst if src is None else src, dst, send_sems.at[at], recv_sems.at[at], to)

        def half(m, chip_idx, h):
            return o_refs[m].at[chip_idx, h]

        own = [copy(m, 6, o_refs[m].at[me], sib, src=s_refs[m]) for m in range(n)]
        first = [copy(m, i, half(m, me, c), (cx, cy, c), src=s_refs[m].at[c])
                 for i, (cx, cy) in enumerate(chips) for m in range(n)]
        return c, sib, chips, copy, half, own, first

    def start(s_refs, o_refs, sems):
        _, _, _, _, _, own, first = plan(s_refs, o_refs, sems)
        for cp in first + own:
            cp.start()

    def finish(s_refs, o_refs, sems):
        c, sib, chips, copy, half, own, first = plan(s_refs, o_refs, sems)
        passed = []
        for i, (cx, cy) in enumerate(chips):
            for m in range(n):
                copy(m, i, half(m, 2 * cx + cy, c), sib).wait_recv()
                passed.append(copy(m, 3 + i, half(m, 2 * cx + cy, c), sib))
                passed[-1].start()
        for i, (cx, cy) in enumerate(chips):
            for m in range(n):
                copy(m, 3 + i, half(m, 2 * cx + cy, 1 - c), sib).wait_recv()
        for cp in first + passed:
            cp.wait_send()
        for cp in own:
            cp.wait()

    return _SideJob(
        inputs=list(shards), out_shape=[jax.ShapeDtypeStruct((N_CHIPS,) + s.shape, s.dtype) for s in shards],
        sems=[pltpu.SemaphoreType.DMA((_GATHER_COPIES * n,)), pltpu.SemaphoreType.DMA((_GATHER_COPIES * n,))],
        start=start, finish=finish)


def _swap_halves(gs, name):
    n = len(gs)

    def body(*refs):
        g_refs, o_refs = refs[:n], refs[n:2 * n]
        send_sems, recv_sems = refs[2 * n:]
        x, y, c = _position()
        cps = [_remote(g_refs[m].at[j, 1 - c], o_refs[m].at[j], send_sems.at[N_CHIPS * m + j],
                       recv_sems.at[N_CHIPS * m + j], (x, y, 1 - c)) for m in range(n) for j in range(N_CHIPS)]
        for cp in cps:
            cp.start()
        for cp in cps:
            cp.wait()

    return pl.pallas_call(
        body, name=name,
        out_shape=[jax.ShapeDtypeStruct((N_CHIPS,) + g.shape[2:], g.dtype) for g in gs],
        in_specs=[_HBM] * n, out_specs=[_HBM] * n,
        scratch_shapes=[pltpu.SemaphoreType.DMA((N_CHIPS * n,)), pltpu.SemaphoreType.DMA((N_CHIPS * n,))],
    )(*gs)


def _scatter_job(ps):
    n = len(ps)

    def copies(p_refs, o_refs, sems):
        send_sems, recv_sems = sems
        x, y, c = _position()
        return [_remote(p_refs[m].at[2 * cx + cy], o_refs[m].at[i], send_sems.at[3 * m + i], recv_sems.at[3 * m + i],
                        (cx, cy, c)) for i, (cx, cy) in enumerate(_other_chips(x, y)) for m in range(n)]

    def start(*parts):
        for cp in copies(*parts):
            cp.start()

    def finish(*parts):
        for cp in copies(*parts):
            cp.wait()

    return _SideJob(
        inputs=list(ps), out_shape=[jax.ShapeDtypeStruct((N_CHIPS - 1,) + p.shape[1:], p.dtype) for p in ps],
        sems=[pltpu.SemaphoreType.DMA((3 * n,)), pltpu.SemaphoreType.DMA((3 * n,))], start=start, finish=finish)


def _join_halves(fulls):
    n = len(fulls)

    def body(*refs):
        o_refs = refs[n:2 * n]
        send_sems, recv_sems = refs[2 * n:]
        x, y, c = _position()
        cps = [_remote(o_refs[m].at[c], o_refs[m].at[c], send_sems.at[m], recv_sems.at[m], (x, y, 1 - c))
               for m in range(n)]
        for cp in cps:
            cp.start()
        for cp in cps:
            cp.wait()

    return pl.pallas_call(
        body, name="join_halves",
        out_shape=[jax.ShapeDtypeStruct(f.shape, f.dtype) for f in fulls],
        in_specs=[_HBM] * n, out_specs=[_HBM] * n, input_output_aliases={m: m for m in range(n)},
        scratch_shapes=[pltpu.SemaphoreType.DMA((n,)), pltpu.SemaphoreType.DMA((n,))],
    )(*fulls)


def _allgather_small(mine, name):
    m_per, n = mine.shape

    def body(x_ref, out_ref, send_sems, recv_sems, local_sem):
        x, y, c = _position()
        me, sibling = (x, y, c), (x, y, 1 - c)
        chips = _other_chips(x, y)

        def rows(px, py, pc):
            return out_ref.at[pl.ds((4 * px + 2 * py + pc) * m_per, m_per), :]

        def copy(k, block, to, src=None):
            return pltpu.make_async_remote_copy(src_ref=rows(*block) if src is None else src, dst_ref=rows(*block),
                                                send_sem=send_sems.at[k], recv_sem=recv_sems.at[k],
                                                device_id=to, device_id_type=_MESH)

        own = pltpu.make_async_copy(x_ref, rows(*me), local_sem)
        own.start()
        first = [copy(0, me, sibling, src=x_ref)]
        first += [copy(1 + j, me, (*chip, c), src=x_ref) for j, chip in enumerate(chips)]
        for cp in first:
            cp.start()
        passed = [copy(4 + j, (*chip, c), sibling) for j, chip in enumerate(chips)]
        for j, chip in enumerate(chips):
            copy(1 + j, (*chip, c), me).wait_recv()
            passed[j].start()
        copy(0, sibling, me).wait_recv()
        for j, chip in enumerate(chips):
            copy(4 + j, (*chip, 1 - c), me).wait_recv()
        for cp in first + passed:
            cp.wait_send()
        own.wait()

    return pl.pallas_call(
        body, name=name, out_shape=jax.ShapeDtypeStruct((N_DEV * m_per, n), mine.dtype),
        in_specs=[pl.BlockSpec(memory_space=pltpu.VMEM)], out_specs=pl.BlockSpec(memory_space=pltpu.VMEM),
        scratch_shapes=[pltpu.SemaphoreType.DMA((7,)), pltpu.SemaphoreType.DMA((7,)), pltpu.SemaphoreType.DMA],
    )(mine)


_ADD_BLOCK_BYTES = 3 << 20


def _add_rows(hr, cols):
    if hr * cols * 4 <= _ADD_BLOCK_BYTES:
        return hr
    return _pick(hr, (256, 128, 64, 32, 16))


def _add_own_half(g, recv, c_idx, name):
    nseg, _, hr, cols = g.shape
    tr = _add_rows(hr, cols)

    def body(c_ref, g_ref, r_ref, o_ref, ob_ref):
        s = g_ref[...] + r_ref[...]
        o_ref[...] = s
        ob_ref[...] = s.astype(jnp.bfloat16)

    blk = pl.BlockSpec((None, tr, cols), lambda j, i, c_ref: (j, i, 0))
    return pl.pallas_call(
        body, name=name,
        out_shape=(jax.ShapeDtypeStruct((nseg, hr, cols), F32), jax.ShapeDtypeStruct((nseg, hr, cols), jnp.bfloat16)),
        grid_spec=pltpu.PrefetchScalarGridSpec(
            num_scalar_prefetch=1, grid=(nseg, hr // tr),
            in_specs=[pl.BlockSpec((None, None, tr, cols), lambda j, i, c_ref: (j, c_ref[0], i, 0)), blk],
            out_specs=(blk, blk)),
        compiler_params=_cp(("parallel", "parallel")),
    )(c_idx, g, recv)


def _add_chips(p, recv, chip_idx, c_idx, name):
    _, hr, cols = p.shape
    tr = _add_rows(hr, cols)

    def body(j_ref, c_ref, p_ref, r_ref, o_ref):
        o_ref[...] = ((p_ref[...] + r_ref[0].astype(F32)) + r_ref[1].astype(F32)) + r_ref[2].astype(F32)

    return pl.pallas_call(
        body, name=name, out_shape=jax.ShapeDtypeStruct((2, hr, cols), F32),
        grid_spec=pltpu.PrefetchScalarGridSpec(
            num_scalar_prefetch=2, grid=(hr // tr,),
            in_specs=[pl.BlockSpec((None, tr, cols), lambda i, j_ref, c_ref: (j_ref[0], i, 0)),
                      pl.BlockSpec((N_CHIPS - 1, tr, cols), lambda i, j_ref, c_ref: (0, i, 0))],
            out_specs=pl.BlockSpec((None, tr, cols), lambda i, j_ref, c_ref: (c_ref[0], i, 0))),
        compiler_params=_cp(("parallel",)),
    )(chip_idx, c_idx, p, recv)


def _adam_math(w, g, m, v):
    m = ADAM_B1 * m + (1.0 - ADAM_B1) * g
    v = ADAM_B2 * v + (1.0 - ADAM_B2) * (g * g)
    m_hat = m / (1.0 - ADAM_B1 ** ADAM_STEP)
    v_hat = v / (1.0 - ADAM_B2 ** ADAM_STEP)
    delta = -ADAM_LR * (m_hat / (jnp.sqrt(v_hat) + ADAM_EPS) + ADAM_WD * w)
    return delta, m, v


def _adam_big(w, g, m, v, name):
    rows, cols = w.shape
    tr = _pick(rows, (256, 128, 64, 32, 16, 8)) if rows % SUB == 0 else rows

    def body(w_ref, g_ref, m_ref, v_ref, d_ref, mo_ref, vo_ref):
        d_ref[...], mo_ref[...], vo_ref[...] = _adam_math(w_ref[...], g_ref[...], m_ref[...], v_ref[...])

    blk = pl.BlockSpec((tr, cols), lambda i: (i, 0))
    shp = jax.ShapeDtypeStruct((rows, cols), F32)
    return pl.pallas_call(
        body, name=name, out_shape=(shp, shp, shp), grid=(rows // tr,),
        in_specs=[blk, blk, blk, blk], out_specs=(blk, blk, blk), compiler_params=_cp(("parallel",)),
    )(w, g, m, v)


def _adam_small(w, gathered, m, v):
    rows = w.shape[0]

    def body(w_ref, a_ref, m_ref, v_ref, g_ref, d_ref, mo_ref, vo_ref):
        g = a_ref[0:rows, :]
        for k in range(1, N_DEV):
            g = g + a_ref[k * rows:(k + 1) * rows, :]
        g_ref[...] = g
        d_ref[...], mo_ref[...], vo_ref[...] = _adam_math(w_ref[...], g, m_ref[...], v_ref[...])

    shp = jax.ShapeDtypeStruct((rows, LANE), F32)
    return pl.pallas_call(body, name="adam_small", out_shape=(shp, shp, shp, shp), compiler_params=_cp(None))(
        w, gathered, m, v)


_WEIGHTS = ("rel_bias", "w_in", "b_gate", "ssm_conv_w", "ssm_conv_b", "ssm_dt_bias", "ssm_a_log", "ssm_d",
            "ssm_norm_w", "attn_sinks", "w_branch_ssm", "w_branch_attn", "w_mix_out", "ln1_g", "ln1_b", "w_up",
            "ffn_conv_w", "ffn_conv_b", "w_down", "ln2_g", "ln2_b")
_REPLICATED = tuple(n for n, _ in _SMALL[:13])


class _Exchange:
    def __init__(self, w, chip, core):
        self.chip = chip
        self.c_idx = jnp.reshape(core, (1,)).astype(jnp.int32)
        self.chip_idx = jnp.reshape(chip, (1,)).astype(jnp.int32)
        self.shards = {n: w[n].astype(jnp.bfloat16).reshape(2, s[0] // 2, s[1]) for n, s, _ in _MATS}
        self.spec = {n: (s, ax) for n, s, ax in _MATS}
        self.sums = {}

    def cast_and_gather_w_in(self, x):
        x_act, (stack,) = _cast_rows(x, _ACT, "cast_x", _gather_job([self.shards["w_in"]]))
        return x_act, _pack_w_in_stack(stack.reshape((N_CHIPS,) + self.spec["w_in"][0]))

    def gather_rest(self):
        return _gather_job([self.shards[n] for n in _REST])

    def rest_weights(self, stacks):
        return {n: _from_stack(st.reshape((N_CHIPS,) + self.spec[n][0]), self.spec[n][1])
                for n, st in zip(_REST, stacks)}

    def reduce_job(self, grads):
        names, stacks = [], []
        for n, g in grads.items():
            name = "w_in" if n == "wp" else n
            s, ax = self.spec[name]
            st = _unpack_w_in_stack(g) if n == "wp" else _to_stack(g, s, ax)
            names.append(name)
            stacks.append(st.reshape(N_CHIPS, 2, s[0] // 2, s[1]))
        swapped = _swap_halves(stacks, "swap_" + names[0])
        halves = []
        for n, g, r in zip(names, stacks, swapped):
            self.sums[n], bf = _add_own_half(g, r, self.c_idx, "add_own_" + n)
            halves.append(bf)
        return _scatter_job(halves)

    def reduced(self, landed):
        names = list(self.sums)
        reds = [_add_chips(self.sums[n], r, self.chip_idx, self.c_idx, "add_chips_" + n)
                for n, r in zip(names, landed)]
        return {n: g.reshape(self.spec[n][0]) for n, g in zip(names, _join_halves(reds))}


def _step(x, target, w, m, v):
    xi, yi, ci = _position()
    chip = 2 * xi + yi
    ex = _Exchange(w, chip, ci)

    wts = {n: w[n] for n in _REPLICATED}
    x_act, wts["wp"] = ex.cast_and_gather_w_in(x)
    taps = jnp.concatenate([w[n].astype(F32).reshape(-1) for n, _ in _CONVS])
    taps = _allgather_small(_as_rows(taps, _CONV_ROWS, F32), "allgather_taps")
    taps = taps.reshape(N_CHIPS, 2, _CONV_ROWS * LANE)[:, 0]
    at = 0
    for n, s in _CONVS:
        wts[n] = _from_stack(taps[:, at:at + s[0] * s[1]].reshape((N_CHIPS,) + s), 1)
        at += s[0] * s[1]

    d_x, grads, small, landed = _local_step(x, x_act, target, wts, ex)

    outs = {"grad": ex.reduced(landed), "delta": {}, "m": {}, "v": {}}

    small = dict(small, g_ssm_conv_w=grads["ssm_conv_w"], g_ffn_conv_w=grads["ffn_conv_w"])
    all_small = _allgather_small(_pack_small(small), "allgather_small")
    packs = [_pack_small({n: d[n] for n in _REPLICATED}) for d in (w, m, v)]
    g_s, d_s, m_s, v_s = (_unpack_small(a) for a in _adam_small(packs[0], all_small, packs[1], packs[2]))
    for kind, part in (("grad", g_s), ("delta", d_s), ("m", m_s), ("v", v_s)):
        outs[kind].update({n: part[n] for n in _REPLICATED})
    for n, s in _CONVS:
        outs["grad"][n] = lax.dynamic_slice_in_dim(g_s["g_" + n], chip * s[1], s[1], axis=1)
    for n in [n for n, _, _ in _MATS] + [n for n, _ in _CONVS]:
        outs["delta"][n], outs["m"][n], outs["v"][n] = _adam_big(
            w[n].astype(F32), outs["grad"][n], m[n].astype(F32), v[n].astype(F32), "adam_" + n)
    loss = (0.5 / D_MODEL) * jnp.sum(g_s["loss_lanes"])
    return loss, d_x, outs


def kernel(x, rel_bias, w_in, b_gate, ssm_conv_w, ssm_conv_b, ssm_dt_bias, ssm_a_log, ssm_d, ssm_norm_w, attn_sinks, w_branch_ssm, w_branch_attn, w_mix_out, ln1_g, ln1_b, w_up, ffn_conv_w, ffn_conv_b, w_down, ln2_g, ln2_b, loss_target, m_rel_bias, m_w_in, m_b_gate, m_ssm_conv_w, m_ssm_conv_b, m_ssm_dt_bias, m_ssm_a_log, m_ssm_d, m_ssm_norm_w, m_attn_sinks, m_w_branch_ssm, m_w_branch_attn, m_w_mix_out, m_ln1_g, m_ln1_b, m_w_up, m_ffn_conv_w, m_ffn_conv_b, m_w_down, m_ln2_g, m_ln2_b, v_rel_bias, v_w_in, v_b_gate, v_ssm_conv_w, v_ssm_conv_b, v_ssm_dt_bias, v_ssm_a_log, v_ssm_d, v_ssm_norm_w, v_attn_sinks, v_w_branch_ssm, v_w_branch_attn, v_w_mix_out, v_ln1_g, v_ln1_b, v_w_up, v_ffn_conv_w, v_ffn_conv_b, v_w_down, v_ln2_g, v_ln2_b):
    given = dict(locals())
    drop = lambda a, n: a if n == "rel_bias" or a.ndim == 2 else a[0]
    w = {n: drop(given[n], n) for n in _WEIGHTS}
    m = {n: drop(given["m_" + n], n) for n in _WEIGHTS}
    v = {n: drop(given["v_" + n], n) for n in _WEIGHTS}
    loss, d_x, outs = _step(x[0], loss_target[0], w, m, v)
    like = lambda a, n: a.reshape(given[n].shape)
    res = [loss, d_x[None]]
    for kind in ("grad", "delta", "m", "v"):
        res += [like(outs[kind][n], n) for n in _WEIGHTS]
    return tuple(res)
```

```python
import math
from typing import NamedTuple

import numpy as np
import jax
import jax.numpy as jnp
from jax import lax
from jax.experimental import pallas as pl
from jax.experimental.pallas import tpu as pltpu

F32 = jnp.float32
_ACT = jnp.bfloat16
_MXU = jnp.bfloat16

D_MODEL = 1024
D_INNER = 2048
N_HEADS = 32
HEAD_P = 64
N_GROUPS = 4
N_STATE = 128
CHUNK = 128
SSM_K = 4
A_HEADS = 16
A_DH = 64
WIN = 128
REL_BUCKETS = 32
BIAS_ROWS = 64
D_FF = 2816
FFN_K = 3
ALPHA = 2.0 ** 0.25
LN_EPS = 1e-5
RMS_EPS = 1e-5
IN_COLS = 8480
NEG = -1e30

ADAM_LR = 0.001
ADAM_B1 = 0.9
ADAM_B2 = 0.999
ADAM_EPS = 1e-08
ADAM_WD = 0.01
ADAM_STEP = 10

LANE = 128
SUB = 8

P_Z, P_XS, P_G, P_Q, P_BC, P_K, P_V, P_DT = 0, 2048, 4096, 6144, 7168, 8192, 8320, 8448
P_W = 8704
P_MAIN = 8192
T_K, T_V, T_DT = P_K - P_MAIN, P_V - P_MAIN, P_DT - P_MAIN
_PIECES = ((0, 2048, P_Z), (2048, 2048, P_XS), (4096, 1024, P_BC), (5120, 32, P_DT), (5152, 1024, P_Q),
           (6176, 128, P_K), (6304, 128, P_V), (6432, 2048, P_G))

N_CHIPS = 4
N_DEV = 8


def _cp(sem=None, vmem_mb=48):
    return pltpu.CompilerParams(dimension_semantics=sem, vmem_limit_bytes=vmem_mb * 1024 * 1024)


def _pick(n, cands):
    for c in cands:
        if n % c == 0:
            return c
    raise ValueError(f"no block size for {n}")


def _rows8(p):
    k, c = p.shape
    return jnp.concatenate([p.astype(F32), jnp.zeros((SUB - k, c), F32)], axis=0)


class _SideJob(NamedTuple):
    inputs: list
    out_shape: list
    sems: list
    start: object
    finish: object


def _mm(a, b, name, *, trans_a=False, out_dtype=F32, res=None, res_scale=1.0, side=None):
    if trans_a:
        k_dim, m = a.shape
    else:
        m, k_dim = a.shape
    k2, n = b.shape
    assert k_dim == k2, (a.shape, b.shape)
    tm = _pick(m, (1408, 1024, 512, 256, 128))
    tn = _pick(n, (1408, 1024, 512, 256, 128))
    tk = _pick(k_dim, (2816, 2176, 2048, 1024, 512, 256, 128))
    nk = k_dim // tk
    grid = (m // tm, n // tn, nk)
    dn = (((0,), (0,)), ((), ())) if trans_a else (((1,), (0,)), ((), ()))
    n_in = 2 if res is None else 3
    ns_in = len(side.inputs) if side else 0
    ns_out = len(side.out_shape) if side else 0

    def body(*refs):
        a_ref, b_ref = refs[0], refs[1]
        o_ref = refs[n_in + ns_in]
        scratch = refs[n_in + ns_in + 1 + ns_out:]
        job_refs = (refs[n_in:n_in + ns_in], refs[n_in + ns_in + 1:n_in + ns_in + 1 + ns_out],
                    scratch[1:] if nk > 1 else scratch)
        i, j, k = pl.program_id(0), pl.program_id(1), pl.program_id(2)

        def finish(r):
            if res is not None:
                r = r + res_scale * refs[2][...]
            o_ref[...] = r.astype(out_dtype)

        if side:
            @pl.when(jnp.logical_and(jnp.logical_and(i == 0, j == 0), k == 0))
            def _():
                side.start(*job_refs)

        part = lax.dot_general(a_ref[...].astype(_MXU), b_ref[...].astype(_MXU), dn, preferred_element_type=F32)
        if nk == 1:
            finish(part)
        else:
            acc = scratch[0]

            @pl.when(k == 0)
            def _():
                acc[...] = part

            @pl.when(k > 0)
            def _():
                acc[...] += part

            @pl.when(k == nk - 1)
            def _():
                finish(acc[...])

        if side:
            @pl.when(jnp.logical_and(jnp.logical_and(i == grid[0] - 1, j == grid[1] - 1), k == nk - 1))
            def _():
                side.finish(*job_refs)

    if trans_a:
        a_spec = pl.BlockSpec((tk, tm), lambda i, j, k: (k, i))
    else:
        a_spec = pl.BlockSpec((tm, tk), lambda i, j, k: (i, k))
    in_specs = [a_spec, pl.BlockSpec((tk, tn), lambda i, j, k: (k, j))]
    args = [a, b]
    if res is not None:
        in_specs.append(pl.BlockSpec((tm, tn), lambda i, j, k: (i, j)))
        args.append(res)
    out_spec = pl.BlockSpec((tm, tn), lambda i, j, k: (i, j))
    out_shape = jax.ShapeDtypeStruct((m, n), out_dtype)
    scratch_shapes = [pltpu.VMEM((tm, tn), F32)] if nk > 1 else []
    if not side:
        return pl.pallas_call(
            body, name=name, out_shape=out_shape, grid=grid, in_specs=in_specs, out_specs=out_spec,
            scratch_shapes=scratch_shapes, compiler_params=_cp(("parallel", "parallel", "arbitrary")),
        )(*args)
    hbm = pl.BlockSpec(memory_space=pltpu.HBM)
    outs = pl.pallas_call(
        body, name=name, out_shape=[out_shape] + list(side.out_shape), grid=grid,
        in_specs=in_specs + [hbm] * ns_in, out_specs=[out_spec] + [hbm] * ns_out,
        scratch_shapes=scratch_shapes + list(side.sems),
        compiler_params=_cp(("arbitrary", "arbitrary", "arbitrary")),
    )(*args, *side.inputs)
    return outs[0], list(outs[1:])


def _shift_down(cur, prev8, s):
    r = pltpu.roll(cur, s, 0)
    p = pltpu.roll(prev8, s, 0)
    row8 = lax.broadcasted_iota(jnp.int32, (SUB, 1), 0)
    fixed = jnp.where(row8 < s, p, r[0:SUB])
    if cur.shape[0] == SUB:
        return fixed
    return jnp.concatenate([fixed, r[SUB:]], axis=0)


def _shift_up(cur, next8, s):
    tm = cur.shape[0]
    r = pltpu.roll(cur, tm - s, 0)
    p = pltpu.roll(next8, SUB - s, 0)
    row8 = lax.broadcasted_iota(jnp.int32, (SUB, 1), 0)
    fixed = jnp.where(row8 >= SUB - s, p, r[tm - SUB:])
    return jnp.concatenate([r[:tm - SUB], fixed], axis=0)


def _conv_pre(cur, prev8, w_ref, b_row, taps):
    acc = cur * w_ref[taps - 1:taps, :] + b_row
    for s in range(1, taps):
        acc = acc + _shift_down(cur, prev8, s) * w_ref[taps - 1 - s:taps - s, :]
    return acc


def _dot01_r(x, m01, parts=3):
    acc = None
    r = x
    for _ in range(parts):
        hi = r.astype(jnp.bfloat16)
        t = jnp.dot(hi, m01, preferred_element_type=F32)
        acc = t if acc is None else acc + t
        r = r - hi.astype(F32)
    return acc


def _dot01_l(m01, x, parts=3):
    acc = None
    r = x
    for _ in range(parts):
        hi = r.astype(jnp.bfloat16)
        t = jnp.dot(m01, hi, preferred_element_type=F32)
        acc = t if acc is None else acc + t
        r = r - hi.astype(F32)
    return acc


def _dot(a, b):
    return jnp.dot(a.astype(_MXU), b.astype(_MXU), preferred_element_type=F32)


def _dot_nt(a, b):
    return lax.dot_general(a.astype(_MXU), b.astype(_MXU), (((1,), (1,)), ((), ())), preferred_element_type=F32)


def _dot_tn(a, b):
    return lax.dot_general(a.astype(_MXU), b.astype(_MXU), (((0,), (0,)), ((), ())), preferred_element_type=F32)


def _sigmoid(x):
    return 1.0 / (1.0 + jnp.exp(-x))


def _half_masks():
    lane = lax.broadcasted_iota(jnp.int32, (1, LANE), 1)
    lo = (lane < 64).astype(F32)
    return lo, 1.0 - lo


_TC = 1024


def _tm_rows(t):
    return min(512, t)


HALO = 16


def _prev_halo(tm, width, pos):
    def index(*ids):
        i, col = pos(*ids)
        return (jnp.maximum(i * (tm // HALO) - 1, 0), col)
    return pl.BlockSpec((HALO, width), index)


def _next_halo(tm, t, width, pos):
    def index(*ids):
        i, col = pos(*ids)
        return (jnp.minimum((i + 1) * (tm // HALO), t // HALO - 1), col)
    return pl.BlockSpec((HALO, width), index)


def _conv_silu_fwd(proj, colblk0, nblk, w8, b8, name):
    t = proj.shape[0]
    tm = _tm_rows(t)

    def body(c_ref, p_ref, w_ref, b_ref, o_ref, pre_ref):
        i = pl.program_id(1)
        prev8 = jnp.where(i > 0, p_ref[SUB:HALO, :].astype(F32), 0.0)
        pre = _conv_pre(c_ref[...].astype(F32), prev8, w_ref, b_ref[0:1, :], SSM_K)
        o_ref[...] = pre * _sigmoid(pre)
        pre_ref[...] = pre.astype(_ACT)

    blk = pl.BlockSpec((tm, _TC), lambda j, i: (i, j))
    return pl.pallas_call(
        body, name=name,
        out_shape=(jax.ShapeDtypeStruct((t, nblk * _TC), F32), jax.ShapeDtypeStruct((t, nblk * _TC), _ACT)),
        grid=(nblk, t // tm),
        in_specs=[pl.BlockSpec((tm, _TC), lambda j, i: (i, colblk0 + j)),
                  _prev_halo(tm, _TC, lambda j, i: (i, colblk0 + j)),
                  pl.BlockSpec((SUB, _TC), lambda j, i: (0, j)),
                  pl.BlockSpec((SUB, _TC), lambda j, i: (0, j))],
        out_specs=(blk, blk),
        compiler_params=_cp(("parallel", "parallel")),
    )(proj, proj, w8, b8)


def _silu_grad(pre):
    sg = _sigmoid(pre)
    return sg * (1.0 + pre * (1.0 - sg))


def _conv_grads(d, d_next8, cur, w_ref, acc_ref, taps, cols=slice(None)):
    du = d * w_ref[taps - 1:taps, :]
    acc_ref[taps:taps + 1, cols] += jnp.sum(d, axis=0, keepdims=True)
    acc_ref[taps - 1:taps, cols] += jnp.sum(d * cur, axis=0, keepdims=True)
    for s in range(1, taps):
        up = _shift_up(d, d_next8, s)
        du = du + up * w_ref[taps - 1 - s:taps - s, :]
        acc_ref[taps - 1 - s:taps - s, cols] += jnp.sum(up * cur, axis=0, keepdims=True)
    return du


def _conv_silu_bwd(proj, colblk0, nblk, pre, w8, d_out, d_proj, name):
    t = proj.shape[0]
    tm = _tm_rows(t)
    nt = t // tm

    def body(c_ref, pre_ref, pren_ref, w_ref, d_ref, dn_ref, _, du_ref, acc_ref):
        i = pl.program_id(1)

        @pl.when(i == 0)
        def _():
            acc_ref[...] = jnp.zeros_like(acc_ref)

        dpre = d_ref[...].astype(F32) * _silu_grad(pre_ref[...].astype(F32))
        dpre_n = jnp.where(i < nt - 1, dn_ref[0:SUB, :].astype(F32) * _silu_grad(pren_ref[0:SUB, :].astype(F32)), 0.0)
        du_ref[...] = _conv_grads(dpre, dpre_n, c_ref[...].astype(F32), w_ref, acc_ref, SSM_K).astype(_ACT)

    c = nblk * _TC
    blk = pl.BlockSpec((tm, _TC), lambda j, i: (i, j))
    nxt = _next_halo(tm, t, _TC, lambda j, i: (i, j))
    par = pl.BlockSpec((SUB, _TC), lambda j, i: (0, j))
    return pl.pallas_call(
        body, name=name,
        out_shape=(jax.ShapeDtypeStruct(d_proj.shape, d_proj.dtype), jax.ShapeDtypeStruct((SUB, c), F32)),
        grid=(nblk, nt),
        in_specs=[pl.BlockSpec((tm, _TC), lambda j, i: (i, colblk0 + j)), blk, nxt, par, blk, nxt,
                  pl.BlockSpec(memory_space=pl.ANY)],
        out_specs=(pl.BlockSpec((tm, _TC), lambda j, i: (i, colblk0 + j)), par),
        input_output_aliases={6: 0},
        compiler_params=_cp(("parallel", "arbitrary")),
    )(proj, pre, pre, w8, d_out, d_out, d_proj)


def _expand_consts():
    e = np.zeros((LANE, D_INNER), np.float32)
    for h in range(N_HEADS):
        e[h, h * HEAD_P:(h + 1) * HEAD_P] = 1.0
    return jnp.asarray(e, jnp.bfloat16), jnp.asarray(e.T.copy(), jnp.bfloat16)


def _ssd_common(dtr_ref, dtb_ref, alog_ref, e_ref):
    lane = lax.broadcasted_iota(jnp.int32, (1, LANE), 1)
    hm = lane < N_HEADS
    pre = dtr_ref[...] + dtb_ref[0:1, :]
    dt = jnp.where(hm, jnp.maximum(pre, 0.0) + jnp.log(1.0 + jnp.exp(-jnp.abs(pre))), 0.0)
    a_row = jnp.where(hm, -jnp.exp(alog_ref[0:1, :]), 0.0)
    adt = dt * a_row
    r = lax.broadcasted_iota(jnp.int32, (CHUNK, CHUNK), 0)
    c = lax.broadcasted_iota(jnp.int32, (CHUNK, CHUNK), 1)
    causal = r >= c
    acs = _dot01_l(causal.astype(jnp.bfloat16), adt)
    e = e_ref[...]
    acs_x = _dot01_r(acs, e, parts=2)
    dt_x = _dot01_r(dt, e, parts=2)
    return pre, dt, a_row, acs, acs_x, dt_x, causal, hm


def _decay(acs, acs_t, h, causal):
    seg = acs[:, h:h + 1] - acs_t[h:h + 1, :]
    return jnp.exp(jnp.where(causal, seg, NEG))


def _ssd_fwd(xs_c, bc_c, proj, tail, dtb8, alog8, dsk8, nw8, name):
    t = xs_c.shape[0]
    nc = t // CHUNK
    e_bf, _ = _expand_consts()
    gw = D_INNER // N_GROUPS
    per_step = 2 if nc % 2 == 0 else 1

    def body(xs_ref, bc_ref, dtr_ref, z_ref, dtb_ref, alog_ref, dsk_ref, nw_ref, e_ref,
             y_ref, ys_ref, hp_ref, h_ref):
        @pl.when(pl.program_id(0) == 0)
        def _():
            h_ref[...] = jnp.zeros_like(h_ref)

        for s in range(per_step):
            rows = pl.ds(s * CHUNK, CHUNK)
            chunk(xs_ref.at[rows], bc_ref.at[rows], dtr_ref.at[rows], z_ref.at[rows], dtb_ref, alog_ref, dsk_ref,
                  nw_ref, e_ref, y_ref.at[rows], ys_ref.at[rows], hp_ref.at[rows], h_ref)

    def chunk(xs_ref, bc_ref, dtr_ref, z_ref, dtb_ref, alog_ref, dsk_ref, nw_ref, e_ref,
              y_ref, ys_ref, hp_ref, h_ref):
        _, dt, a_row, acs, acs_x, dt_x, causal, _ = _ssd_common(dtr_ref, dtb_ref, alog_ref, e_ref)
        acs_t = acs.T
        xs = xs_ref[...]
        x_dt = xs * dt_x
        last_x = acs_x[CHUNK - 1:CHUNK, :]
        w_end = jnp.exp(last_x - acs_x)
        e_in = jnp.exp(acs_x)
        d_x = _dot01_r(dsk_ref[...], e_ref[...])[0:1, :]
        hprev = h_ref[...]
        hp_ref[...] = hprev
        lo, hi = _half_masks()
        for g in range(N_GROUPS):
            bg = bc_ref[:, g * N_STATE:(g + 1) * N_STATE]
            cg = bc_ref[:, N_GROUPS * N_STATE + g * N_STATE:N_GROUPS * N_STATE + (g + 1) * N_STATE]
            sl = slice(g * gw, (g + 1) * gw)
            gm = _dot_nt(cg, bg)
            st = _dot(bg.T, x_dt[:, sl] * w_end[:, sl])
            y_off = _dot(cg, hprev[:, sl]) * e_in[:, sl]
            for j in range(gw // LANE):
                h0 = g * (gw // HEAD_P) + 2 * j
                cs = slice(g * gw + j * LANE, g * gw + (j + 1) * LANE)
                xp = x_dt[:, cs]
                m0 = gm * _decay(acs, acs_t, h0, causal)
                m1 = gm * _decay(acs, acs_t, h0 + 1, causal)
                yd = _dot(m0, xp * lo) + _dot(m1, xp * hi)
                y_ref[:, cs] = yd + y_off[:, j * LANE:(j + 1) * LANE] + xs[:, cs] * d_x[:, cs]
            h_ref[:, sl] = hprev[:, sl] * jnp.exp(last_x[:, sl]) + st
        y = y_ref[...]
        z = z_ref[...].astype(F32)
        y2 = y * (z * _sigmoid(z))
        for g in range(N_GROUPS):
            sl = slice(g * gw, (g + 1) * gw)
            yg = y2[:, sl]
            rinv = lax.rsqrt(jnp.mean(yg * yg, axis=-1, keepdims=True) + RMS_EPS)
            ys_ref[:, sl] = (yg * rinv * nw_ref[0:1, sl]).astype(_ACT)

    small = pl.BlockSpec((SUB, LANE), lambda c: (0, 0))
    return pl.pallas_call(
        body, name=name,
        out_shape=(jax.ShapeDtypeStruct((t, D_INNER), F32), jax.ShapeDtypeStruct((t, D_INNER), _ACT),
                   jax.ShapeDtypeStruct((t, D_INNER), F32)),
        grid=(nc // per_step,),
        in_specs=[pl.BlockSpec((per_step * CHUNK, D_INNER), lambda c: (c, 0)),
                  pl.BlockSpec((per_step * CHUNK, 1024), lambda c: (c, 0)),
                  pl.BlockSpec((per_step * CHUNK, LANE), lambda c: (c, T_DT // LANE)),
                  pl.BlockSpec((per_step * CHUNK, D_INNER), lambda c: (c, P_Z // D_INNER)),
                  small, small, small,
                  pl.BlockSpec((SUB, D_INNER), lambda c: (0, 0)),
                  pl.BlockSpec((LANE, D_INNER), lambda c: (0, 0))],
        out_specs=(pl.BlockSpec((per_step * CHUNK, D_INNER), lambda c: (c, 0)),
                   pl.BlockSpec((per_step * CHUNK, D_INNER), lambda c: (c, 0)),
                   pl.BlockSpec((per_step * N_STATE, D_INNER), lambda c: (c, 0))),
        scratch_shapes=[pltpu.VMEM((N_STATE, D_INNER), F32)],
        compiler_params=_cp(("arbitrary",)),
    )(xs_c, bc_c, tail, proj, dtb8, alog8, dsk8, nw8, e_bf)


def _ssd_bwd(d_ys, y, xs_c, bc_c, proj, tail, hprev_all, dtb8, alog8, dsk8, nw8, d_proj, name):
    t = xs_c.shape[0]
    nc = t // CHUNK
    e_bf, et_bf = _expand_consts()
    gw = D_INNER // N_GROUPS

    def body(dys_ref, y_ref, xs_ref, bc_ref, dtr_ref, z_ref, hp_ref, dtb_ref, alog_ref, dsk_ref, nw_ref,
             e_ref, et_ref, _, dxs_ref, dbc_ref, dz_ref, ddt_ref, acc_ref, dnw_ref, dh_ref, dx_ref):
        step = pl.program_id(0)

        @pl.when(step == 0)
        def _():
            dh_ref[...] = jnp.zeros_like(dh_ref)
            acc_ref[...] = jnp.zeros_like(acc_ref)
            dnw_ref[...] = jnp.zeros_like(dnw_ref)

        pre, dt, a_row, acs, acs_x, dt_x, causal, hm = _ssd_common(dtr_ref, dtb_ref, alog_ref, e_ref)
        acs_t = acs.T
        et = et_ref[...]
        xs = xs_ref[...]
        x_dt = xs * dt_x
        last_x = acs_x[CHUNK - 1:CHUNK, :]
        w_end = jnp.exp(last_x - acs_x)
        e_in = jnp.exp(acs_x)
        e_last = jnp.exp(last_x)
        d_x = _dot01_r(dsk_ref[...], e_ref[...])[0:1, :]

        y = y_ref[...]
        z = z_ref[...].astype(F32)
        sz = _sigmoid(z)
        gz = z * sz
        y2 = y * gz
        dys = dys_ref[...].astype(F32)
        for g in range(N_GROUPS):
            sl = slice(g * gw, (g + 1) * gw)
            yg = y2[:, sl]
            rinv = lax.rsqrt(jnp.mean(yg * yg, axis=-1, keepdims=True) + RMS_EPS)
            nrm = yg * rinv
            dn = dys[:, sl] * nw_ref[0:1, sl]
            dnw_ref[0:1, sl] += jnp.sum(dys[:, sl] * nrm, axis=0, keepdims=True)
            dx_ref[:, sl] = rinv * (dn - nrm * jnp.mean(dn * nrm, axis=-1, keepdims=True))
        dy2 = dx_ref[...]
        dy = dy2 * gz
        dz_ref[...] = (dy2 * y * (sz * (1.0 + z * (1.0 - sz)))).astype(_ACT)

        dh_next = dh_ref[...]
        hprev = hp_ref[...]
        lo, hi = _half_masks()
        r = lax.broadcasted_iota(jnp.int32, (CHUNK, CHUNK), 0)
        c = lax.broadcasted_iota(jnp.int32, (CHUNK, CHUNK), 1)
        from_here = (c >= r).astype(jnp.bfloat16)
        before = c < r
        lane = lax.broadcasted_iota(jnp.int32, (1, LANE), 1)
        da_intra = jnp.zeros((CHUNK, LANE), F32)
        v_seg = jnp.zeros((CHUNK, LANE), F32)
        z_seg = jnp.zeros((CHUNK, LANE), F32)
        tail_parts = []
        for g in range(N_GROUPS):
            bg = bc_ref[:, g * N_STATE:(g + 1) * N_STATE]
            cg = bc_ref[:, N_GROUPS * N_STATE + g * N_STATE:N_GROUPS * N_STATE + (g + 1) * N_STATE]
            sl = slice(g * gw, (g + 1) * gw)
            et_g = et_ref[g * gw:(g + 1) * gw, :]
            gm = _dot_nt(cg, bg)
            dzg = e_in[:, sl] * dy[:, sl]
            dcg = _dot_nt(dzg, hprev[:, sl])
            dh_c = _dot(cg.T, dzg)
            q = _dot(bg, dh_next[:, sl])
            dbg = _dot_nt(x_dt[:, sl] * w_end[:, sl], dh_next[:, sl])
            y_off = _dot(cg, hprev[:, sl]) * e_in[:, sl]
            v_seg = v_seg + _dot01_r(dy[:, sl] * y_off, et_g, parts=1)
            z_seg = z_seg + _dot01_r(w_end[:, sl] * q * x_dt[:, sl], et_g, parts=1)
            dgm = jnp.zeros((CHUNK, CHUNK), F32)
            for j in range(gw // LANE):
                h0 = g * (gw // HEAD_P) + 2 * j
                cs = slice(g * gw + j * LANE, g * gw + (j + 1) * LANE)
                xp = x_dt[:, cs]
                dyp = dy[:, cs]
                dxd = jnp.zeros((CHUNK, LANE), F32)
                for half, msk in ((0, lo), (1, hi)):
                    lam = _decay(acs, acs_t, h0 + half, causal)
                    mm = gm * lam
                    dym = dyp * msk
                    dmm = _dot_nt(dym, xp)
                    dxd = dxd + _dot_tn(mm, dym)
                    dgm = dgm + dmm * lam
                    below = _dot(from_here, dmm * mm)
                    col = jnp.sum(jnp.where(before, below, 0.0), axis=-1, keepdims=True)
                    da_intra = da_intra + jnp.where(lane == h0 + half, col, 0.0)
                dx_ref[:, cs] = dxd + w_end[:, cs] * q[:, j * LANE:(j + 1) * LANE]
            dbc_ref[:, N_GROUPS * N_STATE + g * N_STATE:N_GROUPS * N_STATE + (g + 1) * N_STATE] = dcg + _dot(dgm, bg)
            dbc_ref[:, g * N_STATE:(g + 1) * N_STATE] = dbg + _dot_tn(dgm, cg)
            dh_ref[:, sl] = e_last[:, sl] * dh_next[:, sl] + dh_c
            tail_parts.append(e_last[:, sl] * jnp.sum(dh_next[:, sl] * hprev[:, sl], axis=0, keepdims=True))
        dxt = dx_ref[...]

        u_seg = _dot01_r(xs * dxt, et, parts=1)
        q_full = jnp.concatenate(tail_parts, axis=1)
        t_row = _dot01_r(jnp.broadcast_to(q_full, (SUB, D_INNER)), et)[0:1, :]
        d_alpha = (da_intra + _dot01_l(from_here, v_seg) + _dot01_l(before.astype(jnp.bfloat16), z_seg) + t_row)
        d_dt = a_row * d_alpha + u_seg
        sgp = _sigmoid(pre)
        d_raw = jnp.where(hm, d_dt * sgp, 0.0)
        ddt_ref[...] = d_raw.astype(_ACT)
        acc_ref[0:1, :] += jnp.sum(d_raw, axis=0, keepdims=True)
        acc_ref[1:2, :] += jnp.sum(d_alpha * dt, axis=0, keepdims=True) * a_row
        dd_row = jnp.sum(dy * xs, axis=0, keepdims=True)
        acc_ref[2:3, :] += _dot01_r(jnp.broadcast_to(dd_row, (SUB, D_INNER)), et)[0:1, :]
        dxs_ref[...] = dy * d_x + dxt * dt_x

    rev = lambda c: (nc - 1 - c, 0)
    small = pl.BlockSpec((SUB, LANE), lambda c: (0, 0))
    return pl.pallas_call(
        body, name=name,
        out_shape=(jax.ShapeDtypeStruct((t, D_INNER), F32), jax.ShapeDtypeStruct((t, 1024), F32),
                   jax.ShapeDtypeStruct(d_proj.shape, d_proj.dtype), jax.ShapeDtypeStruct((t, LANE), _ACT),
                   jax.ShapeDtypeStruct((SUB, LANE), F32), jax.ShapeDtypeStruct((SUB, D_INNER), F32)),
        grid=(nc,),
        in_specs=[pl.BlockSpec((CHUNK, D_INNER), rev),
                  pl.BlockSpec((CHUNK, D_INNER), rev),
                  pl.BlockSpec((CHUNK, D_INNER), rev),
                  pl.BlockSpec((CHUNK, 1024), rev),
                  pl.BlockSpec((CHUNK, LANE), lambda c: (nc - 1 - c, T_DT // LANE)),
                  pl.BlockSpec((CHUNK, D_INNER), lambda c: (nc - 1 - c, P_Z // D_INNER)),
                  pl.BlockSpec((N_STATE, D_INNER), rev),
                  small, small, small,
                  pl.BlockSpec((SUB, D_INNER), lambda c: (0, 0)),
                  pl.BlockSpec((LANE, D_INNER), lambda c: (0, 0)),
                  pl.BlockSpec((D_INNER, LANE), lambda c: (0, 0)),
                  pl.BlockSpec(memory_space=pl.ANY)],
        out_specs=(pl.BlockSpec((CHUNK, D_INNER), rev),
                   pl.BlockSpec((CHUNK, 1024), rev),
                   pl.BlockSpec((CHUNK, D_INNER), lambda c: (nc - 1 - c, P_Z // D_INNER)),
                   pl.BlockSpec((CHUNK, LANE), rev),
                   small,
                   pl.BlockSpec((SUB, D_INNER), lambda c: (0, 0))),
        input_output_aliases={13: 2},
        scratch_shapes=[pltpu.VMEM((N_STATE, D_INNER), F32), pltpu.VMEM((CHUNK, D_INNER), F32)],
        compiler_params=_cp(("arbitrary",), vmem_mb=56),
    )(d_ys, y, xs_c, bc_c, tail, proj, hprev_all, dtb8, alog8, dsk8, nw8, e_bf, et_bf, d_proj)


def _rel_tables():
    qi = np.arange(WIN)[:, None] + WIN
    kj = np.arange(2 * WIN)[None, :]
    rel = qi - kj
    n = np.maximum(rel, 0)
    max_exact = REL_BUCKETS // 2
    nf = np.maximum(n, 1).astype(np.float32)
    large = max_exact + (np.log(nf / np.float32(max_exact)) / np.float32(math.log(WIN / max_exact))
                         * np.float32(REL_BUCKETS - max_exact)).astype(np.int32)
    large = np.minimum(large, REL_BUCKETS - 1)
    bucket = np.where(n < max_exact, n, large)
    valid = (rel >= 0) & (rel < WIN)
    sink_col = np.broadcast_to(kj == 0, rel.shape)
    onehot = np.zeros((BIAS_ROWS, WIN * 2 * WIN), np.float32)
    flat_b = np.where(sink_col, REL_BUCKETS, bucket).reshape(-1)
    flat_v = (valid | sink_col).reshape(-1)
    first_v = ((valid & (kj >= WIN)) | sink_col).reshape(-1)
    idx = np.arange(WIN * 2 * WIN)
    onehot[flat_b[flat_v], idx[flat_v]] = 1.0
    return onehot, np.stack([first_v, flat_v]).astype(np.float32)


def _bias_expand(table_t, name):
    onehot, valid = _rel_tables()

    def body(rb_ref, oh_ref, v_ref, o_ref):
        full = _dot01_r(rb_ref[...], oh_ref[...])
        o_ref[0] = jnp.where(v_ref[0:1, :] > 0.5, full, NEG)
        o_ref[1] = jnp.where(v_ref[1:2, :] > 0.5, full, NEG)

    return pl.pallas_call(
        body, name=name, out_shape=jax.ShapeDtypeStruct((2, A_HEADS, WIN * 2 * WIN), F32),
        compiler_params=_cp(None),
    )(table_t, jnp.asarray(onehot, jnp.bfloat16), jnp.asarray(valid, F32))


def _bias_reduce(dbias, name):
    onehot, _ = _rel_tables()

    def body(d_ref, oh_ref, o_ref):
        acc = None
        r = d_ref[...]
        for _ in range(3):
            hi = r.astype(jnp.bfloat16)
            tt = lax.dot_general(hi, oh_ref[...], (((1,), (1,)), ((), ())), preferred_element_type=F32)
            acc = tt if acc is None else acc + tt
            r = r - hi.astype(F32)
        o_ref[...] = acc

    return pl.pallas_call(
        body, name=name, out_shape=jax.ShapeDtypeStruct((A_HEADS, BIAS_ROWS), F32),
        compiler_params=_cp(None),
    )(dbias, jnp.asarray(onehot, jnp.bfloat16))


def _attn_bands(kc_ref, kp_ref, vc_ref, vp_ref, has_prev):
    lo, hi = _half_masks()
    row = lax.broadcasted_iota(jnp.int32, (2 * WIN, 1), 0)
    keep = (row > 0).astype(F32)
    kb = jnp.concatenate([jnp.where(has_prev, kp_ref[...], 0.0), kc_ref[...]], axis=0) * (keep * (A_DH ** -0.5))
    vb = jnp.concatenate([jnp.where(has_prev, vp_ref[...], 0.0), vc_ref[...]], axis=0) * keep
    kr = pltpu.roll(kb, 64, 1)
    vr = pltpu.roll(vb, 64, 1)
    kk = ((kb * lo, kr * hi), (kr * lo, kb * hi))
    vv = ((vb * lo, vr * hi), (vr * lo, vb * hi))
    return kk, vv, (hi, lo)


def _attn_logits(q_ref, kk, lg_ref):
    for h in range(A_HEADS):
        j, half, kv = h // 2, h % 2, h // (A_HEADS // 2)
        lg_ref[h] = _dot_nt(q_ref[:, j * LANE:(j + 1) * LANE], kk[kv][half])


def _attn_fwd(proj, tail, bias, name):
    t = proj.shape[0]
    nb = t // WIN

    def body(q_ref, kc_ref, kp_ref, vc_ref, vp_ref, b_ref, o_ref, lg_ref, p_ref):
        n = pl.program_id(0)
        kk, vv, ones = _attn_bands(kc_ref, kp_ref, vc_ref, vp_ref, n > 0)
        _attn_logits(q_ref, kk, lg_ref)
        for h in range(A_HEADS):
            logits = lg_ref[h] + b_ref[h]
            p_ref[h] = jnp.exp(logits - jnp.max(logits, axis=-1, keepdims=True)).astype(_MXU)
        lane = lax.broadcasted_iota(jnp.int32, (1, LANE), 1)
        for j in range(A_HEADS // 2):
            kv = (2 * j) // (A_HEADS // 2)
            outs = []
            for half in range(2):
                o = jnp.dot(p_ref[2 * j + half], (vv[kv][half] + ones[half]).astype(_MXU), preferred_element_type=F32)
                outs.append(o / pltpu.roll(o, 64, 1))
            o_ref[:, j * LANE:(j + 1) * LANE] = jnp.where(lane < 64, outs[0], outs[1]).astype(_ACT)

    kvspec = lambda col, prev: pl.BlockSpec(
        (WIN, LANE), (lambda n: (jnp.maximum(n - 1, 0), col)) if prev else (lambda n: (n, col)))
    return pl.pallas_call(
        body, name=name, out_shape=jax.ShapeDtypeStruct((t, D_MODEL), _ACT),
        grid=(nb,),
        in_specs=[pl.BlockSpec((WIN, 1024), lambda n: (n, P_Q // 1024)),
                  kvspec(T_K // LANE, False), kvspec(T_K // LANE, True),
                  kvspec(T_V // LANE, False), kvspec(T_V // LANE, True),
                  pl.BlockSpec((None, A_HEADS, WIN, 2 * WIN), lambda n: (jnp.minimum(n, 1), 0, 0, 0))],
        out_specs=pl.BlockSpec((WIN, 1024), lambda n: (n, 0)),
        scratch_shapes=[pltpu.VMEM((A_HEADS, WIN, 2 * WIN), F32), pltpu.VMEM((A_HEADS, WIN, 2 * WIN), _MXU)],
        compiler_params=_cp(("parallel",)),
    )(proj, tail, tail, tail, tail, bias)


def _attn_bwd(proj, tail, bias, y_attn, d_out, d_proj, name):
    t = proj.shape[0]
    nb = t // WIN

    def body(q_ref, kc_ref, kp_ref, vc_ref, vp_ref, b_ref, y_ref, do_ref, _,
             dq_ref, dk_ref, dv_ref, db_ref, ck_ref, cv_ref, lg_ref, dl_ref, p_ref):
        n = pl.program_id(0)

        @pl.when(n == 0)
        def _():
            db_ref[...] = jnp.zeros_like(db_ref)
            ck_ref[...] = jnp.zeros_like(ck_ref)
            cv_ref[...] = jnp.zeros_like(cv_ref)

        @pl.when(n < nb)
        def _():
            kk, vv, _ = _attn_bands(kc_ref, kp_ref, vc_ref, vp_ref, n > 0)
            lo, hi = _half_masks()
            ones_k = jnp.ones((2 * WIN, LANE), jnp.bfloat16)
            ones_d = jnp.ones((LANE, LANE), jnp.bfloat16)
            _attn_logits(q_ref, kk, lg_ref)
            for h in range(A_HEADS):
                j, half, kv = h // 2, h % 2, h // (A_HEADS // 2)
                msk = hi if half else lo
                logits = lg_ref[h] + b_ref[h]
                p = jnp.exp(logits - jnp.max(logits, axis=-1, keepdims=True))
                den = jnp.dot(p.astype(_MXU), ones_k.astype(_MXU), preferred_element_type=F32)
                dop = do_ref[:, j * LANE:(j + 1) * LANE].astype(F32)
                delta = _dot01_r(dop * y_ref[:, j * LANE:(j + 1) * LANE].astype(F32) * msk, ones_d, parts=2)
                inv = 1.0 / den
                probs = p * jnp.concatenate([inv, inv], axis=1)
                dprobs = _dot_nt(dop, vv[kv][half])
                dlog = probs * (dprobs - jnp.concatenate([delta, delta], axis=1))
                db_ref[h] += dlog
                dl_ref[h] = dlog.astype(_MXU)
                p_ref[h] = probs.astype(_MXU)
            dk_t = [[None, None], [None, None]]
            dv_t = [[None, None], [None, None]]
            for j in range(A_HEADS // 2):
                kv = (2 * j) // (A_HEADS // 2)
                qs = q_ref[:, j * LANE:(j + 1) * LANE].astype(F32) * (A_DH ** -0.5)
                dop = do_ref[:, j * LANE:(j + 1) * LANE].astype(F32)
                dq = None
                for half, msk in ((0, lo), (1, hi)):
                    h = 2 * j + half
                    dqh = jnp.dot(dl_ref[h], kk[kv][half].astype(_MXU), preferred_element_type=F32)
                    dq = dqh if dq is None else dq + dqh
                    dkh = lax.dot_general((qs * msk).astype(_MXU), dl_ref[h], (((0,), (0,)), ((), ())),
                                          preferred_element_type=F32)
                    dvh = lax.dot_general((dop * msk).astype(_MXU), p_ref[h], (((0,), (0,)), ((), ())),
                                          preferred_element_type=F32)
                    dk_t[kv][half] = dkh if dk_t[kv][half] is None else dk_t[kv][half] + dkh
                    dv_t[kv][half] = dvh if dv_t[kv][half] is None else dv_t[kv][half] + dvh
                dq_ref[:, j * LANE:(j + 1) * LANE] = dq.astype(_ACT)
            row = lax.broadcasted_iota(jnp.int32, (2 * WIN, 1), 0)

            def band(acc):
                a = (acc[0][0] + pltpu.roll(acc[0][1], 64, 0)) + (pltpu.roll(acc[1][0], 64, 0) + acc[1][1])
                return jnp.where(row > 0, a.T, 0.0)

            dkb = band(dk_t)
            dvb = band(dv_t)
            dk_ref[...] = (ck_ref[...] + dkb[0:WIN]).astype(_ACT)
            dv_ref[...] = (cv_ref[...] + dvb[0:WIN]).astype(_ACT)
            ck_ref[...] = dkb[WIN:]
            cv_ref[...] = dvb[WIN:]

        @pl.when(n == nb)
        def _():
            dk_ref[...] = ck_ref[...].astype(_ACT)
            dv_ref[...] = cv_ref[...].astype(_ACT)

    cur = lambda n: jnp.minimum(n, nb - 1)
    prv = lambda n: jnp.maximum(jnp.minimum(n, nb - 1) - 1, 0)
    kvspec = lambda col, prev: pl.BlockSpec(
        (WIN, LANE), (lambda n: (prv(n), col)) if prev else (lambda n: (cur(n), col)))
    band_shape = (A_HEADS, WIN, 2 * WIN)
    return pl.pallas_call(
        body, name=name,
        out_shape=(jax.ShapeDtypeStruct(d_proj.shape, d_proj.dtype), jax.ShapeDtypeStruct((t, LANE), _ACT),
                   jax.ShapeDtypeStruct((t, LANE), _ACT), jax.ShapeDtypeStruct(band_shape, F32)),
        grid=(nb + 1,),
        in_specs=[pl.BlockSpec((WIN, 1024), lambda n: (cur(n), P_Q // 1024)),
                  kvspec(T_K // LANE, False), kvspec(T_K // LANE, True),
                  kvspec(T_V // LANE, False), kvspec(T_V // LANE, True),
                  pl.BlockSpec((None,) + band_shape, lambda n: (jnp.minimum(n, 1), 0, 0, 0)),
                  pl.BlockSpec((WIN, 1024), lambda n: (cur(n), 0)),
                  pl.BlockSpec((WIN, 1024), lambda n: (cur(n), 0)),
                  pl.BlockSpec(memory_space=pl.ANY)],
        out_specs=(pl.BlockSpec((WIN, 1024), lambda n: (cur(n), P_Q // 1024)),
                   pl.BlockSpec((WIN, LANE), lambda n: (jnp.maximum(n - 1, 0), 0)),
                   pl.BlockSpec((WIN, LANE), lambda n: (jnp.maximum(n - 1, 0), 0)),
                   pl.BlockSpec(band_shape, lambda n: (0, 0, 0))),
        input_output_aliases={8: 0},
        scratch_shapes=[pltpu.VMEM((WIN, LANE), F32), pltpu.VMEM((WIN, LANE), F32),
                        pltpu.VMEM(band_shape, F32), pltpu.VMEM(band_shape, _MXU), pltpu.VMEM(band_shape, _MXU)],
        compiler_params=_cp(("arbitrary",)),
    )(proj, tail, tail, tail, tail, bias, y_attn, d_out, d_proj)


def _merge_fwd(bs, ba, proj, bg8, name):
    t = bs.shape[0]
    tm = _tm_rows(t)

    def body(bs_ref, ba_ref, gs_ref, ga_ref, bgs_ref, bga_ref, o_ref):
        g_s = _sigmoid(gs_ref[...] + bgs_ref[0:1, :])
        g_a = _sigmoid(ga_ref[...] + bga_ref[0:1, :])
        o_ref[...] = (g_s * bs_ref[...] + g_a * ba_ref[...]).astype(_ACT)

    row = lambda col: pl.BlockSpec((tm, 1024), lambda i: (i, col))
    return pl.pallas_call(
        body, name=name, out_shape=jax.ShapeDtypeStruct((t, D_MODEL), _ACT), grid=(t // tm,),
        in_specs=[row(0), row(0), row(P_G // 1024), row(P_G // 1024 + 1),
                  pl.BlockSpec((SUB, 1024), lambda i: (0, 0)), pl.BlockSpec((SUB, 1024), lambda i: (0, 1))],
        out_specs=row(0), compiler_params=_cp(("parallel",)),
    )(bs, ba, proj, proj, bg8, bg8)


def _merge_bwd(d_merged, bs, ba, proj, bg8, name):
    t = bs.shape[0]
    tm = _tm_rows(t)

    def body(dm_ref, bs_ref, ba_ref, gs_ref, ga_ref, bgs_ref, bga_ref, dbs_ref, dba_ref, dg_ref, acc_ref):
        @pl.when(pl.program_id(0) == 0)
        def _():
            acc_ref[...] = jnp.zeros_like(acc_ref)

        dm = dm_ref[...].astype(F32)
        g_s = _sigmoid(gs_ref[...] + bgs_ref[0:1, :])
        g_a = _sigmoid(ga_ref[...] + bga_ref[0:1, :])
        dbs_ref[...] = (dm * g_s).astype(_ACT)
        dba_ref[...] = (dm * g_a).astype(_ACT)
        dgs = dm * bs_ref[...].astype(F32) * g_s * (1.0 - g_s)
        dga = dm * ba_ref[...].astype(F32) * g_a * (1.0 - g_a)
        dg_ref[:, 0:1024] = dgs.astype(_ACT)
        dg_ref[:, 1024:2048] = dga.astype(_ACT)
        acc_ref[0:1, 0:1024] += jnp.sum(dgs, axis=0, keepdims=True)
        acc_ref[0:1, 1024:2048] += jnp.sum(dga, axis=0, keepdims=True)

    row = lambda col: pl.BlockSpec((tm, 1024), lambda i: (i, col))
    return pl.pallas_call(
        body, name=name,
        out_shape=(jax.ShapeDtypeStruct((t, D_MODEL), _ACT), jax.ShapeDtypeStruct((t, D_MODEL), _ACT),
                   jax.ShapeDtypeStruct((t, P_W), _ACT), jax.ShapeDtypeStruct((SUB, 2048), F32)),
        grid=(t // tm,),
        in_specs=[row(0), row(0), row(0), row(P_G // 1024), row(P_G // 1024 + 1),
                  pl.BlockSpec((SUB, 1024), lambda i: (0, 0)), pl.BlockSpec((SUB, 1024), lambda i: (0, 1))],
        out_specs=(row(0), row(0), pl.BlockSpec((tm, 2048), lambda i: (i, P_G // 2048)),
                   pl.BlockSpec((SUB, 2048), lambda i: (0, 0))),
        compiler_params=_cp(("arbitrary",)),
    )(d_merged, bs, ba, proj, proj, bg8, bg8)


def _place_tail(d_k, d_v, d_dt, d_proj, name):
    t = d_k.shape[0]
    tm = _tm_rows(t)
    width = P_W - P_MAIN

    def body(k_ref, v_ref, dt_ref, _, o_ref):
        o_ref[:, T_K:T_K + LANE] = k_ref[...]
        o_ref[:, T_V:T_V + LANE] = v_ref[...]
        o_ref[:, T_DT:T_DT + LANE] = dt_ref[...]
        o_ref[:, T_DT + LANE:width] = jnp.zeros((tm, width - T_DT - LANE), o_ref.dtype)

    blk = pl.BlockSpec((tm, LANE), lambda i: (i, 0))
    return pl.pallas_call(
        body, name=name, out_shape=jax.ShapeDtypeStruct(d_proj.shape, d_proj.dtype), grid=(t // tm,),
        in_specs=[blk, blk, blk, pl.BlockSpec(memory_space=pl.ANY)],
        out_specs=pl.BlockSpec((tm, width), lambda i: (i, P_MAIN // width)),
        input_output_aliases={3: 0}, compiler_params=_cp(("parallel",)),
    )(d_k, d_v, d_dt, d_proj)


def _ln_stats(r):
    mu = jnp.mean(r, axis=-1, keepdims=True)
    xc = r - mu
    var = jnp.mean(xc * xc, axis=-1, keepdims=True)
    rstd = lax.rsqrt(var + LN_EPS)
    return xc * rstd, rstd


def _ln_bwd(dxhat, xhat, rstd):
    return rstd * (dxhat - jnp.mean(dxhat, axis=-1, keepdims=True)
                   - xhat * jnp.mean(dxhat * xhat, axis=-1, keepdims=True))


def _ln1_fwd(x, mix, g8, b8, name):
    t = x.shape[0]
    tm = _tm_rows(t)

    def body(x_ref, m_ref, g_ref, b_ref, xh_ref, h_ref, rs_ref):
        xhat, rstd = _ln_stats(ALPHA * x_ref[...] + m_ref[...])
        xh_ref[...] = xhat
        h_ref[...] = (xhat * g_ref[0:1, :] + b_ref[0:1, :]).astype(_ACT)
        rs_ref[...] = rstd

    row = pl.BlockSpec((tm, D_MODEL), lambda i: (i, 0))
    par = pl.BlockSpec((SUB, D_MODEL), lambda i: (0, 0))
    return pl.pallas_call(
        body, name=name,
        out_shape=(jax.ShapeDtypeStruct((t, D_MODEL), F32), jax.ShapeDtypeStruct((t, D_MODEL), _ACT),
                   jax.ShapeDtypeStruct((t, 1), F32)),
        grid=(t // tm,), in_specs=[row, row, par, par],
        out_specs=(row, row, pl.BlockSpec((tm, 1), lambda i: (i, 0))),
        compiler_params=_cp(("parallel",)),
    )(x, mix, g8, b8)


def _ln2_loss(xhat1, ffn, target, g1_8, b1_8, g2_8, b2_8, name):
    t = xhat1.shape[0]
    tm = _tm_rows(t)

    def body(xh_ref, f_ref, t_ref, g1_ref, b1_ref, g2_ref, b2_ref, d_ref, db_ref, acc_ref):
        @pl.when(pl.program_id(0) == 0)
        def _():
            acc_ref[...] = jnp.zeros_like(acc_ref)

        h1 = xh_ref[...] * g1_ref[0:1, :] + b1_ref[0:1, :]
        xhat, rstd = _ln_stats(ALPHA * h1 + f_ref[...])
        diff = xhat * g2_ref[0:1, :] + b2_ref[0:1, :] - t_ref[...]
        dy = diff * (1.0 / D_MODEL)
        acc_ref[0:1, :] += jnp.sum(dy * xhat, axis=0, keepdims=True)
        acc_ref[1:2, :] += jnp.sum(dy, axis=0, keepdims=True)
        acc_ref[2:3, :] += jnp.sum(diff * diff, axis=0, keepdims=True)
        d = _ln_bwd(dy * g2_ref[0:1, :], xhat, rstd)
        d_ref[...] = d
        db_ref[...] = d.astype(_ACT)

    row = pl.BlockSpec((tm, D_MODEL), lambda i: (i, 0))
    par = pl.BlockSpec((SUB, D_MODEL), lambda i: (0, 0))
    return pl.pallas_call(
        body, name=name,
        out_shape=(jax.ShapeDtypeStruct((t, D_MODEL), F32), jax.ShapeDtypeStruct((t, D_MODEL), _ACT),
                   jax.ShapeDtypeStruct((SUB, D_MODEL), F32)),
        grid=(t // tm,), in_specs=[row, row, row, par, par, par, par],
        out_specs=(row, row, par), compiler_params=_cp(("arbitrary",)),
    )(xhat1, ffn, target, g1_8, b1_8, g2_8, b2_8)


def _ln1_bwd(d_r2, d_h1_ffn, xhat1, rstd1, g1_8, name):
    t = xhat1.shape[0]
    tm = _tm_rows(t)

    def body(d2_ref, df_ref, xh_ref, rs_ref, g_ref, d_ref, db_ref, acc_ref):
        @pl.when(pl.program_id(0) == 0)
        def _():
            acc_ref[...] = jnp.zeros_like(acc_ref)

        dh = ALPHA * d2_ref[...] + df_ref[...]
        xhat = xh_ref[...]
        acc_ref[0:1, :] += jnp.sum(dh * xhat, axis=0, keepdims=True)
        acc_ref[1:2, :] += jnp.sum(dh, axis=0, keepdims=True)
        d = _ln_bwd(dh * g_ref[0:1, :], xhat, rs_ref[...])
        d_ref[...] = d
        db_ref[...] = d.astype(_ACT)

    row = pl.BlockSpec((tm, D_MODEL), lambda i: (i, 0))
    par = pl.BlockSpec((SUB, D_MODEL), lambda i: (0, 0))
    return pl.pallas_call(
        body, name=name,
        out_shape=(jax.ShapeDtypeStruct((t, D_MODEL), F32), jax.ShapeDtypeStruct((t, D_MODEL), _ACT),
                   jax.ShapeDtypeStruct((SUB, D_MODEL), F32)),
        grid=(t // tm,), in_specs=[row, row, row, pl.BlockSpec((tm, 1), lambda i: (i, 0)), par],
        out_specs=(row, row, par), compiler_params=_cp(("arbitrary",)),
    )(d_r2, d_h1_ffn, xhat1, rstd1, g1_8)


def _ffn_tm(t):
    return min(256, t)


def _ffn_act_fwd(u0, cw8, cb8, name):
    t = u0.shape[0]
    tm = _ffn_tm(t)

    def body(g_ref, gp_ref, v_ref, vp_ref, wg_ref, wv_ref, bg_ref, bv_ref, o_ref, u_ref):
        i = pl.program_id(0)
        gprev = jnp.where(i > 0, gp_ref[SUB:HALO, :].astype(F32), 0.0)
        vprev = jnp.where(i > 0, vp_ref[SUB:HALO, :].astype(F32), 0.0)
        gate = _conv_pre(g_ref[...].astype(F32), gprev, wg_ref, bg_ref[0:1, :], FFN_K)
        val = _conv_pre(v_ref[...].astype(F32), vprev, wv_ref, bv_ref[0:1, :], FFN_K)
        o_ref[...] = (gate * _sigmoid(gate) * val).astype(_ACT)
        u_ref[:, 0:D_FF] = gate.astype(_ACT)
        u_ref[:, D_FF:2 * D_FF] = val.astype(_ACT)

    cur = lambda col: pl.BlockSpec((tm, D_FF), lambda i: (i, col))
    prv = lambda col: _prev_halo(tm, D_FF, lambda i: (i, col))
    par = lambda col: pl.BlockSpec((SUB, D_FF), lambda i: (0, col))
    return pl.pallas_call(
        body, name=name,
        out_shape=(jax.ShapeDtypeStruct((t, D_FF), _ACT), jax.ShapeDtypeStruct((t, 2 * D_FF), _ACT)),
        grid=(t // tm,),
        in_specs=[cur(0), prv(0), cur(1), prv(1), par(0), par(1), par(0), par(1)],
        out_specs=(pl.BlockSpec((tm, D_FF), lambda i: (i, 0)), pl.BlockSpec((tm, 2 * D_FF), lambda i: (i, 0))),
        compiler_params=_cp(("parallel",)),
    )(u0, u0, u0, u0, cw8, cw8, cb8, cb8)


def _ffn_act_bwd(u0, u, cw8, d_a, name):
    t = u0.shape[0]
    tm = _ffn_tm(t)
    nt = t // tm

    def body(g0_ref, v0_ref, g_ref, gn_ref, v_ref, vn_ref, wg_ref, wv_ref, da_ref, dan_ref, du_ref, acc_ref):
        i = pl.program_id(0)

        @pl.when(i == 0)
        def _():
            acc_ref[...] = jnp.zeros_like(acc_ref)

        def grads(gate, val, da):
            return da * val * _silu_grad(gate), da * gate * _sigmoid(gate)

        dgate, dval = grads(g_ref[...].astype(F32), v_ref[...].astype(F32), da_ref[...].astype(F32))
        dgate_n, dval_n = grads(gn_ref[0:SUB, :].astype(F32), vn_ref[0:SUB, :].astype(F32),
                                dan_ref[0:SUB, :].astype(F32))
        last = i == nt - 1
        du_ref[:, 0:D_FF] = _conv_grads(dgate, jnp.where(last, 0.0, dgate_n), g0_ref[...].astype(F32), wg_ref,
                                        acc_ref, FFN_K, slice(0, D_FF)).astype(_ACT)
        du_ref[:, D_FF:2 * D_FF] = _conv_grads(dval, jnp.where(last, 0.0, dval_n), v0_ref[...].astype(F32), wv_ref,
                                               acc_ref, FFN_K, slice(D_FF, 2 * D_FF)).astype(_ACT)

    cur = lambda col: pl.BlockSpec((tm, D_FF), lambda i: (i, col))
    nxt = lambda col: _next_halo(tm, t, D_FF, lambda i: (i, col))
    par = lambda col: pl.BlockSpec((SUB, D_FF), lambda i: (0, col))
    return pl.pallas_call(
        body, name=name,
        out_shape=(jax.ShapeDtypeStruct((t, 2 * D_FF), _ACT), jax.ShapeDtypeStruct((SUB, 2 * D_FF), F32)),
        grid=(nt,),
        in_specs=[cur(0), cur(1), cur(0), nxt(0), cur(1), nxt(1), par(0), par(1), cur(0), nxt(0)],
        out_specs=(pl.BlockSpec((tm, 2 * D_FF), lambda i: (i, 0)),
                   pl.BlockSpec((SUB, 2 * D_FF), lambda i: (0, 0))),
        compiler_params=_cp(("arbitrary",)),
    )(u0, u0, u, u, u, u, cw8, cw8, d_a, d_a)


_REST = ("w_branch_ssm", "w_branch_attn", "w_mix_out", "w_up", "w_down")


def _mm_side(*args, side, **kw):
    if side is None:
        return _mm(*args, **kw), []
    return _mm(*args, side=side, **kw)


def _local_step(x, x_bf, target, wts, ex):
    t = x.shape[0]
    wp = wts["wp"]
    scw = wts["ssm_conv_w"]
    scb = wts["ssm_conv_b"]
    fcw8 = _rows8(wts["ffn_conv_w"])
    fcb8 = _rows8(wts["ffn_conv_b"])
    pad_lane = lambda p: jnp.concatenate([p.astype(F32), jnp.zeros((1, LANE - p.shape[1]), F32)], axis=1)
    dtb8 = _rows8(pad_lane(wts["ssm_dt_bias"]))
    alog8 = _rows8(pad_lane(wts["ssm_a_log"]))
    dsk8 = _rows8(pad_lane(wts["ssm_d"]))
    bias_table = jnp.concatenate([wts["rel_bias"].T.astype(F32), wts["attn_sinks"].T.astype(F32),
                                  jnp.zeros((A_HEADS, BIAS_ROWS - REL_BUCKETS - 1), F32)], axis=1)
    nw8 = _rows8(wts["ssm_norm_w"])
    bg8 = _rows8(wts["b_gate"])
    g1_8, b1_8, g2_8, b2_8 = (_rows8(wts[k]) for k in ("ln1_g", "ln1_b", "ln2_g", "ln2_b"))
    xs_w8, xs_b8 = _rows8(scw[:, :D_INNER]), _rows8(scb[:, :D_INNER])
    bc_w8, bc_b8 = _rows8(scw[:, D_INNER:]), _rows8(scb[:, D_INNER:])

    proj, stacks = _mm_side(x_bf, wp[:, :P_MAIN], "mm_in", out_dtype=_ACT, side=ex.gather_rest())
    wts = dict(wts, **ex.rest_weights(stacks))
    w_bs, w_ba, w_mix, w_up, w_dn = (wts[k] for k in _REST)
    tail = _mm(x_bf, wp[:, P_MAIN:], "mm_in_tail")
    xs_c, xs_pre = _conv_silu_fwd(proj, P_XS // _TC, D_INNER // _TC, xs_w8, xs_b8, "conv_xs_fwd")
    bc_c, bc_pre = _conv_silu_fwd(proj, P_BC // _TC, 1024 // _TC, bc_w8, bc_b8, "conv_bc_fwd")
    y_ssd, y_ssm, hprev = _ssd_fwd(xs_c, bc_c, proj, tail, dtb8, alog8, dsk8, nw8, "ssd_fwd")
    bias = _bias_expand(bias_table, "bias_expand").reshape(2, A_HEADS, WIN, 2 * WIN)
    y_attn = _attn_fwd(proj, tail, bias, "attn_fwd")
    bs = _mm(y_ssm, w_bs, "mm_bs", out_dtype=_ACT)
    ba = _mm(y_attn, w_ba, "mm_ba", out_dtype=_ACT)
    merged = _merge_fwd(bs, ba, proj, bg8, "merge_fwd")
    mix = _mm(merged, w_mix, "mm_mix", out_dtype=_ACT)
    xhat1, h1_bf, rstd1 = _ln1_fwd(x, mix, g1_8, b1_8, "ln1_fwd")
    u0 = _mm(h1_bf, w_up, "mm_up", out_dtype=_ACT)
    act, u_conv = _ffn_act_fwd(u0, fcw8, fcb8, "ffn_act_fwd")
    ffn = _mm(act, w_dn, "mm_down", out_dtype=_ACT)
    d_r2, d_r2_bf, acc_ln2 = _ln2_loss(xhat1, ffn, target, g1_8, b1_8, g2_8, b2_8, "ln2_loss")
    d_w_dn = _mm(act, d_r2_bf, "mm_dw_down", trans_a=True)
    d_act = _mm(d_r2_bf, w_dn.T, "mm_d_act", out_dtype=_ACT)
    d_u0, acc_ffn = _ffn_act_bwd(u0, u_conv, fcw8, d_act, "ffn_act_bwd")
    d_w_up = _mm(h1_bf, d_u0, "mm_dw_up", trans_a=True)
    d_h1_ffn = _mm(d_u0, w_up.T, "mm_d_h1", out_dtype=_ACT)
    d_r1, d_r1_bf, acc_ln1 = _ln1_bwd(d_r2, d_h1_ffn, xhat1, rstd1, g1_8, "ln1_bwd")
    d_w_mix = _mm(merged, d_r1_bf, "mm_dw_mix", trans_a=True)
    d_merged = _mm(d_r1_bf, w_mix.T, "mm_d_merged", out_dtype=_ACT)
    d_bs, d_ba, d_proj, acc_bg = _merge_bwd(d_merged, bs, ba, proj, bg8, "merge_bwd")
    d_w_bs = _mm(y_ssm, d_bs, "mm_dw_bs", trans_a=True)
    d_w_ba = _mm(y_attn, d_ba, "mm_dw_ba", trans_a=True)
    d_y_ssm = _mm(d_bs, w_bs.T, "mm_d_yssm", out_dtype=_ACT)
    d_y_attn = _mm(d_ba, w_ba.T, "mm_d_yattn", out_dtype=_ACT)
    d_proj, d_k, d_v, d_bias = _attn_bwd(proj, tail, bias, y_attn, d_y_attn, d_proj, "attn_bwd")
    d_table = _bias_reduce(d_bias.reshape(A_HEADS, WIN * 2 * WIN), "bias_reduce")
    d_xs_c, d_bc_c, d_proj, d_dt, acc_ssd, acc_nw = _ssd_bwd(
        d_y_ssm, y_ssd, xs_c, bc_c, proj, tail, hprev, dtb8, alog8, dsk8, nw8, d_proj, "ssd_bwd")
    d_proj, acc_xs = _conv_silu_bwd(proj, P_XS // _TC, D_INNER // _TC, xs_pre, xs_w8, d_xs_c, d_proj, "conv_xs_bwd")
    d_proj, acc_bc = _conv_silu_bwd(proj, P_BC // _TC, 1024 // _TC, bc_pre, bc_w8, d_bc_c, d_proj, "conv_bc_bwd")
    d_proj = _place_tail(d_k, d_v, d_dt, d_proj, "place_tail")
    grads = {"w_branch_ssm": d_w_bs, "w_branch_attn": d_w_ba, "w_mix_out": d_w_mix, "w_up": d_w_up, "w_down": d_w_dn}
    d_wp, landed_rest = _mm_side(x_bf, d_proj, "mm_dw_in", trans_a=True, side=ex.reduce_job(grads))
    d_x, landed_in = _mm_side(d_proj, wp.T, "mm_d_x", res=d_r1, res_scale=ALPHA, side=ex.reduce_job({"wp": d_wp}))
    grads.update({
        "wp": d_wp,
        "ssm_conv_w": jnp.concatenate([acc_xs[0:SSM_K], acc_bc[0:SSM_K]], axis=1),
        "ffn_conv_w": acc_ffn[0:FFN_K],
    })
    small = {
        "rel_bias": d_table[:, 0:REL_BUCKETS].T,
        "b_gate": acc_bg[0:1],
        "ssm_conv_b": jnp.concatenate([acc_xs[SSM_K:SSM_K + 1], acc_bc[SSM_K:SSM_K + 1]], axis=1),
        "ssm_dt_bias": acc_ssd[0:1, 0:N_HEADS], "ssm_a_log": acc_ssd[1:2, 0:N_HEADS], "ssm_d": acc_ssd[2:3, 0:N_HEADS],
        "ssm_norm_w": acc_nw[0:1],
        "attn_sinks": d_table[:, REL_BUCKETS:REL_BUCKETS + 1].T,
        "ln1_g": acc_ln1[0:1], "ln1_b": acc_ln1[1:2],
        "ffn_conv_b": acc_ffn[FFN_K:FFN_K + 1],
        "ln2_g": acc_ln2[0:1], "ln2_b": acc_ln2[1:2],
        "loss_lanes": acc_ln2[2:3],
    }
    return d_x, grads, small, landed_rest + landed_in


_MATS = (("w_in", (1024, 2120), 1), ("w_branch_ssm", (512, 1024), 0), ("w_branch_attn", (256, 1024), 0),
         ("w_mix_out", (256, 1024), 0), ("w_up", (1024, 1408), 1), ("w_down", (704, 1024), 0))
_CONVS = (("ssm_conv_w", (4, 768)), ("ffn_conv_w", (3, 1408)))
_CONV_ROWS = 64

_SMALL = (("rel_bias", (32, 16)), ("b_gate", (1, 2048)), ("ssm_conv_b", (1, 3072)), ("ssm_dt_bias", (1, 32)),
          ("ssm_a_log", (1, 32)), ("ssm_d", (1, 32)), ("ssm_norm_w", (1, 2048)), ("attn_sinks", (1, 16)),
          ("ln1_g", (1, 1024)), ("ln1_b", (1, 1024)), ("ffn_conv_b", (1, 5632)), ("ln2_g", (1, 1024)),
          ("ln2_b", (1, 1024)), ("g_ssm_conv_w", (4, 3072)), ("g_ffn_conv_w", (3, 5632)), ("loss_lanes", (1, 1024)))


def _small_rows(shape):
    rows = -(-(shape[0] * shape[1]) // LANE)
    return -(-rows // SUB) * SUB


def _as_rows(a, rows, dtype):
    flat = a.reshape(-1).astype(dtype)
    flat = jnp.concatenate([flat, jnp.zeros((rows * LANE - flat.shape[0],), dtype)])
    return flat.reshape(rows, LANE)


def _pack_small(parts):
    blocks = [_as_rows(parts[n], _small_rows(s), F32) if n in parts else jnp.zeros((_small_rows(s), LANE), F32)
              for n, s in _SMALL]
    return jnp.concatenate(blocks, axis=0)


def _unpack_small(packed):
    out, at = {}, 0
    for n, s in _SMALL:
        rows = _small_rows(s)
        out[n] = packed[at:at + rows].reshape(-1)[:s[0] * s[1]].reshape(s)
        at += rows
    return out


def _to_stack(full, shape, axis):
    if axis == 0:
        return full.reshape((N_CHIPS,) + shape)
    return jnp.transpose(full.reshape(shape[0], N_CHIPS, shape[1]), (1, 0, 2))


def _from_stack(stack, axis):
    n, r, c = stack.shape
    if axis == 0:
        return stack.reshape(n * r, c)
    return jnp.transpose(stack, (1, 0, 2)).reshape(r, n * c)


_IN_SHARD = IN_COLS // N_CHIPS


def _cols_of_stack(stack, o, w):
    parts = []
    while w > 0:
        j, a = divmod(o, _IN_SHARD)
        n = min(w, _IN_SHARD - a)
        parts.append(stack[j][:, a:a + n])
        o, w = o + n, w - n
    return parts


def _pack_w_in_stack(stack):
    cols, at = [], 0
    for o, w, pk in sorted(_PIECES, key=lambda p: p[2]):
        if pk > at:
            cols.append(jnp.zeros((stack.shape[1], pk - at), stack.dtype))
        cols += _cols_of_stack(stack, o, w)
        at = pk + w
    cols.append(jnp.zeros((stack.shape[1], P_W - at), stack.dtype))
    return jnp.concatenate(cols, axis=1)


def _unpack_w_in_stack(wp):
    slabs = []
    for j in range(N_CHIPS):
        lo, hi = j * _IN_SHARD, (j + 1) * _IN_SHARD
        cols = []
        for o, w, pk in sorted(_PIECES):
            a, b = max(o, lo), min(o + w, hi)
            if a < b:
                cols.append(wp[:, pk + a - o:pk + b - o])
        slabs.append(jnp.concatenate(cols, axis=1))
    return jnp.stack(slabs)


_MESH = pl.DeviceIdType.MESH
_HBM = pl.BlockSpec(memory_space=pltpu.HBM)


def _position():
    return lax.axis_index("x"), lax.axis_index("y"), lax.axis_index("c")


def _other_chips(x, y):
    return ((1 - x, y), (x, 1 - y), (1 - x, 1 - y))


def _remote(src, dst, send_sem, recv_sem, to):
    return pltpu.make_async_remote_copy(src_ref=src, dst_ref=dst, send_sem=send_sem, recv_sem=recv_sem,
                                        device_id=to, device_id_type=_MESH)


def _cast_rows(x, dtype, name, side):
    t, cols = x.shape
    tm = min(1024, t)
    nt = t // tm
    ns_in, ns_out = len(side.inputs), len(side.out_shape)

    def body(*refs):
        x_ref, o_ref = refs[0], refs[1 + ns_in]
        job_refs = (refs[1:1 + ns_in], refs[2 + ns_in:2 + ns_in + ns_out], refs[2 + ns_in + ns_out:])
        i = pl.program_id(0)

        @pl.when(i == 0)
        def _():
            side.start(*job_refs)

        o_ref[...] = x_ref[...].astype(dtype)

        @pl.when(i == nt - 1)
        def _():
            side.finish(*job_refs)

    blk = pl.BlockSpec((tm, cols), lambda i: (i, 0))
    outs = pl.pallas_call(
        body, name=name, out_shape=[jax.ShapeDtypeStruct((t, cols), dtype)] + list(side.out_shape), grid=(nt,),
        in_specs=[blk] + [_HBM] * ns_in, out_specs=[blk] + [_HBM] * ns_out, scratch_shapes=list(side.sems),
        compiler_params=_cp(("arbitrary",)),
    )(x, *side.inputs)
    return outs[0], list(outs[1:])


_GATHER_COPIES = 7


def _gather_job(shards):
    n = len(shards)

    def plan(s_refs, o_refs, sems):
        send_sems, recv_sems = sems
        x, y, c = _position()
        me = 2 * x + y
        sib = (x, y, 1 - c)
        chips = _other_chips(x, y)

        def copy(m, k, dst, to, src=None):
            at = _GATHER_COPIES * m + k
            return _remote(dst if src is None else src, dst, send_sems.at[at], recv_sems.at[at], to)

        def half(m, chip_idx, h):
            return o_refs[m].at[chip_idx, h]

        own = [copy(m, 6, o_refs[m].at[me], sib, src=s_refs[m]) for m in range(n)]
        first = [copy(m, i, half(m, me, c), (cx, cy, c), src=s_refs[m].at[c])
                 for i, (cx, cy) in enumerate(chips) for m in range(n)]
        return c, sib, chips, copy, half, own, first

    def start(s_refs, o_refs, sems):
        _, _, _, _, _, own, first = plan(s_refs, o_refs, sems)
        for cp in first + own:
            cp.start()

    def finish(s_refs, o_refs, sems):
        c, sib, chips, copy, half, own, first = plan(s_refs, o_refs, sems)
        passed = []
        for i, (cx, cy) in enumerate(chips):
            for m in range(n):
                copy(m, i, half(m, 2 * cx + cy, c), sib).wait_recv()
                passed.append(copy(m, 3 + i, half(m, 2 * cx + cy, c), sib))
                passed[-1].start()
        for i, (cx, cy) in enumerate(chips):
            for m in range(n):
                copy(m, 3 + i, half(m, 2 * cx + cy, 1 - c), sib).wait_recv()
        for cp in first + passed:
            cp.wait_send()
        for cp in own:
            cp.wait()

    return _SideJob(
        inputs=list(shards), out_shape=[jax.ShapeDtypeStruct((N_CHIPS,) + s.shape, s.dtype) for s in shards],
        sems=[pltpu.SemaphoreType.DMA((_GATHER_COPIES * n,)), pltpu.SemaphoreType.DMA((_GATHER_COPIES * n,))],
        start=start, finish=finish)


def _swap_halves(gs, name):
    n = len(gs)

    def body(*refs):
        g_refs, o_refs = refs[:n], refs[n:2 * n]
        send_sems, recv_sems = refs[2 * n:]
        x, y, c = _position()
        cps = [_remote(g_refs[m].at[j, 1 - c], o_refs[m].at[j], send_sems.at[N_CHIPS * m + j],
                       recv_sems.at[N_CHIPS * m + j], (x, y, 1 - c)) for m in range(n) for j in range(N_CHIPS)]
        for cp in cps:
            cp.start()
        for cp in cps:
            cp.wait()

    return pl.pallas_call(
        body, name=name,
        out_shape=[jax.ShapeDtypeStruct((N_CHIPS,) + g.shape[2:], g.dtype) for g in gs],
        in_specs=[_HBM] * n, out_specs=[_HBM] * n,
        scratch_shapes=[pltpu.SemaphoreType.DMA((N_CHIPS * n,)), pltpu.SemaphoreType.DMA((N_CHIPS * n,))],
    )(*gs)


def _scatter_job(ps):
    n = len(ps)

    def copies(p_refs, o_refs, sems):
        send_sems, recv_sems = sems
        x, y, c = _position()
        return [_remote(p_refs[m].at[2 * cx + cy], o_refs[m].at[i], send_sems.at[3 * m + i], recv_sems.at[3 * m + i],
                        (cx, cy, c)) for i, (cx, cy) in enumerate(_other_chips(x, y)) for m in range(n)]

    def start(*parts):
        for cp in copies(*parts):
            cp.start()

    def finish(*parts):
        for cp in copies(*parts):
            cp.wait()

    return _SideJob(
        inputs=list(ps), out_shape=[jax.ShapeDtypeStruct((N_CHIPS - 1,) + p.shape[1:], p.dtype) for p in ps],
        sems=[pltpu.SemaphoreType.DMA((3 * n,)), pltpu.SemaphoreType.DMA((3 * n,))], start=start, finish=finish)


def _join_halves(fulls):
    n = len(fulls)

    def body(*refs):
        o_refs = refs[n:2 * n]
        send_sems, recv_sems = refs[2 * n:]
        x, y, c = _position()
        cps = [_remote(o_refs[m].at[c], o_refs[m].at[c], send_sems.at[m], recv_sems.at[m], (x, y, 1 - c))
               for m in range(n)]
        for cp in cps:
            cp.start()
        for cp in cps:
            cp.wait()

    return pl.pallas_call(
        body, name="join_halves",
        out_shape=[jax.ShapeDtypeStruct(f.shape, f.dtype) for f in fulls],
        in_specs=[_HBM] * n, out_specs=[_HBM] * n, input_output_aliases={m: m for m in range(n)},
        scratch_shapes=[pltpu.SemaphoreType.DMA((n,)), pltpu.SemaphoreType.DMA((n,))],
    )(*fulls)


def _allgather_small(mine, name):
    m_per, n = mine.shape

    def body(x_ref, out_ref, send_sems, recv_sems, local_sem):
        x, y, c = _position()
        me, sibling = (x, y, c), (x, y, 1 - c)
        chips = _other_chips(x, y)

        def rows(px, py, pc):
            return out_ref.at[pl.ds((4 * px + 2 * py + pc) * m_per, m_per), :]

        def copy(k, block, to, src=None):
            return pltpu.make_async_remote_copy(src_ref=rows(*block) if src is None else src, dst_ref=rows(*block),
                                                send_sem=send_sems.at[k], recv_sem=recv_sems.at[k],
                                                device_id=to, device_id_type=_MESH)

        own = pltpu.make_async_copy(x_ref, rows(*me), local_sem)
        own.start()
        first = [copy(0, me, sibling, src=x_ref)]
        first += [copy(1 + j, me, (*chip, c), src=x_ref) for j, chip in enumerate(chips)]
        for cp in first:
            cp.start()
        passed = [copy(4 + j, (*chip, c), sibling) for j, chip in enumerate(chips)]
        for j, chip in enumerate(chips):
            copy(1 + j, (*chip, c), me).wait_recv()
            passed[j].start()
        copy(0, sibling, me).wait_recv()
        for j, chip in enumerate(chips):
            copy(4 + j, (*chip, 1 - c), me).wait_recv()
        for cp in first + passed:
            cp.wait_send()
        own.wait()

    return pl.pallas_call(
        body, name=name, out_shape=jax.ShapeDtypeStruct((N_DEV * m_per, n), mine.dtype),
        in_specs=[pl.BlockSpec(memory_space=pltpu.VMEM)], out_specs=pl.BlockSpec(memory_space=pltpu.VMEM),
        scratch_shapes=[pltpu.SemaphoreType.DMA((7,)), pltpu.SemaphoreType.DMA((7,)), pltpu.SemaphoreType.DMA],
    )(mine)


_ADD_BLOCK_BYTES = 3 << 20


def _add_rows(hr, cols):
    if hr * cols * 4 <= _ADD_BLOCK_BYTES:
        return hr
    return _pick(hr, (256, 128, 64, 32, 16))


def _add_own_half(g, recv, c_idx, name):
    nseg, _, hr, cols = g.shape
    tr = _add_rows(hr, cols)

    def body(c_ref, g_ref, r_ref, o_ref, ob_ref):
        s = g_ref[...] + r_ref[...]
        o_ref[...] = s
        ob_ref[...] = s.astype(jnp.bfloat16)

    blk = pl.BlockSpec((None, tr, cols), lambda j, i, c_ref: (j, i, 0))
    return pl.pallas_call(
        body, name=name,
        out_shape=(jax.ShapeDtypeStruct((nseg, hr, cols), F32), jax.ShapeDtypeStruct((nseg, hr, cols), jnp.bfloat16)),
        grid_spec=pltpu.PrefetchScalarGridSpec(
            num_scalar_prefetch=1, grid=(nseg, hr // tr),
            in_specs=[pl.BlockSpec((None, None, tr, cols), lambda j, i, c_ref: (j, c_ref[0], i, 0)), blk],
            out_specs=(blk, blk)),
        compiler_params=_cp(("parallel", "parallel")),
    )(c_idx, g, recv)


def _add_chips(p, recv, chip_idx, c_idx, name):
    _, hr, cols = p.shape
    tr = _add_rows(hr, cols)

    def body(j_ref, c_ref, p_ref, r_ref, o_ref):
        o_ref[...] = ((p_ref[...] + r_ref[0].astype(F32)) + r_ref[1].astype(F32)) + r_ref[2].astype(F32)

    return pl.pallas_call(
        body, name=name, out_shape=jax.ShapeDtypeStruct((2, hr, cols), F32),
        grid_spec=pltpu.PrefetchScalarGridSpec(
            num_scalar_prefetch=2, grid=(hr // tr,),
            in_specs=[pl.BlockSpec((None, tr, cols), lambda i, j_ref, c_ref: (j_ref[0], i, 0)),
                      pl.BlockSpec((N_CHIPS - 1, tr, cols), lambda i, j_ref, c_ref: (0, i, 0))],
            out_specs=pl.BlockSpec((None, tr, cols), lambda i, j_ref, c_ref: (c_ref[0], i, 0))),
        compiler_params=_cp(("parallel",)),
    )(chip_idx, c_idx, p, recv)


def _adam_math(w, g, m, v):
    m = ADAM_B1 * m + (1.0 - ADAM_B1) * g
    v = ADAM_B2 * v + (1.0 - ADAM_B2) * (g * g)
    m_hat = m / (1.0 - ADAM_B1 ** ADAM_STEP)
    v_hat = v / (1.0 - ADAM_B2 ** ADAM_STEP)
    delta = -ADAM_LR * (m_hat / (jnp.sqrt(v_hat) + ADAM_EPS) + ADAM_WD * w)
    return delta, m, v


def _adam_big(w, g, m, v, name):
    rows, cols = w.shape
    tr = _pick(rows, (256, 128, 64, 32, 16, 8)) if rows % SUB == 0 else rows

    def body(w_ref, g_ref, m_ref, v_ref, d_ref, mo_ref, vo_ref):
        d_ref[...], mo_ref[...], vo_ref[...] = _adam_math(w_ref[...], g_ref[...], m_ref[...], v_ref[...])

    blk = pl.BlockSpec((tr, cols), lambda i: (i, 0))
    shp = jax.ShapeDtypeStruct((rows, cols), F32)
    return pl.pallas_call(
        body, name=name, out_shape=(shp, shp, shp), grid=(rows // tr,),
        in_specs=[blk, blk, blk, blk], out_specs=(blk, blk, blk), compiler_params=_cp(("parallel",)),
    )(w, g, m, v)


def _adam_small(w, gathered, m, v):
    rows = w.shape[0]

    def body(w_ref, a_ref, m_ref, v_ref, g_ref, d_ref, mo_ref, vo_ref):
        g = a_ref[0:rows, :]
        for k in range(1, N_DEV):
            g = g + a_ref[k * rows:(k + 1) * rows, :]
        g_ref[...] = g
        d_ref[...], mo_ref[...], vo_ref[...] = _adam_math(w_ref[...], g, m_ref[...], v_ref[...])

    shp = jax.ShapeDtypeStruct((rows, LANE), F32)
    return pl.pallas_call(body, name="adam_small", out_shape=(shp, shp, shp, shp), compiler_params=_cp(None))(
        w, gathered, m, v)


_WEIGHTS = ("rel_bias", "w_in", "b_gate", "ssm_conv_w", "ssm_conv_b", "ssm_dt_bias", "ssm_a_log", "ssm_d",
            "ssm_norm_w", "attn_sinks", "w_branch_ssm", "w_branch_attn", "w_mix_out", "ln1_g", "ln1_b", "w_up",
            "ffn_conv_w", "ffn_conv_b", "w_down", "ln2_g", "ln2_b")
_REPLICATED = tuple(n for n, _ in _SMALL[:13])


class _Exchange:
    def __init__(self, w, chip, core):
        self.chip = chip
        self.c_idx = jnp.reshape(core, (1,)).astype(jnp.int32)
        self.chip_idx = jnp.reshape(chip, (1,)).astype(jnp.int32)
        self.shards = {n: w[n].astype(jnp.bfloat16).reshape(2, s[0] // 2, s[1]) for n, s, _ in _MATS}
        self.spec = {n: (s, ax) for n, s, ax in _MATS}
        self.sums = {}

    def cast_and_gather_w_in(self, x):
        x_act, (stack,) = _cast_rows(x, _ACT, "cast_x", _gather_job([self.shards["w_in"]]))
        return x_act, _pack_w_in_stack(stack.reshape((N_CHIPS,) + self.spec["w_in"][0]))

    def gather_rest(self):
        return _gather_job([self.shards[n] for n in _REST])

    def rest_weights(self, stacks):
        return {n: _from_stack(st.reshape((N_CHIPS,) + self.spec[n][0]), self.spec[n][1])
                for n, st in zip(_REST, stacks)}

    def reduce_job(self, grads):
        names, stacks = [], []
        for n, g in grads.items():
            name = "w_in" if n == "wp" else n
            s, ax = self.spec[name]
            st = _unpack_w_in_stack(g) if n == "wp" else _to_stack(g, s, ax)
            names.append(name)
            stacks.append(st.reshape(N_CHIPS, 2, s[0] // 2, s[1]))
        swapped = _swap_halves(stacks, "swap_" + names[0])
        halves = []
        for n, g, r in zip(names, stacks, swapped):
            self.sums[n], bf = _add_own_half(g, r, self.c_idx, "add_own_" + n)
            halves.append(bf)
        return _scatter_job(halves)

    def reduced(self, landed):
        names = list(self.sums)
        reds = [_add_chips(self.sums[n], r, self.chip_idx, self.c_idx, "add_chips_" + n)
                for n, r in zip(names, landed)]
        return {n: g.reshape(self.spec[n][0]) for n, g in zip(names, _join_halves(reds))}


def _step(x, target, w, m, v):
    xi, yi, ci = _position()
    chip = 2 * xi + yi
    ex = _Exchange(w, chip, ci)

    wts = {n: w[n] for n in _REPLICATED}
    x_act, wts["wp"] = ex.cast_and_gather_w_in(x)
    taps = jnp.concatenate([w[n].astype(F32).reshape(-1) for n, _ in _CONVS])
    taps = _allgather_small(_as_rows(taps, _CONV_ROWS, F32), "allgather_taps")
    taps = taps.reshape(N_CHIPS, 2, _CONV_ROWS * LANE)[:, 0]
    at = 0
    for n, s in _CONVS:
        wts[n] = _from_stack(taps[:, at:at + s[0] * s[1]].reshape((N_CHIPS,) + s), 1)
        at += s[0] * s[1]

    d_x, grads, small, landed = _local_step(x, x_act, target, wts, ex)

    outs = {"grad": ex.reduced(landed), "delta": {}, "m": {}, "v": {}}

    small = dict(small, g_ssm_conv_w=grads["ssm_conv_w"], g_ffn_conv_w=grads["ffn_conv_w"])
    all_small = _allgather_small(_pack_small(small), "allgather_small")
    packs = [_pack_small({n: d[n] for n in _REPLICATED}) for d in (w, m, v)]
    g_s, d_s, m_s, v_s = (_unpack_small(a) for a in _adam_small(packs[0], all_small, packs[1], packs[2]))
    for kind, part in (("grad", g_s), ("delta", d_s), ("m", m_s), ("v", v_s)):
        outs[kind].update({n: part[n] for n in _REPLICATED})
    for n, s in _CONVS:
        outs["grad"][n] = lax.dynamic_slice_in_dim(g_s["g_" + n], chip * s[1], s[1], axis=1)
    for n in [n for n, _, _ in _MATS] + [n for n, _ in _CONVS]:
        outs["delta"][n], outs["m"][n], outs["v"][n] = _adam_big(
            w[n].astype(F32), outs["grad"][n], m[n].astype(F32), v[n].astype(F32), "adam_" + n)
    loss = (0.5 / D_MODEL) * jnp.sum(g_s["loss_lanes"])
    return loss, d_x, outs


def kernel(x, rel_bias, w_in, b_gate, ssm_conv_w, ssm_conv_b, ssm_dt_bias, ssm_a_log, ssm_d, ssm_norm_w, attn_sinks, w_branch_ssm, w_branch_attn, w_mix_out, ln1_g, ln1_b, w_up, ffn_conv_w, ffn_conv_b, w_down, ln2_g, ln2_b, loss_target, m_rel_bias, m_w_in, m_b_gate, m_ssm_conv_w, m_ssm_conv_b, m_ssm_dt_bias, m_ssm_a_log, m_ssm_d, m_ssm_norm_w, m_attn_sinks, m_w_branch_ssm, m_w_branch_attn, m_w_mix_out, m_ln1_g, m_ln1_b, m_w_up, m_ffn_conv_w, m_ffn_conv_b, m_w_down, m_ln2_g, m_ln2_b, v_rel_bias, v_w_in, v_b_gate, v_ssm_conv_w, v_ssm_conv_b, v_ssm_dt_bias, v_ssm_a_log, v_ssm_d, v_ssm_norm_w, v_attn_sinks, v_w_branch_ssm, v_w_branch_attn, v_w_mix_out, v_ln1_g, v_ln1_b, v_w_up, v_ffn_conv_w, v_ffn_conv_b, v_w_down, v_ln2_g, v_ln2_b):
    given = dict(locals())
    drop = lambda a, n: a if n == "rel_bias" or a.ndim == 2 else a[0]
    w = {n: drop(given[n], n) for n in _WEIGHTS}
    m = {n: drop(given["m_" + n], n) for n in _WEIGHTS}
    v = {n: drop(given["v_" + n], n) for n in _WEIGHTS}
    loss, d_x, outs = _step(x[0], loss_target[0], w, m, v)
    like = lambda a, n: a.reshape(given[n].shape)
    res = [loss, d_x[None]]
    for kind in ("grad", "delta", "m", "v"):
        res += [like(outs[kind][n], n) for n in _WEIGHTS]
    return tuple(res)
```

```python
import math
from typing import NamedTuple

import numpy as np
import jax
import jax.numpy as jnp
from jax import lax
from jax.experimental import pallas as pl
from jax.experimental.pallas import tpu as pltpu

F32 = jnp.float32
_ACT = jnp.bfloat16
_MXU = jnp.bfloat16

D_MODEL = 1024
D_INNER = 2048
N_HEADS = 32
HEAD_P = 64
N_GROUPS = 4
N_STATE = 128
CHUNK = 128
SSM_K = 4
A_HEADS = 16
A_DH = 64
WIN = 128
REL_BUCKETS = 32
BIAS_ROWS = 64
D_FF = 2816
FFN_K = 3
ALPHA = 2.0 ** 0.25
LN_EPS = 1e-5
RMS_EPS = 1e-5
IN_COLS = 8480
NEG = -1e30

ADAM_LR = 0.001
ADAM_B1 = 0.9
ADAM_B2 = 0.999
ADAM_EPS = 1e-08
ADAM_WD = 0.01
ADAM_STEP = 10

LANE = 128
SUB = 8

P_Z, P_XS, P_G, P_Q, P_BC, P_K, P_V, P_DT = 0, 2048, 4096, 6144, 7168, 8192, 8320, 8448
P_W = 8704
P_MAIN = 8192
T_K, T_V, T_DT = P_K - P_MAIN, P_V - P_MAIN, P_DT - P_MAIN
_PIECES = ((0, 2048, P_Z), (2048, 2048, P_XS), (4096, 1024, P_BC), (5120, 32, P_DT), (5152, 1024, P_Q),
           (6176, 128, P_K), (6304, 128, P_V), (6432, 2048, P_G))

N_CHIPS = 4
N_DEV = 8


def _cp(sem=None, vmem_mb=48):
    return pltpu.CompilerParams(dimension_semantics=sem, vmem_limit_bytes=vmem_mb * 1024 * 1024)


def _pick(n, cands):
    for c in cands:
        if n % c == 0:
            return c
    raise ValueError(f"no block size for {n}")


def _rows8(p):
    k, c = p.shape
    return jnp.concatenate([p.astype(F32), jnp.zeros((SUB - k, c), F32)], axis=0)


class _SideJob(NamedTuple):
    inputs: list
    out_shape: list
    sems: list
    start: object
    finish: object


def _mm(a, b, name, *, trans_a=False, out_dtype=F32, res=None, res_scale=1.0, side=None):
    if trans_a:
        k_dim, m = a.shape
    else:
        m, k_dim = a.shape
    k2, n = b.shape
    assert k_dim == k2, (a.shape, b.shape)
    tm = _pick(m, (1408, 1024, 512, 256, 128))
    tn = _pick(n, (1408, 1024, 512, 256, 128))
    tk = _pick(k_dim, (2816, 2176, 2048, 1024, 512, 256, 128))
    nk = k_dim // tk
    grid = (m // tm, n // tn, nk)
    dn = (((0,), (0,)), ((), ())) if trans_a else (((1,), (0,)), ((), ()))
    n_in = 2 if res is None else 3
    ns_in = len(side.inputs) if side else 0
    ns_out = len(side.out_shape) if side else 0

    def body(*refs):
        a_ref, b_ref = refs[0], refs[1]
        o_ref = refs[n_in + ns_in]
        scratch = refs[n_in + ns_in + 1 + ns_out:]
        job_refs = (refs[n_in:n_in + ns_in], refs[n_in + ns_in + 1:n_in + ns_in + 1 + ns_out],
                    scratch[1:] if nk > 1 else scratch)
        i, j, k = pl.program_id(0), pl.program_id(1), pl.program_id(2)

        def finish(r):
            if res is not None:
                r = r + res_scale * refs[2][...]
            o_ref[...] = r.astype(out_dtype)

        if side:
            @pl.when(jnp.logical_and(jnp.logical_and(i == 0, j == 0), k == 0))
            def _():
                side.start(*job_refs)

        part = lax.dot_general(a_ref[...].astype(_MXU), b_ref[...].astype(_MXU), dn, preferred_element_type=F32)
        if nk == 1:
            finish(part)
        else:
            acc = scratch[0]

            @pl.when(k == 0)
            def _():
                acc[...] = part

            @pl.when(k > 0)
            def _():
                acc[...] += part

            @pl.when(k == nk - 1)
            def _():
                finish(acc[...])

        if side:
            @pl.when(jnp.logical_and(jnp.logical_and(i == grid[0] - 1, j == grid[1] - 1), k == nk - 1))
            def _():
                side.finish(*job_refs)

    if trans_a:
        a_spec = pl.BlockSpec((tk, tm), lambda i, j, k: (k, i))
    else:
        a_spec = pl.BlockSpec((tm, tk), lambda i, j, k: (i, k))
    in_specs = [a_spec, pl.BlockSpec((tk, tn), lambda i, j, k: (k, j))]
    args = [a, b]
    if res is not None:
        in_specs.append(pl.BlockSpec((tm, tn), lambda i, j, k: (i, j)))
        args.append(res)
    out_spec = pl.BlockSpec((tm, tn), lambda i, j, k: (i, j))
    out_shape = jax.ShapeDtypeStruct((m, n), out_dtype)
    scratch_shapes = [pltpu.VMEM((tm, tn), F32)] if nk > 1 else []
    if not side:
        return pl.pallas_call(
            body, name=name, out_shape=out_shape, grid=grid, in_specs=in_specs, out_specs=out_spec,
            scratch_shapes=scratch_shapes, compiler_params=_cp(("parallel", "parallel", "arbitrary")),
        )(*args)
    hbm = pl.BlockSpec(memory_space=pltpu.HBM)
    outs = pl.pallas_call(
        body, name=name, out_shape=[out_shape] + list(side.out_shape), grid=grid,
        in_specs=in_specs + [hbm] * ns_in, out_specs=[out_spec] + [hbm] * ns_out,
        scratch_shapes=scratch_shapes + list(side.sems),
        compiler_params=_cp(("arbitrary", "arbitrary", "arbitrary")),
    )(*args, *side.inputs)
    return outs[0], list(outs[1:])


def _shift_down(cur, prev8, s):
    r = pltpu.roll(cur, s, 0)
    p = pltpu.roll(prev8, s, 0)
    row8 = lax.broadcasted_iota(jnp.int32, (SUB, 1), 0)
    fixed = jnp.where(row8 < s, p, r[0:SUB])
    if cur.shape[0] == SUB:
        return fixed
    return jnp.concatenate([fixed, r[SUB:]], axis=0)


def _shift_up(cur, next8, s):
    tm = cur.shape[0]
    r = pltpu.roll(cur, tm - s, 0)
    p = pltpu.roll(next8, SUB - s, 0)
    row8 = lax.broadcasted_iota(jnp.int32, (SUB, 1), 0)
    fixed = jnp.where(row8 >= SUB - s, p, r[tm - SUB:])
    return jnp.concatenate([r[:tm - SUB], fixed], axis=0)


def _conv_pre(cur, prev8, w_ref, b_row, taps):
    acc = cur * w_ref[taps - 1:taps, :] + b_row
    for s in range(1, taps):
        acc = acc + _shift_down(cur, prev8, s) * w_ref[taps - 1 - s:taps - s, :]
    return acc


def _dot01_r(x, m01, parts=3):
    acc = None
    r = x
    for _ in range(parts):
        hi = r.astype(jnp.bfloat16)
        t = jnp.dot(hi, m01, preferred_element_type=F32)
        acc = t if acc is None else acc + t
        r = r - hi.astype(F32)
    return acc


def _dot01_l(m01, x, parts=3):
    acc = None
    r = x
    for _ in range(parts):
        hi = r.astype(jnp.bfloat16)
        t = jnp.dot(m01, hi, preferred_element_type=F32)
        acc = t if acc is None else acc + t
        r = r - hi.astype(F32)
    return acc


def _dot(a, b):
    return jnp.dot(a.astype(_MXU), b.astype(_MXU), preferred_element_type=F32)


def _dot_nt(a, b):
    return lax.dot_general(a.astype(_MXU), b.astype(_MXU), (((1,), (1,)), ((), ())), preferred_element_type=F32)


def _dot_tn(a, b):
    return lax.dot_general(a.astype(_MXU), b.astype(_MXU), (((0,), (0,)), ((), ())), preferred_element_type=F32)


def _sigmoid(x):
    return 1.0 / (1.0 + jnp.exp(-x))


def _half_masks():
    lane = lax.broadcasted_iota(jnp.int32, (1, LANE), 1)
    lo = (lane < 64).astype(F32)
    return lo, 1.0 - lo


_TC = 1024


def _tm_rows(t):
    return min(512, t)


HALO = 16


def _prev_halo(tm, width, pos):
    def index(*ids):
        i, col = pos(*ids)
        return (jnp.maximum(i * (tm // HALO) - 1, 0), col)
    return pl.BlockSpec((HALO, width), index)


def _next_halo(tm, t, width, pos):
    def index(*ids):
        i, col = pos(*ids)
        return (jnp.minimum((i + 1) * (tm // HALO), t // HALO - 1), col)
    return pl.BlockSpec((HALO, width), index)


def _conv_silu_fwd(proj, colblk0, nblk, w8, b8, name):
    t = proj.shape[0]
    tm = _tm_rows(t)

    def body(c_ref, p_ref, w_ref, b_ref, o_ref, pre_ref):
        i = pl.program_id(1)
        prev8 = jnp.where(i > 0, p_ref[SUB:HALO, :].astype(F32), 0.0)
        pre = _conv_pre(c_ref[...].astype(F32), prev8, w_ref, b_ref[0:1, :], SSM_K)
        o_ref[...] = pre * _sigmoid(pre)
        pre_ref[...] = pre.astype(_ACT)

    blk = pl.BlockSpec((tm, _TC), lambda j, i: (i, j))
    return pl.pallas_call(
        body, name=name,
        out_shape=(jax.ShapeDtypeStruct((t, nblk * _TC), F32), jax.ShapeDtypeStruct((t, nblk * _TC), _ACT)),
        grid=(nblk, t // tm),
        in_specs=[pl.BlockSpec((tm, _TC), lambda j, i: (i, colblk0 + j)),
                  _prev_halo(tm, _TC, lambda j, i: (i, colblk0 + j)),
                  pl.BlockSpec((SUB, _TC), lambda j, i: (0, j)),
                  pl.BlockSpec((SUB, _TC), lambda j, i: (0, j))],
        out_specs=(blk, blk),
        compiler_params=_cp(("parallel", "parallel")),
    )(proj, proj, w8, b8)


def _silu_grad(pre):
    sg = _sigmoid(pre)
    return sg * (1.0 + pre * (1.0 - sg))


def _conv_grads(d, d_next8, cur, w_ref, acc_ref, taps, cols=slice(None)):
    ones = jnp.ones((SUB, d.shape[0]), _MXU)

    def colsum(v):
        return jnp.dot(ones, v.astype(_MXU), preferred_element_type=F32)[0:1, :]

    du = d * w_ref[taps - 1:taps, :]
    acc_ref[taps:taps + 1, cols] += colsum(d)
    acc_ref[taps - 1:taps, cols] += colsum(d * cur)
    for s in range(1, taps):
        up = _shift_up(d, d_next8, s)
        du = du + up * w_ref[taps - 1 - s:taps - s, :]
        acc_ref[taps - 1 - s:taps - s, cols] += colsum(up * cur)
    return du


def _conv_silu_bwd(proj, colblk0, nblk, pre, w8, d_out, d_proj, name):
    t = proj.shape[0]
    tm = _tm_rows(t)
    nt = t // tm

    def body(c_ref, pre_ref, pren_ref, w_ref, d_ref, dn_ref, _, du_ref, acc_ref):
        i = pl.program_id(1)

        @pl.when(i == 0)
        def _():
            acc_ref[...] = jnp.zeros_like(acc_ref)

        dpre = d_ref[...].astype(F32) * _silu_grad(pre_ref[...].astype(F32))
        dpre_n = jnp.where(i < nt - 1, dn_ref[0:SUB, :].astype(F32) * _silu_grad(pren_ref[0:SUB, :].astype(F32)), 0.0)
        du_ref[...] = _conv_grads(dpre, dpre_n, c_ref[...].astype(F32), w_ref, acc_ref, SSM_K).astype(_ACT)

    c = nblk * _TC
    blk = pl.BlockSpec((tm, _TC), lambda j, i: (i, j))
    nxt = _next_halo(tm, t, _TC, lambda j, i: (i, j))
    par = pl.BlockSpec((SUB, _TC), lambda j, i: (0, j))
    return pl.pallas_call(
        body, name=name,
        out_shape=(jax.ShapeDtypeStruct(d_proj.shape, d_proj.dtype), jax.ShapeDtypeStruct((SUB, c), F32)),
        grid=(nblk, nt),
        in_specs=[pl.BlockSpec((tm, _TC), lambda j, i: (i, colblk0 + j)), blk, nxt, par, blk, nxt,
                  pl.BlockSpec(memory_space=pl.ANY)],
        out_specs=(pl.BlockSpec((tm, _TC), lambda j, i: (i, colblk0 + j)), par),
        input_output_aliases={6: 0},
        compiler_params=_cp(("parallel", "arbitrary")),
    )(proj, pre, pre, w8, d_out, d_out, d_proj)


def _expand_consts():
    e = np.zeros((LANE, D_INNER), np.float32)
    for h in range(N_HEADS):
        e[h, h * HEAD_P:(h + 1) * HEAD_P] = 1.0
    return jnp.asarray(e, jnp.bfloat16), jnp.asarray(e.T.copy(), jnp.bfloat16)


def _ssd_common(dtr_ref, dtb_ref, alog_ref, e_ref):
    lane = lax.broadcasted_iota(jnp.int32, (1, LANE), 1)
    hm = lane < N_HEADS
    pre = dtr_ref[...] + dtb_ref[0:1, :]
    dt = jnp.where(hm, jnp.maximum(pre, 0.0) + jnp.log(1.0 + jnp.exp(-jnp.abs(pre))), 0.0)
    a_row = jnp.where(hm, -jnp.exp(alog_ref[0:1, :]), 0.0)
    adt = dt * a_row
    r = lax.broadcasted_iota(jnp.int32, (CHUNK, CHUNK), 0)
    c = lax.broadcasted_iota(jnp.int32, (CHUNK, CHUNK), 1)
    causal = r >= c
    acs = _dot01_l(causal.astype(jnp.bfloat16), adt)
    e = e_ref[...]
    acs_x = _dot01_r(acs, e, parts=2)
    dt_x = _dot01_r(dt, e, parts=2)
    return pre, dt, a_row, acs, acs_x, dt_x, causal, hm


def _decay(acs, acs_t, h, causal):
    seg = acs[:, h:h + 1] - acs_t[h:h + 1, :]
    return jnp.exp(jnp.where(causal, seg, NEG))


def _ssd_fwd(xs_c, bc_c, proj, tail, dtb8, alog8, dsk8, nw8, name):
    t = xs_c.shape[0]
    nc = t // CHUNK
    e_bf, _ = _expand_consts()
    gw = D_INNER // N_GROUPS
    per_step = 2 if nc % 2 == 0 else 1

    def body(xs_ref, bc_ref, dtr_ref, z_ref, dtb_ref, alog_ref, dsk_ref, nw_ref, e_ref,
             y_ref, ys_ref, hp_ref, h_ref):
        @pl.when(pl.program_id(0) == 0)
        def _():
            h_ref[...] = jnp.zeros_like(h_ref)

        for s in range(per_step):
            rows = pl.ds(s * CHUNK, CHUNK)
            chunk(xs_ref.at[rows], bc_ref.at[rows], dtr_ref.at[rows], z_ref.at[rows], dtb_ref, alog_ref, dsk_ref,
                  nw_ref, e_ref, y_ref.at[rows], ys_ref.at[rows], hp_ref.at[rows], h_ref)

    def chunk(xs_ref, bc_ref, dtr_ref, z_ref, dtb_ref, alog_ref, dsk_ref, nw_ref, e_ref,
              y_ref, ys_ref, hp_ref, h_ref):
        _, dt, a_row, acs, acs_x, dt_x, causal, _ = _ssd_common(dtr_ref, dtb_ref, alog_ref, e_ref)
        acs_t = acs.T
        xs = xs_ref[...]
        x_dt = xs * dt_x
        last_x = acs_x[CHUNK - 1:CHUNK, :]
        w_end = jnp.exp(last_x - acs_x)
        e_in = jnp.exp(acs_x)
        d_x = _dot01_r(dsk_ref[...], e_ref[...])[0:1, :]
        hprev = h_ref[...]
        hp_ref[...] = hprev
        lo, hi = _half_masks()
        for g in range(N_GROUPS):
            bg = bc_ref[:, g * N_STATE:(g + 1) * N_STATE]
            cg = bc_ref[:, N_GROUPS * N_STATE + g * N_STATE:N_GROUPS * N_STATE + (g + 1) * N_STATE]
            sl = slice(g * gw, (g + 1) * gw)
            gm = _dot_nt(cg, bg)
            st = _dot(bg.T, x_dt[:, sl] * w_end[:, sl])
            y_off = _dot(cg, hprev[:, sl]) * e_in[:, sl]
            for j in range(gw // LANE):
                h0 = g * (gw // HEAD_P) + 2 * j
                cs = slice(g * gw + j * LANE, g * gw + (j + 1) * LANE)
                xp = x_dt[:, cs]
                m0 = gm * _decay(acs, acs_t, h0, causal)
                m1 = gm * _decay(acs, acs_t, h0 + 1, causal)
                yd = _dot(m0, xp * lo) + _dot(m1, xp * hi)
                y_ref[:, cs] = yd + y_off[:, j * LANE:(j + 1) * LANE] + xs[:, cs] * d_x[:, cs]
            h_ref[:, sl] = hprev[:, sl] * jnp.exp(last_x[:, sl]) + st
        y = y_ref[...]
        z = z_ref[...].astype(F32)
        y2 = y * (z * _sigmoid(z))
        for g in range(N_GROUPS):
            sl = slice(g * gw, (g + 1) * gw)
            yg = y2[:, sl]
            rinv = lax.rsqrt(jnp.mean(yg * yg, axis=-1, keepdims=True) + RMS_EPS)
            ys_ref[:, sl] = (yg * rinv * nw_ref[0:1, sl]).astype(_ACT)

    small = pl.BlockSpec((SUB, LANE), lambda c: (0, 0))
    return pl.pallas_call(
        body, name=name,
        out_shape=(jax.ShapeDtypeStruct((t, D_INNER), F32), jax.ShapeDtypeStruct((t, D_INNER), _ACT),
                   jax.ShapeDtypeStruct((t, D_INNER), F32)),
        grid=(nc // per_step,),
        in_specs=[pl.BlockSpec((per_step * CHUNK, D_INNER), lambda c: (c, 0)),
                  pl.BlockSpec((per_step * CHUNK, 1024), lambda c: (c, 0)),
                  pl.BlockSpec((per_step * CHUNK, LANE), lambda c: (c, T_DT // LANE)),
                  pl.BlockSpec((per_step * CHUNK, D_INNER), lambda c: (c, P_Z // D_INNER)),
                  small, small, small,
                  pl.BlockSpec((SUB, D_INNER), lambda c: (0, 0)),
                  pl.BlockSpec((LANE, D_INNER), lambda c: (0, 0))],
        out_specs=(pl.BlockSpec((per_step * CHUNK, D_INNER), lambda c: (c, 0)),
                   pl.BlockSpec((per_step * CHUNK, D_INNER), lambda c: (c, 0)),
                   pl.BlockSpec((per_step * N_STATE, D_INNER), lambda c: (c, 0))),
        scratch_shapes=[pltpu.VMEM((N_STATE, D_INNER), F32)],
        compiler_params=_cp(("arbitrary",)),
    )(xs_c, bc_c, tail, proj, dtb8, alog8, dsk8, nw8, e_bf)


def _ssd_bwd(d_ys, y, xs_c, bc_c, proj, tail, hprev_all, dtb8, alog8, dsk8, nw8, d_proj, name):
    t = xs_c.shape[0]
    nc = t // CHUNK
    e_bf, et_bf = _expand_consts()
    gw = D_INNER // N_GROUPS

    def body(dys_ref, y_ref, xs_ref, bc_ref, dtr_ref, z_ref, hp_ref, dtb_ref, alog_ref, dsk_ref, nw_ref,
             e_ref, et_ref, _, dxs_ref, dbc_ref, dz_ref, ddt_ref, acc_ref, dnw_ref, dh_ref, dx_ref):
        step = pl.program_id(0)

        @pl.when(step == 0)
        def _():
            dh_ref[...] = jnp.zeros_like(dh_ref)
            acc_ref[...] = jnp.zeros_like(acc_ref)
            dnw_ref[...] = jnp.zeros_like(dnw_ref)

        pre, dt, a_row, acs, acs_x, dt_x, causal, hm = _ssd_common(dtr_ref, dtb_ref, alog_ref, e_ref)
        acs_t = acs.T
        et = et_ref[...]
        xs = xs_ref[...]
        x_dt = xs * dt_x
        last_x = acs_x[CHUNK - 1:CHUNK, :]
        w_end = jnp.exp(last_x - acs_x)
        e_in = jnp.exp(acs_x)
        e_last = jnp.exp(last_x)
        d_x = _dot01_r(dsk_ref[...], e_ref[...])[0:1, :]

        y = y_ref[...]
        z = z_ref[...].astype(F32)
        sz = _sigmoid(z)
        gz = z * sz
        y2 = y * gz
        dys = dys_ref[...].astype(F32)
        for g in range(N_GROUPS):
            sl = slice(g * gw, (g + 1) * gw)
            yg = y2[:, sl]
            rinv = lax.rsqrt(jnp.mean(yg * yg, axis=-1, keepdims=True) + RMS_EPS)
            nrm = yg * rinv
            dn = dys[:, sl] * nw_ref[0:1, sl]
            dnw_ref[0:1, sl] += jnp.sum(dys[:, sl] * nrm, axis=0, keepdims=True)
            dx_ref[:, sl] = rinv * (dn - nrm * jnp.mean(dn * nrm, axis=-1, keepdims=True))
        dy2 = dx_ref[...]
        dy = dy2 * gz
        dz_ref[...] = (dy2 * y * (sz * (1.0 + z * (1.0 - sz)))).astype(_ACT)

        dh_next = dh_ref[...]
        hprev = hp_ref[...]
        lo, hi = _half_masks()
        r = lax.broadcasted_iota(jnp.int32, (CHUNK, CHUNK), 0)
        c = lax.broadcasted_iota(jnp.int32, (CHUNK, CHUNK), 1)
        from_here = (c >= r).astype(jnp.bfloat16)
        before = c < r
        lane = lax.broadcasted_iota(jnp.int32, (1, LANE), 1)
        da_intra = jnp.zeros((CHUNK, LANE), F32)
        v_seg = jnp.zeros((CHUNK, LANE), F32)
        z_seg = jnp.zeros((CHUNK, LANE), F32)
        tail_parts = []
        for g in range(N_GROUPS):
            bg = bc_ref[:, g * N_STATE:(g + 1) * N_STATE]
            cg = bc_ref[:, N_GROUPS * N_STATE + g * N_STATE:N_GROUPS * N_STATE + (g + 1) * N_STATE]
            sl = slice(g * gw, (g + 1) * gw)
            et_g = et_ref[g * gw:(g + 1) * gw, :]
            gm = _dot_nt(cg, bg)
            dzg = e_in[:, sl] * dy[:, sl]
            dcg = _dot_nt(dzg, hprev[:, sl])
            dh_c = _dot(cg.T, dzg)
            q = _dot(bg, dh_next[:, sl])
            dbg = _dot_nt(x_dt[:, sl] * w_end[:, sl], dh_next[:, sl])
            y_off = _dot(cg, hprev[:, sl]) * e_in[:, sl]
            v_seg = v_seg + _dot01_r(dy[:, sl] * y_off, et_g, parts=1)
            z_seg = z_seg + _dot01_r(w_end[:, sl] * q * x_dt[:, sl], et_g, parts=1)
            dgm = jnp.zeros((CHUNK, CHUNK), F32)
            for j in range(gw // LANE):
                h0 = g * (gw // HEAD_P) + 2 * j
                cs = slice(g * gw + j * LANE, g * gw + (j + 1) * LANE)
                xp = x_dt[:, cs]
                dyp = dy[:, cs]
                dxd = jnp.zeros((CHUNK, LANE), F32)
                for half, msk in ((0, lo), (1, hi)):
                    lam = _decay(acs, acs_t, h0 + half, causal)
                    mm = gm * lam
                    dym = dyp * msk
                    dmm = _dot_nt(dym, xp)
                    dxd = dxd + _dot_tn(mm, dym)
                    dgm = dgm + dmm * lam
                    below = _dot(from_here, dmm * mm)
                    col = jnp.sum(jnp.where(before, below, 0.0), axis=-1, keepdims=True)
                    da_intra = da_intra + jnp.where(lane == h0 + half, col, 0.0)
                dx_ref[:, cs] = dxd + w_end[:, cs] * q[:, j * LANE:(j + 1) * LANE]
            dbc_ref[:, N_GROUPS * N_STATE + g * N_STATE:N_GROUPS * N_STATE + (g + 1) * N_STATE] = dcg + _dot(dgm, bg)
            dbc_ref[:, g * N_STATE:(g + 1) * N_STATE] = dbg + _dot_tn(dgm, cg)
            dh_ref[:, sl] = e_last[:, sl] * dh_next[:, sl] + dh_c
            tail_parts.append(e_last[:, sl] * jnp.sum(dh_next[:, sl] * hprev[:, sl], axis=0, keepdims=True))
        dxt = dx_ref[...]

        u_seg = _dot01_r(xs * dxt, et, parts=1)
        q_full = jnp.concatenate(tail_parts, axis=1)
        t_row = _dot01_r(jnp.broadcast_to(q_full, (SUB, D_INNER)), et)[0:1, :]
        d_alpha = (da_intra + _dot01_l(from_here, v_seg) + _dot01_l(before.astype(jnp.bfloat16), z_seg) + t_row)
        d_dt = a_row * d_alpha + u_seg
        sgp = _sigmoid(pre)
        d_raw = jnp.where(hm, d_dt * sgp, 0.0)
        ddt_ref[...] = d_raw.astype(_ACT)
        acc_ref[0:1, :] += jnp.sum(d_raw, axis=0, keepdims=True)
        acc_ref[1:2, :] += jnp.sum(d_alpha * dt, axis=0, keepdims=True) * a_row
        dd_row = jnp.sum(dy * xs, axis=0, keepdims=True)
        acc_ref[2:3, :] += _dot01_r(jnp.broadcast_to(dd_row, (SUB, D_INNER)), et)[0:1, :]
        dxs_ref[...] = dy * d_x + dxt * dt_x

    rev = lambda c: (nc - 1 - c, 0)
    small = pl.BlockSpec((SUB, LANE), lambda c: (0, 0))
    return pl.pallas_call(
        body, name=name,
        out_shape=(jax.ShapeDtypeStruct((t, D_INNER), F32), jax.ShapeDtypeStruct((t, 1024), F32),
                   jax.ShapeDtypeStruct(d_proj.shape, d_proj.dtype), jax.ShapeDtypeStruct((t, LANE), _ACT),
                   jax.ShapeDtypeStruct((SUB, LANE), F32), jax.ShapeDtypeStruct((SUB, D_INNER), F32)),
        grid=(nc,),
        in_specs=[pl.BlockSpec((CHUNK, D_INNER), rev),
                  pl.BlockSpec((CHUNK, D_INNER), rev),
                  pl.BlockSpec((CHUNK, D_INNER), rev),
                  pl.BlockSpec((CHUNK, 1024), rev),
                  pl.BlockSpec((CHUNK, LANE), lambda c: (nc - 1 - c, T_DT // LANE)),
                  pl.BlockSpec((CHUNK, D_INNER), lambda c: (nc - 1 - c, P_Z // D_INNER)),
                  pl.BlockSpec((N_STATE, D_INNER), rev),
                  small, small, small,
                  pl.BlockSpec((SUB, D_INNER), lambda c: (0, 0)),
                  pl.BlockSpec((LANE, D_INNER), lambda c: (0, 0)),
                  pl.BlockSpec((D_INNER, LANE), lambda c: (0, 0)),
                  pl.BlockSpec(memory_space=pl.ANY)],
        out_specs=(pl.BlockSpec((CHUNK, D_INNER), rev),
                   pl.BlockSpec((CHUNK, 1024), rev),
                   pl.BlockSpec((CHUNK, D_INNER), lambda c: (nc - 1 - c, P_Z // D_INNER)),
                   pl.BlockSpec((CHUNK, LANE), rev),
                   small,
                   pl.BlockSpec((SUB, D_INNER), lambda c: (0, 0))),
        input_output_aliases={13: 2},
        scratch_shapes=[pltpu.VMEM((N_STATE, D_INNER), F32), pltpu.VMEM((CHUNK, D_INNER), F32)],
        compiler_params=_cp(("arbitrary",), vmem_mb=56),
    )(d_ys, y, xs_c, bc_c, tail, proj, hprev_all, dtb8, alog8, dsk8, nw8, e_bf, et_bf, d_proj)


def _rel_tables():
    qi = np.arange(WIN)[:, None] + WIN
    kj = np.arange(2 * WIN)[None, :]
    rel = qi - kj
    n = np.maximum(rel, 0)
    max_exact = REL_BUCKETS // 2
    nf = np.maximum(n, 1).astype(np.float32)
    large = max_exact + (np.log(nf / np.float32(max_exact)) / np.float32(math.log(WIN / max_exact))
                         * np.float32(REL_BUCKETS - max_exact)).astype(np.int32)
    large = np.minimum(large, REL_BUCKETS - 1)
    bucket = np.where(n < max_exact, n, large)
    valid = (rel >= 0) & (rel < WIN)
    sink_col = np.broadcast_to(kj == 0, rel.shape)
    onehot = np.zeros((BIAS_ROWS, WIN * 2 * WIN), np.float32)
    flat_b = np.where(sink_col, REL_BUCKETS, bucket).reshape(-1)
    flat_v = (valid | sink_col).reshape(-1)
    first_v = ((valid & (kj >= WIN)) | sink_col).reshape(-1)
    idx = np.arange(WIN * 2 * WIN)
    onehot[flat_b[flat_v], idx[flat_v]] = 1.0
    return onehot, np.stack([first_v, flat_v]).astype(np.float32)


def _bias_expand(table_t, name):
    onehot, valid = _rel_tables()

    def body(rb_ref, oh_ref, v_ref, o_ref):
        full = _dot01_r(rb_ref[...], oh_ref[...])
        o_ref[0] = jnp.where(v_ref[0:1, :] > 0.5, full, NEG)
        o_ref[1] = jnp.where(v_ref[1:2, :] > 0.5, full, NEG)

    return pl.pallas_call(
        body, name=name, out_shape=jax.ShapeDtypeStruct((2, A_HEADS, WIN * 2 * WIN), F32),
        compiler_params=_cp(None),
    )(table_t, jnp.asarray(onehot, jnp.bfloat16), jnp.asarray(valid, F32))


def _bias_reduce(dbias, name):
    onehot, _ = _rel_tables()

    def body(d_ref, oh_ref, o_ref):
        acc = None
        r = d_ref[...]
        for _ in range(3):
            hi = r.astype(jnp.bfloat16)
            tt = lax.dot_general(hi, oh_ref[...], (((1,), (1,)), ((), ())), preferred_element_type=F32)
            acc = tt if acc is None else acc + tt
            r = r - hi.astype(F32)
        o_ref[...] = acc

    return pl.pallas_call(
        body, name=name, out_shape=jax.ShapeDtypeStruct((A_HEADS, BIAS_ROWS), F32),
        compiler_params=_cp(None),
    )(dbias, jnp.asarray(onehot, jnp.bfloat16))


def _attn_bands(kc_ref, kp_ref, vc_ref, vp_ref, has_prev):
    lo, hi = _half_masks()
    row = lax.broadcasted_iota(jnp.int32, (2 * WIN, 1), 0)
    keep = (row > 0).astype(F32)
    kb = jnp.concatenate([jnp.where(has_prev, kp_ref[...], 0.0), kc_ref[...]], axis=0) * (keep * (A_DH ** -0.5))
    vb = jnp.concatenate([jnp.where(has_prev, vp_ref[...], 0.0), vc_ref[...]], axis=0) * keep
    kr = pltpu.roll(kb, 64, 1)
    vr = pltpu.roll(vb, 64, 1)
    kk = ((kb * lo, kr * hi), (kr * lo, kb * hi))
    vv = ((vb * lo, vr * hi), (vr * lo, vb * hi))
    return kk, vv, (hi, lo)


def _attn_logits(q_ref, kk, lg_ref):
    for h in range(A_HEADS):
        j, half, kv = h // 2, h % 2, h // (A_HEADS // 2)
        lg_ref[h] = _dot_nt(q_ref[:, j * LANE:(j + 1) * LANE], kk[kv][half])


def _attn_fwd(proj, tail, bias, name):
    t = proj.shape[0]
    nb = t // WIN

    def body(q_ref, kc_ref, kp_ref, vc_ref, vp_ref, b_ref, o_ref, lg_ref, p_ref):
        n = pl.program_id(0)
        kk, vv, ones = _attn_bands(kc_ref, kp_ref, vc_ref, vp_ref, n > 0)
        _attn_logits(q_ref, kk, lg_ref)
        for h in range(A_HEADS):
            logits = lg_ref[h] + b_ref[h]
            p_ref[h] = jnp.exp(logits - jnp.max(logits, axis=-1, keepdims=True)).astype(_MXU)
        lane = lax.broadcasted_iota(jnp.int32, (1, LANE), 1)
        for j in range(A_HEADS // 2):
            kv = (2 * j) // (A_HEADS // 2)
            outs = []
            for half in range(2):
                o = jnp.dot(p_ref[2 * j + half], (vv[kv][half] + ones[half]).astype(_MXU), preferred_element_type=F32)
                outs.append(o / pltpu.roll(o, 64, 1))
            o_ref[:, j * LANE:(j + 1) * LANE] = jnp.where(lane < 64, outs[0], outs[1]).astype(_ACT)

    kvspec = lambda col, prev: pl.BlockSpec(
        (WIN, LANE), (lambda n: (jnp.maximum(n - 1, 0), col)) if prev else (lambda n: (n, col)))
    return pl.pallas_call(
        body, name=name, out_shape=jax.ShapeDtypeStruct((t, D_MODEL), _ACT),
        grid=(nb,),
        in_specs=[pl.BlockSpec((WIN, 1024), lambda n: (n, P_Q // 1024)),
                  kvspec(T_K // LANE, False), kvspec(T_K // LANE, True),
                  kvspec(T_V // LANE, False), kvspec(T_V // LANE, True),
                  pl.BlockSpec((None, A_HEADS, WIN, 2 * WIN), lambda n: (jnp.minimum(n, 1), 0, 0, 0))],
        out_specs=pl.BlockSpec((WIN, 1024), lambda n: (n, 0)),
        scratch_shapes=[pltpu.VMEM((A_HEADS, WIN, 2 * WIN), F32), pltpu.VMEM((A_HEADS, WIN, 2 * WIN), _MXU)],
        compiler_params=_cp(("parallel",)),
    )(proj, tail, tail, tail, tail, bias)


def _attn_bwd(proj, tail, bias, y_attn, d_out, d_proj, name):
    t = proj.shape[0]
    nb = t // WIN

    def body(q_ref, kc_ref, kp_ref, vc_ref, vp_ref, b_ref, y_ref, do_ref, _,
             dq_ref, dk_ref, dv_ref, db_ref, ck_ref, cv_ref, lg_ref, dl_ref, p_ref):
        n = pl.program_id(0)

        @pl.when(n == 0)
        def _():
            db_ref[...] = jnp.zeros_like(db_ref)
            ck_ref[...] = jnp.zeros_like(ck_ref)
            cv_ref[...] = jnp.zeros_like(cv_ref)

        @pl.when(n < nb)
        def _():
            kk, vv, _ = _attn_bands(kc_ref, kp_ref, vc_ref, vp_ref, n > 0)
            lo, hi = _half_masks()
            ones_k = jnp.ones((2 * WIN, LANE), jnp.bfloat16)
            ones_d = jnp.ones((LANE, LANE), jnp.bfloat16)
            _attn_logits(q_ref, kk, lg_ref)
            for h in range(A_HEADS):
                j, half, kv = h // 2, h % 2, h // (A_HEADS // 2)
                msk = hi if half else lo
                logits = lg_ref[h] + b_ref[h]
                p = jnp.exp(logits - jnp.max(logits, axis=-1, keepdims=True))
                den = jnp.dot(p.astype(_MXU), ones_k.astype(_MXU), preferred_element_type=F32)
                dop = do_ref[:, j * LANE:(j + 1) * LANE].astype(F32)
                delta = _dot01_r(dop * y_ref[:, j * LANE:(j + 1) * LANE].astype(F32) * msk, ones_d, parts=2)
                inv = 1.0 / den
                probs = p * jnp.concatenate([inv, inv], axis=1)
                dprobs = _dot_nt(dop, vv[kv][half])
                dlog = probs * (dprobs - jnp.concatenate([delta, delta], axis=1))
                db_ref[h] += dlog
                dl_ref[h] = dlog.astype(_MXU)
                p_ref[h] = probs.astype(_MXU)
            dk_t = [[None, None], [None, None]]
            dv_t = [[None, None], [None, None]]
            for j in range(A_HEADS // 2):
                kv = (2 * j) // (A_HEADS // 2)
                qs = q_ref[:, j * LANE:(j + 1) * LANE].astype(F32) * (A_DH ** -0.5)
                dop = do_ref[:, j * LANE:(j + 1) * LANE].astype(F32)
                dq = None
                for half, msk in ((0, lo), (1, hi)):
                    h = 2 * j + half
                    dqh = jnp.dot(dl_ref[h], kk[kv][half].astype(_MXU), preferred_element_type=F32)
                    dq = dqh if dq is None else dq + dqh
                    dkh = lax.dot_general((qs * msk).astype(_MXU), dl_ref[h], (((0,), (0,)), ((), ())),
                                          preferred_element_type=F32)
                    dvh = lax.dot_general((dop * msk).astype(_MXU), p_ref[h], (((0,), (0,)), ((), ())),
                                          preferred_element_type=F32)
                    dk_t[kv][half] = dkh if dk_t[kv][half] is None else dk_t[kv][half] + dkh
                    dv_t[kv][half] = dvh if dv_t[kv][half] is None else dv_t[kv][half] + dvh
                dq_ref[:, j * LANE:(j + 1) * LANE] = dq.astype(_ACT)
            row = lax.broadcasted_iota(jnp.int32, (2 * WIN, 1), 0)

            def band(acc):
                a = (acc[0][0] + pltpu.roll(acc[0][1], 64, 0)) + (pltpu.roll(acc[1][0], 64, 0) + acc[1][1])
                return jnp.where(row > 0, a.T, 0.0)

            dkb = band(dk_t)
            dvb = band(dv_t)
            dk_ref[...] = (ck_ref[...] + dkb[0:WIN]).astype(_ACT)
            dv_ref[...] = (cv_ref[...] + dvb[0:WIN]).astype(_ACT)
            ck_ref[...] = dkb[WIN:]
            cv_ref[...] = dvb[WIN:]

        @pl.when(n == nb)
        def _():
            dk_ref[...] = ck_ref[...].astype(_ACT)
            dv_ref[...] = cv_ref[...].astype(_ACT)

    cur = lambda n: jnp.minimum(n, nb - 1)
    prv = lambda n: jnp.maximum(jnp.minimum(n, nb - 1) - 1, 0)
    kvspec = lambda col, prev: pl.BlockSpec(
        (WIN, LANE), (lambda n: (prv(n), col)) if prev else (lambda n: (cur(n), col)))
    band_shape = (A_HEADS, WIN, 2 * WIN)
    return pl.pallas_call(
        body, name=name,
        out_shape=(jax.ShapeDtypeStruct(d_proj.shape, d_proj.dtype), jax.ShapeDtypeStruct((t, LANE), _ACT),
                   jax.ShapeDtypeStruct((t, LANE), _ACT), jax.ShapeDtypeStruct(band_shape, F32)),
        grid=(nb + 1,),
        in_specs=[pl.BlockSpec((WIN, 1024), lambda n: (cur(n), P_Q // 1024)),
                  kvspec(T_K // LANE, False), kvspec(T_K // LANE, True),
                  kvspec(T_V // LANE, False), kvspec(T_V // LANE, True),
                  pl.BlockSpec((None,) + band_shape, lambda n: (jnp.minimum(n, 1), 0, 0, 0)),
                  pl.BlockSpec((WIN, 1024), lambda n: (cur(n), 0)),
                  pl.BlockSpec((WIN, 1024), lambda n: (cur(n), 0)),
                  pl.BlockSpec(memory_space=pl.ANY)],
        out_specs=(pl.BlockSpec((WIN, 1024), lambda n: (cur(n), P_Q // 1024)),
                   pl.BlockSpec((WIN, LANE), lambda n: (jnp.maximum(n - 1, 0), 0)),
                   pl.BlockSpec((WIN, LANE), lambda n: (jnp.maximum(n - 1, 0), 0)),
                   pl.BlockSpec(band_shape, lambda n: (0, 0, 0))),
        input_output_aliases={8: 0},
        scratch_shapes=[pltpu.VMEM((WIN, LANE), F32), pltpu.VMEM((WIN, LANE), F32),
                        pltpu.VMEM(band_shape, F32), pltpu.VMEM(band_shape, _MXU), pltpu.VMEM(band_shape, _MXU)],
        compiler_params=_cp(("arbitrary",)),
    )(proj, tail, tail, tail, tail, bias, y_attn, d_out, d_proj)


def _merge_fwd(bs, ba, proj, bg8, name):
    t = bs.shape[0]
    tm = _tm_rows(t)

    def body(bs_ref, ba_ref, gs_ref, ga_ref, bgs_ref, bga_ref, o_ref):
        g_s = _sigmoid(gs_ref[...] + bgs_ref[0:1, :])
        g_a = _sigmoid(ga_ref[...] + bga_ref[0:1, :])
        o_ref[...] = (g_s * bs_ref[...] + g_a * ba_ref[...]).astype(_ACT)

    row = lambda col: pl.BlockSpec((tm, 1024), lambda i: (i, col))
    return pl.pallas_call(
        body, name=name, out_shape=jax.ShapeDtypeStruct((t, D_MODEL), _ACT), grid=(t // tm,),
        in_specs=[row(0), row(0), row(P_G // 1024), row(P_G // 1024 + 1),
                  pl.BlockSpec((SUB, 1024), lambda i: (0, 0)), pl.BlockSpec((SUB, 1024), lambda i: (0, 1))],
        out_specs=row(0), compiler_params=_cp(("parallel",)),
    )(bs, ba, proj, proj, bg8, bg8)


def _merge_bwd(d_merged, bs, ba, proj, bg8, name):
    t = bs.shape[0]
    tm = _tm_rows(t)

    def body(dm_ref, bs_ref, ba_ref, gs_ref, ga_ref, bgs_ref, bga_ref, dbs_ref, dba_ref, dg_ref, acc_ref):
        @pl.when(pl.program_id(0) == 0)
        def _():
            acc_ref[...] = jnp.zeros_like(acc_ref)

        dm = dm_ref[...].astype(F32)
        g_s = _sigmoid(gs_ref[...] + bgs_ref[0:1, :])
        g_a = _sigmoid(ga_ref[...] + bga_ref[0:1, :])
        dbs_ref[...] = (dm * g_s).astype(_ACT)
        dba_ref[...] = (dm * g_a).astype(_ACT)
        dgs = dm * bs_ref[...].astype(F32) * g_s * (1.0 - g_s)
        dga = dm * ba_ref[...].astype(F32) * g_a * (1.0 - g_a)
        dg_ref[:, 0:1024] = dgs.astype(_ACT)
        dg_ref[:, 1024:2048] = dga.astype(_ACT)
        acc_ref[0:1, 0:1024] += jnp.sum(dgs, axis=0, keepdims=True)
        acc_ref[0:1, 1024:2048] += jnp.sum(dga, axis=0, keepdims=True)

    row = lambda col: pl.BlockSpec((tm, 1024), lambda i: (i, col))
    return pl.pallas_call(
        body, name=name,
        out_shape=(jax.ShapeDtypeStruct((t, D_MODEL), _ACT), jax.ShapeDtypeStruct((t, D_MODEL), _ACT),
                   jax.ShapeDtypeStruct((t, P_W), _ACT), jax.ShapeDtypeStruct((SUB, 2048), F32)),
        grid=(t // tm,),
        in_specs=[row(0), row(0), row(0), row(P_G // 1024), row(P_G // 1024 + 1),
                  pl.BlockSpec((SUB, 1024), lambda i: (0, 0)), pl.BlockSpec((SUB, 1024), lambda i: (0, 1))],
        out_specs=(row(0), row(0), pl.BlockSpec((tm, 2048), lambda i: (i, P_G // 2048)),
                   pl.BlockSpec((SUB, 2048), lambda i: (0, 0))),
        compiler_params=_cp(("arbitrary",)),
    )(d_merged, bs, ba, proj, proj, bg8, bg8)


def _place_tail(d_k, d_v, d_dt, d_proj, name):
    t = d_k.shape[0]
    tm = _tm_rows(t)
    width = P_W - P_MAIN

    def body(k_ref, v_ref, dt_ref, _, o_ref):
        o_ref[:, T_K:T_K + LANE] = k_ref[...]
        o_ref[:, T_V:T_V + LANE] = v_ref[...]
        o_ref[:, T_DT:T_DT + LANE] = dt_ref[...]
        o_ref[:, T_DT + LANE:width] = jnp.zeros((tm, width - T_DT - LANE), o_ref.dtype)

    blk = pl.BlockSpec((tm, LANE), lambda i: (i, 0))
    return pl.pallas_call(
        body, name=name, out_shape=jax.ShapeDtypeStruct(d_proj.shape, d_proj.dtype), grid=(t // tm,),
        in_specs=[blk, blk, blk, pl.BlockSpec(memory_space=pl.ANY)],
        out_specs=pl.BlockSpec((tm, width), lambda i: (i, P_MAIN // width)),
        input_output_aliases={3: 0}, compiler_params=_cp(("parallel",)),
    )(d_k, d_v, d_dt, d_proj)


def _ln_stats(r):
    mu = jnp.mean(r, axis=-1, keepdims=True)
    xc = r - mu
    var = jnp.mean(xc * xc, axis=-1, keepdims=True)
    rstd = lax.rsqrt(var + LN_EPS)
    return xc * rstd, rstd


def _ln_bwd(dxhat, xhat, rstd):
    return rstd * (dxhat - jnp.mean(dxhat, axis=-1, keepdims=True)
                   - xhat * jnp.mean(dxhat * xhat, axis=-1, keepdims=True))


def _ln1_fwd(x, mix, g8, b8, name):
    t = x.shape[0]
    tm = _tm_rows(t)

    def body(x_ref, m_ref, g_ref, b_ref, xh_ref, h_ref, rs_ref):
        xhat, rstd = _ln_stats(ALPHA * x_ref[...] + m_ref[...])
        xh_ref[...] = xhat
        h_ref[...] = (xhat * g_ref[0:1, :] + b_ref[0:1, :]).astype(_ACT)
        rs_ref[...] = rstd

    row = pl.BlockSpec((tm, D_MODEL), lambda i: (i, 0))
    par = pl.BlockSpec((SUB, D_MODEL), lambda i: (0, 0))
    return pl.pallas_call(
        body, name=name,
        out_shape=(jax.ShapeDtypeStruct((t, D_MODEL), F32), jax.ShapeDtypeStruct((t, D_MODEL), _ACT),
                   jax.ShapeDtypeStruct((t, 1), F32)),
        grid=(t // tm,), in_specs=[row, row, par, par],
        out_specs=(row, row, pl.BlockSpec((tm, 1), lambda i: (i, 0))),
        compiler_params=_cp(("parallel",)),
    )(x, mix, g8, b8)


def _ln2_loss(xhat1, ffn, target, g1_8, b1_8, g2_8, b2_8, name):
    t = xhat1.shape[0]
    tm = _tm_rows(t)

    def body(xh_ref, f_ref, t_ref, g1_ref, b1_ref, g2_ref, b2_ref, d_ref, db_ref, acc_ref):
        @pl.when(pl.program_id(0) == 0)
        def _():
            acc_ref[...] = jnp.zeros_like(acc_ref)

        h1 = xh_ref[...] * g1_ref[0:1, :] + b1_ref[0:1, :]
        xhat, rstd = _ln_stats(ALPHA * h1 + f_ref[...])
        diff = xhat * g2_ref[0:1, :] + b2_ref[0:1, :] - t_ref[...]
        dy = diff * (1.0 / D_MODEL)
        acc_ref[0:1, :] += jnp.sum(dy * xhat, axis=0, keepdims=True)
        acc_ref[1:2, :] += jnp.sum(dy, axis=0, keepdims=True)
        acc_ref[2:3, :] += jnp.sum(diff * diff, axis=0, keepdims=True)
        d = _ln_bwd(dy * g2_ref[0:1, :], xhat, rstd)
        d_ref[...] = d
        db_ref[...] = d.astype(_ACT)

    row = pl.BlockSpec((tm, D_MODEL), lambda i: (i, 0))
    par = pl.BlockSpec((SUB, D_MODEL), lambda i: (0, 0))
    return pl.pallas_call(
        body, name=name,
        out_shape=(jax.ShapeDtypeStruct((t, D_MODEL), F32), jax.ShapeDtypeStruct((t, D_MODEL), _ACT),
                   jax.ShapeDtypeStruct((SUB, D_MODEL), F32)),
        grid=(t // tm,), in_specs=[row, row, row, par, par, par, par],
        out_specs=(row, row, par), compiler_params=_cp(("arbitrary",)),
    )(xhat1, ffn, target, g1_8, b1_8, g2_8, b2_8)


def _ln1_bwd(d_r2, d_h1_ffn, xhat1, rstd1, g1_8, name):
    t = xhat1.shape[0]
    tm = _tm_rows(t)

    def body(d2_ref, df_ref, xh_ref, rs_ref, g_ref, d_ref, db_ref, acc_ref):
        @pl.when(pl.program_id(0) == 0)
        def _():
            acc_ref[...] = jnp.zeros_like(acc_ref)

        dh = ALPHA * d2_ref[...] + df_ref[...]
        xhat = xh_ref[...]
        acc_ref[0:1, :] += jnp.sum(dh * xhat, axis=0, keepdims=True)
        acc_ref[1:2, :] += jnp.sum(dh, axis=0, keepdims=True)
        d = _ln_bwd(dh * g_ref[0:1, :], xhat, rs_ref[...])
        d_ref[...] = d
        db_ref[...] = d.astype(_ACT)

    row = pl.BlockSpec((tm, D_MODEL), lambda i: (i, 0))
    par = pl.BlockSpec((SUB, D_MODEL), lambda i: (0, 0))
    return pl.pallas_call(
        body, name=name,
        out_shape=(jax.ShapeDtypeStruct((t, D_MODEL), F32), jax.ShapeDtypeStruct((t, D_MODEL), _ACT),
                   jax.ShapeDtypeStruct((SUB, D_MODEL), F32)),
        grid=(t // tm,), in_specs=[row, row, row, pl.BlockSpec((tm, 1), lambda i: (i, 0)), par],
        out_specs=(row, row, par), compiler_params=_cp(("arbitrary",)),
    )(d_r2, d_h1_ffn, xhat1, rstd1, g1_8)


def _ffn_tm(t):
    return min(256, t)


def _ffn_act_fwd(u0, cw8, cb8, name):
    t = u0.shape[0]
    tm = _ffn_tm(t)

    def body(g_ref, gp_ref, v_ref, vp_ref, wg_ref, wv_ref, bg_ref, bv_ref, o_ref, u_ref):
        i = pl.program_id(0)
        gprev = jnp.where(i > 0, gp_ref[SUB:HALO, :].astype(F32), 0.0)
        vprev = jnp.where(i > 0, vp_ref[SUB:HALO, :].astype(F32), 0.0)
        gate = _conv_pre(g_ref[...].astype(F32), gprev, wg_ref, bg_ref[0:1, :], FFN_K)
        val = _conv_pre(v_ref[...].astype(F32), vprev, wv_ref, bv_ref[0:1, :], FFN_K)
        o_ref[...] = (gate * _sigmoid(gate) * val).astype(_ACT)
        u_ref[:, 0:D_FF] = gate.astype(_ACT)
        u_ref[:, D_FF:2 * D_FF] = val.astype(_ACT)

    cur = lambda col: pl.BlockSpec((tm, D_FF), lambda i: (i, col))
    prv = lambda col: _prev_halo(tm, D_FF, lambda i: (i, col))
    par = lambda col: pl.BlockSpec((SUB, D_FF), lambda i: (0, col))
    return pl.pallas_call(
        body, name=name,
        out_shape=(jax.ShapeDtypeStruct((t, D_FF), _ACT), jax.ShapeDtypeStruct((t, 2 * D_FF), _ACT)),
        grid=(t // tm,),
        in_specs=[cur(0), prv(0), cur(1), prv(1), par(0), par(1), par(0), par(1)],
        out_specs=(pl.BlockSpec((tm, D_FF), lambda i: (i, 0)), pl.BlockSpec((tm, 2 * D_FF), lambda i: (i, 0))),
        compiler_params=_cp(("parallel",)),
    )(u0, u0, u0, u0, cw8, cw8, cb8, cb8)


def _ffn_act_bwd(u0, u, cw8, d_a, name):
    t = u0.shape[0]
    tm = _ffn_tm(t)
    nt = t // tm

    def body(g0_ref, v0_ref, g_ref, gn_ref, v_ref, vn_ref, wg_ref, wv_ref, da_ref, dan_ref, du_ref, acc_ref):
        i = pl.program_id(0)

        @pl.when(i == 0)
        def _():
            acc_ref[...] = jnp.zeros_like(acc_ref)

        def grads(gate, val, da):
            return da * val * _silu_grad(gate), da * gate * _sigmoid(gate)

        dgate, dval = grads(g_ref[...].astype(F32), v_ref[...].astype(F32), da_ref[...].astype(F32))
        dgate_n, dval_n = grads(gn_ref[0:SUB, :].astype(F32), vn_ref[0:SUB, :].astype(F32),
                                dan_ref[0:SUB, :].astype(F32))
        last = i == nt - 1
        du_ref[:, 0:D_FF] = _conv_grads(dgate, jnp.where(last, 0.0, dgate_n), g0_ref[...].astype(F32), wg_ref,
                                        acc_ref, FFN_K, slice(0, D_FF)).astype(_ACT)
        du_ref[:, D_FF:2 * D_FF] = _conv_grads(dval, jnp.where(last, 0.0, dval_n), v0_ref[...].astype(F32), wv_ref,
                                               acc_ref, FFN_K, slice(D_FF, 2 * D_FF)).astype(_ACT)

    cur = lambda col: pl.BlockSpec((tm, D_FF), lambda i: (i, col))
    nxt = lambda col: _next_halo(tm, t, D_FF, lambda i: (i, col))
    par = lambda col: pl.BlockSpec((SUB, D_FF), lambda i: (0, col))
    return pl.pallas_call(
        body, name=name,
        out_shape=(jax.ShapeDtypeStruct((t, 2 * D_FF), _ACT), jax.ShapeDtypeStruct((SUB, 2 * D_FF), F32)),
        grid=(nt,),
        in_specs=[cur(0), cur(1), cur(0), nxt(0), cur(1), nxt(1), par(0), par(1), cur(0), nxt(0)],
        out_specs=(pl.BlockSpec((tm, 2 * D_FF), lambda i: (i, 0)),
                   pl.BlockSpec((SUB, 2 * D_FF), lambda i: (0, 0))),
        compiler_params=_cp(("arbitrary",)),
    )(u0, u0, u, u, u, u, cw8, cw8, d_a, d_a)


_REST = ("w_branch_ssm", "w_branch_attn", "w_mix_out", "w_up", "w_down")


def _mm_side(*args, side, **kw):
    if side is None:
        return _mm(*args, **kw), []
    return _mm(*args, side=side, **kw)


def _local_step(x, x_bf, target, wts, ex):
    t = x.shape[0]
    wp = wts["wp"]
    scw = wts["ssm_conv_w"]
    scb = wts["ssm_conv_b"]
    fcw8 = _rows8(wts["ffn_conv_w"])
    fcb8 = _rows8(wts["ffn_conv_b"])
    pad_lane = lambda p: jnp.concatenate([p.astype(F32), jnp.zeros((1, LANE - p.shape[1]), F32)], axis=1)
    dtb8 = _rows8(pad_lane(wts["ssm_dt_bias"]))
    alog8 = _rows8(pad_lane(wts["ssm_a_log"]))
    dsk8 = _rows8(pad_lane(wts["ssm_d"]))
    bias_table = jnp.concatenate([wts["rel_bias"].T.astype(F32), wts["attn_sinks"].T.astype(F32),
                                  jnp.zeros((A_HEADS, BIAS_ROWS - REL_BUCKETS - 1), F32)], axis=1)
    nw8 = _rows8(wts["ssm_norm_w"])
    bg8 = _rows8(wts["b_gate"])
    g1_8, b1_8, g2_8, b2_8 = (_rows8(wts[k]) for k in ("ln1_g", "ln1_b", "ln2_g", "ln2_b"))
    xs_w8, xs_b8 = _rows8(scw[:, :D_INNER]), _rows8(scb[:, :D_INNER])
    bc_w8, bc_b8 = _rows8(scw[:, D_INNER:]), _rows8(scb[:, D_INNER:])

    proj, stacks = _mm_side(x_bf, wp[:, :P_MAIN], "mm_in", out_dtype=_ACT, side=ex.gather_rest())
    wts = dict(wts, **ex.rest_weights(stacks))
    w_bs, w_ba, w_mix, w_up, w_dn = (wts[k] for k in _REST)
    tail = _mm(x_bf, wp[:, P_MAIN:], "mm_in_tail")
    xs_c, xs_pre = _conv_silu_fwd(proj, P_XS // _TC, D_INNER // _TC, xs_w8, xs_b8, "conv_xs_fwd")
    bc_c, bc_pre = _conv_silu_fwd(proj, P_BC // _TC, 1024 // _TC, bc_w8, bc_b8, "conv_bc_fwd")
    y_ssd, y_ssm, hprev = _ssd_fwd(xs_c, bc_c, proj, tail, dtb8, alog8, dsk8, nw8, "ssd_fwd")
    bias = _bias_expand(bias_table, "bias_expand").reshape(2, A_HEADS, WIN, 2 * WIN)
    y_attn = _attn_fwd(proj, tail, bias, "attn_fwd")
    bs = _mm(y_ssm, w_bs, "mm_bs", out_dtype=_ACT)
    ba = _mm(y_attn, w_ba, "mm_ba", out_dtype=_ACT)
    merged = _merge_fwd(bs, ba, proj, bg8, "merge_fwd")
    mix = _mm(merged, w_mix, "mm_mix", out_dtype=_ACT)
    xhat1, h1_bf, rstd1 = _ln1_fwd(x, mix, g1_8, b1_8, "ln1_fwd")
    u0 = _mm(h1_bf, w_up, "mm_up", out_dtype=_ACT)
    act, u_conv = _ffn_act_fwd(u0, fcw8, fcb8, "ffn_act_fwd")
    ffn = _mm(act, w_dn, "mm_down", out_dtype=_ACT)
    d_r2, d_r2_bf, acc_ln2 = _ln2_loss(xhat1, ffn, target, g1_8, b1_8, g2_8, b2_8, "ln2_loss")
    d_w_dn = _mm(act, d_r2_bf, "mm_dw_down", trans_a=True)
    d_act = _mm(d_r2_bf, w_dn.T, "mm_d_act", out_dtype=_ACT)
    d_u0, acc_ffn = _ffn_act_bwd(u0, u_conv, fcw8, d_act, "ffn_act_bwd")
    d_w_up = _mm(h1_bf, d_u0, "mm_dw_up", trans_a=True)
    d_h1_ffn = _mm(d_u0, w_up.T, "mm_d_h1", out_dtype=_ACT)
    d_r1, d_r1_bf, acc_ln1 = _ln1_bwd(d_r2, d_h1_ffn, xhat1, rstd1, g1_8, "ln1_bwd")
    d_w_mix = _mm(merged, d_r1_bf, "mm_dw_mix", trans_a=True)
    d_merged = _mm(d_r1_bf, w_mix.T, "mm_d_merged", out_dtype=_ACT)
    d_bs, d_ba, d_proj, acc_bg = _merge_bwd(d_merged, bs, ba, proj, bg8, "merge_bwd")
    d_w_bs = _mm(y_ssm, d_bs, "mm_dw_bs", trans_a=True)
    d_w_ba = _mm(y_attn, d_ba, "mm_dw_ba", trans_a=True)
    d_y_ssm = _mm(d_bs, w_bs.T, "mm_d_yssm", out_dtype=_ACT)
    d_y_attn = _mm(d_ba, w_ba.T, "mm_d_yattn", out_dtype=_ACT)
    d_proj, d_k, d_v, d_bias = _attn_bwd(proj, tail, bias, y_attn, d_y_attn, d_proj, "attn_bwd")
    d_table = _bias_reduce(d_bias.reshape(A_HEADS, WIN * 2 * WIN), "bias_reduce")
    d_xs_c, d_bc_c, d_proj, d_dt, acc_ssd, acc_nw = _ssd_bwd(
        d_y_ssm, y_ssd, xs_c, bc_c, proj, tail, hprev, dtb8, alog8, dsk8, nw8, d_proj, "ssd_bwd")
    d_proj, acc_xs = _conv_silu_bwd(proj, P_XS // _TC, D_INNER // _TC, xs_pre, xs_w8, d_xs_c, d_proj, "conv_xs_bwd")
    d_proj, acc_bc = _conv_silu_bwd(proj, P_BC // _TC, 1024 // _TC, bc_pre, bc_w8, d_bc_c, d_proj, "conv_bc_bwd")
    d_proj = _place_tail(d_k, d_v, d_dt, d_proj, "place_tail")
    grads = {"w_branch_ssm": d_w_bs, "w_branch_attn": d_w_ba, "w_mix_out": d_w_mix, "w_up": d_w_up, "w_down": d_w_dn}
    d_wp, landed_rest = _mm_side(x_bf, d_proj, "mm_dw_in", trans_a=True, side=ex.reduce_job(grads))
    d_x, landed_in = _mm_side(d_proj, wp.T, "mm_d_x", res=d_r1, res_scale=ALPHA, side=ex.reduce_job({"wp": d_wp}))
    grads.update({
        "wp": d_wp,
        "ssm_conv_w": jnp.concatenate([acc_xs[0:SSM_K], acc_bc[0:SSM_K]], axis=1),
        "ffn_conv_w": acc_ffn[0:FFN_K],
    })
    small = {
        "rel_bias": d_table[:, 0:REL_BUCKETS].T,
        "b_gate": acc_bg[0:1],
        "ssm_conv_b": jnp.concatenate([acc_xs[SSM_K:SSM_K + 1], acc_bc[SSM_K:SSM_K + 1]], axis=1),
        "ssm_dt_bias": acc_ssd[0:1, 0:N_HEADS], "ssm_a_log": acc_ssd[1:2, 0:N_HEADS], "ssm_d": acc_ssd[2:3, 0:N_HEADS],
        "ssm_norm_w": acc_nw[0:1],
        "attn_sinks": d_table[:, REL_BUCKETS:REL_BUCKETS + 1].T,
        "ln1_g": acc_ln1[0:1], "ln1_b": acc_ln1[1:2],
        "ffn_conv_b": acc_ffn[FFN_K:FFN_K + 1],
        "ln2_g": acc_ln2[0:1], "ln2_b": acc_ln2[1:2],
        "loss_lanes": acc_ln2[2:3],
    }
    return d_x, grads, small, landed_rest + landed_in


_MATS = (("w_in", (1024, 2120), 1), ("w_branch_ssm", (512, 1024), 0), ("w_branch_attn", (256, 1024), 0),
         ("w_mix_out", (256, 1024), 0), ("w_up", (1024, 1408), 1), ("w_down", (704, 1024), 0))
_CONVS = (("ssm_conv_w", (4, 768)), ("ffn_conv_w", (3, 1408)))
_CONV_ROWS = 64

_SMALL = (("rel_bias", (32, 16)), ("b_gate", (1, 2048)), ("ssm_conv_b", (1, 3072)), ("ssm_dt_bias", (1, 32)),
          ("ssm_a_log", (1, 32)), ("ssm_d", (1, 32)), ("ssm_norm_w", (1, 2048)), ("attn_sinks", (1, 16)),
          ("ln1_g", (1, 1024)), ("ln1_b", (1, 1024)), ("ffn_conv_b", (1, 5632)), ("ln2_g", (1, 1024)),
          ("ln2_b", (1, 1024)), ("g_ssm_conv_w", (4, 3072)), ("g_ffn_conv_w", (3, 5632)), ("loss_lanes", (1, 1024)))


def _small_rows(shape):
    rows = -(-(shape[0] * shape[1]) // LANE)
    return -(-rows // SUB) * SUB


def _as_rows(a, rows, dtype):
    flat = a.reshape(-1).astype(dtype)
    flat = jnp.concatenate([flat, jnp.zeros((rows * LANE - flat.shape[0],), dtype)])
    return flat.reshape(rows, LANE)


def _pack_small(parts):
    blocks = [_as_rows(parts[n], _small_rows(s), F32) if n in parts else jnp.zeros((_small_rows(s), LANE), F32)
              for n, s in _SMALL]
    return jnp.concatenate(blocks, axis=0)


def _unpack_small(packed):
    out, at = {}, 0
    for n, s in _SMALL:
        rows = _small_rows(s)
        out[n] = packed[at:at + rows].reshape(-1)[:s[0] * s[1]].reshape(s)
        at += rows
    return out


def _to_stack(full, shape, axis):
    if axis == 0:
        return full.reshape((N_CHIPS,) + shape)
    return jnp.transpose(full.reshape(shape[0], N_CHIPS, shape[1]), (1, 0, 2))


def _from_stack(stack, axis):
    n, r, c = stack.shape
    if axis == 0:
        return stack.reshape(n * r, c)
    return jnp.transpose(stack, (1, 0, 2)).reshape(r, n * c)


_IN_SHARD = IN_COLS // N_CHIPS


def _cols_of_stack(stack, o, w):
    parts = []
    while w > 0:
        j, a = divmod(o, _IN_SHARD)
        n = min(w, _IN_SHARD - a)
        parts.append(stack[j][:, a:a + n])
        o, w = o + n, w - n
    return parts


def _pack_w_in_stack(stack):
    cols, at = [], 0
    for o, w, pk in sorted(_PIECES, key=lambda p: p[2]):
        if pk > at:
            cols.append(jnp.zeros((stack.shape[1], pk - at), stack.dtype))
        cols += _cols_of_stack(stack, o, w)
        at = pk + w
    cols.append(jnp.zeros((stack.shape[1], P_W - at), stack.dtype))
    return jnp.concatenate(cols, axis=1)


def _unpack_w_in_stack(wp):
    slabs = []
    for j in range(N_CHIPS):
        lo, hi = j * _IN_SHARD, (j + 1) * _IN_SHARD
        cols = []
        for o, w, pk in sorted(_PIECES):
            a, b = max(o, lo), min(o + w, hi)
            if a < b:
                cols.append(wp[:, pk + a - o:pk + b - o])
        slabs.append(jnp.concatenate(cols, axis=1))
    return jnp.stack(slabs)


_MESH = pl.DeviceIdType.MESH
_HBM = pl.BlockSpec(memory_space=pltpu.HBM)


def _position():
    return lax.axis_index("x"), lax.axis_index("y"), lax.axis_index("c")


def _other_chips(x, y):
    return ((1 - x, y), (x, 1 - y), (1 - x, 1 - y))


def _remote(src, dst, send_sem, recv_sem, to):
    return pltpu.make_async_remote_copy(src_ref=src, dst_ref=dst, send_sem=send_sem, recv_sem=recv_sem,
                                        device_id=to, device_id_type=_MESH)


def _cast_rows(x, dtype, name, side):
    t, cols = x.shape
    tm = min(1024, t)
    nt = t // tm
    ns_in, ns_out = len(side.inputs), len(side.out_shape)

    def body(*refs):
        x_ref, o_ref = refs[0], refs[1 + ns_in]
        job_refs = (refs[1:1 + ns_in], refs[2 + ns_in:2 + ns_in + ns_out], refs[2 + ns_in + ns_out:])
        i = pl.program_id(0)

        @pl.when(i == 0)
        def _():
            side.start(*job_refs)

        o_ref[...] = x_ref[...].astype(dtype)

        @pl.when(i == nt - 1)
        def _():
            side.finish(*job_refs)

    blk = pl.BlockSpec((tm, cols), lambda i: (i, 0))
    outs = pl.pallas_call(
        body, name=name, out_shape=[jax.ShapeDtypeStruct((t, cols), dtype)] + list(side.out_shape), grid=(nt,),
        in_specs=[blk] + [_HBM] * ns_in, out_specs=[blk] + [_HBM] * ns_out, scratch_shapes=list(side.sems),
        compiler_params=_cp(("arbitrary",)),
    )(x, *side.inputs)
    return outs[0], list(outs[1:])


_GATHER_COPIES = 7


def _gather_job(shards):
    n = len(shards)

    def plan(s_refs, o_refs, sems):
        send_sems, recv_sems = sems
        x, y, c = _position()
        me = 2 * x + y
        sib = (x, y, 1 - c)
        chips = _other_chips(x, y)

        def copy(m, k, dst, to, src=None):
            at = _GATHER_COPIES * m + k
            return _remote(dst if src is None else src, dst, send_sems.at[at], recv_sems.at[at], to)

        def half(m, chip_idx, h):
            return o_refs[m].at[chip_idx, h]

        own = [copy(m, 6, o_refs[m].at[me], sib, src=s_refs[m]) for m in range(n)]
        first = [copy(m, i, half(m, me, c), (cx, cy, c), src=s_refs[m].at[c])
                 for i, (cx, cy) in enumerate(chips) for m in range(n)]
        return c, sib, chips, copy, half, own, first

    def start(s_refs, o_refs, sems):
        _, _, _, _, _, own, first = plan(s_refs, o_refs, sems)
        for cp in first + own:
            cp.start()

    def finish(s_refs, o_refs, sems):
        c, sib, chips, copy, half, own, first = plan(s_refs, o_refs, sems)
        passed = []
        for i, (cx, cy) in enumerate(chips):
            for m in range(n):
                copy(m, i, half(m, 2 * cx + cy, c), sib).wait_recv()
                passed.append(copy(m, 3 + i, half(m, 2 * cx + cy, c), sib))
                passed[-1].start()
        for i, (cx, cy) in enumerate(chips):
            for m in range(n):
                copy(m, 3 + i, half(m, 2 * cx + cy, 1 - c), sib).wait_recv()
        for cp in first + passed:
            cp.wait_send()
        for cp in own:
            cp.wait()

    return _SideJob(
        inputs=list(shards), out_shape=[jax.ShapeDtypeStruct((N_CHIPS,) + s.shape, s.dtype) for s in shards],
        sems=[pltpu.SemaphoreType.DMA((_GATHER_COPIES * n,)), pltpu.SemaphoreType.DMA((_GATHER_COPIES * n,))],
        start=start, finish=finish)


def _swap_halves(gs, name):
    n = len(gs)

    def body(*refs):
        g_refs, o_refs = refs[:n], refs[n:2 * n]
        send_sems, recv_sems = refs[2 * n:]
        x, y, c = _position()
        cps = [_remote(g_refs[m].at[j, 1 - c], o_refs[m].at[j], send_sems.at[N_CHIPS * m + j],
                       recv_sems.at[N_CHIPS * m + j], (x, y, 1 - c)) for m in range(n) for j in range(N_CHIPS)]
        for cp in cps:
            cp.start()
        for cp in cps:
            cp.wait()

    return pl.pallas_call(
        body, name=name,
        out_shape=[jax.ShapeDtypeStruct((N_CHIPS,) + g.shape[2:], g.dtype) for g in gs],
        in_specs=[_HBM] * n, out_specs=[_HBM] * n,
        scratch_shapes=[pltpu.SemaphoreType.DMA((N_CHIPS * n,)), pltpu.SemaphoreType.DMA((N_CHIPS * n,))],
    )(*gs)


def _scatter_job(ps):
    n = len(ps)

    def copies(p_refs, o_refs, sems):
        send_sems, recv_sems = sems
        x, y, c = _position()
        return [_remote(p_refs[m].at[2 * cx + cy], o_refs[m].at[i], send_sems.at[3 * m + i], recv_sems.at[3 * m + i],
                        (cx, cy, c)) for i, (cx, cy) in enumerate(_other_chips(x, y)) for m in range(n)]

    def start(*parts):
        for cp in copies(*parts):
            cp.start()

    def finish(*parts):
        for cp in copies(*parts):
            cp.wait()

    return _SideJob(
        inputs=list(ps), out_shape=[jax.ShapeDtypeStruct((N_CHIPS - 1,) + p.shape[1:], p.dtype) for p in ps],
        sems=[pltpu.SemaphoreType.DMA((3 * n,)), pltpu.SemaphoreType.DMA((3 * n,))], start=start, finish=finish)


def _join_halves(fulls):
    n = len(fulls)

    def body(*refs):
        o_refs = refs[n:2 * n]
        send_sems, recv_sems = refs[2 * n:]
        x, y, c = _position()
        cps = [_remote(o_refs[m].at[c], o_refs[m].at[c], send_sems.at[m], recv_sems.at[m], (x, y, 1 - c))
               for m in range(n)]
        for cp in cps:
            cp.start()
        for cp in cps:
            cp.wait()

    return pl.pallas_call(
        body, name="join_halves",
        out_shape=[jax.ShapeDtypeStruct(f.shape, f.dtype) for f in fulls],
        in_specs=[_HBM] * n, out_specs=[_HBM] * n, input_output_aliases={m: m for m in range(n)},
        scratch_shapes=[pltpu.SemaphoreType.DMA((n,)), pltpu.SemaphoreType.DMA((n,))],
    )(*fulls)


def _allgather_small(mine, name):
    m_per, n = mine.shape

    def body(x_ref, out_ref, send_sems, recv_sems, local_sem):
        x, y, c = _position()
        me, sibling = (x, y, c), (x, y, 1 - c)
        chips = _other_chips(x, y)

        def rows(px, py, pc):
            return out_ref.at[pl.ds((4 * px + 2 * py + pc) * m_per, m_per), :]

        def copy(k, block, to, src=None):
            return pltpu.make_async_remote_copy(src_ref=rows(*block) if src is None else src, dst_ref=rows(*block),
                                                send_sem=send_sems.at[k], recv_sem=recv_sems.at[k],
                                                device_id=to, device_id_type=_MESH)

        own = pltpu.make_async_copy(x_ref, rows(*me), local_sem)
        own.start()
        first = [copy(0, me, sibling, src=x_ref)]
        first += [copy(1 + j, me, (*chip, c), src=x_ref) for j, chip in enumerate(chips)]
        for cp in first:
            cp.start()
        passed = [copy(4 + j, (*chip, c), sibling) for j, chip in enumerate(chips)]
        for j, chip in enumerate(chips):
            copy(1 + j, (*chip, c), me).wait_recv()
            passed[j].start()
        copy(0, sibling, me).wait_recv()
        for j, chip in enumerate(chips):
            copy(4 + j, (*chip, 1 - c), me).wait_recv()
        for cp in first + passed:
            cp.wait_send()
        own.wait()

    return pl.pallas_call(
        body, name=name, out_shape=jax.ShapeDtypeStruct((N_DEV * m_per, n), mine.dtype),
        in_specs=[pl.BlockSpec(memory_space=pltpu.VMEM)], out_specs=pl.BlockSpec(memory_space=pltpu.VMEM),
        scratch_shapes=[pltpu.SemaphoreType.DMA((7,)), pltpu.SemaphoreType.DMA((7,)), pltpu.SemaphoreType.DMA],
    )(mine)


_ADD_BLOCK_BYTES = 3 << 20


def _add_rows(hr, cols):
    if hr * cols * 4 <= _ADD_BLOCK_BYTES:
        return hr
    return _pick(hr, (256, 128, 64, 32, 16))


def _add_own_half(g, recv, c_idx, name):
    nseg, _, hr, cols = g.shape
    tr = _add_rows(hr, cols)

    def body(c_ref, g_ref, r_ref, o_ref, ob_ref):
        s = g_ref[...] + r_ref[...]
        o_ref[...] = s
        ob_ref[...] = s.astype(jnp.bfloat16)

    blk = pl.BlockSpec((None, tr, cols), lambda j, i, c_ref: (j, i, 0))
    return pl.pallas_call(
        body, name=name,
        out_shape=(jax.ShapeDtypeStruct((nseg, hr, cols), F32), jax.ShapeDtypeStruct((nseg, hr, cols), jnp.bfloat16)),
        grid_spec=pltpu.PrefetchScalarGridSpec(
            num_scalar_prefetch=1, grid=(nseg, hr // tr),
            in_specs=[pl.BlockSpec((None, None, tr, cols), lambda j, i, c_ref: (j, c_ref[0], i, 0)), blk],
            out_specs=(blk, blk)),
        compiler_params=_cp(("parallel", "parallel")),
    )(c_idx, g, recv)


def _add_chips(p, recv, chip_idx, c_idx, name):
    _, hr, cols = p.shape
    tr = _add_rows(hr, cols)

    def body(j_ref, c_ref, p_ref, r_ref, o_ref):
        o_ref[...] = ((p_ref[...] + r_ref[0].astype(F32)) + r_ref[1].astype(F32)) + r_ref[2].astype(F32)

    return pl.pallas_call(
        body, name=name, out_shape=jax.ShapeDtypeStruct((2, hr, cols), F32),
        grid_spec=pltpu.PrefetchScalarGridSpec(
            num_scalar_prefetch=2, grid=(hr // tr,),
            in_specs=[pl.BlockSpec((None, tr, cols), lambda i, j_ref, c_ref: (j_ref[0], i, 0)),
                      pl.BlockSpec((N_CHIPS - 1, tr, cols), lambda i, j_ref, c_ref: (0, i, 0))],
            out_specs=pl.BlockSpec((None, tr, cols), lambda i, j_ref, c_ref: (c_ref[0], i, 0))),
        compiler_params=_cp(("parallel",)),
    )(chip_idx, c_idx, p, recv)


def _adam_math(w, g, m, v):
    m = ADAM_B1 * m + (1.0 - ADAM_B1) * g
    v = ADAM_B2 * v + (1.0 - ADAM_B2) * (g * g)
    m_hat = m / (1.0 - ADAM_B1 ** ADAM_STEP)
    v_hat = v / (1.0 - ADAM_B2 ** ADAM_STEP)
    delta = -ADAM_LR * (m_hat / (jnp.sqrt(v_hat) + ADAM_EPS) + ADAM_WD * w)
    return delta, m, v


def _adam_big(w, g, m, v, name):
    rows, cols = w.shape
    tr = _pick(rows, (256, 128, 64, 32, 16, 8)) if rows % SUB == 0 else rows

    def body(w_ref, g_ref, m_ref, v_ref, d_ref, mo_ref, vo_ref):
        d_ref[...], mo_ref[...], vo_ref[...] = _adam_math(w_ref[...], g_ref[...], m_ref[...], v_ref[...])

    blk = pl.BlockSpec((tr, cols), lambda i: (i, 0))
    shp = jax.ShapeDtypeStruct((rows, cols), F32)
    return pl.pallas_call(
        body, name=name, out_shape=(shp, shp, shp), grid=(rows // tr,),
        in_specs=[blk, blk, blk, blk], out_specs=(blk, blk, blk), compiler_params=_cp(("parallel",)),
    )(w, g, m, v)


def _adam_small(w, gathered, m, v):
    rows = w.shape[0]

    def body(w_ref, a_ref, m_ref, v_ref, g_ref, d_ref, mo_ref, vo_ref):
        g = a_ref[0:rows, :]
        for k in range(1, N_DEV):
            g = g + a_ref[k * rows:(k + 1) * rows, :]
        g_ref[...] = g
        d_ref[...], mo_ref[...], vo_ref[...] = _adam_math(w_ref[...], g, m_ref[...], v_ref[...])

    shp = jax.ShapeDtypeStruct((rows, LANE), F32)
    return pl.pallas_call(body, name="adam_small", out_shape=(shp, shp, shp, shp), compiler_params=_cp(None))(
        w, gathered, m, v)


_WEIGHTS = ("rel_bias", "w_in", "b_gate", "ssm_conv_w", "ssm_conv_b", "ssm_dt_bias", "ssm_a_log", "ssm_d",
            "ssm_norm_w", "attn_sinks", "w_branch_ssm", "w_branch_attn", "w_mix_out", "ln1_g", "ln1_b", "w_up",
            "ffn_conv_w", "ffn_conv_b", "w_down", "ln2_g", "ln2_b")
_REPLICATED = tuple(n for n, _ in _SMALL[:13])


class _Exchange:
    def __init__(self, w, chip, core):
        self.chip = chip
        self.c_idx = jnp.reshape(core, (1,)).astype(jnp.int32)
        self.chip_idx = jnp.reshape(chip, (1,)).astype(jnp.int32)
        self.shards = {n: w[n].astype(jnp.bfloat16).reshape(2, s[0] // 2, s[1]) for n, s, _ in _MATS}
        self.spec = {n: (s, ax) for n, s, ax in _MATS}
        self.sums = {}

    def cast_and_gather_w_in(self, x):
        x_act, (stack,) = _cast_rows(x, _ACT, "cast_x", _gather_job([self.shards["w_in"]]))
        return x_act, _pack_w_in_stack(stack.reshape((N_CHIPS,) + self.spec["w_in"][0]))

    def gather_rest(self):
        return _gather_job([self.shards[n] for n in _REST])

    def rest_weights(self, stacks):
        return {n: _from_stack(st.reshape((N_CHIPS,) + self.spec[n][0]), self.spec[n][1])
                for n, st in zip(_REST, stacks)}

    def reduce_job(self, grads):
        names, stacks = [], []
        for n, g in grads.items():
            name = "w_in" if n == "wp" else n
            s, ax = self.spec[name]
            st = _unpack_w_in_stack(g) if n == "wp" else _to_stack(g, s, ax)
            names.append(name)
            stacks.append(st.reshape(N_CHIPS, 2, s[0] // 2, s[1]))
        swapped = _swap_halves(stacks, "swap_" + names[0])
        halves = []
        for n, g, r in zip(names, stacks, swapped):
            self.sums[n], bf = _add_own_half(g, r, self.c_idx, "add_own_" + n)
            halves.append(bf)
        return _scatter_job(halves)

    def reduced(self, landed):
        names = list(self.sums)
        reds = [_add_chips(self.sums[n], r, self.chip_idx, self.c_idx, "add_chips_" + n)
                for n, r in zip(names, landed)]
        return {n: g.reshape(self.spec[n][0]) for n, g in zip(names, _join_halves(reds))}


def _step(x, target, w, m, v):
    xi, yi, ci = _position()
    chip = 2 * xi + yi
    ex = _Exchange(w, chip, ci)

    wts = {n: w[n] for n in _REPLICATED}
    x_act, wts["wp"] = ex.cast_and_gather_w_in(x)
    taps = jnp.concatenate([w[n].astype(F32).reshape(-1) for n, _ in _CONVS])
    taps = _allgather_small(_as_rows(taps, _CONV_ROWS, F32), "allgather_taps")
    taps = taps.reshape(N_CHIPS, 2, _CONV_ROWS * LANE)[:, 0]
    at = 0
    for n, s in _CONVS:
        wts[n] = _from_stack(taps[:, at:at + s[0] * s[1]].reshape((N_CHIPS,) + s), 1)
        at += s[0] * s[1]

    d_x, grads, small, landed = _local_step(x, x_act, target, wts, ex)

    outs = {"grad": ex.reduced(landed), "delta": {}, "m": {}, "v": {}}

    small = dict(small, g_ssm_conv_w=grads["ssm_conv_w"], g_ffn_conv_w=grads["ffn_conv_w"])
    all_small = _allgather_small(_pack_small(small), "allgather_small")
    packs = [_pack_small({n: d[n] for n in _REPLICATED}) for d in (w, m, v)]
    g_s, d_s, m_s, v_s = (_unpack_small(a) for a in _adam_small(packs[0], all_small, packs[1], packs[2]))
    for kind, part in (("grad", g_s), ("delta", d_s), ("m", m_s), ("v", v_s)):
        outs[kind].update({n: part[n] for n in _REPLICATED})
    for n, s in _CONVS:
        outs["grad"][n] = lax.dynamic_slice_in_dim(g_s["g_" + n], chip * s[1], s[1], axis=1)
    for n in [n for n, _, _ in _MATS] + [n for n, _ in _CONVS]:
        outs["delta"][n], outs["m"][n], outs["v"][n] = _adam_big(
            w[n].astype(F32), outs["grad"][n], m[n].astype(F32), v[n].astype(F32), "adam_" + n)
    loss = (0.5 / D_MODEL) * jnp.sum(g_s["loss_lanes"])
    return loss, d_x, outs


def kernel(x, rel_bias, w_in, b_gate, ssm_conv_w, ssm_conv_b, ssm_dt_bias, ssm_a_log, ssm_d, ssm_norm_w, attn_sinks, w_branch_ssm, w_branch_attn, w_mix_out, ln1_g, ln1_b, w_up, ffn_conv_w, ffn_conv_b, w_down, ln2_g, ln2_b, loss_target, m_rel_bias, m_w_in, m_b_gate, m_ssm_conv_w, m_ssm_conv_b, m_ssm_dt_bias, m_ssm_a_log, m_ssm_d, m_ssm_norm_w, m_attn_sinks, m_w_branch_ssm, m_w_branch_attn, m_w_mix_out, m_ln1_g, m_ln1_b, m_w_up, m_ffn_conv_w, m_ffn_conv_b, m_w_down, m_ln2_g, m_ln2_b, v_rel_bias, v_w_in, v_b_gate, v_ssm_conv_w, v_ssm_conv_b, v_ssm_dt_bias, v_ssm_a_log, v_ssm_d, v_ssm_norm_w, v_attn_sinks, v_w_branch_ssm, v_w_branch_attn, v_w_mix_out, v_ln1_g, v_ln1_b, v_w_up, v_ffn_conv_w, v_ffn_conv_b, v_w_down, v_ln2_g, v_ln2_b):
    given = dict(locals())
    drop = lambda a, n: a if n == "rel_bias" or a.ndim == 2 else a[0]
    w = {n: drop(given[n], n) for n in _WEIGHTS}
    m = {n: drop(given["m_" + n], n) for n in _WEIGHTS}
    v = {n: drop(given["v_" + n], n) for n in _WEIGHTS}
    loss, d_x, outs = _step(x[0], loss_target[0], w, m, v)
    like = lambda a, n: a.reshape(given[n].shape)
    res = [loss, d_x[None]]
    for kind in ("grad", "delta", "m", "v"):
        res += [like(outs[kind][n], n) for n in _WEIGHTS]
    return tuple(res)
```
